```python
import math
import jax, jax.numpy as jnp
from jax import lax
import numpy as np

D_MODEL = 1024
BATCH = 8
SEQ = 8192
DEPTH = 2

D_SSD = D_MODEL
SSD_HEADDIM = 64
SSD_HEADS = D_SSD // SSD_HEADDIM
SSD_GROUPS = 4
SSD_HPG = SSD_HEADS // SSD_GROUPS
D_STATE = 128
SSD_CONV = 4
CHUNK = 128
D_FOX = D_MODEL
FOX_HEADDIM = 64
FOX_HEADS = D_FOX // FOX_HEADDIM
Q_BLOCK = 128
D_MIX = D_SSD + D_FOX
EVEN_SIZES = (D_MIX, D_SSD + 2 * SSD_GROUPS * D_STATE, SSD_HEADS, D_FOX, D_FOX, D_FOX, FOX_HEADS)
EVEN_IN = D_MIX + D_SSD + 2 * SSD_GROUPS * D_STATE + SSD_HEADS + 3 * D_FOX + FOX_HEADS
D_CONV = 2 * D_MODEL
CONV_WIDTH = 31
ODD_IN = 3 * D_CONV
N_EVEN = (DEPTH + 1) // 2
N_ODD = DEPTH // 2
EPS = 1e-6

kernel_name = "hybrid_ssd_fox_conformer_trunk"


def _split_points(sizes):
    pts, acc = [], 0
    for s in sizes[:-1]:
        acc += s
        pts.append(acc)
    return pts


def rmsnorm(x, g):
    xf = x.astype(jnp.float32)
    y = xf * lax.rsqrt(jnp.mean(xf * xf, axis=-1, keepdims=True) + EPS)
    return (y * g.astype(jnp.float32)).astype(x.dtype)


def layernorm(x, g, b):
    xf = x.astype(jnp.float32)
    mu = jnp.mean(xf, axis=-1, keepdims=True)
    xc = xf - mu
    y = xc * lax.rsqrt(jnp.mean(xc * xc, axis=-1, keepdims=True) + EPS)
    return (y * g.astype(jnp.float32) + b.astype(jnp.float32)).astype(x.dtype)


def causal_depthwise_conv(x, w, b):
    k, c = w.shape
    y = lax.conv_general_dilated(x, w[:, None, :], window_strides=(1,), padding=[(k - 1, 0)],
                                 dimension_numbers=('NWC', 'WIO', 'NWC'), feature_group_count=c)
    return y + b


def ssd_chunked(x, dt, a, bm, cm):
    bsz, s, g, r, p = x.shape
    n = bm.shape[-1]
    nc = s // CHUNK
    xd = (x * dt[..., None]).reshape(bsz, nc, CHUNK, g, r, p)
    da = (dt * a).reshape(bsz, nc, CHUNK, g, r)
    bc = bm.reshape(bsz, nc, CHUNK, g, n)
    cc = cm.reshape(bsz, nc, CHUNK, g, n)
    cs = jnp.cumsum(da, axis=2)
    li = jnp.arange(CHUNK)
    causal = (li[:, None] >= li[None, :])[None, None, :, :, None, None]
    seg = cs[:, :, :, None] - cs[:, :, None, :]
    decay = jnp.exp(jnp.where(causal, seg, -jnp.inf))
    cb = jnp.einsum('bclgn,bcsgn->bclsg', cc, bc)
    y_diag = jnp.einsum('bclsgr,bcsgrp->bclgrp', cb[..., None] * decay, xd)
    decay_to_end = jnp.exp(cs[:, :, -1:] - cs)
    chunk_states = jnp.einsum('bclgn,bclgrp->bcgrpn', bc, xd * decay_to_end[..., None])
    chunk_decay = jnp.exp(cs[:, :, -1])

    def step(h, inp):
        st, dec = inp
        h_new = h * dec[..., None, None] + st
        return h_new, h

    h0 = jnp.zeros((bsz, g, r, p, n), chunk_states.dtype)
    _, h_in = lax.scan(step, h0, (jnp.moveaxis(chunk_states, 1, 0), jnp.moveaxis(chunk_decay, 1, 0)))
    h_in = jnp.moveaxis(h_in, 0, 1)
    y_off = jnp.einsum('bclgn,bcgrpn->bclgrp', cc, h_in) * jnp.exp(cs)[..., None]
    return (y_diag + y_off).reshape(bsz, s, g, r, p)


def forgetting_attention(q, k, v, log_f):
    bsz, s, h, p = q.shape
    c = jnp.cumsum(log_f, axis=1)
    ck = jnp.transpose(c, (0, 2, 1))
    nb = s // Q_BLOCK
    qb = jnp.moveaxis(q.reshape(bsz, nb, Q_BLOCK, h, p), 1, 0)
    cqb = jnp.moveaxis(c.reshape(bsz, nb, Q_BLOCK, h), 1, 0)
    kpos = jnp.arange(s)
    scale = p ** -0.5

    def block(args):
        qi, cqi, i = args
        qpos = i * Q_BLOCK + jnp.arange(Q_BLOCK)
        logits = jnp.einsum('bqhp,bkhp->bhqk', qi, k).astype(jnp.float32) * scale
        logits = logits + jnp.transpose(cqi, (0, 2, 1))[..., None] - ck[:, :, None, :]
        logits = jnp.where(kpos[None, :] <= qpos[:, None], logits, -jnp.inf)
        w = jax.nn.softmax(logits, axis=-1).astype(v.dtype)
        return jnp.einsum('bhqk,bkhp->bqhp', w, v)

    out = lax.map(block, (qb, cqb, jnp.arange(nb)))
    return jnp.moveaxis(out, 0, 1).reshape(bsz, s, h, p)


def ssd_fox_layer(x, g_pre, w_in, conv_w, conv_b, dt_bias, a_log, d_skip, fgate_b, ssd_norm, w_out, g_post):
    bsz, s, _ = x.shape
    u = rmsnorm(x, g_pre)
    proj = jnp.einsum('bsd,de->bse', u, w_in)
    z, xbc, dt_raw, q, k, v, f_raw = jnp.split(proj, _split_points(EVEN_SIZES), axis=-1)
    z_ssd, z_fox = jnp.split(z, [D_SSD], axis=-1)
    xbc = jax.nn.silu(causal_depthwise_conv(xbc, conv_w, conv_b))
    xs, bm, cm = jnp.split(xbc, [D_SSD, D_SSD + SSD_GROUPS * D_STATE], axis=-1)
    dt = jax.nn.softplus(dt_raw + dt_bias).reshape(bsz, s, SSD_GROUPS, SSD_HPG)
    a = (-jnp.exp(a_log)).reshape(SSD_GROUPS, SSD_HPG)
    xs = xs.reshape(bsz, s, SSD_GROUPS, SSD_HPG, SSD_HEADDIM)
    y = ssd_chunked(xs, dt, a, bm.reshape(bsz, s, SSD_GROUPS, D_STATE), cm.reshape(bsz, s, SSD_GROUPS, D_STATE))
    y = (y + xs * d_skip.reshape(SSD_GROUPS, SSD_HPG)[:, :, None]).reshape(bsz, s, D_SSD)
    yg = (y * jax.nn.silu(z_ssd)).reshape(bsz, s, SSD_GROUPS, D_SSD // SSD_GROUPS).astype(jnp.float32)
    yg = yg * lax.rsqrt(jnp.mean(yg * yg, axis=-1, keepdims=True) + EPS)
    y = (yg.reshape(bsz, s, D_SSD) * ssd_norm.astype(jnp.float32)).astype(x.dtype)
    log_f = jax.nn.log_sigmoid((f_raw + fgate_b).astype(jnp.float32))
    o = forgetting_attention(q.reshape(bsz, s, FOX_HEADS, FOX_HEADDIM),
                             k.reshape(bsz, s, FOX_HEADS, FOX_HEADDIM),
                             v.reshape(bsz, s, FOX_HEADS, FOX_HEADDIM), log_f)
    o = o.reshape(bsz, s, D_FOX) * jax.nn.silu(z_fox)
    out = jnp.einsum('bse,ed->bsd', jnp.concatenate([y, o], axis=-1), w_out)
    return x + rmsnorm(out, g_post)


def conformer_conv_layer(x, g_pre, w_in, conv_w, conv_b, ln_g, ln_b, w_out, g_post):
    u = rmsnorm(x, g_pre)
    proj = jnp.einsum('bsd,de->bse', u, w_in)
    val, glu_gate, z = jnp.split(proj, [D_CONV, 2 * D_CONV], axis=-1)
    h = val * jax.nn.sigmoid(glu_gate)
    h = causal_depthwise_conv(h, conv_w, conv_b)
    h = jax.nn.silu(layernorm(h, ln_g, ln_b))
    h = h * jax.nn.silu(z)
    out = jnp.einsum('bse,ed->bsd', h, w_out)
    return x + rmsnorm(out, g_post)


def _fwd_setup_inputs(seed: int = 0) -> dict:
    key = jax.random.key(seed)
    ks = jax.random.split(key, 20)
    f32 = jnp.float32

    def nrm(k, shape, scale):
        return jax.random.normal(k, shape, f32) * scale

    def gain(k, shape):
        return 1.0 + 0.02 * jax.random.normal(k, shape, f32)

    ne, no = N_EVEN, N_ODD
    x = jax.random.normal(ks[0], (BATCH, SEQ, D_MODEL), f32)
    dt0 = jnp.exp(jax.random.uniform(ks[5], (ne, SSD_HEADS), f32, minval=math.log(1e-3), maxval=math.log(1e-1)))
    e_dt_bias = dt0 + jnp.log(-jnp.expm1(-dt0))
    e_a_log = jnp.log(jax.random.uniform(ks[6], (ne, SSD_HEADS), f32, minval=1.0, maxval=16.0))
    return {
        "x": x,
        "e_norm_pre": gain(ks[1], (ne, D_MODEL)),
        "e_w_in": nrm(ks[2], (ne, D_MODEL, EVEN_IN), D_MODEL ** -0.5),
        "e_conv_w": nrm(ks[3], (ne, SSD_CONV, D_SSD + 2 * SSD_GROUPS * D_STATE), SSD_CONV ** -0.5),
        "e_conv_b": nrm(ks[4], (ne, D_SSD + 2 * SSD_GROUPS * D_STATE), 0.02),
        "e_dt_bias": e_dt_bias,
        "e_a_log": e_a_log,
        "e_d_skip": 1.0 + 0.1 * jax.random.normal(ks[7], (ne, SSD_HEADS), f32),
        "e_fgate_b": jax.random.uniform(ks[8], (ne, FOX_HEADS), f32, minval=1.0, maxval=6.0),
        "e_ssd_norm": gain(ks[9], (ne, D_SSD)),
        "e_w_out": nrm(ks[10], (ne, D_MIX, D_MODEL), D_MIX ** -0.5),
        "e_norm_post": gain(ks[11], (ne, D_MODEL)),
        "o_norm_pre": gain(ks[12], (no, D_MODEL)),
        "o_w_in": nrm(ks[13], (no, D_MODEL, ODD_IN), D_MODEL ** -0.5),
        "o_conv_w": nrm(ks[14], (no, CONV_WIDTH, D_CONV), CONV_WIDTH ** -0.5),
        "o_conv_b": nrm(ks[15], (no, D_CONV), 0.02),
        "o_ln_g": gain(ks[16], (no, D_CONV)),
        "o_ln_b": nrm(ks[17], (no, D_CONV), 0.02),
        "o_w_out": nrm(ks[18], (no, D_CONV, D_MODEL), D_CONV ** -0.5),
        "o_norm_post": gain(ks[19], (no, D_MODEL)),
    }


def _fwd_reference(x, e_norm_pre, e_w_in, e_conv_w, e_conv_b, e_dt_bias, e_a_log, e_d_skip, e_fgate_b,
              e_ssd_norm, e_w_out, e_norm_post, o_norm_pre, o_w_in, o_conv_w, o_conv_b, o_ln_g, o_ln_b,
              o_w_out, o_norm_post):
    for layer in range(DEPTH):
        i = layer // 2
        if layer % 2 == 0:
            x = ssd_fox_layer(x, e_norm_pre[i], e_w_in[i], e_conv_w[i], e_conv_b[i], e_dt_bias[i],
                              e_a_log[i], e_d_skip[i], e_fgate_b[i], e_ssd_norm[i], e_w_out[i],
                              e_norm_post[i])
        else:
            x = conformer_conv_layer(x, o_norm_pre[i], o_w_in[i], o_conv_w[i], o_conv_b[i], o_ln_g[i],
                                     o_ln_b[i], o_w_out[i], o_norm_post[i])
    return x


import jax as _jax
import jax.numpy as _jnp

TWIN_FORMAT = 'train_step'
FWD_PARAMS = ['x', 'e_norm_pre', 'e_w_in', 'e_conv_w', 'e_conv_b', 'e_dt_bias', 'e_a_log', 'e_d_skip', 'e_fgate_b', 'e_ssd_norm', 'e_w_out', 'e_norm_post', 'o_norm_pre', 'o_w_in', 'o_conv_w', 'o_conv_b', 'o_ln_g', 'o_ln_b', 'o_w_out', 'o_norm_post']
TWIN_WEIGHTS = ['e_norm_pre', 'e_w_in', 'e_conv_w', 'e_conv_b', 'e_dt_bias', 'e_a_log', 'e_d_skip', 'e_fgate_b', 'e_ssd_norm', 'e_w_out', 'e_norm_post', 'o_norm_pre', 'o_w_in', 'o_conv_w', 'o_conv_b', 'o_ln_g', 'o_ln_b', 'o_w_out', 'o_norm_post']
TWIN_DIFF_INPUT = 'x'
TWIN_INPUTS = ['x', 'e_norm_pre', 'e_w_in', 'e_conv_w', 'e_conv_b', 'e_dt_bias', 'e_a_log', 'e_d_skip', 'e_fgate_b', 'e_ssd_norm', 'e_w_out', 'e_norm_post', 'o_norm_pre', 'o_w_in', 'o_conv_w', 'o_conv_b', 'o_ln_g', 'o_ln_b', 'o_w_out', 'o_norm_post', 'loss_target', 'm_e_norm_pre', 'm_e_w_in', 'm_e_conv_w', 'm_e_conv_b', 'm_e_dt_bias', 'm_e_a_log', 'm_e_d_skip', 'm_e_fgate_b', 'm_e_ssd_norm', 'm_e_w_out', 'm_e_norm_post', 'm_o_norm_pre', 'm_o_w_in', 'm_o_conv_w', 'm_o_conv_b', 'm_o_ln_g', 'm_o_ln_b', 'm_o_w_out', 'm_o_norm_post', 'v_e_norm_pre', 'v_e_w_in', 'v_e_conv_w', 'v_e_conv_b', 'v_e_dt_bias', 'v_e_a_log', 'v_e_d_skip', 'v_e_fgate_b', 'v_e_ssd_norm', 'v_e_w_out', 'v_e_norm_post', 'v_o_norm_pre', 'v_o_w_in', 'v_o_conv_w', 'v_o_conv_b', 'v_o_ln_g', 'v_o_ln_b', 'v_o_w_out', 'v_o_norm_post']
TWIN_OUTPUTS = ['loss', 'grad_x', 'grad_e_norm_pre', 'grad_e_w_in', 'grad_e_conv_w', 'grad_e_conv_b', 'grad_e_dt_bias', 'grad_e_a_log', 'grad_e_d_skip', 'grad_e_fgate_b', 'grad_e_ssd_norm', 'grad_e_w_out', 'grad_e_norm_post', 'grad_o_norm_pre', 'grad_o_w_in', 'grad_o_conv_w', 'grad_o_conv_b', 'grad_o_ln_g', 'grad_o_ln_b', 'grad_o_w_out', 'grad_o_norm_post', 'delta_e_norm_pre', 'delta_e_w_in', 'delta_e_conv_w', 'delta_e_conv_b', 'delta_e_dt_bias', 'delta_e_a_log', 'delta_e_d_skip', 'delta_e_fgate_b', 'delta_e_ssd_norm', 'delta_e_w_out', 'delta_e_norm_post', 'delta_o_norm_pre', 'delta_o_w_in', 'delta_o_conv_w', 'delta_o_conv_b', 'delta_o_ln_g', 'delta_o_ln_b', 'delta_o_w_out', 'delta_o_norm_post', 'new_m_e_norm_pre', 'new_m_e_w_in', 'new_m_e_conv_w', 'new_m_e_conv_b', 'new_m_e_dt_bias', 'new_m_e_a_log', 'new_m_e_d_skip', 'new_m_e_fgate_b', 'new_m_e_ssd_norm', 'new_m_e_w_out', 'new_m_e_norm_post', 'new_m_o_norm_pre', 'new_m_o_w_in', 'new_m_o_conv_w', 'new_m_o_conv_b', 'new_m_o_ln_g', 'new_m_o_ln_b', 'new_m_o_w_out', 'new_m_o_norm_post', 'new_v_e_norm_pre', 'new_v_e_w_in', 'new_v_e_conv_w', 'new_v_e_conv_b', 'new_v_e_dt_bias', 'new_v_e_a_log', 'new_v_e_d_skip', 'new_v_e_fgate_b', 'new_v_e_ssd_norm', 'new_v_e_w_out', 'new_v_e_norm_post', 'new_v_o_norm_pre', 'new_v_o_w_in', 'new_v_o_conv_w', 'new_v_o_conv_b', 'new_v_o_ln_g', 'new_v_o_ln_b', 'new_v_o_w_out', 'new_v_o_norm_post']
TWIN_LEAF_KINDS = {'loss': 'loss', 'grad_x': 'grad_x', 'grad_e_norm_pre': 'grad_w', 'grad_e_w_in': 'grad_w', 'grad_e_conv_w': 'grad_w', 'grad_e_conv_b': 'grad_w', 'grad_e_dt_bias': 'grad_w', 'grad_e_a_log': 'grad_w', 'grad_e_d_skip': 'grad_w', 'grad_e_fgate_b': 'grad_w', 'grad_e_ssd_norm': 'grad_w', 'grad_e_w_out': 'grad_w', 'grad_e_norm_post': 'grad_w', 'grad_o_norm_pre': 'grad_w', 'grad_o_w_in': 'grad_w', 'grad_o_conv_w': 'grad_w', 'grad_o_conv_b': 'grad_w', 'grad_o_ln_g': 'grad_w', 'grad_o_ln_b': 'grad_w', 'grad_o_w_out': 'grad_w', 'grad_o_norm_post': 'grad_w', 'delta_e_norm_pre': 'delta_w', 'delta_e_w_in': 'delta_w', 'delta_e_conv_w': 'delta_w', 'delta_e_conv_b': 'delta_w', 'delta_e_dt_bias': 'delta_w', 'delta_e_a_log': 'delta_w', 'delta_e_d_skip': 'delta_w', 'delta_e_fgate_b': 'delta_w', 'delta_e_ssd_norm': 'delta_w', 'delta_e_w_out': 'delta_w', 'delta_e_norm_post': 'delta_w', 'delta_o_norm_pre': 'delta_w', 'delta_o_w_in': 'delta_w', 'delta_o_conv_w': 'delta_w', 'delta_o_conv_b': 'delta_w', 'delta_o_ln_g': 'delta_w', 'delta_o_ln_b': 'delta_w', 'delta_o_w_out': 'delta_w', 'delta_o_norm_post': 'delta_w', 'new_m_e_norm_pre': 'new_m', 'new_m_e_w_in': 'new_m', 'new_m_e_conv_w': 'new_m', 'new_m_e_conv_b': 'new_m', 'new_m_e_dt_bias': 'new_m', 'new_m_e_a_log': 'new_m', 'new_m_e_d_skip': 'new_m', 'new_m_e_fgate_b': 'new_m', 'new_m_e_ssd_norm': 'new_m', 'new_m_e_w_out': 'new_m', 'new_m_e_norm_post': 'new_m', 'new_m_o_norm_pre': 'new_m', 'new_m_o_w_in': 'new_m', 'new_m_o_conv_w': 'new_m', 'new_m_o_conv_b': 'new_m', 'new_m_o_ln_g': 'new_m', 'new_m_o_ln_b': 'new_m', 'new_m_o_w_out': 'new_m', 'new_m_o_norm_post': 'new_m', 'new_v_e_norm_pre': 'new_v', 'new_v_e_w_in': 'new_v', 'new_v_e_conv_w': 'new_v', 'new_v_e_conv_b': 'new_v', 'new_v_e_dt_bias': 'new_v', 'new_v_e_a_log': 'new_v', 'new_v_e_d_skip': 'new_v', 'new_v_e_fgate_b': 'new_v', 'new_v_e_ssd_norm': 'new_v', 'new_v_e_w_out': 'new_v', 'new_v_e_norm_post': 'new_v', 'new_v_o_norm_pre': 'new_v', 'new_v_o_w_in': 'new_v', 'new_v_o_conv_w': 'new_v', 'new_v_o_conv_b': 'new_v', 'new_v_o_ln_g': 'new_v', 'new_v_o_ln_b': 'new_v', 'new_v_o_w_out': 'new_v', 'new_v_o_norm_post': 'new_v'}


def _forward(args):
    return _fwd_reference(*[args[k] for k in FWD_PARAMS])


def _output_shape():
    def fwd():
        inp = _fwd_setup_inputs(0)
        return _fwd_reference(*[inp[k] for k in FWD_PARAMS])
    out = _jax.eval_shape(fwd)
    return out.shape, out.dtype

N_MICROBATCH = 1
ADAM_LR = 0.001
ADAM_B1 = 0.9
ADAM_B2 = 0.999
ADAM_EPS = 1e-08
ADAM_WD = 0.01
ADAM_STEP = 10
PER_EXAMPLE_BATCH_AXIS = {'x': 0, 'loss_target': 0}
SHARED_INPUTS = []
_WEIGHT_DTYPES = {'e_norm_pre': _jnp.float32, 'e_w_in': _jnp.float32, 'e_conv_w': _jnp.float32, 'e_conv_b': _jnp.float32, 'e_dt_bias': _jnp.float32, 'e_a_log': _jnp.float32, 'e_d_skip': _jnp.float32, 'e_fgate_b': _jnp.float32, 'e_ssd_norm': _jnp.float32, 'e_w_out': _jnp.float32, 'e_norm_post': _jnp.float32, 'o_norm_pre': _jnp.float32, 'o_w_in': _jnp.float32, 'o_conv_w': _jnp.float32, 'o_conv_b': _jnp.float32, 'o_ln_g': _jnp.float32, 'o_ln_b': _jnp.float32, 'o_w_out': _jnp.float32, 'o_norm_post': _jnp.float32}
MOMENT_SCALE = {'e_norm_pre': 1.138338e+00, 'e_w_in': 4.158289e-01, 'e_conv_w': 7.125930e-01, 'e_conv_b': 2.245198e+00, 'e_dt_bias': 1.611077e+00, 'e_a_log': 4.421438e+00, 'e_d_skip': 1.004767e+01, 'e_fgate_b': 7.440676e-01, 'e_ssd_norm': 1.402445e+00, 'e_w_out': 1.353053e+00, 'e_norm_post': 6.380354e+01, 'o_norm_pre': 1.063055e+00, 'o_w_in': 3.883509e-01, 'o_conv_w': 4.155467e-01, 'o_conv_b': 4.099148e+00, 'o_ln_g': 1.564137e+00, 'o_ln_b': 2.426004e+00, 'o_w_out': 1.262142e+00, 'o_norm_post': 6.384428e+01}


def _to_microbatches(a, axis):
    t = _jnp.moveaxis(a, axis, 0)
    t = t.reshape((N_MICROBATCH, t.shape[0] // N_MICROBATCH) + t.shape[1:])
    return _jnp.moveaxis(t, 1, axis + 1)


def setup_inputs(seed: int = 0) -> dict:
    inp = _fwd_setup_inputs(seed)
    key = _jax.random.fold_in(_jax.random.key(seed), 7919)
    shape, _ = _output_shape()
    out = dict(inp)
    out["loss_target"] = _jax.random.normal(_jax.random.fold_in(key, 0), shape, _jnp.float32)
    for i, name in enumerate(TWIN_WEIGHTS):
        w = inp[name].astype(_jnp.float32)
        if MOMENT_SCALE is None:
            s = _jnp.sqrt(_jnp.mean(_jnp.square(w)) + 1e-30)
        else:
            s = MOMENT_SCALE[name]
        km, kv = _jax.random.split(_jax.random.fold_in(key, i + 1))
        out[name] = w
        out["m_" + name] = s * _jax.random.normal(km, w.shape, _jnp.float32)
        out["v_" + name] = (s * s) * _jax.random.uniform(kv, w.shape, _jnp.float32, 0.5, 1.5)
    if N_MICROBATCH > 1:
        for name, axis in PER_EXAMPLE_BATCH_AXIS.items():
            out[name] = _to_microbatches(out[name], axis)
    return {'x': out['x'], 'e_norm_pre': out['e_norm_pre'], 'e_w_in': out['e_w_in'], 'e_conv_w': out['e_conv_w'], 'e_conv_b': out['e_conv_b'], 'e_dt_bias': out['e_dt_bias'], 'e_a_log': out['e_a_log'], 'e_d_skip': out['e_d_skip'], 'e_fgate_b': out['e_fgate_b'], 'e_ssd_norm': out['e_ssd_norm'], 'e_w_out': out['e_w_out'], 'e_norm_post': out['e_norm_post'], 'o_norm_pre': out['o_norm_pre'], 'o_w_in': out['o_w_in'], 'o_conv_w': out['o_conv_w'], 'o_conv_b': out['o_conv_b'], 'o_ln_g': out['o_ln_g'], 'o_ln_b': out['o_ln_b'], 'o_w_out': out['o_w_out'], 'o_norm_post': out['o_norm_post'], 'loss_target': out['loss_target'], 'm_e_norm_pre': out['m_e_norm_pre'], 'm_e_w_in': out['m_e_w_in'], 'm_e_conv_w': out['m_e_conv_w'], 'm_e_conv_b': out['m_e_conv_b'], 'm_e_dt_bias': out['m_e_dt_bias'], 'm_e_a_log': out['m_e_a_log'], 'm_e_d_skip': out['m_e_d_skip'], 'm_e_fgate_b': out['m_e_fgate_b'], 'm_e_ssd_norm': out['m_e_ssd_norm'], 'm_e_w_out': out['m_e_w_out'], 'm_e_norm_post': out['m_e_norm_post'], 'm_o_norm_pre': out['m_o_norm_pre'], 'm_o_w_in': out['m_o_w_in'], 'm_o_conv_w': out['m_o_conv_w'], 'm_o_conv_b': out['m_o_conv_b'], 'm_o_ln_g': out['m_o_ln_g'], 'm_o_ln_b': out['m_o_ln_b'], 'm_o_w_out': out['m_o_w_out'], 'm_o_norm_post': out['m_o_norm_post'], 'v_e_norm_pre': out['v_e_norm_pre'], 'v_e_w_in': out['v_e_w_in'], 'v_e_conv_w': out['v_e_conv_w'], 'v_e_conv_b': out['v_e_conv_b'], 'v_e_dt_bias': out['v_e_dt_bias'], 'v_e_a_log': out['v_e_a_log'], 'v_e_d_skip': out['v_e_d_skip'], 'v_e_fgate_b': out['v_e_fgate_b'], 'v_e_ssd_norm': out['v_e_ssd_norm'], 'v_e_w_out': out['v_e_w_out'], 'v_e_norm_post': out['v_e_norm_post'], 'v_o_norm_pre': out['v_o_norm_pre'], 'v_o_w_in': out['v_o_w_in'], 'v_o_conv_w': out['v_o_conv_w'], 'v_o_conv_b': out['v_o_conv_b'], 'v_o_ln_g': out['v_o_ln_g'], 'v_o_ln_b': out['v_o_ln_b'], 'v_o_w_out': out['v_o_w_out'], 'v_o_norm_post': out['v_o_norm_post']}


def _loss(weights, diff, rest, loss_target):
    with _jax.named_scope("forward"):
        args = {**rest, TWIN_DIFF_INPUT: diff, **{k: w.astype(_WEIGHT_DTYPES[k]) for k, w in weights.items()}}
        y = _forward(args)
    with _jax.named_scope("loss_head"):
        err = _jnp.square(y.astype(_jnp.float32) - loss_target)
        return 0.5 * _jnp.sum(_jnp.mean(err, axis=-1)) if err.ndim else 0.5 * err


def _adamw(w, g, m, v):
    m = ADAM_B1 * m + (1.0 - ADAM_B1) * g
    v = ADAM_B2 * v + (1.0 - ADAM_B2) * _jnp.square(g)
    m_hat = m / (1.0 - ADAM_B1 ** ADAM_STEP)
    v_hat = v / (1.0 - ADAM_B2 ** ADAM_STEP)
    delta = -ADAM_LR * (m_hat / (_jnp.sqrt(v_hat) + ADAM_EPS) + ADAM_WD * w)
    return delta, m, v


def reference(x, e_norm_pre, e_w_in, e_conv_w, e_conv_b, e_dt_bias, e_a_log, e_d_skip, e_fgate_b, e_ssd_norm, e_w_out, e_norm_post, o_norm_pre, o_w_in, o_conv_w, o_conv_b, o_ln_g, o_ln_b, o_w_out, o_norm_post, loss_target, m_e_norm_pre, m_e_w_in, m_e_conv_w, m_e_conv_b, m_e_dt_bias, m_e_a_log, m_e_d_skip, m_e_fgate_b, m_e_ssd_norm, m_e_w_out, m_e_norm_post, m_o_norm_pre, m_o_w_in, m_o_conv_w, m_o_conv_b, m_o_ln_g, m_o_ln_b, m_o_w_out, m_o_norm_post, v_e_norm_pre, v_e_w_in, v_e_conv_w, v_e_conv_b, v_e_dt_bias, v_e_a_log, v_e_d_skip, v_e_fgate_b, v_e_ssd_norm, v_e_w_out, v_e_norm_post, v_o_norm_pre, v_o_w_in, v_o_conv_w, v_o_conv_b, v_o_ln_g, v_o_ln_b, v_o_w_out, v_o_norm_post):
    given = dict(x=x, e_norm_pre=e_norm_pre, e_w_in=e_w_in, e_conv_w=e_conv_w, e_conv_b=e_conv_b, e_dt_bias=e_dt_bias, e_a_log=e_a_log, e_d_skip=e_d_skip, e_fgate_b=e_fgate_b, e_ssd_norm=e_ssd_norm, e_w_out=e_w_out, e_norm_post=e_norm_post, o_norm_pre=o_norm_pre, o_w_in=o_w_in, o_conv_w=o_conv_w, o_conv_b=o_conv_b, o_ln_g=o_ln_g, o_ln_b=o_ln_b, o_w_out=o_w_out, o_norm_post=o_norm_post, loss_target=loss_target, m_e_norm_pre=m_e_norm_pre, m_e_w_in=m_e_w_in, m_e_conv_w=m_e_conv_w, m_e_conv_b=m_e_conv_b, m_e_dt_bias=m_e_dt_bias, m_e_a_log=m_e_a_log, m_e_d_skip=m_e_d_skip, m_e_fgate_b=m_e_fgate_b, m_e_ssd_norm=m_e_ssd_norm, m_e_w_out=m_e_w_out, m_e_norm_post=m_e_norm_post, m_o_norm_pre=m_o_norm_pre, m_o_w_in=m_o_w_in, m_o_conv_w=m_o_conv_w, m_o_conv_b=m_o_conv_b, m_o_ln_g=m_o_ln_g, m_o_ln_b=m_o_ln_b, m_o_w_out=m_o_w_out, m_o_norm_post=m_o_norm_post, v_e_norm_pre=v_e_norm_pre, v_e_w_in=v_e_w_in, v_e_conv_w=v_e_conv_w, v_e_conv_b=v_e_conv_b, v_e_dt_bias=v_e_dt_bias, v_e_a_log=v_e_a_log, v_e_d_skip=v_e_d_skip, v_e_fgate_b=v_e_fgate_b, v_e_ssd_norm=v_e_ssd_norm, v_e_w_out=v_e_w_out, v_e_norm_post=v_e_norm_post, v_o_norm_pre=v_o_norm_pre, v_o_w_in=v_o_w_in, v_o_conv_w=v_o_conv_w, v_o_conv_b=v_o_conv_b, v_o_ln_g=v_o_ln_g, v_o_ln_b=v_o_ln_b, v_o_w_out=v_o_w_out, v_o_norm_post=v_o_norm_post)
    weights = {n: given[n] for n in TWIN_WEIGHTS}
    shared = {n: given[n] for n in SHARED_INPUTS}
    per_example = {n: given[n] for n in ['x']}
    grad_fn = _jax.value_and_grad(_loss, argnums=(0, 1))

    def one_microbatch(ex, loss_target):
        ex = dict(ex)
        diff = ex.pop(TWIN_DIFF_INPUT)
        return grad_fn(weights, diff, {**shared, **ex}, loss_target)

    if N_MICROBATCH == 1:
        loss, (grad_w, grad_x) = one_microbatch(per_example, given["loss_target"])
    else:
        def body(carry, xs):
            loss_sum, grad_sum = carry
            l_k, (gw_k, gx_k) = one_microbatch(xs[0], xs[1])
            with _jax.named_scope("update"):
                return (loss_sum + l_k, _jax.tree.map(_jnp.add, grad_sum, gw_k)), gx_k

        init = (_jnp.zeros((), _jnp.float32), _jax.tree.map(_jnp.zeros_like, weights))
        (loss, grad_w), grad_x = _jax.lax.scan(body, init, (per_example, given["loss_target"]))
    with _jax.named_scope("update"):
        delta_w, new_m, new_v = {}, {}, {}
        for n in TWIN_WEIGHTS:
            delta_w[n], new_m[n], new_v[n] = _adamw(weights[n], grad_w[n], given["m_" + n], given["v_" + n])
    return (loss, grad_x, *[grad_w[n] for n in TWIN_WEIGHTS], *[delta_w[n] for n in TWIN_WEIGHTS],
            *[new_m[n] for n in TWIN_WEIGHTS], *[new_v[n] for n in TWIN_WEIGHTS])
```

```python
import functools

import jax
import jax.numpy as jnp
from jax import lax
from jax.experimental import pallas as pl
from jax.experimental.pallas import tpu as pltpu

F32 = jnp.float32
BF16 = jnp.bfloat16
MESH = pl.DeviceIdType.MESH

D_MODEL = 1024
D_SSD = 1024
SSD_HEADS = 16
SSD_HEADDIM = 64
SSD_GROUPS = 4
SSD_HPG = 4
D_STATE = 128
SSD_CONV = 4
CHUNK = 128
D_FOX = 1024
FOX_HEADS = 16
FOX_HEADDIM = 64
D_CONV = 2048
CONV_WIDTH = 31
EPS = 1e-6
LANES = 128
VMEM_LIMIT = 56 * 1024 * 1024

ADAM_LR = 0.001
ADAM_B1 = 0.9
ADAM_B2 = 0.999
ADAM_EPS = 1e-08
ADAM_WD = 0.01
ADAM_STEP = 10


def _cparams(sem=None):
    return pltpu.CompilerParams(dimension_semantics=sem, vmem_limit_bytes=VMEM_LIMIT)


def _mm(a, b, *, ta=False, tb=False, add=None, out_dtype=F32, tm=512, tn=512, tk=2048, name):
    m = a.shape[1] if ta else a.shape[0]
    k = a.shape[0] if ta else a.shape[1]
    n = b.shape[0] if tb else b.shape[1]
    tm, tn = min(tm, m), min(tn, n)
    tk = max(t for t in range(LANES, min(tk, k) + 1, LANES) if k % t == 0)
    assert m % tm == 0 and n % tn == 0 and k % tk == 0, (m, n, k, tm, tn, tk)
    nk = k // tk
    dims = (((0 if ta else 1,), (1 if tb else 0,)), ((), ()))

    def body(*refs):
        if add is None:
            a_ref, b_ref, o_ref, acc_ref = refs
            c_ref = None
        else:
            a_ref, b_ref, c_ref, o_ref, acc_ref = refs
        kk = pl.program_id(2)

        @pl.when(kk == 0)
        def _():
            if c_ref is None:
                acc_ref[...] = jnp.zeros_like(acc_ref)
            else:
                acc_ref[...] = c_ref[...].astype(F32)

        acc_ref[...] += lax.dot_general(a_ref[...].astype(BF16), b_ref[...].astype(BF16), dims,
                                        preferred_element_type=F32)

        @pl.when(kk == nk - 1)
        def _():
            o_ref[...] = acc_ref[...].astype(o_ref.dtype)

    a_spec = (pl.BlockSpec((tk, tm), lambda i, j, kk: (kk, i)) if ta
              else pl.BlockSpec((tm, tk), lambda i, j, kk: (i, kk)))
    b_spec = (pl.BlockSpec((tn, tk), lambda i, j, kk: (j, kk)) if tb
              else pl.BlockSpec((tk, tn), lambda i, j, kk: (kk, j)))
    o_spec = pl.BlockSpec((tm, tn), lambda i, j, kk: (i, j))
    in_specs, args = [a_spec, b_spec], [a, b]
    if add is not None:
        in_specs.append(o_spec)
        args.append(add)
    return pl.pallas_call(
        body, name=name, grid=(m // tm, n // tn, nk),
        in_specs=in_specs, out_specs=o_spec,
        out_shape=jax.ShapeDtypeStruct((m, n), out_dtype),
        scratch_shapes=[pltpu.VMEM((tm, tn), F32)],
        compiler_params=_cparams(("parallel", "parallel", "arbitrary")),
    )(*args)


def _col(arr, cb, width):
    return (arr, cb, width)


def _row_specs(ops, tm):
    return [pl.BlockSpec((tm, w), lambda i, cb=cb: (i, cb)) for (_, cb, w) in ops]


def _par_specs(ops):
    return [pl.BlockSpec((a.shape[0], w), lambda i, cb=cb: (0, cb)) for (a, cb, w) in ops]


def _rowwise_fwd(fn, rows, params, outs, *, tm, name):
    s = rows[0][0].shape[0]
    tm = min(tm, s)
    nr, npar = len(rows), len(params)

    def body(*refs):
        rv = [r[...].astype(F32) for r in refs[:nr]]
        pv = [p[...].astype(F32) for p in refs[nr:nr + npar]]
        res = fn(*rv, *pv)
        for o_ref, val in zip(refs[nr + npar:], res):
            o_ref[...] = val.astype(o_ref.dtype)

    return pl.pallas_call(
        body, name=name, grid=(s // tm,),
        in_specs=_row_specs(rows, tm) + _par_specs(params),
        out_specs=[pl.BlockSpec((tm, w), lambda i: (i, 0)) for (w, _) in outs],
        out_shape=[jax.ShapeDtypeStruct((s, w), dt) for (w, dt) in outs],
        compiler_params=_cparams(("parallel",)),
    )(*[r[0] for r in rows], *[p[0] for p in params])


def _rowwise_bwd(fn, rows, params, couts, row_grads, *, adds=None, tm, name):
    adds = adds or {}
    s = rows[0][0].shape[0]
    tm = min(tm, s)
    nr, npar, nc = len(rows), len(params), len(couts)
    add_keys = sorted(adds)
    want = [i for i, dt in enumerate(row_grads) if dt is not None]

    def body(*refs):
        i = pl.program_id(0)
        rv = [r[...].astype(F32) for r in refs[:nr]]
        pv = [p[...].astype(F32) for p in refs[nr:nr + npar]]
        cv = [c[...].astype(F32) for c in refs[nr + npar:nr + npar + nc]]
        av = {k: r[...].astype(F32) for k, r in zip(add_keys, refs[nr + npar + nc:nr + npar + nc + len(add_keys)])}
        orefs = refs[nr + npar + nc + len(add_keys):]
        _, vjp = jax.vjp(lambda rr, pp: tuple(fn(*rr, *pp)), rv, pv)
        drows, dpars = vjp(tuple(cv))
        for o_ref, ri in zip(orefs[:len(want)], want):
            g = drows[ri]
            if ri in av:
                g = g + av[ri]
            o_ref[...] = g.astype(o_ref.dtype)

        @pl.when(i == 0)
        def _():
            for o_ref in orefs[len(want):]:
                o_ref[...] = jnp.zeros_like(o_ref)

        for o_ref, g in zip(orefs[len(want):], dpars):
            o_ref[...] += g

    add_ops = [adds[k] for k in add_keys]
    out_specs = ([pl.BlockSpec((tm, rows[ri][2]), lambda i: (i, 0)) for ri in want]
                 + [pl.BlockSpec((p[0].shape[0], p[2]), lambda i: (0, 0)) for p in params])
    out_shape = ([jax.ShapeDtypeStruct((s, rows[ri][2]), row_grads[ri]) for ri in want]
                 + [jax.ShapeDtypeStruct((p[0].shape[0], p[2]), F32) for p in params])
    res = pl.pallas_call(
        body, name=name, grid=(s // tm,),
        in_specs=_row_specs(rows, tm) + _par_specs(params) + _row_specs(couts, tm) + _row_specs(add_ops, tm),
        out_specs=out_specs, out_shape=out_shape,
        compiler_params=_cparams(("arbitrary",)),
    )(*[r[0] for r in rows], *[p[0] for p in params], *[c[0] for c in couts], *[a[0] for a in add_ops])
    return res[:len(want)], res[len(want):]


def _silu(v):
    return v * jax.nn.sigmoid(v)


def _rms(v, g):
    return v * lax.rsqrt(jnp.mean(v * v, axis=-1, keepdims=True) + EPS) * g


def _conv_pad(k):
    return -(-(k - 1) // 8) * 8


def _shift_conv(xp, w, b, offsets, *, s, tr=512, name):
    rows_p, c = xp.shape
    pad = rows_p - s
    tr = min(tr, s)
    nk = len(offsets)

    def body(*refs):
        if b is None:
            xp_ref, w_ref, o_ref = refs
        else:
            xp_ref, w_ref, b_ref, o_ref = refs
        r = pl.program_id(1)
        base = pl.multiple_of(r * tr, tr)
        win = xp_ref[pl.ds(base, tr + pad), :]
        acc = jnp.zeros((tr, LANES), F32) if b is None else jnp.broadcast_to(b_ref[...], (tr, LANES))
        for kk in range(nk):
            off = offsets[kk]
            sh = win if off == 0 else pltpu.roll(win, (tr + pad - off) % (tr + pad), 0)
            acc = acc + sh[:tr] * w_ref[kk:kk + 1, :]
        o_ref[...] = acc

    in_specs = [pl.BlockSpec((rows_p, LANES), lambda cb, r: (0, cb)),
                pl.BlockSpec((nk, LANES), lambda cb, r: (0, cb))]
    args = [xp, w]
    if b is not None:
        in_specs.append(pl.BlockSpec((1, LANES), lambda cb, r: (0, cb)))
        args.append(b)
    return pl.pallas_call(
        body, name=name, grid=(c // LANES, s // tr),
        in_specs=in_specs, out_specs=pl.BlockSpec((tr, LANES), lambda cb, r: (r, cb)),
        out_shape=jax.ShapeDtypeStruct((s, c), F32),
        compiler_params=_cparams(("parallel", "arbitrary")),
    )(*args)


def _conv_fwd(x, w, b, *, name):
    k = w.shape[0]
    pad = _conv_pad(k)
    xp = jnp.pad(x, ((pad, 0), (0, 0)))
    return _shift_conv(xp, w, b, [pad - (k - 1) + kk for kk in range(k)], s=x.shape[0], name=name)


def _conv_bwd_x(dy, w, *, name):
    k = w.shape[0]
    pad = _conv_pad(k)
    dyp = jnp.pad(dy, ((0, pad), (0, 0)))
    return _shift_conv(dyp, w, None, [(k - 1) - kk for kk in range(k)], s=dy.shape[0], name=name)


def _conv_bwd_w(x, dy, k, *, tr=512, name):
    s, c = x.shape
    pad = _conv_pad(k)
    xp = jnp.pad(x, ((pad, 0), (0, 0)))
    tr = min(tr, s)
    offsets = [pad - (k - 1) + kk for kk in range(k)]

    def body(xp_ref, dy_ref, dw_ref, db_ref):
        r = pl.program_id(1)

        @pl.when(r == 0)
        def _():
            dw_ref[...] = jnp.zeros_like(dw_ref)
            db_ref[...] = jnp.zeros_like(db_ref)

        base = pl.multiple_of(r * tr, tr)
        win = xp_ref[pl.ds(base, tr + pad), :]
        dyv = dy_ref[...]
        db_ref[...] += jnp.sum(dyv, axis=0, keepdims=True)
        for kk in range(k):
            off = offsets[kk]
            sh = win if off == 0 else pltpu.roll(win, (tr + pad - off) % (tr + pad), 0)
            dw_ref[kk:kk + 1, :] += jnp.sum(sh[:tr] * dyv, axis=0, keepdims=True)

    return pl.pallas_call(
        body, name=name, grid=(c // LANES, s // tr),
        in_specs=[pl.BlockSpec((s + pad, LANES), lambda cb, r: (0, cb)),
                  pl.BlockSpec((tr, LANES), lambda cb, r: (r, cb))],
        out_specs=[pl.BlockSpec((k, LANES), lambda cb, r: (0, cb)),
                   pl.BlockSpec((1, LANES), lambda cb, r: (0, cb))],
        out_shape=[jax.ShapeDtypeStruct((k, c), F32), jax.ShapeDtypeStruct((1, c), F32)],
        compiler_params=_cparams(("parallel", "arbitrary")),
    )(xp, dy)


_DIMS = {"nn": ((1,), (0,)), "nt": ((1,), (1,)), "tn": ((0,), (0,))}


def _bd(a, b, mode):
    return lax.dot_general(a.astype(BF16), b.astype(BF16), (_DIMS[mode], ((), ())), preferred_element_type=F32)


@functools.partial(jax.custom_vjp, nondiff_argnums=(2,))
def _bdot(a, b, mode):
    return _bd(a, b, mode)


def _bdot_fwd(a, b, mode):
    return _bd(a, b, mode), (a, b)


def _bdot_bwd(mode, res, g):
    a, b = res
    if mode == "nn":
        return _bd(g, b, "nt"), _bd(a, g, "tn")
    if mode == "nt":
        return _bd(g, b, "nn"), _bd(g, a, "tn")
    return _bd(b, g, "nt"), _bd(a, g, "nn")


_bdot.defvjp(_bdot_fwd, _bdot_bwd)


def _split3(v):
    hi = v.astype(BF16)
    r1 = v - hi.astype(F32)
    mid = r1.astype(BF16)
    lo = (r1 - mid.astype(F32)).astype(BF16)
    return hi, mid, lo


def _mask_dot(mask01, v, mode):
    out = None
    for part in _split3(v):
        if mode == "vn":
            t = lax.dot_general(part, mask01, (_DIMS["nn"], ((), ())), preferred_element_type=F32)
        else:
            t = lax.dot_general(mask01, part, (_DIMS[mode], ((), ())), preferred_element_type=F32)
        out = t if out is None else out + t
    return out


def _lower_tri(n):
    r = lax.broadcasted_iota(jnp.int32, (n, n), 0)
    c = lax.broadcasted_iota(jnp.int32, (n, n), 1)
    return (r >= c).astype(BF16)


@jax.custom_vjp
def _tri_dot(w):
    return _mask_dot(_lower_tri(w.shape[0]), w, "nn")


def _tri_dot_fwd(w):
    return _tri_dot(w), None


def _tri_dot_bwd(_, g):
    return (_mask_dot(_lower_tri(g.shape[0]), g, "tn"),)


_tri_dot.defvjp(_tri_dot_fwd, _tri_dot_bwd)


def _cumsum_lanes(x, *, reverse, name):
    h, s = x.shape
    n = s // LANES

    def body(x_ref, o_ref):
        r = lax.broadcasted_iota(jnp.int32, (LANES, LANES), 0)
        c = lax.broadcasted_iota(jnp.int32, (LANES, LANES), 1)
        m01 = ((r >= c) if reverse else (r <= c)).astype(BF16)

        def step(t, carry):
            ci = (n - 1 - t) if reverse else t
            at = pl.ds(pl.multiple_of(ci * LANES, LANES), LANES)
            blk = x_ref[:, at]
            o_ref[:, at] = _mask_dot(m01, blk, "vn") + carry
            return carry + jnp.sum(blk, axis=1, keepdims=True)

        lax.fori_loop(0, n, step, jnp.zeros((h, 1), F32))

    return pl.pallas_call(body, name=name, out_shape=jax.ShapeDtypeStruct((h, s), F32),
                          compiler_params=_cparams())(x)


def _ssd_chunk(xs, dt, bm, cm, hin, a, dsk, head0):
    n = CHUNK
    row = lax.broadcasted_iota(jnp.int32, (n, n), 0)
    col = lax.broadcasted_iota(jnp.int32, (n, n), 1)
    lower = row >= col
    ustrict = (row > col).astype(F32)
    lane = lax.broadcasted_iota(jnp.int32, (1, LANES), 1)
    sub = lax.broadcasted_iota(jnp.int32, (n, 1), 0)
    e_first = (sub == 0).astype(F32)
    e_last = (sub == n - 1).astype(F32)
    lane0 = (lane == 0).astype(F32)
    cb = _bdot(cm, bm, "nt")
    da = dt * (-jnp.exp(a))
    ys, houts = [], []
    for r in range(SSD_HPG):
        oh = (lane == head0 + r).astype(F32)
        dt_col = jnp.sum(dt * oh, axis=1, keepdims=True)
        da_col = jnp.sum(da * oh, axis=1, keepdims=True)
        dsk_h = jnp.sum(dsk * oh, axis=1, keepdims=True)
        seg = _tri_dot(da_col * ustrict)
        decay = jnp.where(lower, jnp.exp(seg), 0.0)
        cs_col = jnp.sum(seg * lane0, axis=1, keepdims=True) + jnp.sum(da_col * e_first, axis=0, keepdims=True)
        total = jnp.sum(cs_col * e_last, axis=0, keepdims=True)
        xd = xs[r] * dt_col
        y_diag = _bdot(cb * decay, xd, "nn")
        contrib = _bdot(xd * jnp.exp(total - cs_col), bm, "tn")
        houts.append(hin[r] * jnp.exp(total) + contrib)
        y_off = _bdot(cm, hin[r], "nt") * jnp.exp(cs_col)
        ys.append(y_diag + y_off + xs[r] * dsk_h)
    return ys, houts


def _ssd_specs(nc, rev):
    cc = (lambda c: nc - 1 - c) if rev else (lambda c: c)
    hm = pl.BlockSpec((SSD_HPG, CHUNK, SSD_HEADDIM), lambda c, g: (g, cc(c), 0))
    row = pl.BlockSpec((CHUNK, LANES), lambda c, g: (cc(c), 0))
    bmat = pl.BlockSpec((CHUNK, LANES), lambda c, g: (cc(c), D_SSD // LANES + g))
    cmat = pl.BlockSpec((CHUNK, LANES), lambda c, g: (cc(c), D_SSD // LANES + SSD_GROUPS + g))
    par = pl.BlockSpec((1, LANES), lambda c, g: (0, 0))
    hs = pl.BlockSpec((1, SSD_HPG, SSD_HEADDIM, D_STATE), lambda c, g: (cc(c), g, 0, 0))
    return hm, row, bmat, cmat, par, hs


def _ssd_fwd(xs_hm, dt, xbc, a, dsk, *, name):
    s = xs_hm.shape[1]
    nc = s // CHUNK
    hm, row, bmat, cmat, par, hs = _ssd_specs(nc, False)

    def body(xs_ref, dt_ref, bm_ref, cm_ref, a_ref, dsk_ref, y_ref, hs_ref, h_sc):
        c, g = pl.program_id(0), pl.program_id(1)
        mine = pl.ds(g * SSD_HPG, SSD_HPG)

        @pl.when(c == 0)
        def _():
            h_sc[mine] = jnp.zeros((SSD_HPG, SSD_HEADDIM, D_STATE), F32)

        hin = [h_sc[g * SSD_HPG + r] for r in range(SSD_HPG)]
        ys, houts = _ssd_chunk([xs_ref[r] for r in range(SSD_HPG)], dt_ref[...], bm_ref[...], cm_ref[...],
                               hin, a_ref[...], dsk_ref[...], g * SSD_HPG)
        for r in range(SSD_HPG):
            y_ref[r] = ys[r]
            hs_ref[0, r] = hin[r]
            h_sc[g * SSD_HPG + r] = houts[r]

    return pl.pallas_call(
        body, name=name, grid=(nc, SSD_GROUPS),
        in_specs=[hm, row, bmat, cmat, par, par], out_specs=[hm, hs],
        out_shape=[jax.ShapeDtypeStruct(xs_hm.shape, F32),
                   jax.ShapeDtypeStruct((nc, SSD_HEADS, SSD_HEADDIM, D_STATE), F32)],
        scratch_shapes=[pltpu.VMEM((SSD_HEADS, SSD_HEADDIM, D_STATE), F32)],
        compiler_params=_cparams(("arbitrary", "arbitrary")),
    )(xs_hm, dt, xbc, xbc, a, dsk)


def _ssd_bwd(xs_hm, dt, xbc, a, dsk, hsave, dy_hm, *, name):
    s = xs_hm.shape[1]
    nc = s // CHUNK
    hm, row, bmat, cmat, par, hs = _ssd_specs(nc, True)
    gmat = pl.BlockSpec((CHUNK, LANES), lambda c, g: (nc - 1 - c, g))

    def body(xs_ref, dt_ref, bm_ref, cm_ref, a_ref, dsk_ref, hs_ref, dy_ref,
             dxs_ref, ddt_ref, dbm_ref, dcm_ref, da_ref, ddsk_ref, dh_sc):
        c, g = pl.program_id(0), pl.program_id(1)
        mine = pl.ds(g * SSD_HPG, SSD_HPG)

        @pl.when(c == 0)
        def _():
            dh_sc[mine] = jnp.zeros((SSD_HPG, SSD_HEADDIM, D_STATE), F32)

        @pl.when((c == 0) & (g == 0))
        def _():
            da_ref[...] = jnp.zeros_like(da_ref)
            ddsk_ref[...] = jnp.zeros_like(ddsk_ref)

        @pl.when(g == 0)
        def _():
            ddt_ref[...] = jnp.zeros_like(ddt_ref)

        head0 = g * SSD_HPG
        prim = ([xs_ref[r] for r in range(SSD_HPG)], dt_ref[...], bm_ref[...], cm_ref[...],
                [hs_ref[0, r] for r in range(SSD_HPG)], a_ref[...], dsk_ref[...])
        _, vjp = jax.vjp(lambda *p: _ssd_chunk(*p, head0), *prim)
        cot = ([dy_ref[r] for r in range(SSD_HPG)], [dh_sc[g * SSD_HPG + r] for r in range(SSD_HPG)])
        dxs, ddt, dbm, dcm, dhin, da, ddsk = vjp(cot)
        for r in range(SSD_HPG):
            dxs_ref[r] = dxs[r]
            dh_sc[g * SSD_HPG + r] = dhin[r]
        ddt_ref[...] += ddt
        dbm_ref[...] = dbm
        dcm_ref[...] = dcm
        da_ref[...] += da
        ddsk_ref[...] += ddsk

    return pl.pallas_call(
        body, name=name, grid=(nc, SSD_GROUPS),
        in_specs=[hm, row, bmat, cmat, par, par, hs, hm],
        out_specs=[hm, row, gmat, gmat, par, par],
        out_shape=[jax.ShapeDtypeStruct(xs_hm.shape, F32), jax.ShapeDtypeStruct((s, LANES), F32),
                   jax.ShapeDtypeStruct((s, SSD_GROUPS * D_STATE), F32),
                   jax.ShapeDtypeStruct((s, SSD_GROUPS * D_STATE), F32),
                   jax.ShapeDtypeStruct((1, LANES), F32), jax.ShapeDtypeStruct((1, LANES), F32)],
        scratch_shapes=[pltpu.VMEM((SSD_HEADS, SSD_HEADDIM, D_STATE), F32)],
        compiler_params=_cparams(("arbitrary", "arbitrary")),
    )(xs_hm, dt, xbc, xbc, a, dsk, hsave, dy_hm)


FOX_BLOCK = 512
NEG = -1e30


def _fox_scores(q, k, cref, ck, strictly_below):
    t = q.shape[0]
    s = lax.dot_general(q, k, (_DIMS["nt"], ((), ())), preferred_element_type=F32) * (FOX_HEADDIM ** -0.5)
    s = s + (cref - ck)
    row = lax.broadcasted_iota(jnp.int32, (t, t), 0)
    col = lax.broadcasted_iota(jnp.int32, (t, t), 1)
    mask = (row >= col) | strictly_below
    return s, mask


def _fox_fwd(q, k, v, c, *, name):
    h, s, p = q.shape
    t = min(FOX_BLOCK, s)
    nb = s // t

    def body(q_ref, k_ref, v_ref, cq_ref, ck_ref, o_ref, lse_ref, m_sc, l_sc, acc_sc):
        i, j = pl.program_id(1), pl.program_id(2)

        @pl.when(j == 0)
        def _():
            m_sc[...] = jnp.full_like(m_sc, NEG)
            l_sc[...] = jnp.zeros_like(l_sc)
            acc_sc[...] = jnp.zeros_like(acc_sc)

        @pl.when(j <= i)
        def _():
            sc, mask = _fox_scores(q_ref[0], k_ref[0], cq_ref[0, 0:1, 0:1], ck_ref[0], j < i)
            sc = jnp.where(mask, sc, NEG)
            m_old = m_sc[...]
            m_new = jnp.maximum(m_old, jnp.max(sc, axis=1, keepdims=True))
            alpha = jnp.exp(m_old - m_new)
            pr = jnp.exp(sc - m_new)
            l_sc[...] = alpha * l_sc[...] + jnp.sum(pr, axis=1, keepdims=True)
            pr_hi = pr.astype(BF16)
            pr_lo = (pr - pr_hi.astype(F32)).astype(BF16)
            pv = (lax.dot_general(pr_hi, v_ref[0], (_DIMS["nn"], ((), ())), preferred_element_type=F32)
                  + lax.dot_general(pr_lo, v_ref[0], (_DIMS["nn"], ((), ())), preferred_element_type=F32))
            acc_sc[...] = alpha * acc_sc[...] + pv
            m_sc[...] = m_new

        @pl.when(j == i)
        def _():
            o_ref[0] = acc_sc[...] / l_sc[...]
            lse_ref[0] = jnp.broadcast_to(m_sc[...] + jnp.log(l_sc[...]), (t, LANES))

    qspec = pl.BlockSpec((1, t, p), lambda hh, i, j: (hh, i, 0))
    kspec = pl.BlockSpec((1, t, p), lambda hh, i, j: (hh, jnp.minimum(j, i), 0))
    return pl.pallas_call(
        body, name=name, grid=(h, nb, nb),
        in_specs=[qspec, kspec, kspec,
                  pl.BlockSpec((1, 1, t), lambda hh, i, j: (hh, 0, i)),
                  pl.BlockSpec((1, 1, t), lambda hh, i, j: (hh, 0, jnp.minimum(j, i)))],
        out_specs=[qspec, pl.BlockSpec((1, t, LANES), lambda hh, i, j: (hh, i, 0))],
        out_shape=[jax.ShapeDtypeStruct((h, s, p), F32), jax.ShapeDtypeStruct((h, s, LANES), F32)],
        scratch_shapes=[pltpu.VMEM((t, 1), F32), pltpu.VMEM((t, 1), F32), pltpu.VMEM((t, p), F32)],
        compiler_params=_cparams(("parallel", "arbitrary", "arbitrary")),
    )(q, k, v, c, c)


def _fox_bwd(q, k, v, c, o, lse, do, *, name):
    h, s, p = q.shape
    t = min(FOX_BLOCK, s)
    nb = s // t
    scale = FOX_HEADDIM ** -0.5

    def body(q_ref, k_ref, v_ref, cq_ref, ck_ref, o_ref, lse_ref, do_ref,
             dq_ref, dk_ref, dv_ref, dc_ref, dk_sc, dv_sc, dc_sc):
        j, i = pl.program_id(1), pl.program_id(2)

        @pl.when(i == 0)
        def _():
            dk_sc[...] = jnp.zeros_like(dk_sc)
            dv_sc[...] = jnp.zeros_like(dv_sc)
            dc_sc[...] = jnp.zeros_like(dc_sc)

        @pl.when(i >= j)
        def _():
            qv, kv, vv = q_ref[0], k_ref[0], v_ref[0]
            sc, mask = _fox_scores(qv, kv, cq_ref[0, 0:1, 0:1], ck_ref[0], i > j)
            pr = jnp.where(mask, jnp.exp(sc - lse_ref[0, :, 0:1]), 0.0)
            dov = do_ref[0]
            dob = dov.astype(BF16)
            prb = pr.astype(BF16)
            dv_sc[...] += lax.dot_general(prb, dob, (_DIMS["tn"], ((), ())), preferred_element_type=F32)
            dp = lax.dot_general(dob, vv, (_DIMS["nt"], ((), ())), preferred_element_type=F32)
            dcol = jnp.sum(dob.astype(F32) * o_ref[0], axis=1, keepdims=True)
            ds = pr * (dp - dcol)
            dc_sc[...] -= jnp.sum(ds, axis=0, keepdims=True)
            dsb = ds.astype(BF16)
            dqc = scale * lax.dot_general(dsb, kv, (_DIMS["nn"], ((), ())), preferred_element_type=F32)
            at = pl.ds(pl.multiple_of(i * t, t), t)

            @pl.when(j == 0)
            def _():
                dq_ref[0, at, :] = dqc

            @pl.when(j > 0)
            def _():
                dq_ref[0, at, :] += dqc

            dk_sc[...] += scale * lax.dot_general(dsb, qv, (_DIMS["tn"], ((), ())), preferred_element_type=F32)

        @pl.when(i == nb - 1)
        def _():
            dk_ref[0] = dk_sc[...]
            dv_ref[0] = dv_sc[...]
            dc_ref[0] = dc_sc[...]

    qspec = pl.BlockSpec((1, t, p), lambda hh, j, i: (hh, jnp.maximum(i, j), 0))
    kspec = pl.BlockSpec((1, t, p), lambda hh, j, i: (hh, j, 0))
    cq = pl.BlockSpec((1, 1, t), lambda hh, j, i: (hh, 0, jnp.maximum(i, j)))
    ck = pl.BlockSpec((1, 1, t), lambda hh, j, i: (hh, 0, j))
    return pl.pallas_call(
        body, name=name, grid=(h, nb, nb),
        in_specs=[qspec, kspec, kspec, cq, ck, qspec,
                  pl.BlockSpec((1, t, LANES), lambda hh, j, i: (hh, jnp.maximum(i, j), 0)), qspec],
        out_specs=[pl.BlockSpec((1, s, p), lambda hh, j, i: (hh, 0, 0)), kspec, kspec, ck],
        out_shape=[jax.ShapeDtypeStruct((h, s, p), F32), jax.ShapeDtypeStruct((h, s, p), F32),
                   jax.ShapeDtypeStruct((h, s, p), F32), jax.ShapeDtypeStruct((h, 1, s), F32)],
        scratch_shapes=[pltpu.VMEM((t, p), F32), pltpu.VMEM((t, p), F32), pltpu.VMEM((1, t), F32)],
        compiler_params=_cparams(("parallel", "arbitrary", "arbitrary")),
    )(q, k, v, c, c, o, lse, do)


def _final(x1, out1, g, tgt, *, tm=256, name):
    s, d = x1.shape
    tm = min(tm, s)

    def body(x_ref, o_ref, g_ref, t_ref, dx_ref, do_ref, dg_ref, loss_ref):
        i = pl.program_id(0)

        @pl.when(i == 0)
        def _():
            dg_ref[...] = jnp.zeros_like(dg_ref)
            loss_ref[...] = jnp.zeros_like(loss_ref)

        tv = t_ref[...]

        def lossf(xv, ov, gv):
            err = jnp.square(xv + _rms(ov, gv) - tv)
            return 0.5 * jnp.sum(jnp.mean(err, axis=-1, keepdims=True), axis=0, keepdims=True)

        val, vjp = jax.vjp(lossf, x_ref[...], o_ref[...], g_ref[...])
        dx, do, dg = vjp(jnp.ones((1, 1), F32))
        dx_ref[...] = dx
        do_ref[...] = do.astype(do_ref.dtype)
        dg_ref[...] += dg
        loss_ref[...] += val

    row = pl.BlockSpec((tm, d), lambda i: (i, 0))
    par = pl.BlockSpec((1, d), lambda i: (0, 0))
    return pl.pallas_call(
        body, name=name, grid=(s // tm,), in_specs=[row, row, par, row],
        out_specs=[row, row, par, pl.BlockSpec((1, 1), lambda i: (0, 0))],
        out_shape=[jax.ShapeDtypeStruct((s, d), F32), jax.ShapeDtypeStruct((s, d), BF16),
                   jax.ShapeDtypeStruct((1, d), F32), jax.ShapeDtypeStruct((1, 1), F32)],
        compiler_params=_cparams(("arbitrary",)),
    )(x1, out1, g, tgt)


def _row_tile(r):
    return LANES if r % LANES == 0 else r


def _sum_slots(parts, *, name):
    p, r, c = parts.shape
    tr = _row_tile(r)

    def body(p_ref, o_ref):
        acc = p_ref[0].astype(F32)
        for k in range(1, p):
            acc = acc + p_ref[k].astype(F32)
        o_ref[...] = acc

    return pl.pallas_call(
        body, name=name, grid=(r // tr,),
        in_specs=[pl.BlockSpec((p, tr, c), lambda i: (0, i, 0))],
        out_specs=pl.BlockSpec((tr, c), lambda i: (i, 0)),
        out_shape=jax.ShapeDtypeStruct((r, c), F32),
        compiler_params=_cparams(("parallel",)),
    )(parts)


def _adamw(w, gparts, m, v, *, name):
    r, c = w.shape
    p = gparts.shape[0]
    tr = _row_tile(r)

    def body(w_ref, g_ref, m_ref, v_ref, go_ref, d_ref, mo_ref, vo_ref):
        g = g_ref[0]
        for k in range(1, p):
            g = g + g_ref[k]
        mn = ADAM_B1 * m_ref[...] + (1.0 - ADAM_B1) * g
        vn = ADAM_B2 * v_ref[...] + (1.0 - ADAM_B2) * jnp.square(g)
        m_hat = mn / (1.0 - ADAM_B1 ** ADAM_STEP)
        v_hat = vn / (1.0 - ADAM_B2 ** ADAM_STEP)
        go_ref[...] = g
        d_ref[...] = -ADAM_LR * (m_hat / (jnp.sqrt(v_hat) + ADAM_EPS) + ADAM_WD * w_ref[...])
        mo_ref[...] = mn
        vo_ref[...] = vn

    spec = pl.BlockSpec((tr, c), lambda i: (i, 0))
    return pl.pallas_call(
        body, name=name, grid=(r // tr,),
        in_specs=[spec, pl.BlockSpec((p, tr, c), lambda i: (0, i, 0)), spec, spec],
        out_specs=[spec] * 4, out_shape=[jax.ShapeDtypeStruct((r, c), F32)] * 4,
        compiler_params=_cparams(("parallel",)),
    )(w, gparts, m, v)


_FLIPS = {
    "xy": [(1, 0, 0), (0, 1, 0), (1, 1, 0)],
    "c": [(0, 0, 1)],
    "xyc": [(fx, fy, fc) for fx in (0, 1) for fy in (0, 1) for fc in (0, 1) if (fx, fy, fc) != (0, 0, 0)],
}


def _slot(mode, px, py, pc):
    return {"xy": 2 * px + py, "c": pc, "xyc": 4 * px + 2 * py + pc}[mode]


def _exchange(arrs, mode, scatter, *, name):
    n = len(arrs)
    flips = _FLIPS[mode]
    nf = len(flips)

    def body(*refs):
        ins, outs = refs[:n], refs[n:2 * n]
        send, recv, loc = refs[2 * n:]
        x, y, c = lax.axis_index("x"), lax.axis_index("y"), lax.axis_index("c")
        me = _slot(mode, x, y, c)
        peers = [(x ^ fx, y ^ fy, c ^ fc) for (fx, fy, fc) in flips]

        def src(a, slot):
            return ins[a].at[slot] if scatter else ins[a]

        def copy(a, j, dst_slot):
            return pltpu.make_async_remote_copy(
                src_ref=src(a, _slot(mode, *peers[j])), dst_ref=outs[a].at[dst_slot],
                send_sem=send.at[a * nf + j], recv_sem=recv.at[a * nf + j],
                device_id=peers[j], device_id_type=MESH)

        local = [pltpu.make_async_copy(src(a, me), outs[a].at[me], loc.at[a]) for a in range(n)]
        for lc in local:
            lc.start()
        sends = [copy(a, j, me) for a in range(n) for j in range(nf)]
        for cp in sends:
            cp.start()
        for a in range(n):
            for j in range(nf):
                copy(a, j, _slot(mode, *peers[j])).wait_recv()
        for cp in sends:
            cp.wait_send()
        for lc in local:
            lc.wait()

    anyspec = pl.BlockSpec(memory_space=pl.ANY)
    return pl.pallas_call(
        body, name=name, in_specs=[anyspec] * n, out_specs=[anyspec] * n,
        out_shape=[jax.ShapeDtypeStruct((nf + 1,) + (a.shape[1:] if scatter else a.shape), a.dtype) for a in arrs],
        scratch_shapes=[pltpu.SemaphoreType.DMA((n * nf,)), pltpu.SemaphoreType.DMA((n * nf,)),
                        pltpu.SemaphoreType.DMA((n,))],
    )(*arrs)


def _softplus(v):
    return jnp.maximum(v, 0.0) + jnp.log1p(jnp.exp(-jnp.abs(v)))


def _pad_lanes(v):
    r, n = v.shape
    return jnp.pad(v, ((0, 0), (0, -n % LANES)))


def _to_heads(v):
    s = v.shape[0]
    return v.reshape(s, -1, 64).transpose(1, 0, 2)


def _from_heads(v):
    h, s, p = v.shape
    return v.transpose(1, 0, 2).reshape(s, h * p)


def _fn_rms(v, g):
    return (_rms(v, g),)


def _fn_post(xv, ov, g):
    return (xv + _rms(ov, g),)


def _fn_act(xbc, dtp, fp, dtb, fb):
    return _silu(xbc), _softplus(dtp + dtb), -_softplus(-(fp + fb))


def _fn_mix(y, zs, o, zf, g):
    yg = y * _silu(zs)
    sq = yg * yg
    lane = lax.broadcasted_iota(jnp.int32, (1, D_SSD), 1)
    width = D_SSD // SSD_GROUPS
    rstd = jnp.zeros_like(yg)
    for gi in range(SSD_GROUPS):
        msk = ((lane >= gi * width) & (lane < (gi + 1) * width)).astype(F32)
        ms = jnp.sum(sq * msk, axis=1, keepdims=True) / width
        rstd = rstd + lax.rsqrt(ms + EPS) * msk
    return (jnp.concatenate([yg * rstd * g, o * _silu(zf)], axis=1),)


def _fn_glu(val, gate):
    return (val * jax.nn.sigmoid(gate),)


def _fn_ln(hc, z, g, b):
    mu = jnp.mean(hc, axis=-1, keepdims=True)
    xc = hc - mu
    yn = xc * lax.rsqrt(jnp.mean(xc * xc, axis=-1, keepdims=True) + EPS) * g + b
    return (_silu(yn) * _silu(z),)


def _local_step(x, tgt, w):
    s = x.shape[0]
    d = D_MODEL
    tm = 256
    bf = lambda v: v.astype(BF16)
    c1 = lambda arr: _col(arr, 0, arr.shape[1])
    g = {}

    ew = w["e_w_in"]
    w_z, w_xbc = bf(ew[:, 0:2048]), bf(ew[:, 2048:4096])
    w_dt = bf(_pad_lanes(ew[:, 4096:4112]))
    w_qkv = bf(ew[:, 4112:7184])
    w_f = bf(_pad_lanes(ew[:, 7184:7200]))
    w_eo = bf(w["e_w_out"])
    dtb, fgb = _pad_lanes(w["e_dt_bias"]), _pad_lanes(w["e_fgate_b"])
    alog, dsk = _pad_lanes(w["e_a_log"]), _pad_lanes(w["e_d_skip"])
    w_oi, w_oo = bf(w["o_w_in"]), bf(w["o_w_out"])

    (u0,) = _rowwise_fwd(_fn_rms, [c1(x)], [c1(w["e_norm_pre"])], [(d, BF16)], tm=tm, name="e_pre")
    z = _mm(u0, w_z, name="e_in_z")
    xbc_raw = _mm(u0, w_xbc, name="e_in_xbc")
    qkv = _mm(u0, w_qkv, out_dtype=BF16, name="e_in_qkv")
    dtp = _mm(u0, w_dt, name="e_in_dt")
    fp = _mm(u0, w_f, name="e_in_f")
    xbc_pre = _conv_fwd(xbc_raw, w["e_conv_w"], w["e_conv_b"], name="e_conv")
    act_rows = [c1(xbc_pre), c1(dtp), c1(fp)]
    act_pars = [c1(dtb), c1(fgb)]
    xbc, dt, lf = _rowwise_fwd(_fn_act, act_rows, act_pars, [(2048, F32), (LANES, F32), (LANES, F32)],
                               tm=tm, name="e_act")
    xs_hm = _to_heads(xbc[:, :D_SSD])
    y_hm, hsave = _ssd_fwd(xs_hm, dt, xbc, alog, dsk, name="e_ssd")
    y = _from_heads(y_hm)
    csum = _cumsum_lanes(lf[:, :FOX_HEADS].T, reverse=False, name="e_cumsum").reshape(FOX_HEADS, 1, s)
    qkv_hm = qkv.reshape(s, 3, FOX_HEADS, FOX_HEADDIM).transpose(1, 2, 0, 3)
    o_hm, lse = _fox_fwd(qkv_hm[0], qkv_hm[1], qkv_hm[2], csum, name="e_fox")
    o = _from_heads(o_hm)
    mix_rows = [c1(y), _col(z, 0, D_SSD), c1(o), _col(z, 1, D_FOX)]
    mix_pars = [c1(w["e_ssd_norm"])]
    (hmix,) = _rowwise_fwd(_fn_mix, mix_rows, mix_pars, [(2048, BF16)], tm=tm, name="e_mix")
    out0 = _mm(hmix, w_eo, name="e_out")
    post_rows = [c1(x), c1(out0)]
    (x1,) = _rowwise_fwd(_fn_post, post_rows, [c1(w["e_norm_post"])], [(d, F32)], tm=tm, name="e_post")

    (u1,) = _rowwise_fwd(_fn_rms, [c1(x1)], [c1(w["o_norm_pre"])], [(d, BF16)], tm=tm, name="o_pre")
    p1 = _mm(u1, w_oi, name="o_in")
    glu_rows = [_col(p1, 0, D_CONV), _col(p1, 1, D_CONV)]
    (hg,) = _rowwise_fwd(_fn_glu, glu_rows, [], [(D_CONV, F32)], tm=tm, name="o_glu")
    hc = _conv_fwd(hg, w["o_conv_w"], w["o_conv_b"], name="o_conv")
    ln_rows = [c1(hc), _col(p1, 2, D_CONV)]
    ln_pars = [c1(w["o_ln_g"]), c1(w["o_ln_b"])]
    (h2,) = _rowwise_fwd(_fn_ln, ln_rows, ln_pars, [(D_CONV, BF16)], tm=tm, name="o_ln")
    out1 = _mm(h2, w_oo, name="o_out")

    dx2, dout1, g["o_norm_post"], loss = _final(x1, out1, w["o_norm_post"], tgt, name="loss_head")
    dh2 = _mm(dout1, w_oo, tb=True, name="o_out_dx")
    g["o_w_out"] = _mm(h2, dout1, ta=True, name="o_out_dw")
    (dhc, dz1), (g["o_ln_g"], g["o_ln_b"]) = _rowwise_bwd(_fn_ln, ln_rows, ln_pars, [c1(dh2)], [F32, BF16],
                                                         tm=tm, name="o_ln_bwd")
    dhg = _conv_bwd_x(dhc, w["o_conv_w"], name="o_conv_dx")
    g["o_conv_w"], g["o_conv_b"] = _conv_bwd_w(hg, dhc, CONV_WIDTH, name="o_conv_dw")
    (dval, dgate), _ = _rowwise_bwd(_fn_glu, glu_rows, [], [c1(dhg)], [BF16, BF16], tm=tm, name="o_glu_bwd")
    du1 = _mm(dval, w_oi[:, 0:2048], tb=True, name="o_in_dx0")
    du1 = _mm(dgate, w_oi[:, 2048:4096], tb=True, add=du1, name="o_in_dx1")
    du1 = _mm(dz1, w_oi[:, 4096:6144], tb=True, add=du1, name="o_in_dx2")
    g["o_w_in"] = jnp.concatenate([_mm(u1, dval, ta=True, name="o_in_dw0"), _mm(u1, dgate, ta=True, name="o_in_dw1"),
                                   _mm(u1, dz1, ta=True, name="o_in_dw2")], axis=1)
    (dx1,), (g["o_norm_pre"],) = _rowwise_bwd(_fn_rms, [c1(x1)], [c1(w["o_norm_pre"])], [c1(du1)], [F32],
                                              adds={0: c1(dx2)}, tm=tm, name="o_pre_bwd")

    (dout0,), (g["e_norm_post"],) = _rowwise_bwd(_fn_post, post_rows, [c1(w["e_norm_post"])], [c1(dx1)],
                                                 [None, BF16], tm=tm, name="e_post_bwd")
    dhmix = _mm(dout0, w_eo, tb=True, name="e_out_dx")
    g["e_w_out"] = _mm(hmix, dout0, ta=True, name="e_out_dw")
    (dy, dzs, do, dzf), (g["e_ssd_norm"],) = _rowwise_bwd(_fn_mix, mix_rows, mix_pars, [c1(dhmix)],
                                                        [F32, BF16, F32, BF16], tm=tm, name="e_mix_bwd")
    dq, dk, dv, dcs = _fox_bwd(qkv_hm[0], qkv_hm[1], qkv_hm[2], csum, o_hm, lse, _to_heads(do), name="e_fox_bwd")
    dqkv = bf(jnp.stack([dq, dk, dv])).transpose(2, 0, 1, 3).reshape(s, 3 * D_FOX)
    dlf = _pad_lanes(_cumsum_lanes(dcs.reshape(FOX_HEADS, s), reverse=True, name="e_cumsum_bwd").T)
    dxs_hm, ddt, dbm, dcm, dalog, ddsk = _ssd_bwd(xs_hm, dt, xbc, alog, dsk, hsave, _to_heads(dy), name="e_ssd_bwd")
    dxbc = jnp.concatenate([_from_heads(dxs_hm), dbm, dcm], axis=1)
    (dxbc_pre, ddtp, dfp), (ddtb, dfgb) = _rowwise_bwd(_fn_act, act_rows, act_pars, [c1(dxbc), c1(ddt), c1(dlf)],
                                                      [F32, BF16, BF16], tm=tm, name="e_act_bwd")
    dxbc_raw = bf(_conv_bwd_x(dxbc_pre, w["e_conv_w"], name="e_conv_dx"))
    g["e_conv_w"], g["e_conv_b"] = _conv_bwd_w(xbc_raw, dxbc_pre, SSD_CONV, name="e_conv_dw")
    du0 = _mm(dzs, w_z[:, :D_SSD], tb=True, name="e_in_dx0")
    du0 = _mm(dzf, w_z[:, D_SSD:], tb=True, add=du0, name="e_in_dx1")
    du0 = _mm(dxbc_raw, w_xbc, tb=True, add=du0, name="e_in_dx2")
    du0 = _mm(dqkv, w_qkv, tb=True, add=du0, name="e_in_dx3")
    du0 = _mm(ddtp, w_dt, tb=True, add=du0, name="e_in_dx4")
    du0 = _mm(dfp, w_f, tb=True, add=du0, name="e_in_dx5")
    g["e_w_in"] = jnp.concatenate([
        _mm(u0, dzs, ta=True, name="e_in_dw0"), _mm(u0, dzf, ta=True, name="e_in_dw1"),
        _mm(u0, dxbc_raw, ta=True, name="e_in_dw2"), _mm(u0, ddtp, ta=True, name="e_in_dw3")[:, :SSD_HEADS],
        _mm(u0, dqkv, ta=True, name="e_in_dw4"), _mm(u0, dfp, ta=True, name="e_in_dw5")[:, :FOX_HEADS]], axis=1)
    (dx,), (g["e_norm_pre"],) = _rowwise_bwd(_fn_rms, [c1(x)], [c1(w["e_norm_pre"])], [c1(du0)], [F32],
                                             adds={0: c1(dx1)}, tm=tm, name="e_pre_bwd")
    g["e_dt_bias"], g["e_fgate_b"] = ddtb[:, :SSD_HEADS], dfgb[:, :FOX_HEADS]
    g["e_a_log"], g["e_d_skip"] = dalog[:, :SSD_HEADS], ddsk[:, :SSD_HEADS]
    return loss, dx, g


_WEIGHTS = ["e_norm_pre", "e_w_in", "e_conv_w", "e_conv_b", "e_dt_bias", "e_a_log", "e_d_skip", "e_fgate_b",
            "e_ssd_norm", "e_w_out", "e_norm_post", "o_norm_pre", "o_w_in", "o_conv_w", "o_conv_b", "o_ln_g",
            "o_ln_b", "o_w_out", "o_norm_post"]
_BIG = ["e_w_in", "e_w_out", "o_w_in", "o_w_out"]
_ROW_SHARDED = ["e_w_out", "o_w_out"]
_SMALL_SHARDED = ["e_conv_w", "o_norm_pre", "o_conv_w", "o_conv_b", "o_ln_g", "o_ln_b", "o_norm_post"]
_REPLICATED = ["e_norm_pre", "e_conv_b", "e_dt_bias", "e_a_log", "e_d_skip", "e_fgate_b", "e_ssd_norm", "e_norm_post"]
_SMALL = [n for n in _WEIGHTS if n not in _BIG]
N_CHIPS = 4


def _join(gathered, rows):
    k, r, c = gathered.shape
    return gathered.reshape(k * r, c) if rows else gathered.transpose(1, 0, 2).reshape(r, k * c)


def _split(full, rows):
    r, c = full.shape
    return full.reshape(N_CHIPS, r // N_CHIPS, c) if rows else full.reshape(r, N_CHIPS, c // N_CHIPS).transpose(1, 0, 2)


def kernel(x, e_norm_pre, e_w_in, e_conv_w, e_conv_b, e_dt_bias, e_a_log, e_d_skip, e_fgate_b, e_ssd_norm, e_w_out, e_norm_post, o_norm_pre, o_w_in, o_conv_w, o_conv_b, o_ln_g, o_ln_b, o_w_out, o_norm_post, loss_target, m_e_norm_pre, m_e_w_in, m_e_conv_w, m_e_conv_b, m_e_dt_bias, m_e_a_log, m_e_d_skip, m_e_fgate_b, m_e_ssd_norm, m_e_w_out, m_e_norm_post, m_o_norm_pre, m_o_w_in, m_o_conv_w, m_o_conv_b, m_o_ln_g, m_o_ln_b, m_o_w_out, m_o_norm_post, v_e_norm_pre, v_e_w_in, v_e_conv_w, v_e_conv_b, v_e_dt_bias, v_e_a_log, v_e_d_skip, v_e_fgate_b, v_e_ssd_norm, v_e_w_out, v_e_norm_post, v_o_norm_pre, v_o_w_in, v_o_conv_w, v_o_conv_b, v_o_ln_g, v_o_ln_b, v_o_w_out, v_o_norm_post):
    wvals = (e_norm_pre, e_w_in, e_conv_w, e_conv_b, e_dt_bias, e_a_log, e_d_skip, e_fgate_b, e_ssd_norm, e_w_out,
             e_norm_post, o_norm_pre, o_w_in, o_conv_w, o_conv_b, o_ln_g, o_ln_b, o_w_out, o_norm_post)
    mvals = (m_e_norm_pre, m_e_w_in, m_e_conv_w, m_e_conv_b, m_e_dt_bias, m_e_a_log, m_e_d_skip, m_e_fgate_b,
             m_e_ssd_norm, m_e_w_out, m_e_norm_post, m_o_norm_pre, m_o_w_in, m_o_conv_w, m_o_conv_b, m_o_ln_g,
             m_o_ln_b, m_o_w_out, m_o_norm_post)
    vvals = (v_e_norm_pre, v_e_w_in, v_e_conv_w, v_e_conv_b, v_e_dt_bias, v_e_a_log, v_e_d_skip, v_e_fgate_b,
             v_e_ssd_norm, v_e_w_out, v_e_norm_post, v_o_norm_pre, v_o_w_in, v_o_conv_w, v_o_conv_b, v_o_ln_g,
             v_o_ln_b, v_o_w_out, v_o_norm_post)

    def mat(v):
        return v.reshape(v.shape[-2:]) if v.ndim == 3 else v

    w = {n: mat(v) for n, v in zip(_WEIGHTS, wvals)}
    m = {n: mat(v) for n, v in zip(_WEIGHTS, mvals)}
    v2 = {n: mat(v) for n, v in zip(_WEIGHTS, vvals)}
    me_xy = 2 * lax.axis_index("x") + lax.axis_index("y")

    sharded = _BIG + _SMALL_SHARDED
    gathered = _exchange([w[n].astype(BF16) if n in _BIG else w[n] for n in sharded], "xy", False, name="gather_weights")
    full = {n: w[n] for n in _REPLICATED}
    for n, gth in zip(sharded, gathered):
        full[n] = _join(gth, n in _ROW_SHARDED)

    loss, dx, g = _local_step(x[0], loss_target[0], full)
    loss = lax.psum(loss[0, 0], ("x", "y", "c"))

    scattered = _exchange([_split(g[n], n in _ROW_SHARDED).astype(BF16) for n in _BIG], "xy", True, name="scatter_grads")
    partial = [_sum_slots(p, name="sum_" + n) for n, p in zip(_BIG, scattered)]
    pairs = _exchange(partial, "c", False, name="pair_grads")
    gparts = dict(zip(_BIG, pairs))

    flat = jnp.concatenate([_pad_lanes(g[n].reshape(1, -1)) for n in _SMALL], axis=1).reshape(-1, LANES)
    (all8,) = _exchange([flat], "xyc", False, name="gather_small_grads")
    total = _sum_slots(all8, name="sum_small").reshape(1, -1)
    at = 0
    for n in _SMALL:
        size = g[n].size
        gn = total[:, at:at + size].reshape(g[n].shape)
        at += size + (-size % LANES)
        if n in _SMALL_SHARDED:
            cols = gn.shape[1] // N_CHIPS
            gn = lax.dynamic_slice(gn, (0, me_xy * cols), (gn.shape[0], cols))
        gparts[n] = gn[None]

    grads, deltas, new_m, new_v = [], [], [], []
    for n, orig in zip(_WEIGHTS, wvals):
        gn, dn, mn, vn = _adamw(w[n], gparts[n], m[n], v2[n], name="adamw_" + n)
        for lst, val in zip((grads, deltas, new_m, new_v), (gn, dn, mn, vn)):
            lst.append(val.reshape(orig.shape))
    return (loss, dx[None], *grads, *deltas, *new_m, *new_v)
```

```python
import functools

import jax
import jax.numpy as jnp
from jax import lax
from jax.experimental import pallas as pl
from jax.experimental.pallas import tpu as pltpu

F32 = jnp.float32
BF16 = jnp.bfloat16
MESH = pl.DeviceIdType.MESH

D_MODEL = 1024
D_SSD = 1024
SSD_HEADS = 16
SSD_HEADDIM = 64
SSD_GROUPS = 4
SSD_HPG = 4
D_STATE = 128
SSD_CONV = 4
CHUNK = 128
D_FOX = 1024
FOX_HEADS = 16
FOX_HEADDIM = 64
D_CONV = 2048
CONV_WIDTH = 31
EPS = 1e-6
LANES = 128
VMEM_LIMIT = 56 * 1024 * 1024

ADAM_LR = 0.001
ADAM_B1 = 0.9
ADAM_B2 = 0.999
ADAM_EPS = 1e-08
ADAM_WD = 0.01
ADAM_STEP = 10


def _cparams(sem=None):
    return pltpu.CompilerParams(dimension_semantics=sem, vmem_limit_bytes=VMEM_LIMIT)


def _mm(a, b, *, ta=False, tb=False, add=None, out_dtype=F32, tm=1024, tn=None, tk=2048, name):
    m = a.shape[1] if ta else a.shape[0]
    k = a.shape[0] if ta else a.shape[1]
    n = b.shape[0] if tb else b.shape[1]
    if tn is None:
        tn = 1024 if ta else 512
    tm, tn = min(tm, m), min(tn, n)
    tk = max(t for t in range(LANES, min(tk, k) + 1, LANES) if k % t == 0)
    assert m % tm == 0 and n % tn == 0 and k % tk == 0, (m, n, k, tm, tn, tk)
    nk = k // tk
    dims = (((0 if ta else 1,), (1 if tb else 0,)), ((), ()))

    def body(*refs):
        if add is None:
            a_ref, b_ref, o_ref, acc_ref = refs
            c_ref = None
        else:
            a_ref, b_ref, c_ref, o_ref, acc_ref = refs
        kk = pl.program_id(2)

        @pl.when(kk == 0)
        def _():
            if c_ref is None:
                acc_ref[...] = jnp.zeros_like(acc_ref)
            else:
                acc_ref[...] = c_ref[...].astype(F32)

        acc_ref[...] += lax.dot_general(a_ref[...].astype(BF16), b_ref[...].astype(BF16), dims,
                                        preferred_element_type=F32)

        @pl.when(kk == nk - 1)
        def _():
            o_ref[...] = acc_ref[...].astype(o_ref.dtype)

    a_spec = (pl.BlockSpec((tk, tm), lambda i, j, kk: (kk, i)) if ta
              else pl.BlockSpec((tm, tk), lambda i, j, kk: (i, kk)))
    b_spec = (pl.BlockSpec((tn, tk), lambda i, j, kk: (j, kk)) if tb
              else pl.BlockSpec((tk, tn), lambda i, j, kk: (kk, j)))
    o_spec = pl.BlockSpec((tm, tn), lambda i, j, kk: (i, j))
    in_specs, args = [a_spec, b_spec], [a, b]
    if add is not None:
        in_specs.append(o_spec)
        args.append(add)
    return pl.pallas_call(
        body, name=name, grid=(m // tm, n // tn, nk),
        in_specs=in_specs, out_specs=o_spec,
        out_shape=jax.ShapeDtypeStruct((m, n), out_dtype),
        scratch_shapes=[pltpu.VMEM((tm, tn), F32)],
        compiler_params=_cparams(("parallel", "parallel", "arbitrary")),
    )(*args)


def _col(arr, cb, width):
    return (arr, cb, width)


def _row_specs(ops, tm):
    return [pl.BlockSpec((tm, w), lambda i, cb=cb: (i, cb)) for (_, cb, w) in ops]


def _par_specs(ops):
    return [pl.BlockSpec((a.shape[0], w), lambda i, cb=cb: (0, cb)) for (a, cb, w) in ops]


def _rowwise_fwd(fn, rows, params, outs, *, tm, name):
    s = rows[0][0].shape[0]
    tm = min(tm, s)
    nr, npar = len(rows), len(params)

    def body(*refs):
        rv = [r[...].astype(F32) for r in refs[:nr]]
        pv = [p[...].astype(F32) for p in refs[nr:nr + npar]]
        res = fn(*rv, *pv)
        for o_ref, val in zip(refs[nr + npar:], res):
            o_ref[...] = val.astype(o_ref.dtype)

    return pl.pallas_call(
        body, name=name, grid=(s // tm,),
        in_specs=_row_specs(rows, tm) + _par_specs(params),
        out_specs=[pl.BlockSpec((tm, w), lambda i: (i, 0)) for (w, _) in outs],
        out_shape=[jax.ShapeDtypeStruct((s, w), dt) for (w, dt) in outs],
        compiler_params=_cparams(("parallel",)),
    )(*[r[0] for r in rows], *[p[0] for p in params])


def _rowwise_bwd(fn, rows, params, couts, row_grads, *, adds=None, tm, name):
    adds = adds or {}
    s = rows[0][0].shape[0]
    tm = min(tm, s)
    nr, npar, nc = len(rows), len(params), len(couts)
    add_keys = sorted(adds)
    want = [i for i, dt in enumerate(row_grads) if dt is not None]

    def body(*refs):
        i = pl.program_id(0)
        rv = [r[...].astype(F32) for r in refs[:nr]]
        pv = [p[...].astype(F32) for p in refs[nr:nr + npar]]
        cv = [c[...].astype(F32) for c in refs[nr + npar:nr + npar + nc]]
        av = {k: r[...].astype(F32) for k, r in zip(add_keys, refs[nr + npar + nc:nr + npar + nc + len(add_keys)])}
        orefs = refs[nr + npar + nc + len(add_keys):]
        _, vjp = jax.vjp(lambda rr, pp: tuple(fn(*rr, *pp)), rv, pv)
        drows, dpars = vjp(tuple(cv))
        for o_ref, ri in zip(orefs[:len(want)], want):
            g = drows[ri]
            if ri in av:
                g = g + av[ri]
            o_ref[...] = g.astype(o_ref.dtype)

        @pl.when(i == 0)
        def _():
            for o_ref in orefs[len(want):]:
                o_ref[...] = jnp.zeros_like(o_ref)

        for o_ref, g in zip(orefs[len(want):], dpars):
            o_ref[...] += g

    add_ops = [adds[k] for k in add_keys]
    out_specs = ([pl.BlockSpec((tm, rows[ri][2]), lambda i: (i, 0)) for ri in want]
                 + [pl.BlockSpec((p[0].shape[0], p[2]), lambda i: (0, 0)) for p in params])
    out_shape = ([jax.ShapeDtypeStruct((s, rows[ri][2]), row_grads[ri]) for ri in want]
                 + [jax.ShapeDtypeStruct((p[0].shape[0], p[2]), F32) for p in params])
    res = pl.pallas_call(
        body, name=name, grid=(s // tm,),
        in_specs=_row_specs(rows, tm) + _par_specs(params) + _row_specs(couts, tm) + _row_specs(add_ops, tm),
        out_specs=out_specs, out_shape=out_shape,
        compiler_params=_cparams(("arbitrary",)),
    )(*[r[0] for r in rows], *[p[0] for p in params], *[c[0] for c in couts], *[a[0] for a in add_ops])
    return res[:len(want)], res[len(want):]


def _silu(v):
    return v * jax.nn.sigmoid(v)


def _rms(v, g):
    return v * lax.rsqrt(jnp.mean(v * v, axis=-1, keepdims=True) + EPS) * g


def _conv_pad(k):
    return -(-(k - 1) // 8) * 8


def _shift_conv(xp, w, b, offsets, *, s, tr=512, name):
    rows_p, c = xp.shape
    pad = rows_p - s
    tr = min(tr, s)
    nk = len(offsets)

    def body(*refs):
        if b is None:
            xp_ref, w_ref, o_ref = refs
        else:
            xp_ref, w_ref, b_ref, o_ref = refs
        r = pl.program_id(1)
        base = pl.multiple_of(r * tr, tr)
        win = xp_ref[pl.ds(base, tr + pad), :]
        acc = jnp.zeros((tr, LANES), F32) if b is None else jnp.broadcast_to(b_ref[...], (tr, LANES))
        for kk in range(nk):
            off = offsets[kk]
            sh = win if off == 0 else pltpu.roll(win, (tr + pad - off) % (tr + pad), 0)
            acc = acc + sh[:tr] * w_ref[kk:kk + 1, :]
        o_ref[...] = acc

    in_specs = [pl.BlockSpec((rows_p, LANES), lambda cb, r: (0, cb)),
                pl.BlockSpec((nk, LANES), lambda cb, r: (0, cb))]
    args = [xp, w]
    if b is not None:
        in_specs.append(pl.BlockSpec((1, LANES), lambda cb, r: (0, cb)))
        args.append(b)
    return pl.pallas_call(
        body, name=name, grid=(c // LANES, s // tr),
        in_specs=in_specs, out_specs=pl.BlockSpec((tr, LANES), lambda cb, r: (r, cb)),
        out_shape=jax.ShapeDtypeStruct((s, c), F32),
        compiler_params=_cparams(("parallel", "arbitrary")),
    )(*args)


def _conv_fwd(x, w, b, *, name):
    k = w.shape[0]
    pad = _conv_pad(k)
    xp = jnp.pad(x, ((pad, 0), (0, 0)))
    return _shift_conv(xp, w, b, [pad - (k - 1) + kk for kk in range(k)], s=x.shape[0], name=name)


def _conv_bwd_x(dy, w, *, name):
    k = w.shape[0]
    pad = _conv_pad(k)
    dyp = jnp.pad(dy, ((0, pad), (0, 0)))
    return _shift_conv(dyp, w, None, [(k - 1) - kk for kk in range(k)], s=dy.shape[0], name=name)


def _conv_bwd_w(x, dy, k, *, tr=512, name):
    s, c = x.shape
    pad = _conv_pad(k)
    xp = jnp.pad(x, ((pad, 0), (0, 0)))
    tr = min(tr, s)
    offsets = [pad - (k - 1) + kk for kk in range(k)]

    def body(xp_ref, dy_ref, dw_ref, db_ref):
        r = pl.program_id(1)

        @pl.when(r == 0)
        def _():
            dw_ref[...] = jnp.zeros_like(dw_ref)
            db_ref[...] = jnp.zeros_like(db_ref)

        base = pl.multiple_of(r * tr, tr)
        win = xp_ref[pl.ds(base, tr + pad), :]
        dyv = dy_ref[...]
        db_ref[...] += jnp.sum(dyv, axis=0, keepdims=True)
        for kk in range(k):
            off = offsets[kk]
            sh = win if off == 0 else pltpu.roll(win, (tr + pad - off) % (tr + pad), 0)
            dw_ref[kk:kk + 1, :] += jnp.sum(sh[:tr] * dyv, axis=0, keepdims=True)

    return pl.pallas_call(
        body, name=name, grid=(c // LANES, s // tr),
        in_specs=[pl.BlockSpec((s + pad, LANES), lambda cb, r: (0, cb)),
                  pl.BlockSpec((tr, LANES), lambda cb, r: (r, cb))],
        out_specs=[pl.BlockSpec((k, LANES), lambda cb, r: (0, cb)),
                   pl.BlockSpec((1, LANES), lambda cb, r: (0, cb))],
        out_shape=[jax.ShapeDtypeStruct((k, c), F32), jax.ShapeDtypeStruct((1, c), F32)],
        compiler_params=_cparams(("parallel", "arbitrary")),
    )(xp, dy)


_DIMS = {"nn": ((1,), (0,)), "nt": ((1,), (1,)), "tn": ((0,), (0,))}


def _bd(a, b, mode):
    return lax.dot_general(a.astype(BF16), b.astype(BF16), (_DIMS[mode], ((), ())), preferred_element_type=F32)


@functools.partial(jax.custom_vjp, nondiff_argnums=(2,))
def _bdot(a, b, mode):
    return _bd(a, b, mode)


def _bdot_fwd(a, b, mode):
    return _bd(a, b, mode), (a, b)


def _bdot_bwd(mode, res, g):
    a, b = res
    if mode == "nn":
        return _bd(g, b, "nt"), _bd(a, g, "tn")
    if mode == "nt":
        return _bd(g, b, "nn"), _bd(g, a, "tn")
    return _bd(b, g, "nt"), _bd(a, g, "nn")


_bdot.defvjp(_bdot_fwd, _bdot_bwd)


def _split3(v):
    hi = v.astype(BF16)
    r1 = v - hi.astype(F32)
    mid = r1.astype(BF16)
    lo = (r1 - mid.astype(F32)).astype(BF16)
    return hi, mid, lo


def _mask_dot(mask01, v, mode):
    out = None
    for part in _split3(v):
        if mode == "vn":
            t = lax.dot_general(part, mask01, (_DIMS["nn"], ((), ())), preferred_element_type=F32)
        else:
            t = lax.dot_general(mask01, part, (_DIMS[mode], ((), ())), preferred_element_type=F32)
        out = t if out is None else out + t
    return out


def _lower_tri(n):
    r = lax.broadcasted_iota(jnp.int32, (n, n), 0)
    c = lax.broadcasted_iota(jnp.int32, (n, n), 1)
    return (r >= c).astype(BF16)


@jax.custom_vjp
def _tri_dot(w):
    return _mask_dot(_lower_tri(w.shape[0]), w, "nn")


def _tri_dot_fwd(w):
    return _tri_dot(w), None


def _tri_dot_bwd(_, g):
    return (_mask_dot(_lower_tri(g.shape[0]), g, "tn"),)


_tri_dot.defvjp(_tri_dot_fwd, _tri_dot_bwd)


def _cumsum_lanes(x, *, reverse, name):
    h, s = x.shape
    n = s // LANES

    def body(x_ref, o_ref):
        r = lax.broadcasted_iota(jnp.int32, (LANES, LANES), 0)
        c = lax.broadcasted_iota(jnp.int32, (LANES, LANES), 1)
        m01 = ((r >= c) if reverse else (r <= c)).astype(BF16)

        def step(t, carry):
            ci = (n - 1 - t) if reverse else t
            at = pl.ds(pl.multiple_of(ci * LANES, LANES), LANES)
            blk = x_ref[:, at]
            o_ref[:, at] = _mask_dot(m01, blk, "vn") + carry
            return carry + jnp.sum(blk, axis=1, keepdims=True)

        lax.fori_loop(0, n, step, jnp.zeros((h, 1), F32))

    return pl.pallas_call(body, name=name, out_shape=jax.ShapeDtypeStruct((h, s), F32),
                          compiler_params=_cparams())(x)


def _ssd_chunk(xs, dt, bm, cm, hin, a, dsk, head0):
    n = CHUNK
    row = lax.broadcasted_iota(jnp.int32, (n, n), 0)
    col = lax.broadcasted_iota(jnp.int32, (n, n), 1)
    lower = row >= col
    ustrict = (row > col).astype(F32)
    lane = lax.broadcasted_iota(jnp.int32, (1, LANES), 1)
    sub = lax.broadcasted_iota(jnp.int32, (n, 1), 0)
    e_first = (sub == 0).astype(F32)
    e_last = (sub == n - 1).astype(F32)
    lane0 = (lane == 0).astype(F32)
    cb = _bdot(cm, bm, "nt")
    da = dt * (-jnp.exp(a))
    ys, houts = [], []
    for r in range(SSD_HPG):
        oh = (lane == head0 + r).astype(F32)
        dt_col = jnp.sum(dt * oh, axis=1, keepdims=True)
        da_col = jnp.sum(da * oh, axis=1, keepdims=True)
        dsk_h = jnp.sum(dsk * oh, axis=1, keepdims=True)
        seg = _tri_dot(da_col * ustrict)
        decay = jnp.where(lower, jnp.exp(seg), 0.0)
        cs_col = jnp.sum(seg * lane0, axis=1, keepdims=True) + jnp.sum(da_col * e_first, axis=0, keepdims=True)
        total = jnp.sum(cs_col * e_last, axis=0, keepdims=True)
        xd = xs[r] * dt_col
        y_diag = _bdot(cb * decay, xd, "nn")
        contrib = _bdot(xd * jnp.exp(total - cs_col), bm, "tn")
        houts.append(hin[r] * jnp.exp(total) + contrib)
        y_off = _bdot(cm, hin[r], "nt") * jnp.exp(cs_col)
        ys.append(y_diag + y_off + xs[r] * dsk_h)
    return ys, houts


def _ssd_specs(nc, rev):
    cc = (lambda c: nc - 1 - c) if rev else (lambda c: c)
    hm = pl.BlockSpec((SSD_HPG, CHUNK, SSD_HEADDIM), lambda c, g: (g, cc(c), 0))
    row = pl.BlockSpec((CHUNK, LANES), lambda c, g: (cc(c), 0))
    bmat = pl.BlockSpec((CHUNK, LANES), lambda c, g: (cc(c), D_SSD // LANES + g))
    cmat = pl.BlockSpec((CHUNK, LANES), lambda c, g: (cc(c), D_SSD // LANES + SSD_GROUPS + g))
    par = pl.BlockSpec((1, LANES), lambda c, g: (0, 0))
    hs = pl.BlockSpec((1, SSD_HPG, SSD_HEADDIM, D_STATE), lambda c, g: (cc(c), g, 0, 0))
    return hm, row, bmat, cmat, par, hs


def _ssd_fwd(xs_hm, dt, xbc, a, dsk, *, name):
    s = xs_hm.shape[1]
    nc = s // CHUNK
    hm, row, bmat, cmat, par, hs = _ssd_specs(nc, False)

    def body(xs_ref, dt_ref, bm_ref, cm_ref, a_ref, dsk_ref, y_ref, hs_ref, h_sc):
        c, g = pl.program_id(0), pl.program_id(1)
        mine = pl.ds(g * SSD_HPG, SSD_HPG)

        @pl.when(c == 0)
        def _():
            h_sc[mine] = jnp.zeros((SSD_HPG, SSD_HEADDIM, D_STATE), F32)

        hin = [h_sc[g * SSD_HPG + r] for r in range(SSD_HPG)]
        ys, houts = _ssd_chunk([xs_ref[r] for r in range(SSD_HPG)], dt_ref[...], bm_ref[...], cm_ref[...],
                               hin, a_ref[...], dsk_ref[...], g * SSD_HPG)
        for r in range(SSD_HPG):
            y_ref[r] = ys[r]
            hs_ref[0, r] = hin[r]
            h_sc[g * SSD_HPG + r] = houts[r]

    return pl.pallas_call(
        body, name=name, grid=(nc, SSD_GROUPS),
        in_specs=[hm, row, bmat, cmat, par, par], out_specs=[hm, hs],
        out_shape=[jax.ShapeDtypeStruct(xs_hm.shape, F32),
                   jax.ShapeDtypeStruct((nc, SSD_HEADS, SSD_HEADDIM, D_STATE), F32)],
        scratch_shapes=[pltpu.VMEM((SSD_HEADS, SSD_HEADDIM, D_STATE), F32)],
        compiler_params=_cparams(("arbitrary", "arbitrary")),
    )(xs_hm, dt, xbc, xbc, a, dsk)


def _ssd_bwd(xs_hm, dt, xbc, a, dsk, hsave, dy_hm, *, name):
    s = xs_hm.shape[1]
    nc = s // CHUNK
    hm, row, bmat, cmat, par, hs = _ssd_specs(nc, True)
    gmat = pl.BlockSpec((CHUNK, LANES), lambda c, g: (nc - 1 - c, g))

    def body(xs_ref, dt_ref, bm_ref, cm_ref, a_ref, dsk_ref, hs_ref, dy_ref,
             dxs_ref, ddt_ref, dbm_ref, dcm_ref, da_ref, ddsk_ref, dh_sc):
        c, g = pl.program_id(0), pl.program_id(1)
        mine = pl.ds(g * SSD_HPG, SSD_HPG)

        @pl.when(c == 0)
        def _():
            dh_sc[mine] = jnp.zeros((SSD_HPG, SSD_HEADDIM, D_STATE), F32)

        @pl.when((c == 0) & (g == 0))
        def _():
            da_ref[...] = jnp.zeros_like(da_ref)
            ddsk_ref[...] = jnp.zeros_like(ddsk_ref)

        @pl.when(g == 0)
        def _():
            ddt_ref[...] = jnp.zeros_like(ddt_ref)

        head0 = g * SSD_HPG
        prim = ([xs_ref[r] for r in range(SSD_HPG)], dt_ref[...], bm_ref[...], cm_ref[...],
                [hs_ref[0, r] for r in range(SSD_HPG)], a_ref[...], dsk_ref[...])
        _, vjp = jax.vjp(lambda *p: _ssd_chunk(*p, head0), *prim)
        cot = ([dy_ref[r] for r in range(SSD_HPG)], [dh_sc[g * SSD_HPG + r] for r in range(SSD_HPG)])
        dxs, ddt, dbm, dcm, dhin, da, ddsk = vjp(cot)
        for r in range(SSD_HPG):
            dxs_ref[r] = dxs[r]
            dh_sc[g * SSD_HPG + r] = dhin[r]
        ddt_ref[...] += ddt
        dbm_ref[...] = dbm
        dcm_ref[...] = dcm
        da_ref[...] += da
        ddsk_ref[...] += ddsk

    return pl.pallas_call(
        body, name=name, grid=(nc, SSD_GROUPS),
        in_specs=[hm, row, bmat, cmat, par, par, hs, hm],
        out_specs=[hm, row, gmat, gmat, par, par],
        out_shape=[jax.ShapeDtypeStruct(xs_hm.shape, F32), jax.ShapeDtypeStruct((s, LANES), F32),
                   jax.ShapeDtypeStruct((s, SSD_GROUPS * D_STATE), F32),
                   jax.ShapeDtypeStruct((s, SSD_GROUPS * D_STATE), F32),
                   jax.ShapeDtypeStruct((1, LANES), F32), jax.ShapeDtypeStruct((1, LANES), F32)],
        scratch_shapes=[pltpu.VMEM((SSD_HEADS, SSD_HEADDIM, D_STATE), F32)],
        compiler_params=_cparams(("arbitrary", "arbitrary")),
    )(xs_hm, dt, xbc, xbc, a, dsk, hsave, dy_hm)


FOX_BLOCK = 512
NEG = -1e30


def _fox_scores(q, k, cref, ck, strictly_below):
    t = q.shape[0]
    s = lax.dot_general(q, k, (_DIMS["nt"], ((), ())), preferred_element_type=F32) * (FOX_HEADDIM ** -0.5)
    s = s + (cref - ck)
    row = lax.broadcasted_iota(jnp.int32, (t, t), 0)
    col = lax.broadcasted_iota(jnp.int32, (t, t), 1)
    mask = (row >= col) | strictly_below
    return s, mask


def _fox_fwd(q, k, v, c, *, name):
    h, s, p = q.shape
    t = min(FOX_BLOCK, s)
    nb = s // t

    def body(q_ref, k_ref, v_ref, cq_ref, ck_ref, o_ref, lse_ref, m_sc, l_sc, acc_sc):
        i, j = pl.program_id(1), pl.program_id(2)

        @pl.when(j == 0)
        def _():
            m_sc[...] = jnp.full_like(m_sc, NEG)
            l_sc[...] = jnp.zeros_like(l_sc)
            acc_sc[...] = jnp.zeros_like(acc_sc)

        @pl.when(j <= i)
        def _():
            sc, mask = _fox_scores(q_ref[0], k_ref[0], cq_ref[0, 0:1, 0:1], ck_ref[0], j < i)
            sc = jnp.where(mask, sc, NEG)
            m_old = m_sc[...]
            m_new = jnp.maximum(m_old, jnp.max(sc, axis=1, keepdims=True))
            alpha = jnp.exp(m_old - m_new)
            pr = jnp.exp(sc - m_new)
            l_sc[...] = alpha * l_sc[...] + jnp.sum(pr, axis=1, keepdims=True)
            pr_hi = pr.astype(BF16)
            pr_lo = (pr - pr_hi.astype(F32)).astype(BF16)
            pv = (lax.dot_general(pr_hi, v_ref[0], (_DIMS["nn"], ((), ())), preferred_element_type=F32)
                  + lax.dot_general(pr_lo, v_ref[0], (_DIMS["nn"], ((), ())), preferred_element_type=F32))
            acc_sc[...] = alpha * acc_sc[...] + pv
            m_sc[...] = m_new

        @pl.when(j == i)
        def _():
            o_ref[0] = acc_sc[...] / l_sc[...]
            lse_ref[0] = jnp.broadcast_to(m_sc[...] + jnp.log(l_sc[...]), (t, LANES))

    qspec = pl.BlockSpec((1, t, p), lambda hh, i, j: (hh, i, 0))
    kspec = pl.BlockSpec((1, t, p), lambda hh, i, j: (hh, jnp.minimum(j, i), 0))
    return pl.pallas_call(
        body, name=name, grid=(h, nb, nb),
        in_specs=[qspec, kspec, kspec,
                  pl.BlockSpec((1, 1, t), lambda hh, i, j: (hh, 0, i)),
                  pl.BlockSpec((1, 1, t), lambda hh, i, j: (hh, 0, jnp.minimum(j, i)))],
        out_specs=[qspec, pl.BlockSpec((1, t, LANES), lambda hh, i, j: (hh, i, 0))],
        out_shape=[jax.ShapeDtypeStruct((h, s, p), F32), jax.ShapeDtypeStruct((h, s, LANES), F32)],
        scratch_shapes=[pltpu.VMEM((t, 1), F32), pltpu.VMEM((t, 1), F32), pltpu.VMEM((t, p), F32)],
        compiler_params=_cparams(("parallel", "arbitrary", "arbitrary")),
    )(q, k, v, c, c)


def _fox_bwd(q, k, v, c, o, lse, do, *, name):
    h, s, p = q.shape
    t = min(FOX_BLOCK, s)
    nb = s // t
    scale = FOX_HEADDIM ** -0.5

    def body(q_ref, k_ref, v_ref, cq_ref, ck_ref, o_ref, lse_ref, do_ref,
             dq_ref, dk_ref, dv_ref, dc_ref, dk_sc, dv_sc, dc_sc):
        j, i = pl.program_id(1), pl.program_id(2)

        @pl.when(i == 0)
        def _():
            dk_sc[...] = jnp.zeros_like(dk_sc)
            dv_sc[...] = jnp.zeros_like(dv_sc)
            dc_sc[...] = jnp.zeros_like(dc_sc)

        @pl.when(i >= j)
        def _():
            qv, kv, vv = q_ref[0], k_ref[0], v_ref[0]
            sc, mask = _fox_scores(qv, kv, cq_ref[0, 0:1, 0:1], ck_ref[0], i > j)
            pr = jnp.where(mask, jnp.exp(sc - lse_ref[0, :, 0:1]), 0.0)
            dov = do_ref[0]
            dob = dov.astype(BF16)
            prb = pr.astype(BF16)
            dv_sc[...] += lax.dot_general(prb, dob, (_DIMS["tn"], ((), ())), preferred_element_type=F32)
            dp = lax.dot_general(dob, vv, (_DIMS["nt"], ((), ())), preferred_element_type=F32)
            dcol = jnp.sum(dob.astype(F32) * o_ref[0], axis=1, keepdims=True)
            ds = pr * (dp - dcol)
            dc_sc[...] -= jnp.sum(ds, axis=0, keepdims=True)
            dsb = ds.astype(BF16)
            dqc = scale * lax.dot_general(dsb, kv, (_DIMS["nn"], ((), ())), preferred_element_type=F32)
            at = pl.ds(pl.multiple_of(i * t, t), t)

            @pl.when(j == 0)
            def _():
                dq_ref[0, at, :] = dqc

            @pl.when(j > 0)
            def _():
                dq_ref[0, at, :] += dqc

            dk_sc[...] += scale * lax.dot_general(dsb, qv, (_DIMS["tn"], ((), ())), preferred_element_type=F32)

        @pl.when(i == nb - 1)
        def _():
            dk_ref[0] = dk_sc[...]
            dv_ref[0] = dv_sc[...]
            dc_ref[0] = dc_sc[...]

    qspec = pl.BlockSpec((1, t, p), lambda hh, j, i: (hh, jnp.maximum(i, j), 0))
    kspec = pl.BlockSpec((1, t, p), lambda hh, j, i: (hh, j, 0))
    cq = pl.BlockSpec((1, 1, t), lambda hh, j, i: (hh, 0, jnp.maximum(i, j)))
    ck = pl.BlockSpec((1, 1, t), lambda hh, j, i: (hh, 0, j))
    return pl.pallas_call(
        body, name=name, grid=(h, nb, nb),
        in_specs=[qspec, kspec, kspec, cq, ck, qspec,
                  pl.BlockSpec((1, t, LANES), lambda hh, j, i: (hh, jnp.maximum(i, j), 0)), qspec],
        out_specs=[pl.BlockSpec((1, s, p), lambda hh, j, i: (hh, 0, 0)), kspec, kspec, ck],
        out_shape=[jax.ShapeDtypeStruct((h, s, p), F32), jax.ShapeDtypeStruct((h, s, p), F32),
                   jax.ShapeDtypeStruct((h, s, p), F32), jax.ShapeDtypeStruct((h, 1, s), F32)],
        scratch_shapes=[pltpu.VMEM((t, p), F32), pltpu.VMEM((t, p), F32), pltpu.VMEM((1, t), F32)],
        compiler_params=_cparams(("parallel", "arbitrary", "arbitrary")),
    )(q, k, v, c, c, o, lse, do)


AUX = 64


def _pack(main, cols):
    h, s, p = main.shape
    parts = [main.astype(BF16)]
    if cols:
        parts.append(jnp.stack(cols, axis=-1).astype(BF16))
    parts.append(jnp.zeros((h, s, LANES - p - len(cols)), BF16))
    return jnp.concatenate(parts, axis=-1)


def _terms(v):
    hi = lax.reduce_precision(v, 8, 7)
    mid = lax.reduce_precision(v - hi, 8, 7)
    lo = lax.reduce_precision(v - hi - mid, 8, 7)
    return [hi, mid, lo]


def _fox_pack_qkv(q, k, v):
    h, s, _ = q.shape
    one = jnp.ones((h, s), F32)
    return _pack(q * (FOX_HEADDIM ** -0.5), []), _pack(k, []), _pack(v, [one, one, one])


def _fox_bias(c_ref, qblock, kblock, t):
    lane = lax.broadcasted_iota(jnp.int32, (1, LANES), 1)
    cq = c_ref[0, :, pl.ds(pl.multiple_of(qblock * t, LANES), LANES)]
    cref = jnp.sum(jnp.where(lane == 0, cq, 0.0), axis=1, keepdims=True)
    return cref - c_ref[0, :, pl.ds(pl.multiple_of(kblock * t, LANES), t)]


def _fox_rowdot(do, o, *, tm=256, name):
    s, d = do.shape
    tm = min(tm, s)

    def body(do_ref, o_ref, d_ref):
        prod = do_ref[...].astype(BF16).astype(F32) * o_ref[...]
        r = lax.broadcasted_iota(jnp.int32, (d, LANES), 0)
        c = lax.broadcasted_iota(jnp.int32, (d, LANES), 1)
        mine = (r >= c * FOX_HEADDIM) & (r < (c + 1) * FOX_HEADDIM)
        d_ref[...] = _mask_dot(mine.astype(BF16), prod, "vn")

    row = pl.BlockSpec((tm, d), lambda i: (i, 0))
    return pl.pallas_call(body, name=name, grid=(s // tm,), in_specs=[row, row],
                          out_specs=pl.BlockSpec((tm, LANES), lambda i: (i, 0)),
                          out_shape=jax.ShapeDtypeStruct((s, LANES), F32),
                          compiler_params=_cparams(("parallel",)))(do, o)


def _causal(t):
    return lax.broadcasted_iota(jnp.int32, (t, t), 0) >= lax.broadcasted_iota(jnp.int32, (t, t), 1)


def _fox2_fwd(qp, kp, vp, c, *, name):
    h, s, _ = qp.shape
    t = min(FOX_BLOCK, s)
    nb = s // t
    nt = (((1,), (1,)), ((), ()))
    nn = (((1,), (0,)), ((), ()))

    def body(q_ref, k_ref, v_ref, c_ref, o_ref, lse_ref, m_sc, acc_sc):
        i = pl.program_id(1)
        m_sc[...] = jnp.full_like(m_sc, NEG)
        acc_sc[...] = jnp.zeros_like(acc_sc)
        qv = q_ref[0]

        def step(j, masked):
            at = pl.ds(pl.multiple_of(j * t, t), t)
            kv, vv = k_ref[0, at, :], v_ref[0, at, :]
            sc = lax.dot_general(qv, kv, nt, preferred_element_type=F32) + _fox_bias(c_ref, i, j, t)
            if masked:
                sc = jnp.where(_causal(t), sc, NEG)
            m_prev = m_sc[...]
            m_new = jnp.maximum(m_prev, jnp.max(sc, axis=1, keepdims=True))
            pr = jnp.exp(sc - jnp.tile(m_new, (1, t // LANES)))
            pr_hi = pr.astype(BF16)
            pr_lo = (pr - pr_hi.astype(F32)).astype(BF16)
            pv = (lax.dot_general(pr_hi, vv, nn, preferred_element_type=F32)
                  + lax.dot_general(pr_lo, vv, nn, preferred_element_type=F32))
            acc_sc[...] = jnp.exp(m_prev - m_new) * acc_sc[...] + pv
            m_sc[...] = m_new

        lax.fori_loop(0, i, lambda j, carry: (step(j, False), carry)[1], 0)
        step(i, True)
        acc = acc_sc[...]
        lane = lax.broadcasted_iota(jnp.int32, (1, LANES), 1)
        den = jnp.sum(jnp.where(lane == AUX, acc, 0.0), axis=1, keepdims=True)
        o_ref[0] = (acc / den)[:, :FOX_HEADDIM]
        lse_ref[0] = m_sc[...] + jnp.log(den)

    whole = pl.BlockSpec((1, s, LANES), lambda hh, i: (hh, 0, 0))
    return pl.pallas_call(
        body, name=name, grid=(h, nb),
        in_specs=[pl.BlockSpec((1, t, LANES), lambda hh, i: (hh, i, 0)), whole, whole,
                  pl.BlockSpec((1, 1, s), lambda hh, i: (hh, 0, 0))],
        out_specs=[pl.BlockSpec((1, t, FOX_HEADDIM), lambda hh, i: (hh, i, 0)),
                   pl.BlockSpec((1, t, LANES), lambda hh, i: (hh, i, 0))],
        out_shape=[jax.ShapeDtypeStruct((h, s, FOX_HEADDIM), F32), jax.ShapeDtypeStruct((h, s, LANES), F32)],
        scratch_shapes=[pltpu.VMEM((t, LANES), F32), pltpu.VMEM((t, LANES), F32)],
        compiler_params=_cparams(("parallel", "arbitrary")),
    )(qp, kp, vp, c)


def _fox2_bwd(qp, kp, vp, c, dop, lse, *, name):
    h, s, _ = qp.shape
    t = min(FOX_BLOCK, s)
    nb = s // t
    nt = (((1,), (1,)), ((), ()))
    nn = (((1,), (0,)), ((), ()))
    tn = (((0,), (0,)), ((), ()))

    def body(k_ref, v_ref, q_ref, c_ref, do_ref, lse_ref, dq_ref, dk_ref, dv_ref, dc_ref, dk_sc, dv_sc, dc_sc):
        j = pl.program_id(1)

        @pl.when(j == 0)
        def _():
            dq_ref[...] = jnp.zeros_like(dq_ref)

        dk_sc[...] = jnp.zeros_like(dk_sc)
        dv_sc[...] = jnp.zeros_like(dv_sc)
        dc_sc[...] = jnp.zeros_like(dc_sc)
        kv, vv = k_ref[0], v_ref[0]

        def step(i, masked):
            at = pl.ds(pl.multiple_of(i * t, t), t)
            qv, dov = q_ref[0, at, :], do_ref[0, at, :]
            sc = lax.dot_general(qv, kv, nt, preferred_element_type=F32) + _fox_bias(c_ref, i, j, t)
            pr = jnp.exp(sc - jnp.tile(lse_ref[0, at, :], (1, t // LANES)))
            if masked:
                pr = jnp.where(_causal(t), pr, 0.0)
            ds = pr * lax.dot_general(dov, vv, nt, preferred_element_type=F32)
            dc_sc[...] -= jnp.sum(ds, axis=0, keepdims=True)
            dsb = ds.astype(BF16)
            dv_sc[...] += lax.dot_general(pr.astype(BF16), dov, tn, preferred_element_type=F32)
            dk_sc[...] += lax.dot_general(dsb, qv, tn, preferred_element_type=F32)
            dq_ref[0, at, :] += lax.dot_general(dsb, kv, nn, preferred_element_type=F32)

        step(j, True)
        lax.fori_loop(j + 1, nb, lambda i, carry: (step(i, False), carry)[1], 0)
        dk_ref[0] = dk_sc[...]
        dv_ref[0] = dv_sc[...].astype(dv_ref.dtype)
        dc_ref[0] = dc_sc[...]

    whole = pl.BlockSpec((1, s, LANES), lambda hh, j: (hh, 0, 0))
    blk = pl.BlockSpec((1, t, LANES), lambda hh, j: (hh, j, 0))
    return pl.pallas_call(
        body, name=name, grid=(h, nb),
        in_specs=[blk, blk, whole, pl.BlockSpec((1, 1, s), lambda hh, j: (hh, 0, 0)), whole, whole],
        out_specs=[whole, blk, blk, pl.BlockSpec((1, 1, t), lambda hh, j: (hh, 0, j))],
        out_shape=[jax.ShapeDtypeStruct((h, s, LANES), F32), jax.ShapeDtypeStruct((h, s, LANES), F32),
                   jax.ShapeDtypeStruct((h, s, LANES), BF16), jax.ShapeDtypeStruct((h, 1, s), F32)],
        scratch_shapes=[pltpu.VMEM((t, LANES), F32), pltpu.VMEM((t, LANES), F32), pltpu.VMEM((1, t), F32)],
        compiler_params=_cparams(("parallel", "arbitrary")),
    )(kp, vp, qp, c, dop, lse)


def _final(x1, out1, g, tgt, *, tm=256, name):
    s, d = x1.shape
    tm = min(tm, s)

    def body(x_ref, o_ref, g_ref, t_ref, dx_ref, do_ref, dg_ref, loss_ref):
        i = pl.program_id(0)

        @pl.when(i == 0)
        def _():
            dg_ref[...] = jnp.zeros_like(dg_ref)
            loss_ref[...] = jnp.zeros_like(loss_ref)

        tv = t_ref[...]

        def lossf(xv, ov, gv):
            err = jnp.square(xv + _rms(ov, gv) - tv)
            return 0.5 * jnp.sum(jnp.mean(err, axis=-1, keepdims=True), axis=0, keepdims=True)

        val, vjp = jax.vjp(lossf, x_ref[...], o_ref[...], g_ref[...])
        dx, do, dg = vjp(jnp.ones((1, 1), F32))
        dx_ref[...] = dx
        do_ref[...] = do.astype(do_ref.dtype)
        dg_ref[...] += dg
        loss_ref[...] += val

    row = pl.BlockSpec((tm, d), lambda i: (i, 0))
    par = pl.BlockSpec((1, d), lambda i: (0, 0))
    return pl.pallas_call(
        body, name=name, grid=(s // tm,), in_specs=[row, row, par, row],
        out_specs=[row, row, par, pl.BlockSpec((1, 1), lambda i: (0, 0))],
        out_shape=[jax.ShapeDtypeStruct((s, d), F32), jax.ShapeDtypeStruct((s, d), BF16),
                   jax.ShapeDtypeStruct((1, d), F32), jax.ShapeDtypeStruct((1, 1), F32)],
        compiler_params=_cparams(("arbitrary",)),
    )(x1, out1, g, tgt)


def _row_tile(r):
    return LANES if r % LANES == 0 else r


def _sum_slots(parts, *, out_dtype=F32, name):
    p, r, c = parts.shape
    tr = _row_tile(r)

    def body(p_ref, o_ref):
        acc = p_ref[0].astype(F32)
        for k in range(1, p):
            acc = acc + p_ref[k].astype(F32)
        o_ref[...] = acc.astype(o_ref.dtype)

    return pl.pallas_call(
        body, name=name, grid=(r // tr,),
        in_specs=[pl.BlockSpec((p, tr, c), lambda i: (0, i, 0))],
        out_specs=pl.BlockSpec((tr, c), lambda i: (i, 0)),
        out_shape=jax.ShapeDtypeStruct((r, c), out_dtype),
        compiler_params=_cparams(("parallel",)),
    )(parts)


def _adamw(w, gparts, m, v, *, name):
    r, c = w.shape
    p = gparts.shape[0]
    tr = _row_tile(r)

    def body(w_ref, g_ref, m_ref, v_ref, go_ref, d_ref, mo_ref, vo_ref):
        g = g_ref[0].astype(F32)
        for k in range(1, p):
            g = g + g_ref[k].astype(F32)
        mn = ADAM_B1 * m_ref[...] + (1.0 - ADAM_B1) * g
        vn = ADAM_B2 * v_ref[...] + (1.0 - ADAM_B2) * jnp.square(g)
        m_hat = mn / (1.0 - ADAM_B1 ** ADAM_STEP)
        v_hat = vn / (1.0 - ADAM_B2 ** ADAM_STEP)
        go_ref[...] = g
        d_ref[...] = -ADAM_LR * (m_hat / (jnp.sqrt(v_hat) + ADAM_EPS) + ADAM_WD * w_ref[...])
        mo_ref[...] = mn
        vo_ref[...] = vn

    spec = pl.BlockSpec((tr, c), lambda i: (i, 0))
    return pl.pallas_call(
        body, name=name, grid=(r // tr,),
        in_specs=[spec, pl.BlockSpec((p, tr, c), lambda i: (0, i, 0)), spec, spec],
        out_specs=[spec] * 4, out_shape=[jax.ShapeDtypeStruct((r, c), F32)] * 4,
        compiler_params=_cparams(("parallel",)),
    )(w, gparts, m, v)


_FLIPS = {
    "xy": [(1, 0, 0), (0, 1, 0), (1, 1, 0)],
    "c": [(0, 0, 1)],
    "xyc": [(fx, fy, fc) for fx in (0, 1) for fy in (0, 1) for fc in (0, 1) if (fx, fy, fc) != (0, 0, 0)],
}


def _slot(mode, px, py, pc):
    return {"xy": 2 * px + py, "c": pc, "xyc": 4 * px + 2 * py + pc}[mode]


def _exchange(arrs, mode, scatter, *, pieces=1, name):
    n = len(arrs)
    flips = _FLIPS[mode]
    nf = len(flips)

    def body(*refs):
        ins, outs = refs[:n], refs[n:2 * n]
        send, recv, loc = refs[2 * n:]
        x, y, c = lax.axis_index("x"), lax.axis_index("y"), lax.axis_index("c")
        me = _slot(mode, x, y, c)
        peers = [(x ^ fx, y ^ fy, c ^ fc) for (fx, fy, fc) in flips]

        def src(a, slot):
            return ins[a].at[slot] if scatter else ins[a]

        def copy(a, j, piece, dst_slot):
            block = src(a, _slot(mode, *peers[j]))
            rows = block.shape[0] // pieces
            at = pl.ds(piece * rows, rows)
            sem = (a * nf + j) * pieces + piece
            return pltpu.make_async_remote_copy(
                src_ref=block.at[at], dst_ref=outs[a].at[dst_slot].at[at],
                send_sem=send.at[sem], recv_sem=recv.at[sem], device_id=peers[j], device_id_type=MESH)

        local = [pltpu.make_async_copy(src(a, me), outs[a].at[me], loc.at[a]) for a in range(n)]
        for lc in local:
            lc.start()
        sends = [copy(a, j, p, me) for a in range(n) for j in range(nf) for p in range(pieces)]
        for cp in sends:
            cp.start()
        for a in range(n):
            for j in range(nf):
                for p in range(pieces):
                    copy(a, j, p, _slot(mode, *peers[j])).wait_recv()
        for cp in sends:
            cp.wait_send()
        for lc in local:
            lc.wait()

    anyspec = pl.BlockSpec(memory_space=pl.ANY)
    return pl.pallas_call(
        body, name=name, in_specs=[anyspec] * n, out_specs=[anyspec] * n,
        out_shape=[jax.ShapeDtypeStruct((nf + 1,) + (a.shape[1:] if scatter else a.shape), a.dtype) for a in arrs],
        scratch_shapes=[pltpu.SemaphoreType.DMA((n * nf * pieces,)), pltpu.SemaphoreType.DMA((n * nf * pieces,)),
                        pltpu.SemaphoreType.DMA((n,))],
    )(*arrs)


def _softplus(v):
    return jnp.maximum(v, 0.0) + jnp.log1p(jnp.exp(-jnp.abs(v)))


def _pad_lanes(v):
    r, n = v.shape
    return jnp.pad(v, ((0, 0), (0, -n % LANES)))


def _to_heads(v):
    s = v.shape[0]
    return v.reshape(s, -1, 64).transpose(1, 0, 2)


def _from_heads(v):
    h, s, p = v.shape
    return v.transpose(1, 0, 2).reshape(s, h * p)


def _fn_rms(v, g):
    return (_rms(v, g),)


def _fn_post(xv, ov, g):
    return (xv + _rms(ov, g),)


def _fn_act(xbc, dtp, fp, dtb, fb):
    return _silu(xbc), _softplus(dtp + dtb), -_softplus(-(fp + fb))


def _fn_mix(y, zs, o, zf, g):
    yg = y * _silu(zs)
    sq = yg * yg
    lane = lax.broadcasted_iota(jnp.int32, (1, D_SSD), 1)
    width = D_SSD // SSD_GROUPS
    rstd = jnp.zeros_like(yg)
    for gi in range(SSD_GROUPS):
        msk = ((lane >= gi * width) & (lane < (gi + 1) * width)).astype(F32)
        ms = jnp.sum(sq * msk, axis=1, keepdims=True) / width
        rstd = rstd + lax.rsqrt(ms + EPS) * msk
    return (jnp.concatenate([yg * rstd * g, o * _silu(zf)], axis=1),)


def _fn_glu(val, gate):
    return (val * jax.nn.sigmoid(gate),)


def _fn_ln(hc, z, g, b):
    mu = jnp.mean(hc, axis=-1, keepdims=True)
    xc = hc - mu
    yn = xc * lax.rsqrt(jnp.mean(xc * xc, axis=-1, keepdims=True) + EPS) * g + b
    return (_silu(yn) * _silu(z),)


def _local_step(x, tgt, w):
    s = x.shape[0]
    d = D_MODEL
    tm = 256
    bf = lambda v: v.astype(BF16)
    c1 = lambda arr: _col(arr, 0, arr.shape[1])
    g = {}

    ew = w["e_w_in"]
    w_z, w_xbc = bf(ew[:, 0:2048]), bf(ew[:, 2048:4096])
    w_dt = bf(_pad_lanes(ew[:, 4096:4112]))
    w_qkv = bf(ew[:, 4112:7184])
    w_f = bf(_pad_lanes(ew[:, 7184:7200]))
    w_eo = bf(w["e_w_out"])
    dtb, fgb = _pad_lanes(w["e_dt_bias"]), _pad_lanes(w["e_fgate_b"])
    alog, dsk = _pad_lanes(w["e_a_log"]), _pad_lanes(w["e_d_skip"])
    w_oi, w_oo = bf(w["o_w_in"]), bf(w["o_w_out"])

    (u0,) = _rowwise_fwd(_fn_rms, [c1(x)], [c1(w["e_norm_pre"])], [(d, BF16)], tm=tm, name="e_pre")
    z = _mm(u0, w_z, name="e_in_z")
    xbc_raw = _mm(u0, w_xbc, name="e_in_xbc")
    qkv = _mm(u0, w_qkv, out_dtype=BF16, name="e_in_qkv")
    dtp = _mm(u0, w_dt, name="e_in_dt")
    fp = _mm(u0, w_f, name="e_in_f")
    xbc_pre = _conv_fwd(xbc_raw, w["e_conv_w"], w["e_conv_b"], name="e_conv")
    act_rows = [c1(xbc_pre), c1(dtp), c1(fp)]
    act_pars = [c1(dtb), c1(fgb)]
    xbc, dt, lf = _rowwise_fwd(_fn_act, act_rows, act_pars, [(2048, F32), (LANES, F32), (LANES, F32)],
                               tm=tm, name="e_act")
    xs_hm = _to_heads(xbc[:, :D_SSD])
    y_hm, hsave = _ssd_fwd(xs_hm, dt, xbc, alog, dsk, name="e_ssd")
    y = _from_heads(y_hm)
    csum = _cumsum_lanes(lf[:, :FOX_HEADS].T, reverse=False, name="e_cumsum").reshape(FOX_HEADS, 1, s)
    qkv_hm = qkv.reshape(s, 3, FOX_HEADS, FOX_HEADDIM).transpose(1, 2, 0, 3)
    qp, kp, vp = _fox_pack_qkv(qkv_hm[0], qkv_hm[1], qkv_hm[2])
    o_hm, lse = _fox2_fwd(qp, kp, vp, csum, name="e_fox")
    o = _from_heads(o_hm)
    mix_rows = [c1(y), _col(z, 0, D_SSD), c1(o), _col(z, 1, D_FOX)]
    mix_pars = [c1(w["e_ssd_norm"])]
    (hmix,) = _rowwise_fwd(_fn_mix, mix_rows, mix_pars, [(2048, BF16)], tm=tm, name="e_mix")
    out0 = _mm(hmix, w_eo, name="e_out")
    post_rows = [c1(x), c1(out0)]
    (x1,) = _rowwise_fwd(_fn_post, post_rows, [c1(w["e_norm_post"])], [(d, F32)], tm=tm, name="e_post")

    (u1,) = _rowwise_fwd(_fn_rms, [c1(x1)], [c1(w["o_norm_pre"])], [(d, BF16)], tm=tm, name="o_pre")
    p1 = _mm(u1, w_oi, name="o_in")
    glu_rows = [_col(p1, 0, D_CONV), _col(p1, 1, D_CONV)]
    (hg,) = _rowwise_fwd(_fn_glu, glu_rows, [], [(D_CONV, F32)], tm=tm, name="o_glu")
    hc = _conv_fwd(hg, w["o_conv_w"], w["o_conv_b"], name="o_conv")
    ln_rows = [c1(hc), _col(p1, 2, D_CONV)]
    ln_pars = [c1(w["o_ln_g"]), c1(w["o_ln_b"])]
    (h2,) = _rowwise_fwd(_fn_ln, ln_rows, ln_pars, [(D_CONV, BF16)], tm=tm, name="o_ln")
    out1 = _mm(h2, w_oo, name="o_out")

    dx2, dout1, g["o_norm_post"], loss = _final(x1, out1, w["o_norm_post"], tgt, name="loss_head")
    dh2 = _mm(dout1, w_oo, tb=True, name="o_out_dx")
    g["o_w_out"] = _mm(h2, dout1, ta=True, name="o_out_dw")
    (dhc, dz1), (g["o_ln_g"], g["o_ln_b"]) = _rowwise_bwd(_fn_ln, ln_rows, ln_pars, [c1(dh2)], [F32, BF16],
                                                         tm=tm, name="o_ln_bwd")
    dhg = _conv_bwd_x(dhc, w["o_conv_w"], name="o_conv_dx")
    g["o_conv_w"], g["o_conv_b"] = _conv_bwd_w(hg, dhc, CONV_WIDTH, name="o_conv_dw")
    (dval, dgate), _ = _rowwise_bwd(_fn_glu, glu_rows, [], [c1(dhg)], [BF16, BF16], tm=tm, name="o_glu_bwd")
    du1 = _mm(dval, w_oi[:, 0:2048], tb=True, name="o_in_dx0")
    du1 = _mm(dgate, w_oi[:, 2048:4096], tb=True, add=du1, name="o_in_dx1")
    du1 = _mm(dz1, w_oi[:, 4096:6144], tb=True, add=du1, name="o_in_dx2")
    g["o_w_in"] = jnp.concatenate([_mm(u1, dval, ta=True, name="o_in_dw0"), _mm(u1, dgate, ta=True, name="o_in_dw1"),
                                   _mm(u1, dz1, ta=True, name="o_in_dw2")], axis=1)
    (dx1,), (g["o_norm_pre"],) = _rowwise_bwd(_fn_rms, [c1(x1)], [c1(w["o_norm_pre"])], [c1(du1)], [F32],
                                              adds={0: c1(dx2)}, tm=tm, name="o_pre_bwd")

    (dout0,), (g["e_norm_post"],) = _rowwise_bwd(_fn_post, post_rows, [c1(w["e_norm_post"])], [c1(dx1)],
                                                 [None, BF16], tm=tm, name="e_post_bwd")
    dhmix = _mm(dout0, w_eo, tb=True, name="e_out_dx")
    g["e_w_out"] = _mm(hmix, dout0, ta=True, name="e_out_dw")
    (dy, dzs, do, dzf), (g["e_ssd_norm"],) = _rowwise_bwd(_fn_mix, mix_rows, mix_pars, [c1(dhmix)],
                                                        [F32, BF16, F32, BF16], tm=tm, name="e_mix_bwd")
    drow = _fox_rowdot(do, o, name="e_fox_rowdot")
    dop = _pack(_to_heads(do), _terms(-drow[:, :FOX_HEADS].T))
    dqp, dkp, dvp, dcs = _fox2_bwd(qp, kp, vp, csum, dop, lse, name="e_fox_bwd")
    dq = bf(dqp[:, :, :FOX_HEADDIM] * (FOX_HEADDIM ** -0.5))
    dqkv = jnp.stack([dq, bf(dkp[:, :, :FOX_HEADDIM]), dvp[:, :, :FOX_HEADDIM]])
    dqkv = dqkv.transpose(2, 0, 1, 3).reshape(s, 3 * D_FOX)
    dlf = _pad_lanes(_cumsum_lanes(dcs.reshape(FOX_HEADS, s), reverse=True, name="e_cumsum_bwd").T)
    dxs_hm, ddt, dbm, dcm, dalog, ddsk = _ssd_bwd(xs_hm, dt, xbc, alog, dsk, hsave, _to_heads(dy), name="e_ssd_bwd")
    dxbc = jnp.concatenate([_from_heads(dxs_hm), dbm, dcm], axis=1)
    (dxbc_pre, ddtp, dfp), (ddtb, dfgb) = _rowwise_bwd(_fn_act, act_rows, act_pars, [c1(dxbc), c1(ddt), c1(dlf)],
                                                      [F32, BF16, BF16], tm=tm, name="e_act_bwd")
    dxbc_raw = bf(_conv_bwd_x(dxbc_pre, w["e_conv_w"], name="e_conv_dx"))
    g["e_conv_w"], g["e_conv_b"] = _conv_bwd_w(xbc_raw, dxbc_pre, SSD_CONV, name="e_conv_dw")
    du0 = _mm(dzs, w_z[:, :D_SSD], tb=True, name="e_in_dx0")
    du0 = _mm(dzf, w_z[:, D_SSD:], tb=True, add=du0, name="e_in_dx1")
    du0 = _mm(dxbc_raw, w_xbc, tb=True, add=du0, name="e_in_dx2")
    du0 = _mm(dqkv, w_qkv, tb=True, add=du0, name="e_in_dx3")
    du0 = _mm(ddtp, w_dt, tb=True, add=du0, name="e_in_dx4")
    du0 = _mm(dfp, w_f, tb=True, add=du0, name="e_in_dx5")
    g["e_w_in"] = jnp.concatenate([
        _mm(u0, dzs, ta=True, name="e_in_dw0"), _mm(u0, dzf, ta=True, name="e_in_dw1"),
        _mm(u0, dxbc_raw, ta=True, name="e_in_dw2"), _mm(u0, ddtp, ta=True, name="e_in_dw3")[:, :SSD_HEADS],
        _mm(u0, dqkv, ta=True, name="e_in_dw4"), _mm(u0, dfp, ta=True, name="e_in_dw5")[:, :FOX_HEADS]], axis=1)
    (dx,), (g["e_norm_pre"],) = _rowwise_bwd(_fn_rms, [c1(x)], [c1(w["e_norm_pre"])], [c1(du0)], [F32],
                                             adds={0: c1(dx1)}, tm=tm, name="e_pre_bwd")
    g["e_dt_bias"], g["e_fgate_b"] = ddtb[:, :SSD_HEADS], dfgb[:, :FOX_HEADS]
    g["e_a_log"], g["e_d_skip"] = dalog[:, :SSD_HEADS], ddsk[:, :SSD_HEADS]
    return loss, dx, g


_WEIGHTS = ["e_norm_pre", "e_w_in", "e_conv_w", "e_conv_b", "e_dt_bias", "e_a_log", "e_d_skip", "e_fgate_b",
            "e_ssd_norm", "e_w_out", "e_norm_post", "o_norm_pre", "o_w_in", "o_conv_w", "o_conv_b", "o_ln_g",
            "o_ln_b", "o_w_out", "o_norm_post"]
_BIG = ["e_w_in", "e_w_out", "o_w_in", "o_w_out"]
_ROW_SHARDED = ["e_w_out", "o_w_out"]
_SMALL_SHARDED = ["e_conv_w", "o_norm_pre", "o_conv_w", "o_conv_b", "o_ln_g", "o_ln_b", "o_norm_post"]
_REPLICATED = ["e_norm_pre", "e_conv_b", "e_dt_bias", "e_a_log", "e_d_skip", "e_fgate_b", "e_ssd_norm", "e_norm_post"]
_SMALL = [n for n in _WEIGHTS if n not in _BIG]
N_CHIPS = 4


def _join(gathered, rows):
    k, r, c = gathered.shape
    return gathered.reshape(k * r, c) if rows else gathered.transpose(1, 0, 2).reshape(r, k * c)


def _split(full, rows):
    r, c = full.shape
    return full.reshape(N_CHIPS, r // N_CHIPS, c) if rows else full.reshape(r, N_CHIPS, c // N_CHIPS).transpose(1, 0, 2)


def kernel(x, e_norm_pre, e_w_in, e_conv_w, e_conv_b, e_dt_bias, e_a_log, e_d_skip, e_fgate_b, e_ssd_norm, e_w_out, e_norm_post, o_norm_pre, o_w_in, o_conv_w, o_conv_b, o_ln_g, o_ln_b, o_w_out, o_norm_post, loss_target, m_e_norm_pre, m_e_w_in, m_e_conv_w, m_e_conv_b, m_e_dt_bias, m_e_a_log, m_e_d_skip, m_e_fgate_b, m_e_ssd_norm, m_e_w_out, m_e_norm_post, m_o_norm_pre, m_o_w_in, m_o_conv_w, m_o_conv_b, m_o_ln_g, m_o_ln_b, m_o_w_out, m_o_norm_post, v_e_norm_pre, v_e_w_in, v_e_conv_w, v_e_conv_b, v_e_dt_bias, v_e_a_log, v_e_d_skip, v_e_fgate_b, v_e_ssd_norm, v_e_w_out, v_e_norm_post, v_o_norm_pre, v_o_w_in, v_o_conv_w, v_o_conv_b, v_o_ln_g, v_o_ln_b, v_o_w_out, v_o_norm_post):
    wvals = (e_norm_pre, e_w_in, e_conv_w, e_conv_b, e_dt_bias, e_a_log, e_d_skip, e_fgate_b, e_ssd_norm, e_w_out,
             e_norm_post, o_norm_pre, o_w_in, o_conv_w, o_conv_b, o_ln_g, o_ln_b, o_w_out, o_norm_post)
    mvals = (m_e_norm_pre, m_e_w_in, m_e_conv_w, m_e_conv_b, m_e_dt_bias, m_e_a_log, m_e_d_skip, m_e_fgate_b,
             m_e_ssd_norm, m_e_w_out, m_e_norm_post, m_o_norm_pre, m_o_w_in, m_o_conv_w, m_o_conv_b, m_o_ln_g,
             m_o_ln_b, m_o_w_out, m_o_norm_post)
    vvals = (v_e_norm_pre, v_e_w_in, v_e_conv_w, v_e_conv_b, v_e_dt_bias, v_e_a_log, v_e_d_skip, v_e_fgate_b,
             v_e_ssd_norm, v_e_w_out, v_e_norm_post, v_o_norm_pre, v_o_w_in, v_o_conv_w, v_o_conv_b, v_o_ln_g,
             v_o_ln_b, v_o_w_out, v_o_norm_post)

    def mat(v):
        return v.reshape(v.shape[-2:]) if v.ndim == 3 else v

    w = {n: mat(v) for n, v in zip(_WEIGHTS, wvals)}
    m = {n: mat(v) for n, v in zip(_WEIGHTS, mvals)}
    v2 = {n: mat(v) for n, v in zip(_WEIGHTS, vvals)}
    me_xy = 2 * lax.axis_index("x") + lax.axis_index("y")

    sharded = _BIG + _SMALL_SHARDED
    gathered = _exchange([w[n].astype(BF16) if n in _BIG else w[n] for n in sharded], "xy", False, name="gather_weights")
    full = {n: w[n] for n in _REPLICATED}
    for n, gth in zip(sharded, gathered):
        full[n] = _join(gth, n in _ROW_SHARDED)

    loss, dx, g = _local_step(x[0], loss_target[0], full)
    loss = lax.psum(loss[0, 0], ("x", "y", "c"))

    scattered = _exchange([_split(g[n], n in _ROW_SHARDED).astype(BF16) for n in _BIG], "xy", True, name="scatter_grads")
    partial = [_sum_slots(p, out_dtype=BF16, name="sum_" + n) for n, p in zip(_BIG, scattered)]
    pairs = _exchange(partial, "c", False, pieces=4, name="pair_grads")
    gparts = dict(zip(_BIG, pairs))

    flat = jnp.concatenate([_pad_lanes(g[n].reshape(1, -1)) for n in _SMALL], axis=1).reshape(-1, LANES)
    (all8,) = _exchange([flat], "xyc", False, name="gather_small_grads")
    total = _sum_slots(all8, name="sum_small").reshape(1, -1)
    at = 0
    for n in _SMALL:
        size = g[n].size
        gn = total[:, at:at + size].reshape(g[n].shape)
        at += size + (-size % LANES)
        if n in _SMALL_SHARDED:
            cols = gn.shape[1] // N_CHIPS
            gn = lax.dynamic_slice(gn, (0, me_xy * cols), (gn.shape[0], cols))
        gparts[n] = gn[None]

    grads, deltas, new_m, new_v = [], [], [], []
    for n, orig in zip(_WEIGHTS, wvals):
        gn, dn, mn, vn = _adamw(w[n], gparts[n], m[n], v2[n], name="adamw_" + n)
        for lst, val in zip((grads, deltas, new_m, new_v), (gn, dn, mn, vn)):
            lst.append(val.reshape(orig.shape))
    return (loss, dx[None], *grads, *deltas, *new_m, *new_v)
```

```python
import functools

import jax
import jax.numpy as jnp
from jax import lax
from jax.experimental import pallas as pl
from jax.experimental.pallas import tpu as pltpu

F32 = jnp.float32
BF16 = jnp.bfloat16
MESH = pl.DeviceIdType.MESH

D_MODEL = 1024
D_SSD = 1024
SSD_HEADS = 16
SSD_HEADDIM = 64
SSD_GROUPS = 4
SSD_HPG = 4
D_STATE = 128
SSD_CONV = 4
CHUNK = 128
D_FOX = 1024
FOX_HEADS = 16
FOX_HEADDIM = 64
D_CONV = 2048
CONV_WIDTH = 31
EPS = 1e-6
LANES = 128
VMEM_LIMIT = 56 * 1024 * 1024

ADAM_LR = 0.001
ADAM_B1 = 0.9
ADAM_B2 = 0.999
ADAM_EPS = 1e-08
ADAM_WD = 0.01
ADAM_STEP = 10


def _cparams(sem=None):
    return pltpu.CompilerParams(dimension_semantics=sem, vmem_limit_bytes=VMEM_LIMIT)


def _mm(a, b, *, ta=False, tb=False, add=None, out_dtype=F32, tm=1024, tn=None, tk=2048, name):
    m = a.shape[1] if ta else a.shape[0]
    k = a.shape[0] if ta else a.shape[1]
    n = b.shape[0] if tb else b.shape[1]
    if tn is None:
        tn = 1024 if ta else 512
    tm, tn = min(tm, m), min(tn, n)
    tk = max(t for t in range(LANES, min(tk, k) + 1, LANES) if k % t == 0)
    assert m % tm == 0 and n % tn == 0 and k % tk == 0, (m, n, k, tm, tn, tk)
    nk = k // tk
    dims = (((0 if ta else 1,), (1 if tb else 0,)), ((), ()))

    def body(*refs):
        if add is None:
            a_ref, b_ref, o_ref, acc_ref = refs
            c_ref = None
        else:
            a_ref, b_ref, c_ref, o_ref, acc_ref = refs
        kk = pl.program_id(2)

        @pl.when(kk == 0)
        def _():
            if c_ref is None:
                acc_ref[...] = jnp.zeros_like(acc_ref)
            else:
                acc_ref[...] = c_ref[...].astype(F32)

        acc_ref[...] += lax.dot_general(a_ref[...].astype(BF16), b_ref[...].astype(BF16), dims,
                                        preferred_element_type=F32)

        @pl.when(kk == nk - 1)
        def _():
            o_ref[...] = acc_ref[...].astype(o_ref.dtype)

    a_spec = (pl.BlockSpec((tk, tm), lambda i, j, kk: (kk, i)) if ta
              else pl.BlockSpec((tm, tk), lambda i, j, kk: (i, kk)))
    b_spec = (pl.BlockSpec((tn, tk), lambda i, j, kk: (j, kk)) if tb
              else pl.BlockSpec((tk, tn), lambda i, j, kk: (kk, j)))
    o_spec = pl.BlockSpec((tm, tn), lambda i, j, kk: (i, j))
    in_specs, args = [a_spec, b_spec], [a, b]
    if add is not None:
        in_specs.append(o_spec)
        args.append(add)
    return pl.pallas_call(
        body, name=name, grid=(m // tm, n // tn, nk),
        in_specs=in_specs, out_specs=o_spec,
        out_shape=jax.ShapeDtypeStruct((m, n), out_dtype),
        scratch_shapes=[pltpu.VMEM((tm, tn), F32)],
        compiler_params=_cparams(("parallel", "parallel", "arbitrary")),
    )(*args)


def _col(arr, cb, width):
    return (arr, cb, width)


def _row_specs(ops, tm):
    return [pl.BlockSpec((tm, w), lambda i, cb=cb: (i, cb)) for (_, cb, w) in ops]


def _par_specs(ops):
    return [pl.BlockSpec((a.shape[0], w), lambda i, cb=cb: (0, cb)) for (a, cb, w) in ops]


def _rowwise_fwd(fn, rows, params, outs, *, tm, name):
    s = rows[0][0].shape[0]
    tm = min(tm, s)
    nr, npar = len(rows), len(params)

    def body(*refs):
        rv = [r[...].astype(F32) for r in refs[:nr]]
        pv = [p[...].astype(F32) for p in refs[nr:nr + npar]]
        res = fn(*rv, *pv)
        for o_ref, val in zip(refs[nr + npar:], res):
            o_ref[...] = val.astype(o_ref.dtype)

    return pl.pallas_call(
        body, name=name, grid=(s // tm,),
        in_specs=_row_specs(rows, tm) + _par_specs(params),
        out_specs=[pl.BlockSpec((tm, w), lambda i: (i, 0)) for (w, _) in outs],
        out_shape=[jax.ShapeDtypeStruct((s, w), dt) for (w, dt) in outs],
        compiler_params=_cparams(("parallel",)),
    )(*[r[0] for r in rows], *[p[0] for p in params])


def _rowwise_bwd(fn, rows, params, couts, row_grads, *, adds=None, tm, name):
    adds = adds or {}
    s = rows[0][0].shape[0]
    tm = min(tm, s)
    nr, npar, nc = len(rows), len(params), len(couts)
    add_keys = sorted(adds)
    want = [i for i, dt in enumerate(row_grads) if dt is not None]

    def body(*refs):
        i = pl.program_id(0)
        rv = [r[...].astype(F32) for r in refs[:nr]]
        pv = [p[...].astype(F32) for p in refs[nr:nr + npar]]
        cv = [c[...].astype(F32) for c in refs[nr + npar:nr + npar + nc]]
        av = {k: r[...].astype(F32) for k, r in zip(add_keys, refs[nr + npar + nc:nr + npar + nc + len(add_keys)])}
        orefs = refs[nr + npar + nc + len(add_keys):]
        _, vjp = jax.vjp(lambda rr, pp: tuple(fn(*rr, *pp)), rv, pv)
        drows, dpars = vjp(tuple(cv))
        for o_ref, ri in zip(orefs[:len(want)], want):
            g = drows[ri]
            if ri in av:
                g = g + av[ri]
            o_ref[...] = g.astype(o_ref.dtype)

        @pl.when(i == 0)
        def _():
            for o_ref in orefs[len(want):]:
                o_ref[...] = jnp.zeros_like(o_ref)

        for o_ref, g in zip(orefs[len(want):], dpars):
            o_ref[...] += g

    add_ops = [adds[k] for k in add_keys]
    out_specs = ([pl.BlockSpec((tm, rows[ri][2]), lambda i: (i, 0)) for ri in want]
                 + [pl.BlockSpec((p[0].shape[0], p[2]), lambda i: (0, 0)) for p in params])
    out_shape = ([jax.ShapeDtypeStruct((s, rows[ri][2]), row_grads[ri]) for ri in want]
                 + [jax.ShapeDtypeStruct((p[0].shape[0], p[2]), F32) for p in params])
    res = pl.pallas_call(
        body, name=name, grid=(s // tm,),
        in_specs=_row_specs(rows, tm) + _par_specs(params) + _row_specs(couts, tm) + _row_specs(add_ops, tm),
        out_specs=out_specs, out_shape=out_shape,
        compiler_params=_cparams(("arbitrary",)),
    )(*[r[0] for r in rows], *[p[0] for p in params], *[c[0] for c in couts], *[a[0] for a in add_ops])
    return res[:len(want)], res[len(want):]


def _silu(v):
    return v * jax.nn.sigmoid(v)


def _rms(v, g):
    return v * lax.rsqrt(jnp.mean(v * v, axis=-1, keepdims=True) + EPS) * g


SUBLANES = 8
CONV_ROWS = 256


def _halo(shifts):
    up = lambda v: -(-v // SUBLANES) * SUBLANES
    return up(max(0, -min(shifts))), up(max(0, max(shifts)))


def _fill_halo(xp_sc, x_ref, front, back):
    s = x_ref.shape[0]
    if front:
        xp_sc[0:front, :] = jnp.zeros((front, LANES), F32)
    if back:
        xp_sc[front + s:front + s + back, :] = jnp.zeros((back, LANES), F32)
    xp_sc[front:front + s, :] = x_ref[...]


def _shift_conv(x, w, b, shifts, *, name):
    s, c = x.shape
    tr = min(CONV_ROWS, s)
    nk = len(shifts)
    front, back = _halo(shifts)

    def body(*refs):
        if b is None:
            x_ref, w_ref, o_ref, xp_sc = refs
        else:
            x_ref, w_ref, b_ref, o_ref, xp_sc = refs
        _fill_halo(xp_sc, x_ref, front, back)

        def chunk(r, carry):
            base = pl.multiple_of(r * tr, tr)
            acc = jnp.zeros((tr, LANES), F32) if b is None else jnp.broadcast_to(b_ref[...], (tr, LANES))
            for kk in range(nk):
                acc = acc + xp_sc[pl.ds(base + front + shifts[kk], tr), :] * w_ref[kk:kk + 1, :]
            o_ref[pl.ds(base, tr), :] = acc
            return carry

        lax.fori_loop(0, s // tr, chunk, 0)

    strip = pl.BlockSpec((s, LANES), lambda cb: (0, cb))
    in_specs = [strip, pl.BlockSpec((nk, LANES), lambda cb: (0, cb))]
    args = [x, w]
    if b is not None:
        in_specs.append(pl.BlockSpec((1, LANES), lambda cb: (0, cb)))
        args.append(b)
    return pl.pallas_call(
        body, name=name, grid=(c // LANES,), in_specs=in_specs, out_specs=strip,
        out_shape=jax.ShapeDtypeStruct((s, c), F32),
        scratch_shapes=[pltpu.VMEM((front + s + back, LANES), F32)],
        compiler_params=_cparams(("parallel",)),
    )(*args)


def _conv_fwd(x, w, b, *, name):
    k = w.shape[0]
    return _shift_conv(x, w, b, [kk - (k - 1) for kk in range(k)], name=name)


def _conv_bwd_x(dy, w, *, name):
    k = w.shape[0]
    return _shift_conv(dy, w, None, [(k - 1) - kk for kk in range(k)], name=name)


def _conv_bwd_w(x, dy, k, *, name):
    s, c = x.shape
    tr = min(CONV_ROWS, s)
    shifts = [kk - (k - 1) for kk in range(k)]
    front, back = _halo(shifts)

    def fold(v):
        return jnp.sum(v.reshape(tr // SUBLANES, SUBLANES, LANES), axis=0)

    def body(x_ref, dy_ref, dw_ref, db_ref, xp_sc, dw_sc, db_sc):
        _fill_halo(xp_sc, x_ref, front, back)
        dw_sc[...] = jnp.zeros_like(dw_sc)
        db_sc[...] = jnp.zeros_like(db_sc)

        def chunk(r, carry):
            base = pl.multiple_of(r * tr, tr)
            dyv = dy_ref[pl.ds(base, tr), :]
            db_sc[...] += fold(dyv)
            for kk in range(k):
                dw_sc[kk] += fold(xp_sc[pl.ds(base + front + shifts[kk], tr), :] * dyv)
            return carry

        lax.fori_loop(0, s // tr, chunk, 0)
        db_ref[...] = jnp.sum(db_sc[...], axis=0, keepdims=True)
        for kk in range(k):
            dw_ref[kk:kk + 1, :] = jnp.sum(dw_sc[kk], axis=0, keepdims=True)

    strip = pl.BlockSpec((s, LANES), lambda cb: (0, cb))
    return pl.pallas_call(
        body, name=name, grid=(c // LANES,), in_specs=[strip, strip],
        out_specs=[pl.BlockSpec((k, LANES), lambda cb: (0, cb)), pl.BlockSpec((1, LANES), lambda cb: (0, cb))],
        out_shape=[jax.ShapeDtypeStruct((k, c), F32), jax.ShapeDtypeStruct((1, c), F32)],
        scratch_shapes=[pltpu.VMEM((front + s + back, LANES), F32), pltpu.VMEM((k, SUBLANES, LANES), F32),
                        pltpu.VMEM((SUBLANES, LANES), F32)],
        compiler_params=_cparams(("parallel",)),
    )(x, dy)


_DIMS = {"nn": ((1,), (0,)), "nt": ((1,), (1,)), "tn": ((0,), (0,))}


def _bd(a, b, mode):
    return lax.dot_general(a.astype(BF16), b.astype(BF16), (_DIMS[mode], ((), ())), preferred_element_type=F32)


@functools.partial(jax.custom_vjp, nondiff_argnums=(2,))
def _bdot(a, b, mode):
    return _bd(a, b, mode)


def _bdot_fwd(a, b, mode):
    return _bd(a, b, mode), (a, b)


def _bdot_bwd(mode, res, g):
    a, b = res
    if mode == "nn":
        return _bd(g, b, "nt"), _bd(a, g, "tn")
    if mode == "nt":
        return _bd(g, b, "nn"), _bd(g, a, "tn")
    return _bd(b, g, "nt"), _bd(a, g, "nn")


_bdot.defvjp(_bdot_fwd, _bdot_bwd)


def _split3(v):
    hi = v.astype(BF16)
    r1 = v - hi.astype(F32)
    mid = r1.astype(BF16)
    lo = (r1 - mid.astype(F32)).astype(BF16)
    return hi, mid, lo


def _mask_dot(mask01, v, mode):
    out = None
    for part in _split3(v):
        if mode == "vn":
            t = lax.dot_general(part, mask01, (_DIMS["nn"], ((), ())), preferred_element_type=F32)
        else:
            t = lax.dot_general(mask01, part, (_DIMS[mode], ((), ())), preferred_element_type=F32)
        out = t if out is None else out + t
    return out


def _lower_tri(n):
    r = lax.broadcasted_iota(jnp.int32, (n, n), 0)
    c = lax.broadcasted_iota(jnp.int32, (n, n), 1)
    return (r >= c).astype(BF16)


@jax.custom_vjp
def _tri_dot(w):
    return _mask_dot(_lower_tri(w.shape[0]), w, "nn")


def _tri_dot_fwd(w):
    return _tri_dot(w), None


def _tri_dot_bwd(_, g):
    return (_mask_dot(_lower_tri(g.shape[0]), g, "tn"),)


_tri_dot.defvjp(_tri_dot_fwd, _tri_dot_bwd)


def _cumsum_lanes(x, *, reverse, name):
    h, s = x.shape
    n = s // LANES

    def body(x_ref, o_ref):
        r = lax.broadcasted_iota(jnp.int32, (LANES, LANES), 0)
        c = lax.broadcasted_iota(jnp.int32, (LANES, LANES), 1)
        m01 = ((r >= c) if reverse else (r <= c)).astype(BF16)

        def step(t, carry):
            ci = (n - 1 - t) if reverse else t
            at = pl.ds(pl.multiple_of(ci * LANES, LANES), LANES)
            blk = x_ref[:, at]
            o_ref[:, at] = _mask_dot(m01, blk, "vn") + carry
            return carry + jnp.sum(blk, axis=1, keepdims=True)

        lax.fori_loop(0, n, step, jnp.zeros((h, 1), F32))

    return pl.pallas_call(body, name=name, out_shape=jax.ShapeDtypeStruct((h, s), F32),
                          compiler_params=_cparams())(x)


def _ssd_chunk(xs, dt, bm, cm, hin, a, dsk, head0):
    n = CHUNK
    row = lax.broadcasted_iota(jnp.int32, (n, n), 0)
    col = lax.broadcasted_iota(jnp.int32, (n, n), 1)
    lower = row >= col
    ustrict = (row > col).astype(F32)
    lane = lax.broadcasted_iota(jnp.int32, (1, LANES), 1)
    sub = lax.broadcasted_iota(jnp.int32, (n, 1), 0)
    e_first = (sub == 0).astype(F32)
    e_last = (sub == n - 1).astype(F32)
    lane0 = (lane == 0).astype(F32)
    cb = _bdot(cm, bm, "nt")
    da = dt * (-jnp.exp(a))
    ys, houts = [], []
    for r in range(SSD_HPG):
        oh = (lane == head0 + r).astype(F32)
        dt_col = jnp.sum(dt * oh, axis=1, keepdims=True)
        da_col = jnp.sum(da * oh, axis=1, keepdims=True)
        dsk_h = jnp.sum(dsk * oh, axis=1, keepdims=True)
        seg = _tri_dot(da_col * ustrict)
        decay = jnp.where(lower, jnp.exp(seg), 0.0)
        cs_col = jnp.sum(seg * lane0, axis=1, keepdims=True) + jnp.sum(da_col * e_first, axis=0, keepdims=True)
        total = jnp.sum(cs_col * e_last, axis=0, keepdims=True)
        xd = xs[r] * dt_col
        y_diag = _bdot(cb * decay, xd, "nn")
        contrib = _bdot(xd * jnp.exp(total - cs_col), bm, "tn")
        houts.append(hin[r] * jnp.exp(total) + contrib)
        y_off = _bdot(cm, hin[r], "nt") * jnp.exp(cs_col)
        ys.append(y_diag + y_off + xs[r] * dsk_h)
    return ys, houts


def _ssd_specs(nc, rev):
    cc = (lambda c: nc - 1 - c) if rev else (lambda c: c)
    hm = pl.BlockSpec((SSD_HPG, CHUNK, SSD_HEADDIM), lambda c, g: (g, cc(c), 0))
    row = pl.BlockSpec((CHUNK, LANES), lambda c, g: (cc(c), 0))
    bmat = pl.BlockSpec((CHUNK, LANES), lambda c, g: (cc(c), D_SSD // LANES + g))
    cmat = pl.BlockSpec((CHUNK, LANES), lambda c, g: (cc(c), D_SSD // LANES + SSD_GROUPS + g))
    par = pl.BlockSpec((1, LANES), lambda c, g: (0, 0))
    hs = pl.BlockSpec((1, SSD_HPG, SSD_HEADDIM, D_STATE), lambda c, g: (cc(c), g, 0, 0))
    return hm, row, bmat, cmat, par, hs


def _ssd_fwd(xs_hm, dt, xbc, a, dsk, *, name):
    s = xs_hm.shape[1]
    nc = s // CHUNK
    hm, row, bmat, cmat, par, hs = _ssd_specs(nc, False)

    def body(xs_ref, dt_ref, bm_ref, cm_ref, a_ref, dsk_ref, y_ref, hs_ref, h_sc):
        c, g = pl.program_id(0), pl.program_id(1)
        mine = pl.ds(g * SSD_HPG, SSD_HPG)

        @pl.when(c == 0)
        def _():
            h_sc[mine] = jnp.zeros((SSD_HPG, SSD_HEADDIM, D_STATE), F32)

        hin = [h_sc[g * SSD_HPG + r] for r in range(SSD_HPG)]
        ys, houts = _ssd_chunk([xs_ref[r] for r in range(SSD_HPG)], dt_ref[...], bm_ref[...], cm_ref[...],
                               hin, a_ref[...], dsk_ref[...], g * SSD_HPG)
        for r in range(SSD_HPG):
            y_ref[r] = ys[r]
            hs_ref[0, r] = hin[r]
            h_sc[g * SSD_HPG + r] = houts[r]

    return pl.pallas_call(
        body, name=name, grid=(nc, SSD_GROUPS),
        in_specs=[hm, row, bmat, cmat, par, par], out_specs=[hm, hs],
        out_shape=[jax.ShapeDtypeStruct(xs_hm.shape, F32),
                   jax.ShapeDtypeStruct((nc, SSD_HEADS, SSD_HEADDIM, D_STATE), F32)],
        scratch_shapes=[pltpu.VMEM((SSD_HEADS, SSD_HEADDIM, D_STATE), F32)],
        compiler_params=_cparams(("arbitrary", "arbitrary")),
    )(xs_hm, dt, xbc, xbc, a, dsk)


def _ssd_bwd(xs_hm, dt, xbc, a, dsk, hsave, dy_hm, *, name):
    s = xs_hm.shape[1]
    nc = s // CHUNK
    hm, row, bmat, cmat, par, hs = _ssd_specs(nc, True)
    gmat = pl.BlockSpec((CHUNK, LANES), lambda c, g: (nc - 1 - c, g))

    def body(xs_ref, dt_ref, bm_ref, cm_ref, a_ref, dsk_ref, hs_ref, dy_ref,
             dxs_ref, ddt_ref, dbm_ref, dcm_ref, da_ref, ddsk_ref, dh_sc):
        c, g = pl.program_id(0), pl.program_id(1)
        mine = pl.ds(g * SSD_HPG, SSD_HPG)

        @pl.when(c == 0)
        def _():
            dh_sc[mine] = jnp.zeros((SSD_HPG, SSD_HEADDIM, D_STATE), F32)

        @pl.when((c == 0) & (g == 0))
        def _():
            da_ref[...] = jnp.zeros_like(da_ref)
            ddsk_ref[...] = jnp.zeros_like(ddsk_ref)

        @pl.when(g == 0)
        def _():
            ddt_ref[...] = jnp.zeros_like(ddt_ref)

        head0 = g * SSD_HPG
        prim = ([xs_ref[r] for r in range(SSD_HPG)], dt_ref[...], bm_ref[...], cm_ref[...],
                [hs_ref[0, r] for r in range(SSD_HPG)], a_ref[...], dsk_ref[...])
        _, vjp = jax.vjp(lambda *p: _ssd_chunk(*p, head0), *prim)
        cot = ([dy_ref[r] for r in range(SSD_HPG)], [dh_sc[g * SSD_HPG + r] for r in range(SSD_HPG)])
        dxs, ddt, dbm, dcm, dhin, da, ddsk = vjp(cot)
        for r in range(SSD_HPG):
            dxs_ref[r] = dxs[r]
            dh_sc[g * SSD_HPG + r] = dhin[r]
        ddt_ref[...] += ddt
        dbm_ref[...] = dbm
        dcm_ref[...] = dcm
        da_ref[...] += da
        ddsk_ref[...] += ddsk

    return pl.pallas_call(
        body, name=name, grid=(nc, SSD_GROUPS),
        in_specs=[hm, row, bmat, cmat, par, par, hs, hm],
        out_specs=[hm, row, gmat, gmat, par, par],
        out_shape=[jax.ShapeDtypeStruct(xs_hm.shape, F32), jax.ShapeDtypeStruct((s, LANES), F32),
                   jax.ShapeDtypeStruct((s, SSD_GROUPS * D_STATE), F32),
                   jax.ShapeDtypeStruct((s, SSD_GROUPS * D_STATE), F32),
                   jax.ShapeDtypeStruct((1, LANES), F32), jax.ShapeDtypeStruct((1, LANES), F32)],
        scratch_shapes=[pltpu.VMEM((SSD_HEADS, SSD_HEADDIM, D_STATE), F32)],
        compiler_params=_cparams(("arbitrary", "arbitrary")),
    )(xs_hm, dt, xbc, xbc, a, dsk, hsave, dy_hm)


SSD_PAIRS = SSD_HPG // 2


def _ssd2_chunk(xs, dt, bm, cm, hin, a, dsk, head0):
    n = CHUNK
    row = lax.broadcasted_iota(jnp.int32, (n, n), 0)
    col = lax.broadcasted_iota(jnp.int32, (n, n), 1)
    lower = row >= col
    ustrict = (row > col).astype(F32)
    lane = lax.broadcasted_iota(jnp.int32, (1, LANES), 1)
    sub = lax.broadcasted_iota(jnp.int32, (n, 1), 0)
    e_first = (sub == 0).astype(F32)
    e_last = (sub == n - 1).astype(F32)
    lane0 = (lane == 0).astype(F32)
    half_l = [(lane < LANES // 2).astype(F32), (lane >= LANES // 2).astype(F32)]
    half_s = [(sub < LANES // 2).astype(F32), (sub >= LANES // 2).astype(F32)]
    cb = _bdot(cm, bm, "nt")
    da = dt * (-jnp.exp(a))
    ys, houts = [], []
    for pr in range(SSD_PAIRS):
        y = jnp.zeros((n, LANES), F32)
        xdte = jnp.zeros((n, LANES), F32)
        lane_gain = jnp.zeros((n, LANES), F32)
        row_gain = jnp.zeros((LANES, 1), F32)
        for hf in range(2):
            oh = (lane == head0 + 2 * pr + hf).astype(F32)
            dt_col = jnp.sum(dt * oh, axis=1, keepdims=True)
            da_col = jnp.sum(da * oh, axis=1, keepdims=True)
            dsk_h = jnp.sum(dsk * oh, axis=1, keepdims=True)
            seg = _tri_dot(da_col * ustrict)
            decay = jnp.where(lower, jnp.exp(seg), 0.0)
            cs_col = jnp.sum(seg * lane0, axis=1, keepdims=True) + jnp.sum(da_col * e_first, axis=0, keepdims=True)
            total = jnp.sum(cs_col * e_last, axis=0, keepdims=True)
            xh = xs[pr] * half_l[hf]
            xd = xh * dt_col
            y = y + _bdot(cb * decay, xd, "nn") + xh * dsk_h
            xdte = xdte + xd * jnp.exp(total - cs_col)
            lane_gain = lane_gain + jnp.exp(cs_col) * half_l[hf]
            row_gain = row_gain + jnp.exp(total) * half_s[hf]
        houts.append(hin[pr] * row_gain + _bdot(xdte, bm, "tn"))
        ys.append(y + _bdot(cm, hin[pr], "nt") * lane_gain)
    return ys, houts


def _ssd2_specs(nc, rev):
    cc = (lambda c: nc - 1 - c) if rev else (lambda c: c)
    wide = SSD_PAIRS * LANES
    act = pl.BlockSpec((CHUNK, wide), lambda c, g: (cc(c), g))
    row = pl.BlockSpec((CHUNK, LANES), lambda c, g: (cc(c), 0))
    bmat = pl.BlockSpec((CHUNK, LANES), lambda c, g: (cc(c), D_SSD // LANES + g))
    cmat = pl.BlockSpec((CHUNK, LANES), lambda c, g: (cc(c), D_SSD // LANES + SSD_GROUPS + g))
    par = pl.BlockSpec((1, LANES), lambda c, g: (0, 0))
    hs = pl.BlockSpec((1, SSD_PAIRS, LANES, D_STATE), lambda c, g: (cc(c), g, 0, 0))
    return act, row, bmat, cmat, par, hs


def _pair_cols(ref):
    return [ref[:, pr * LANES:(pr + 1) * LANES] for pr in range(SSD_PAIRS)]


def _ssd2_fwd(dt, xbc, a, dsk, *, name):
    s = xbc.shape[0]
    nc = s // CHUNK
    act, row, bmat, cmat, par, hs = _ssd2_specs(nc, False)

    def body(xs_ref, dt_ref, bm_ref, cm_ref, a_ref, dsk_ref, y_ref, hs_ref, h_sc):
        c, g = pl.program_id(0), pl.program_id(1)

        @pl.when(c == 0)
        def _():
            h_sc[pl.ds(g * SSD_PAIRS, SSD_PAIRS)] = jnp.zeros((SSD_PAIRS, LANES, D_STATE), F32)

        hin = [h_sc[g * SSD_PAIRS + pr] for pr in range(SSD_PAIRS)]
        ys, houts = _ssd2_chunk(_pair_cols(xs_ref), dt_ref[...], bm_ref[...], cm_ref[...], hin,
                                a_ref[...], dsk_ref[...], g * SSD_HPG)
        for pr in range(SSD_PAIRS):
            y_ref[:, pr * LANES:(pr + 1) * LANES] = ys[pr]
            hs_ref[0, pr] = hin[pr]
            h_sc[g * SSD_PAIRS + pr] = houts[pr]

    return pl.pallas_call(
        body, name=name, grid=(nc, SSD_GROUPS),
        in_specs=[act, row, bmat, cmat, par, par], out_specs=[act, hs],
        out_shape=[jax.ShapeDtypeStruct((s, D_SSD), F32),
                   jax.ShapeDtypeStruct((nc, SSD_HEADS // 2, LANES, D_STATE), F32)],
        scratch_shapes=[pltpu.VMEM((SSD_HEADS // 2, LANES, D_STATE), F32)],
        compiler_params=_cparams(("arbitrary", "arbitrary")),
    )(xbc, dt, xbc, xbc, a, dsk)


def _ssd2_bwd(dt, xbc, a, dsk, hsave, dy, *, name):
    s = xbc.shape[0]
    nc = s // CHUNK
    act, row, bmat, cmat, par, hs = _ssd2_specs(nc, True)
    gmat = pl.BlockSpec((CHUNK, LANES), lambda c, g: (nc - 1 - c, g))

    def body(xs_ref, dt_ref, bm_ref, cm_ref, a_ref, dsk_ref, hs_ref, dy_ref,
             dxs_ref, ddt_ref, dbm_ref, dcm_ref, da_ref, ddsk_ref, dh_sc):
        c, g = pl.program_id(0), pl.program_id(1)

        @pl.when(c == 0)
        def _():
            dh_sc[pl.ds(g * SSD_PAIRS, SSD_PAIRS)] = jnp.zeros((SSD_PAIRS, LANES, D_STATE), F32)

        @pl.when((c == 0) & (g == 0))
        def _():
            da_ref[...] = jnp.zeros_like(da_ref)
            ddsk_ref[...] = jnp.zeros_like(ddsk_ref)

        @pl.when(g == 0)
        def _():
            ddt_ref[...] = jnp.zeros_like(ddt_ref)

        head0 = g * SSD_HPG
        prim = (_pair_cols(xs_ref), dt_ref[...], bm_ref[...], cm_ref[...],
                [hs_ref[0, pr] for pr in range(SSD_PAIRS)], a_ref[...], dsk_ref[...])
        _, vjp = jax.vjp(lambda *p: _ssd2_chunk(*p, head0), *prim)
        cot = (_pair_cols(dy_ref), [dh_sc[g * SSD_PAIRS + pr] for pr in range(SSD_PAIRS)])
        dxs, ddt, dbm, dcm, dhin, da, ddsk = vjp(cot)
        for pr in range(SSD_PAIRS):
            dxs_ref[:, pr * LANES:(pr + 1) * LANES] = dxs[pr]
            dh_sc[g * SSD_PAIRS + pr] = dhin[pr]
        ddt_ref[...] += ddt
        dbm_ref[...] = dbm
        dcm_ref[...] = dcm
        da_ref[...] += da
        ddsk_ref[...] += ddsk

    return pl.pallas_call(
        body, name=name, grid=(nc, SSD_GROUPS),
        in_specs=[act, row, bmat, cmat, par, par, hs, act],
        out_specs=[act, row, gmat, gmat, par, par],
        out_shape=[jax.ShapeDtypeStruct((s, D_SSD), F32), jax.ShapeDtypeStruct((s, LANES), F32),
                   jax.ShapeDtypeStruct((s, SSD_GROUPS * D_STATE), F32),
                   jax.ShapeDtypeStruct((s, SSD_GROUPS * D_STATE), F32),
                   jax.ShapeDtypeStruct((1, LANES), F32), jax.ShapeDtypeStruct((1, LANES), F32)],
        scratch_shapes=[pltpu.VMEM((SSD_HEADS // 2, LANES, D_STATE), F32)],
        compiler_params=_cparams(("arbitrary", "arbitrary")),
    )(xbc, dt, xbc, xbc, a, dsk, hsave, dy)


FOX_BLOCK = 512
NEG = -1e30


def _fox_scores(q, k, cref, ck, strictly_below):
    t = q.shape[0]
    s = lax.dot_general(q, k, (_DIMS["nt"], ((), ())), preferred_element_type=F32) * (FOX_HEADDIM ** -0.5)
    s = s + (cref - ck)
    row = lax.broadcasted_iota(jnp.int32, (t, t), 0)
    col = lax.broadcasted_iota(jnp.int32, (t, t), 1)
    mask = (row >= col) | strictly_below
    return s, mask


def _fox_fwd(q, k, v, c, *, name):
    h, s, p = q.shape
    t = min(FOX_BLOCK, s)
    nb = s // t

    def body(q_ref, k_ref, v_ref, cq_ref, ck_ref, o_ref, lse_ref, m_sc, l_sc, acc_sc):
        i, j = pl.program_id(1), pl.program_id(2)

        @pl.when(j == 0)
        def _():
            m_sc[...] = jnp.full_like(m_sc, NEG)
            l_sc[...] = jnp.zeros_like(l_sc)
            acc_sc[...] = jnp.zeros_like(acc_sc)

        @pl.when(j <= i)
        def _():
            sc, mask = _fox_scores(q_ref[0], k_ref[0], cq_ref[0, 0:1, 0:1], ck_ref[0], j < i)
            sc = jnp.where(mask, sc, NEG)
            m_old = m_sc[...]
            m_new = jnp.maximum(m_old, jnp.max(sc, axis=1, keepdims=True))
            alpha = jnp.exp(m_old - m_new)
            pr = jnp.exp(sc - m_new)
            l_sc[...] = alpha * l_sc[...] + jnp.sum(pr, axis=1, keepdims=True)
            pr_hi = pr.astype(BF16)
            pr_lo = (pr - pr_hi.astype(F32)).astype(BF16)
            pv = (lax.dot_general(pr_hi, v_ref[0], (_DIMS["nn"], ((), ())), preferred_element_type=F32)
                  + lax.dot_general(pr_lo, v_ref[0], (_DIMS["nn"], ((), ())), preferred_element_type=F32))
            acc_sc[...] = alpha * acc_sc[...] + pv
            m_sc[...] = m_new

        @pl.when(j == i)
        def _():
            o_ref[0] = acc_sc[...] / l_sc[...]
            lse_ref[0] = jnp.broadcast_to(m_sc[...] + jnp.log(l_sc[...]), (t, LANES))

    qspec = pl.BlockSpec((1, t, p), lambda hh, i, j: (hh, i, 0))
    kspec = pl.BlockSpec((1, t, p), lambda hh, i, j: (hh, jnp.minimum(j, i), 0))
    return pl.pallas_call(
        body, name=name, grid=(h, nb, nb),
        in_specs=[qspec, kspec, kspec,
                  pl.BlockSpec((1, 1, t), lambda hh, i, j: (hh, 0, i)),
                  pl.BlockSpec((1, 1, t), lambda hh, i, j: (hh, 0, jnp.minimum(j, i)))],
        out_specs=[qspec, pl.BlockSpec((1, t, LANES), lambda hh, i, j: (hh, i, 0))],
        out_shape=[jax.ShapeDtypeStruct((h, s, p), F32), jax.ShapeDtypeStruct((h, s, LANES), F32)],
        scratch_shapes=[pltpu.VMEM((t, 1), F32), pltpu.VMEM((t, 1), F32), pltpu.VMEM((t, p), F32)],
        compiler_params=_cparams(("parallel", "arbitrary", "arbitrary")),
    )(q, k, v, c, c)


def _fox_bwd(q, k, v, c, o, lse, do, *, name):
    h, s, p = q.shape
    t = min(FOX_BLOCK, s)
    nb = s // t
    scale = FOX_HEADDIM ** -0.5

    def body(q_ref, k_ref, v_ref, cq_ref, ck_ref, o_ref, lse_ref, do_ref,
             dq_ref, dk_ref, dv_ref, dc_ref, dk_sc, dv_sc, dc_sc):
        j, i = pl.program_id(1), pl.program_id(2)

        @pl.when(i == 0)
        def _():
            dk_sc[...] = jnp.zeros_like(dk_sc)
            dv_sc[...] = jnp.zeros_like(dv_sc)
            dc_sc[...] = jnp.zeros_like(dc_sc)

        @pl.when(i >= j)
        def _():
            qv, kv, vv = q_ref[0], k_ref[0], v_ref[0]
            sc, mask = _fox_scores(qv, kv, cq_ref[0, 0:1, 0:1], ck_ref[0], i > j)
            pr = jnp.where(mask, jnp.exp(sc - lse_ref[0, :, 0:1]), 0.0)
            dov = do_ref[0]
            dob = dov.astype(BF16)
            prb = pr.astype(BF16)
            dv_sc[...] += lax.dot_general(prb, dob, (_DIMS["tn"], ((), ())), preferred_element_type=F32)
            dp = lax.dot_general(dob, vv, (_DIMS["nt"], ((), ())), preferred_element_type=F32)
            dcol = jnp.sum(dob.astype(F32) * o_ref[0], axis=1, keepdims=True)
            ds = pr * (dp - dcol)
            dc_sc[...] -= jnp.sum(ds, axis=0, keepdims=True)
            dsb = ds.astype(BF16)
            dqc = scale * lax.dot_general(dsb, kv, (_DIMS["nn"], ((), ())), preferred_element_type=F32)
            at = pl.ds(pl.multiple_of(i * t, t), t)

            @pl.when(j == 0)
            def _():
                dq_ref[0, at, :] = dqc

            @pl.when(j > 0)
            def _():
                dq_ref[0, at, :] += dqc

            dk_sc[...] += scale * lax.dot_general(dsb, qv, (_DIMS["tn"], ((), ())), preferred_element_type=F32)

        @pl.when(i == nb - 1)
        def _():
            dk_ref[0] = dk_sc[...]
            dv_ref[0] = dv_sc[...]
            dc_ref[0] = dc_sc[...]

    qspec = pl.BlockSpec((1, t, p), lambda hh, j, i: (hh, jnp.maximum(i, j), 0))
    kspec = pl.BlockSpec((1, t, p), lambda hh, j, i: (hh, j, 0))
    cq = pl.BlockSpec((1, 1, t), lambda hh, j, i: (hh, 0, jnp.maximum(i, j)))
    ck = pl.BlockSpec((1, 1, t), lambda hh, j, i: (hh, 0, j))
    return pl.pallas_call(
        body, name=name, grid=(h, nb, nb),
        in_specs=[qspec, kspec, kspec, cq, ck, qspec,
                  pl.BlockSpec((1, t, LANES), lambda hh, j, i: (hh, jnp.maximum(i, j), 0)), qspec],
        out_specs=[pl.BlockSpec((1, s, p), lambda hh, j, i: (hh, 0, 0)), kspec, kspec, ck],
        out_shape=[jax.ShapeDtypeStruct((h, s, p), F32), jax.ShapeDtypeStruct((h, s, p), F32),
                   jax.ShapeDtypeStruct((h, s, p), F32), jax.ShapeDtypeStruct((h, 1, s), F32)],
        scratch_shapes=[pltpu.VMEM((t, p), F32), pltpu.VMEM((t, p), F32), pltpu.VMEM((1, t), F32)],
        compiler_params=_cparams(("parallel", "arbitrary", "arbitrary")),
    )(q, k, v, c, c, o, lse, do)


AUX = 64


def _pack(main, cols):
    h, s, p = main.shape
    parts = [main.astype(BF16)]
    if cols:
        parts.append(jnp.stack(cols, axis=-1).astype(BF16))
    parts.append(jnp.zeros((h, s, LANES - p - len(cols)), BF16))
    return jnp.concatenate(parts, axis=-1)


def _terms(v):
    hi = lax.reduce_precision(v, 8, 7)
    mid = lax.reduce_precision(v - hi, 8, 7)
    lo = lax.reduce_precision(v - hi - mid, 8, 7)
    return [hi, mid, lo]


def _fox_pack_qkv(q, k, v):
    h, s, _ = q.shape
    one = jnp.ones((h, s), F32)
    return _pack(q * (FOX_HEADDIM ** -0.5), []), _pack(k, []), _pack(v, [one, one, one])


def _fox_bias(c_ref, qblock, kblock, t):
    lane = lax.broadcasted_iota(jnp.int32, (1, LANES), 1)
    cq = c_ref[0, :, pl.ds(pl.multiple_of(qblock * t, LANES), LANES)]
    cref = jnp.sum(jnp.where(lane == 0, cq, 0.0), axis=1, keepdims=True)
    return cref - c_ref[0, :, pl.ds(pl.multiple_of(kblock * t, LANES), t)]


def _fox_rowdot(do, o, *, tm=256, name):
    s, d = do.shape
    tm = min(tm, s)

    def body(do_ref, o_ref, d_ref):
        prod = do_ref[...].astype(BF16).astype(F32) * o_ref[...]
        r = lax.broadcasted_iota(jnp.int32, (d, LANES), 0)
        c = lax.broadcasted_iota(jnp.int32, (d, LANES), 1)
        mine = (r >= c * FOX_HEADDIM) & (r < (c + 1) * FOX_HEADDIM)
        d_ref[...] = _mask_dot(mine.astype(BF16), prod, "vn")

    row = pl.BlockSpec((tm, d), lambda i: (i, 0))
    return pl.pallas_call(body, name=name, grid=(s // tm,), in_specs=[row, row],
                          out_specs=pl.BlockSpec((tm, LANES), lambda i: (i, 0)),
                          out_shape=jax.ShapeDtypeStruct((s, LANES), F32),
                          compiler_params=_cparams(("parallel",)))(do, o)


def _causal(t):
    return lax.broadcasted_iota(jnp.int32, (t, t), 0) >= lax.broadcasted_iota(jnp.int32, (t, t), 1)


def _fox2_fwd(qp, kp, vp, c, *, name):
    h, s, _ = qp.shape
    t = min(FOX_BLOCK, s)
    nb = s // t
    nt = (((1,), (1,)), ((), ()))
    nn = (((1,), (0,)), ((), ()))

    def body(q_ref, k_ref, v_ref, c_ref, o_ref, lse_ref, m_sc, acc_sc):
        i = pl.program_id(1)
        m_sc[...] = jnp.full_like(m_sc, NEG)
        acc_sc[...] = jnp.zeros_like(acc_sc)
        qv = q_ref[0]

        def step(j, masked):
            at = pl.ds(pl.multiple_of(j * t, t), t)
            kv, vv = k_ref[0, at, :], v_ref[0, at, :]
            sc = lax.dot_general(qv, kv, nt, preferred_element_type=F32) + _fox_bias(c_ref, i, j, t)
            if masked:
                sc = jnp.where(_causal(t), sc, NEG)
            m_prev = m_sc[...]
            m_new = jnp.maximum(m_prev, jnp.max(sc, axis=1, keepdims=True))
            pr = jnp.exp(sc - jnp.tile(m_new, (1, t // LANES)))
            pr_hi = pr.astype(BF16)
            pr_lo = (pr - pr_hi.astype(F32)).astype(BF16)
            pv = (lax.dot_general(pr_hi, vv, nn, preferred_element_type=F32)
                  + lax.dot_general(pr_lo, vv, nn, preferred_element_type=F32))
            acc_sc[...] = jnp.exp(m_prev - m_new) * acc_sc[...] + pv
            m_sc[...] = m_new

        lax.fori_loop(0, i, lambda j, carry: (step(j, False), carry)[1], 0)
        step(i, True)
        acc = acc_sc[...]
        lane = lax.broadcasted_iota(jnp.int32, (1, LANES), 1)
        den = jnp.sum(jnp.where(lane == AUX, acc, 0.0), axis=1, keepdims=True)
        o_ref[0] = (acc / den)[:, :FOX_HEADDIM]
        lse_ref[0] = m_sc[...] + jnp.log(den)

    whole = pl.BlockSpec((1, s, LANES), lambda hh, i: (hh, 0, 0))
    return pl.pallas_call(
        body, name=name, grid=(h, nb),
        in_specs=[pl.BlockSpec((1, t, LANES), lambda hh, i: (hh, i, 0)), whole, whole,
                  pl.BlockSpec((1, 1, s), lambda hh, i: (hh, 0, 0))],
        out_specs=[pl.BlockSpec((1, t, FOX_HEADDIM), lambda hh, i: (hh, i, 0)),
                   pl.BlockSpec((1, t, LANES), lambda hh, i: (hh, i, 0))],
        out_shape=[jax.ShapeDtypeStruct((h, s, FOX_HEADDIM), F32), jax.ShapeDtypeStruct((h, s, LANES), F32)],
        scratch_shapes=[pltpu.VMEM((t, LANES), F32), pltpu.VMEM((t, LANES), F32)],
        compiler_params=_cparams(("parallel", "arbitrary")),
    )(qp, kp, vp, c)


def _fox2_bwd(qp, kp, vp, c, dop, lse, *, name):
    h, s, _ = qp.shape
    t = min(FOX_BLOCK, s)
    nb = s // t
    nt = (((1,), (1,)), ((), ()))
    nn = (((1,), (0,)), ((), ()))
    tn = (((0,), (0,)), ((), ()))

    def body(k_ref, v_ref, q_ref, c_ref, do_ref, lse_ref, dq_ref, dk_ref, dv_ref, dc_ref, dk_sc, dv_sc, dc_sc):
        j = pl.program_id(1)

        @pl.when(j == 0)
        def _():
            dq_ref[...] = jnp.zeros_like(dq_ref)

        dk_sc[...] = jnp.zeros_like(dk_sc)
        dv_sc[...] = jnp.zeros_like(dv_sc)
        dc_sc[...] = jnp.zeros_like(dc_sc)
        kv, vv = k_ref[0], v_ref[0]

        def step(i, masked):
            at = pl.ds(pl.multiple_of(i * t, t), t)
            qv, dov = q_ref[0, at, :], do_ref[0, at, :]
            sc = lax.dot_general(qv, kv, nt, preferred_element_type=F32) + _fox_bias(c_ref, i, j, t)
            pr = jnp.exp(sc - jnp.tile(lse_ref[0, at, :], (1, t // LANES)))
            if masked:
                pr = jnp.where(_causal(t), pr, 0.0)
            ds = pr * lax.dot_general(dov, vv, nt, preferred_element_type=F32)
            dc_sc[...] -= jnp.sum(ds, axis=0, keepdims=True)
            dsb = ds.astype(BF16)
            dv_sc[...] += lax.dot_general(pr.astype(BF16), dov, tn, preferred_element_type=F32)
            dk_sc[...] += lax.dot_general(dsb, qv, tn, preferred_element_type=F32)
            dq_ref[0, at, :] += lax.dot_general(dsb, kv, nn, preferred_element_type=F32)

        step(j, True)
        lax.fori_loop(j + 1, nb, lambda i, carry: (step(i, False), carry)[1], 0)
        dk_ref[0] = dk_sc[...]
        dv_ref[0] = dv_sc[...].astype(dv_ref.dtype)
        dc_ref[0] = dc_sc[...]

    whole = pl.BlockSpec((1, s, LANES), lambda hh, j: (hh, 0, 0))
    blk = pl.BlockSpec((1, t, LANES), lambda hh, j: (hh, j, 0))
    return pl.pallas_call(
        body, name=name, grid=(h, nb),
        in_specs=[blk, blk, whole, pl.BlockSpec((1, 1, s), lambda hh, j: (hh, 0, 0)), whole, whole],
        out_specs=[whole, blk, blk, pl.BlockSpec((1, 1, t), lambda hh, j: (hh, 0, j))],
        out_shape=[jax.ShapeDtypeStruct((h, s, LANES), F32), jax.ShapeDtypeStruct((h, s, LANES), F32),
                   jax.ShapeDtypeStruct((h, s, LANES), BF16), jax.ShapeDtypeStruct((h, 1, s), F32)],
        scratch_shapes=[pltpu.VMEM((t, LANES), F32), pltpu.VMEM((t, LANES), F32), pltpu.VMEM((1, t), F32)],
        compiler_params=_cparams(("parallel", "arbitrary")),
    )(kp, vp, qp, c, dop, lse)


PAIRS = FOX_HEADS // 2
HALF = LANES // 2


def _first_half():
    return lax.broadcasted_iota(jnp.int32, (1, LANES), 1) < HALF


def _pair_bias(c_ref, hh, qblock, kblock, t):
    lane = lax.broadcasted_iota(jnp.int32, (1, LANES), 1)
    cq = c_ref[hh, :, pl.ds(pl.multiple_of(qblock * t, LANES), LANES)]
    cref = jnp.sum(jnp.where(lane == 0, cq, 0.0), axis=1, keepdims=True)
    return cref - c_ref[hh, :, pl.ds(pl.multiple_of(kblock * t, LANES), t)]


def _fox_dopack(do, o, *, tm=256, name):
    s, d = do.shape
    tm = min(tm, s)

    def body(do_ref, o_ref, out_ref):
        dov = do_ref[...].astype(BF16)
        prod = dov.astype(F32) * o_ref[...]
        r = lax.broadcasted_iota(jnp.int32, (d, LANES), 0)
        c = lax.broadcasted_iota(jnp.int32, (d, LANES), 1)
        heads = ((r >= c * FOX_HEADDIM) & (r < (c + 1) * FOX_HEADDIM)).astype(BF16)
        negd = -_mask_dot(heads, prod, "vn")
        hr = lax.broadcasted_iota(jnp.int32, (LANES, 2 * d), 0)
        col = lax.broadcasted_iota(jnp.int32, (LANES, 2 * d), 1)
        base = (hr >> 1) * (2 * LANES) + jnp.where((hr & 1) == 0, HALF, LANES)
        terms = None
        for kk, part in enumerate(_split3(negd)):
            place = ((col == base + kk) & (hr < FOX_HEADS)).astype(BF16)
            tk = lax.dot_general(part, place, (_DIMS["nn"], ((), ())), preferred_element_type=F32)
            terms = tk if terms is None else terms + tk
        first = _first_half()
        zero = jnp.zeros((tm, LANES), BF16)
        pieces = []
        for hp in range(PAIRS):
            blk = dov[:, hp * LANES:(hp + 1) * LANES]
            pieces += [jnp.where(first, blk, zero), jnp.where(first, zero, blk)]
        out_ref[...] = (jnp.concatenate(pieces, axis=1).astype(F32) + terms).astype(BF16)

    row = pl.BlockSpec((tm, d), lambda i: (i, 0))
    return pl.pallas_call(body, name=name, grid=(s // tm,), in_specs=[row, row],
                          out_specs=pl.BlockSpec((tm, 2 * d), lambda i: (i, 0)),
                          out_shape=jax.ShapeDtypeStruct((s, 2 * d), BF16),
                          compiler_params=_cparams(("parallel",)))(do, o)


def _fox3_fwd(qkv, c, *, name):
    s = qkv.shape[0]
    t = min(FOX_BLOCK, s)
    nb = s // t
    nt = (((1,), (1,)), ((), ()))
    nn = (((1,), (0,)), ((), ()))
    scale = FOX_HEADDIM ** -0.5

    def body(q_ref, k_ref, v_ref, c_ref, o_ref, lse_ref, m_sc, acc_sc):
        i = pl.program_id(1)
        first = _first_half()
        m_sc[...] = jnp.full_like(m_sc, NEG)
        acc_sc[...] = jnp.zeros_like(acc_sc)
        q2 = q_ref[...] * scale
        zero = jnp.zeros_like(q2)
        qs = [jnp.where(first, q2, zero), jnp.where(first, zero, q2)]

        def step(j, masked):
            at = pl.ds(pl.multiple_of(j * t, t), t)
            k2, v2 = k_ref[at, :], v_ref[at, :]
            one = jnp.ones_like(v2)
            vx = [jnp.where(first, v2, one), jnp.where(first, one, v2)]
            for hh in range(2):
                sc = lax.dot_general(qs[hh], k2, nt, preferred_element_type=F32) + _pair_bias(c_ref, hh, i, j, t)
                if masked:
                    sc = jnp.where(_causal(t), sc, NEG)
                m_prev = m_sc[hh]
                m_new = jnp.maximum(m_prev, jnp.max(sc, axis=1, keepdims=True))
                pr = jnp.exp(sc - jnp.tile(m_new, (1, t // LANES)))
                pr_hi = pr.astype(BF16)
                pr_lo = (pr - pr_hi.astype(F32)).astype(BF16)
                pv = (lax.dot_general(pr_hi, vx[hh], nn, preferred_element_type=F32)
                      + lax.dot_general(pr_lo, vx[hh], nn, preferred_element_type=F32))
                acc_sc[hh] = jnp.exp(m_prev - m_new) * acc_sc[hh] + pv
                m_sc[hh] = m_new

        lax.fori_loop(0, i, lambda j, carry: (step(j, False), carry)[1], 0)
        step(i, True)
        acc_a, acc_b = acc_sc[0], acc_sc[1]
        den_a = jnp.where(first, pltpu.roll(acc_a, HALF, 1), acc_a)
        den_b = jnp.where(first, acc_b, pltpu.roll(acc_b, HALF, 1))
        o_ref[...] = jnp.where(first, acc_a / den_a, acc_b / den_b)
        lse_ref[:, :LANES] = m_sc[0] + jnp.log(den_a)
        lse_ref[:, LANES:] = m_sc[1] + jnp.log(den_b)

    nq = D_FOX // LANES
    return pl.pallas_call(
        body, name=name, grid=(PAIRS, nb),
        in_specs=[pl.BlockSpec((t, LANES), lambda hp, i: (i, hp)),
                  pl.BlockSpec((s, LANES), lambda hp, i: (0, nq + hp)),
                  pl.BlockSpec((s, LANES), lambda hp, i: (0, 2 * nq + hp)),
                  pl.BlockSpec((2, 1, s), lambda hp, i: (hp, 0, 0))],
        out_specs=[pl.BlockSpec((t, LANES), lambda hp, i: (i, hp)),
                   pl.BlockSpec((t, 2 * LANES), lambda hp, i: (i, hp))],
        out_shape=[jax.ShapeDtypeStruct((s, D_FOX), F32), jax.ShapeDtypeStruct((s, 2 * D_FOX), F32)],
        scratch_shapes=[pltpu.VMEM((2, t, LANES), F32), pltpu.VMEM((2, t, LANES), F32)],
        compiler_params=_cparams(("parallel", "arbitrary")),
    )(qkv, qkv, qkv, c)


def _fox3_bwd(qkv, c, dox, lse, *, name):
    s = qkv.shape[0]
    t = min(FOX_BLOCK, s)
    nb = s // t
    nt = (((1,), (1,)), ((), ()))
    nn = (((1,), (0,)), ((), ()))
    tn = (((0,), (0,)), ((), ()))
    scale = FOX_HEADDIM ** -0.5

    def body(k_ref, v_ref, q_ref, c_ref, do_ref, lse_ref, dq_ref, dk_ref, dv_ref, dc_ref, dk_sc, dv_sc, dc_sc):
        j = pl.program_id(1)
        first = _first_half()
        halves = [first, jnp.logical_not(first)]

        @pl.when(j == 0)
        def _():
            dq_ref[...] = jnp.zeros_like(dq_ref)

        dk_sc[...] = jnp.zeros_like(dk_sc)
        dv_sc[...] = jnp.zeros_like(dv_sc)
        dc_sc[...] = jnp.zeros_like(dc_sc)
        k2, v2 = k_ref[...], v_ref[...]
        one = jnp.ones_like(v2)
        vx = [jnp.where(first, v2, one), jnp.where(first, one, v2)]

        def step(i, masked):
            at = pl.ds(pl.multiple_of(i * t, t), t)
            q2 = q_ref[at, :] * scale
            zero = jnp.zeros_like(q2)
            qs = [jnp.where(first, q2, zero), jnp.where(first, zero, q2)]
            for hh in range(2):
                dov = do_ref[at, hh * LANES:(hh + 1) * LANES]
                sc = lax.dot_general(qs[hh], k2, nt, preferred_element_type=F32) + _pair_bias(c_ref, hh, i, j, t)
                pr = jnp.exp(sc - jnp.tile(lse_ref[at, hh * LANES:(hh + 1) * LANES], (1, t // LANES)))
                if masked:
                    pr = jnp.where(_causal(t), pr, 0.0)
                ds = pr * lax.dot_general(dov, vx[hh], nt, preferred_element_type=F32)
                dc_sc[hh] -= jnp.sum(ds, axis=0, keepdims=True)
                dsb = ds.astype(BF16)
                dvh = lax.dot_general(pr.astype(BF16), dov, tn, preferred_element_type=F32)
                dv_sc[...] += jnp.where(halves[hh], dvh, 0.0)
                dk_sc[...] += lax.dot_general(dsb, qs[hh], tn, preferred_element_type=F32)
                dqh = lax.dot_general(dsb, k2, nn, preferred_element_type=F32)
                dq_ref[at, :] += jnp.where(halves[hh], dqh, 0.0)

        step(j, True)
        lax.fori_loop(j + 1, nb, lambda i, carry: (step(i, False), carry)[1], 0)
        dk_ref[...] = dk_sc[...]
        dv_ref[...] = dv_sc[...].astype(dv_ref.dtype)
        dc_ref[...] = dc_sc[...]

    nq = D_FOX // LANES
    blk = pl.BlockSpec((t, LANES), lambda hp, j: (j, hp))
    return pl.pallas_call(
        body, name=name, grid=(PAIRS, nb),
        in_specs=[pl.BlockSpec((t, LANES), lambda hp, j: (j, nq + hp)),
                  pl.BlockSpec((t, LANES), lambda hp, j: (j, 2 * nq + hp)),
                  pl.BlockSpec((s, LANES), lambda hp, j: (0, hp)),
                  pl.BlockSpec((2, 1, s), lambda hp, j: (hp, 0, 0)),
                  pl.BlockSpec((s, 2 * LANES), lambda hp, j: (0, hp)),
                  pl.BlockSpec((s, 2 * LANES), lambda hp, j: (0, hp))],
        out_specs=[pl.BlockSpec((s, LANES), lambda hp, j: (0, hp)), blk, blk,
                   pl.BlockSpec((2, 1, t), lambda hp, j: (hp, 0, j))],
        out_shape=[jax.ShapeDtypeStruct((s, D_FOX), F32), jax.ShapeDtypeStruct((s, D_FOX), F32),
                   jax.ShapeDtypeStruct((s, D_FOX), BF16), jax.ShapeDtypeStruct((FOX_HEADS, 1, s), F32)],
        scratch_shapes=[pltpu.VMEM((t, LANES), F32), pltpu.VMEM((t, LANES), F32), pltpu.VMEM((2, 1, t), F32)],
        compiler_params=_cparams(("parallel", "arbitrary")),
    )(qkv, qkv, qkv, c, dox, lse)


def _final(x1, out1, g, tgt, *, tm=256, name):
    s, d = x1.shape
    tm = min(tm, s)

    def body(x_ref, o_ref, g_ref, t_ref, dx_ref, do_ref, dg_ref, loss_ref):
        i = pl.program_id(0)

        @pl.when(i == 0)
        def _():
            dg_ref[...] = jnp.zeros_like(dg_ref)
            loss_ref[...] = jnp.zeros_like(loss_ref)

        tv = t_ref[...]

        def lossf(xv, ov, gv):
            err = jnp.square(xv + _rms(ov, gv) - tv)
            return 0.5 * jnp.sum(jnp.mean(err, axis=-1, keepdims=True), axis=0, keepdims=True)

        val, vjp = jax.vjp(lossf, x_ref[...], o_ref[...], g_ref[...])
        dx, do, dg = vjp(jnp.ones((1, 1), F32))
        dx_ref[...] = dx
        do_ref[...] = do.astype(do_ref.dtype)
        dg_ref[...] += dg
        loss_ref[...] += val

    row = pl.BlockSpec((tm, d), lambda i: (i, 0))
    par = pl.BlockSpec((1, d), lambda i: (0, 0))
    return pl.pallas_call(
        body, name=name, grid=(s // tm,), in_specs=[row, row, par, row],
        out_specs=[row, row, par, pl.BlockSpec((1, 1), lambda i: (0, 0))],
        out_shape=[jax.ShapeDtypeStruct((s, d), F32), jax.ShapeDtypeStruct((s, d), BF16),
                   jax.ShapeDtypeStruct((1, d), F32), jax.ShapeDtypeStruct((1, 1), F32)],
        compiler_params=_cparams(("arbitrary",)),
    )(x1, out1, g, tgt)


def _row_tile(r):
    return LANES if r % LANES == 0 else r


def _sum_slots(parts, *, out_dtype=F32, name):
    p, r, c = parts.shape
    tr = _row_tile(r)

    def body(p_ref, o_ref):
        acc = p_ref[0].astype(F32)
        for k in range(1, p):
            acc = acc + p_ref[k].astype(F32)
        o_ref[...] = acc.astype(o_ref.dtype)

    return pl.pallas_call(
        body, name=name, grid=(r // tr,),
        in_specs=[pl.BlockSpec((p, tr, c), lambda i: (0, i, 0))],
        out_specs=pl.BlockSpec((tr, c), lambda i: (i, 0)),
        out_shape=jax.ShapeDtypeStruct((r, c), out_dtype),
        compiler_params=_cparams(("parallel",)),
    )(parts)


def _adamw(w, gparts, m, v, *, name):
    r, c = w.shape
    p = gparts.shape[0]
    tr = _row_tile(r)

    def body(w_ref, g_ref, m_ref, v_ref, go_ref, d_ref, mo_ref, vo_ref):
        g = g_ref[0].astype(F32)
        for k in range(1, p):
            g = g + g_ref[k].astype(F32)
        mn = ADAM_B1 * m_ref[...] + (1.0 - ADAM_B1) * g
        vn = ADAM_B2 * v_ref[...] + (1.0 - ADAM_B2) * jnp.square(g)
        m_hat = mn / (1.0 - ADAM_B1 ** ADAM_STEP)
        v_hat = vn / (1.0 - ADAM_B2 ** ADAM_STEP)
        go_ref[...] = g
        d_ref[...] = -ADAM_LR * (m_hat / (jnp.sqrt(v_hat) + ADAM_EPS) + ADAM_WD * w_ref[...])
        mo_ref[...] = mn
        vo_ref[...] = vn

    spec = pl.BlockSpec((tr, c), lambda i: (i, 0))
    return pl.pallas_call(
        body, name=name, grid=(r // tr,),
        in_specs=[spec, pl.BlockSpec((p, tr, c), lambda i: (0, i, 0)), spec, spec],
        out_specs=[spec] * 4, out_shape=[jax.ShapeDtypeStruct((r, c), F32)] * 4,
        compiler_params=_cparams(("parallel",)),
    )(w, gparts, m, v)


_FLIPS = {
    "xy": [(1, 0, 0), (0, 1, 0), (1, 1, 0)],
    "c": [(0, 0, 1)],
    "xyc": [(fx, fy, fc) for fx in (0, 1) for fy in (0, 1) for fc in (0, 1) if (fx, fy, fc) != (0, 0, 0)],
}


def _slot(mode, px, py, pc):
    return {"xy": 2 * px + py, "c": pc, "xyc": 4 * px + 2 * py + pc}[mode]


def _exchange(arrs, mode, scatter, *, pieces=1, name):
    n = len(arrs)
    flips = _FLIPS[mode]
    nf = len(flips)

    def body(*refs):
        ins, outs = refs[:n], refs[n:2 * n]
        send, recv, loc = refs[2 * n:]
        x, y, c = lax.axis_index("x"), lax.axis_index("y"), lax.axis_index("c")
        me = _slot(mode, x, y, c)
        peers = [(x ^ fx, y ^ fy, c ^ fc) for (fx, fy, fc) in flips]

        def src(a, slot):
            return ins[a].at[slot] if scatter else ins[a]

        def copy(a, j, piece, dst_slot):
            block = src(a, _slot(mode, *peers[j]))
            rows = block.shape[0] // pieces
            at = pl.ds(piece * rows, rows)
            sem = (a * nf + j) * pieces + piece
            return pltpu.make_async_remote_copy(
                src_ref=block.at[at], dst_ref=outs[a].at[dst_slot].at[at],
                send_sem=send.at[sem], recv_sem=recv.at[sem], device_id=peers[j], device_id_type=MESH)

        local = [pltpu.make_async_copy(src(a, me), outs[a].at[me], loc.at[a]) for a in range(n)]
        for lc in local:
            lc.start()
        sends = [copy(a, j, p, me) for a in range(n) for j in range(nf) for p in range(pieces)]
        for cp in sends:
            cp.start()
        for a in range(n):
            for j in range(nf):
                for p in range(pieces):
                    copy(a, j, p, _slot(mode, *peers[j])).wait_recv()
        for cp in sends:
            cp.wait_send()
        for lc in local:
            lc.wait()

    anyspec = pl.BlockSpec(memory_space=pl.ANY)
    return pl.pallas_call(
        body, name=name, in_specs=[anyspec] * n, out_specs=[anyspec] * n,
        out_shape=[jax.ShapeDtypeStruct((nf + 1,) + (a.shape[1:] if scatter else a.shape), a.dtype) for a in arrs],
        scratch_shapes=[pltpu.SemaphoreType.DMA((n * nf * pieces,)), pltpu.SemaphoreType.DMA((n * nf * pieces,)),
                        pltpu.SemaphoreType.DMA((n,))],
    )(*arrs)


def _softplus(v):
    return jnp.maximum(v, 0.0) + jnp.log1p(jnp.exp(-jnp.abs(v)))


def _pad_lanes(v):
    r, n = v.shape
    return jnp.pad(v, ((0, 0), (0, -n % LANES)))


def _to_heads(v):
    s = v.shape[0]
    return v.reshape(s, -1, 64).transpose(1, 0, 2)


def _from_heads(v):
    h, s, p = v.shape
    return v.transpose(1, 0, 2).reshape(s, h * p)


def _fn_rms(v, g):
    return (_rms(v, g),)


def _fn_post(xv, ov, g):
    return (xv + _rms(ov, g),)


def _fn_act(xbc, dtp, fp, dtb, fb):
    return _silu(xbc), _softplus(dtp + dtb), -_softplus(-(fp + fb))


def _fn_mix(y, zs, o, zf, g):
    yg = y * _silu(zs)
    sq = yg * yg
    lane = lax.broadcasted_iota(jnp.int32, (1, D_SSD), 1)
    width = D_SSD // SSD_GROUPS
    rstd = jnp.zeros_like(yg)
    for gi in range(SSD_GROUPS):
        msk = ((lane >= gi * width) & (lane < (gi + 1) * width)).astype(F32)
        ms = jnp.sum(sq * msk, axis=1, keepdims=True) / width
        rstd = rstd + lax.rsqrt(ms + EPS) * msk
    return (jnp.concatenate([yg * rstd * g, o * _silu(zf)], axis=1),)


def _fn_glu(val, gate):
    return (val * jax.nn.sigmoid(gate),)


def _fn_ln(hc, z, g, b):
    mu = jnp.mean(hc, axis=-1, keepdims=True)
    xc = hc - mu
    yn = xc * lax.rsqrt(jnp.mean(xc * xc, axis=-1, keepdims=True) + EPS) * g + b
    return (_silu(yn) * _silu(z),)


def _local_step(x, tgt, w):
    s = x.shape[0]
    d = D_MODEL
    tm = 256
    bf = lambda v: v.astype(BF16)
    c1 = lambda arr: _col(arr, 0, arr.shape[1])
    g = {}

    ew = w["e_w_in"]
    w_z, w_xbc = bf(ew[:, 0:2048]), bf(ew[:, 2048:4096])
    w_dt = bf(_pad_lanes(ew[:, 4096:4112]))
    w_qkv = bf(ew[:, 4112:7184])
    w_f = bf(_pad_lanes(ew[:, 7184:7200]))
    w_eo = bf(w["e_w_out"])
    dtb, fgb = _pad_lanes(w["e_dt_bias"]), _pad_lanes(w["e_fgate_b"])
    alog, dsk = _pad_lanes(w["e_a_log"]), _pad_lanes(w["e_d_skip"])
    w_oi, w_oo = bf(w["o_w_in"]), bf(w["o_w_out"])

    (u0,) = _rowwise_fwd(_fn_rms, [c1(x)], [c1(w["e_norm_pre"])], [(d, BF16)], tm=tm, name="e_pre")
    z = _mm(u0, w_z, name="e_in_z")
    xbc_raw = _mm(u0, w_xbc, name="e_in_xbc")
    qkv = _mm(u0, w_qkv, out_dtype=BF16, name="e_in_qkv")
    dtp = _mm(u0, w_dt, name="e_in_dt")
    fp = _mm(u0, w_f, name="e_in_f")
    xbc_pre = _conv_fwd(xbc_raw, w["e_conv_w"], w["e_conv_b"], name="e_conv")
    act_rows = [c1(xbc_pre), c1(dtp), c1(fp)]
    act_pars = [c1(dtb), c1(fgb)]
    xbc, dt, lf = _rowwise_fwd(_fn_act, act_rows, act_pars, [(2048, F32), (LANES, F32), (LANES, F32)],
                               tm=tm, name="e_act")
    y, hsave = _ssd2_fwd(dt, xbc, alog, dsk, name="e_ssd")
    csum = _cumsum_lanes(lf[:, :FOX_HEADS].T, reverse=False, name="e_cumsum").reshape(FOX_HEADS, 1, s)
    o, lse = _fox3_fwd(qkv, csum, name="e_fox")
    mix_rows = [c1(y), _col(z, 0, D_SSD), c1(o), _col(z, 1, D_FOX)]
    mix_pars = [c1(w["e_ssd_norm"])]
    (hmix,) = _rowwise_fwd(_fn_mix, mix_rows, mix_pars, [(2048, BF16)], tm=tm, name="e_mix")
    out0 = _mm(hmix, w_eo, name="e_out")
    post_rows = [c1(x), c1(out0)]
    (x1,) = _rowwise_fwd(_fn_post, post_rows, [c1(w["e_norm_post"])], [(d, F32)], tm=tm, name="e_post")

    (u1,) = _rowwise_fwd(_fn_rms, [c1(x1)], [c1(w["o_norm_pre"])], [(d, BF16)], tm=tm, name="o_pre")
    p1 = _mm(u1, w_oi, name="o_in")
    glu_rows = [_col(p1, 0, D_CONV), _col(p1, 1, D_CONV)]
    (hg,) = _rowwise_fwd(_fn_glu, glu_rows, [], [(D_CONV, F32)], tm=tm, name="o_glu")
    hc = _conv_fwd(hg, w["o_conv_w"], w["o_conv_b"], name="o_conv")
    ln_rows = [c1(hc), _col(p1, 2, D_CONV)]
    ln_pars = [c1(w["o_ln_g"]), c1(w["o_ln_b"])]
    (h2,) = _rowwise_fwd(_fn_ln, ln_rows, ln_pars, [(D_CONV, BF16)], tm=tm, name="o_ln")
    out1 = _mm(h2, w_oo, name="o_out")

    dx2, dout1, g["o_norm_post"], loss = _final(x1, out1, w["o_norm_post"], tgt, name="loss_head")
    dh2 = _mm(dout1, w_oo, tb=True, name="o_out_dx")
    g["o_w_out"] = _mm(h2, dout1, ta=True, name="o_out_dw")
    (dhc, dz1), (g["o_ln_g"], g["o_ln_b"]) = _rowwise_bwd(_fn_ln, ln_rows, ln_pars, [c1(dh2)], [F32, BF16],
                                                         tm=tm, name="o_ln_bwd")
    dhg = _conv_bwd_x(dhc, w["o_conv_w"], name="o_conv_dx")
    g["o_conv_w"], g["o_conv_b"] = _conv_bwd_w(hg, dhc, CONV_WIDTH, name="o_conv_dw")
    (dval, dgate), _ = _rowwise_bwd(_fn_glu, glu_rows, [], [c1(dhg)], [BF16, BF16], tm=tm, name="o_glu_bwd")
    du1 = _mm(dval, w_oi[:, 0:2048], tb=True, name="o_in_dx0")
    du1 = _mm(dgate, w_oi[:, 2048:4096], tb=True, add=du1, name="o_in_dx1")
    du1 = _mm(dz1, w_oi[:, 4096:6144], tb=True, add=du1, name="o_in_dx2")
    g["o_w_in"] = jnp.concatenate([_mm(u1, dval, ta=True, name="o_in_dw0"), _mm(u1, dgate, ta=True, name="o_in_dw1"),
                                   _mm(u1, dz1, ta=True, name="o_in_dw2")], axis=1)
    (dx1,), (g["o_norm_pre"],) = _rowwise_bwd(_fn_rms, [c1(x1)], [c1(w["o_norm_pre"])], [c1(du1)], [F32],
                                              adds={0: c1(dx2)}, tm=tm, name="o_pre_bwd")

    (dout0,), (g["e_norm_post"],) = _rowwise_bwd(_fn_post, post_rows, [c1(w["e_norm_post"])], [c1(dx1)],
                                                 [None, BF16], tm=tm, name="e_post_bwd")
    dhmix = _mm(dout0, w_eo, tb=True, name="e_out_dx")
    g["e_w_out"] = _mm(hmix, dout0, ta=True, name="e_out_dw")
    (dy, dzs, do, dzf), (g["e_ssd_norm"],) = _rowwise_bwd(_fn_mix, mix_rows, mix_pars, [c1(dhmix)],
                                                        [F32, BF16, F32, BF16], tm=tm, name="e_mix_bwd")
    dox = _fox_dopack(do, o, name="e_fox_dopack")
    dq8, dk, dv, dcs = _fox3_bwd(qkv, csum, dox, lse, name="e_fox_bwd")
    dlf = _pad_lanes(_cumsum_lanes(dcs.reshape(FOX_HEADS, s), reverse=True, name="e_cumsum_bwd").T)
    dxs, ddt, dbm, dcm, dalog, ddsk = _ssd2_bwd(dt, xbc, alog, dsk, hsave, dy, name="e_ssd_bwd")
    dxbc = jnp.concatenate([dxs, dbm, dcm], axis=1)
    (dxbc_pre, ddtp, dfp), (ddtb, dfgb) = _rowwise_bwd(_fn_act, act_rows, act_pars, [c1(dxbc), c1(ddt), c1(dlf)],
                                                      [F32, BF16, BF16], tm=tm, name="e_act_bwd")
    dxbc_raw = bf(_conv_bwd_x(dxbc_pre, w["e_conv_w"], name="e_conv_dx"))
    g["e_conv_w"], g["e_conv_b"] = _conv_bwd_w(xbc_raw, dxbc_pre, SSD_CONV, name="e_conv_dw")
    du0 = _mm(dzs, w_z[:, :D_SSD], tb=True, name="e_in_dx0")
    du0 = _mm(dzf, w_z[:, D_SSD:], tb=True, add=du0, name="e_in_dx1")
    du0 = _mm(dxbc_raw, w_xbc, tb=True, add=du0, name="e_in_dx2")
    eighth = FOX_HEADDIM ** -0.5
    du0 = _mm(dq8, w_qkv[:, :D_FOX] * eighth, tb=True, add=du0, name="e_in_dx3q")
    du0 = _mm(dk, w_qkv[:, D_FOX:2 * D_FOX], tb=True, add=du0, name="e_in_dx3k")
    du0 = _mm(dv, w_qkv[:, 2 * D_FOX:], tb=True, add=du0, name="e_in_dx3v")
    du0 = _mm(ddtp, w_dt, tb=True, add=du0, name="e_in_dx4")
    du0 = _mm(dfp, w_f, tb=True, add=du0, name="e_in_dx5")
    g["e_w_in"] = jnp.concatenate([
        _mm(u0, dzs, ta=True, name="e_in_dw0"), _mm(u0, dzf, ta=True, name="e_in_dw1"),
        _mm(u0, dxbc_raw, ta=True, name="e_in_dw2"), _mm(u0, ddtp, ta=True, name="e_in_dw3")[:, :SSD_HEADS],
        _mm(u0, dq8, ta=True, name="e_in_dw4q") * eighth, _mm(u0, dk, ta=True, name="e_in_dw4k"),
        _mm(u0, dv, ta=True, name="e_in_dw4v"), _mm(u0, dfp, ta=True, name="e_in_dw5")[:, :FOX_HEADS]], axis=1)
    (dx,), (g["e_norm_pre"],) = _rowwise_bwd(_fn_rms, [c1(x)], [c1(w["e_norm_pre"])], [c1(du0)], [F32],
                                             adds={0: c1(dx1)}, tm=tm, name="e_pre_bwd")
    g["e_dt_bias"], g["e_fgate_b"] = ddtb[:, :SSD_HEADS], dfgb[:, :FOX_HEADS]
    g["e_a_log"], g["e_d_skip"] = dalog[:, :SSD_HEADS], ddsk[:, :SSD_HEADS]
    return loss, dx, g


_WEIGHTS = ["e_norm_pre", "e_w_in", "e_conv_w", "e_conv_b", "e_dt_bias", "e_a_log", "e_d_skip", "e_fgate_b",
            "e_ssd_norm", "e_w_out", "e_norm_post", "o_norm_pre", "o_w_in", "o_conv_w", "o_conv_b", "o_ln_g",
            "o_ln_b", "o_w_out", "o_norm_post"]
_BIG = ["e_w_in", "e_w_out", "o_w_in", "o_w_out"]
_ROW_SHARDED = ["e_w_out", "o_w_out"]
_SMALL_SHARDED = ["e_conv_w", "o_norm_pre", "o_conv_w", "o_conv_b", "o_ln_g", "o_ln_b", "o_norm_post"]
_REPLICATED = ["e_norm_pre", "e_conv_b", "e_dt_bias", "e_a_log", "e_d_skip", "e_fgate_b", "e_ssd_norm", "e_norm_post"]
_SMALL = [n for n in _WEIGHTS if n not in _BIG]
N_CHIPS = 4


def _join(gathered, rows):
    k, r, c = gathered.shape
    return gathered.reshape(k * r, c) if rows else gathered.transpose(1, 0, 2).reshape(r, k * c)


def _split(full, rows):
    r, c = full.shape
    return full.reshape(N_CHIPS, r // N_CHIPS, c) if rows else full.reshape(r, N_CHIPS, c // N_CHIPS).transpose(1, 0, 2)


def kernel(x, e_norm_pre, e_w_in, e_conv_w, e_conv_b, e_dt_bias, e_a_log, e_d_skip, e_fgate_b, e_ssd_norm, e_w_out, e_norm_post, o_norm_pre, o_w_in, o_conv_w, o_conv_b, o_ln_g, o_ln_b, o_w_out, o_norm_post, loss_target, m_e_norm_pre, m_e_w_in, m_e_conv_w, m_e_conv_b, m_e_dt_bias, m_e_a_log, m_e_d_skip, m_e_fgate_b, m_e_ssd_norm, m_e_w_out, m_e_norm_post, m_o_norm_pre, m_o_w_in, m_o_conv_w, m_o_conv_b, m_o_ln_g, m_o_ln_b, m_o_w_out, m_o_norm_post, v_e_norm_pre, v_e_w_in, v_e_conv_w, v_e_conv_b, v_e_dt_bias, v_e_a_log, v_e_d_skip, v_e_fgate_b, v_e_ssd_norm, v_e_w_out, v_e_norm_post, v_o_norm_pre, v_o_w_in, v_o_conv_w, v_o_conv_b, v_o_ln_g, v_o_ln_b, v_o_w_out, v_o_norm_post):
    wvals = (e_norm_pre, e_w_in, e_conv_w, e_conv_b, e_dt_bias, e_a_log, e_d_skip, e_fgate_b, e_ssd_norm, e_w_out,
             e_norm_post, o_norm_pre, o_w_in, o_conv_w, o_conv_b, o_ln_g, o_ln_b, o_w_out, o_norm_post)
    mvals = (m_e_norm_pre, m_e_w_in, m_e_conv_w, m_e_conv_b, m_e_dt_bias, m_e_a_log, m_e_d_skip, m_e_fgate_b,
             m_e_ssd_norm, m_e_w_out, m_e_norm_post, m_o_norm_pre, m_o_w_in, m_o_conv_w, m_o_conv_b, m_o_ln_g,
             m_o_ln_b, m_o_w_out, m_o_norm_post)
    vvals = (v_e_norm_pre, v_e_w_in, v_e_conv_w, v_e_conv_b, v_e_dt_bias, v_e_a_log, v_e_d_skip, v_e_fgate_b,
             v_e_ssd_norm, v_e_w_out, v_e_norm_post, v_o_norm_pre, v_o_w_in, v_o_conv_w, v_o_conv_b, v_o_ln_g,
             v_o_ln_b, v_o_w_out, v_o_norm_post)

    def mat(v):
        return v.reshape(v.shape[-2:]) if v.ndim == 3 else v

    w = {n: mat(v) for n, v in zip(_WEIGHTS, wvals)}
    m = {n: mat(v) for n, v in zip(_WEIGHTS, mvals)}
    v2 = {n: mat(v) for n, v in zip(_WEIGHTS, vvals)}
    me_xy = 2 * lax.axis_index("x") + lax.axis_index("y")

    sharded = _BIG + _SMALL_SHARDED
    gathered = _exchange([w[n].astype(BF16) if n in _BIG else w[n] for n in sharded], "xy", False, name="gather_weights")
    full = {n: w[n] for n in _REPLICATED}
    for n, gth in zip(sharded, gathered):
        full[n] = _join(gth, n in _ROW_SHARDED)

    loss, dx, g = _local_step(x[0], loss_target[0], full)
    loss = lax.psum(loss[0, 0], ("x", "y", "c"))

    scattered = _exchange([_split(g[n], n in _ROW_SHARDED).astype(BF16) for n in _BIG], "xy", True, name="scatter_grads")
    partial = [_sum_slots(p, out_dtype=BF16, name="sum_" + n) for n, p in zip(_BIG, scattered)]
    pairs = _exchange(partial, "c", False, pieces=4, name="pair_grads")
    gparts = dict(zip(_BIG, pairs))

    flat = jnp.concatenate([_pad_lanes(g[n].reshape(1, -1)) for n in _SMALL], axis=1).reshape(-1, LANES)
    (all8,) = _exchange([flat], "xyc", False, name="gather_small_grads")
    total = _sum_slots(all8, name="sum_small").reshape(1, -1)
    at = 0
    for n in _SMALL:
        size = g[n].size
        gn = total[:, at:at + size].reshape(g[n].shape)
        at += size + (-size % LANES)
        if n in _SMALL_SHARDED:
            cols = gn.shape[1] // N_CHIPS
            gn = lax.dynamic_slice(gn, (0, me_xy * cols), (gn.shape[0], cols))
        gparts[n] = gn[None]

    grads, deltas, new_m, new_v = [], [], [], []
    for n, orig in zip(_WEIGHTS, wvals):
        gn, dn, mn, vn = _adamw(w[n], gparts[n], m[n], v2[n], name="adamw_" + n)
        for lst, val in zip((grads, deltas, new_m, new_v), (gn, dn, mn, vn)):
            lst.append(val.reshape(orig.shape))
    return (loss, dx[None], *grads, *deltas, *new_m, *new_v)
```

```python
import functools

import jax
import jax.numpy as jnp
from jax import lax
from jax.experimental import pallas as pl
from jax.experimental.pallas import tpu as pltpu

F32 = jnp.float32
BF16 = jnp.bfloat16
MESH = pl.DeviceIdType.MESH

D_MODEL = 1024
D_SSD = 1024
SSD_HEADS = 16
SSD_HEADDIM = 64
SSD_GROUPS = 4
SSD_HPG = 4
D_STATE = 128
SSD_CONV = 4
CHUNK = 128
D_FOX = 1024
FOX_HEADS = 16
FOX_HEADDIM = 64
D_CONV = 2048
CONV_WIDTH = 31
EPS = 1e-6
LANES = 128
VMEM_LIMIT = 56 * 1024 * 1024

ADAM_LR = 0.001
ADAM_B1 = 0.9
ADAM_B2 = 0.999
ADAM_EPS = 1e-08
ADAM_WD = 0.01
ADAM_STEP = 10


def _cparams(sem=None):
    return pltpu.CompilerParams(dimension_semantics=sem, vmem_limit_bytes=VMEM_LIMIT)


def _mm(a, b, *, ta=False, tb=False, add=None, out_dtype=F32, tm=1024, tn=None, tk=2048, name):
    m = a.shape[1] if ta else a.shape[0]
    k = a.shape[0] if ta else a.shape[1]
    n = b.shape[0] if tb else b.shape[1]
    if tn is None:
        tn = 1024 if ta else 512
    tm, tn = min(tm, m), min(tn, n)
    tk = max(t for t in range(LANES, min(tk, k) + 1, LANES) if k % t == 0)
    assert m % tm == 0 and n % tn == 0 and k % tk == 0, (m, n, k, tm, tn, tk)
    nk = k // tk
    dims = (((0 if ta else 1,), (1 if tb else 0,)), ((), ()))

    def body(*refs):
        if add is None:
            a_ref, b_ref, o_ref, acc_ref = refs
            c_ref = None
        else:
            a_ref, b_ref, c_ref, o_ref, acc_ref = refs
        kk = pl.program_id(2)

        @pl.when(kk == 0)
        def _():
            if c_ref is None:
                acc_ref[...] = jnp.zeros_like(acc_ref)
            else:
                acc_ref[...] = c_ref[...].astype(F32)

        acc_ref[...] += lax.dot_general(a_ref[...].astype(BF16), b_ref[...].astype(BF16), dims,
                                        preferred_element_type=F32)

        @pl.when(kk == nk - 1)
        def _():
            o_ref[...] = acc_ref[...].astype(o_ref.dtype)

    a_spec = (pl.BlockSpec((tk, tm), lambda i, j, kk: (kk, i)) if ta
              else pl.BlockSpec((tm, tk), lambda i, j, kk: (i, kk)))
    b_spec = (pl.BlockSpec((tn, tk), lambda i, j, kk: (j, kk)) if tb
              else pl.BlockSpec((tk, tn), lambda i, j, kk: (kk, j)))
    o_spec = pl.BlockSpec((tm, tn), lambda i, j, kk: (i, j))
    in_specs, args = [a_spec, b_spec], [a, b]
    if add is not None:
        in_specs.append(o_spec)
        args.append(add)
    return pl.pallas_call(
        body, name=name, grid=(m // tm, n // tn, nk),
        in_specs=in_specs, out_specs=o_spec,
        out_shape=jax.ShapeDtypeStruct((m, n), out_dtype),
        scratch_shapes=[pltpu.VMEM((tm, tn), F32)],
        compiler_params=_cparams(("parallel", "parallel", "arbitrary")),
    )(*args)


def _col(arr, cb, width):
    return (arr, cb, width)


def _row_specs(ops, tm):
    return [pl.BlockSpec((tm, w), lambda i, cb=cb: (i, cb)) for (_, cb, w) in ops]


def _par_specs(ops):
    return [pl.BlockSpec((a.shape[0], w), lambda i, cb=cb: (0, cb)) for (a, cb, w) in ops]


def _rowwise_fwd(fn, rows, params, outs, *, tm, name):
    s = rows[0][0].shape[0]
    tm = min(tm, s)
    nr, npar = len(rows), len(params)

    def body(*refs):
        rv = [r[...].astype(F32) for r in refs[:nr]]
        pv = [p[...].astype(F32) for p in refs[nr:nr + npar]]
        res = fn(*rv, *pv)
        for o_ref, val in zip(refs[nr + npar:], res):
            o_ref[...] = val.astype(o_ref.dtype)

    return pl.pallas_call(
        body, name=name, grid=(s // tm,),
        in_specs=_row_specs(rows, tm) + _par_specs(params),
        out_specs=[pl.BlockSpec((tm, w), lambda i: (i, 0)) for (w, _) in outs],
        out_shape=[jax.ShapeDtypeStruct((s, w), dt) for (w, dt) in outs],
        compiler_params=_cparams(("parallel",)),
    )(*[r[0] for r in rows], *[p[0] for p in params])


def _rowwise_bwd(fn, rows, params, couts, row_grads, *, adds=None, tm, name):
    adds = adds or {}
    s = rows[0][0].shape[0]
    tm = min(tm, s)
    nr, npar, nc = len(rows), len(params), len(couts)
    add_keys = sorted(adds)
    want = [i for i, dt in enumerate(row_grads) if dt is not None]

    def body(*refs):
        i = pl.program_id(0)
        rv = [r[...].astype(F32) for r in refs[:nr]]
        pv = [p[...].astype(F32) for p in refs[nr:nr + npar]]
        cv = [c[...].astype(F32) for c in refs[nr + npar:nr + npar + nc]]
        av = {k: r[...].astype(F32) for k, r in zip(add_keys, refs[nr + npar + nc:nr + npar + nc + len(add_keys)])}
        orefs = refs[nr + npar + nc + len(add_keys):]
        _, vjp = jax.vjp(lambda rr, pp: tuple(fn(*rr, *pp)), rv, pv)
        drows, dpars = vjp(tuple(cv))
        for o_ref, ri in zip(orefs[:len(want)], want):
            g = drows[ri]
            if ri in av:
                g = g + av[ri]
            o_ref[...] = g.astype(o_ref.dtype)

        @pl.when(i == 0)
        def _():
            for o_ref in orefs[len(want):]:
                o_ref[...] = jnp.zeros_like(o_ref)

        for o_ref, g in zip(orefs[len(want):], dpars):
            o_ref[...] += g

    add_ops = [adds[k] for k in add_keys]
    out_specs = ([pl.BlockSpec((tm, rows[ri][2]), lambda i: (i, 0)) for ri in want]
                 + [pl.BlockSpec((p[0].shape[0], p[2]), lambda i: (0, 0)) for p in params])
    out_shape = ([jax.ShapeDtypeStruct((s, rows[ri][2]), row_grads[ri]) for ri in want]
                 + [jax.ShapeDtypeStruct((p[0].shape[0], p[2]), F32) for p in params])
    res = pl.pallas_call(
        body, name=name, grid=(s // tm,),
        in_specs=_row_specs(rows, tm) + _par_specs(params) + _row_specs(couts, tm) + _row_specs(add_ops, tm),
        out_specs=out_specs, out_shape=out_shape,
        compiler_params=_cparams(("arbitrary",)),
    )(*[r[0] for r in rows], *[p[0] for p in params], *[c[0] for c in couts], *[a[0] for a in add_ops])
    return res[:len(want)], res[len(want):]


def _silu(v):
    return v * jax.nn.sigmoid(v)


def _rms(v, g):
    return v * lax.rsqrt(jnp.mean(v * v, axis=-1, keepdims=True) + EPS) * g


SUBLANES = 8
CONV_ROWS = 256


def _halo(shifts):
    up = lambda v: -(-v // SUBLANES) * SUBLANES
    return up(max(0, -min(shifts))), up(max(0, max(shifts)))


def _fill_halo(xp_sc, x_ref, front, back):
    s = x_ref.shape[0]
    if front:
        xp_sc[0:front, :] = jnp.zeros((front, LANES), F32)
    if back:
        xp_sc[front + s:front + s + back, :] = jnp.zeros((back, LANES), F32)
    xp_sc[front:front + s, :] = x_ref[...]


def _shift_conv(x, w, b, shifts, *, name):
    s, c = x.shape
    tr = min(CONV_ROWS, s)
    nk = len(shifts)
    front, back = _halo(shifts)

    def body(*refs):
        if b is None:
            x_ref, w_ref, o_ref, xp_sc = refs
        else:
            x_ref, w_ref, b_ref, o_ref, xp_sc = refs
        _fill_halo(xp_sc, x_ref, front, back)

        def chunk(r, carry):
            base = pl.multiple_of(r * tr, tr)
            acc = jnp.zeros((tr, LANES), F32) if b is None else jnp.broadcast_to(b_ref[...], (tr, LANES))
            for kk in range(nk):
                acc = acc + xp_sc[pl.ds(base + front + shifts[kk], tr), :] * w_ref[kk:kk + 1, :]
            o_ref[pl.ds(base, tr), :] = acc
            return carry

        lax.fori_loop(0, s // tr, chunk, 0)

    strip = pl.BlockSpec((s, LANES), lambda cb: (0, cb))
    in_specs = [strip, pl.BlockSpec((nk, LANES), lambda cb: (0, cb))]
    args = [x, w]
    if b is not None:
        in_specs.append(pl.BlockSpec((1, LANES), lambda cb: (0, cb)))
        args.append(b)
    return pl.pallas_call(
        body, name=name, grid=(c // LANES,), in_specs=in_specs, out_specs=strip,
        out_shape=jax.ShapeDtypeStruct((s, c), F32),
        scratch_shapes=[pltpu.VMEM((front + s + back, LANES), F32)],
        compiler_params=_cparams(("parallel",)),
    )(*args)


def _conv_fwd(x, w, b, *, name):
    k = w.shape[0]
    return _shift_conv(x, w, b, [kk - (k - 1) for kk in range(k)], name=name)


def _conv_bwd_x(dy, w, *, name):
    k = w.shape[0]
    return _shift_conv(dy, w, None, [(k - 1) - kk for kk in range(k)], name=name)


def _conv_bwd_w(x, dy, k, *, name):
    s, c = x.shape
    tr = min(CONV_ROWS, s)
    shifts = [kk - (k - 1) for kk in range(k)]
    front, back = _halo(shifts)

    def fold(v):
        return jnp.sum(v.reshape(tr // SUBLANES, SUBLANES, LANES), axis=0)

    def body(x_ref, dy_ref, dw_ref, db_ref, xp_sc, dw_sc, db_sc):
        _fill_halo(xp_sc, x_ref, front, back)
        dw_sc[...] = jnp.zeros_like(dw_sc)
        db_sc[...] = jnp.zeros_like(db_sc)

        def chunk(r, carry):
            base = pl.multiple_of(r * tr, tr)
            dyv = dy_ref[pl.ds(base, tr), :]
            db_sc[...] += fold(dyv)
            for kk in range(k):
                dw_sc[kk] += fold(xp_sc[pl.ds(base + front + shifts[kk], tr), :] * dyv)
            return carry

        lax.fori_loop(0, s // tr, chunk, 0)
        db_ref[...] = jnp.sum(db_sc[...], axis=0, keepdims=True)
        for kk in range(k):
            dw_ref[kk:kk + 1, :] = jnp.sum(dw_sc[kk], axis=0, keepdims=True)

    strip = pl.BlockSpec((s, LANES), lambda cb: (0, cb))
    return pl.pallas_call(
        body, name=name, grid=(c // LANES,), in_specs=[strip, strip],
        out_specs=[pl.BlockSpec((k, LANES), lambda cb: (0, cb)), pl.BlockSpec((1, LANES), lambda cb: (0, cb))],
        out_shape=[jax.ShapeDtypeStruct((k, c), F32), jax.ShapeDtypeStruct((1, c), F32)],
        scratch_shapes=[pltpu.VMEM((front + s + back, LANES), F32), pltpu.VMEM((k, SUBLANES, LANES), F32),
                        pltpu.VMEM((SUBLANES, LANES), F32)],
        compiler_params=_cparams(("parallel",)),
    )(x, dy)


_DIMS = {"nn": ((1,), (0,)), "nt": ((1,), (1,)), "tn": ((0,), (0,))}


def _bd(a, b, mode):
    return lax.dot_general(a.astype(BF16), b.astype(BF16), (_DIMS[mode], ((), ())), preferred_element_type=F32)


@functools.partial(jax.custom_vjp, nondiff_argnums=(2,))
def _bdot(a, b, mode):
    return _bd(a, b, mode)


def _bdot_fwd(a, b, mode):
    return _bd(a, b, mode), (a, b)


def _bdot_bwd(mode, res, g):
    a, b = res
    if mode == "nn":
        return _bd(g, b, "nt"), _bd(a, g, "tn")
    if mode == "nt":
        return _bd(g, b, "nn"), _bd(g, a, "tn")
    return _bd(b, g, "nt"), _bd(a, g, "nn")


_bdot.defvjp(_bdot_fwd, _bdot_bwd)


def _split3(v):
    hi = v.astype(BF16)
    r1 = v - hi.astype(F32)
    mid = r1.astype(BF16)
    lo = (r1 - mid.astype(F32)).astype(BF16)
    return hi, mid, lo


def _mask_dot(mask01, v, mode):
    out = None
    for part in _split3(v):
        if mode == "vn":
            t = lax.dot_general(part, mask01, (_DIMS["nn"], ((), ())), preferred_element_type=F32)
        else:
            t = lax.dot_general(mask01, part, (_DIMS[mode], ((), ())), preferred_element_type=F32)
        out = t if out is None else out + t
    return out


def _lower_tri(n):
    r = lax.broadcasted_iota(jnp.int32, (n, n), 0)
    c = lax.broadcasted_iota(jnp.int32, (n, n), 1)
    return (r >= c).astype(BF16)


@jax.custom_vjp
def _tri_dot(w):
    return _mask_dot(_lower_tri(w.shape[0]), w, "nn")


def _tri_dot_fwd(w):
    return _tri_dot(w), None


def _tri_dot_bwd(_, g):
    return (_mask_dot(_lower_tri(g.shape[0]), g, "tn"),)


_tri_dot.defvjp(_tri_dot_fwd, _tri_dot_bwd)


def _cumsum_lanes(x, *, reverse, name):
    h, s = x.shape
    n = s // LANES

    def body(x_ref, o_ref):
        r = lax.broadcasted_iota(jnp.int32, (LANES, LANES), 0)
        c = lax.broadcasted_iota(jnp.int32, (LANES, LANES), 1)
        m01 = ((r >= c) if reverse else (r <= c)).astype(BF16)

        def step(t, carry):
            ci = (n - 1 - t) if reverse else t
            at = pl.ds(pl.multiple_of(ci * LANES, LANES), LANES)
            blk = x_ref[:, at]
            o_ref[:, at] = _mask_dot(m01, blk, "vn") + carry
            return carry + jnp.sum(blk, axis=1, keepdims=True)

        lax.fori_loop(0, n, step, jnp.zeros((h, 1), F32))

    return pl.pallas_call(body, name=name, out_shape=jax.ShapeDtypeStruct((h, s), F32),
                          compiler_params=_cparams())(x)


def _ssd_chunk(xs, dt, bm, cm, hin, a, dsk, head0):
    n = CHUNK
    row = lax.broadcasted_iota(jnp.int32, (n, n), 0)
    col = lax.broadcasted_iota(jnp.int32, (n, n), 1)
    lower = row >= col
    ustrict = (row > col).astype(F32)
    lane = lax.broadcasted_iota(jnp.int32, (1, LANES), 1)
    sub = lax.broadcasted_iota(jnp.int32, (n, 1), 0)
    e_first = (sub == 0).astype(F32)
    e_last = (sub == n - 1).astype(F32)
    lane0 = (lane == 0).astype(F32)
    cb = _bdot(cm, bm, "nt")
    da = dt * (-jnp.exp(a))
    ys, houts = [], []
    for r in range(SSD_HPG):
        oh = (lane == head0 + r).astype(F32)
        dt_col = jnp.sum(dt * oh, axis=1, keepdims=True)
        da_col = jnp.sum(da * oh, axis=1, keepdims=True)
        dsk_h = jnp.sum(dsk * oh, axis=1, keepdims=True)
        seg = _tri_dot(da_col * ustrict)
        decay = jnp.where(lower, jnp.exp(seg), 0.0)
        cs_col = jnp.sum(seg * lane0, axis=1, keepdims=True) + jnp.sum(da_col * e_first, axis=0, keepdims=True)
        total = jnp.sum(cs_col * e_last, axis=0, keepdims=True)
        xd = xs[r] * dt_col
        y_diag = _bdot(cb * decay, xd, "nn")
        contrib = _bdot(xd * jnp.exp(total - cs_col), bm, "tn")
        houts.append(hin[r] * jnp.exp(total) + contrib)
        y_off = _bdot(cm, hin[r], "nt") * jnp.exp(cs_col)
        ys.append(y_diag + y_off + xs[r] * dsk_h)
    return ys, houts


def _ssd_specs(nc, rev):
    cc = (lambda c: nc - 1 - c) if rev else (lambda c: c)
    hm = pl.BlockSpec((SSD_HPG, CHUNK, SSD_HEADDIM), lambda c, g: (g, cc(c), 0))
    row = pl.BlockSpec((CHUNK, LANES), lambda c, g: (cc(c), 0))
    bmat = pl.BlockSpec((CHUNK, LANES), lambda c, g: (cc(c), D_SSD // LANES + g))
    cmat = pl.BlockSpec((CHUNK, LANES), lambda c, g: (cc(c), D_SSD // LANES + SSD_GROUPS + g))
    par = pl.BlockSpec((1, LANES), lambda c, g: (0, 0))
    hs = pl.BlockSpec((1, SSD_HPG, SSD_HEADDIM, D_STATE), lambda c, g: (cc(c), g, 0, 0))
    return hm, row, bmat, cmat, par, hs


def _ssd_fwd(xs_hm, dt, xbc, a, dsk, *, name):
    s = xs_hm.shape[1]
    nc = s // CHUNK
    hm, row, bmat, cmat, par, hs = _ssd_specs(nc, False)

    def body(xs_ref, dt_ref, bm_ref, cm_ref, a_ref, dsk_ref, y_ref, hs_ref, h_sc):
        c, g = pl.program_id(0), pl.program_id(1)
        mine = pl.ds(g * SSD_HPG, SSD_HPG)

        @pl.when(c == 0)
        def _():
            h_sc[mine] = jnp.zeros((SSD_HPG, SSD_HEADDIM, D_STATE), F32)

        hin = [h_sc[g * SSD_HPG + r] for r in range(SSD_HPG)]
        ys, houts = _ssd_chunk([xs_ref[r] for r in range(SSD_HPG)], dt_ref[...], bm_ref[...], cm_ref[...],
                               hin, a_ref[...], dsk_ref[...], g * SSD_HPG)
        for r in range(SSD_HPG):
            y_ref[r] = ys[r]
            hs_ref[0, r] = hin[r]
            h_sc[g * SSD_HPG + r] = houts[r]

    return pl.pallas_call(
        body, name=name, grid=(nc, SSD_GROUPS),
        in_specs=[hm, row, bmat, cmat, par, par], out_specs=[hm, hs],
        out_shape=[jax.ShapeDtypeStruct(xs_hm.shape, F32),
                   jax.ShapeDtypeStruct((nc, SSD_HEADS, SSD_HEADDIM, D_STATE), F32)],
        scratch_shapes=[pltpu.VMEM((SSD_HEADS, SSD_HEADDIM, D_STATE), F32)],
        compiler_params=_cparams(("arbitrary", "arbitrary")),
    )(xs_hm, dt, xbc, xbc, a, dsk)


def _ssd_bwd(xs_hm, dt, xbc, a, dsk, hsave, dy_hm, *, name):
    s = xs_hm.shape[1]
    nc = s // CHUNK
    hm, row, bmat, cmat, par, hs = _ssd_specs(nc, True)
    gmat = pl.BlockSpec((CHUNK, LANES), lambda c, g: (nc - 1 - c, g))

    def body(xs_ref, dt_ref, bm_ref, cm_ref, a_ref, dsk_ref, hs_ref, dy_ref,
             dxs_ref, ddt_ref, dbm_ref, dcm_ref, da_ref, ddsk_ref, dh_sc):
        c, g = pl.program_id(0), pl.program_id(1)
        mine = pl.ds(g * SSD_HPG, SSD_HPG)

        @pl.when(c == 0)
        def _():
            dh_sc[mine] = jnp.zeros((SSD_HPG, SSD_HEADDIM, D_STATE), F32)

        @pl.when((c == 0) & (g == 0))
        def _():
            da_ref[...] = jnp.zeros_like(da_ref)
            ddsk_ref[...] = jnp.zeros_like(ddsk_ref)

        @pl.when(g == 0)
        def _():
            ddt_ref[...] = jnp.zeros_like(ddt_ref)

        head0 = g * SSD_HPG
        prim = ([xs_ref[r] for r in range(SSD_HPG)], dt_ref[...], bm_ref[...], cm_ref[...],
                [hs_ref[0, r] for r in range(SSD_HPG)], a_ref[...], dsk_ref[...])
        _, vjp = jax.vjp(lambda *p: _ssd_chunk(*p, head0), *prim)
        cot = ([dy_ref[r] for r in range(SSD_HPG)], [dh_sc[g * SSD_HPG + r] for r in range(SSD_HPG)])
        dxs, ddt, dbm, dcm, dhin, da, ddsk = vjp(cot)
        for r in range(SSD_HPG):
            dxs_ref[r] = dxs[r]
            dh_sc[g * SSD_HPG + r] = dhin[r]
        ddt_ref[...] += ddt
        dbm_ref[...] = dbm
        dcm_ref[...] = dcm
        da_ref[...] += da
        ddsk_ref[...] += ddsk

    return pl.pallas_call(
        body, name=name, grid=(nc, SSD_GROUPS),
        in_specs=[hm, row, bmat, cmat, par, par, hs, hm],
        out_specs=[hm, row, gmat, gmat, par, par],
        out_shape=[jax.ShapeDtypeStruct(xs_hm.shape, F32), jax.ShapeDtypeStruct((s, LANES), F32),
                   jax.ShapeDtypeStruct((s, SSD_GROUPS * D_STATE), F32),
                   jax.ShapeDtypeStruct((s, SSD_GROUPS * D_STATE), F32),
                   jax.ShapeDtypeStruct((1, LANES), F32), jax.ShapeDtypeStruct((1, LANES), F32)],
        scratch_shapes=[pltpu.VMEM((SSD_HEADS, SSD_HEADDIM, D_STATE), F32)],
        compiler_params=_cparams(("arbitrary", "arbitrary")),
    )(xs_hm, dt, xbc, xbc, a, dsk, hsave, dy_hm)


SSD_PAIRS = SSD_HPG // 2


def _ssd2_chunk(xs, dt, bm, cm, hin, a, dsk, head0):
    n = CHUNK
    row = lax.broadcasted_iota(jnp.int32, (n, n), 0)
    col = lax.broadcasted_iota(jnp.int32, (n, n), 1)
    lower = row >= col
    ustrict = (row > col).astype(F32)
    lane = lax.broadcasted_iota(jnp.int32, (1, LANES), 1)
    sub = lax.broadcasted_iota(jnp.int32, (n, 1), 0)
    e_first = (sub == 0).astype(F32)
    e_last = (sub == n - 1).astype(F32)
    lane0 = (lane == 0).astype(F32)
    half_l = [(lane < LANES // 2).astype(F32), (lane >= LANES // 2).astype(F32)]
    half_s = [(sub < LANES // 2).astype(F32), (sub >= LANES // 2).astype(F32)]
    cb = _bdot(cm, bm, "nt")
    da = dt * (-jnp.exp(a))
    ys, houts = [], []
    for pr in range(SSD_PAIRS):
        y = jnp.zeros((n, LANES), F32)
        xdte = jnp.zeros((n, LANES), F32)
        lane_gain = jnp.zeros((n, LANES), F32)
        row_gain = jnp.zeros((LANES, 1), F32)
        for hf in range(2):
            oh = (lane == head0 + 2 * pr + hf).astype(F32)
            dt_col = jnp.sum(dt * oh, axis=1, keepdims=True)
            da_col = jnp.sum(da * oh, axis=1, keepdims=True)
            dsk_h = jnp.sum(dsk * oh, axis=1, keepdims=True)
            seg = _tri_dot(da_col * ustrict)
            decay = jnp.where(lower, jnp.exp(seg), 0.0)
            cs_col = jnp.sum(seg * lane0, axis=1, keepdims=True) + jnp.sum(da_col * e_first, axis=0, keepdims=True)
            total = jnp.sum(cs_col * e_last, axis=0, keepdims=True)
            xh = xs[pr] * half_l[hf]
            xd = xh * dt_col
            y = y + _bdot(cb * decay, xd, "nn") + xh * dsk_h
            xdte = xdte + xd * jnp.exp(total - cs_col)
            lane_gain = lane_gain + jnp.exp(cs_col) * half_l[hf]
            row_gain = row_gain + jnp.exp(total) * half_s[hf]
        houts.append(hin[pr] * row_gain + _bdot(xdte, bm, "tn"))
        ys.append(y + _bdot(cm, hin[pr], "nt") * lane_gain)
    return ys, houts


def _ssd2_specs(nc, rev):
    cc = (lambda c: nc - 1 - c) if rev else (lambda c: c)
    wide = SSD_PAIRS * LANES
    act = pl.BlockSpec((CHUNK, wide), lambda c, g: (cc(c), g))
    row = pl.BlockSpec((CHUNK, LANES), lambda c, g: (cc(c), 0))
    bmat = pl.BlockSpec((CHUNK, LANES), lambda c, g: (cc(c), D_SSD // LANES + g))
    cmat = pl.BlockSpec((CHUNK, LANES), lambda c, g: (cc(c), D_SSD // LANES + SSD_GROUPS + g))
    par = pl.BlockSpec((1, LANES), lambda c, g: (0, 0))
    hs = pl.BlockSpec((1, SSD_PAIRS, LANES, D_STATE), lambda c, g: (cc(c), g, 0, 0))
    return act, row, bmat, cmat, par, hs


def _pair_cols(ref):
    return [ref[:, pr * LANES:(pr + 1) * LANES] for pr in range(SSD_PAIRS)]


def _ssd2_fwd(dt, xbc, a, dsk, *, name):
    s = xbc.shape[0]
    nc = s // CHUNK
    act, row, bmat, cmat, par, hs = _ssd2_specs(nc, False)

    def body(xs_ref, dt_ref, bm_ref, cm_ref, a_ref, dsk_ref, y_ref, hs_ref, h_sc):
        c, g = pl.program_id(0), pl.program_id(1)

        @pl.when(c == 0)
        def _():
            h_sc[pl.ds(g * SSD_PAIRS, SSD_PAIRS)] = jnp.zeros((SSD_PAIRS, LANES, D_STATE), F32)

        hin = [h_sc[g * SSD_PAIRS + pr] for pr in range(SSD_PAIRS)]
        ys, houts = _ssd2_chunk(_pair_cols(xs_ref), dt_ref[...], bm_ref[...], cm_ref[...], hin,
                                a_ref[...], dsk_ref[...], g * SSD_HPG)
        for pr in range(SSD_PAIRS):
            y_ref[:, pr * LANES:(pr + 1) * LANES] = ys[pr]
            hs_ref[0, pr] = hin[pr]
            h_sc[g * SSD_PAIRS + pr] = houts[pr]

    return pl.pallas_call(
        body, name=name, grid=(nc, SSD_GROUPS),
        in_specs=[act, row, bmat, cmat, par, par], out_specs=[act, hs],
        out_shape=[jax.ShapeDtypeStruct((s, D_SSD), F32),
                   jax.ShapeDtypeStruct((nc, SSD_HEADS // 2, LANES, D_STATE), F32)],
        scratch_shapes=[pltpu.VMEM((SSD_HEADS // 2, LANES, D_STATE), F32)],
        compiler_params=_cparams(("arbitrary", "arbitrary")),
    )(xbc, dt, xbc, xbc, a, dsk)


def _ssd2_bwd(dt, xbc, a, dsk, hsave, dy, *, carry=None, name):
    s = xbc.shape[0]
    nc = s // CHUNK
    act, row, bmat, cmat, par, hs = _ssd2_specs(nc, True)
    gmat = pl.BlockSpec((CHUNK, LANES), lambda c, g: (nc - 1 - c, g))

    def body(*refs):
        ins, outs, (dh_sc,), comm = _carried(carry, refs, 8, 6, 1)
        xs_ref, dt_ref, bm_ref, cm_ref, a_ref, dsk_ref, hs_ref, dy_ref = ins
        dxs_ref, ddt_ref, dbm_ref, dcm_ref, da_ref, ddsk_ref = outs
        c, g = pl.program_id(0), pl.program_id(1)
        if carry is not None:
            @pl.when((c == 0) & (g == 0))
            def _():
                carry.start(*comm)

        @pl.when(c == 0)
        def _():
            dh_sc[pl.ds(g * SSD_PAIRS, SSD_PAIRS)] = jnp.zeros((SSD_PAIRS, LANES, D_STATE), F32)

        @pl.when((c == 0) & (g == 0))
        def _():
            da_ref[...] = jnp.zeros_like(da_ref)
            ddsk_ref[...] = jnp.zeros_like(ddsk_ref)

        @pl.when(g == 0)
        def _():
            ddt_ref[...] = jnp.zeros_like(ddt_ref)

        head0 = g * SSD_HPG
        prim = (_pair_cols(xs_ref), dt_ref[...], bm_ref[...], cm_ref[...],
                [hs_ref[0, pr] for pr in range(SSD_PAIRS)], a_ref[...], dsk_ref[...])
        _, vjp = jax.vjp(lambda *p: _ssd2_chunk(*p, head0), *prim)
        cot = (_pair_cols(dy_ref), [dh_sc[g * SSD_PAIRS + pr] for pr in range(SSD_PAIRS)])
        dxs, ddt, dbm, dcm, dhin, da, ddsk = vjp(cot)
        for pr in range(SSD_PAIRS):
            dxs_ref[:, pr * LANES:(pr + 1) * LANES] = dxs[pr]
            dh_sc[g * SSD_PAIRS + pr] = dhin[pr]
        ddt_ref[...] += ddt
        dbm_ref[...] = dbm
        dcm_ref[...] = dcm
        da_ref[...] += da
        ddsk_ref[...] += ddsk
        if carry is not None:
            @pl.when((c == nc - 1) & (g == SSD_GROUPS - 1))
            def _():
                carry.wait(*comm)

    extra = carry if carry is not None else _Exchange([], "c", False)
    return pl.pallas_call(
        body, name=name, grid=(nc, SSD_GROUPS),
        in_specs=[act, row, bmat, cmat, par, par, hs, act] + extra.in_specs,
        out_specs=[act, row, gmat, gmat, par, par] + extra.out_specs,
        out_shape=[jax.ShapeDtypeStruct((s, D_SSD), F32), jax.ShapeDtypeStruct((s, LANES), F32),
                   jax.ShapeDtypeStruct((s, SSD_GROUPS * D_STATE), F32),
                   jax.ShapeDtypeStruct((s, SSD_GROUPS * D_STATE), F32),
                   jax.ShapeDtypeStruct((1, LANES), F32), jax.ShapeDtypeStruct((1, LANES), F32)] + extra.out_shape,
        scratch_shapes=[pltpu.VMEM((SSD_HEADS // 2, LANES, D_STATE), F32)]
        + (carry.scratch if carry is not None else []),
        compiler_params=_cparams(("arbitrary", "arbitrary")),
    )(xbc, dt, xbc, xbc, a, dsk, hsave, dy, *extra.arrs)


FOX_BLOCK = 512
NEG = -1e30


def _fox_scores(q, k, cref, ck, strictly_below):
    t = q.shape[0]
    s = lax.dot_general(q, k, (_DIMS["nt"], ((), ())), preferred_element_type=F32) * (FOX_HEADDIM ** -0.5)
    s = s + (cref - ck)
    row = lax.broadcasted_iota(jnp.int32, (t, t), 0)
    col = lax.broadcasted_iota(jnp.int32, (t, t), 1)
    mask = (row >= col) | strictly_below
    return s, mask


def _fox_fwd(q, k, v, c, *, name):
    h, s, p = q.shape
    t = min(FOX_BLOCK, s)
    nb = s // t

    def body(q_ref, k_ref, v_ref, cq_ref, ck_ref, o_ref, lse_ref, m_sc, l_sc, acc_sc):
        i, j = pl.program_id(1), pl.program_id(2)

        @pl.when(j == 0)
        def _():
            m_sc[...] = jnp.full_like(m_sc, NEG)
            l_sc[...] = jnp.zeros_like(l_sc)
            acc_sc[...] = jnp.zeros_like(acc_sc)

        @pl.when(j <= i)
        def _():
            sc, mask = _fox_scores(q_ref[0], k_ref[0], cq_ref[0, 0:1, 0:1], ck_ref[0], j < i)
            sc = jnp.where(mask, sc, NEG)
            m_old = m_sc[...]
            m_new = jnp.maximum(m_old, jnp.max(sc, axis=1, keepdims=True))
            alpha = jnp.exp(m_old - m_new)
            pr = jnp.exp(sc - m_new)
            l_sc[...] = alpha * l_sc[...] + jnp.sum(pr, axis=1, keepdims=True)
            pr_hi = pr.astype(BF16)
            pr_lo = (pr - pr_hi.astype(F32)).astype(BF16)
            pv = (lax.dot_general(pr_hi, v_ref[0], (_DIMS["nn"], ((), ())), preferred_element_type=F32)
                  + lax.dot_general(pr_lo, v_ref[0], (_DIMS["nn"], ((), ())), preferred_element_type=F32))
            acc_sc[...] = alpha * acc_sc[...] + pv
            m_sc[...] = m_new

        @pl.when(j == i)
        def _():
            o_ref[0] = acc_sc[...] / l_sc[...]
            lse_ref[0] = jnp.broadcast_to(m_sc[...] + jnp.log(l_sc[...]), (t, LANES))

    qspec = pl.BlockSpec((1, t, p), lambda hh, i, j: (hh, i, 0))
    kspec = pl.BlockSpec((1, t, p), lambda hh, i, j: (hh, jnp.minimum(j, i), 0))
    return pl.pallas_call(
        body, name=name, grid=(h, nb, nb),
        in_specs=[qspec, kspec, kspec,
                  pl.BlockSpec((1, 1, t), lambda hh, i, j: (hh, 0, i)),
                  pl.BlockSpec((1, 1, t), lambda hh, i, j: (hh, 0, jnp.minimum(j, i)))],
        out_specs=[qspec, pl.BlockSpec((1, t, LANES), lambda hh, i, j: (hh, i, 0))],
        out_shape=[jax.ShapeDtypeStruct((h, s, p), F32), jax.ShapeDtypeStruct((h, s, LANES), F32)],
        scratch_shapes=[pltpu.VMEM((t, 1), F32), pltpu.VMEM((t, 1), F32), pltpu.VMEM((t, p), F32)],
        compiler_params=_cparams(("parallel", "arbitrary", "arbitrary")),
    )(q, k, v, c, c)


def _fox_bwd(q, k, v, c, o, lse, do, *, name):
    h, s, p = q.shape
    t = min(FOX_BLOCK, s)
    nb = s // t
    scale = FOX_HEADDIM ** -0.5

    def body(q_ref, k_ref, v_ref, cq_ref, ck_ref, o_ref, lse_ref, do_ref,
             dq_ref, dk_ref, dv_ref, dc_ref, dk_sc, dv_sc, dc_sc):
        j, i = pl.program_id(1), pl.program_id(2)

        @pl.when(i == 0)
        def _():
            dk_sc[...] = jnp.zeros_like(dk_sc)
            dv_sc[...] = jnp.zeros_like(dv_sc)
            dc_sc[...] = jnp.zeros_like(dc_sc)

        @pl.when(i >= j)
        def _():
            qv, kv, vv = q_ref[0], k_ref[0], v_ref[0]
            sc, mask = _fox_scores(qv, kv, cq_ref[0, 0:1, 0:1], ck_ref[0], i > j)
            pr = jnp.where(mask, jnp.exp(sc - lse_ref[0, :, 0:1]), 0.0)
            dov = do_ref[0]
            dob = dov.astype(BF16)
            prb = pr.astype(BF16)
            dv_sc[...] += lax.dot_general(prb, dob, (_DIMS["tn"], ((), ())), preferred_element_type=F32)
            dp = lax.dot_general(dob, vv, (_DIMS["nt"], ((), ())), preferred_element_type=F32)
            dcol = jnp.sum(dob.astype(F32) * o_ref[0], axis=1, keepdims=True)
            ds = pr * (dp - dcol)
            dc_sc[...] -= jnp.sum(ds, axis=0, keepdims=True)
            dsb = ds.astype(BF16)
            dqc = scale * lax.dot_general(dsb, kv, (_DIMS["nn"], ((), ())), preferred_element_type=F32)
            at = pl.ds(pl.multiple_of(i * t, t), t)

            @pl.when(j == 0)
            def _():
                dq_ref[0, at, :] = dqc

            @pl.when(j > 0)
            def _():
                dq_ref[0, at, :] += dqc

            dk_sc[...] += scale * lax.dot_general(dsb, qv, (_DIMS["tn"], ((), ())), preferred_element_type=F32)

        @pl.when(i == nb - 1)
        def _():
            dk_ref[0] = dk_sc[...]
            dv_ref[0] = dv_sc[...]
            dc_ref[0] = dc_sc[...]

    qspec = pl.BlockSpec((1, t, p), lambda hh, j, i: (hh, jnp.maximum(i, j), 0))
    kspec = pl.BlockSpec((1, t, p), lambda hh, j, i: (hh, j, 0))
    cq = pl.BlockSpec((1, 1, t), lambda hh, j, i: (hh, 0, jnp.maximum(i, j)))
    ck = pl.BlockSpec((1, 1, t), lambda hh, j, i: (hh, 0, j))
    return pl.pallas_call(
        body, name=name, grid=(h, nb, nb),
        in_specs=[qspec, kspec, kspec, cq, ck, qspec,
                  pl.BlockSpec((1, t, LANES), lambda hh, j, i: (hh, jnp.maximum(i, j), 0)), qspec],
        out_specs=[pl.BlockSpec((1, s, p), lambda hh, j, i: (hh, 0, 0)), kspec, kspec, ck],
        out_shape=[jax.ShapeDtypeStruct((h, s, p), F32), jax.ShapeDtypeStruct((h, s, p), F32),
                   jax.ShapeDtypeStruct((h, s, p), F32), jax.ShapeDtypeStruct((h, 1, s), F32)],
        scratch_shapes=[pltpu.VMEM((t, p), F32), pltpu.VMEM((t, p), F32), pltpu.VMEM((1, t), F32)],
        compiler_params=_cparams(("parallel", "arbitrary", "arbitrary")),
    )(q, k, v, c, c, o, lse, do)


AUX = 64


def _pack(main, cols):
    h, s, p = main.shape
    parts = [main.astype(BF16)]
    if cols:
        parts.append(jnp.stack(cols, axis=-1).astype(BF16))
    parts.append(jnp.zeros((h, s, LANES - p - len(cols)), BF16))
    return jnp.concatenate(parts, axis=-1)


def _terms(v):
    hi = lax.reduce_precision(v, 8, 7)
    mid = lax.reduce_precision(v - hi, 8, 7)
    lo = lax.reduce_precision(v - hi - mid, 8, 7)
    return [hi, mid, lo]


def _fox_pack_qkv(q, k, v):
    h, s, _ = q.shape
    one = jnp.ones((h, s), F32)
    return _pack(q * (FOX_HEADDIM ** -0.5), []), _pack(k, []), _pack(v, [one, one, one])


def _fox_bias(c_ref, qblock, kblock, t):
    lane = lax.broadcasted_iota(jnp.int32, (1, LANES), 1)
    cq = c_ref[0, :, pl.ds(pl.multiple_of(qblock * t, LANES), LANES)]
    cref = jnp.sum(jnp.where(lane == 0, cq, 0.0), axis=1, keepdims=True)
    return cref - c_ref[0, :, pl.ds(pl.multiple_of(kblock * t, LANES), t)]


def _fox_rowdot(do, o, *, tm=256, name):
    s, d = do.shape
    tm = min(tm, s)

    def body(do_ref, o_ref, d_ref):
        prod = do_ref[...].astype(BF16).astype(F32) * o_ref[...]
        r = lax.broadcasted_iota(jnp.int32, (d, LANES), 0)
        c = lax.broadcasted_iota(jnp.int32, (d, LANES), 1)
        mine = (r >= c * FOX_HEADDIM) & (r < (c + 1) * FOX_HEADDIM)
        d_ref[...] = _mask_dot(mine.astype(BF16), prod, "vn")

    row = pl.BlockSpec((tm, d), lambda i: (i, 0))
    return pl.pallas_call(body, name=name, grid=(s // tm,), in_specs=[row, row],
                          out_specs=pl.BlockSpec((tm, LANES), lambda i: (i, 0)),
                          out_shape=jax.ShapeDtypeStruct((s, LANES), F32),
                          compiler_params=_cparams(("parallel",)))(do, o)


def _causal(t):
    return lax.broadcasted_iota(jnp.int32, (t, t), 0) >= lax.broadcasted_iota(jnp.int32, (t, t), 1)


def _fox2_fwd(qp, kp, vp, c, *, name):
    h, s, _ = qp.shape
    t = min(FOX_BLOCK, s)
    nb = s // t
    nt = (((1,), (1,)), ((), ()))
    nn = (((1,), (0,)), ((), ()))

    def body(q_ref, k_ref, v_ref, c_ref, o_ref, lse_ref, m_sc, acc_sc):
        i = pl.program_id(1)
        m_sc[...] = jnp.full_like(m_sc, NEG)
        acc_sc[...] = jnp.zeros_like(acc_sc)
        qv = q_ref[0]

        def step(j, masked):
            at = pl.ds(pl.multiple_of(j * t, t), t)
            kv, vv = k_ref[0, at, :], v_ref[0, at, :]
            sc = lax.dot_general(qv, kv, nt, preferred_element_type=F32) + _fox_bias(c_ref, i, j, t)
            if masked:
                sc = jnp.where(_causal(t), sc, NEG)
            m_prev = m_sc[...]
            m_new = jnp.maximum(m_prev, jnp.max(sc, axis=1, keepdims=True))
            pr = jnp.exp(sc - jnp.tile(m_new, (1, t // LANES)))
            pr_hi = pr.astype(BF16)
            pr_lo = (pr - pr_hi.astype(F32)).astype(BF16)
            pv = (lax.dot_general(pr_hi, vv, nn, preferred_element_type=F32)
                  + lax.dot_general(pr_lo, vv, nn, preferred_element_type=F32))
            acc_sc[...] = jnp.exp(m_prev - m_new) * acc_sc[...] + pv
            m_sc[...] = m_new

        lax.fori_loop(0, i, lambda j, carry: (step(j, False), carry)[1], 0)
        step(i, True)
        acc = acc_sc[...]
        lane = lax.broadcasted_iota(jnp.int32, (1, LANES), 1)
        den = jnp.sum(jnp.where(lane == AUX, acc, 0.0), axis=1, keepdims=True)
        o_ref[0] = (acc / den)[:, :FOX_HEADDIM]
        lse_ref[0] = m_sc[...] + jnp.log(den)

    whole = pl.BlockSpec((1, s, LANES), lambda hh, i: (hh, 0, 0))
    return pl.pallas_call(
        body, name=name, grid=(h, nb),
        in_specs=[pl.BlockSpec((1, t, LANES), lambda hh, i: (hh, i, 0)), whole, whole,
                  pl.BlockSpec((1, 1, s), lambda hh, i: (hh, 0, 0))],
        out_specs=[pl.BlockSpec((1, t, FOX_HEADDIM), lambda hh, i: (hh, i, 0)),
                   pl.BlockSpec((1, t, LANES), lambda hh, i: (hh, i, 0))],
        out_shape=[jax.ShapeDtypeStruct((h, s, FOX_HEADDIM), F32), jax.ShapeDtypeStruct((h, s, LANES), F32)],
        scratch_shapes=[pltpu.VMEM((t, LANES), F32), pltpu.VMEM((t, LANES), F32)],
        compiler_params=_cparams(("parallel", "arbitrary")),
    )(qp, kp, vp, c)


def _fox2_bwd(qp, kp, vp, c, dop, lse, *, name):
    h, s, _ = qp.shape
    t = min(FOX_BLOCK, s)
    nb = s // t
    nt = (((1,), (1,)), ((), ()))
    nn = (((1,), (0,)), ((), ()))
    tn = (((0,), (0,)), ((), ()))

    def body(k_ref, v_ref, q_ref, c_ref, do_ref, lse_ref, dq_ref, dk_ref, dv_ref, dc_ref, dk_sc, dv_sc, dc_sc):
        j = pl.program_id(1)

        @pl.when(j == 0)
        def _():
            dq_ref[...] = jnp.zeros_like(dq_ref)

        dk_sc[...] = jnp.zeros_like(dk_sc)
        dv_sc[...] = jnp.zeros_like(dv_sc)
        dc_sc[...] = jnp.zeros_like(dc_sc)
        kv, vv = k_ref[0], v_ref[0]

        def step(i, masked):
            at = pl.ds(pl.multiple_of(i * t, t), t)
            qv, dov = q_ref[0, at, :], do_ref[0, at, :]
            sc = lax.dot_general(qv, kv, nt, preferred_element_type=F32) + _fox_bias(c_ref, i, j, t)
            pr = jnp.exp(sc - jnp.tile(lse_ref[0, at, :], (1, t // LANES)))
            if masked:
                pr = jnp.where(_causal(t), pr, 0.0)
            ds = pr * lax.dot_general(dov, vv, nt, preferred_element_type=F32)
            dc_sc[...] -= jnp.sum(ds, axis=0, keepdims=True)
            dsb = ds.astype(BF16)
            dv_sc[...] += lax.dot_general(pr.astype(BF16), dov, tn, preferred_element_type=F32)
            dk_sc[...] += lax.dot_general(dsb, qv, tn, preferred_element_type=F32)
            dq_ref[0, at, :] += lax.dot_general(dsb, kv, nn, preferred_element_type=F32)

        step(j, True)
        lax.fori_loop(j + 1, nb, lambda i, carry: (step(i, False), carry)[1], 0)
        dk_ref[0] = dk_sc[...]
        dv_ref[0] = dv_sc[...].astype(dv_ref.dtype)
        dc_ref[0] = dc_sc[...]

    whole = pl.BlockSpec((1, s, LANES), lambda hh, j: (hh, 0, 0))
    blk = pl.BlockSpec((1, t, LANES), lambda hh, j: (hh, j, 0))
    return pl.pallas_call(
        body, name=name, grid=(h, nb),
        in_specs=[blk, blk, whole, pl.BlockSpec((1, 1, s), lambda hh, j: (hh, 0, 0)), whole, whole],
        out_specs=[whole, blk, blk, pl.BlockSpec((1, 1, t), lambda hh, j: (hh, 0, j))],
        out_shape=[jax.ShapeDtypeStruct((h, s, LANES), F32), jax.ShapeDtypeStruct((h, s, LANES), F32),
                   jax.ShapeDtypeStruct((h, s, LANES), BF16), jax.ShapeDtypeStruct((h, 1, s), F32)],
        scratch_shapes=[pltpu.VMEM((t, LANES), F32), pltpu.VMEM((t, LANES), F32), pltpu.VMEM((1, t), F32)],
        compiler_params=_cparams(("parallel", "arbitrary")),
    )(kp, vp, qp, c, dop, lse)


PAIRS = FOX_HEADS // 2
HALF = LANES // 2


def _first_half():
    return lax.broadcasted_iota(jnp.int32, (1, LANES), 1) < HALF


def _pair_bias(c_ref, hh, qblock, kblock, t):
    lane = lax.broadcasted_iota(jnp.int32, (1, LANES), 1)
    cq = c_ref[hh, :, pl.ds(pl.multiple_of(qblock * t, LANES), LANES)]
    cref = jnp.sum(jnp.where(lane == 0, cq, 0.0), axis=1, keepdims=True)
    return cref - c_ref[hh, :, pl.ds(pl.multiple_of(kblock * t, LANES), t)]


def _fox_dopack(do, o, *, tm=256, name):
    s, d = do.shape
    tm = min(tm, s)

    def body(do_ref, o_ref, out_ref):
        dov = do_ref[...].astype(BF16)
        prod = dov.astype(F32) * o_ref[...]
        r = lax.broadcasted_iota(jnp.int32, (d, LANES), 0)
        c = lax.broadcasted_iota(jnp.int32, (d, LANES), 1)
        heads = ((r >= c * FOX_HEADDIM) & (r < (c + 1) * FOX_HEADDIM)).astype(BF16)
        negd = -_mask_dot(heads, prod, "vn")
        hr = lax.broadcasted_iota(jnp.int32, (LANES, 2 * d), 0)
        col = lax.broadcasted_iota(jnp.int32, (LANES, 2 * d), 1)
        base = (hr >> 1) * (2 * LANES) + jnp.where((hr & 1) == 0, HALF, LANES)
        terms = None
        for kk, part in enumerate(_split3(negd)):
            place = ((col == base + kk) & (hr < FOX_HEADS)).astype(BF16)
            tk = lax.dot_general(part, place, (_DIMS["nn"], ((), ())), preferred_element_type=F32)
            terms = tk if terms is None else terms + tk
        first = _first_half()
        zero = jnp.zeros((tm, LANES), BF16)
        pieces = []
        for hp in range(PAIRS):
            blk = dov[:, hp * LANES:(hp + 1) * LANES]
            pieces += [jnp.where(first, blk, zero), jnp.where(first, zero, blk)]
        out_ref[...] = (jnp.concatenate(pieces, axis=1).astype(F32) + terms).astype(BF16)

    row = pl.BlockSpec((tm, d), lambda i: (i, 0))
    return pl.pallas_call(body, name=name, grid=(s // tm,), in_specs=[row, row],
                          out_specs=pl.BlockSpec((tm, 2 * d), lambda i: (i, 0)),
                          out_shape=jax.ShapeDtypeStruct((s, 2 * d), BF16),
                          compiler_params=_cparams(("parallel",)))(do, o)


def _fox3_fwd(qkv, c, *, carry=None, name):
    s = qkv.shape[0]
    t = min(FOX_BLOCK, s)
    nb = s // t
    nt = (((1,), (1,)), ((), ()))
    nn = (((1,), (0,)), ((), ()))
    scale = FOX_HEADDIM ** -0.5

    def body(*refs):
        (q_ref, k_ref, v_ref, c_ref), (o_ref, lse_ref), (m_sc, acc_sc), comm = _carried(carry, refs, 4, 2, 2)
        i = pl.program_id(1)
        if carry is not None:
            @pl.when((pl.program_id(0) == 0) & (i == 0))
            def _():
                carry.start(*comm)

        first = _first_half()
        m_sc[...] = jnp.full_like(m_sc, NEG)
        acc_sc[...] = jnp.zeros_like(acc_sc)
        q2 = q_ref[...] * scale
        zero = jnp.zeros_like(q2)
        qs = [jnp.where(first, q2, zero), jnp.where(first, zero, q2)]

        def step(j, masked):
            at = pl.ds(pl.multiple_of(j * t, t), t)
            k2, v2 = k_ref[at, :], v_ref[at, :]
            one = jnp.ones_like(v2)
            vx = [jnp.where(first, v2, one), jnp.where(first, one, v2)]
            for hh in range(2):
                sc = lax.dot_general(qs[hh], k2, nt, preferred_element_type=F32) + _pair_bias(c_ref, hh, i, j, t)
                if masked:
                    sc = jnp.where(_causal(t), sc, NEG)
                m_prev = m_sc[hh]
                m_new = jnp.maximum(m_prev, jnp.max(sc, axis=1, keepdims=True))
                pr = jnp.exp(sc - jnp.tile(m_new, (1, t // LANES)))
                pr_hi = pr.astype(BF16)
                pr_lo = (pr - pr_hi.astype(F32)).astype(BF16)
                pv = (lax.dot_general(pr_hi, vx[hh], nn, preferred_element_type=F32)
                      + lax.dot_general(pr_lo, vx[hh], nn, preferred_element_type=F32))
                acc_sc[hh] = jnp.exp(m_prev - m_new) * acc_sc[hh] + pv
                m_sc[hh] = m_new

        lax.fori_loop(0, i, lambda j, carry: (step(j, False), carry)[1], 0)
        step(i, True)
        acc_a, acc_b = acc_sc[0], acc_sc[1]
        den_a = jnp.where(first, pltpu.roll(acc_a, HALF, 1), acc_a)
        den_b = jnp.where(first, acc_b, pltpu.roll(acc_b, HALF, 1))
        o_ref[...] = jnp.where(first, acc_a / den_a, acc_b / den_b)
        lse_ref[:, :LANES] = m_sc[0] + jnp.log(den_a)
        lse_ref[:, LANES:] = m_sc[1] + jnp.log(den_b)
        if carry is not None:
            @pl.when((pl.program_id(0) == PAIRS - 1) & (i == nb - 1))
            def _():
                carry.wait(*comm)

    nq = D_FOX // LANES
    extra = carry if carry is not None else _Exchange([], "c", False)
    return pl.pallas_call(
        body, name=name, grid=(PAIRS, nb),
        in_specs=[pl.BlockSpec((t, LANES), lambda hp, i: (i, hp)),
                  pl.BlockSpec((s, LANES), lambda hp, i: (0, nq + hp)),
                  pl.BlockSpec((s, LANES), lambda hp, i: (0, 2 * nq + hp)),
                  pl.BlockSpec((2, 1, s), lambda hp, i: (hp, 0, 0))] + extra.in_specs,
        out_specs=[pl.BlockSpec((t, LANES), lambda hp, i: (i, hp)),
                   pl.BlockSpec((t, 2 * LANES), lambda hp, i: (i, hp))] + extra.out_specs,
        out_shape=[jax.ShapeDtypeStruct((s, D_FOX), F32), jax.ShapeDtypeStruct((s, 2 * D_FOX), F32)] + extra.out_shape,
        scratch_shapes=[pltpu.VMEM((2, t, LANES), F32), pltpu.VMEM((2, t, LANES), F32)]
        + (carry.scratch if carry is not None else []),
        compiler_params=_cparams(("arbitrary", "arbitrary")),
    )(qkv, qkv, qkv, c, *extra.arrs)


def _fox3_bwd(qkv, c, dox, lse, *, carry=None, name):
    s = qkv.shape[0]
    t = min(FOX_BLOCK, s)
    nb = s // t
    nt = (((1,), (1,)), ((), ()))
    nn = (((1,), (0,)), ((), ()))
    tn = (((0,), (0,)), ((), ()))
    scale = FOX_HEADDIM ** -0.5

    def body(*refs):
        ins, outs, scratch, comm = _carried(carry, refs, 6, 4, 3)
        k_ref, v_ref, q_ref, c_ref, do_ref, lse_ref = ins
        dq_ref, dk_ref, dv_ref, dc_ref = outs
        dk_sc, dv_sc, dc_sc = scratch
        j = pl.program_id(1)
        if carry is not None:
            @pl.when((pl.program_id(0) == 0) & (j == 0))
            def _():
                carry.start(*comm)

        first = _first_half()
        halves = [first, jnp.logical_not(first)]

        @pl.when(j == 0)
        def _():
            dq_ref[...] = jnp.zeros_like(dq_ref)

        dk_sc[...] = jnp.zeros_like(dk_sc)
        dv_sc[...] = jnp.zeros_like(dv_sc)
        dc_sc[...] = jnp.zeros_like(dc_sc)
        k2, v2 = k_ref[...], v_ref[...]
        one = jnp.ones_like(v2)
        vx = [jnp.where(first, v2, one), jnp.where(first, one, v2)]

        def step(i, masked):
            at = pl.ds(pl.multiple_of(i * t, t), t)
            q2 = q_ref[at, :] * scale
            zero = jnp.zeros_like(q2)
            qs = [jnp.where(first, q2, zero), jnp.where(first, zero, q2)]
            for hh in range(2):
                dov = do_ref[at, hh * LANES:(hh + 1) * LANES]
                sc = lax.dot_general(qs[hh], k2, nt, preferred_element_type=F32) + _pair_bias(c_ref, hh, i, j, t)
                pr = jnp.exp(sc - jnp.tile(lse_ref[at, hh * LANES:(hh + 1) * LANES], (1, t // LANES)))
                if masked:
                    pr = jnp.where(_causal(t), pr, 0.0)
                ds = pr * lax.dot_general(dov, vx[hh], nt, preferred_element_type=F32)
                dc_sc[hh] -= jnp.sum(ds, axis=0, keepdims=True)
                dsb = ds.astype(BF16)
                dvh = lax.dot_general(pr.astype(BF16), dov, tn, preferred_element_type=F32)
                dv_sc[...] += jnp.where(halves[hh], dvh, 0.0)
                dk_sc[...] += lax.dot_general(dsb, qs[hh], tn, preferred_element_type=F32)
                dqh = lax.dot_general(dsb, k2, nn, preferred_element_type=F32)
                dq_ref[at, :] += jnp.where(halves[hh], dqh, 0.0)

        step(j, True)
        lax.fori_loop(j + 1, nb, lambda i, carry: (step(i, False), carry)[1], 0)
        dk_ref[...] = dk_sc[...]
        dv_ref[...] = dv_sc[...].astype(dv_ref.dtype)
        dc_ref[...] = dc_sc[...]
        if carry is not None:
            @pl.when((pl.program_id(0) == PAIRS - 1) & (j == nb - 1))
            def _():
                carry.wait(*comm)

    nq = D_FOX // LANES
    blk = pl.BlockSpec((t, LANES), lambda hp, j: (j, hp))
    extra = carry if carry is not None else _Exchange([], "c", False)
    return pl.pallas_call(
        body, name=name, grid=(PAIRS, nb),
        in_specs=[pl.BlockSpec((t, LANES), lambda hp, j: (j, nq + hp)),
                  pl.BlockSpec((t, LANES), lambda hp, j: (j, 2 * nq + hp)),
                  pl.BlockSpec((s, LANES), lambda hp, j: (0, hp)),
                  pl.BlockSpec((2, 1, s), lambda hp, j: (hp, 0, 0)),
                  pl.BlockSpec((s, 2 * LANES), lambda hp, j: (0, hp)),
                  pl.BlockSpec((s, 2 * LANES), lambda hp, j: (0, hp))] + extra.in_specs,
        out_specs=[pl.BlockSpec((s, LANES), lambda hp, j: (0, hp)), blk, blk,
                   pl.BlockSpec((2, 1, t), lambda hp, j: (hp, 0, j))] + extra.out_specs,
        out_shape=[jax.ShapeDtypeStruct((s, D_FOX), F32), jax.ShapeDtypeStruct((s, D_FOX), F32),
                   jax.ShapeDtypeStruct((s, D_FOX), BF16), jax.ShapeDtypeStruct((FOX_HEADS, 1, s), F32)]
        + extra.out_shape,
        scratch_shapes=[pltpu.VMEM((t, LANES), F32), pltpu.VMEM((t, LANES), F32), pltpu.VMEM((2, 1, t), F32)]
        + (carry.scratch if carry is not None else []),
        compiler_params=_cparams(("arbitrary", "arbitrary")),
    )(qkv, qkv, qkv, c, dox, lse, *extra.arrs)


def _final(x1, out1, g, tgt, *, tm=256, name):
    s, d = x1.shape
    tm = min(tm, s)

    def body(x_ref, o_ref, g_ref, t_ref, dx_ref, do_ref, dg_ref, loss_ref):
        i = pl.program_id(0)

        @pl.when(i == 0)
        def _():
            dg_ref[...] = jnp.zeros_like(dg_ref)
            loss_ref[...] = jnp.zeros_like(loss_ref)

        tv = t_ref[...]

        def lossf(xv, ov, gv):
            err = jnp.square(xv + _rms(ov, gv) - tv)
            return 0.5 * jnp.sum(jnp.mean(err, axis=-1, keepdims=True), axis=0, keepdims=True)

        val, vjp = jax.vjp(lossf, x_ref[...], o_ref[...], g_ref[...])
        dx, do, dg = vjp(jnp.ones((1, 1), F32))
        dx_ref[...] = dx
        do_ref[...] = do.astype(do_ref.dtype)
        dg_ref[...] += dg
        loss_ref[...] += val

    row = pl.BlockSpec((tm, d), lambda i: (i, 0))
    par = pl.BlockSpec((1, d), lambda i: (0, 0))
    return pl.pallas_call(
        body, name=name, grid=(s // tm,), in_specs=[row, row, par, row],
        out_specs=[row, row, par, pl.BlockSpec((1, 1), lambda i: (0, 0))],
        out_shape=[jax.ShapeDtypeStruct((s, d), F32), jax.ShapeDtypeStruct((s, d), BF16),
                   jax.ShapeDtypeStruct((1, d), F32), jax.ShapeDtypeStruct((1, 1), F32)],
        compiler_params=_cparams(("arbitrary",)),
    )(x1, out1, g, tgt)


def _row_tile(r):
    return LANES if r % LANES == 0 else r


def _sum_slots(parts, *, out_dtype=F32, name):
    p, r, c = parts.shape
    tr = _row_tile(r)

    def body(p_ref, o_ref):
        acc = p_ref[0].astype(F32)
        for k in range(1, p):
            acc = acc + p_ref[k].astype(F32)
        o_ref[...] = acc.astype(o_ref.dtype)

    return pl.pallas_call(
        body, name=name, grid=(r // tr,),
        in_specs=[pl.BlockSpec((p, tr, c), lambda i: (0, i, 0))],
        out_specs=pl.BlockSpec((tr, c), lambda i: (i, 0)),
        out_shape=jax.ShapeDtypeStruct((r, c), out_dtype),
        compiler_params=_cparams(("parallel",)),
    )(parts)


def _adamw(w, gparts, m, v, *, name):
    r, c = w.shape
    p = gparts.shape[0]
    tr = _row_tile(r)

    def body(w_ref, g_ref, m_ref, v_ref, go_ref, d_ref, mo_ref, vo_ref):
        g = g_ref[0].astype(F32)
        for k in range(1, p):
            g = g + g_ref[k].astype(F32)
        mn = ADAM_B1 * m_ref[...] + (1.0 - ADAM_B1) * g
        vn = ADAM_B2 * v_ref[...] + (1.0 - ADAM_B2) * jnp.square(g)
        m_hat = mn / (1.0 - ADAM_B1 ** ADAM_STEP)
        v_hat = vn / (1.0 - ADAM_B2 ** ADAM_STEP)
        go_ref[...] = g
        d_ref[...] = -ADAM_LR * (m_hat / (jnp.sqrt(v_hat) + ADAM_EPS) + ADAM_WD * w_ref[...])
        mo_ref[...] = mn
        vo_ref[...] = vn

    spec = pl.BlockSpec((tr, c), lambda i: (i, 0))
    return pl.pallas_call(
        body, name=name, grid=(r // tr,),
        in_specs=[spec, pl.BlockSpec((p, tr, c), lambda i: (0, i, 0)), spec, spec],
        out_specs=[spec] * 4, out_shape=[jax.ShapeDtypeStruct((r, c), F32)] * 4,
        compiler_params=_cparams(("parallel",)),
    )(w, gparts, m, v)


_FLIPS = {
    "xy": [(1, 0, 0), (0, 1, 0), (1, 1, 0)],
    "c": [(0, 0, 1)],
    "xyc": [(fx, fy, fc) for fx in (0, 1) for fy in (0, 1) for fc in (0, 1) if (fx, fy, fc) != (0, 0, 0)],
}


def _slot(mode, px, py, pc):
    return {"xy": 2 * px + py, "c": pc, "xyc": 4 * px + 2 * py + pc}[mode]


class _Exchange:
    def __init__(self, arrs, mode, scatter):
        self.arrs, self.mode, self.scatter = list(arrs), mode, scatter
        self.n = len(self.arrs)
        self.flips = _FLIPS[mode]
        nf = len(self.flips)
        anyspec = pl.BlockSpec(memory_space=pl.ANY)
        self.in_specs = [anyspec] * self.n
        self.out_specs = [anyspec] * self.n
        self.out_shape = [jax.ShapeDtypeStruct((nf + 1,) + (a.shape[1:] if scatter else a.shape), a.dtype)
                          for a in self.arrs]
        self.scratch = [pltpu.SemaphoreType.DMA((self.n * nf,)), pltpu.SemaphoreType.DMA((self.n * nf,)),
                        pltpu.SemaphoreType.DMA((self.n,))]

    def _copies(self, ins, outs, sems, arrivals=True):
        send, recv, loc = sems
        nf = len(self.flips)
        x, y, c = lax.axis_index("x"), lax.axis_index("y"), lax.axis_index("c")
        me = _slot(self.mode, x, y, c)
        peers = [(x ^ fx, y ^ fy, c ^ fc) for (fx, fy, fc) in self.flips]

        def src(a, slot):
            return ins[a].at[slot] if self.scatter else ins[a]

        def copy(a, j, dst_slot):
            return pltpu.make_async_remote_copy(
                src_ref=src(a, _slot(self.mode, *peers[j])), dst_ref=outs[a].at[dst_slot],
                send_sem=send.at[a * nf + j], recv_sem=recv.at[a * nf + j], device_id=peers[j], device_id_type=MESH)

        pairs = [(a, j) for a in range(self.n) for j in range(nf)]
        local = [pltpu.make_async_copy(src(a, me), outs[a].at[me], loc.at[a]) for a in range(self.n)]
        sends = [copy(a, j, me) for a, j in pairs]
        recvs = [copy(a, j, _slot(self.mode, *peers[j])) for a, j in pairs] if arrivals else []
        return local, sends, recvs

    def start(self, ins, outs, sems):
        local, sends, _ = self._copies(ins, outs, sems, arrivals=False)
        for cp in local + sends:
            cp.start()

    def wait(self, ins, outs, sems):
        local, sends, recvs = self._copies(ins, outs, sems)
        for cp in recvs:
            cp.wait_recv()
        for cp in sends:
            cp.wait_send()
        for cp in local:
            cp.wait()


def _carried(carry, refs, n_in, n_out, n_scratch):
    k = carry.n if carry is not None else 0
    ins, refs = refs[:n_in], refs[n_in:]
    cin, refs = refs[:k], refs[k:]
    outs, refs = refs[:n_out], refs[n_out:]
    cout, refs = refs[:k], refs[k:]
    scratch, sems = refs[:n_scratch], refs[n_scratch:]
    return ins, outs, scratch, (cin, cout, sems)


def _exchange(arrs, mode, scatter, *, name):
    ex = _Exchange(arrs, mode, scatter)

    def body(*refs):
        _, _, _, comm = _carried(ex, refs, 0, 0, 0)
        ex.start(*comm)
        ex.wait(*comm)

    return pl.pallas_call(body, name=name, in_specs=ex.in_specs, out_specs=ex.out_specs, out_shape=ex.out_shape,
                          scratch_shapes=ex.scratch)(*ex.arrs)


def _softplus(v):
    return jnp.maximum(v, 0.0) + jnp.log1p(jnp.exp(-jnp.abs(v)))


def _pad_lanes(v):
    r, n = v.shape
    return jnp.pad(v, ((0, 0), (0, -n % LANES)))


def _to_heads(v):
    s = v.shape[0]
    return v.reshape(s, -1, 64).transpose(1, 0, 2)


def _from_heads(v):
    h, s, p = v.shape
    return v.transpose(1, 0, 2).reshape(s, h * p)


def _fn_rms(v, g):
    return (_rms(v, g),)


def _fn_post(xv, ov, g):
    return (xv + _rms(ov, g),)


def _fn_act(xbc, dtp, fp, dtb, fb):
    return _silu(xbc), _softplus(dtp + dtb), -_softplus(-(fp + fb))


def _fn_mix(y, zs, o, zf, g):
    yg = y * _silu(zs)
    sq = yg * yg
    lane = lax.broadcasted_iota(jnp.int32, (1, D_SSD), 1)
    width = D_SSD // SSD_GROUPS
    rstd = jnp.zeros_like(yg)
    for gi in range(SSD_GROUPS):
        msk = ((lane >= gi * width) & (lane < (gi + 1) * width)).astype(F32)
        ms = jnp.sum(sq * msk, axis=1, keepdims=True) / width
        rstd = rstd + lax.rsqrt(ms + EPS) * msk
    return (jnp.concatenate([yg * rstd * g, o * _silu(zf)], axis=1),)


def _fn_glu(val, gate):
    return (val * jax.nn.sigmoid(gate),)


def _fn_ln(hc, z, g, b):
    mu = jnp.mean(hc, axis=-1, keepdims=True)
    xc = hc - mu
    yn = xc * lax.rsqrt(jnp.mean(xc * xc, axis=-1, keepdims=True) + EPS) * g + b
    return (_silu(yn) * _silu(z),)


class _NoComm:
    def odd_weights(self):
        return None

    def got_odd_weights(self, got, w):
        pass

    def early_grads(self, g):
        return None

    def got_early_grads(self, got):
        pass

    def early_sums(self):
        return None

    def got_early_sums(self, got):
        pass


def _local_step(x, tgt, w, comm=None):
    comm = comm or _NoComm()
    s = x.shape[0]
    d = D_MODEL
    tm = 256
    bf = lambda v: v.astype(BF16)
    c1 = lambda arr: _col(arr, 0, arr.shape[1])
    g = {}

    ew = w["e_w_in"]
    w_z, w_xbc = bf(ew[:, 0:2048]), bf(ew[:, 2048:4096])
    w_dt = bf(_pad_lanes(ew[:, 4096:4112]))
    w_qkv = bf(ew[:, 4112:7184])
    w_f = bf(_pad_lanes(ew[:, 7184:7200]))
    w_eo = bf(w["e_w_out"])
    dtb, fgb = _pad_lanes(w["e_dt_bias"]), _pad_lanes(w["e_fgate_b"])
    alog, dsk = _pad_lanes(w["e_a_log"]), _pad_lanes(w["e_d_skip"])

    (u0,) = _rowwise_fwd(_fn_rms, [c1(x)], [c1(w["e_norm_pre"])], [(d, BF16)], tm=tm, name="e_pre")
    z = _mm(u0, w_z, name="e_in_z")
    xbc_raw = _mm(u0, w_xbc, name="e_in_xbc")
    qkv = _mm(u0, w_qkv, out_dtype=BF16, name="e_in_qkv")
    dtp = _mm(u0, w_dt, name="e_in_dt")
    fp = _mm(u0, w_f, name="e_in_f")
    xbc_pre = _conv_fwd(xbc_raw, w["e_conv_w"], w["e_conv_b"], name="e_conv")
    act_rows = [c1(xbc_pre), c1(dtp), c1(fp)]
    act_pars = [c1(dtb), c1(fgb)]
    xbc, dt, lf = _rowwise_fwd(_fn_act, act_rows, act_pars, [(2048, F32), (LANES, F32), (LANES, F32)],
                               tm=tm, name="e_act")
    y, hsave = _ssd2_fwd(dt, xbc, alog, dsk, name="e_ssd")
    csum = _cumsum_lanes(lf[:, :FOX_HEADS].T, reverse=False, name="e_cumsum").reshape(FOX_HEADS, 1, s)
    o, lse, *got = _fox3_fwd(qkv, csum, carry=comm.odd_weights(), name="e_fox")
    comm.got_odd_weights(got, w)
    w_oi, w_oo = bf(w["o_w_in"]), bf(w["o_w_out"])
    mix_rows = [c1(y), _col(z, 0, D_SSD), c1(o), _col(z, 1, D_FOX)]
    mix_pars = [c1(w["e_ssd_norm"])]
    (hmix,) = _rowwise_fwd(_fn_mix, mix_rows, mix_pars, [(2048, BF16)], tm=tm, name="e_mix")
    out0 = _mm(hmix, w_eo, name="e_out")
    post_rows = [c1(x), c1(out0)]
    (x1,) = _rowwise_fwd(_fn_post, post_rows, [c1(w["e_norm_post"])], [(d, F32)], tm=tm, name="e_post")

    (u1,) = _rowwise_fwd(_fn_rms, [c1(x1)], [c1(w["o_norm_pre"])], [(d, BF16)], tm=tm, name="o_pre")
    p1 = _mm(u1, w_oi, name="o_in")
    glu_rows = [_col(p1, 0, D_CONV), _col(p1, 1, D_CONV)]
    (hg,) = _rowwise_fwd(_fn_glu, glu_rows, [], [(D_CONV, F32)], tm=tm, name="o_glu")
    hc = _conv_fwd(hg, w["o_conv_w"], w["o_conv_b"], name="o_conv")
    ln_rows = [c1(hc), _col(p1, 2, D_CONV)]
    ln_pars = [c1(w["o_ln_g"]), c1(w["o_ln_b"])]
    (h2,) = _rowwise_fwd(_fn_ln, ln_rows, ln_pars, [(D_CONV, BF16)], tm=tm, name="o_ln")
    out1 = _mm(h2, w_oo, name="o_out")

    dx2, dout1, g["o_norm_post"], loss = _final(x1, out1, w["o_norm_post"], tgt, name="loss_head")
    dh2 = _mm(dout1, w_oo, tb=True, name="o_out_dx")
    g["o_w_out"] = _mm(h2, dout1, ta=True, name="o_out_dw")
    (dhc, dz1), (g["o_ln_g"], g["o_ln_b"]) = _rowwise_bwd(_fn_ln, ln_rows, ln_pars, [c1(dh2)], [F32, BF16],
                                                         tm=tm, name="o_ln_bwd")
    dhg = _conv_bwd_x(dhc, w["o_conv_w"], name="o_conv_dx")
    g["o_conv_w"], g["o_conv_b"] = _conv_bwd_w(hg, dhc, CONV_WIDTH, name="o_conv_dw")
    (dval, dgate), _ = _rowwise_bwd(_fn_glu, glu_rows, [], [c1(dhg)], [BF16, BF16], tm=tm, name="o_glu_bwd")
    du1 = _mm(dval, w_oi[:, 0:2048], tb=True, name="o_in_dx0")
    du1 = _mm(dgate, w_oi[:, 2048:4096], tb=True, add=du1, name="o_in_dx1")
    du1 = _mm(dz1, w_oi[:, 4096:6144], tb=True, add=du1, name="o_in_dx2")
    g["o_w_in"] = jnp.concatenate([_mm(u1, dval, ta=True, name="o_in_dw0"), _mm(u1, dgate, ta=True, name="o_in_dw1"),
                                   _mm(u1, dz1, ta=True, name="o_in_dw2")], axis=1)
    (dx1,), (g["o_norm_pre"],) = _rowwise_bwd(_fn_rms, [c1(x1)], [c1(w["o_norm_pre"])], [c1(du1)], [F32],
                                              adds={0: c1(dx2)}, tm=tm, name="o_pre_bwd")

    (dout0,), (g["e_norm_post"],) = _rowwise_bwd(_fn_post, post_rows, [c1(w["e_norm_post"])], [c1(dx1)],
                                                 [None, BF16], tm=tm, name="e_post_bwd")
    dhmix = _mm(dout0, w_eo, tb=True, name="e_out_dx")
    g["e_w_out"] = _mm(hmix, dout0, ta=True, name="e_out_dw")
    (dy, dzs, do, dzf), (g["e_ssd_norm"],) = _rowwise_bwd(_fn_mix, mix_rows, mix_pars, [c1(dhmix)],
                                                        [F32, BF16, F32, BF16], tm=tm, name="e_mix_bwd")
    dox = _fox_dopack(do, o, name="e_fox_dopack")
    dq8, dk, dv, dcs, *got = _fox3_bwd(qkv, csum, dox, lse, carry=comm.early_grads(g), name="e_fox_bwd")
    comm.got_early_grads(got)
    dlf = _pad_lanes(_cumsum_lanes(dcs.reshape(FOX_HEADS, s), reverse=True, name="e_cumsum_bwd").T)
    dxs, ddt, dbm, dcm, dalog, ddsk, *got = _ssd2_bwd(dt, xbc, alog, dsk, hsave, dy, carry=comm.early_sums(),
                                                       name="e_ssd_bwd")
    comm.got_early_sums(got)
    dxbc = jnp.concatenate([dxs, dbm, dcm], axis=1)
    (dxbc_pre, ddtp, dfp), (ddtb, dfgb) = _rowwise_bwd(_fn_act, act_rows, act_pars, [c1(dxbc), c1(ddt), c1(dlf)],
                                                      [F32, BF16, BF16], tm=tm, name="e_act_bwd")
    dxbc_raw = bf(_conv_bwd_x(dxbc_pre, w["e_conv_w"], name="e_conv_dx"))
    g["e_conv_w"], g["e_conv_b"] = _conv_bwd_w(xbc_raw, dxbc_pre, SSD_CONV, name="e_conv_dw")
    du0 = _mm(dzs, w_z[:, :D_SSD], tb=True, name="e_in_dx0")
    du0 = _mm(dzf, w_z[:, D_SSD:], tb=True, add=du0, name="e_in_dx1")
    du0 = _mm(dxbc_raw, w_xbc, tb=True, add=du0, name="e_in_dx2")
    eighth = FOX_HEADDIM ** -0.5
    du0 = _mm(dq8, w_qkv[:, :D_FOX] * eighth, tb=True, add=du0, name="e_in_dx3q")
    du0 = _mm(dk, w_qkv[:, D_FOX:2 * D_FOX], tb=True, add=du0, name="e_in_dx3k")
    du0 = _mm(dv, w_qkv[:, 2 * D_FOX:], tb=True, add=du0, name="e_in_dx3v")
    du0 = _mm(ddtp, w_dt, tb=True, add=du0, name="e_in_dx4")
    du0 = _mm(dfp, w_f, tb=True, add=du0, name="e_in_dx5")
    g["e_w_in"] = jnp.concatenate([
        _mm(u0, dzs, ta=True, name="e_in_dw0"), _mm(u0, dzf, ta=True, name="e_in_dw1"),
        _mm(u0, dxbc_raw, ta=True, name="e_in_dw2"), _mm(u0, ddtp, ta=True, name="e_in_dw3")[:, :SSD_HEADS],
        _mm(u0, dq8, ta=True, name="e_in_dw4q") * eighth, _mm(u0, dk, ta=True, name="e_in_dw4k"),
        _mm(u0, dv, ta=True, name="e_in_dw4v"), _mm(u0, dfp, ta=True, name="e_in_dw5")[:, :FOX_HEADS]], axis=1)
    (dx,), (g["e_norm_pre"],) = _rowwise_bwd(_fn_rms, [c1(x)], [c1(w["e_norm_pre"])], [c1(du0)], [F32],
                                             adds={0: c1(dx1)}, tm=tm, name="e_pre_bwd")
    g["e_dt_bias"], g["e_fgate_b"] = ddtb[:, :SSD_HEADS], dfgb[:, :FOX_HEADS]
    g["e_a_log"], g["e_d_skip"] = dalog[:, :SSD_HEADS], ddsk[:, :SSD_HEADS]
    return loss, dx, g


_WEIGHTS = ["e_norm_pre", "e_w_in", "e_conv_w", "e_conv_b", "e_dt_bias", "e_a_log", "e_d_skip", "e_fgate_b",
            "e_ssd_norm", "e_w_out", "e_norm_post", "o_norm_pre", "o_w_in", "o_conv_w", "o_conv_b", "o_ln_g",
            "o_ln_b", "o_w_out", "o_norm_post"]
_BIG = ["e_w_in", "e_w_out", "o_w_in", "o_w_out"]
_ROW_SHARDED = ["e_w_out", "o_w_out"]
_SMALL_SHARDED = ["e_conv_w", "o_norm_pre", "o_conv_w", "o_conv_b", "o_ln_g", "o_ln_b", "o_norm_post"]
_REPLICATED = ["e_norm_pre", "e_conv_b", "e_dt_bias", "e_a_log", "e_d_skip", "e_fgate_b", "e_ssd_norm", "e_norm_post"]
_SMALL = [n for n in _WEIGHTS if n not in _BIG]
_EVEN_SHARDED = ["e_w_in", "e_w_out", "e_conv_w"]
_ODD_SHARDED = ["o_w_in", "o_w_out", "o_norm_pre", "o_conv_w", "o_conv_b", "o_ln_g", "o_ln_b", "o_norm_post"]
_EARLY_GRADS = ["o_w_in", "o_w_out", "e_w_out"]
N_CHIPS = 4


def _join(gathered, rows):
    k, r, c = gathered.shape
    return gathered.reshape(k * r, c) if rows else gathered.transpose(1, 0, 2).reshape(r, k * c)


def _split(full, rows):
    r, c = full.shape
    return full.reshape(N_CHIPS, r // N_CHIPS, c) if rows else full.reshape(r, N_CHIPS, c // N_CHIPS).transpose(1, 0, 2)


def kernel(x, e_norm_pre, e_w_in, e_conv_w, e_conv_b, e_dt_bias, e_a_log, e_d_skip, e_fgate_b, e_ssd_norm, e_w_out, e_norm_post, o_norm_pre, o_w_in, o_conv_w, o_conv_b, o_ln_g, o_ln_b, o_w_out, o_norm_post, loss_target, m_e_norm_pre, m_e_w_in, m_e_conv_w, m_e_conv_b, m_e_dt_bias, m_e_a_log, m_e_d_skip, m_e_fgate_b, m_e_ssd_norm, m_e_w_out, m_e_norm_post, m_o_norm_pre, m_o_w_in, m_o_conv_w, m_o_conv_b, m_o_ln_g, m_o_ln_b, m_o_w_out, m_o_norm_post, v_e_norm_pre, v_e_w_in, v_e_conv_w, v_e_conv_b, v_e_dt_bias, v_e_a_log, v_e_d_skip, v_e_fgate_b, v_e_ssd_norm, v_e_w_out, v_e_norm_post, v_o_norm_pre, v_o_w_in, v_o_conv_w, v_o_conv_b, v_o_ln_g, v_o_ln_b, v_o_w_out, v_o_norm_post):
    wvals = (e_norm_pre, e_w_in, e_conv_w, e_conv_b, e_dt_bias, e_a_log, e_d_skip, e_fgate_b, e_ssd_norm, e_w_out,
             e_norm_post, o_norm_pre, o_w_in, o_conv_w, o_conv_b, o_ln_g, o_ln_b, o_w_out, o_norm_post)
    mvals = (m_e_norm_pre, m_e_w_in, m_e_conv_w, m_e_conv_b, m_e_dt_bias, m_e_a_log, m_e_d_skip, m_e_fgate_b,
             m_e_ssd_norm, m_e_w_out, m_e_norm_post, m_o_norm_pre, m_o_w_in, m_o_conv_w, m_o_conv_b, m_o_ln_g,
             m_o_ln_b, m_o_w_out, m_o_norm_post)
    vvals = (v_e_norm_pre, v_e_w_in, v_e_conv_w, v_e_conv_b, v_e_dt_bias, v_e_a_log, v_e_d_skip, v_e_fgate_b,
             v_e_ssd_norm, v_e_w_out, v_e_norm_post, v_o_norm_pre, v_o_w_in, v_o_conv_w, v_o_conv_b, v_o_ln_g,
             v_o_ln_b, v_o_w_out, v_o_norm_post)

    def mat(v):
        return v.reshape(v.shape[-2:]) if v.ndim == 3 else v

    w = {n: mat(v) for n, v in zip(_WEIGHTS, wvals)}
    m = {n: mat(v) for n, v in zip(_WEIGHTS, mvals)}
    v2 = {n: mat(v) for n, v in zip(_WEIGHTS, vvals)}
    me_xy = 2 * lax.axis_index("x") + lax.axis_index("y")

    def shard(n):
        return w[n].astype(BF16) if n in _BIG else w[n]

    gathered = _exchange([shard(n) for n in _EVEN_SHARDED], "xy", False, name="gather_weights")
    full = {n: w[n] for n in _REPLICATED}
    for n, gth in zip(_EVEN_SHARDED, gathered):
        full[n] = _join(gth, n in _ROW_SHARDED)
    gparts = {}

    class _StepComm(_NoComm):
        def odd_weights(self):
            return _Exchange([shard(n) for n in _ODD_SHARDED], "xy", False)

        def got_odd_weights(self, got, wdict):
            for n, gth in zip(_ODD_SHARDED, got):
                wdict[n] = _join(gth, n in _ROW_SHARDED)

        def early_grads(self, g):
            return _Exchange([_split(g[n], n in _ROW_SHARDED).astype(BF16) for n in _EARLY_GRADS], "xy", True)

        def got_early_grads(self, got):
            self.sums = [_sum_slots(p, out_dtype=BF16, name="sum_" + n) for n, p in zip(_EARLY_GRADS, got)]

        def early_sums(self):
            return _Exchange(self.sums, "c", False)

        def got_early_sums(self, got):
            gparts.update(zip(_EARLY_GRADS, got))

    loss, dx, g = _local_step(x[0], loss_target[0], full, _StepComm())
    loss = lax.psum(loss[0, 0], ("x", "y", "c"))

    (scattered,) = _exchange([_split(g["e_w_in"], False).astype(BF16)], "xy", True, name="scatter_grads")
    (gparts["e_w_in"],) = _exchange([_sum_slots(scattered, out_dtype=BF16, name="sum_e_w_in")], "c", False,
                                    name="pair_grads")

    flat = jnp.concatenate([_pad_lanes(g[n].reshape(1, -1)) for n in _SMALL], axis=1).reshape(-1, LANES)
    (all8,) = _exchange([flat], "xyc", False, name="gather_small_grads")
    total = _sum_slots(all8, name="sum_small").reshape(1, -1)
    at = 0
    for n in _SMALL:
        size = g[n].size
        gn = total[:, at:at + size].reshape(g[n].shape)
        at += size + (-size % LANES)
        if n in _SMALL_SHARDED:
            cols = gn.shape[1] // N_CHIPS
            gn = lax.dynamic_slice(gn, (0, me_xy * cols), (gn.shape[0], cols))
        gparts[n] = gn[None]

    grads, deltas, new_m, new_v = [], [], [], []
    for n, orig in zip(_WEIGHTS, wvals):
        gn, dn, mn, vn = _adamw(w[n], gparts[n], m[n], v2[n], name="adamw_" + n)
        for lst, val in zip((grads, deltas, new_m, new_v), (gn, dn, mn, vn)):
            lst.append(val.reshape(orig.shape))
    return (loss, dx[None], *grads, *deltas, *new_m, *new_v)
```

```python
import functools

import jax
import jax.numpy as jnp
from jax import lax
from jax.experimental import pallas as pl
from jax.experimental.pallas import tpu as pltpu

F32 = jnp.float32
BF16 = jnp.bfloat16
MESH = pl.DeviceIdType.MESH

D_MODEL = 1024
D_SSD = 1024
SSD_HEADS = 16
SSD_HEADDIM = 64
SSD_GROUPS = 4
SSD_HPG = 4
D_STATE = 128
SSD_CONV = 4
CHUNK = 128
D_FOX = 1024
FOX_HEADS = 16
FOX_HEADDIM = 64
D_CONV = 2048
CONV_WIDTH = 31
EPS = 1e-6
LANES = 128
VMEM_LIMIT = 56 * 1024 * 1024

ADAM_LR = 0.001
ADAM_B1 = 0.9
ADAM_B2 = 0.999
ADAM_EPS = 1e-08
ADAM_WD = 0.01
ADAM_STEP = 10


def _cparams(sem=None):
    return pltpu.CompilerParams(dimension_semantics=sem, vmem_limit_bytes=VMEM_LIMIT)


def _mm(a, b, *, ta=False, tb=False, add=None, out_dtype=F32, tm=1024, tn=None, tk=2048, name):
    m = a.shape[1] if ta else a.shape[0]
    k = a.shape[0] if ta else a.shape[1]
    n = b.shape[0] if tb else b.shape[1]
    if tn is None:
        tn = 1024 if ta else 512
    tm, tn = min(tm, m), min(tn, n)
    tk = max(t for t in range(LANES, min(tk, k) + 1, LANES) if k % t == 0)
    assert m % tm == 0 and n % tn == 0 and k % tk == 0, (m, n, k, tm, tn, tk)
    nk = k // tk
    dims = (((0 if ta else 1,), (1 if tb else 0,)), ((), ()))

    def body(*refs):
        if add is None:
            a_ref, b_ref, o_ref = refs[:3]
            c_ref = None
        else:
            a_ref, b_ref, c_ref, o_ref = refs[:4]
        kk = pl.program_id(2)
        prod = lax.dot_general(a_ref[...].astype(BF16), b_ref[...].astype(BF16), dims, preferred_element_type=F32)
        if nk == 1:
            o_ref[...] = (prod if c_ref is None else prod + c_ref[...].astype(F32)).astype(o_ref.dtype)
            return
        acc_ref = refs[-1]

        @pl.when(kk == 0)
        def _():
            acc_ref[...] = prod if c_ref is None else prod + c_ref[...].astype(F32)

        @pl.when((kk > 0) & (kk < nk - 1))
        def _():
            acc_ref[...] += prod

        @pl.when(kk == nk - 1)
        def _():
            o_ref[...] = (acc_ref[...] + prod).astype(o_ref.dtype)

    a_spec = (pl.BlockSpec((tk, tm), lambda i, j, kk: (kk, i)) if ta
              else pl.BlockSpec((tm, tk), lambda i, j, kk: (i, kk)))
    b_spec = (pl.BlockSpec((tn, tk), lambda i, j, kk: (j, kk)) if tb
              else pl.BlockSpec((tk, tn), lambda i, j, kk: (kk, j)))
    o_spec = pl.BlockSpec((tm, tn), lambda i, j, kk: (i, j))
    in_specs, args = [a_spec, b_spec], [a, b]
    if add is not None:
        in_specs.append(o_spec)
        args.append(add)
    return pl.pallas_call(
        body, name=name, grid=(m // tm, n // tn, nk),
        in_specs=in_specs, out_specs=o_spec,
        out_shape=jax.ShapeDtypeStruct((m, n), out_dtype),
        scratch_shapes=[pltpu.VMEM((tm, tn), F32)] if nk > 1 else [],
        compiler_params=_cparams(("parallel", "parallel", "arbitrary")),
    )(*args)


def _col(arr, cb, width):
    return (arr, cb, width)


def _row_specs(ops, tm):
    return [pl.BlockSpec((tm, w), lambda i, cb=cb: (i, cb)) for (_, cb, w) in ops]


def _par_specs(ops):
    return [pl.BlockSpec((a.shape[0], w), lambda i, cb=cb: (0, cb)) for (a, cb, w) in ops]


def _rowwise_fwd(fn, rows, params, outs, *, tm, name):
    s = rows[0][0].shape[0]
    tm = min(tm, s)
    nr, npar = len(rows), len(params)

    def body(*refs):
        rv = [r[...].astype(F32) for r in refs[:nr]]
        pv = [p[...].astype(F32) for p in refs[nr:nr + npar]]
        res = fn(*rv, *pv)
        for o_ref, val in zip(refs[nr + npar:], res):
            o_ref[...] = val.astype(o_ref.dtype)

    return pl.pallas_call(
        body, name=name, grid=(s // tm,),
        in_specs=_row_specs(rows, tm) + _par_specs(params),
        out_specs=[pl.BlockSpec((tm, w), lambda i: (i, 0)) for (w, _) in outs],
        out_shape=[jax.ShapeDtypeStruct((s, w), dt) for (w, dt) in outs],
        compiler_params=_cparams(("parallel",)),
    )(*[r[0] for r in rows], *[p[0] for p in params])


def _rowwise_bwd(fn, rows, params, couts, row_grads, *, adds=None, tm, name):
    adds = adds or {}
    s = rows[0][0].shape[0]
    tm = min(tm, s)
    nr, npar, nc = len(rows), len(params), len(couts)
    add_keys = sorted(adds)
    want = [i for i, dt in enumerate(row_grads) if dt is not None]

    def body(*refs):
        i = pl.program_id(0)
        rv = [r[...].astype(F32) for r in refs[:nr]]
        pv = [p[...].astype(F32) for p in refs[nr:nr + npar]]
        cv = [c[...].astype(F32) for c in refs[nr + npar:nr + npar + nc]]
        av = {k: r[...].astype(F32) for k, r in zip(add_keys, refs[nr + npar + nc:nr + npar + nc + len(add_keys)])}
        orefs = refs[nr + npar + nc + len(add_keys):]
        _, vjp = jax.vjp(lambda rr, pp: tuple(fn(*rr, *pp)), rv, pv)
        drows, dpars = vjp(tuple(cv))
        for o_ref, ri in zip(orefs[:len(want)], want):
            g = drows[ri]
            if ri in av:
                g = g + av[ri]
            o_ref[...] = g.astype(o_ref.dtype)

        @pl.when(i == 0)
        def _():
            for o_ref in orefs[len(want):]:
                o_ref[...] = jnp.zeros_like(o_ref)

        for o_ref, g in zip(orefs[len(want):], dpars):
            o_ref[...] += g

    add_ops = [adds[k] for k in add_keys]
    out_specs = ([pl.BlockSpec((tm, rows[ri][2]), lambda i: (i, 0)) for ri in want]
                 + [pl.BlockSpec((p[0].shape[0], p[2]), lambda i: (0, 0)) for p in params])
    out_shape = ([jax.ShapeDtypeStruct((s, rows[ri][2]), row_grads[ri]) for ri in want]
                 + [jax.ShapeDtypeStruct((p[0].shape[0], p[2]), F32) for p in params])
    res = pl.pallas_call(
        body, name=name, grid=(s // tm,),
        in_specs=_row_specs(rows, tm) + _par_specs(params) + _row_specs(couts, tm) + _row_specs(add_ops, tm),
        out_specs=out_specs, out_shape=out_shape,
        compiler_params=_cparams(("arbitrary",)),
    )(*[r[0] for r in rows], *[p[0] for p in params], *[c[0] for c in couts], *[a[0] for a in add_ops])
    return res[:len(want)], res[len(want):]


def _silu(v):
    return v * jax.nn.sigmoid(v)


def _rms(v, g):
    return v * lax.rsqrt(jnp.mean(v * v, axis=-1, keepdims=True) + EPS) * g


SUBLANES = 8
CONV_ROWS = 256


def _halo(shifts):
    up = lambda v: -(-v // SUBLANES) * SUBLANES
    return up(max(0, -min(shifts))), up(max(0, max(shifts)))


def _fill_halo(xp_sc, x_ref, front, back):
    s = x_ref.shape[0]
    if front:
        xp_sc[0:front, :] = jnp.zeros((front, LANES), F32)
    if back:
        xp_sc[front + s:front + s + back, :] = jnp.zeros((back, LANES), F32)
    xp_sc[front:front + s, :] = x_ref[...].astype(F32)


def _shift_conv(x, w, b, shifts, *, name):
    s, c = x.shape
    tr = min(CONV_ROWS, s)
    nk = len(shifts)
    front, back = _halo(shifts)

    def body(*refs):
        if b is None:
            x_ref, w_ref, o_ref, xp_sc = refs
        else:
            x_ref, w_ref, b_ref, o_ref, xp_sc = refs
        _fill_halo(xp_sc, x_ref, front, back)

        def chunk(r, carry):
            base = pl.multiple_of(r * tr, tr)
            acc = jnp.zeros((tr, LANES), F32) if b is None else jnp.broadcast_to(b_ref[...], (tr, LANES))
            for kk in range(nk):
                acc = acc + xp_sc[pl.ds(base + front + shifts[kk], tr), :] * w_ref[kk:kk + 1, :]
            o_ref[pl.ds(base, tr), :] = acc
            return carry

        lax.fori_loop(0, s // tr, chunk, 0)

    strip = pl.BlockSpec((s, LANES), lambda cb: (0, cb))
    in_specs = [strip, pl.BlockSpec((nk, LANES), lambda cb: (0, cb))]
    args = [x, w]
    if b is not None:
        in_specs.append(pl.BlockSpec((1, LANES), lambda cb: (0, cb)))
        args.append(b)
    return pl.pallas_call(
        body, name=name, grid=(c // LANES,), in_specs=in_specs, out_specs=strip,
        out_shape=jax.ShapeDtypeStruct((s, c), F32),
        scratch_shapes=[pltpu.VMEM((front + s + back, LANES), F32)],
        compiler_params=_cparams(("parallel",)),
    )(*args)


def _conv_fwd(x, w, b, *, name):
    k = w.shape[0]
    return _shift_conv(x, w, b, [kk - (k - 1) for kk in range(k)], name=name)


def _conv_bwd_x(dy, w, *, name):
    k = w.shape[0]
    return _shift_conv(dy, w, None, [(k - 1) - kk for kk in range(k)], name=name)


def _conv_bwd_w(x, dy, k, *, name):
    s, c = x.shape
    tr = min(CONV_ROWS, s)
    shifts = [kk - (k - 1) for kk in range(k)]
    front, back = _halo(shifts)

    def fold(v):
        return jnp.sum(v.reshape(tr // SUBLANES, SUBLANES, LANES), axis=0)

    def body(x_ref, dy_ref, dw_ref, db_ref, xp_sc, dw_sc, db_sc):
        _fill_halo(xp_sc, x_ref, front, back)
        dw_sc[...] = jnp.zeros_like(dw_sc)
        db_sc[...] = jnp.zeros_like(db_sc)

        def chunk(r, carry):
            base = pl.multiple_of(r * tr, tr)
            dyv = dy_ref[pl.ds(base, tr), :]
            db_sc[...] += fold(dyv)
            for kk in range(k):
                dw_sc[kk] += fold(xp_sc[pl.ds(base + front + shifts[kk], tr), :] * dyv)
            return carry

        lax.fori_loop(0, s // tr, chunk, 0)
        db_ref[...] = jnp.sum(db_sc[...], axis=0, keepdims=True)
        for kk in range(k):
            dw_ref[kk:kk + 1, :] = jnp.sum(dw_sc[kk], axis=0, keepdims=True)

    strip = pl.BlockSpec((s, LANES), lambda cb: (0, cb))
    return pl.pallas_call(
        body, name=name, grid=(c // LANES,), in_specs=[strip, strip],
        out_specs=[pl.BlockSpec((k, LANES), lambda cb: (0, cb)), pl.BlockSpec((1, LANES), lambda cb: (0, cb))],
        out_shape=[jax.ShapeDtypeStruct((k, c), F32), jax.ShapeDtypeStruct((1, c), F32)],
        scratch_shapes=[pltpu.VMEM((front + s + back, LANES), F32), pltpu.VMEM((k, SUBLANES, LANES), F32),
                        pltpu.VMEM((SUBLANES, LANES), F32)],
        compiler_params=_cparams(("parallel",)),
    )(x, dy)


_DIMS = {"nn": ((1,), (0,)), "nt": ((1,), (1,)), "tn": ((0,), (0,))}


def _bd(a, b, mode):
    return lax.dot_general(a.astype(BF16), b.astype(BF16), (_DIMS[mode], ((), ())), preferred_element_type=F32)


@functools.partial(jax.custom_vjp, nondiff_argnums=(2,))
def _bdot(a, b, mode):
    return _bd(a, b, mode)


def _bdot_fwd(a, b, mode):
    return _bd(a, b, mode), (a, b)


def _bdot_bwd(mode, res, g):
    a, b = res
    if mode == "nn":
        return _bd(g, b, "nt"), _bd(a, g, "tn")
    if mode == "nt":
        return _bd(g, b, "nn"), _bd(g, a, "tn")
    return _bd(b, g, "nt"), _bd(a, g, "nn")


_bdot.defvjp(_bdot_fwd, _bdot_bwd)


def _split3(v):
    hi = v.astype(BF16)
    r1 = v - hi.astype(F32)
    mid = r1.astype(BF16)
    lo = (r1 - mid.astype(F32)).astype(BF16)
    return hi, mid, lo


def _mask_dot(mask01, v, mode):
    out = None
    for part in _split3(v):
        if mode == "vn":
            t = lax.dot_general(part, mask01, (_DIMS["nn"], ((), ())), preferred_element_type=F32)
        else:
            t = lax.dot_general(mask01, part, (_DIMS[mode], ((), ())), preferred_element_type=F32)
        out = t if out is None else out + t
    return out


def _lower_tri(n):
    r = lax.broadcasted_iota(jnp.int32, (n, n), 0)
    c = lax.broadcasted_iota(jnp.int32, (n, n), 1)
    return (r >= c).astype(BF16)


@jax.custom_vjp
def _tri_dot(w):
    return _mask_dot(_lower_tri(w.shape[0]), w, "nn")


def _tri_dot_fwd(w):
    return _tri_dot(w), None


def _tri_dot_bwd(_, g):
    return (_mask_dot(_lower_tri(g.shape[0]), g, "tn"),)


_tri_dot.defvjp(_tri_dot_fwd, _tri_dot_bwd)


def _cumsum_lanes(x, *, reverse, name):
    h, s = x.shape
    n = s // LANES

    def body(x_ref, o_ref):
        r = lax.broadcasted_iota(jnp.int32, (LANES, LANES), 0)
        c = lax.broadcasted_iota(jnp.int32, (LANES, LANES), 1)
        m01 = ((r >= c) if reverse else (r <= c)).astype(BF16)

        def step(t, carry):
            ci = (n - 1 - t) if reverse else t
            at = pl.ds(pl.multiple_of(ci * LANES, LANES), LANES)
            blk = x_ref[:, at]
            o_ref[:, at] = _mask_dot(m01, blk, "vn") + carry
            return carry + jnp.sum(blk, axis=1, keepdims=True)

        lax.fori_loop(0, n, step, jnp.zeros((h, 1), F32))

    return pl.pallas_call(body, name=name, out_shape=jax.ShapeDtypeStruct((h, s), F32),
                          compiler_params=_cparams())(x)


def _ssd_chunk(xs, dt, bm, cm, hin, a, dsk, head0):
    n = CHUNK
    row = lax.broadcasted_iota(jnp.int32, (n, n), 0)
    col = lax.broadcasted_iota(jnp.int32, (n, n), 1)
    lower = row >= col
    ustrict = (row > col).astype(F32)
    lane = lax.broadcasted_iota(jnp.int32, (1, LANES), 1)
    sub = lax.broadcasted_iota(jnp.int32, (n, 1), 0)
    e_first = (sub == 0).astype(F32)
    e_last = (sub == n - 1).astype(F32)
    lane0 = (lane == 0).astype(F32)
    cb = _bdot(cm, bm, "nt")
    da = dt * (-jnp.exp(a))
    ys, houts = [], []
    for r in range(SSD_HPG):
        oh = (lane == head0 + r).astype(F32)
        dt_col = jnp.sum(dt * oh, axis=1, keepdims=True)
        da_col = jnp.sum(da * oh, axis=1, keepdims=True)
        dsk_h = jnp.sum(dsk * oh, axis=1, keepdims=True)
        seg = _tri_dot(da_col * ustrict)
        decay = jnp.where(lower, jnp.exp(seg), 0.0)
        cs_col = jnp.sum(seg * lane0, axis=1, keepdims=True) + jnp.sum(da_col * e_first, axis=0, keepdims=True)
        total = jnp.sum(cs_col * e_last, axis=0, keepdims=True)
        xd = xs[r] * dt_col
        y_diag = _bdot(cb * decay, xd, "nn")
        contrib = _bdot(xd * jnp.exp(total - cs_col), bm, "tn")
        houts.append(hin[r] * jnp.exp(total) + contrib)
        y_off = _bdot(cm, hin[r], "nt") * jnp.exp(cs_col)
        ys.append(y_diag + y_off + xs[r] * dsk_h)
    return ys, houts


def _ssd_specs(nc, rev):
    cc = (lambda c: nc - 1 - c) if rev else (lambda c: c)
    hm = pl.BlockSpec((SSD_HPG, CHUNK, SSD_HEADDIM), lambda c, g: (g, cc(c), 0))
    row = pl.BlockSpec((CHUNK, LANES), lambda c, g: (cc(c), 0))
    bmat = pl.BlockSpec((CHUNK, LANES), lambda c, g: (cc(c), D_SSD // LANES + g))
    cmat = pl.BlockSpec((CHUNK, LANES), lambda c, g: (cc(c), D_SSD // LANES + SSD_GROUPS + g))
    par = pl.BlockSpec((1, LANES), lambda c, g: (0, 0))
    hs = pl.BlockSpec((1, SSD_HPG, SSD_HEADDIM, D_STATE), lambda c, g: (cc(c), g, 0, 0))
    return hm, row, bmat, cmat, par, hs


def _ssd_fwd(xs_hm, dt, xbc, a, dsk, *, name):
    s = xs_hm.shape[1]
    nc = s // CHUNK
    hm, row, bmat, cmat, par, hs = _ssd_specs(nc, False)

    def body(xs_ref, dt_ref, bm_ref, cm_ref, a_ref, dsk_ref, y_ref, hs_ref, h_sc):
        c, g = pl.program_id(0), pl.program_id(1)
        mine = pl.ds(g * SSD_HPG, SSD_HPG)

        @pl.when(c == 0)
        def _():
            h_sc[mine] = jnp.zeros((SSD_HPG, SSD_HEADDIM, D_STATE), F32)

        hin = [h_sc[g * SSD_HPG + r] for r in range(SSD_HPG)]
        ys, houts = _ssd_chunk([xs_ref[r] for r in range(SSD_HPG)], dt_ref[...], bm_ref[...], cm_ref[...],
                               hin, a_ref[...], dsk_ref[...], g * SSD_HPG)
        for r in range(SSD_HPG):
            y_ref[r] = ys[r]
            hs_ref[0, r] = hin[r]
            h_sc[g * SSD_HPG + r] = houts[r]

    return pl.pallas_call(
        body, name=name, grid=(nc, SSD_GROUPS),
        in_specs=[hm, row, bmat, cmat, par, par], out_specs=[hm, hs],
        out_shape=[jax.ShapeDtypeStruct(xs_hm.shape, F32),
                   jax.ShapeDtypeStruct((nc, SSD_HEADS, SSD_HEADDIM, D_STATE), F32)],
        scratch_shapes=[pltpu.VMEM((SSD_HEADS, SSD_HEADDIM, D_STATE), F32)],
        compiler_params=_cparams(("arbitrary", "arbitrary")),
    )(xs_hm, dt, xbc, xbc, a, dsk)


def _ssd_bwd(xs_hm, dt, xbc, a, dsk, hsave, dy_hm, *, name):
    s = xs_hm.shape[1]
    nc = s // CHUNK
    hm, row, bmat, cmat, par, hs = _ssd_specs(nc, True)
    gmat = pl.BlockSpec((CHUNK, LANES), lambda c, g: (nc - 1 - c, g))

    def body(xs_ref, dt_ref, bm_ref, cm_ref, a_ref, dsk_ref, hs_ref, dy_ref,
             dxs_ref, ddt_ref, dbm_ref, dcm_ref, da_ref, ddsk_ref, dh_sc):
        c, g = pl.program_id(0), pl.program_id(1)
        mine = pl.ds(g * SSD_HPG, SSD_HPG)

        @pl.when(c == 0)
        def _():
            dh_sc[mine] = jnp.zeros((SSD_HPG, SSD_HEADDIM, D_STATE), F32)

        @pl.when((c == 0) & (g == 0))
        def _():
            da_ref[...] = jnp.zeros_like(da_ref)
            ddsk_ref[...] = jnp.zeros_like(ddsk_ref)

        @pl.when(g == 0)
        def _():
            ddt_ref[...] = jnp.zeros_like(ddt_ref)

        head0 = g * SSD_HPG
        prim = ([xs_ref[r] for r in range(SSD_HPG)], dt_ref[...], bm_ref[...], cm_ref[...],
                [hs_ref[0, r] for r in range(SSD_HPG)], a_ref[...], dsk_ref[...])
        _, vjp = jax.vjp(lambda *p: _ssd_chunk(*p, head0), *prim)
        cot = ([dy_ref[r] for r in range(SSD_HPG)], [dh_sc[g * SSD_HPG + r] for r in range(SSD_HPG)])
        dxs, ddt, dbm, dcm, dhin, da, ddsk = vjp(cot)
        for r in range(SSD_HPG):
            dxs_ref[r] = dxs[r]
            dh_sc[g * SSD_HPG + r] = dhin[r]
        ddt_ref[...] += ddt
        dbm_ref[...] = dbm
        dcm_ref[...] = dcm
        da_ref[...] += da
        ddsk_ref[...] += ddsk

    return pl.pallas_call(
        body, name=name, grid=(nc, SSD_GROUPS),
        in_specs=[hm, row, bmat, cmat, par, par, hs, hm],
        out_specs=[hm, row, gmat, gmat, par, par],
        out_shape=[jax.ShapeDtypeStruct(xs_hm.shape, F32), jax.ShapeDtypeStruct((s, LANES), F32),
                   jax.ShapeDtypeStruct((s, SSD_GROUPS * D_STATE), F32),
                   jax.ShapeDtypeStruct((s, SSD_GROUPS * D_STATE), F32),
                   jax.ShapeDtypeStruct((1, LANES), F32), jax.ShapeDtypeStruct((1, LANES), F32)],
        scratch_shapes=[pltpu.VMEM((SSD_HEADS, SSD_HEADDIM, D_STATE), F32)],
        compiler_params=_cparams(("arbitrary", "arbitrary")),
    )(xs_hm, dt, xbc, xbc, a, dsk, hsave, dy_hm)


SSD_PAIRS = SSD_HPG // 2


def _ssd2_chunk(xs, dt, bm, cm, hin, a, dsk, head0):
    n = CHUNK
    row = lax.broadcasted_iota(jnp.int32, (n, n), 0)
    col = lax.broadcasted_iota(jnp.int32, (n, n), 1)
    lower = row >= col
    ustrict = (row > col).astype(F32)
    lane = lax.broadcasted_iota(jnp.int32, (1, LANES), 1)
    sub = lax.broadcasted_iota(jnp.int32, (n, 1), 0)
    e_first = (sub == 0).astype(F32)
    e_last = (sub == n - 1).astype(F32)
    lane0 = (lane == 0).astype(F32)
    half_l = [(lane < LANES // 2).astype(F32), (lane >= LANES // 2).astype(F32)]
    half_s = [(sub < LANES // 2).astype(F32), (sub >= LANES // 2).astype(F32)]
    cb = _bdot(cm, bm, "nt")
    da = dt * (-jnp.exp(a))
    ys, houts = [], []
    for pr in range(SSD_PAIRS):
        y = jnp.zeros((n, LANES), F32)
        xdte = jnp.zeros((n, LANES), F32)
        lane_gain = jnp.zeros((n, LANES), F32)
        row_gain = jnp.zeros((LANES, 1), F32)
        for hf in range(2):
            oh = (lane == head0 + 2 * pr + hf).astype(F32)
            dt_col = jnp.sum(dt * oh, axis=1, keepdims=True)
            da_col = jnp.sum(da * oh, axis=1, keepdims=True)
            dsk_h = jnp.sum(dsk * oh, axis=1, keepdims=True)
            seg = _tri_dot(da_col * ustrict)
            decay = jnp.where(lower, jnp.exp(seg), 0.0)
            cs_col = jnp.sum(seg * lane0, axis=1, keepdims=True) + jnp.sum(da_col * e_first, axis=0, keepdims=True)
            total = jnp.sum(cs_col * e_last, axis=0, keepdims=True)
            xh = xs[pr] * half_l[hf]
            xd = xh * dt_col
            y = y + _bdot(cb * decay, xd, "nn") + xh * dsk_h
            xdte = xdte + xd * jnp.exp(total - cs_col)
            lane_gain = lane_gain + jnp.exp(cs_col) * half_l[hf]
            row_gain = row_gain + jnp.exp(total) * half_s[hf]
        houts.append(hin[pr] * row_gain + _bdot(xdte, bm, "tn"))
        ys.append(y + _bdot(cm, hin[pr], "nt") * lane_gain)
    return ys, houts


SSD_STEP = 4


def _ssd2_steps(s):
    per = min(SSD_STEP, s // CHUNK)
    return per, s // (CHUNK * per)


def _ssd2_multi(xs, dt, bm, cm, hin, a, dsk, head0):
    ys = []
    for k in range(len(dt)):
        y, hin = _ssd2_chunk(xs[k], dt[k], bm[k], cm[k], hin, a, dsk, head0)
        ys.append(y)
    return ys, hin


def _ssd2_specs(per, nc, rev):
    cc = (lambda c: nc - 1 - c) if rev else (lambda c: c)
    rows = CHUNK * per
    act = pl.BlockSpec((rows, SSD_PAIRS * LANES), lambda c, g: (cc(c), g))
    row = pl.BlockSpec((rows, LANES), lambda c, g: (cc(c), 0))
    bmat = pl.BlockSpec((rows, LANES), lambda c, g: (cc(c), D_SSD // LANES + g))
    cmat = pl.BlockSpec((rows, LANES), lambda c, g: (cc(c), D_SSD // LANES + SSD_GROUPS + g))
    par = pl.BlockSpec((1, LANES), lambda c, g: (0, 0))
    hs = pl.BlockSpec((1, SSD_PAIRS, LANES, D_STATE), lambda c, g: (cc(c), g, 0, 0))
    return act, row, bmat, cmat, par, hs


def _chunk_rows(ref, k):
    return ref[k * CHUNK:(k + 1) * CHUNK, :]


def _pair_cols(ref, k):
    return [ref[k * CHUNK:(k + 1) * CHUNK, pr * LANES:(pr + 1) * LANES] for pr in range(SSD_PAIRS)]


def _ssd2_fwd(dt, xbc, a, dsk, *, name):
    s = xbc.shape[0]
    per, nc = _ssd2_steps(s)
    act, row, bmat, cmat, par, hs = _ssd2_specs(per, nc, False)

    def body(xs_ref, dt_ref, bm_ref, cm_ref, a_ref, dsk_ref, y_ref, hs_ref, h_sc):
        c, g = pl.program_id(0), pl.program_id(1)

        @pl.when(c == 0)
        def _():
            h_sc[pl.ds(g * SSD_PAIRS, SSD_PAIRS)] = jnp.zeros((SSD_PAIRS, LANES, D_STATE), F32)

        hin = [h_sc[g * SSD_PAIRS + pr] for pr in range(SSD_PAIRS)]
        ks = range(per)
        ys, houts = _ssd2_multi([_pair_cols(xs_ref, k) for k in ks], [_chunk_rows(dt_ref, k) for k in ks],
                                [_chunk_rows(bm_ref, k) for k in ks], [_chunk_rows(cm_ref, k) for k in ks],
                                hin, a_ref[...], dsk_ref[...], g * SSD_HPG)
        for pr in range(SSD_PAIRS):
            for k in ks:
                y_ref[k * CHUNK:(k + 1) * CHUNK, pr * LANES:(pr + 1) * LANES] = ys[k][pr]
            hs_ref[0, pr] = hin[pr]
            h_sc[g * SSD_PAIRS + pr] = houts[pr]

    return pl.pallas_call(
        body, name=name, grid=(nc, SSD_GROUPS),
        in_specs=[act, row, bmat, cmat, par, par], out_specs=[act, hs],
        out_shape=[jax.ShapeDtypeStruct((s, D_SSD), F32),
                   jax.ShapeDtypeStruct((nc, SSD_HEADS // 2, LANES, D_STATE), F32)],
        scratch_shapes=[pltpu.VMEM((SSD_HEADS // 2, LANES, D_STATE), F32)],
        compiler_params=_cparams(("arbitrary", "arbitrary")),
    )(xbc, dt, xbc, xbc, a, dsk)


def _ssd2_bwd(dt, xbc, a, dsk, hsave, dy, *, carry=None, name):
    s = xbc.shape[0]
    per, nc = _ssd2_steps(s)
    act, row, bmat, cmat, par, hs = _ssd2_specs(per, nc, True)
    gmat = pl.BlockSpec((CHUNK * per, LANES), lambda c, g: (nc - 1 - c, g))

    def body(*refs):
        ins, outs, (dh_sc,), comm = _carried(carry, refs, 8, 6, 1)
        xs_ref, dt_ref, bm_ref, cm_ref, a_ref, dsk_ref, hs_ref, dy_ref = ins
        dxs_ref, ddt_ref, dbm_ref, dcm_ref, da_ref, ddsk_ref = outs
        c, g = pl.program_id(0), pl.program_id(1)
        if carry is not None:
            @pl.when((c == 0) & (g == 0))
            def _():
                carry.start(*comm)

        @pl.when(c == 0)
        def _():
            dh_sc[pl.ds(g * SSD_PAIRS, SSD_PAIRS)] = jnp.zeros((SSD_PAIRS, LANES, D_STATE), F32)

        @pl.when((c == 0) & (g == 0))
        def _():
            da_ref[...] = jnp.zeros_like(da_ref)
            ddsk_ref[...] = jnp.zeros_like(ddsk_ref)

        @pl.when(g == 0)
        def _():
            ddt_ref[...] = jnp.zeros_like(ddt_ref)

        head0 = g * SSD_HPG
        ks = range(per)
        prim = ([_pair_cols(xs_ref, k) for k in ks], [_chunk_rows(dt_ref, k) for k in ks],
                [_chunk_rows(bm_ref, k) for k in ks], [_chunk_rows(cm_ref, k) for k in ks],
                [hs_ref[0, pr] for pr in range(SSD_PAIRS)], a_ref[...], dsk_ref[...])
        _, vjp = jax.vjp(lambda *p: _ssd2_multi(*p, head0), *prim)
        cot = ([_pair_cols(dy_ref, k) for k in ks], [dh_sc[g * SSD_PAIRS + pr] for pr in range(SSD_PAIRS)])
        dxs, ddt, dbm, dcm, dhin, da, ddsk = vjp(cot)
        for pr in range(SSD_PAIRS):
            for k in ks:
                dxs_ref[k * CHUNK:(k + 1) * CHUNK, pr * LANES:(pr + 1) * LANES] = dxs[k][pr]
            dh_sc[g * SSD_PAIRS + pr] = dhin[pr]
        for k in ks:
            ddt_ref[k * CHUNK:(k + 1) * CHUNK, :] += ddt[k]
            dbm_ref[k * CHUNK:(k + 1) * CHUNK, :] = dbm[k]
            dcm_ref[k * CHUNK:(k + 1) * CHUNK, :] = dcm[k]
        da_ref[...] += da
        ddsk_ref[...] += ddsk
        if carry is not None:
            @pl.when((c == nc - 1) & (g == SSD_GROUPS - 1))
            def _():
                carry.wait(*comm)

    extra = carry if carry is not None else _Exchange([], "c", False)
    return pl.pallas_call(
        body, name=name, grid=(nc, SSD_GROUPS),
        in_specs=[act, row, bmat, cmat, par, par, hs, act] + extra.in_specs,
        out_specs=[act, row, gmat, gmat, par, par] + extra.out_specs,
        out_shape=[jax.ShapeDtypeStruct((s, D_SSD), F32), jax.ShapeDtypeStruct((s, LANES), F32),
                   jax.ShapeDtypeStruct((s, SSD_GROUPS * D_STATE), F32),
                   jax.ShapeDtypeStruct((s, SSD_GROUPS * D_STATE), F32),
                   jax.ShapeDtypeStruct((1, LANES), F32), jax.ShapeDtypeStruct((1, LANES), F32)] + extra.out_shape,
        scratch_shapes=[pltpu.VMEM((SSD_HEADS // 2, LANES, D_STATE), F32)]
        + (carry.scratch if carry is not None else []),
        compiler_params=_cparams(("arbitrary", "arbitrary")),
    )(xbc, dt, xbc, xbc, a, dsk, hsave, dy, *extra.arrs)


FOX_BLOCK = 512
NEG = -1e30


def _fox_scores(q, k, cref, ck, strictly_below):
    t = q.shape[0]
    s = lax.dot_general(q, k, (_DIMS["nt"], ((), ())), preferred_element_type=F32) * (FOX_HEADDIM ** -0.5)
    s = s + (cref - ck)
    row = lax.broadcasted_iota(jnp.int32, (t, t), 0)
    col = lax.broadcasted_iota(jnp.int32, (t, t), 1)
    mask = (row >= col) | strictly_below
    return s, mask


def _fox_fwd(q, k, v, c, *, name):
    h, s, p = q.shape
    t = min(FOX_BLOCK, s)
    nb = s // t

    def body(q_ref, k_ref, v_ref, cq_ref, ck_ref, o_ref, lse_ref, m_sc, l_sc, acc_sc):
        i, j = pl.program_id(1), pl.program_id(2)

        @pl.when(j == 0)
        def _():
            m_sc[...] = jnp.full_like(m_sc, NEG)
            l_sc[...] = jnp.zeros_like(l_sc)
            acc_sc[...] = jnp.zeros_like(acc_sc)

        @pl.when(j <= i)
        def _():
            sc, mask = _fox_scores(q_ref[0], k_ref[0], cq_ref[0, 0:1, 0:1], ck_ref[0], j < i)
            sc = jnp.where(mask, sc, NEG)
            m_old = m_sc[...]
            m_new = jnp.maximum(m_old, jnp.max(sc, axis=1, keepdims=True))
            alpha = jnp.exp(m_old - m_new)
            pr = jnp.exp(sc - m_new)
            l_sc[...] = alpha * l_sc[...] + jnp.sum(pr, axis=1, keepdims=True)
            pr_hi = pr.astype(BF16)
            pr_lo = (pr - pr_hi.astype(F32)).astype(BF16)
            pv = (lax.dot_general(pr_hi, v_ref[0], (_DIMS["nn"], ((), ())), preferred_element_type=F32)
                  + lax.dot_general(pr_lo, v_ref[0], (_DIMS["nn"], ((), ())), preferred_element_type=F32))
            acc_sc[...] = alpha * acc_sc[...] + pv
            m_sc[...] = m_new

        @pl.when(j == i)
        def _():
            o_ref[0] = acc_sc[...] / l_sc[...]
            lse_ref[0] = jnp.broadcast_to(m_sc[...] + jnp.log(l_sc[...]), (t, LANES))

    qspec = pl.BlockSpec((1, t, p), lambda hh, i, j: (hh, i, 0))
    kspec = pl.BlockSpec((1, t, p), lambda hh, i, j: (hh, jnp.minimum(j, i), 0))
    return pl.pallas_call(
        body, name=name, grid=(h, nb, nb),
        in_specs=[qspec, kspec, kspec,
                  pl.BlockSpec((1, 1, t), lambda hh, i, j: (hh, 0, i)),
                  pl.BlockSpec((1, 1, t), lambda hh, i, j: (hh, 0, jnp.minimum(j, i)))],
        out_specs=[qspec, pl.BlockSpec((1, t, LANES), lambda hh, i, j: (hh, i, 0))],
        out_shape=[jax.ShapeDtypeStruct((h, s, p), F32), jax.ShapeDtypeStruct((h, s, LANES), F32)],
        scratch_shapes=[pltpu.VMEM((t, 1), F32), pltpu.VMEM((t, 1), F32), pltpu.VMEM((t, p), F32)],
        compiler_params=_cparams(("parallel", "arbitrary", "arbitrary")),
    )(q, k, v, c, c)


def _fox_bwd(q, k, v, c, o, lse, do, *, name):
    h, s, p = q.shape
    t = min(FOX_BLOCK, s)
    nb = s // t
    scale = FOX_HEADDIM ** -0.5

    def body(q_ref, k_ref, v_ref, cq_ref, ck_ref, o_ref, lse_ref, do_ref,
             dq_ref, dk_ref, dv_ref, dc_ref, dk_sc, dv_sc, dc_sc):
        j, i = pl.program_id(1), pl.program_id(2)

        @pl.when(i == 0)
        def _():
            dk_sc[...] = jnp.zeros_like(dk_sc)
            dv_sc[...] = jnp.zeros_like(dv_sc)
            dc_sc[...] = jnp.zeros_like(dc_sc)

        @pl.when(i >= j)
        def _():
            qv, kv, vv = q_ref[0], k_ref[0], v_ref[0]
            sc, mask = _fox_scores(qv, kv, cq_ref[0, 0:1, 0:1], ck_ref[0], i > j)
            pr = jnp.where(mask, jnp.exp(sc - lse_ref[0, :, 0:1]), 0.0)
            dov = do_ref[0]
            dob = dov.astype(BF16)
            prb = pr.astype(BF16)
            dv_sc[...] += lax.dot_general(prb, dob, (_DIMS["tn"], ((), ())), preferred_element_type=F32)
            dp = lax.dot_general(dob, vv, (_DIMS["nt"], ((), ())), preferred_element_type=F32)
            dcol = jnp.sum(dob.astype(F32) * o_ref[0], axis=1, keepdims=True)
            ds = pr * (dp - dcol)
            dc_sc[...] -= jnp.sum(ds, axis=0, keepdims=True)
            dsb = ds.astype(BF16)
            dqc = scale * lax.dot_general(dsb, kv, (_DIMS["nn"], ((), ())), preferred_element_type=F32)
            at = pl.ds(pl.multiple_of(i * t, t), t)

            @pl.when(j == 0)
            def _():
                dq_ref[0, at, :] = dqc

            @pl.when(j > 0)
            def _():
                dq_ref[0, at, :] += dqc

            dk_sc[...] += scale * lax.dot_general(dsb, qv, (_DIMS["tn"], ((), ())), preferred_element_type=F32)

        @pl.when(i == nb - 1)
        def _():
            dk_ref[0] = dk_sc[...]
            dv_ref[0] = dv_sc[...]
            dc_ref[0] = dc_sc[...]

    qspec = pl.BlockSpec((1, t, p), lambda hh, j, i: (hh, jnp.maximum(i, j), 0))
    kspec = pl.BlockSpec((1, t, p), lambda hh, j, i: (hh, j, 0))
    cq = pl.BlockSpec((1, 1, t), lambda hh, j, i: (hh, 0, jnp.maximum(i, j)))
    ck = pl.BlockSpec((1, 1, t), lambda hh, j, i: (hh, 0, j))
    return pl.pallas_call(
        body, name=name, grid=(h, nb, nb),
        in_specs=[qspec, kspec, kspec, cq, ck, qspec,
                  pl.BlockSpec((1, t, LANES), lambda hh, j, i: (hh, jnp.maximum(i, j), 0)), qspec],
        out_specs=[pl.BlockSpec((1, s, p), lambda hh, j, i: (hh, 0, 0)), kspec, kspec, ck],
        out_shape=[jax.ShapeDtypeStruct((h, s, p), F32), jax.ShapeDtypeStruct((h, s, p), F32),
                   jax.ShapeDtypeStruct((h, s, p), F32), jax.ShapeDtypeStruct((h, 1, s), F32)],
        scratch_shapes=[pltpu.VMEM((t, p), F32), pltpu.VMEM((t, p), F32), pltpu.VMEM((1, t), F32)],
        compiler_params=_cparams(("parallel", "arbitrary", "arbitrary")),
    )(q, k, v, c, c, o, lse, do)


AUX = 64


def _pack(main, cols):
    h, s, p = main.shape
    parts = [main.astype(BF16)]
    if cols:
        parts.append(jnp.stack(cols, axis=-1).astype(BF16))
    parts.append(jnp.zeros((h, s, LANES - p - len(cols)), BF16))
    return jnp.concatenate(parts, axis=-1)


def _terms(v):
    hi = lax.reduce_precision(v, 8, 7)
    mid = lax.reduce_precision(v - hi, 8, 7)
    lo = lax.reduce_precision(v - hi - mid, 8, 7)
    return [hi, mid, lo]


def _fox_pack_qkv(q, k, v):
    h, s, _ = q.shape
    one = jnp.ones((h, s), F32)
    return _pack(q * (FOX_HEADDIM ** -0.5), []), _pack(k, []), _pack(v, [one, one, one])


def _fox_bias(c_ref, qblock, kblock, t):
    lane = lax.broadcasted_iota(jnp.int32, (1, LANES), 1)
    cq = c_ref[0, :, pl.ds(pl.multiple_of(qblock * t, LANES), LANES)]
    cref = jnp.sum(jnp.where(lane == 0, cq, 0.0), axis=1, keepdims=True)
    return cref - c_ref[0, :, pl.ds(pl.multiple_of(kblock * t, LANES), t)]


def _fox_rowdot(do, o, *, tm=256, name):
    s, d = do.shape
    tm = min(tm, s)

    def body(do_ref, o_ref, d_ref):
        prod = do_ref[...].astype(BF16).astype(F32) * o_ref[...]
        r = lax.broadcasted_iota(jnp.int32, (d, LANES), 0)
        c = lax.broadcasted_iota(jnp.int32, (d, LANES), 1)
        mine = (r >= c * FOX_HEADDIM) & (r < (c + 1) * FOX_HEADDIM)
        d_ref[...] = _mask_dot(mine.astype(BF16), prod, "vn")

    row = pl.BlockSpec((tm, d), lambda i: (i, 0))
    return pl.pallas_call(body, name=name, grid=(s // tm,), in_specs=[row, row],
                          out_specs=pl.BlockSpec((tm, LANES), lambda i: (i, 0)),
                          out_shape=jax.ShapeDtypeStruct((s, LANES), F32),
                          compiler_params=_cparams(("parallel",)))(do, o)


def _causal(t):
    return lax.broadcasted_iota(jnp.int32, (t, t), 0) >= lax.broadcasted_iota(jnp.int32, (t, t), 1)


def _fox2_fwd(qp, kp, vp, c, *, name):
    h, s, _ = qp.shape
    t = min(FOX_BLOCK, s)
    nb = s // t
    nt = (((1,), (1,)), ((), ()))
    nn = (((1,), (0,)), ((), ()))

    def body(q_ref, k_ref, v_ref, c_ref, o_ref, lse_ref, m_sc, acc_sc):
        i = pl.program_id(1)
        m_sc[...] = jnp.full_like(m_sc, NEG)
        acc_sc[...] = jnp.zeros_like(acc_sc)
        qv = q_ref[0]

        def step(j, masked):
            at = pl.ds(pl.multiple_of(j * t, t), t)
            kv, vv = k_ref[0, at, :], v_ref[0, at, :]
            sc = lax.dot_general(qv, kv, nt, preferred_element_type=F32) + _fox_bias(c_ref, i, j, t)
            if masked:
                sc = jnp.where(_causal(t), sc, NEG)
            m_prev = m_sc[...]
            m_new = jnp.maximum(m_prev, jnp.max(sc, axis=1, keepdims=True))
            pr = jnp.exp(sc - jnp.tile(m_new, (1, t // LANES)))
            pr_hi = pr.astype(BF16)
            pr_lo = (pr - pr_hi.astype(F32)).astype(BF16)
            pv = (lax.dot_general(pr_hi, vv, nn, preferred_element_type=F32)
                  + lax.dot_general(pr_lo, vv, nn, preferred_element_type=F32))
            acc_sc[...] = jnp.exp(m_prev - m_new) * acc_sc[...] + pv
            m_sc[...] = m_new

        lax.fori_loop(0, i, lambda j, carry: (step(j, False), carry)[1], 0)
        step(i, True)
        acc = acc_sc[...]
        lane = lax.broadcasted_iota(jnp.int32, (1, LANES), 1)
        den = jnp.sum(jnp.where(lane == AUX, acc, 0.0), axis=1, keepdims=True)
        o_ref[0] = (acc / den)[:, :FOX_HEADDIM]
        lse_ref[0] = m_sc[...] + jnp.log(den)

    whole = pl.BlockSpec((1, s, LANES), lambda hh, i: (hh, 0, 0))
    return pl.pallas_call(
        body, name=name, grid=(h, nb),
        in_specs=[pl.BlockSpec((1, t, LANES), lambda hh, i: (hh, i, 0)), whole, whole,
                  pl.BlockSpec((1, 1, s), lambda hh, i: (hh, 0, 0))],
        out_specs=[pl.BlockSpec((1, t, FOX_HEADDIM), lambda hh, i: (hh, i, 0)),
                   pl.BlockSpec((1, t, LANES), lambda hh, i: (hh, i, 0))],
        out_shape=[jax.ShapeDtypeStruct((h, s, FOX_HEADDIM), F32), jax.ShapeDtypeStruct((h, s, LANES), F32)],
        scratch_shapes=[pltpu.VMEM((t, LANES), F32), pltpu.VMEM((t, LANES), F32)],
        compiler_params=_cparams(("parallel", "arbitrary")),
    )(qp, kp, vp, c)


def _fox2_bwd(qp, kp, vp, c, dop, lse, *, name):
    h, s, _ = qp.shape
    t = min(FOX_BLOCK, s)
    nb = s // t
    nt = (((1,), (1,)), ((), ()))
    nn = (((1,), (0,)), ((), ()))
    tn = (((0,), (0,)), ((), ()))

    def body(k_ref, v_ref, q_ref, c_ref, do_ref, lse_ref, dq_ref, dk_ref, dv_ref, dc_ref, dk_sc, dv_sc, dc_sc):
        j = pl.program_id(1)

        @pl.when(j == 0)
        def _():
            dq_ref[...] = jnp.zeros_like(dq_ref)

        dk_sc[...] = jnp.zeros_like(dk_sc)
        dv_sc[...] = jnp.zeros_like(dv_sc)
        dc_sc[...] = jnp.zeros_like(dc_sc)
        kv, vv = k_ref[0], v_ref[0]

        def step(i, masked):
            at = pl.ds(pl.multiple_of(i * t, t), t)
            qv, dov = q_ref[0, at, :], do_ref[0, at, :]
            sc = lax.dot_general(qv, kv, nt, preferred_element_type=F32) + _fox_bias(c_ref, i, j, t)
            pr = jnp.exp(sc - jnp.tile(lse_ref[0, at, :], (1, t // LANES)))
            if masked:
                pr = jnp.where(_causal(t), pr, 0.0)
            ds = pr * lax.dot_general(dov, vv, nt, preferred_element_type=F32)
            dc_sc[...] -= jnp.sum(ds, axis=0, keepdims=True)
            dsb = ds.astype(BF16)
            dv_sc[...] += lax.dot_general(pr.astype(BF16), dov, tn, preferred_element_type=F32)
            dk_sc[...] += lax.dot_general(dsb, qv, tn, preferred_element_type=F32)
            dq_ref[0, at, :] += lax.dot_general(dsb, kv, nn, preferred_element_type=F32)

        step(j, True)
        lax.fori_loop(j + 1, nb, lambda i, carry: (step(i, False), carry)[1], 0)
        dk_ref[0] = dk_sc[...]
        dv_ref[0] = dv_sc[...].astype(dv_ref.dtype)
        dc_ref[0] = dc_sc[...]

    whole = pl.BlockSpec((1, s, LANES), lambda hh, j: (hh, 0, 0))
    blk = pl.BlockSpec((1, t, LANES), lambda hh, j: (hh, j, 0))
    return pl.pallas_call(
        body, name=name, grid=(h, nb),
        in_specs=[blk, blk, whole, pl.BlockSpec((1, 1, s), lambda hh, j: (hh, 0, 0)), whole, whole],
        out_specs=[whole, blk, blk, pl.BlockSpec((1, 1, t), lambda hh, j: (hh, 0, j))],
        out_shape=[jax.ShapeDtypeStruct((h, s, LANES), F32), jax.ShapeDtypeStruct((h, s, LANES), F32),
                   jax.ShapeDtypeStruct((h, s, LANES), BF16), jax.ShapeDtypeStruct((h, 1, s), F32)],
        scratch_shapes=[pltpu.VMEM((t, LANES), F32), pltpu.VMEM((t, LANES), F32), pltpu.VMEM((1, t), F32)],
        compiler_params=_cparams(("parallel", "arbitrary")),
    )(kp, vp, qp, c, dop, lse)


PAIRS = FOX_HEADS // 2
HALF = LANES // 2


def _first_half():
    return lax.broadcasted_iota(jnp.int32, (1, LANES), 1) < HALF


def _pair_bias(c_ref, hh, qblock, kblock, t):
    lane = lax.broadcasted_iota(jnp.int32, (1, LANES), 1)
    cq = c_ref[hh, :, pl.ds(pl.multiple_of(qblock * t, LANES), LANES)]
    cref = jnp.sum(jnp.where(lane == 0, cq, 0.0), axis=1, keepdims=True)
    return cref - c_ref[hh, :, pl.ds(pl.multiple_of(kblock * t, LANES), t)]


def _fox_dopack(do, o, *, tm=256, name):
    s, d = do.shape
    tm = min(tm, s)

    def body(do_ref, o_ref, out_ref):
        dov = do_ref[...].astype(BF16)
        prod = dov.astype(F32) * o_ref[...]
        r = lax.broadcasted_iota(jnp.int32, (d, LANES), 0)
        c = lax.broadcasted_iota(jnp.int32, (d, LANES), 1)
        heads = ((r >= c * FOX_HEADDIM) & (r < (c + 1) * FOX_HEADDIM)).astype(BF16)
        negd = -_mask_dot(heads, prod, "vn")
        hr = lax.broadcasted_iota(jnp.int32, (LANES, 2 * d), 0)
        col = lax.broadcasted_iota(jnp.int32, (LANES, 2 * d), 1)
        base = (hr >> 1) * (2 * LANES) + jnp.where((hr & 1) == 0, HALF, LANES)
        terms = None
        for kk, part in enumerate(_split3(negd)):
            place = ((col == base + kk) & (hr < FOX_HEADS)).astype(BF16)
            tk = lax.dot_general(part, place, (_DIMS["nn"], ((), ())), preferred_element_type=F32)
            terms = tk if terms is None else terms + tk
        first = _first_half()
        zero = jnp.zeros((tm, LANES), BF16)
        pieces = []
        for hp in range(PAIRS):
            blk = dov[:, hp * LANES:(hp + 1) * LANES]
            pieces += [jnp.where(first, blk, zero), jnp.where(first, zero, blk)]
        out_ref[...] = (jnp.concatenate(pieces, axis=1).astype(F32) + terms).astype(BF16)

    row = pl.BlockSpec((tm, d), lambda i: (i, 0))
    return pl.pallas_call(body, name=name, grid=(s // tm,), in_specs=[row, row],
                          out_specs=pl.BlockSpec((tm, 2 * d), lambda i: (i, 0)),
                          out_shape=jax.ShapeDtypeStruct((s, 2 * d), BF16),
                          compiler_params=_cparams(("parallel",)))(do, o)


def _fox3_fwd(qkv, c, *, carry=None, name):
    s = qkv.shape[0]
    t = min(FOX_BLOCK, s)
    nb = s // t
    nt = (((1,), (1,)), ((), ()))
    nn = (((1,), (0,)), ((), ()))
    scale = FOX_HEADDIM ** -0.5

    def body(*refs):
        (q_ref, k_ref, v_ref, c_ref), (o_ref, lse_ref), (m_sc, acc_sc), comm = _carried(carry, refs, 4, 2, 2)
        i = pl.program_id(1)
        if carry is not None:
            @pl.when((pl.program_id(0) == 0) & (i == 0))
            def _():
                carry.start(*comm)

        first = _first_half()
        m_sc[...] = jnp.full_like(m_sc, NEG)
        acc_sc[...] = jnp.zeros_like(acc_sc)
        q2 = q_ref[...] * scale
        zero = jnp.zeros_like(q2)
        qs = [jnp.where(first, q2, zero), jnp.where(first, zero, q2)]

        def step(j, masked):
            at = pl.ds(pl.multiple_of(j * t, t), t)
            k2, v2 = k_ref[at, :], v_ref[at, :]
            one = jnp.ones_like(v2)
            vx = [jnp.where(first, v2, one), jnp.where(first, one, v2)]
            for hh in range(2):
                sc = lax.dot_general(qs[hh], k2, nt, preferred_element_type=F32) + _pair_bias(c_ref, hh, i, j, t)
                if masked:
                    sc = jnp.where(_causal(t), sc, NEG)
                m_prev = m_sc[hh]
                m_new = jnp.maximum(m_prev, jnp.max(sc, axis=1, keepdims=True))
                pr = jnp.exp(sc - jnp.tile(m_new, (1, t // LANES)))
                pr_hi = pr.astype(BF16)
                pr_lo = (pr - pr_hi.astype(F32)).astype(BF16)
                pv = (lax.dot_general(pr_hi, vx[hh], nn, preferred_element_type=F32)
                      + lax.dot_general(pr_lo, vx[hh], nn, preferred_element_type=F32))
                acc_sc[hh] = jnp.exp(m_prev - m_new) * acc_sc[hh] + pv
                m_sc[hh] = m_new

        lax.fori_loop(0, i, lambda j, carry: (step(j, False), carry)[1], 0)
        step(i, True)
        acc_a, acc_b = acc_sc[0], acc_sc[1]
        den_a = jnp.where(first, pltpu.roll(acc_a, HALF, 1), acc_a)
        den_b = jnp.where(first, acc_b, pltpu.roll(acc_b, HALF, 1))
        o_ref[...] = jnp.where(first, acc_a / den_a, acc_b / den_b)
        lse_ref[:, :LANES] = m_sc[0] + jnp.log(den_a)
        lse_ref[:, LANES:] = m_sc[1] + jnp.log(den_b)
        if carry is not None:
            @pl.when((pl.program_id(0) == PAIRS - 1) & (i == nb - 1))
            def _():
                carry.wait(*comm)

    nq = D_FOX // LANES
    extra = carry if carry is not None else _Exchange([], "c", False)
    return pl.pallas_call(
        body, name=name, grid=(PAIRS, nb),
        in_specs=[pl.BlockSpec((t, LANES), lambda hp, i: (i, hp)),
                  pl.BlockSpec((s, LANES), lambda hp, i: (0, nq + hp)),
                  pl.BlockSpec((s, LANES), lambda hp, i: (0, 2 * nq + hp)),
                  pl.BlockSpec((2, 1, s), lambda hp, i: (hp, 0, 0))] + extra.in_specs,
        out_specs=[pl.BlockSpec((t, LANES), lambda hp, i: (i, hp)),
                   pl.BlockSpec((t, 2 * LANES), lambda hp, i: (i, hp))] + extra.out_specs,
        out_shape=[jax.ShapeDtypeStruct((s, D_FOX), F32), jax.ShapeDtypeStruct((s, 2 * D_FOX), F32)] + extra.out_shape,
        scratch_shapes=[pltpu.VMEM((2, t, LANES), F32), pltpu.VMEM((2, t, LANES), F32)]
        + (carry.scratch if carry is not None else []),
        compiler_params=_cparams(("arbitrary", "arbitrary")),
    )(qkv, qkv, qkv, c, *extra.arrs)


def _fox3_bwd(qkv, c, dox, lse, *, carry=None, name):
    s = qkv.shape[0]
    t = min(FOX_BLOCK, s)
    nb = s // t
    nt = (((1,), (1,)), ((), ()))
    nn = (((1,), (0,)), ((), ()))
    tn = (((0,), (0,)), ((), ()))
    scale = FOX_HEADDIM ** -0.5

    def body(*refs):
        ins, outs, scratch, comm = _carried(carry, refs, 6, 4, 3)
        k_ref, v_ref, q_ref, c_ref, do_ref, lse_ref = ins
        dq_ref, dk_ref, dv_ref, dc_ref = outs
        dk_sc, dv_sc, dc_sc = scratch
        j = pl.program_id(1)
        if carry is not None:
            @pl.when((pl.program_id(0) == 0) & (j == 0))
            def _():
                carry.start(*comm)

        first = _first_half()
        halves = [first, jnp.logical_not(first)]

        @pl.when(j == 0)
        def _():
            dq_ref[...] = jnp.zeros_like(dq_ref)

        dk_sc[...] = jnp.zeros_like(dk_sc)
        dv_sc[...] = jnp.zeros_like(dv_sc)
        dc_sc[...] = jnp.zeros_like(dc_sc)
        k2, v2 = k_ref[...], v_ref[...]
        one = jnp.ones_like(v2)
        vx = [jnp.where(first, v2, one), jnp.where(first, one, v2)]

        def step(i, masked):
            at = pl.ds(pl.multiple_of(i * t, t), t)
            q2 = q_ref[at, :] * scale
            zero = jnp.zeros_like(q2)
            qs = [jnp.where(first, q2, zero), jnp.where(first, zero, q2)]
            for hh in range(2):
                dov = do_ref[at, hh * LANES:(hh + 1) * LANES]
                sc = lax.dot_general(qs[hh], k2, nt, preferred_element_type=F32) + _pair_bias(c_ref, hh, i, j, t)
                pr = jnp.exp(sc - jnp.tile(lse_ref[at, hh * LANES:(hh + 1) * LANES], (1, t // LANES)))
                if masked:
                    pr = jnp.where(_causal(t), pr, 0.0)
                ds = pr * lax.dot_general(dov, vx[hh], nt, preferred_element_type=F32)
                dc_sc[hh] -= jnp.sum(ds, axis=0, keepdims=True)
                dsb = ds.astype(BF16)
                dvh = lax.dot_general(pr.astype(BF16), dov, tn, preferred_element_type=F32)
                dv_sc[...] += jnp.where(halves[hh], dvh, 0.0)
                dk_sc[...] += lax.dot_general(dsb, qs[hh], tn, preferred_element_type=F32)
                dqh = lax.dot_general(dsb, k2, nn, preferred_element_type=F32)
                dq_ref[at, :] += jnp.where(halves[hh], dqh, 0.0)

        step(j, True)
        lax.fori_loop(j + 1, nb, lambda i, carry: (step(i, False), carry)[1], 0)
        dk_ref[...] = dk_sc[...]
        dv_ref[...] = dv_sc[...].astype(dv_ref.dtype)
        dc_ref[...] = dc_sc[...]
        if carry is not None:
            @pl.when((pl.program_id(0) == PAIRS - 1) & (j == nb - 1))
            def _():
                carry.wait(*comm)

    nq = D_FOX // LANES
    blk = pl.BlockSpec((t, LANES), lambda hp, j: (j, hp))
    extra = carry if carry is not None else _Exchange([], "c", False)
    return pl.pallas_call(
        body, name=name, grid=(PAIRS, nb),
        in_specs=[pl.BlockSpec((t, LANES), lambda hp, j: (j, nq + hp)),
                  pl.BlockSpec((t, LANES), lambda hp, j: (j, 2 * nq + hp)),
                  pl.BlockSpec((s, LANES), lambda hp, j: (0, hp)),
                  pl.BlockSpec((2, 1, s), lambda hp, j: (hp, 0, 0)),
                  pl.BlockSpec((s, 2 * LANES), lambda hp, j: (0, hp)),
                  pl.BlockSpec((s, 2 * LANES), lambda hp, j: (0, hp))] + extra.in_specs,
        out_specs=[pl.BlockSpec((s, LANES), lambda hp, j: (0, hp)), blk, blk,
                   pl.BlockSpec((2, 1, t), lambda hp, j: (hp, 0, j))] + extra.out_specs,
        out_shape=[jax.ShapeDtypeStruct((s, D_FOX), F32), jax.ShapeDtypeStruct((s, D_FOX), F32),
                   jax.ShapeDtypeStruct((s, D_FOX), BF16), jax.ShapeDtypeStruct((FOX_HEADS, 1, s), F32)]
        + extra.out_shape,
        scratch_shapes=[pltpu.VMEM((t, LANES), F32), pltpu.VMEM((t, LANES), F32), pltpu.VMEM((2, 1, t), F32)]
        + (carry.scratch if carry is not None else []),
        compiler_params=_cparams(("arbitrary", "arbitrary")),
    )(qkv, qkv, qkv, c, dox, lse, *extra.arrs)


def _final(x1, out1, g, tgt, *, tm=256, name):
    s, d = x1.shape
    tm = min(tm, s)

    def body(x_ref, o_ref, g_ref, t_ref, dx_ref, do_ref, dg_ref, loss_ref):
        i = pl.program_id(0)

        @pl.when(i == 0)
        def _():
            dg_ref[...] = jnp.zeros_like(dg_ref)
            loss_ref[...] = jnp.zeros_like(loss_ref)

        tv = t_ref[...]

        def lossf(xv, ov, gv):
            err = jnp.square(xv + _rms(ov, gv) - tv)
            return 0.5 * jnp.sum(jnp.mean(err, axis=-1, keepdims=True), axis=0, keepdims=True)

        val, vjp = jax.vjp(lossf, x_ref[...], o_ref[...], g_ref[...])
        dx, do, dg = vjp(jnp.ones((1, 1), F32))
        dx_ref[...] = dx
        do_ref[...] = do.astype(do_ref.dtype)
        dg_ref[...] += dg
        loss_ref[...] += val

    row = pl.BlockSpec((tm, d), lambda i: (i, 0))
    par = pl.BlockSpec((1, d), lambda i: (0, 0))
    return pl.pallas_call(
        body, name=name, grid=(s // tm,), in_specs=[row, row, par, row],
        out_specs=[row, row, par, pl.BlockSpec((1, 1), lambda i: (0, 0))],
        out_shape=[jax.ShapeDtypeStruct((s, d), F32), jax.ShapeDtypeStruct((s, d), BF16),
                   jax.ShapeDtypeStruct((1, d), F32), jax.ShapeDtypeStruct((1, 1), F32)],
        compiler_params=_cparams(("arbitrary",)),
    )(x1, out1, g, tgt)


def _row_tile(r):
    return LANES if r % LANES == 0 else r


def _sum_slots(parts, *, out_dtype=F32, name):
    p, r, c = parts.shape
    tr = _row_tile(r)

    def body(p_ref, o_ref):
        acc = p_ref[0].astype(F32)
        for k in range(1, p):
            acc = acc + p_ref[k].astype(F32)
        o_ref[...] = acc.astype(o_ref.dtype)

    return pl.pallas_call(
        body, name=name, grid=(r // tr,),
        in_specs=[pl.BlockSpec((p, tr, c), lambda i: (0, i, 0))],
        out_specs=pl.BlockSpec((tr, c), lambda i: (i, 0)),
        out_shape=jax.ShapeDtypeStruct((r, c), out_dtype),
        compiler_params=_cparams(("parallel",)),
    )(parts)


def _adamw(w, gparts, m, v, *, name):
    r, c = w.shape
    p = gparts.shape[0]
    tr = _row_tile(r)

    def body(w_ref, g_ref, m_ref, v_ref, go_ref, d_ref, mo_ref, vo_ref):
        g = g_ref[0].astype(F32)
        for k in range(1, p):
            g = g + g_ref[k].astype(F32)
        mn = ADAM_B1 * m_ref[...] + (1.0 - ADAM_B1) * g
        vn = ADAM_B2 * v_ref[...] + (1.0 - ADAM_B2) * jnp.square(g)
        m_hat = mn / (1.0 - ADAM_B1 ** ADAM_STEP)
        v_hat = vn / (1.0 - ADAM_B2 ** ADAM_STEP)
        go_ref[...] = g
        d_ref[...] = -ADAM_LR * (m_hat / (jnp.sqrt(v_hat) + ADAM_EPS) + ADAM_WD * w_ref[...])
        mo_ref[...] = mn
        vo_ref[...] = vn

    spec = pl.BlockSpec((tr, c), lambda i: (i, 0))
    return pl.pallas_call(
        body, name=name, grid=(r // tr,),
        in_specs=[spec, pl.BlockSpec((p, tr, c), lambda i: (0, i, 0)), spec, spec],
        out_specs=[spec] * 4, out_shape=[jax.ShapeDtypeStruct((r, c), F32)] * 4,
        compiler_params=_cparams(("parallel",)),
    )(w, gparts, m, v)


_FLIPS = {
    "xy": [(1, 0, 0), (0, 1, 0), (1, 1, 0)],
    "c": [(0, 0, 1)],
    "xyc": [(fx, fy, fc) for fx in (0, 1) for fy in (0, 1) for fc in (0, 1) if (fx, fy, fc) != (0, 0, 0)],
}


def _slot(mode, px, py, pc):
    return {"xy": 2 * px + py, "c": pc, "xyc": 4 * px + 2 * py + pc}[mode]


class _Exchange:
    def __init__(self, arrs, mode, scatter):
        self.arrs, self.mode, self.scatter = list(arrs), mode, scatter
        self.n = len(self.arrs)
        self.flips = _FLIPS[mode]
        nf = len(self.flips)
        anyspec = pl.BlockSpec(memory_space=pl.ANY)
        self.in_specs = [anyspec] * self.n
        self.out_specs = [anyspec] * self.n
        self.out_shape = [jax.ShapeDtypeStruct((nf + 1,) + (a.shape[1:] if scatter else a.shape), a.dtype)
                          for a in self.arrs]
        self.scratch = [pltpu.SemaphoreType.DMA((self.n * nf,)), pltpu.SemaphoreType.DMA((self.n * nf,)),
                        pltpu.SemaphoreType.DMA((self.n,))]

    def _copies(self, ins, outs, sems, arrivals=True):
        send, recv, loc = sems
        nf = len(self.flips)
        x, y, c = lax.axis_index("x"), lax.axis_index("y"), lax.axis_index("c")
        me = _slot(self.mode, x, y, c)
        peers = [(x ^ fx, y ^ fy, c ^ fc) for (fx, fy, fc) in self.flips]

        def src(a, slot):
            return ins[a].at[slot] if self.scatter else ins[a]

        def copy(a, j, dst_slot):
            return pltpu.make_async_remote_copy(
                src_ref=src(a, _slot(self.mode, *peers[j])), dst_ref=outs[a].at[dst_slot],
                send_sem=send.at[a * nf + j], recv_sem=recv.at[a * nf + j], device_id=peers[j], device_id_type=MESH)

        pairs = [(a, j) for a in range(self.n) for j in range(nf)]
        local = [pltpu.make_async_copy(src(a, me), outs[a].at[me], loc.at[a]) for a in range(self.n)]
        sends = [copy(a, j, me) for a, j in pairs]
        recvs = [copy(a, j, _slot(self.mode, *peers[j])) for a, j in pairs] if arrivals else []
        return local, sends, recvs

    def start(self, ins, outs, sems):
        local, sends, _ = self._copies(ins, outs, sems, arrivals=False)
        for cp in local + sends:
            cp.start()

    def wait(self, ins, outs, sems):
        local, sends, recvs = self._copies(ins, outs, sems)
        for cp in recvs:
            cp.wait_recv()
        for cp in sends:
            cp.wait_send()
        for cp in local:
            cp.wait()


def _carried(carry, refs, n_in, n_out, n_scratch):
    k = carry.n if carry is not None else 0
    ins, refs = refs[:n_in], refs[n_in:]
    cin, refs = refs[:k], refs[k:]
    outs, refs = refs[:n_out], refs[n_out:]
    cout, refs = refs[:k], refs[k:]
    scratch, sems = refs[:n_scratch], refs[n_scratch:]
    return ins, outs, scratch, (cin, cout, sems)


def _exchanges(exs, *, name):
    counts = [ex.n for ex in exs]
    total = sum(counts)

    def body(*refs):
        ins, outs, sems = refs[:total], refs[total:2 * total], refs[2 * total:]
        comms, at = [], 0
        for k, ex in enumerate(exs):
            comms.append((ins[at:at + ex.n], outs[at:at + ex.n], sems[3 * k:3 * k + 3]))
            at += ex.n
        for ex, comm in zip(exs, comms):
            ex.start(*comm)
        for ex, comm in zip(exs, comms):
            ex.wait(*comm)

    res = pl.pallas_call(
        body, name=name, in_specs=[sp for ex in exs for sp in ex.in_specs],
        out_specs=[sp for ex in exs for sp in ex.out_specs], out_shape=[sh for ex in exs for sh in ex.out_shape],
        scratch_shapes=[sc for ex in exs for sc in ex.scratch])(*[a for ex in exs for a in ex.arrs])
    out, at = [], 0
    for n in counts:
        out.append(list(res[at:at + n]))
        at += n
    return out


def _exchange(arrs, mode, scatter, *, name):
    return _exchanges([_Exchange(arrs, mode, scatter)], name=name)[0]


def _softplus(v):
    return jnp.maximum(v, 0.0) + jnp.log1p(jnp.exp(-jnp.abs(v)))


def _pad_lanes(v):
    r, n = v.shape
    return jnp.pad(v, ((0, 0), (0, -n % LANES)))


def _to_heads(v):
    s = v.shape[0]
    return v.reshape(s, -1, 64).transpose(1, 0, 2)


def _from_heads(v):
    h, s, p = v.shape
    return v.transpose(1, 0, 2).reshape(s, h * p)


def _fn_rms(v, g):
    return (_rms(v, g),)


def _fn_post(xv, ov, g):
    return (xv + _rms(ov, g),)


def _fn_act(xbc, dtp, fp, dtb, fb):
    return _silu(xbc), _softplus(dtp + dtb), -_softplus(-(fp + fb))


def _fn_mix(y, zs, o, zf, g):
    yg = y * _silu(zs)
    sq = yg * yg
    lane = lax.broadcasted_iota(jnp.int32, (1, D_SSD), 1)
    width = D_SSD // SSD_GROUPS
    rstd = jnp.zeros_like(yg)
    for gi in range(SSD_GROUPS):
        msk = ((lane >= gi * width) & (lane < (gi + 1) * width)).astype(F32)
        ms = jnp.sum(sq * msk, axis=1, keepdims=True) / width
        rstd = rstd + lax.rsqrt(ms + EPS) * msk
    return (jnp.concatenate([yg * rstd * g, o * _silu(zf)], axis=1),)


def _fn_glu(val, gate):
    return (val * jax.nn.sigmoid(gate),)


def _fn_ln(hc, z, g, b):
    mu = jnp.mean(hc, axis=-1, keepdims=True)
    xc = hc - mu
    yn = xc * lax.rsqrt(jnp.mean(xc * xc, axis=-1, keepdims=True) + EPS) * g + b
    return (_silu(yn) * _silu(z),)


class _NoComm:
    def odd_weights(self):
        return None

    def got_odd_weights(self, got, w):
        pass

    def early_grads(self, g):
        return None

    def got_early_grads(self, got):
        pass

    def early_sums(self):
        return None

    def got_early_sums(self, got):
        pass


def _local_step(x, tgt, w, comm=None):
    comm = comm or _NoComm()
    s = x.shape[0]
    d = D_MODEL
    tm = 256
    bf = lambda v: v.astype(BF16)
    c1 = lambda arr: _col(arr, 0, arr.shape[1])
    g = {}

    ew = w["e_w_in"]
    w_z, w_xbc = bf(ew[:, 0:2048]), bf(ew[:, 2048:4096])
    w_dt = bf(_pad_lanes(ew[:, 4096:4112]))
    w_qkv = bf(ew[:, 4112:7184])
    w_f = bf(_pad_lanes(ew[:, 7184:7200]))
    w_eo = bf(w["e_w_out"])
    dtb, fgb = _pad_lanes(w["e_dt_bias"]), _pad_lanes(w["e_fgate_b"])
    alog, dsk = _pad_lanes(w["e_a_log"]), _pad_lanes(w["e_d_skip"])

    (u0,) = _rowwise_fwd(_fn_rms, [c1(x)], [c1(w["e_norm_pre"])], [(d, BF16)], tm=tm, name="e_pre")
    z = _mm(u0, w_z, out_dtype=BF16, name="e_in_z")
    xbc_raw = _mm(u0, w_xbc, out_dtype=BF16, name="e_in_xbc")
    qkv = _mm(u0, w_qkv, out_dtype=BF16, name="e_in_qkv")
    dtp = _mm(u0, w_dt, name="e_in_dt")
    fp = _mm(u0, w_f, name="e_in_f")
    xbc_pre = _conv_fwd(xbc_raw, w["e_conv_w"], w["e_conv_b"], name="e_conv")
    act_rows = [c1(xbc_pre), c1(dtp), c1(fp)]
    act_pars = [c1(dtb), c1(fgb)]
    xbc, dt, lf = _rowwise_fwd(_fn_act, act_rows, act_pars, [(2048, F32), (LANES, F32), (LANES, F32)],
                               tm=tm, name="e_act")
    y, hsave = _ssd2_fwd(dt, xbc, alog, dsk, name="e_ssd")
    csum = _cumsum_lanes(lf[:, :FOX_HEADS].T, reverse=False, name="e_cumsum").reshape(FOX_HEADS, 1, s)
    o, lse, *got = _fox3_fwd(qkv, csum, carry=comm.odd_weights(), name="e_fox")
    comm.got_odd_weights(got, w)
    w_oi, w_oo = bf(w["o_w_in"]), bf(w["o_w_out"])
    mix_rows = [c1(y), _col(z, 0, D_SSD), c1(o), _col(z, 1, D_FOX)]
    mix_pars = [c1(w["e_ssd_norm"])]
    (hmix,) = _rowwise_fwd(_fn_mix, mix_rows, mix_pars, [(2048, BF16)], tm=tm, name="e_mix")
    out0 = _mm(hmix, w_eo, name="e_out")
    post_rows = [c1(x), c1(out0)]
    (x1,) = _rowwise_fwd(_fn_post, post_rows, [c1(w["e_norm_post"])], [(d, F32)], tm=tm, name="e_post")

    (u1,) = _rowwise_fwd(_fn_rms, [c1(x1)], [c1(w["o_norm_pre"])], [(d, BF16)], tm=tm, name="o_pre")
    p1 = _mm(u1, w_oi, out_dtype=BF16, name="o_in")
    glu_rows = [_col(p1, 0, D_CONV), _col(p1, 1, D_CONV)]
    (hg,) = _rowwise_fwd(_fn_glu, glu_rows, [], [(D_CONV, BF16)], tm=tm, name="o_glu")
    hc = _conv_fwd(hg, w["o_conv_w"], w["o_conv_b"], name="o_conv")
    ln_rows = [c1(hc), _col(p1, 2, D_CONV)]
    ln_pars = [c1(w["o_ln_g"]), c1(w["o_ln_b"])]
    (h2,) = _rowwise_fwd(_fn_ln, ln_rows, ln_pars, [(D_CONV, BF16)], tm=tm, name="o_ln")
    out1 = _mm(h2, w_oo, name="o_out")

    dx2, dout1, g["o_norm_post"], loss = _final(x1, out1, w["o_norm_post"], tgt, name="loss_head")
    dh2 = _mm(dout1, w_oo, tb=True, name="o_out_dx")
    g["o_w_out"] = _mm(h2, dout1, ta=True, name="o_out_dw")
    (dhc, dz1), (g["o_ln_g"], g["o_ln_b"]) = _rowwise_bwd(_fn_ln, ln_rows, ln_pars, [c1(dh2)], [F32, BF16],
                                                         tm=tm, name="o_ln_bwd")
    dhg = _conv_bwd_x(dhc, w["o_conv_w"], name="o_conv_dx")
    g["o_conv_w"], g["o_conv_b"] = _conv_bwd_w(hg, dhc, CONV_WIDTH, name="o_conv_dw")
    (dval, dgate), _ = _rowwise_bwd(_fn_glu, glu_rows, [], [c1(dhg)], [BF16, BF16], tm=tm, name="o_glu_bwd")
    du1 = _mm(dval, w_oi[:, 0:2048], tb=True, name="o_in_dx0")
    du1 = _mm(dgate, w_oi[:, 2048:4096], tb=True, add=du1, name="o_in_dx1")
    du1 = _mm(dz1, w_oi[:, 4096:6144], tb=True, add=du1, name="o_in_dx2")
    g["o_w_in"] = jnp.concatenate([_mm(u1, dval, ta=True, name="o_in_dw0"), _mm(u1, dgate, ta=True, name="o_in_dw1"),
                                   _mm(u1, dz1, ta=True, name="o_in_dw2")], axis=1)
    (dx1,), (g["o_norm_pre"],) = _rowwise_bwd(_fn_rms, [c1(x1)], [c1(w["o_norm_pre"])], [c1(du1)], [F32],
                                              adds={0: c1(dx2)}, tm=tm, name="o_pre_bwd")

    (dout0,), (g["e_norm_post"],) = _rowwise_bwd(_fn_post, post_rows, [c1(w["e_norm_post"])], [c1(dx1)],
                                                 [None, BF16], tm=tm, name="e_post_bwd")
    dhmix = _mm(dout0, w_eo, tb=True, name="e_out_dx")
    g["e_w_out"] = _mm(hmix, dout0, ta=True, name="e_out_dw")
    (dy, dzs, do, dzf), (g["e_ssd_norm"],) = _rowwise_bwd(_fn_mix, mix_rows, mix_pars, [c1(dhmix)],
                                                        [F32, BF16, F32, BF16], tm=tm, name="e_mix_bwd")
    dox = _fox_dopack(do, o, name="e_fox_dopack")
    dq8, dk, dv, dcs, *got = _fox3_bwd(qkv, csum, dox, lse, carry=comm.early_grads(g), name="e_fox_bwd")
    comm.got_early_grads(got)
    dlf = _pad_lanes(_cumsum_lanes(dcs.reshape(FOX_HEADS, s), reverse=True, name="e_cumsum_bwd").T)
    dxs, ddt, dbm, dcm, dalog, ddsk, *got = _ssd2_bwd(dt, xbc, alog, dsk, hsave, dy, carry=comm.early_sums(),
                                                       name="e_ssd_bwd")
    comm.got_early_sums(got)
    dxbc = jnp.concatenate([dxs, dbm, dcm], axis=1)
    (dxbc_pre, ddtp, dfp), (ddtb, dfgb) = _rowwise_bwd(_fn_act, act_rows, act_pars, [c1(dxbc), c1(ddt), c1(dlf)],
                                                      [F32, BF16, BF16], tm=tm, name="e_act_bwd")
    dxbc_raw = bf(_conv_bwd_x(dxbc_pre, w["e_conv_w"], name="e_conv_dx"))
    g["e_conv_w"], g["e_conv_b"] = _conv_bwd_w(xbc_raw, dxbc_pre, SSD_CONV, name="e_conv_dw")
    du0 = _mm(dzs, w_z[:, :D_SSD], tb=True, name="e_in_dx0")
    du0 = _mm(dzf, w_z[:, D_SSD:], tb=True, add=du0, name="e_in_dx1")
    du0 = _mm(dxbc_raw, w_xbc, tb=True, add=du0, name="e_in_dx2")
    eighth = FOX_HEADDIM ** -0.5
    du0 = _mm(dq8, w_qkv[:, :D_FOX] * eighth, tb=True, add=du0, name="e_in_dx3q")
    du0 = _mm(dk, w_qkv[:, D_FOX:2 * D_FOX], tb=True, add=du0, name="e_in_dx3k")
    du0 = _mm(dv, w_qkv[:, 2 * D_FOX:], tb=True, add=du0, name="e_in_dx3v")
    du0 = _mm(ddtp, w_dt, tb=True, add=du0, name="e_in_dx4")
    du0 = _mm(dfp, w_f, tb=True, add=du0, name="e_in_dx5")
    g["e_w_in"] = jnp.concatenate([
        _mm(u0, dzs, ta=True, name="e_in_dw0"), _mm(u0, dzf, ta=True, name="e_in_dw1"),
        _mm(u0, dxbc_raw, ta=True, name="e_in_dw2"), _mm(u0, ddtp, ta=True, name="e_in_dw3")[:, :SSD_HEADS],
        _mm(u0, dq8, ta=True, name="e_in_dw4q") * eighth, _mm(u0, dk, ta=True, name="e_in_dw4k"),
        _mm(u0, dv, ta=True, name="e_in_dw4v"), _mm(u0, dfp, ta=True, name="e_in_dw5")[:, :FOX_HEADS]], axis=1)
    (dx,), (g["e_norm_pre"],) = _rowwise_bwd(_fn_rms, [c1(x)], [c1(w["e_norm_pre"])], [c1(du0)], [F32],
                                             adds={0: c1(dx1)}, tm=tm, name="e_pre_bwd")
    g["e_dt_bias"], g["e_fgate_b"] = ddtb[:, :SSD_HEADS], dfgb[:, :FOX_HEADS]
    g["e_a_log"], g["e_d_skip"] = dalog[:, :SSD_HEADS], ddsk[:, :SSD_HEADS]
    return loss, dx, g


_WEIGHTS = ["e_norm_pre", "e_w_in", "e_conv_w", "e_conv_b", "e_dt_bias", "e_a_log", "e_d_skip", "e_fgate_b",
            "e_ssd_norm", "e_w_out", "e_norm_post", "o_norm_pre", "o_w_in", "o_conv_w", "o_conv_b", "o_ln_g",
            "o_ln_b", "o_w_out", "o_norm_post"]
_BIG = ["e_w_in", "e_w_out", "o_w_in", "o_w_out"]
_ROW_SHARDED = ["e_w_out", "o_w_out"]
_SMALL_SHARDED = ["e_conv_w", "o_norm_pre", "o_conv_w", "o_conv_b", "o_ln_g", "o_ln_b", "o_norm_post"]
_REPLICATED = ["e_norm_pre", "e_conv_b", "e_dt_bias", "e_a_log", "e_d_skip", "e_fgate_b", "e_ssd_norm", "e_norm_post"]
_SMALL = [n for n in _WEIGHTS if n not in _BIG]
_EVEN_SHARDED = ["e_w_in", "e_w_out", "e_conv_w"]
_ODD_SHARDED = ["o_w_in", "o_w_out", "o_norm_pre", "o_conv_w", "o_conv_b", "o_ln_g", "o_ln_b", "o_norm_post"]
_EARLY_GRADS = ["o_w_in", "o_w_out", "e_w_out"]
N_CHIPS = 4


def _join(gathered, rows):
    k, r, c = gathered.shape
    return gathered.reshape(k * r, c) if rows else gathered.transpose(1, 0, 2).reshape(r, k * c)


def _split(full, rows):
    r, c = full.shape
    return full.reshape(N_CHIPS, r // N_CHIPS, c) if rows else full.reshape(r, N_CHIPS, c // N_CHIPS).transpose(1, 0, 2)


def kernel(x, e_norm_pre, e_w_in, e_conv_w, e_conv_b, e_dt_bias, e_a_log, e_d_skip, e_fgate_b, e_ssd_norm, e_w_out, e_norm_post, o_norm_pre, o_w_in, o_conv_w, o_conv_b, o_ln_g, o_ln_b, o_w_out, o_norm_post, loss_target, m_e_norm_pre, m_e_w_in, m_e_conv_w, m_e_conv_b, m_e_dt_bias, m_e_a_log, m_e_d_skip, m_e_fgate_b, m_e_ssd_norm, m_e_w_out, m_e_norm_post, m_o_norm_pre, m_o_w_in, m_o_conv_w, m_o_conv_b, m_o_ln_g, m_o_ln_b, m_o_w_out, m_o_norm_post, v_e_norm_pre, v_e_w_in, v_e_conv_w, v_e_conv_b, v_e_dt_bias, v_e_a_log, v_e_d_skip, v_e_fgate_b, v_e_ssd_norm, v_e_w_out, v_e_norm_post, v_o_norm_pre, v_o_w_in, v_o_conv_w, v_o_conv_b, v_o_ln_g, v_o_ln_b, v_o_w_out, v_o_norm_post):
    wvals = (e_norm_pre, e_w_in, e_conv_w, e_conv_b, e_dt_bias, e_a_log, e_d_skip, e_fgate_b, e_ssd_norm, e_w_out,
             e_norm_post, o_norm_pre, o_w_in, o_conv_w, o_conv_b, o_ln_g, o_ln_b, o_w_out, o_norm_post)
    mvals = (m_e_norm_pre, m_e_w_in, m_e_conv_w, m_e_conv_b, m_e_dt_bias, m_e_a_log, m_e_d_skip, m_e_fgate_b,
             m_e_ssd_norm, m_e_w_out, m_e_norm_post, m_o_norm_pre, m_o_w_in, m_o_conv_w, m_o_conv_b, m_o_ln_g,
             m_o_ln_b, m_o_w_out, m_o_norm_post)
    vvals = (v_e_norm_pre, v_e_w_in, v_e_conv_w, v_e_conv_b, v_e_dt_bias, v_e_a_log, v_e_d_skip, v_e_fgate_b,
             v_e_ssd_norm, v_e_w_out, v_e_norm_post, v_o_norm_pre, v_o_w_in, v_o_conv_w, v_o_conv_b, v_o_ln_g,
             v_o_ln_b, v_o_w_out, v_o_norm_post)

    def mat(v):
        return v.reshape(v.shape[-2:]) if v.ndim == 3 else v

    w = {n: mat(v) for n, v in zip(_WEIGHTS, wvals)}
    m = {n: mat(v) for n, v in zip(_WEIGHTS, mvals)}
    v2 = {n: mat(v) for n, v in zip(_WEIGHTS, vvals)}
    me_xy = 2 * lax.axis_index("x") + lax.axis_index("y")

    def shard(n):
        return w[n].astype(BF16) if n in _BIG else w[n]

    gathered = _exchange([shard(n) for n in _EVEN_SHARDED], "xy", False, name="gather_weights")
    full = {n: w[n] for n in _REPLICATED}
    for n, gth in zip(_EVEN_SHARDED, gathered):
        full[n] = _join(gth, n in _ROW_SHARDED)
    gparts = {}

    class _StepComm(_NoComm):
        def odd_weights(self):
            return _Exchange([shard(n) for n in _ODD_SHARDED], "xy", False)

        def got_odd_weights(self, got, wdict):
            for n, gth in zip(_ODD_SHARDED, got):
                wdict[n] = _join(gth, n in _ROW_SHARDED)

        def early_grads(self, g):
            return _Exchange([_split(g[n], n in _ROW_SHARDED).astype(BF16) for n in _EARLY_GRADS], "xy", True)

        def got_early_grads(self, got):
            self.sums = [_sum_slots(p, out_dtype=BF16, name="sum_" + n) for n, p in zip(_EARLY_GRADS, got)]

        def early_sums(self):
            return _Exchange(self.sums, "c", False)

        def got_early_sums(self, got):
            gparts.update(zip(_EARLY_GRADS, got))

    loss, dx, g = _local_step(x[0], loss_target[0], full, _StepComm())
    loss = lax.psum(loss[0, 0], ("x", "y", "c"))

    flat = jnp.concatenate([_pad_lanes(g[n].reshape(1, -1)) for n in _SMALL], axis=1).reshape(-1, LANES)
    (scattered,), (all8,) = _exchanges([_Exchange([_split(g["e_w_in"], False).astype(BF16)], "xy", True),
                                        _Exchange([flat], "xyc", False)], name="scatter_grads")
    (gparts["e_w_in"],) = _exchange([_sum_slots(scattered, out_dtype=BF16, name="sum_e_w_in")], "c", False,
                                    name="pair_grads")
    total = _sum_slots(all8, name="sum_small").reshape(1, -1)
    at = 0
    for n in _SMALL:
        size = g[n].size
        gn = total[:, at:at + size].reshape(g[n].shape)
        at += size + (-size % LANES)
        if n in _SMALL_SHARDED:
            cols = gn.shape[1] // N_CHIPS
            gn = lax.dynamic_slice(gn, (0, me_xy * cols), (gn.shape[0], cols))
        gparts[n] = gn[None]

    grads, deltas, new_m, new_v = [], [], [], []
    for n, orig in zip(_WEIGHTS, wvals):
        gn, dn, mn, vn = _adamw(w[n], gparts[n], m[n], v2[n], name="adamw_" + n)
        for lst, val in zip((grads, deltas, new_m, new_v), (gn, dn, mn, vn)):
            lst.append(val.reshape(orig.shape))
    return (loss, dx[None], *grads, *deltas, *new_m, *new_v)
```

```python
import functools

import jax
import jax.numpy as jnp
from jax import lax
from jax.experimental import pallas as pl
from jax.experimental.pallas import tpu as pltpu

F32 = jnp.float32
BF16 = jnp.bfloat16
MESH = pl.DeviceIdType.MESH

D_MODEL = 1024
D_SSD = 1024
SSD_HEADS = 16
SSD_HEADDIM = 64
SSD_GROUPS = 4
SSD_HPG = 4
D_STATE = 128
SSD_CONV = 4
CHUNK = 128
D_FOX = 1024
FOX_HEADS = 16
FOX_HEADDIM = 64
D_CONV = 2048
CONV_WIDTH = 31
EPS = 1e-6
LANES = 128
VMEM_LIMIT = 56 * 1024 * 1024

ADAM_LR = 0.001
ADAM_B1 = 0.9
ADAM_B2 = 0.999
ADAM_EPS = 1e-08
ADAM_WD = 0.01
ADAM_STEP = 10


def _cparams(sem=None):
    return pltpu.CompilerParams(dimension_semantics=sem, vmem_limit_bytes=VMEM_LIMIT)


def _mm(a, b, *, ta=False, tb=False, add=None, out_dtype=F32, tm=1024, tn=None, tk=2048, name):
    m = a.shape[1] if ta else a.shape[0]
    k = a.shape[0] if ta else a.shape[1]
    n = b.shape[0] if tb else b.shape[1]
    if tn is None:
        tn = 1024
    tm, tn = min(tm, m), min(tn, n)
    tk = max(t for t in range(LANES, min(tk, k) + 1, LANES) if k % t == 0)
    assert m % tm == 0 and n % tn == 0 and k % tk == 0, (m, n, k, tm, tn, tk)
    nk = k // tk
    dims = (((0 if ta else 1,), (1 if tb else 0,)), ((), ()))

    def body(*refs):
        if add is None:
            a_ref, b_ref, o_ref = refs[:3]
            c_ref = None
        else:
            a_ref, b_ref, c_ref, o_ref = refs[:4]
        kk = pl.program_id(2)
        prod = lax.dot_general(a_ref[...].astype(BF16), b_ref[...].astype(BF16), dims, preferred_element_type=F32)
        if nk == 1:
            o_ref[...] = (prod if c_ref is None else prod + c_ref[...].astype(F32)).astype(o_ref.dtype)
            return
        acc_ref = refs[-1]

        @pl.when(kk == 0)
        def _():
            acc_ref[...] = prod if c_ref is None else prod + c_ref[...].astype(F32)

        @pl.when((kk > 0) & (kk < nk - 1))
        def _():
            acc_ref[...] += prod

        @pl.when(kk == nk - 1)
        def _():
            o_ref[...] = (acc_ref[...] + prod).astype(o_ref.dtype)

    a_spec = (pl.BlockSpec((tk, tm), lambda j, i, kk: (kk, i)) if ta
              else pl.BlockSpec((tm, tk), lambda j, i, kk: (i, kk)))
    b_spec = (pl.BlockSpec((tn, tk), lambda j, i, kk: (j, kk)) if tb
              else pl.BlockSpec((tk, tn), lambda j, i, kk: (kk, j)))
    o_spec = pl.BlockSpec((tm, tn), lambda j, i, kk: (i, j))
    in_specs, args = [a_spec, b_spec], [a, b]
    if add is not None:
        in_specs.append(o_spec)
        args.append(add)
    return pl.pallas_call(
        body, name=name, grid=(n // tn, m // tm, nk),
        in_specs=in_specs, out_specs=o_spec,
        out_shape=jax.ShapeDtypeStruct((m, n), out_dtype),
        scratch_shapes=[pltpu.VMEM((tm, tn), F32)] if nk > 1 else [],
        compiler_params=_cparams(("parallel", "parallel", "arbitrary")),
    )(*args)


def _col(arr, cb, width):
    return (arr, cb, width)


def _row_specs(ops, tm):
    return [pl.BlockSpec((tm, w), lambda i, cb=cb: (i, cb)) for (_, cb, w) in ops]


def _par_specs(ops):
    return [pl.BlockSpec((a.shape[0], w), lambda i, cb=cb: (0, cb)) for (a, cb, w) in ops]


def _rowwise_fwd(fn, rows, params, outs, *, tm, name):
    s = rows[0][0].shape[0]
    tm = min(tm, s)
    nr, npar = len(rows), len(params)

    def body(*refs):
        rv = [r[...].astype(F32) for r in refs[:nr]]
        pv = [p[...].astype(F32) for p in refs[nr:nr + npar]]
        res = fn(*rv, *pv)
        for o_ref, val in zip(refs[nr + npar:], res):
            o_ref[...] = val.astype(o_ref.dtype)

    return pl.pallas_call(
        body, name=name, grid=(s // tm,),
        in_specs=_row_specs(rows, tm) + _par_specs(params),
        out_specs=[pl.BlockSpec((tm, w), lambda i: (i, 0)) for (w, _) in outs],
        out_shape=[jax.ShapeDtypeStruct((s, w), dt) for (w, dt) in outs],
        compiler_params=_cparams(("parallel",)),
    )(*[r[0] for r in rows], *[p[0] for p in params])


def _rowwise_bwd(fn, rows, params, couts, row_grads, *, adds=None, tm, name):
    adds = adds or {}
    s = rows[0][0].shape[0]
    tm = min(tm, s)
    nr, npar, nc = len(rows), len(params), len(couts)
    add_keys = sorted(adds)
    want = [i for i, dt in enumerate(row_grads) if dt is not None]

    def body(*refs):
        i = pl.program_id(0)
        rv = [r[...].astype(F32) for r in refs[:nr]]
        pv = [p[...].astype(F32) for p in refs[nr:nr + npar]]
        cv = [c[...].astype(F32) for c in refs[nr + npar:nr + npar + nc]]
        av = {k: r[...].astype(F32) for k, r in zip(add_keys, refs[nr + npar + nc:nr + npar + nc + len(add_keys)])}
        orefs = refs[nr + npar + nc + len(add_keys):]
        _, vjp = jax.vjp(lambda rr, pp: tuple(fn(*rr, *pp)), rv, pv)
        drows, dpars = vjp(tuple(cv))
        for o_ref, ri in zip(orefs[:len(want)], want):
            g = drows[ri]
            if ri in av:
                g = g + av[ri]
            o_ref[...] = g.astype(o_ref.dtype)

        @pl.when(i == 0)
        def _():
            for o_ref in orefs[len(want):]:
                o_ref[...] = jnp.zeros_like(o_ref)

        for o_ref, g in zip(orefs[len(want):], dpars):
            o_ref[...] += g

    add_ops = [adds[k] for k in add_keys]
    out_specs = ([pl.BlockSpec((tm, rows[ri][2]), lambda i: (i, 0)) for ri in want]
                 + [pl.BlockSpec((p[0].shape[0], p[2]), lambda i: (0, 0)) for p in params])
    out_shape = ([jax.ShapeDtypeStruct((s, rows[ri][2]), row_grads[ri]) for ri in want]
                 + [jax.ShapeDtypeStruct((p[0].shape[0], p[2]), F32) for p in params])
    res = pl.pallas_call(
        body, name=name, grid=(s // tm,),
        in_specs=_row_specs(rows, tm) + _par_specs(params) + _row_specs(couts, tm) + _row_specs(add_ops, tm),
        out_specs=out_specs, out_shape=out_shape,
        compiler_params=_cparams(("arbitrary",)),
    )(*[r[0] for r in rows], *[p[0] for p in params], *[c[0] for c in couts], *[a[0] for a in add_ops])
    return res[:len(want)], res[len(want):]


def _silu(v):
    return v * jax.nn.sigmoid(v)


def _rms(v, g):
    return v * lax.rsqrt(jnp.mean(v * v, axis=-1, keepdims=True) + EPS) * g


SUBLANES = 8
CONV_ROWS = 256


def _halo(shifts):
    up = lambda v: -(-v // SUBLANES) * SUBLANES
    return up(max(0, -min(shifts))), up(max(0, max(shifts)))


def _fill_halo(xp_sc, x_ref, front, back):
    s = x_ref.shape[0]
    if front:
        xp_sc[0:front, :] = jnp.zeros((front, LANES), F32)
    if back:
        xp_sc[front + s:front + s + back, :] = jnp.zeros((back, LANES), F32)
    xp_sc[front:front + s, :] = x_ref[...].astype(F32)


def _shift_conv(x, w, b, shifts, *, name):
    s, c = x.shape
    tr = min(CONV_ROWS, s)
    nk = len(shifts)
    front, back = _halo(shifts)

    def body(*refs):
        if b is None:
            x_ref, w_ref, o_ref, xp_sc = refs
        else:
            x_ref, w_ref, b_ref, o_ref, xp_sc = refs
        _fill_halo(xp_sc, x_ref, front, back)

        def chunk(r, carry):
            base = pl.multiple_of(r * tr, tr)
            acc = jnp.zeros((tr, LANES), F32) if b is None else jnp.broadcast_to(b_ref[...], (tr, LANES))
            for kk in range(nk):
                acc = acc + xp_sc[pl.ds(base + front + shifts[kk], tr), :] * w_ref[kk:kk + 1, :]
            o_ref[pl.ds(base, tr), :] = acc
            return carry

        lax.fori_loop(0, s // tr, chunk, 0)

    strip = pl.BlockSpec((s, LANES), lambda cb: (0, cb))
    in_specs = [strip, pl.BlockSpec((nk, LANES), lambda cb: (0, cb))]
    args = [x, w]
    if b is not None:
        in_specs.append(pl.BlockSpec((1, LANES), lambda cb: (0, cb)))
        args.append(b)
    return pl.pallas_call(
        body, name=name, grid=(c // LANES,), in_specs=in_specs, out_specs=strip,
        out_shape=jax.ShapeDtypeStruct((s, c), F32),
        scratch_shapes=[pltpu.VMEM((front + s + back, LANES), F32)],
        compiler_params=_cparams(("parallel",)),
    )(*args)


def _conv_fwd(x, w, b, *, name):
    k = w.shape[0]
    return _shift_conv(x, w, b, [kk - (k - 1) for kk in range(k)], name=name)


def _conv_bwd_x(dy, w, *, name):
    k = w.shape[0]
    return _shift_conv(dy, w, None, [(k - 1) - kk for kk in range(k)], name=name)


def _conv_bwd_w(x, dy, k, *, name):
    s, c = x.shape
    tr = min(CONV_ROWS, s)
    shifts = [kk - (k - 1) for kk in range(k)]
    front, back = _halo(shifts)

    def fold(v):
        return jnp.sum(v.reshape(tr // SUBLANES, SUBLANES, LANES), axis=0)

    def body(x_ref, dy_ref, dw_ref, db_ref, xp_sc, dw_sc, db_sc):
        _fill_halo(xp_sc, x_ref, front, back)
        dw_sc[...] = jnp.zeros_like(dw_sc)
        db_sc[...] = jnp.zeros_like(db_sc)

        def chunk(r, carry):
            base = pl.multiple_of(r * tr, tr)
            dyv = dy_ref[pl.ds(base, tr), :]
            db_sc[...] += fold(dyv)
            for kk in range(k):
                dw_sc[kk] += fold(xp_sc[pl.ds(base + front + shifts[kk], tr), :] * dyv)
            return carry

        lax.fori_loop(0, s // tr, chunk, 0)
        db_ref[...] = jnp.sum(db_sc[...], axis=0, keepdims=True)
        for kk in range(k):
            dw_ref[kk:kk + 1, :] = jnp.sum(dw_sc[kk], axis=0, keepdims=True)

    strip = pl.BlockSpec((s, LANES), lambda cb: (0, cb))
    return pl.pallas_call(
        body, name=name, grid=(c // LANES,), in_specs=[strip, strip],
        out_specs=[pl.BlockSpec((k, LANES), lambda cb: (0, cb)), pl.BlockSpec((1, LANES), lambda cb: (0, cb))],
        out_shape=[jax.ShapeDtypeStruct((k, c), F32), jax.ShapeDtypeStruct((1, c), F32)],
        scratch_shapes=[pltpu.VMEM((front + s + back, LANES), F32), pltpu.VMEM((k, SUBLANES, LANES), F32),
                        pltpu.VMEM((SUBLANES, LANES), F32)],
        compiler_params=_cparams(("parallel",)),
    )(x, dy)


_DIMS = {"nn": ((1,), (0,)), "nt": ((1,), (1,)), "tn": ((0,), (0,))}


def _bd(a, b, mode):
    return lax.dot_general(a.astype(BF16), b.astype(BF16), (_DIMS[mode], ((), ())), preferred_element_type=F32)


@functools.partial(jax.custom_vjp, nondiff_argnums=(2,))
def _bdot(a, b, mode):
    return _bd(a, b, mode)


def _bdot_fwd(a, b, mode):
    return _bd(a, b, mode), (a, b)


def _bdot_bwd(mode, res, g):
    a, b = res
    if mode == "nn":
        return _bd(g, b, "nt"), _bd(a, g, "tn")
    if mode == "nt":
        return _bd(g, b, "nn"), _bd(g, a, "tn")
    return _bd(b, g, "nt"), _bd(a, g, "nn")


_bdot.defvjp(_bdot_fwd, _bdot_bwd)


def _split3(v):
    hi = v.astype(BF16)
    r1 = v - hi.astype(F32)
    mid = r1.astype(BF16)
    lo = (r1 - mid.astype(F32)).astype(BF16)
    return hi, mid, lo


def _mask_dot(mask01, v, mode):
    out = None
    for part in _split3(v):
        if mode == "vn":
            t = lax.dot_general(part, mask01, (_DIMS["nn"], ((), ())), preferred_element_type=F32)
        else:
            t = lax.dot_general(mask01, part, (_DIMS[mode], ((), ())), preferred_element_type=F32)
        out = t if out is None else out + t
    return out


def _lower_tri(n):
    r = lax.broadcasted_iota(jnp.int32, (n, n), 0)
    c = lax.broadcasted_iota(jnp.int32, (n, n), 1)
    return (r >= c).astype(BF16)


@jax.custom_vjp
def _tri_dot(w):
    return _mask_dot(_lower_tri(w.shape[0]), w, "nn")


def _tri_dot_fwd(w):
    return _tri_dot(w), None


def _tri_dot_bwd(_, g):
    return (_mask_dot(_lower_tri(g.shape[0]), g, "tn"),)


_tri_dot.defvjp(_tri_dot_fwd, _tri_dot_bwd)


def _cumsum_lanes(x, *, reverse, name):
    h, s = x.shape
    n = s // LANES

    def body(x_ref, o_ref):
        r = lax.broadcasted_iota(jnp.int32, (LANES, LANES), 0)
        c = lax.broadcasted_iota(jnp.int32, (LANES, LANES), 1)
        m01 = ((r >= c) if reverse else (r <= c)).astype(BF16)

        def step(t, carry):
            ci = (n - 1 - t) if reverse else t
            at = pl.ds(pl.multiple_of(ci * LANES, LANES), LANES)
            blk = x_ref[:, at]
            o_ref[:, at] = _mask_dot(m01, blk, "vn") + carry
            return carry + jnp.sum(blk, axis=1, keepdims=True)

        lax.fori_loop(0, n, step, jnp.zeros((h, 1), F32))

    return pl.pallas_call(body, name=name, out_shape=jax.ShapeDtypeStruct((h, s), F32),
                          compiler_params=_cparams())(x)


def _ssd_chunk(xs, dt, bm, cm, hin, a, dsk, head0):
    n = CHUNK
    row = lax.broadcasted_iota(jnp.int32, (n, n), 0)
    col = lax.broadcasted_iota(jnp.int32, (n, n), 1)
    lower = row >= col
    ustrict = (row > col).astype(F32)
    lane = lax.broadcasted_iota(jnp.int32, (1, LANES), 1)
    sub = lax.broadcasted_iota(jnp.int32, (n, 1), 0)
    e_first = (sub == 0).astype(F32)
    e_last = (sub == n - 1).astype(F32)
    lane0 = (lane == 0).astype(F32)
    cb = _bdot(cm, bm, "nt")
    da = dt * (-jnp.exp(a))
    ys, houts = [], []
    for r in range(SSD_HPG):
        oh = (lane == head0 + r).astype(F32)
        dt_col = jnp.sum(dt * oh, axis=1, keepdims=True)
        da_col = jnp.sum(da * oh, axis=1, keepdims=True)
        dsk_h = jnp.sum(dsk * oh, axis=1, keepdims=True)
        seg = _tri_dot(da_col * ustrict)
        decay = jnp.where(lower, jnp.exp(seg), 0.0)
        cs_col = jnp.sum(seg * lane0, axis=1, keepdims=True) + jnp.sum(da_col * e_first, axis=0, keepdims=True)
        total = jnp.sum(cs_col * e_last, axis=0, keepdims=True)
        xd = xs[r] * dt_col
        y_diag = _bdot(cb * decay, xd, "nn")
        contrib = _bdot(xd * jnp.exp(total - cs_col), bm, "tn")
        houts.append(hin[r] * jnp.exp(total) + contrib)
        y_off = _bdot(cm, hin[r], "nt") * jnp.exp(cs_col)
        ys.append(y_diag + y_off + xs[r] * dsk_h)
    return ys, houts


def _ssd_specs(nc, rev):
    cc = (lambda c: nc - 1 - c) if rev else (lambda c: c)
    hm = pl.BlockSpec((SSD_HPG, CHUNK, SSD_HEADDIM), lambda c, g: (g, cc(c), 0))
    row = pl.BlockSpec((CHUNK, LANES), lambda c, g: (cc(c), 0))
    bmat = pl.BlockSpec((CHUNK, LANES), lambda c, g: (cc(c), D_SSD // LANES + g))
    cmat = pl.BlockSpec((CHUNK, LANES), lambda c, g: (cc(c), D_SSD // LANES + SSD_GROUPS + g))
    par = pl.BlockSpec((1, LANES), lambda c, g: (0, 0))
    hs = pl.BlockSpec((1, SSD_HPG, SSD_HEADDIM, D_STATE), lambda c, g: (cc(c), g, 0, 0))
    return hm, row, bmat, cmat, par, hs


def _ssd_fwd(xs_hm, dt, xbc, a, dsk, *, name):
    s = xs_hm.shape[1]
    nc = s // CHUNK
    hm, row, bmat, cmat, par, hs = _ssd_specs(nc, False)

    def body(xs_ref, dt_ref, bm_ref, cm_ref, a_ref, dsk_ref, y_ref, hs_ref, h_sc):
        c, g = pl.program_id(0), pl.program_id(1)
        mine = pl.ds(g * SSD_HPG, SSD_HPG)

        @pl.when(c == 0)
        def _():
            h_sc[mine] = jnp.zeros((SSD_HPG, SSD_HEADDIM, D_STATE), F32)

        hin = [h_sc[g * SSD_HPG + r] for r in range(SSD_HPG)]
        ys, houts = _ssd_chunk([xs_ref[r] for r in range(SSD_HPG)], dt_ref[...], bm_ref[...], cm_ref[...],
                               hin, a_ref[...], dsk_ref[...], g * SSD_HPG)
        for r in range(SSD_HPG):
            y_ref[r] = ys[r]
            hs_ref[0, r] = hin[r]
            h_sc[g * SSD_HPG + r] = houts[r]

    return pl.pallas_call(
        body, name=name, grid=(nc, SSD_GROUPS),
        in_specs=[hm, row, bmat, cmat, par, par], out_specs=[hm, hs],
        out_shape=[jax.ShapeDtypeStruct(xs_hm.shape, F32),
                   jax.ShapeDtypeStruct((nc, SSD_HEADS, SSD_HEADDIM, D_STATE), F32)],
        scratch_shapes=[pltpu.VMEM((SSD_HEADS, SSD_HEADDIM, D_STATE), F32)],
        compiler_params=_cparams(("arbitrary", "arbitrary")),
    )(xs_hm, dt, xbc, xbc, a, dsk)


def _ssd_bwd(xs_hm, dt, xbc, a, dsk, hsave, dy_hm, *, name):
    s = xs_hm.shape[1]
    nc = s // CHUNK
    hm, row, bmat, cmat, par, hs = _ssd_specs(nc, True)
    gmat = pl.BlockSpec((CHUNK, LANES), lambda c, g: (nc - 1 - c, g))

    def body(xs_ref, dt_ref, bm_ref, cm_ref, a_ref, dsk_ref, hs_ref, dy_ref,
             dxs_ref, ddt_ref, dbm_ref, dcm_ref, da_ref, ddsk_ref, dh_sc):
        c, g = pl.program_id(0), pl.program_id(1)
        mine = pl.ds(g * SSD_HPG, SSD_HPG)

        @pl.when(c == 0)
        def _():
            dh_sc[mine] = jnp.zeros((SSD_HPG, SSD_HEADDIM, D_STATE), F32)

        @pl.when((c == 0) & (g == 0))
        def _():
            da_ref[...] = jnp.zeros_like(da_ref)
            ddsk_ref[...] = jnp.zeros_like(ddsk_ref)

        @pl.when(g == 0)
        def _():
            ddt_ref[...] = jnp.zeros_like(ddt_ref)

        head0 = g * SSD_HPG
        prim = ([xs_ref[r] for r in range(SSD_HPG)], dt_ref[...], bm_ref[...], cm_ref[...],
                [hs_ref[0, r] for r in range(SSD_HPG)], a_ref[...], dsk_ref[...])
        _, vjp = jax.vjp(lambda *p: _ssd_chunk(*p, head0), *prim)
        cot = ([dy_ref[r] for r in range(SSD_HPG)], [dh_sc[g * SSD_HPG + r] for r in range(SSD_HPG)])
        dxs, ddt, dbm, dcm, dhin, da, ddsk = vjp(cot)
        for r in range(SSD_HPG):
            dxs_ref[r] = dxs[r]
            dh_sc[g * SSD_HPG + r] = dhin[r]
        ddt_ref[...] += ddt
        dbm_ref[...] = dbm
        dcm_ref[...] = dcm
        da_ref[...] += da
        ddsk_ref[...] += ddsk

    return pl.pallas_call(
        body, name=name, grid=(nc, SSD_GROUPS),
        in_specs=[hm, row, bmat, cmat, par, par, hs, hm],
        out_specs=[hm, row, gmat, gmat, par, par],
        out_shape=[jax.ShapeDtypeStruct(xs_hm.shape, F32), jax.ShapeDtypeStruct((s, LANES), F32),
                   jax.ShapeDtypeStruct((s, SSD_GROUPS * D_STATE), F32),
                   jax.ShapeDtypeStruct((s, SSD_GROUPS * D_STATE), F32),
                   jax.ShapeDtypeStruct((1, LANES), F32), jax.ShapeDtypeStruct((1, LANES), F32)],
        scratch_shapes=[pltpu.VMEM((SSD_HEADS, SSD_HEADDIM, D_STATE), F32)],
        compiler_params=_cparams(("arbitrary", "arbitrary")),
    )(xs_hm, dt, xbc, xbc, a, dsk, hsave, dy_hm)


SSD_PAIRS = SSD_HPG // 2


def _ssd2_chunk(xs, dt, bm, cm, hin, a, dsk, head0):
    n = CHUNK
    row = lax.broadcasted_iota(jnp.int32, (n, n), 0)
    col = lax.broadcasted_iota(jnp.int32, (n, n), 1)
    lower = row >= col
    ustrict = (row > col).astype(F32)
    lane = lax.broadcasted_iota(jnp.int32, (1, LANES), 1)
    sub = lax.broadcasted_iota(jnp.int32, (n, 1), 0)
    e_first = (sub == 0).astype(F32)
    e_last = (sub == n - 1).astype(F32)
    lane0 = (lane == 0).astype(F32)
    half_l = [(lane < LANES // 2).astype(F32), (lane >= LANES // 2).astype(F32)]
    half_s = [(sub < LANES // 2).astype(F32), (sub >= LANES // 2).astype(F32)]
    cb = _bdot(cm, bm, "nt")
    da = dt * (-jnp.exp(a))
    ys, houts = [], []
    for pr in range(SSD_PAIRS):
        y = jnp.zeros((n, LANES), F32)
        xdte = jnp.zeros((n, LANES), F32)
        lane_gain = jnp.zeros((n, LANES), F32)
        row_gain = jnp.zeros((LANES, 1), F32)
        for hf in range(2):
            oh = (lane == head0 + 2 * pr + hf).astype(F32)
            dt_col = jnp.sum(dt * oh, axis=1, keepdims=True)
            da_col = jnp.sum(da * oh, axis=1, keepdims=True)
            dsk_h = jnp.sum(dsk * oh, axis=1, keepdims=True)
            seg = _tri_dot(da_col * ustrict)
            decay = jnp.where(lower, jnp.exp(seg), 0.0)
            cs_col = jnp.sum(seg * lane0, axis=1, keepdims=True) + jnp.sum(da_col * e_first, axis=0, keepdims=True)
            total = jnp.sum(cs_col * e_last, axis=0, keepdims=True)
            xh = xs[pr] * half_l[hf]
            xd = xh * dt_col
            y = y + _bdot(cb * decay, xd, "nn") + xh * dsk_h
            xdte = xdte + xd * jnp.exp(total - cs_col)
            lane_gain = lane_gain + jnp.exp(cs_col) * half_l[hf]
            row_gain = row_gain + jnp.exp(total) * half_s[hf]
        houts.append(hin[pr] * row_gain + _bdot(xdte, bm, "tn"))
        ys.append(y + _bdot(cm, hin[pr], "nt") * lane_gain)
    return ys, houts


SSD_STEP = 4


def _ssd2_steps(s):
    per = min(SSD_STEP, s // CHUNK)
    return per, s // (CHUNK * per)


def _ssd2_multi(xs, dt, bm, cm, hin, a, dsk, head0):
    ys = []
    for k in range(len(dt)):
        y, hin = _ssd2_chunk(xs[k], dt[k], bm[k], cm[k], hin, a, dsk, head0)
        ys.append(y)
    return ys, hin


def _ssd2_specs(per, nc, rev):
    cc = (lambda c: nc - 1 - c) if rev else (lambda c: c)
    rows = CHUNK * per
    act = pl.BlockSpec((rows, SSD_PAIRS * LANES), lambda c, g: (cc(c), g))
    row = pl.BlockSpec((rows, LANES), lambda c, g: (cc(c), 0))
    bmat = pl.BlockSpec((rows, LANES), lambda c, g: (cc(c), D_SSD // LANES + g))
    cmat = pl.BlockSpec((rows, LANES), lambda c, g: (cc(c), D_SSD // LANES + SSD_GROUPS + g))
    par = pl.BlockSpec((1, LANES), lambda c, g: (0, 0))
    hs = pl.BlockSpec((1, SSD_PAIRS, LANES, D_STATE), lambda c, g: (cc(c), g, 0, 0))
    return act, row, bmat, cmat, par, hs


def _chunk_rows(ref, k):
    return ref[k * CHUNK:(k + 1) * CHUNK, :]


def _pair_cols(ref, k):
    return [ref[k * CHUNK:(k + 1) * CHUNK, pr * LANES:(pr + 1) * LANES] for pr in range(SSD_PAIRS)]


def _ssd2_fwd(dt, xbc, a, dsk, *, name):
    s = xbc.shape[0]
    per, nc = _ssd2_steps(s)
    act, row, bmat, cmat, par, hs = _ssd2_specs(per, nc, False)

    def body(xs_ref, dt_ref, bm_ref, cm_ref, a_ref, dsk_ref, y_ref, hs_ref, h_sc):
        c, g = pl.program_id(0), pl.program_id(1)

        @pl.when(c == 0)
        def _():
            h_sc[pl.ds(g * SSD_PAIRS, SSD_PAIRS)] = jnp.zeros((SSD_PAIRS, LANES, D_STATE), F32)

        hin = [h_sc[g * SSD_PAIRS + pr] for pr in range(SSD_PAIRS)]
        ks = range(per)
        ys, houts = _ssd2_multi([_pair_cols(xs_ref, k) for k in ks], [_chunk_rows(dt_ref, k) for k in ks],
                                [_chunk_rows(bm_ref, k) for k in ks], [_chunk_rows(cm_ref, k) for k in ks],
                                hin, a_ref[...], dsk_ref[...], g * SSD_HPG)
        for pr in range(SSD_PAIRS):
            for k in ks:
                y_ref[k * CHUNK:(k + 1) * CHUNK, pr * LANES:(pr + 1) * LANES] = ys[k][pr]
            hs_ref[0, pr] = hin[pr]
            h_sc[g * SSD_PAIRS + pr] = houts[pr]

    return pl.pallas_call(
        body, name=name, grid=(nc, SSD_GROUPS),
        in_specs=[act, row, bmat, cmat, par, par], out_specs=[act, hs],
        out_shape=[jax.ShapeDtypeStruct((s, D_SSD), F32),
                   jax.ShapeDtypeStruct((nc, SSD_HEADS // 2, LANES, D_STATE), F32)],
        scratch_shapes=[pltpu.VMEM((SSD_HEADS // 2, LANES, D_STATE), F32)],
        compiler_params=_cparams(("arbitrary", "arbitrary")),
    )(xbc, dt, xbc, xbc, a, dsk)


def _ssd2_bwd(dt, xbc, a, dsk, hsave, dy, *, carry=None, name):
    s = xbc.shape[0]
    per, nc = _ssd2_steps(s)
    act, row, bmat, cmat, par, hs = _ssd2_specs(per, nc, True)
    gmat = pl.BlockSpec((CHUNK * per, LANES), lambda c, g: (nc - 1 - c, g))

    def body(*refs):
        ins, outs, (dh_sc,), comm = _carried(carry, refs, 8, 6, 1)
        xs_ref, dt_ref, bm_ref, cm_ref, a_ref, dsk_ref, hs_ref, dy_ref = ins
        dxs_ref, ddt_ref, dbm_ref, dcm_ref, da_ref, ddsk_ref = outs
        c, g = pl.program_id(0), pl.program_id(1)
        if carry is not None:
            @pl.when((c == 0) & (g == 0))
            def _():
                carry.start(*comm)

        @pl.when(c == 0)
        def _():
            dh_sc[pl.ds(g * SSD_PAIRS, SSD_PAIRS)] = jnp.zeros((SSD_PAIRS, LANES, D_STATE), F32)

        @pl.when((c == 0) & (g == 0))
        def _():
            da_ref[...] = jnp.zeros_like(da_ref)
            ddsk_ref[...] = jnp.zeros_like(ddsk_ref)

        @pl.when(g == 0)
        def _():
            ddt_ref[...] = jnp.zeros_like(ddt_ref)

        head0 = g * SSD_HPG
        ks = range(per)
        prim = ([_pair_cols(xs_ref, k) for k in ks], [_chunk_rows(dt_ref, k) for k in ks],
                [_chunk_rows(bm_ref, k) for k in ks], [_chunk_rows(cm_ref, k) for k in ks],
                [hs_ref[0, pr] for pr in range(SSD_PAIRS)], a_ref[...], dsk_ref[...])
        _, vjp = jax.vjp(lambda *p: _ssd2_multi(*p, head0), *prim)
        cot = ([_pair_cols(dy_ref, k) for k in ks], [dh_sc[g * SSD_PAIRS + pr] for pr in range(SSD_PAIRS)])
        dxs, ddt, dbm, dcm, dhin, da, ddsk = vjp(cot)
        for pr in range(SSD_PAIRS):
            for k in ks:
                dxs_ref[k * CHUNK:(k + 1) * CHUNK, pr * LANES:(pr + 1) * LANES] = dxs[k][pr]
            dh_sc[g * SSD_PAIRS + pr] = dhin[pr]
        for k in ks:
            ddt_ref[k * CHUNK:(k + 1) * CHUNK, :] += ddt[k]
            dbm_ref[k * CHUNK:(k + 1) * CHUNK, :] = dbm[k]
            dcm_ref[k * CHUNK:(k + 1) * CHUNK, :] = dcm[k]
        da_ref[...] += da
        ddsk_ref[...] += ddsk
        if carry is not None:
            @pl.when((c == nc - 1) & (g == SSD_GROUPS - 1))
            def _():
                carry.wait(*comm)

    extra = carry if carry is not None else _Exchange([], "c", False)
    return pl.pallas_call(
        body, name=name, grid=(nc, SSD_GROUPS),
        in_specs=[act, row, bmat, cmat, par, par, hs, act] + extra.in_specs,
        out_specs=[act, row, gmat, gmat, par, par] + extra.out_specs,
        out_shape=[jax.ShapeDtypeStruct((s, D_SSD), F32), jax.ShapeDtypeStruct((s, LANES), F32),
                   jax.ShapeDtypeStruct((s, SSD_GROUPS * D_STATE), F32),
                   jax.ShapeDtypeStruct((s, SSD_GROUPS * D_STATE), F32),
                   jax.ShapeDtypeStruct((1, LANES), F32), jax.ShapeDtypeStruct((1, LANES), F32)] + extra.out_shape,
        scratch_shapes=[pltpu.VMEM((SSD_HEADS // 2, LANES, D_STATE), F32)]
        + (carry.scratch if carry is not None else []),
        compiler_params=_cparams(("arbitrary", "arbitrary")),
    )(xbc, dt, xbc, xbc, a, dsk, hsave, dy, *extra.arrs)


FOX_BLOCK = 512
NEG = -1e30


def _fox_scores(q, k, cref, ck, strictly_below):
    t = q.shape[0]
    s = lax.dot_general(q, k, (_DIMS["nt"], ((), ())), preferred_element_type=F32) * (FOX_HEADDIM ** -0.5)
    s = s + (cref - ck)
    row = lax.broadcasted_iota(jnp.int32, (t, t), 0)
    col = lax.broadcasted_iota(jnp.int32, (t, t), 1)
    mask = (row >= col) | strictly_below
    return s, mask


def _fox_fwd(q, k, v, c, *, name):
    h, s, p = q.shape
    t = min(FOX_BLOCK, s)
    nb = s // t

    def body(q_ref, k_ref, v_ref, cq_ref, ck_ref, o_ref, lse_ref, m_sc, l_sc, acc_sc):
        i, j = pl.program_id(1), pl.program_id(2)

        @pl.when(j == 0)
        def _():
            m_sc[...] = jnp.full_like(m_sc, NEG)
            l_sc[...] = jnp.zeros_like(l_sc)
            acc_sc[...] = jnp.zeros_like(acc_sc)

        @pl.when(j <= i)
        def _():
            sc, mask = _fox_scores(q_ref[0], k_ref[0], cq_ref[0, 0:1, 0:1], ck_ref[0], j < i)
            sc = jnp.where(mask, sc, NEG)
            m_old = m_sc[...]
            m_new = jnp.maximum(m_old, jnp.max(sc, axis=1, keepdims=True))
            alpha = jnp.exp(m_old - m_new)
            pr = jnp.exp(sc - m_new)
            l_sc[...] = alpha * l_sc[...] + jnp.sum(pr, axis=1, keepdims=True)
            pr_hi = pr.astype(BF16)
            pr_lo = (pr - pr_hi.astype(F32)).astype(BF16)
            pv = (lax.dot_general(pr_hi, v_ref[0], (_DIMS["nn"], ((), ())), preferred_element_type=F32)
                  + lax.dot_general(pr_lo, v_ref[0], (_DIMS["nn"], ((), ())), preferred_element_type=F32))
            acc_sc[...] = alpha * acc_sc[...] + pv
            m_sc[...] = m_new

        @pl.when(j == i)
        def _():
            o_ref[0] = acc_sc[...] / l_sc[...]
            lse_ref[0] = jnp.broadcast_to(m_sc[...] + jnp.log(l_sc[...]), (t, LANES))

    qspec = pl.BlockSpec((1, t, p), lambda hh, i, j: (hh, i, 0))
    kspec = pl.BlockSpec((1, t, p), lambda hh, i, j: (hh, jnp.minimum(j, i), 0))
    return pl.pallas_call(
        body, name=name, grid=(h, nb, nb),
        in_specs=[qspec, kspec, kspec,
                  pl.BlockSpec((1, 1, t), lambda hh, i, j: (hh, 0, i)),
                  pl.BlockSpec((1, 1, t), lambda hh, i, j: (hh, 0, jnp.minimum(j, i)))],
        out_specs=[qspec, pl.BlockSpec((1, t, LANES), lambda hh, i, j: (hh, i, 0))],
        out_shape=[jax.ShapeDtypeStruct((h, s, p), F32), jax.ShapeDtypeStruct((h, s, LANES), F32)],
        scratch_shapes=[pltpu.VMEM((t, 1), F32), pltpu.VMEM((t, 1), F32), pltpu.VMEM((t, p), F32)],
        compiler_params=_cparams(("parallel", "arbitrary", "arbitrary")),
    )(q, k, v, c, c)


def _fox_bwd(q, k, v, c, o, lse, do, *, name):
    h, s, p = q.shape
    t = min(FOX_BLOCK, s)
    nb = s // t
    scale = FOX_HEADDIM ** -0.5

    def body(q_ref, k_ref, v_ref, cq_ref, ck_ref, o_ref, lse_ref, do_ref,
             dq_ref, dk_ref, dv_ref, dc_ref, dk_sc, dv_sc, dc_sc):
        j, i = pl.program_id(1), pl.program_id(2)

        @pl.when(i == 0)
        def _():
            dk_sc[...] = jnp.zeros_like(dk_sc)
            dv_sc[...] = jnp.zeros_like(dv_sc)
            dc_sc[...] = jnp.zeros_like(dc_sc)

        @pl.when(i >= j)
        def _():
            qv, kv, vv = q_ref[0], k_ref[0], v_ref[0]
            sc, mask = _fox_scores(qv, kv, cq_ref[0, 0:1, 0:1], ck_ref[0], i > j)
            pr = jnp.where(mask, jnp.exp(sc - lse_ref[0, :, 0:1]), 0.0)
            dov = do_ref[0]
            dob = dov.astype(BF16)
            prb = pr.astype(BF16)
            dv_sc[...] += lax.dot_general(prb, dob, (_DIMS["tn"], ((), ())), preferred_element_type=F32)
            dp = lax.dot_general(dob, vv, (_DIMS["nt"], ((), ())), preferred_element_type=F32)
            dcol = jnp.sum(dob.astype(F32) * o_ref[0], axis=1, keepdims=True)
            ds = pr * (dp - dcol)
            dc_sc[...] -= jnp.sum(ds, axis=0, keepdims=True)
            dsb = ds.astype(BF16)
            dqc = scale * lax.dot_general(dsb, kv, (_DIMS["nn"], ((), ())), preferred_element_type=F32)
            at = pl.ds(pl.multiple_of(i * t, t), t)

            @pl.when(j == 0)
            def _():
                dq_ref[0, at, :] = dqc

            @pl.when(j > 0)
            def _():
                dq_ref[0, at, :] += dqc

            dk_sc[...] += scale * lax.dot_general(dsb, qv, (_DIMS["tn"], ((), ())), preferred_element_type=F32)

        @pl.when(i == nb - 1)
        def _():
            dk_ref[0] = dk_sc[...]
            dv_ref[0] = dv_sc[...]
            dc_ref[0] = dc_sc[...]

    qspec = pl.BlockSpec((1, t, p), lambda hh, j, i: (hh, jnp.maximum(i, j), 0))
    kspec = pl.BlockSpec((1, t, p), lambda hh, j, i: (hh, j, 0))
    cq = pl.BlockSpec((1, 1, t), lambda hh, j, i: (hh, 0, jnp.maximum(i, j)))
    ck = pl.BlockSpec((1, 1, t), lambda hh, j, i: (hh, 0, j))
    return pl.pallas_call(
        body, name=name, grid=(h, nb, nb),
        in_specs=[qspec, kspec, kspec, cq, ck, qspec,
                  pl.BlockSpec((1, t, LANES), lambda hh, j, i: (hh, jnp.maximum(i, j), 0)), qspec],
        out_specs=[pl.BlockSpec((1, s, p), lambda hh, j, i: (hh, 0, 0)), kspec, kspec, ck],
        out_shape=[jax.ShapeDtypeStruct((h, s, p), F32), jax.ShapeDtypeStruct((h, s, p), F32),
                   jax.ShapeDtypeStruct((h, s, p), F32), jax.ShapeDtypeStruct((h, 1, s), F32)],
        scratch_shapes=[pltpu.VMEM((t, p), F32), pltpu.VMEM((t, p), F32), pltpu.VMEM((1, t), F32)],
        compiler_params=_cparams(("parallel", "arbitrary", "arbitrary")),
    )(q, k, v, c, c, o, lse, do)


AUX = 64


def _pack(main, cols):
    h, s, p = main.shape
    parts = [main.astype(BF16)]
    if cols:
        parts.append(jnp.stack(cols, axis=-1).astype(BF16))
    parts.append(jnp.zeros((h, s, LANES - p - len(cols)), BF16))
    return jnp.concatenate(parts, axis=-1)


def _terms(v):
    hi = lax.reduce_precision(v, 8, 7)
    mid = lax.reduce_precision(v - hi, 8, 7)
    lo = lax.reduce_precision(v - hi - mid, 8, 7)
    return [hi, mid, lo]


def _fox_pack_qkv(q, k, v):
    h, s, _ = q.shape
    one = jnp.ones((h, s), F32)
    return _pack(q * (FOX_HEADDIM ** -0.5), []), _pack(k, []), _pack(v, [one, one, one])


def _fox_bias(c_ref, qblock, kblock, t):
    lane = lax.broadcasted_iota(jnp.int32, (1, LANES), 1)
    cq = c_ref[0, :, pl.ds(pl.multiple_of(qblock * t, LANES), LANES)]
    cref = jnp.sum(jnp.where(lane == 0, cq, 0.0), axis=1, keepdims=True)
    return cref - c_ref[0, :, pl.ds(pl.multiple_of(kblock * t, LANES), t)]


def _fox_rowdot(do, o, *, tm=256, name):
    s, d = do.shape
    tm = min(tm, s)

    def body(do_ref, o_ref, d_ref):
        prod = do_ref[...].astype(BF16).astype(F32) * o_ref[...]
        r = lax.broadcasted_iota(jnp.int32, (d, LANES), 0)
        c = lax.broadcasted_iota(jnp.int32, (d, LANES), 1)
        mine = (r >= c * FOX_HEADDIM) & (r < (c + 1) * FOX_HEADDIM)
        d_ref[...] = _mask_dot(mine.astype(BF16), prod, "vn")

    row = pl.BlockSpec((tm, d), lambda i: (i, 0))
    return pl.pallas_call(body, name=name, grid=(s // tm,), in_specs=[row, row],
                          out_specs=pl.BlockSpec((tm, LANES), lambda i: (i, 0)),
                          out_shape=jax.ShapeDtypeStruct((s, LANES), F32),
                          compiler_params=_cparams(("parallel",)))(do, o)


def _causal(t):
    return lax.broadcasted_iota(jnp.int32, (t, t), 0) >= lax.broadcasted_iota(jnp.int32, (t, t), 1)


def _fox2_fwd(qp, kp, vp, c, *, name):
    h, s, _ = qp.shape
    t = min(FOX_BLOCK, s)
    nb = s // t
    nt = (((1,), (1,)), ((), ()))
    nn = (((1,), (0,)), ((), ()))

    def body(q_ref, k_ref, v_ref, c_ref, o_ref, lse_ref, m_sc, acc_sc):
        i = pl.program_id(1)
        m_sc[...] = jnp.full_like(m_sc, NEG)
        acc_sc[...] = jnp.zeros_like(acc_sc)
        qv = q_ref[0]

        def step(j, masked):
            at = pl.ds(pl.multiple_of(j * t, t), t)
            kv, vv = k_ref[0, at, :], v_ref[0, at, :]
            sc = lax.dot_general(qv, kv, nt, preferred_element_type=F32) + _fox_bias(c_ref, i, j, t)
            if masked:
                sc = jnp.where(_causal(t), sc, NEG)
            m_prev = m_sc[...]
            m_new = jnp.maximum(m_prev, jnp.max(sc, axis=1, keepdims=True))
            pr = jnp.exp(sc - jnp.tile(m_new, (1, t // LANES)))
            pr_hi = pr.astype(BF16)
            pr_lo = (pr - pr_hi.astype(F32)).astype(BF16)
            pv = (lax.dot_general(pr_hi, vv, nn, preferred_element_type=F32)
                  + lax.dot_general(pr_lo, vv, nn, preferred_element_type=F32))
            acc_sc[...] = jnp.exp(m_prev - m_new) * acc_sc[...] + pv
            m_sc[...] = m_new

        lax.fori_loop(0, i, lambda j, carry: (step(j, False), carry)[1], 0)
        step(i, True)
        acc = acc_sc[...]
        lane = lax.broadcasted_iota(jnp.int32, (1, LANES), 1)
        den = jnp.sum(jnp.where(lane == AUX, acc, 0.0), axis=1, keepdims=True)
        o_ref[0] = (acc / den)[:, :FOX_HEADDIM]
        lse_ref[0] = m_sc[...] + jnp.log(den)

    whole = pl.BlockSpec((1, s, LANES), lambda hh, i: (hh, 0, 0))
    return pl.pallas_call(
        body, name=name, grid=(h, nb),
        in_specs=[pl.BlockSpec((1, t, LANES), lambda hh, i: (hh, i, 0)), whole, whole,
                  pl.BlockSpec((1, 1, s), lambda hh, i: (hh, 0, 0))],
        out_specs=[pl.BlockSpec((1, t, FOX_HEADDIM), lambda hh, i: (hh, i, 0)),
                   pl.BlockSpec((1, t, LANES), lambda hh, i: (hh, i, 0))],
        out_shape=[jax.ShapeDtypeStruct((h, s, FOX_HEADDIM), F32), jax.ShapeDtypeStruct((h, s, LANES), F32)],
        scratch_shapes=[pltpu.VMEM((t, LANES), F32), pltpu.VMEM((t, LANES), F32)],
        compiler_params=_cparams(("parallel", "arbitrary")),
    )(qp, kp, vp, c)


def _fox2_bwd(qp, kp, vp, c, dop, lse, *, name):
    h, s, _ = qp.shape
    t = min(FOX_BLOCK, s)
    nb = s // t
    nt = (((1,), (1,)), ((), ()))
    nn = (((1,), (0,)), ((), ()))
    tn = (((0,), (0,)), ((), ()))

    def body(k_ref, v_ref, q_ref, c_ref, do_ref, lse_ref, dq_ref, dk_ref, dv_ref, dc_ref, dk_sc, dv_sc, dc_sc):
        j = pl.program_id(1)

        @pl.when(j == 0)
        def _():
            dq_ref[...] = jnp.zeros_like(dq_ref)

        dk_sc[...] = jnp.zeros_like(dk_sc)
        dv_sc[...] = jnp.zeros_like(dv_sc)
        dc_sc[...] = jnp.zeros_like(dc_sc)
        kv, vv = k_ref[0], v_ref[0]

        def step(i, masked):
            at = pl.ds(pl.multiple_of(i * t, t), t)
            qv, dov = q_ref[0, at, :], do_ref[0, at, :]
            sc = lax.dot_general(qv, kv, nt, preferred_element_type=F32) + _fox_bias(c_ref, i, j, t)
            pr = jnp.exp(sc - jnp.tile(lse_ref[0, at, :], (1, t // LANES)))
            if masked:
                pr = jnp.where(_causal(t), pr, 0.0)
            ds = pr * lax.dot_general(dov, vv, nt, preferred_element_type=F32)
            dc_sc[...] -= jnp.sum(ds, axis=0, keepdims=True)
            dsb = ds.astype(BF16)
            dv_sc[...] += lax.dot_general(pr.astype(BF16), dov, tn, preferred_element_type=F32)
            dk_sc[...] += lax.dot_general(dsb, qv, tn, preferred_element_type=F32)
            dq_ref[0, at, :] += lax.dot_general(dsb, kv, nn, preferred_element_type=F32)

        step(j, True)
        lax.fori_loop(j + 1, nb, lambda i, carry: (step(i, False), carry)[1], 0)
        dk_ref[0] = dk_sc[...]
        dv_ref[0] = dv_sc[...].astype(dv_ref.dtype)
        dc_ref[0] = dc_sc[...]

    whole = pl.BlockSpec((1, s, LANES), lambda hh, j: (hh, 0, 0))
    blk = pl.BlockSpec((1, t, LANES), lambda hh, j: (hh, j, 0))
    return pl.pallas_call(
        body, name=name, grid=(h, nb),
        in_specs=[blk, blk, whole, pl.BlockSpec((1, 1, s), lambda hh, j: (hh, 0, 0)), whole, whole],
        out_specs=[whole, blk, blk, pl.BlockSpec((1, 1, t), lambda hh, j: (hh, 0, j))],
        out_shape=[jax.ShapeDtypeStruct((h, s, LANES), F32), jax.ShapeDtypeStruct((h, s, LANES), F32),
                   jax.ShapeDtypeStruct((h, s, LANES), BF16), jax.ShapeDtypeStruct((h, 1, s), F32)],
        scratch_shapes=[pltpu.VMEM((t, LANES), F32), pltpu.VMEM((t, LANES), F32), pltpu.VMEM((1, t), F32)],
        compiler_params=_cparams(("parallel", "arbitrary")),
    )(kp, vp, qp, c, dop, lse)


PAIRS = FOX_HEADS // 2
HALF = LANES // 2


def _first_half():
    return lax.broadcasted_iota(jnp.int32, (1, LANES), 1) < HALF


def _pair_bias(c_ref, hh, qblock, kblock, t):
    lane = lax.broadcasted_iota(jnp.int32, (1, LANES), 1)
    cq = c_ref[hh, :, pl.ds(pl.multiple_of(qblock * t, LANES), LANES)]
    cref = jnp.sum(jnp.where(lane == 0, cq, 0.0), axis=1, keepdims=True)
    return cref - c_ref[hh, :, pl.ds(pl.multiple_of(kblock * t, LANES), t)]


def _fox_dopack(do, o, *, tm=256, name):
    s, d = do.shape
    tm = min(tm, s)

    def body(do_ref, o_ref, out_ref):
        dov = do_ref[...].astype(BF16)
        prod = dov.astype(F32) * o_ref[...]
        r = lax.broadcasted_iota(jnp.int32, (d, LANES), 0)
        c = lax.broadcasted_iota(jnp.int32, (d, LANES), 1)
        heads = ((r >= c * FOX_HEADDIM) & (r < (c + 1) * FOX_HEADDIM)).astype(BF16)
        negd = -_mask_dot(heads, prod, "vn")
        hr = lax.broadcasted_iota(jnp.int32, (LANES, 2 * d), 0)
        col = lax.broadcasted_iota(jnp.int32, (LANES, 2 * d), 1)
        base = (hr >> 1) * (2 * LANES) + jnp.where((hr & 1) == 0, HALF, LANES)
        terms = None
        for kk, part in enumerate(_split3(negd)):
            place = ((col == base + kk) & (hr < FOX_HEADS)).astype(BF16)
            tk = lax.dot_general(part, place, (_DIMS["nn"], ((), ())), preferred_element_type=F32)
            terms = tk if terms is None else terms + tk
        first = _first_half()
        zero = jnp.zeros((tm, LANES), BF16)
        pieces = []
        for hp in range(PAIRS):
            blk = dov[:, hp * LANES:(hp + 1) * LANES]
            pieces += [jnp.where(first, blk, zero), jnp.where(first, zero, blk)]
        out_ref[...] = (jnp.concatenate(pieces, axis=1).astype(F32) + terms).astype(BF16)

    row = pl.BlockSpec((tm, d), lambda i: (i, 0))
    return pl.pallas_call(body, name=name, grid=(s // tm,), in_specs=[row, row],
                          out_specs=pl.BlockSpec((tm, 2 * d), lambda i: (i, 0)),
                          out_shape=jax.ShapeDtypeStruct((s, 2 * d), BF16),
                          compiler_params=_cparams(("parallel",)))(do, o)


def _fox3_fwd(qkv, c, *, carry=None, name):
    assert qkv.shape[0] // min(FOX_BLOCK, qkv.shape[0]) < LANES
    s = qkv.shape[0]
    t = min(FOX_BLOCK, s)
    nb = s // t
    nt = (((1,), (1,)), ((), ()))
    nn = (((1,), (0,)), ((), ()))
    scale = FOX_HEADDIM ** -0.5

    def body(*refs):
        (q_ref, k_ref, v_ref, c_ref), (o_ref, mrun_ref), (m_sc, acc_sc, mt_sc), comm = _carried(carry, refs, 4, 2, 3)
        i = pl.program_id(1)
        if carry is not None:
            @pl.when((pl.program_id(0) == 0) & (i == 0))
            def _():
                carry.start(*comm)

        first = _first_half()
        lane = lax.broadcasted_iota(jnp.int32, (1, LANES), 1)
        m_sc[...] = jnp.full_like(m_sc, NEG)
        acc_sc[...] = jnp.zeros_like(acc_sc)
        mt_sc[...] = jnp.zeros_like(mt_sc)
        q2 = q_ref[...] * scale
        zero = jnp.zeros_like(q2)
        qs = [jnp.where(first, q2, zero), jnp.where(first, zero, q2)]

        def step(j, masked):
            at = pl.ds(pl.multiple_of(j * t, t), t)
            k2, v2 = k_ref[at, :], v_ref[at, :]
            one = jnp.ones_like(v2)
            vx = [jnp.where(first, v2, one), jnp.where(first, one, v2)]
            for hh in range(2):
                sc = lax.dot_general(qs[hh], k2, nt, preferred_element_type=F32) + _pair_bias(c_ref, hh, i, j, t)
                if masked:
                    sc = jnp.where(_causal(t), sc, NEG)
                m_prev = m_sc[hh]
                m_new = jnp.maximum(m_prev, jnp.max(sc, axis=1, keepdims=True))
                pr = jnp.exp(sc - jnp.tile(m_new, (1, t // LANES))).astype(BF16)
                pv = lax.dot_general(pr, vx[hh], nn, preferred_element_type=F32)
                acc_sc[hh] = jnp.exp(m_prev - m_new) * acc_sc[hh] + pv
                m_sc[hh] = m_new
                mt_sc[hh] = jnp.where(lane == j, m_new, mt_sc[hh])

        lax.fori_loop(0, i, lambda j, carry: (step(j, False), carry)[1], 0)
        step(i, True)
        acc_a, acc_b = acc_sc[0], acc_sc[1]
        den_a = jnp.where(first, pltpu.roll(acc_a, HALF, 1), acc_a)
        den_b = jnp.where(first, acc_b, pltpu.roll(acc_b, HALF, 1))
        o_ref[...] = jnp.where(first, acc_a / den_a, acc_b / den_b)
        mrun_ref[:, :LANES] = jnp.where(lane == LANES - 1, m_sc[0] + jnp.log(den_a), mt_sc[0])
        mrun_ref[:, LANES:] = jnp.where(lane == LANES - 1, m_sc[1] + jnp.log(den_b), mt_sc[1])
        if carry is not None:
            @pl.when((pl.program_id(0) == PAIRS - 1) & (i == nb - 1))
            def _():
                carry.wait(*comm)

    nq = D_FOX // LANES
    extra = carry if carry is not None else _Exchange([], "c", False)
    return pl.pallas_call(
        body, name=name, grid=(PAIRS, nb),
        in_specs=[pl.BlockSpec((t, LANES), lambda hp, i: (i, hp)),
                  pl.BlockSpec((s, LANES), lambda hp, i: (0, nq + hp)),
                  pl.BlockSpec((s, LANES), lambda hp, i: (0, 2 * nq + hp)),
                  pl.BlockSpec((2, 1, s), lambda hp, i: (hp, 0, 0))] + extra.in_specs,
        out_specs=[pl.BlockSpec((t, LANES), lambda hp, i: (i, hp)),
                   pl.BlockSpec((t, 2 * LANES), lambda hp, i: (i, hp))] + extra.out_specs,
        out_shape=[jax.ShapeDtypeStruct((s, D_FOX), F32), jax.ShapeDtypeStruct((s, 2 * D_FOX), F32)] + extra.out_shape,
        scratch_shapes=[pltpu.VMEM((2, t, LANES), F32)] * 3 + (carry.scratch if carry is not None else []),
        compiler_params=_cparams(("arbitrary", "arbitrary")),
    )(qkv, qkv, qkv, c, *extra.arrs)


def _fox3_bwd(qkv, c, dox, mrun, *, carry=None, name):
    s = qkv.shape[0]
    t = min(FOX_BLOCK, s)
    nb = s // t
    nt = (((1,), (1,)), ((), ()))
    nn = (((1,), (0,)), ((), ()))
    tn = (((0,), (0,)), ((), ()))
    scale = FOX_HEADDIM ** -0.5

    def body(*refs):
        ins, outs, scratch, comm = _carried(carry, refs, 6, 4, 3)
        k_ref, v_ref, q_ref, c_ref, do_ref, mrun_ref = ins
        dq_ref, dk_ref, dv_ref, dc_ref = outs
        dk_sc, dv_sc, dc_sc = scratch
        j = pl.program_id(1)
        if carry is not None:
            @pl.when((pl.program_id(0) == 0) & (j == 0))
            def _():
                carry.start(*comm)

        first = _first_half()
        halves = [first, jnp.logical_not(first)]
        lane = lax.broadcasted_iota(jnp.int32, (1, LANES), 1)

        @pl.when(j == 0)
        def _():
            dq_ref[...] = jnp.zeros_like(dq_ref)

        dk_sc[...] = jnp.zeros_like(dk_sc)
        dv_sc[...] = jnp.zeros_like(dv_sc)
        dc_sc[...] = jnp.zeros_like(dc_sc)
        k2, v2 = k_ref[...], v_ref[...]
        one = jnp.ones_like(v2)
        vx = [jnp.where(first, v2, one), jnp.where(first, one, v2)]

        def step(i, masked):
            at = pl.ds(pl.multiple_of(i * t, t), t)
            q2 = q_ref[at, :] * scale
            zero = jnp.zeros_like(q2)
            qs = [jnp.where(first, q2, zero), jnp.where(first, zero, q2)]
            for hh in range(2):
                mine = slice(hh * LANES, (hh + 1) * LANES)
                dov = do_ref[at, mine]
                sc = lax.dot_general(qs[hh], k2, nt, preferred_element_type=F32) + _pair_bias(c_ref, hh, i, j, t)
                stats = mrun_ref[at, mine]
                mj = jnp.sum(jnp.where(lane == j, stats, 0.0), axis=1, keepdims=True)
                lse = jnp.sum(jnp.where(lane == LANES - 1, stats, 0.0), axis=1, keepdims=True)
                gain = jnp.broadcast_to(jnp.exp(mj - lse), (t, LANES))
                mj = jnp.broadcast_to(mj, (t, LANES))
                pb = jnp.exp(sc - jnp.tile(mj, (1, t // LANES))).astype(BF16)
                if masked:
                    pb = jnp.where(_causal(t), pb, jnp.zeros_like(pb))
                pr = pb.astype(F32) * jnp.tile(gain, (1, t // LANES))
                ds = pr * lax.dot_general(dov, vx[hh], nt, preferred_element_type=F32)
                dc_sc[hh] -= jnp.sum(ds, axis=0, keepdims=True)
                dsb = ds.astype(BF16)
                dvh = lax.dot_general(pr.astype(BF16), dov, tn, preferred_element_type=F32)
                dv_sc[...] += jnp.where(halves[hh], dvh, 0.0)
                dk_sc[...] += lax.dot_general(dsb, qs[hh], tn, preferred_element_type=F32)
                dqh = lax.dot_general(dsb, k2, nn, preferred_element_type=F32)
                dq_ref[at, :] += jnp.where(halves[hh], dqh, 0.0)

        step(j, True)
        lax.fori_loop(j + 1, nb, lambda i, carry: (step(i, False), carry)[1], 0)
        dk_ref[...] = dk_sc[...]
        dv_ref[...] = dv_sc[...].astype(dv_ref.dtype)
        dc_ref[...] = dc_sc[...]
        if carry is not None:
            @pl.when((pl.program_id(0) == PAIRS - 1) & (j == nb - 1))
            def _():
                carry.wait(*comm)

    nq = D_FOX // LANES
    blk = pl.BlockSpec((t, LANES), lambda hp, j: (j, hp))
    extra = carry if carry is not None else _Exchange([], "c", False)
    return pl.pallas_call(
        body, name=name, grid=(PAIRS, nb),
        in_specs=[pl.BlockSpec((t, LANES), lambda hp, j: (j, nq + hp)),
                  pl.BlockSpec((t, LANES), lambda hp, j: (j, 2 * nq + hp)),
                  pl.BlockSpec((s, LANES), lambda hp, j: (0, hp)),
                  pl.BlockSpec((2, 1, s), lambda hp, j: (hp, 0, 0)),
                  pl.BlockSpec((s, 2 * LANES), lambda hp, j: (0, hp)),
                  pl.BlockSpec((s, 2 * LANES), lambda hp, j: (0, hp))] + extra.in_specs,
        out_specs=[pl.BlockSpec((s, LANES), lambda hp, j: (0, hp)), blk, blk,
                   pl.BlockSpec((2, 1, t), lambda hp, j: (hp, 0, j))] + extra.out_specs,
        out_shape=[jax.ShapeDtypeStruct((s, D_FOX), F32), jax.ShapeDtypeStruct((s, D_FOX), F32),
                   jax.ShapeDtypeStruct((s, D_FOX), BF16), jax.ShapeDtypeStruct((FOX_HEADS, 1, s), F32)]
        + extra.out_shape,
        scratch_shapes=[pltpu.VMEM((t, LANES), F32), pltpu.VMEM((t, LANES), F32), pltpu.VMEM((2, 1, t), F32)]
        + (carry.scratch if carry is not None else []),
        compiler_params=_cparams(("arbitrary", "arbitrary")),
    )(qkv, qkv, qkv, c, dox, mrun, *extra.arrs)


def _final(x1, out1, g, tgt, *, tm=256, name):
    s, d = x1.shape
    tm = min(tm, s)

    def body(x_ref, o_ref, g_ref, t_ref, dx_ref, do_ref, dg_ref, loss_ref):
        i = pl.program_id(0)

        @pl.when(i == 0)
        def _():
            dg_ref[...] = jnp.zeros_like(dg_ref)
            loss_ref[...] = jnp.zeros_like(loss_ref)

        tv = t_ref[...]

        def lossf(xv, ov, gv):
            err = jnp.square(xv + _rms(ov, gv) - tv)
            return 0.5 * jnp.sum(jnp.mean(err, axis=-1, keepdims=True), axis=0, keepdims=True)

        val, vjp = jax.vjp(lossf, x_ref[...], o_ref[...], g_ref[...])
        dx, do, dg = vjp(jnp.ones((1, 1), F32))
        dx_ref[...] = dx
        do_ref[...] = do.astype(do_ref.dtype)
        dg_ref[...] += dg
        loss_ref[...] += val

    row = pl.BlockSpec((tm, d), lambda i: (i, 0))
    par = pl.BlockSpec((1, d), lambda i: (0, 0))
    return pl.pallas_call(
        body, name=name, grid=(s // tm,), in_specs=[row, row, par, row],
        out_specs=[row, row, par, pl.BlockSpec((1, 1), lambda i: (0, 0))],
        out_shape=[jax.ShapeDtypeStruct((s, d), F32), jax.ShapeDtypeStruct((s, d), BF16),
                   jax.ShapeDtypeStruct((1, d), F32), jax.ShapeDtypeStruct((1, 1), F32)],
        compiler_params=_cparams(("arbitrary",)),
    )(x1, out1, g, tgt)


def _row_tile(r):
    return LANES if r % LANES == 0 else r


def _sum_slots(parts, *, out_dtype=F32, name):
    p, r, c = parts.shape
    tr = _row_tile(r)

    def body(p_ref, o_ref):
        acc = p_ref[0].astype(F32)
        for k in range(1, p):
            acc = acc + p_ref[k].astype(F32)
        o_ref[...] = acc.astype(o_ref.dtype)

    return pl.pallas_call(
        body, name=name, grid=(r // tr,),
        in_specs=[pl.BlockSpec((p, tr, c), lambda i: (0, i, 0))],
        out_specs=pl.BlockSpec((tr, c), lambda i: (i, 0)),
        out_shape=jax.ShapeDtypeStruct((r, c), out_dtype),
        compiler_params=_cparams(("parallel",)),
    )(parts)


def _adamw(w, gparts, m, v, *, name):
    r, c = w.shape
    p = gparts.shape[0]
    tr = _row_tile(r)

    def body(w_ref, g_ref, m_ref, v_ref, go_ref, d_ref, mo_ref, vo_ref):
        g = g_ref[0].astype(F32)
        for k in range(1, p):
            g = g + g_ref[k].astype(F32)
        mn = ADAM_B1 * m_ref[...] + (1.0 - ADAM_B1) * g
        vn = ADAM_B2 * v_ref[...] + (1.0 - ADAM_B2) * jnp.square(g)
        m_hat = mn / (1.0 - ADAM_B1 ** ADAM_STEP)
        v_hat = vn / (1.0 - ADAM_B2 ** ADAM_STEP)
        go_ref[...] = g
        d_ref[...] = -ADAM_LR * (m_hat / (jnp.sqrt(v_hat) + ADAM_EPS) + ADAM_WD * w_ref[...])
        mo_ref[...] = mn
        vo_ref[...] = vn

    spec = pl.BlockSpec((tr, c), lambda i: (i, 0))
    return pl.pallas_call(
        body, name=name, grid=(r // tr,),
        in_specs=[spec, pl.BlockSpec((p, tr, c), lambda i: (0, i, 0)), spec, spec],
        out_specs=[spec] * 4, out_shape=[jax.ShapeDtypeStruct((r, c), F32)] * 4,
        compiler_params=_cparams(("parallel",)),
    )(w, gparts, m, v)


_FLIPS = {
    "xy": [(1, 0, 0), (0, 1, 0), (1, 1, 0)],
    "c": [(0, 0, 1)],
    "xyc": [(fx, fy, fc) for fx in (0, 1) for fy in (0, 1) for fc in (0, 1) if (fx, fy, fc) != (0, 0, 0)],
}


def _slot(mode, px, py, pc):
    return {"xy": 2 * px + py, "c": pc, "xyc": 4 * px + 2 * py + pc}[mode]


class _Exchange:
    def __init__(self, arrs, mode, scatter):
        self.arrs, self.mode, self.scatter = list(arrs), mode, scatter
        self.n = len(self.arrs)
        self.flips = _FLIPS[mode]
        nf = len(self.flips)
        anyspec = pl.BlockSpec(memory_space=pl.ANY)
        self.in_specs = [anyspec] * self.n
        self.out_specs = [anyspec] * self.n
        self.out_shape = [jax.ShapeDtypeStruct((nf + 1,) + (a.shape[1:] if scatter else a.shape), a.dtype)
                          for a in self.arrs]
        self.scratch = [pltpu.SemaphoreType.DMA((self.n * nf,)), pltpu.SemaphoreType.DMA((self.n * nf,)),
                        pltpu.SemaphoreType.DMA((self.n,))]

    def _copies(self, ins, outs, sems, arrivals=True):
        send, recv, loc = sems
        nf = len(self.flips)
        x, y, c = lax.axis_index("x"), lax.axis_index("y"), lax.axis_index("c")
        me = _slot(self.mode, x, y, c)
        peers = [(x ^ fx, y ^ fy, c ^ fc) for (fx, fy, fc) in self.flips]

        def src(a, slot):
            return ins[a].at[slot] if self.scatter else ins[a]

        def copy(a, j, dst_slot):
            return pltpu.make_async_remote_copy(
                src_ref=src(a, _slot(self.mode, *peers[j])), dst_ref=outs[a].at[dst_slot],
                send_sem=send.at[a * nf + j], recv_sem=recv.at[a * nf + j], device_id=peers[j], device_id_type=MESH)

        pairs = [(a, j) for a in range(self.n) for j in range(nf)]
        local = [pltpu.make_async_copy(src(a, me), outs[a].at[me], loc.at[a]) for a in range(self.n)]
        sends = [copy(a, j, me) for a, j in pairs]
        recvs = [copy(a, j, _slot(self.mode, *peers[j])) for a, j in pairs] if arrivals else []
        return local, sends, recvs

    def start(self, ins, outs, sems):
        local, sends, _ = self._copies(ins, outs, sems, arrivals=False)
        for cp in local + sends:
            cp.start()

    def wait(self, ins, outs, sems):
        local, sends, recvs = self._copies(ins, outs, sems)
        for cp in recvs:
            cp.wait_recv()
        for cp in sends:
            cp.wait_send()
        for cp in local:
            cp.wait()


def _carried(carry, refs, n_in, n_out, n_scratch):
    k = carry.n if carry is not None else 0
    ins, refs = refs[:n_in], refs[n_in:]
    cin, refs = refs[:k], refs[k:]
    outs, refs = refs[:n_out], refs[n_out:]
    cout, refs = refs[:k], refs[k:]
    scratch, sems = refs[:n_scratch], refs[n_scratch:]
    return ins, outs, scratch, (cin, cout, sems)


def _exchanges(exs, *, name):
    counts = [ex.n for ex in exs]
    total = sum(counts)

    def body(*refs):
        ins, outs, sems = refs[:total], refs[total:2 * total], refs[2 * total:]
        comms, at = [], 0
        for k, ex in enumerate(exs):
            comms.append((ins[at:at + ex.n], outs[at:at + ex.n], sems[3 * k:3 * k + 3]))
            at += ex.n
        for ex, comm in zip(exs, comms):
            ex.start(*comm)
        for ex, comm in zip(exs, comms):
            ex.wait(*comm)

    res = pl.pallas_call(
        body, name=name, in_specs=[sp for ex in exs for sp in ex.in_specs],
        out_specs=[sp for ex in exs for sp in ex.out_specs], out_shape=[sh for ex in exs for sh in ex.out_shape],
        scratch_shapes=[sc for ex in exs for sc in ex.scratch])(*[a for ex in exs for a in ex.arrs])
    out, at = [], 0
    for n in counts:
        out.append(list(res[at:at + n]))
        at += n
    return out


def _exchange(arrs, mode, scatter, *, name):
    return _exchanges([_Exchange(arrs, mode, scatter)], name=name)[0]


def _softplus(v):
    return jnp.maximum(v, 0.0) + jnp.log1p(jnp.exp(-jnp.abs(v)))


def _pad_lanes(v):
    r, n = v.shape
    return jnp.pad(v, ((0, 0), (0, -n % LANES)))


def _to_heads(v):
    s = v.shape[0]
    return v.reshape(s, -1, 64).transpose(1, 0, 2)


def _from_heads(v):
    h, s, p = v.shape
    return v.transpose(1, 0, 2).reshape(s, h * p)


def _fn_rms(v, g):
    return (_rms(v, g),)


def _fn_post(xv, ov, g):
    return (xv + _rms(ov, g),)


def _fn_act(xbc, dtp, fp, dtb, fb):
    return _silu(xbc), _softplus(dtp + dtb), -_softplus(-(fp + fb))


def _fn_mix(y, zs, o, zf, g):
    yg = y * _silu(zs)
    sq = yg * yg
    lane = lax.broadcasted_iota(jnp.int32, (1, D_SSD), 1)
    width = D_SSD // SSD_GROUPS
    rstd = jnp.zeros_like(yg)
    for gi in range(SSD_GROUPS):
        msk = ((lane >= gi * width) & (lane < (gi + 1) * width)).astype(F32)
        ms = jnp.sum(sq * msk, axis=1, keepdims=True) / width
        rstd = rstd + lax.rsqrt(ms + EPS) * msk
    return (jnp.concatenate([yg * rstd * g, o * _silu(zf)], axis=1),)


def _fn_glu(val, gate):
    return (val * jax.nn.sigmoid(gate),)


def _fn_ln(hc, z, g, b):
    mu = jnp.mean(hc, axis=-1, keepdims=True)
    xc = hc - mu
    yn = xc * lax.rsqrt(jnp.mean(xc * xc, axis=-1, keepdims=True) + EPS) * g + b
    return (_silu(yn) * _silu(z),)


class _NoComm:
    def odd_weights(self):
        return None

    def got_odd_weights(self, got, w):
        pass

    def early_grads(self, g):
        return None

    def got_early_grads(self, got):
        pass

    def early_sums(self):
        return None

    def got_early_sums(self, got):
        pass


def _local_step(x, tgt, w, comm=None):
    comm = comm or _NoComm()
    s = x.shape[0]
    d = D_MODEL
    tm = 256
    bf = lambda v: v.astype(BF16)
    c1 = lambda arr: _col(arr, 0, arr.shape[1])
    g = {}

    ew = w["e_w_in"]
    w_z, w_xbc = bf(ew[:, 0:2048]), bf(ew[:, 2048:4096])
    w_dt = bf(_pad_lanes(ew[:, 4096:4112]))
    w_qkv = bf(ew[:, 4112:7184])
    w_f = bf(_pad_lanes(ew[:, 7184:7200]))
    w_eo = bf(w["e_w_out"])
    dtb, fgb = _pad_lanes(w["e_dt_bias"]), _pad_lanes(w["e_fgate_b"])
    alog, dsk = _pad_lanes(w["e_a_log"]), _pad_lanes(w["e_d_skip"])

    (u0,) = _rowwise_fwd(_fn_rms, [c1(x)], [c1(w["e_norm_pre"])], [(d, BF16)], tm=tm, name="e_pre")
    z = _mm(u0, w_z, name="e_in_z")
    xbc_raw = _mm(u0, w_xbc, out_dtype=BF16, name="e_in_xbc")
    qkv = _mm(u0, w_qkv, out_dtype=BF16, name="e_in_qkv")
    dtp = _mm(u0, w_dt, name="e_in_dt")
    fp = _mm(u0, w_f, name="e_in_f")
    xbc_pre = _conv_fwd(xbc_raw, w["e_conv_w"], w["e_conv_b"], name="e_conv")
    act_rows = [c1(xbc_pre), c1(dtp), c1(fp)]
    act_pars = [c1(dtb), c1(fgb)]
    xbc, dt, lf = _rowwise_fwd(_fn_act, act_rows, act_pars, [(2048, F32), (LANES, F32), (LANES, F32)],
                               tm=tm, name="e_act")
    y, hsave = _ssd2_fwd(dt, xbc, alog, dsk, name="e_ssd")
    csum = _cumsum_lanes(lf[:, :FOX_HEADS].T, reverse=False, name="e_cumsum").reshape(FOX_HEADS, 1, s)
    o, mrun, *got = _fox3_fwd(qkv, csum, carry=comm.odd_weights(), name="e_fox")
    comm.got_odd_weights(got, w)
    w_oi, w_oo = bf(w["o_w_in"]), bf(w["o_w_out"])
    mix_rows = [c1(y), _col(z, 0, D_SSD), c1(o), _col(z, 1, D_FOX)]
    mix_pars = [c1(w["e_ssd_norm"])]
    (hmix,) = _rowwise_fwd(_fn_mix, mix_rows, mix_pars, [(2048, BF16)], tm=tm, name="e_mix")
    out0 = _mm(hmix, w_eo, name="e_out")
    post_rows = [c1(x), c1(out0)]
    (x1,) = _rowwise_fwd(_fn_post, post_rows, [c1(w["e_norm_post"])], [(d, F32)], tm=tm, name="e_post")

    (u1,) = _rowwise_fwd(_fn_rms, [c1(x1)], [c1(w["o_norm_pre"])], [(d, BF16)], tm=tm, name="o_pre")
    p1 = _mm(u1, w_oi, out_dtype=BF16, name="o_in")
    glu_rows = [_col(p1, 0, D_CONV), _col(p1, 1, D_CONV)]
    (hg,) = _rowwise_fwd(_fn_glu, glu_rows, [], [(D_CONV, BF16)], tm=tm, name="o_glu")
    hc = _conv_fwd(hg, w["o_conv_w"], w["o_conv_b"], name="o_conv")
    ln_rows = [c1(hc), _col(p1, 2, D_CONV)]
    ln_pars = [c1(w["o_ln_g"]), c1(w["o_ln_b"])]
    (h2,) = _rowwise_fwd(_fn_ln, ln_rows, ln_pars, [(D_CONV, BF16)], tm=tm, name="o_ln")
    out1 = _mm(h2, w_oo, name="o_out")

    dx2, dout1, g["o_norm_post"], loss = _final(x1, out1, w["o_norm_post"], tgt, name="loss_head")
    dh2 = _mm(dout1, w_oo, tb=True, name="o_out_dx")
    g["o_w_out"] = _mm(h2, dout1, ta=True, name="o_out_dw")
    (dhc, dz1), (g["o_ln_g"], g["o_ln_b"]) = _rowwise_bwd(_fn_ln, ln_rows, ln_pars, [c1(dh2)], [F32, BF16],
                                                         tm=tm, name="o_ln_bwd")
    dhg = _conv_bwd_x(dhc, w["o_conv_w"], name="o_conv_dx")
    g["o_conv_w"], g["o_conv_b"] = _conv_bwd_w(hg, dhc, CONV_WIDTH, name="o_conv_dw")
    (dval, dgate), _ = _rowwise_bwd(_fn_glu, glu_rows, [], [c1(dhg)], [BF16, BF16], tm=tm, name="o_glu_bwd")
    du1 = _mm(dval, w_oi[:, 0:2048], tb=True, name="o_in_dx0")
    du1 = _mm(dgate, w_oi[:, 2048:4096], tb=True, add=du1, name="o_in_dx1")
    du1 = _mm(dz1, w_oi[:, 4096:6144], tb=True, add=du1, name="o_in_dx2")
    g["o_w_in"] = jnp.concatenate([_mm(u1, dval, ta=True, name="o_in_dw0"), _mm(u1, dgate, ta=True, name="o_in_dw1"),
                                   _mm(u1, dz1, ta=True, name="o_in_dw2")], axis=1)
    (dx1,), (g["o_norm_pre"],) = _rowwise_bwd(_fn_rms, [c1(x1)], [c1(w["o_norm_pre"])], [c1(du1)], [F32],
                                              adds={0: c1(dx2)}, tm=tm, name="o_pre_bwd")

    (dout0,), (g["e_norm_post"],) = _rowwise_bwd(_fn_post, post_rows, [c1(w["e_norm_post"])], [c1(dx1)],
                                                 [None, BF16], tm=tm, name="e_post_bwd")
    dhmix = _mm(dout0, w_eo, tb=True, name="e_out_dx")
    g["e_w_out"] = _mm(hmix, dout0, ta=True, name="e_out_dw")
    (dy, dzs, do, dzf), (g["e_ssd_norm"],) = _rowwise_bwd(_fn_mix, mix_rows, mix_pars, [c1(dhmix)],
                                                        [F32, BF16, F32, BF16], tm=tm, name="e_mix_bwd")
    dox = _fox_dopack(do, o, name="e_fox_dopack")
    dq8, dk, dv, dcs, *got = _fox3_bwd(qkv, csum, dox, mrun, carry=comm.early_grads(g), name="e_fox_bwd")
    comm.got_early_grads(got)
    dlf = _pad_lanes(_cumsum_lanes(dcs.reshape(FOX_HEADS, s), reverse=True, name="e_cumsum_bwd").T)
    dxs, ddt, dbm, dcm, dalog, ddsk, *got = _ssd2_bwd(dt, xbc, alog, dsk, hsave, dy, carry=comm.early_sums(),
                                                       name="e_ssd_bwd")
    comm.got_early_sums(got)
    dxbc = jnp.concatenate([dxs, dbm, dcm], axis=1)
    (dxbc_pre, ddtp, dfp), (ddtb, dfgb) = _rowwise_bwd(_fn_act, act_rows, act_pars, [c1(dxbc), c1(ddt), c1(dlf)],
                                                      [F32, BF16, BF16], tm=tm, name="e_act_bwd")
    dxbc_raw = bf(_conv_bwd_x(dxbc_pre, w["e_conv_w"], name="e_conv_dx"))
    g["e_conv_w"], g["e_conv_b"] = _conv_bwd_w(xbc_raw, dxbc_pre, SSD_CONV, name="e_conv_dw")
    du0 = _mm(dzs, w_z[:, :D_SSD], tb=True, name="e_in_dx0")
    du0 = _mm(dzf, w_z[:, D_SSD:], tb=True, add=du0, name="e_in_dx1")
    du0 = _mm(dxbc_raw, w_xbc, tb=True, add=du0, name="e_in_dx2")
    eighth = FOX_HEADDIM ** -0.5
    du0 = _mm(dq8, w_qkv[:, :D_FOX] * eighth, tb=True, add=du0, name="e_in_dx3q")
    du0 = _mm(dk, w_qkv[:, D_FOX:2 * D_FOX], tb=True, add=du0, name="e_in_dx3k")
    du0 = _mm(dv, w_qkv[:, 2 * D_FOX:], tb=True, add=du0, name="e_in_dx3v")
    du0 = _mm(ddtp, w_dt, tb=True, add=du0, name="e_in_dx4")
    du0 = _mm(dfp, w_f, tb=True, add=du0, name="e_in_dx5")
    g["e_w_in"] = jnp.concatenate([
        _mm(u0, dzs, ta=True, name="e_in_dw0"), _mm(u0, dzf, ta=True, name="e_in_dw1"),
        _mm(u0, dxbc_raw, ta=True, name="e_in_dw2"), _mm(u0, ddtp, ta=True, name="e_in_dw3")[:, :SSD_HEADS],
        _mm(u0, dq8, ta=True, name="e_in_dw4q") * eighth, _mm(u0, dk, ta=True, name="e_in_dw4k"),
        _mm(u0, dv, ta=True, name="e_in_dw4v"), _mm(u0, dfp, ta=True, name="e_in_dw5")[:, :FOX_HEADS]], axis=1)
    (dx,), (g["e_norm_pre"],) = _rowwise_bwd(_fn_rms, [c1(x)], [c1(w["e_norm_pre"])], [c1(du0)], [F32],
                                             adds={0: c1(dx1)}, tm=tm, name="e_pre_bwd")
    g["e_dt_bias"], g["e_fgate_b"] = ddtb[:, :SSD_HEADS], dfgb[:, :FOX_HEADS]
    g["e_a_log"], g["e_d_skip"] = dalog[:, :SSD_HEADS], ddsk[:, :SSD_HEADS]
    return loss, dx, g


_WEIGHTS = ["e_norm_pre", "e_w_in", "e_conv_w", "e_conv_b", "e_dt_bias", "e_a_log", "e_d_skip", "e_fgate_b",
            "e_ssd_norm", "e_w_out", "e_norm_post", "o_norm_pre", "o_w_in", "o_conv_w", "o_conv_b", "o_ln_g",
            "o_ln_b", "o_w_out", "o_norm_post"]
_BIG = ["e_w_in", "e_w_out", "o_w_in", "o_w_out"]
_ROW_SHARDED = ["e_w_out", "o_w_out"]
_SMALL_SHARDED = ["e_conv_w", "o_norm_pre", "o_conv_w", "o_conv_b", "o_ln_g", "o_ln_b", "o_norm_post"]
_REPLICATED = ["e_norm_pre", "e_conv_b", "e_dt_bias", "e_a_log", "e_d_skip", "e_fgate_b", "e_ssd_norm", "e_norm_post"]
_SMALL = [n for n in _WEIGHTS if n not in _BIG]
_EVEN_SHARDED = ["e_w_in", "e_w_out", "e_conv_w"]
_ODD_SHARDED = ["o_w_in", "o_w_out", "o_norm_pre", "o_conv_w", "o_conv_b", "o_ln_g", "o_ln_b", "o_norm_post"]
_EARLY_GRADS = ["o_w_in", "o_w_out", "e_w_out"]
N_CHIPS = 4


def _join(gathered, rows):
    k, r, c = gathered.shape
    return gathered.reshape(k * r, c) if rows else gathered.transpose(1, 0, 2).reshape(r, k * c)


def _split(full, rows):
    r, c = full.shape
    return full.reshape(N_CHIPS, r // N_CHIPS, c) if rows else full.reshape(r, N_CHIPS, c // N_CHIPS).transpose(1, 0, 2)


def kernel(x, e_norm_pre, e_w_in, e_conv_w, e_conv_b, e_dt_bias, e_a_log, e_d_skip, e_fgate_b, e_ssd_norm, e_w_out, e_norm_post, o_norm_pre, o_w_in, o_conv_w, o_conv_b, o_ln_g, o_ln_b, o_w_out, o_norm_post, loss_target, m_e_norm_pre, m_e_w_in, m_e_conv_w, m_e_conv_b, m_e_dt_bias, m_e_a_log, m_e_d_skip, m_e_fgate_b, m_e_ssd_norm, m_e_w_out, m_e_norm_post, m_o_norm_pre, m_o_w_in, m_o_conv_w, m_o_conv_b, m_o_ln_g, m_o_ln_b, m_o_w_out, m_o_norm_post, v_e_norm_pre, v_e_w_in, v_e_conv_w, v_e_conv_b, v_e_dt_bias, v_e_a_log, v_e_d_skip, v_e_fgate_b, v_e_ssd_norm, v_e_w_out, v_e_norm_post, v_o_norm_pre, v_o_w_in, v_o_conv_w, v_o_conv_b, v_o_ln_g, v_o_ln_b, v_o_w_out, v_o_norm_post):
    wvals = (e_norm_pre, e_w_in, e_conv_w, e_conv_b, e_dt_bias, e_a_log, e_d_skip, e_fgate_b, e_ssd_norm, e_w_out,
             e_norm_post, o_norm_pre, o_w_in, o_conv_w, o_conv_b, o_ln_g, o_ln_b, o_w_out, o_norm_post)
    mvals = (m_e_norm_pre, m_e_w_in, m_e_conv_w, m_e_conv_b, m_e_dt_bias, m_e_a_log, m_e_d_skip, m_e_fgate_b,
             m_e_ssd_norm, m_e_w_out, m_e_norm_post, m_o_norm_pre, m_o_w_in, m_o_conv_w, m_o_conv_b, m_o_ln_g,
             m_o_ln_b, m_o_w_out, m_o_norm_post)
    vvals = (v_e_norm_pre, v_e_w_in, v_e_conv_w, v_e_conv_b, v_e_dt_bias, v_e_a_log, v_e_d_skip, v_e_fgate_b,
             v_e_ssd_norm, v_e_w_out, v_e_norm_post, v_o_norm_pre, v_o_w_in, v_o_conv_w, v_o_conv_b, v_o_ln_g,
             v_o_ln_b, v_o_w_out, v_o_norm_post)

    def mat(v):
        return v.reshape(v.shape[-2:]) if v.ndim == 3 else v

    w = {n: mat(v) for n, v in zip(_WEIGHTS, wvals)}
    m = {n: mat(v) for n, v in zip(_WEIGHTS, mvals)}
    v2 = {n: mat(v) for n, v in zip(_WEIGHTS, vvals)}
    me_xy = 2 * lax.axis_index("x") + lax.axis_index("y")

    def shard(n):
        return w[n].astype(BF16) if n in _BIG else w[n]

    gathered = _exchange([shard(n) for n in _EVEN_SHARDED], "xy", False, name="gather_weights")
    full = {n: w[n] for n in _REPLICATED}
    for n, gth in zip(_EVEN_SHARDED, gathered):
        full[n] = _join(gth, n in _ROW_SHARDED)
    gparts = {}

    class _StepComm(_NoComm):
        def odd_weights(self):
            return _Exchange([shard(n) for n in _ODD_SHARDED], "xy", False)

        def got_odd_weights(self, got, wdict):
            for n, gth in zip(_ODD_SHARDED, got):
                wdict[n] = _join(gth, n in _ROW_SHARDED)

        def early_grads(self, g):
            return _Exchange([_split(g[n], n in _ROW_SHARDED).astype(BF16) for n in _EARLY_GRADS], "xy", True)

        def got_early_grads(self, got):
            self.sums = [_sum_slots(p, out_dtype=BF16, name="sum_" + n) for n, p in zip(_EARLY_GRADS, got)]

        def early_sums(self):
            return _Exchange(self.sums, "c", False)

        def got_early_sums(self, got):
            gparts.update(zip(_EARLY_GRADS, got))

    loss, dx, g = _local_step(x[0], loss_target[0], full, _StepComm())
    loss = lax.psum(loss[0, 0], ("x", "y", "c"))

    flat = jnp.concatenate([_pad_lanes(g[n].reshape(1, -1)) for n in _SMALL], axis=1).reshape(-1, LANES)
    (scattered,), (all8,) = _exchanges([_Exchange([_split(g["e_w_in"], False).astype(BF16)], "xy", True),
                                        _Exchange([flat], "xyc", False)], name="scatter_grads")
    (gparts["e_w_in"],) = _exchange([_sum_slots(scattered, out_dtype=BF16, name="sum_e_w_in")], "c", False,
                                    name="pair_grads")
    total = _sum_slots(all8, name="sum_small").reshape(1, -1)
    at = 0
    for n in _SMALL:
        size = g[n].size
        gn = total[:, at:at + size].reshape(g[n].shape)
        at += size + (-size % LANES)
        if n in _SMALL_SHARDED:
            cols = gn.shape[1] // N_CHIPS
            gn = lax.dynamic_slice(gn, (0, me_xy * cols), (gn.shape[0], cols))
        gparts[n] = gn[None]

    grads, deltas, new_m, new_v = [], [], [], []
    for n, orig in zip(_WEIGHTS, wvals):
        gn, dn, mn, vn = _adamw(w[n], gparts[n], m[n], v2[n], name="adamw_" + n)
        for lst, val in zip((grads, deltas, new_m, new_v), (gn, dn, mn, vn)):
            lst.append(val.reshape(orig.shape))
    return (loss, dx[None], *grads, *deltas, *new_m, *new_v)
```

```python
import functools

import jax
import jax.numpy as jnp
from jax import lax
from jax.experimental import pallas as pl
from jax.experimental.pallas import tpu as pltpu

F32 = jnp.float32
BF16 = jnp.bfloat16
MESH = pl.DeviceIdType.MESH

D_MODEL = 1024
D_SSD = 1024
SSD_HEADS = 16
SSD_HEADDIM = 64
SSD_GROUPS = 4
SSD_HPG = 4
D_STATE = 128
SSD_CONV = 4
CHUNK = 128
D_FOX = 1024
FOX_HEADS = 16
FOX_HEADDIM = 64
D_CONV = 2048
CONV_WIDTH = 31
EPS = 1e-6
LANES = 128
VMEM_LIMIT = 56 * 1024 * 1024

ADAM_LR = 0.001
ADAM_B1 = 0.9
ADAM_B2 = 0.999
ADAM_EPS = 1e-08
ADAM_WD = 0.01
ADAM_STEP = 10


def _cparams(sem=None):
    return pltpu.CompilerParams(dimension_semantics=sem, vmem_limit_bytes=VMEM_LIMIT)


def _mm(a, b, *, ta=False, tb=False, add=None, out_dtype=F32, tm=1024, tn=None, tk=2048, name):
    m = a.shape[1] if ta else a.shape[0]
    k = a.shape[0] if ta else a.shape[1]
    n = b.shape[0] if tb else b.shape[1]
    if tn is None:
        tn = 1024
    tm, tn = min(tm, m), min(tn, n)
    tk = max(t for t in range(LANES, min(tk, k) + 1, LANES) if k % t == 0)
    assert m % tm == 0 and n % tn == 0 and k % tk == 0, (m, n, k, tm, tn, tk)
    nk = k // tk
    dims = (((0 if ta else 1,), (1 if tb else 0,)), ((), ()))

    def body(*refs):
        if add is None:
            a_ref, b_ref, o_ref = refs[:3]
            c_ref = None
        else:
            a_ref, b_ref, c_ref, o_ref = refs[:4]
        kk = pl.program_id(2)
        prod = lax.dot_general(a_ref[...].astype(BF16), b_ref[...].astype(BF16), dims, preferred_element_type=F32)
        if nk == 1:
            o_ref[...] = (prod if c_ref is None else prod + c_ref[...].astype(F32)).astype(o_ref.dtype)
            return
        acc_ref = refs[-1]

        @pl.when(kk == 0)
        def _():
            acc_ref[...] = prod if c_ref is None else prod + c_ref[...].astype(F32)

        @pl.when((kk > 0) & (kk < nk - 1))
        def _():
            acc_ref[...] += prod

        @pl.when(kk == nk - 1)
        def _():
            o_ref[...] = (acc_ref[...] + prod).astype(o_ref.dtype)

    a_spec = (pl.BlockSpec((tk, tm), lambda j, i, kk: (kk, i)) if ta
              else pl.BlockSpec((tm, tk), lambda j, i, kk: (i, kk)))
    b_spec = (pl.BlockSpec((tn, tk), lambda j, i, kk: (j, kk)) if tb
              else pl.BlockSpec((tk, tn), lambda j, i, kk: (kk, j)))
    o_spec = pl.BlockSpec((tm, tn), lambda j, i, kk: (i, j))
    in_specs, args = [a_spec, b_spec], [a, b]
    if add is not None:
        in_specs.append(o_spec)
        args.append(add)
    return pl.pallas_call(
        body, name=name, grid=(n // tn, m // tm, nk),
        in_specs=in_specs, out_specs=o_spec,
        out_shape=jax.ShapeDtypeStruct((m, n), out_dtype),
        scratch_shapes=[pltpu.VMEM((tm, tn), F32)] if nk > 1 else [],
        compiler_params=_cparams(("parallel", "parallel", "arbitrary")),
    )(*args)


def _col(arr, cb, width):
    return (arr, cb, width)


def _row_specs(ops, tm):
    return [pl.BlockSpec((tm, w), lambda i, cb=cb: (i, cb)) for (_, cb, w) in ops]


def _par_specs(ops):
    return [pl.BlockSpec((a.shape[0], w), lambda i, cb=cb: (0, cb)) for (a, cb, w) in ops]


def _rowwise_fwd(fn, rows, params, outs, *, tm, name):
    s = rows[0][0].shape[0]
    tm = min(tm, s)
    nr, npar = len(rows), len(params)

    def body(*refs):
        rv = [r[...].astype(F32) for r in refs[:nr]]
        pv = [p[...].astype(F32) for p in refs[nr:nr + npar]]
        res = fn(*rv, *pv)
        for o_ref, val in zip(refs[nr + npar:], res):
            o_ref[...] = val.astype(o_ref.dtype)

    return pl.pallas_call(
        body, name=name, grid=(s // tm,),
        in_specs=_row_specs(rows, tm) + _par_specs(params),
        out_specs=[pl.BlockSpec((tm, w), lambda i: (i, 0)) for (w, _) in outs],
        out_shape=[jax.ShapeDtypeStruct((s, w), dt) for (w, dt) in outs],
        compiler_params=_cparams(("parallel",)),
    )(*[r[0] for r in rows], *[p[0] for p in params])


def _rowwise_bwd(fn, rows, params, couts, row_grads, *, adds=None, tm, name):
    adds = adds or {}
    s = rows[0][0].shape[0]
    tm = min(tm, s)
    nr, npar, nc = len(rows), len(params), len(couts)
    add_keys = sorted(adds)
    want = [i for i, dt in enumerate(row_grads) if dt is not None]

    def body(*refs):
        i = pl.program_id(0)
        rv = [r[...].astype(F32) for r in refs[:nr]]
        pv = [p[...].astype(F32) for p in refs[nr:nr + npar]]
        cv = [c[...].astype(F32) for c in refs[nr + npar:nr + npar + nc]]
        av = {k: r[...].astype(F32) for k, r in zip(add_keys, refs[nr + npar + nc:nr + npar + nc + len(add_keys)])}
        orefs = refs[nr + npar + nc + len(add_keys):]
        _, vjp = jax.vjp(lambda rr, pp: tuple(fn(*rr, *pp)), rv, pv)
        drows, dpars = vjp(tuple(cv))
        for o_ref, ri in zip(orefs[:len(want)], want):
            g = drows[ri]
            if ri in av:
                g = g + av[ri]
            o_ref[...] = g.astype(o_ref.dtype)

        @pl.when(i == 0)
        def _():
            for o_ref in orefs[len(want):]:
                o_ref[...] = jnp.zeros_like(o_ref)

        for o_ref, g in zip(orefs[len(want):], dpars):
            o_ref[...] += g

    add_ops = [adds[k] for k in add_keys]
    out_specs = ([pl.BlockSpec((tm, rows[ri][2]), lambda i: (i, 0)) for ri in want]
                 + [pl.BlockSpec((p[0].shape[0], p[2]), lambda i: (0, 0)) for p in params])
    out_shape = ([jax.ShapeDtypeStruct((s, rows[ri][2]), row_grads[ri]) for ri in want]
                 + [jax.ShapeDtypeStruct((p[0].shape[0], p[2]), F32) for p in params])
    res = pl.pallas_call(
        body, name=name, grid=(s // tm,),
        in_specs=_row_specs(rows, tm) + _par_specs(params) + _row_specs(couts, tm) + _row_specs(add_ops, tm),
        out_specs=out_specs, out_shape=out_shape,
        compiler_params=_cparams(("arbitrary",)),
    )(*[r[0] for r in rows], *[p[0] for p in params], *[c[0] for c in couts], *[a[0] for a in add_ops])
    return res[:len(want)], res[len(want):]


def _silu(v):
    return v * jax.nn.sigmoid(v)


def _rms(v, g):
    return v * lax.rsqrt(jnp.mean(v * v, axis=-1, keepdims=True) + EPS) * g


SUBLANES = 8
CONV_ROWS = 256


def _halo(shifts):
    up = lambda v: -(-v // SUBLANES) * SUBLANES
    return up(max(0, -min(shifts))), up(max(0, max(shifts)))


def _fill_halo(xp_sc, x_ref, front, back):
    s = x_ref.shape[0]
    if front:
        xp_sc[0:front, :] = jnp.zeros((front, LANES), F32)
    if back:
        xp_sc[front + s:front + s + back, :] = jnp.zeros((back, LANES), F32)
    xp_sc[front:front + s, :] = x_ref[...].astype(F32)


def _shift_conv(x, w, b, shifts, *, name):
    s, c = x.shape
    tr = min(CONV_ROWS, s)
    nk = len(shifts)
    front, back = _halo(shifts)

    def body(*refs):
        if b is None:
            x_ref, w_ref, o_ref, xp_sc = refs
        else:
            x_ref, w_ref, b_ref, o_ref, xp_sc = refs
        _fill_halo(xp_sc, x_ref, front, back)

        def chunk(r, carry):
            base = pl.multiple_of(r * tr, tr)
            acc = jnp.zeros((tr, LANES), F32) if b is None else jnp.broadcast_to(b_ref[...], (tr, LANES))
            for kk in range(nk):
                acc = acc + xp_sc[pl.ds(base + front + shifts[kk], tr), :] * w_ref[kk:kk + 1, :]
            o_ref[pl.ds(base, tr), :] = acc
            return carry

        lax.fori_loop(0, s // tr, chunk, 0)

    strip = pl.BlockSpec((s, LANES), lambda cb: (0, cb))
    in_specs = [strip, pl.BlockSpec((nk, LANES), lambda cb: (0, cb))]
    args = [x, w]
    if b is not None:
        in_specs.append(pl.BlockSpec((1, LANES), lambda cb: (0, cb)))
        args.append(b)
    return pl.pallas_call(
        body, name=name, grid=(c // LANES,), in_specs=in_specs, out_specs=strip,
        out_shape=jax.ShapeDtypeStruct((s, c), F32),
        scratch_shapes=[pltpu.VMEM((front + s + back, LANES), F32)],
        compiler_params=_cparams(("parallel",)),
    )(*args)


def _conv_fwd(x, w, b, *, name):
    k = w.shape[0]
    return _shift_conv(x, w, b, [kk - (k - 1) for kk in range(k)], name=name)


def _conv_bwd_x(dy, w, *, name):
    k = w.shape[0]
    return _shift_conv(dy, w, None, [(k - 1) - kk for kk in range(k)], name=name)


def _conv_bwd_w(x, dy, k, *, name):
    s, c = x.shape
    tr = min(CONV_ROWS, s)
    shifts = [kk - (k - 1) for kk in range(k)]
    front, back = _halo(shifts)

    def fold(v):
        return jnp.sum(v.reshape(tr // SUBLANES, SUBLANES, LANES), axis=0)

    def body(x_ref, dy_ref, dw_ref, db_ref, xp_sc, dw_sc, db_sc):
        _fill_halo(xp_sc, x_ref, front, back)
        dw_sc[...] = jnp.zeros_like(dw_sc)
        db_sc[...] = jnp.zeros_like(db_sc)

        def chunk(r, carry):
            base = pl.multiple_of(r * tr, tr)
            dyv = dy_ref[pl.ds(base, tr), :]
            db_sc[...] += fold(dyv)
            for kk in range(k):
                dw_sc[kk] += fold(xp_sc[pl.ds(base + front + shifts[kk], tr), :] * dyv)
            return carry

        lax.fori_loop(0, s // tr, chunk, 0)
        db_ref[...] = jnp.sum(db_sc[...], axis=0, keepdims=True)
        for kk in range(k):
            dw_ref[kk:kk + 1, :] = jnp.sum(dw_sc[kk], axis=0, keepdims=True)

    strip = pl.BlockSpec((s, LANES), lambda cb: (0, cb))
    return pl.pallas_call(
        body, name=name, grid=(c // LANES,), in_specs=[strip, strip],
        out_specs=[pl.BlockSpec((k, LANES), lambda cb: (0, cb)), pl.BlockSpec((1, LANES), lambda cb: (0, cb))],
        out_shape=[jax.ShapeDtypeStruct((k, c), F32), jax.ShapeDtypeStruct((1, c), F32)],
        scratch_shapes=[pltpu.VMEM((front + s + back, LANES), F32), pltpu.VMEM((k, SUBLANES, LANES), F32),
                        pltpu.VMEM((SUBLANES, LANES), F32)],
        compiler_params=_cparams(("parallel",)),
    )(x, dy)


_DIMS = {"nn": ((1,), (0,)), "nt": ((1,), (1,)), "tn": ((0,), (0,))}


def _bd(a, b, mode):
    return lax.dot_general(a.astype(BF16), b.astype(BF16), (_DIMS[mode], ((), ())), preferred_element_type=F32)


@functools.partial(jax.custom_vjp, nondiff_argnums=(2,))
def _bdot(a, b, mode):
    return _bd(a, b, mode)


def _bdot_fwd(a, b, mode):
    return _bd(a, b, mode), (a, b)


def _bdot_bwd(mode, res, g):
    a, b = res
    if mode == "nn":
        return _bd(g, b, "nt"), _bd(a, g, "tn")
    if mode == "nt":
        return _bd(g, b, "nn"), _bd(g, a, "tn")
    return _bd(b, g, "nt"), _bd(a, g, "nn")


_bdot.defvjp(_bdot_fwd, _bdot_bwd)


def _split3(v):
    hi = v.astype(BF16)
    r1 = v - hi.astype(F32)
    mid = r1.astype(BF16)
    lo = (r1 - mid.astype(F32)).astype(BF16)
    return hi, mid, lo


def _mask_dot(mask01, v, mode):
    out = None
    for part in _split3(v):
        if mode == "vn":
            t = lax.dot_general(part, mask01, (_DIMS["nn"], ((), ())), preferred_element_type=F32)
        else:
            t = lax.dot_general(mask01, part, (_DIMS[mode], ((), ())), preferred_element_type=F32)
        out = t if out is None else out + t
    return out


def _lower_tri(n):
    r = lax.broadcasted_iota(jnp.int32, (n, n), 0)
    c = lax.broadcasted_iota(jnp.int32, (n, n), 1)
    return (r >= c).astype(BF16)


@jax.custom_vjp
def _tri_dot(w):
    return _mask_dot(_lower_tri(w.shape[0]), w, "nn")


def _tri_dot_fwd(w):
    return _tri_dot(w), None


def _tri_dot_bwd(_, g):
    return (_mask_dot(_lower_tri(g.shape[0]), g, "tn"),)


_tri_dot.defvjp(_tri_dot_fwd, _tri_dot_bwd)


def _cumsum_lanes(x, *, reverse, name):
    h, s = x.shape
    n = s // LANES

    def body(x_ref, o_ref):
        r = lax.broadcasted_iota(jnp.int32, (LANES, LANES), 0)
        c = lax.broadcasted_iota(jnp.int32, (LANES, LANES), 1)
        m01 = ((r >= c) if reverse else (r <= c)).astype(BF16)

        def step(t, carry):
            ci = (n - 1 - t) if reverse else t
            at = pl.ds(pl.multiple_of(ci * LANES, LANES), LANES)
            blk = x_ref[:, at]
            o_ref[:, at] = _mask_dot(m01, blk, "vn") + carry
            return carry + jnp.sum(blk, axis=1, keepdims=True)

        lax.fori_loop(0, n, step, jnp.zeros((h, 1), F32))

    return pl.pallas_call(body, name=name, out_shape=jax.ShapeDtypeStruct((h, s), F32),
                          compiler_params=_cparams())(x)


def _ssd_chunk(xs, dt, bm, cm, hin, a, dsk, head0):
    n = CHUNK
    row = lax.broadcasted_iota(jnp.int32, (n, n), 0)
    col = lax.broadcasted_iota(jnp.int32, (n, n), 1)
    lower = row >= col
    ustrict = (row > col).astype(F32)
    lane = lax.broadcasted_iota(jnp.int32, (1, LANES), 1)
    sub = lax.broadcasted_iota(jnp.int32, (n, 1), 0)
    e_first = (sub == 0).astype(F32)
    e_last = (sub == n - 1).astype(F32)
    lane0 = (lane == 0).astype(F32)
    cb = _bdot(cm, bm, "nt")
    da = dt * (-jnp.exp(a))
    ys, houts = [], []
    for r in range(SSD_HPG):
        oh = (lane == head0 + r).astype(F32)
        dt_col = jnp.sum(dt * oh, axis=1, keepdims=True)
        da_col = jnp.sum(da * oh, axis=1, keepdims=True)
        dsk_h = jnp.sum(dsk * oh, axis=1, keepdims=True)
        seg = _tri_dot(da_col * ustrict)
        decay = jnp.where(lower, jnp.exp(seg), 0.0)
        cs_col = jnp.sum(seg * lane0, axis=1, keepdims=True) + jnp.sum(da_col * e_first, axis=0, keepdims=True)
        total = jnp.sum(cs_col * e_last, axis=0, keepdims=True)
        xd = xs[r] * dt_col
        y_diag = _bdot(cb * decay, xd, "nn")
        contrib = _bdot(xd * jnp.exp(total - cs_col), bm, "tn")
        houts.append(hin[r] * jnp.exp(total) + contrib)
        y_off = _bdot(cm, hin[r], "nt") * jnp.exp(cs_col)
        ys.append(y_diag + y_off + xs[r] * dsk_h)
    return ys, houts


def _ssd_specs(nc, rev):
    cc = (lambda c: nc - 1 - c) if rev else (lambda c: c)
    hm = pl.BlockSpec((SSD_HPG, CHUNK, SSD_HEADDIM), lambda c, g: (g, cc(c), 0))
    row = pl.BlockSpec((CHUNK, LANES), lambda c, g: (cc(c), 0))
    bmat = pl.BlockSpec((CHUNK, LANES), lambda c, g: (cc(c), D_SSD // LANES + g))
    cmat = pl.BlockSpec((CHUNK, LANES), lambda c, g: (cc(c), D_SSD // LANES + SSD_GROUPS + g))
    par = pl.BlockSpec((1, LANES), lambda c, g: (0, 0))
    hs = pl.BlockSpec((1, SSD_HPG, SSD_HEADDIM, D_STATE), lambda c, g: (cc(c), g, 0, 0))
    return hm, row, bmat, cmat, par, hs


def _ssd_fwd(xs_hm, dt, xbc, a, dsk, *, name):
    s = xs_hm.shape[1]
    nc = s // CHUNK
    hm, row, bmat, cmat, par, hs = _ssd_specs(nc, False)

    def body(xs_ref, dt_ref, bm_ref, cm_ref, a_ref, dsk_ref, y_ref, hs_ref, h_sc):
        c, g = pl.program_id(0), pl.program_id(1)
        mine = pl.ds(g * SSD_HPG, SSD_HPG)

        @pl.when(c == 0)
        def _():
            h_sc[mine] = jnp.zeros((SSD_HPG, SSD_HEADDIM, D_STATE), F32)

        hin = [h_sc[g * SSD_HPG + r] for r in range(SSD_HPG)]
        ys, houts = _ssd_chunk([xs_ref[r] for r in range(SSD_HPG)], dt_ref[...], bm_ref[...], cm_ref[...],
                               hin, a_ref[...], dsk_ref[...], g * SSD_HPG)
        for r in range(SSD_HPG):
            y_ref[r] = ys[r]
            hs_ref[0, r] = hin[r]
            h_sc[g * SSD_HPG + r] = houts[r]

    return pl.pallas_call(
        body, name=name, grid=(nc, SSD_GROUPS),
        in_specs=[hm, row, bmat, cmat, par, par], out_specs=[hm, hs],
        out_shape=[jax.ShapeDtypeStruct(xs_hm.shape, F32),
                   jax.ShapeDtypeStruct((nc, SSD_HEADS, SSD_HEADDIM, D_STATE), F32)],
        scratch_shapes=[pltpu.VMEM((SSD_HEADS, SSD_HEADDIM, D_STATE), F32)],
        compiler_params=_cparams(("arbitrary", "arbitrary")),
    )(xs_hm, dt, xbc, xbc, a, dsk)


def _ssd_bwd(xs_hm, dt, xbc, a, dsk, hsave, dy_hm, *, name):
    s = xs_hm.shape[1]
    nc = s // CHUNK
    hm, row, bmat, cmat, par, hs = _ssd_specs(nc, True)
    gmat = pl.BlockSpec((CHUNK, LANES), lambda c, g: (nc - 1 - c, g))

    def body(xs_ref, dt_ref, bm_ref, cm_ref, a_ref, dsk_ref, hs_ref, dy_ref,
             dxs_ref, ddt_ref, dbm_ref, dcm_ref, da_ref, ddsk_ref, dh_sc):
        c, g = pl.program_id(0), pl.program_id(1)
        mine = pl.ds(g * SSD_HPG, SSD_HPG)

        @pl.when(c == 0)
        def _():
            dh_sc[mine] = jnp.zeros((SSD_HPG, SSD_HEADDIM, D_STATE), F32)

        @pl.when((c == 0) & (g == 0))
        def _():
            da_ref[...] = jnp.zeros_like(da_ref)
            ddsk_ref[...] = jnp.zeros_like(ddsk_ref)

        @pl.when(g == 0)
        def _():
            ddt_ref[...] = jnp.zeros_like(ddt_ref)

        head0 = g * SSD_HPG
        prim = ([xs_ref[r] for r in range(SSD_HPG)], dt_ref[...], bm_ref[...], cm_ref[...],
                [hs_ref[0, r] for r in range(SSD_HPG)], a_ref[...], dsk_ref[...])
        _, vjp = jax.vjp(lambda *p: _ssd_chunk(*p, head0), *prim)
        cot = ([dy_ref[r] for r in range(SSD_HPG)], [dh_sc[g * SSD_HPG + r] for r in range(SSD_HPG)])
        dxs, ddt, dbm, dcm, dhin, da, ddsk = vjp(cot)
        for r in range(SSD_HPG):
            dxs_ref[r] = dxs[r]
            dh_sc[g * SSD_HPG + r] = dhin[r]
        ddt_ref[...] += ddt
        dbm_ref[...] = dbm
        dcm_ref[...] = dcm
        da_ref[...] += da
        ddsk_ref[...] += ddsk

    return pl.pallas_call(
        body, name=name, grid=(nc, SSD_GROUPS),
        in_specs=[hm, row, bmat, cmat, par, par, hs, hm],
        out_specs=[hm, row, gmat, gmat, par, par],
        out_shape=[jax.ShapeDtypeStruct(xs_hm.shape, F32), jax.ShapeDtypeStruct((s, LANES), F32),
                   jax.ShapeDtypeStruct((s, SSD_GROUPS * D_STATE), F32),
                   jax.ShapeDtypeStruct((s, SSD_GROUPS * D_STATE), F32),
                   jax.ShapeDtypeStruct((1, LANES), F32), jax.ShapeDtypeStruct((1, LANES), F32)],
        scratch_shapes=[pltpu.VMEM((SSD_HEADS, SSD_HEADDIM, D_STATE), F32)],
        compiler_params=_cparams(("arbitrary", "arbitrary")),
    )(xs_hm, dt, xbc, xbc, a, dsk, hsave, dy_hm)


SSD_PAIRS = SSD_HPG // 2


def _ssd2_chunk(xs, dt, bm, cm, hin, a, dsk, head0):
    n = CHUNK
    row = lax.broadcasted_iota(jnp.int32, (n, n), 0)
    col = lax.broadcasted_iota(jnp.int32, (n, n), 1)
    lower = row >= col
    ustrict = (row > col).astype(F32)
    lane = lax.broadcasted_iota(jnp.int32, (1, LANES), 1)
    sub = lax.broadcasted_iota(jnp.int32, (n, 1), 0)
    e_first = (sub == 0).astype(F32)
    e_last = (sub == n - 1).astype(F32)
    lane0 = (lane == 0).astype(F32)
    half_l = [(lane < LANES // 2).astype(F32), (lane >= LANES // 2).astype(F32)]
    half_s = [(sub < LANES // 2).astype(F32), (sub >= LANES // 2).astype(F32)]
    cb = _bdot(cm, bm, "nt")
    da = dt * (-jnp.exp(a))
    ys, houts = [], []
    for pr in range(SSD_PAIRS):
        y = jnp.zeros((n, LANES), F32)
        xdte = jnp.zeros((n, LANES), F32)
        lane_gain = jnp.zeros((n, LANES), F32)
        row_gain = jnp.zeros((LANES, 1), F32)
        for hf in range(2):
            oh = (lane == head0 + 2 * pr + hf).astype(F32)
            dt_col = jnp.sum(dt * oh, axis=1, keepdims=True)
            da_col = jnp.sum(da * oh, axis=1, keepdims=True)
            dsk_h = jnp.sum(dsk * oh, axis=1, keepdims=True)
            seg = _tri_dot(da_col * ustrict)
            decay = jnp.where(lower, jnp.exp(seg), 0.0)
            cs_col = jnp.sum(seg * lane0, axis=1, keepdims=True) + jnp.sum(da_col * e_first, axis=0, keepdims=True)
            total = jnp.sum(cs_col * e_last, axis=0, keepdims=True)
            xh = xs[pr] * half_l[hf]
            xd = xh * dt_col
            y = y + _bdot(cb * decay, xd, "nn") + xh * dsk_h
            xdte = xdte + xd * jnp.exp(total - cs_col)
            lane_gain = lane_gain + jnp.exp(cs_col) * half_l[hf]
            row_gain = row_gain + jnp.exp(total) * half_s[hf]
        houts.append(hin[pr] * row_gain + _bdot(xdte, bm, "tn"))
        ys.append(y + _bdot(cm, hin[pr], "nt") * lane_gain)
    return ys, houts


SSD_STEP = 4


def _ssd2_steps(s):
    per = min(SSD_STEP, s // CHUNK)
    return per, s // (CHUNK * per)


def _ssd2_multi(xs, dt, bm, cm, hin, a, dsk, head0):
    ys = []
    for k in range(len(dt)):
        y, hin = _ssd2_chunk(xs[k], dt[k], bm[k], cm[k], hin, a, dsk, head0)
        ys.append(y)
    return ys, hin


def _ssd2_specs(per, nc, rev):
    cc = (lambda c: nc - 1 - c) if rev else (lambda c: c)
    rows = CHUNK * per
    act = pl.BlockSpec((rows, SSD_PAIRS * LANES), lambda c, g: (cc(c), g))
    row = pl.BlockSpec((rows, LANES), lambda c, g: (cc(c), 0))
    bmat = pl.BlockSpec((rows, LANES), lambda c, g: (cc(c), D_SSD // LANES + g))
    cmat = pl.BlockSpec((rows, LANES), lambda c, g: (cc(c), D_SSD // LANES + SSD_GROUPS + g))
    par = pl.BlockSpec((1, LANES), lambda c, g: (0, 0))
    hs = pl.BlockSpec((1, SSD_PAIRS, LANES, D_STATE), lambda c, g: (cc(c), g, 0, 0))
    return act, row, bmat, cmat, par, hs


def _chunk_rows(ref, k):
    return ref[k * CHUNK:(k + 1) * CHUNK, :]


def _pair_cols(ref, k):
    return [ref[k * CHUNK:(k + 1) * CHUNK, pr * LANES:(pr + 1) * LANES] for pr in range(SSD_PAIRS)]


def _ssd2_fwd(dt, xbc, a, dsk, *, name):
    s = xbc.shape[0]
    per, nc = _ssd2_steps(s)
    act, row, bmat, cmat, par, hs = _ssd2_specs(per, nc, False)

    def body(xs_ref, dt_ref, bm_ref, cm_ref, a_ref, dsk_ref, y_ref, hs_ref, h_sc):
        c, g = pl.program_id(0), pl.program_id(1)

        @pl.when(c == 0)
        def _():
            h_sc[pl.ds(g * SSD_PAIRS, SSD_PAIRS)] = jnp.zeros((SSD_PAIRS, LANES, D_STATE), F32)

        hin = [h_sc[g * SSD_PAIRS + pr] for pr in range(SSD_PAIRS)]
        ks = range(per)
        ys, houts = _ssd2_multi([_pair_cols(xs_ref, k) for k in ks], [_chunk_rows(dt_ref, k) for k in ks],
                                [_chunk_rows(bm_ref, k) for k in ks], [_chunk_rows(cm_ref, k) for k in ks],
                                hin, a_ref[...], dsk_ref[...], g * SSD_HPG)
        for pr in range(SSD_PAIRS):
            for k in ks:
                y_ref[k * CHUNK:(k + 1) * CHUNK, pr * LANES:(pr + 1) * LANES] = ys[k][pr]
            hs_ref[0, pr] = hin[pr]
            h_sc[g * SSD_PAIRS + pr] = houts[pr]

    return pl.pallas_call(
        body, name=name, grid=(nc, SSD_GROUPS),
        in_specs=[act, row, bmat, cmat, par, par], out_specs=[act, hs],
        out_shape=[jax.ShapeDtypeStruct((s, D_SSD), F32),
                   jax.ShapeDtypeStruct((nc, SSD_HEADS // 2, LANES, D_STATE), F32)],
        scratch_shapes=[pltpu.VMEM((SSD_HEADS // 2, LANES, D_STATE), F32)],
        compiler_params=_cparams(("arbitrary", "arbitrary")),
    )(xbc, dt, xbc, xbc, a, dsk)


def _ssd2_bwd(dt, xbc, a, dsk, hsave, dy, *, carry=None, name):
    s = xbc.shape[0]
    per, nc = _ssd2_steps(s)
    act, row, bmat, cmat, par, hs = _ssd2_specs(per, nc, True)
    gmat = pl.BlockSpec((CHUNK * per, LANES), lambda c, g: (nc - 1 - c, g))

    def body(*refs):
        ins, outs, (dh_sc,), comm = _carried(carry, refs, 8, 6, 1)
        xs_ref, dt_ref, bm_ref, cm_ref, a_ref, dsk_ref, hs_ref, dy_ref = ins
        dxs_ref, ddt_ref, dbm_ref, dcm_ref, da_ref, ddsk_ref = outs
        c, g = pl.program_id(0), pl.program_id(1)
        if carry is not None:
            @pl.when((c == 0) & (g == 0))
            def _():
                carry.start(*comm)

        @pl.when(c == 0)
        def _():
            dh_sc[pl.ds(g * SSD_PAIRS, SSD_PAIRS)] = jnp.zeros((SSD_PAIRS, LANES, D_STATE), F32)

        @pl.when((c == 0) & (g == 0))
        def _():
            da_ref[...] = jnp.zeros_like(da_ref)
            ddsk_ref[...] = jnp.zeros_like(ddsk_ref)

        @pl.when(g == 0)
        def _():
            ddt_ref[...] = jnp.zeros_like(ddt_ref)

        head0 = g * SSD_HPG
        ks = range(per)
        prim = ([_pair_cols(xs_ref, k) for k in ks], [_chunk_rows(dt_ref, k) for k in ks],
                [_chunk_rows(bm_ref, k) for k in ks], [_chunk_rows(cm_ref, k) for k in ks],
                [hs_ref[0, pr] for pr in range(SSD_PAIRS)], a_ref[...], dsk_ref[...])
        _, vjp = jax.vjp(lambda *p: _ssd2_multi(*p, head0), *prim)
        cot = ([_pair_cols(dy_ref, k) for k in ks], [dh_sc[g * SSD_PAIRS + pr] for pr in range(SSD_PAIRS)])
        dxs, ddt, dbm, dcm, dhin, da, ddsk = vjp(cot)
        for pr in range(SSD_PAIRS):
            for k in ks:
                dxs_ref[k * CHUNK:(k + 1) * CHUNK, pr * LANES:(pr + 1) * LANES] = dxs[k][pr]
            dh_sc[g * SSD_PAIRS + pr] = dhin[pr]
        for k in ks:
            ddt_ref[k * CHUNK:(k + 1) * CHUNK, :] += ddt[k]
            dbm_ref[k * CHUNK:(k + 1) * CHUNK, :] = dbm[k]
            dcm_ref[k * CHUNK:(k + 1) * CHUNK, :] = dcm[k]
        da_ref[...] += da
        ddsk_ref[...] += ddsk
        if carry is not None:
            @pl.when((c == nc - 1) & (g == SSD_GROUPS - 1))
            def _():
                carry.wait(*comm)

    extra = carry if carry is not None else _Exchange([], "c", False)
    return pl.pallas_call(
        body, name=name, grid=(nc, SSD_GROUPS),
        in_specs=[act, row, bmat, cmat, par, par, hs, act] + extra.in_specs,
        out_specs=[act, row, gmat, gmat, par, par] + extra.out_specs,
        out_shape=[jax.ShapeDtypeStruct((s, D_SSD), F32), jax.ShapeDtypeStruct((s, LANES), F32),
                   jax.ShapeDtypeStruct((s, SSD_GROUPS * D_STATE), F32),
                   jax.ShapeDtypeStruct((s, SSD_GROUPS * D_STATE), F32),
                   jax.ShapeDtypeStruct((1, LANES), F32), jax.ShapeDtypeStruct((1, LANES), F32)] + extra.out_shape,
        scratch_shapes=[pltpu.VMEM((SSD_HEADS // 2, LANES, D_STATE), F32)]
        + (carry.scratch if carry is not None else []),
        compiler_params=_cparams(("arbitrary", "arbitrary")),
    )(xbc, dt, xbc, xbc, a, dsk, hsave, dy, *extra.arrs)


FOX_BLOCK = 512
NEG = -1e30


def _fox_scores(q, k, cref, ck, strictly_below):
    t = q.shape[0]
    s = lax.dot_general(q, k, (_DIMS["nt"], ((), ())), preferred_element_type=F32) * (FOX_HEADDIM ** -0.5)
    s = s + (cref - ck)
    row = lax.broadcasted_iota(jnp.int32, (t, t), 0)
    col = lax.broadcasted_iota(jnp.int32, (t, t), 1)
    mask = (row >= col) | strictly_below
    return s, mask


def _fox_fwd(q, k, v, c, *, name):
    h, s, p = q.shape
    t = min(FOX_BLOCK, s)
    nb = s // t

    def body(q_ref, k_ref, v_ref, cq_ref, ck_ref, o_ref, lse_ref, m_sc, l_sc, acc_sc):
        i, j = pl.program_id(1), pl.program_id(2)

        @pl.when(j == 0)
        def _():
            m_sc[...] = jnp.full_like(m_sc, NEG)
            l_sc[...] = jnp.zeros_like(l_sc)
            acc_sc[...] = jnp.zeros_like(acc_sc)

        @pl.when(j <= i)
        def _():
            sc, mask = _fox_scores(q_ref[0], k_ref[0], cq_ref[0, 0:1, 0:1], ck_ref[0], j < i)
            sc = jnp.where(mask, sc, NEG)
            m_old = m_sc[...]
            m_new = jnp.maximum(m_old, jnp.max(sc, axis=1, keepdims=True))
            alpha = jnp.exp(m_old - m_new)
            pr = jnp.exp(sc - m_new)
            l_sc[...] = alpha * l_sc[...] + jnp.sum(pr, axis=1, keepdims=True)
            pr_hi = pr.astype(BF16)
            pr_lo = (pr - pr_hi.astype(F32)).astype(BF16)
            pv = (lax.dot_general(pr_hi, v_ref[0], (_DIMS["nn"], ((), ())), preferred_element_type=F32)
                  + lax.dot_general(pr_lo, v_ref[0], (_DIMS["nn"], ((), ())), preferred_element_type=F32))
            acc_sc[...] = alpha * acc_sc[...] + pv
            m_sc[...] = m_new

        @pl.when(j == i)
        def _():
            o_ref[0] = acc_sc[...] / l_sc[...]
            lse_ref[0] = jnp.broadcast_to(m_sc[...] + jnp.log(l_sc[...]), (t, LANES))

    qspec = pl.BlockSpec((1, t, p), lambda hh, i, j: (hh, i, 0))
    kspec = pl.BlockSpec((1, t, p), lambda hh, i, j: (hh, jnp.minimum(j, i), 0))
    return pl.pallas_call(
        body, name=name, grid=(h, nb, nb),
        in_specs=[qspec, kspec, kspec,
                  pl.BlockSpec((1, 1, t), lambda hh, i, j: (hh, 0, i)),
                  pl.BlockSpec((1, 1, t), lambda hh, i, j: (hh, 0, jnp.minimum(j, i)))],
        out_specs=[qspec, pl.BlockSpec((1, t, LANES), lambda hh, i, j: (hh, i, 0))],
        out_shape=[jax.ShapeDtypeStruct((h, s, p), F32), jax.ShapeDtypeStruct((h, s, LANES), F32)],
        scratch_shapes=[pltpu.VMEM((t, 1), F32), pltpu.VMEM((t, 1), F32), pltpu.VMEM((t, p), F32)],
        compiler_params=_cparams(("parallel", "arbitrary", "arbitrary")),
    )(q, k, v, c, c)


def _fox_bwd(q, k, v, c, o, lse, do, *, name):
    h, s, p = q.shape
    t = min(FOX_BLOCK, s)
    nb = s // t
    scale = FOX_HEADDIM ** -0.5

    def body(q_ref, k_ref, v_ref, cq_ref, ck_ref, o_ref, lse_ref, do_ref,
             dq_ref, dk_ref, dv_ref, dc_ref, dk_sc, dv_sc, dc_sc):
        j, i = pl.program_id(1), pl.program_id(2)

        @pl.when(i == 0)
        def _():
            dk_sc[...] = jnp.zeros_like(dk_sc)
            dv_sc[...] = jnp.zeros_like(dv_sc)
            dc_sc[...] = jnp.zeros_like(dc_sc)

        @pl.when(i >= j)
        def _():
            qv, kv, vv = q_ref[0], k_ref[0], v_ref[0]
            sc, mask = _fox_scores(qv, kv, cq_ref[0, 0:1, 0:1], ck_ref[0], i > j)
            pr = jnp.where(mask, jnp.exp(sc - lse_ref[0, :, 0:1]), 0.0)
            dov = do_ref[0]
            dob = dov.astype(BF16)
            prb = pr.astype(BF16)
            dv_sc[...] += lax.dot_general(prb, dob, (_DIMS["tn"], ((), ())), preferred_element_type=F32)
            dp = lax.dot_general(dob, vv, (_DIMS["nt"], ((), ())), preferred_element_type=F32)
            dcol = jnp.sum(dob.astype(F32) * o_ref[0], axis=1, keepdims=True)
            ds = pr * (dp - dcol)
            dc_sc[...] -= jnp.sum(ds, axis=0, keepdims=True)
            dsb = ds.astype(BF16)
            dqc = scale * lax.dot_general(dsb, kv, (_DIMS["nn"], ((), ())), preferred_element_type=F32)
            at = pl.ds(pl.multiple_of(i * t, t), t)

            @pl.when(j == 0)
            def _():
                dq_ref[0, at, :] = dqc

            @pl.when(j > 0)
            def _():
                dq_ref[0, at, :] += dqc

            dk_sc[...] += scale * lax.dot_general(dsb, qv, (_DIMS["tn"], ((), ())), preferred_element_type=F32)

        @pl.when(i == nb - 1)
        def _():
            dk_ref[0] = dk_sc[...]
            dv_ref[0] = dv_sc[...]
            dc_ref[0] = dc_sc[...]

    qspec = pl.BlockSpec((1, t, p), lambda hh, j, i: (hh, jnp.maximum(i, j), 0))
    kspec = pl.BlockSpec((1, t, p), lambda hh, j, i: (hh, j, 0))
    cq = pl.BlockSpec((1, 1, t), lambda hh, j, i: (hh, 0, jnp.maximum(i, j)))
    ck = pl.BlockSpec((1, 1, t), lambda hh, j, i: (hh, 0, j))
    return pl.pallas_call(
        body, name=name, grid=(h, nb, nb),
        in_specs=[qspec, kspec, kspec, cq, ck, qspec,
                  pl.BlockSpec((1, t, LANES), lambda hh, j, i: (hh, jnp.maximum(i, j), 0)), qspec],
        out_specs=[pl.BlockSpec((1, s, p), lambda hh, j, i: (hh, 0, 0)), kspec, kspec, ck],
        out_shape=[jax.ShapeDtypeStruct((h, s, p), F32), jax.ShapeDtypeStruct((h, s, p), F32),
                   jax.ShapeDtypeStruct((h, s, p), F32), jax.ShapeDtypeStruct((h, 1, s), F32)],
        scratch_shapes=[pltpu.VMEM((t, p), F32), pltpu.VMEM((t, p), F32), pltpu.VMEM((1, t), F32)],
        compiler_params=_cparams(("parallel", "arbitrary", "arbitrary")),
    )(q, k, v, c, c, o, lse, do)


AUX = 64


def _pack(main, cols):
    h, s, p = main.shape
    parts = [main.astype(BF16)]
    if cols:
        parts.append(jnp.stack(cols, axis=-1).astype(BF16))
    parts.append(jnp.zeros((h, s, LANES - p - len(cols)), BF16))
    return jnp.concatenate(parts, axis=-1)


def _terms(v):
    hi = lax.reduce_precision(v, 8, 7)
    mid = lax.reduce_precision(v - hi, 8, 7)
    lo = lax.reduce_precision(v - hi - mid, 8, 7)
    return [hi, mid, lo]


def _fox_pack_qkv(q, k, v):
    h, s, _ = q.shape
    one = jnp.ones((h, s), F32)
    return _pack(q * (FOX_HEADDIM ** -0.5), []), _pack(k, []), _pack(v, [one, one, one])


def _fox_bias(c_ref, qblock, kblock, t):
    lane = lax.broadcasted_iota(jnp.int32, (1, LANES), 1)
    cq = c_ref[0, :, pl.ds(pl.multiple_of(qblock * t, LANES), LANES)]
    cref = jnp.sum(jnp.where(lane == 0, cq, 0.0), axis=1, keepdims=True)
    return cref - c_ref[0, :, pl.ds(pl.multiple_of(kblock * t, LANES), t)]


def _fox_rowdot(do, o, *, tm=256, name):
    s, d = do.shape
    tm = min(tm, s)

    def body(do_ref, o_ref, d_ref):
        prod = do_ref[...].astype(BF16).astype(F32) * o_ref[...]
        r = lax.broadcasted_iota(jnp.int32, (d, LANES), 0)
        c = lax.broadcasted_iota(jnp.int32, (d, LANES), 1)
        mine = (r >= c * FOX_HEADDIM) & (r < (c + 1) * FOX_HEADDIM)
        d_ref[...] = _mask_dot(mine.astype(BF16), prod, "vn")

    row = pl.BlockSpec((tm, d), lambda i: (i, 0))
    return pl.pallas_call(body, name=name, grid=(s // tm,), in_specs=[row, row],
                          out_specs=pl.BlockSpec((tm, LANES), lambda i: (i, 0)),
                          out_shape=jax.ShapeDtypeStruct((s, LANES), F32),
                          compiler_params=_cparams(("parallel",)))(do, o)


def _causal(t):
    return lax.broadcasted_iota(jnp.int32, (t, t), 0) >= lax.broadcasted_iota(jnp.int32, (t, t), 1)


def _fox2_fwd(qp, kp, vp, c, *, name):
    h, s, _ = qp.shape
    t = min(FOX_BLOCK, s)
    nb = s // t
    nt = (((1,), (1,)), ((), ()))
    nn = (((1,), (0,)), ((), ()))

    def body(q_ref, k_ref, v_ref, c_ref, o_ref, lse_ref, m_sc, acc_sc):
        i = pl.program_id(1)
        m_sc[...] = jnp.full_like(m_sc, NEG)
        acc_sc[...] = jnp.zeros_like(acc_sc)
        qv = q_ref[0]

        def step(j, masked):
            at = pl.ds(pl.multiple_of(j * t, t), t)
            kv, vv = k_ref[0, at, :], v_ref[0, at, :]
            sc = lax.dot_general(qv, kv, nt, preferred_element_type=F32) + _fox_bias(c_ref, i, j, t)
            if masked:
                sc = jnp.where(_causal(t), sc, NEG)
            m_prev = m_sc[...]
            m_new = jnp.maximum(m_prev, jnp.max(sc, axis=1, keepdims=True))
            pr = jnp.exp(sc - jnp.tile(m_new, (1, t // LANES)))
            pr_hi = pr.astype(BF16)
            pr_lo = (pr - pr_hi.astype(F32)).astype(BF16)
            pv = (lax.dot_general(pr_hi, vv, nn, preferred_element_type=F32)
                  + lax.dot_general(pr_lo, vv, nn, preferred_element_type=F32))
            acc_sc[...] = jnp.exp(m_prev - m_new) * acc_sc[...] + pv
            m_sc[...] = m_new

        lax.fori_loop(0, i, lambda j, carry: (step(j, False), carry)[1], 0)
        step(i, True)
        acc = acc_sc[...]
        lane = lax.broadcasted_iota(jnp.int32, (1, LANES), 1)
        den = jnp.sum(jnp.where(lane == AUX, acc, 0.0), axis=1, keepdims=True)
        o_ref[0] = (acc / den)[:, :FOX_HEADDIM]
        lse_ref[0] = m_sc[...] + jnp.log(den)

    whole = pl.BlockSpec((1, s, LANES), lambda hh, i: (hh, 0, 0))
    return pl.pallas_call(
        body, name=name, grid=(h, nb),
        in_specs=[pl.BlockSpec((1, t, LANES), lambda hh, i: (hh, i, 0)), whole, whole,
                  pl.BlockSpec((1, 1, s), lambda hh, i: (hh, 0, 0))],
        out_specs=[pl.BlockSpec((1, t, FOX_HEADDIM), lambda hh, i: (hh, i, 0)),
                   pl.BlockSpec((1, t, LANES), lambda hh, i: (hh, i, 0))],
        out_shape=[jax.ShapeDtypeStruct((h, s, FOX_HEADDIM), F32), jax.ShapeDtypeStruct((h, s, LANES), F32)],
        scratch_shapes=[pltpu.VMEM((t, LANES), F32), pltpu.VMEM((t, LANES), F32)],
        compiler_params=_cparams(("parallel", "arbitrary")),
    )(qp, kp, vp, c)


def _fox2_bwd(qp, kp, vp, c, dop, lse, *, name):
    h, s, _ = qp.shape
    t = min(FOX_BLOCK, s)
    nb = s // t
    nt = (((1,), (1,)), ((), ()))
    nn = (((1,), (0,)), ((), ()))
    tn = (((0,), (0,)), ((), ()))

    def body(k_ref, v_ref, q_ref, c_ref, do_ref, lse_ref, dq_ref, dk_ref, dv_ref, dc_ref, dk_sc, dv_sc, dc_sc):
        j = pl.program_id(1)

        @pl.when(j == 0)
        def _():
            dq_ref[...] = jnp.zeros_like(dq_ref)

        dk_sc[...] = jnp.zeros_like(dk_sc)
        dv_sc[...] = jnp.zeros_like(dv_sc)
        dc_sc[...] = jnp.zeros_like(dc_sc)
        kv, vv = k_ref[0], v_ref[0]

        def step(i, masked):
            at = pl.ds(pl.multiple_of(i * t, t), t)
            qv, dov = q_ref[0, at, :], do_ref[0, at, :]
            sc = lax.dot_general(qv, kv, nt, preferred_element_type=F32) + _fox_bias(c_ref, i, j, t)
            pr = jnp.exp(sc - jnp.tile(lse_ref[0, at, :], (1, t // LANES)))
            if masked:
                pr = jnp.where(_causal(t), pr, 0.0)
            ds = pr * lax.dot_general(dov, vv, nt, preferred_element_type=F32)
            dc_sc[...] -= jnp.sum(ds, axis=0, keepdims=True)
            dsb = ds.astype(BF16)
            dv_sc[...] += lax.dot_general(pr.astype(BF16), dov, tn, preferred_element_type=F32)
            dk_sc[...] += lax.dot_general(dsb, qv, tn, preferred_element_type=F32)
            dq_ref[0, at, :] += lax.dot_general(dsb, kv, nn, preferred_element_type=F32)

        step(j, True)
        lax.fori_loop(j + 1, nb, lambda i, carry: (step(i, False), carry)[1], 0)
        dk_ref[0] = dk_sc[...]
        dv_ref[0] = dv_sc[...].astype(dv_ref.dtype)
        dc_ref[0] = dc_sc[...]

    whole = pl.BlockSpec((1, s, LANES), lambda hh, j: (hh, 0, 0))
    blk = pl.BlockSpec((1, t, LANES), lambda hh, j: (hh, j, 0))
    return pl.pallas_call(
        body, name=name, grid=(h, nb),
        in_specs=[blk, blk, whole, pl.BlockSpec((1, 1, s), lambda hh, j: (hh, 0, 0)), whole, whole],
        out_specs=[whole, blk, blk, pl.BlockSpec((1, 1, t), lambda hh, j: (hh, 0, j))],
        out_shape=[jax.ShapeDtypeStruct((h, s, LANES), F32), jax.ShapeDtypeStruct((h, s, LANES), F32),
                   jax.ShapeDtypeStruct((h, s, LANES), BF16), jax.ShapeDtypeStruct((h, 1, s), F32)],
        scratch_shapes=[pltpu.VMEM((t, LANES), F32), pltpu.VMEM((t, LANES), F32), pltpu.VMEM((1, t), F32)],
        compiler_params=_cparams(("parallel", "arbitrary")),
    )(kp, vp, qp, c, dop, lse)


PAIRS = FOX_HEADS // 2
HALF = LANES // 2


def _first_half():
    return lax.broadcasted_iota(jnp.int32, (1, LANES), 1) < HALF


def _pair_bias(c_ref, hh, qblock, kblock, t):
    lane = lax.broadcasted_iota(jnp.int32, (1, LANES), 1)
    cq = c_ref[hh, :, pl.ds(pl.multiple_of(qblock * t, LANES), LANES)]
    cref = jnp.sum(jnp.where(lane == 0, cq, 0.0), axis=1, keepdims=True)
    return cref - c_ref[hh, :, pl.ds(pl.multiple_of(kblock * t, LANES), t)]


def _fox_dopack(do, o, *, tm=256, name):
    s, d = do.shape
    tm = min(tm, s)

    def body(do_ref, o_ref, out_ref):
        dov = do_ref[...].astype(BF16)
        prod = dov.astype(F32) * o_ref[...]
        r = lax.broadcasted_iota(jnp.int32, (d, LANES), 0)
        c = lax.broadcasted_iota(jnp.int32, (d, LANES), 1)
        heads = ((r >= c * FOX_HEADDIM) & (r < (c + 1) * FOX_HEADDIM)).astype(BF16)
        negd = -_mask_dot(heads, prod, "vn")
        hr = lax.broadcasted_iota(jnp.int32, (LANES, 2 * d), 0)
        col = lax.broadcasted_iota(jnp.int32, (LANES, 2 * d), 1)
        base = (hr >> 1) * (2 * LANES) + jnp.where((hr & 1) == 0, HALF, LANES)
        terms = None
        for kk, part in enumerate(_split3(negd)):
            place = ((col == base + kk) & (hr < FOX_HEADS)).astype(BF16)
            tk = lax.dot_general(part, place, (_DIMS["nn"], ((), ())), preferred_element_type=F32)
            terms = tk if terms is None else terms + tk
        first = _first_half()
        zero = jnp.zeros((tm, LANES), BF16)
        pieces = []
        for hp in range(PAIRS):
            blk = dov[:, hp * LANES:(hp + 1) * LANES]
            pieces += [jnp.where(first, blk, zero), jnp.where(first, zero, blk)]
        out_ref[...] = (jnp.concatenate(pieces, axis=1).astype(F32) + terms).astype(BF16)

    row = pl.BlockSpec((tm, d), lambda i: (i, 0))
    return pl.pallas_call(body, name=name, grid=(s // tm,), in_specs=[row, row],
                          out_specs=pl.BlockSpec((tm, 2 * d), lambda i: (i, 0)),
                          out_shape=jax.ShapeDtypeStruct((s, 2 * d), BF16),
                          compiler_params=_cparams(("parallel",)))(do, o)


FOX_DEAD = 110.0
BOUND_SLACK = 1.001


def _fox_block_bounds(qkv, c):
    s = qkv.shape[0]
    t = min(FOX_BLOCK, s)
    k = qkv[:, D_FOX:2 * D_FOX].astype(F32).reshape(s // t, t, FOX_HEADS, FOX_HEADDIM)
    kn = jnp.sqrt(jnp.max(jnp.sum(k * k, axis=-1), axis=1)).T
    return kn, c[:, 0, ::t], c[:, 0, t - 1::t]


def _fox_bound(qn, kn_ref, cs_ref, ce_ref, h, qblock, kblock):
    return qn * (kn_ref[h, kblock] * BOUND_SLACK) + (cs_ref[h, qblock] - ce_ref[h, kblock])


def _fox3_fwd(qkv, c, bounds, *, carry=None, name):
    assert qkv.shape[0] // min(FOX_BLOCK, qkv.shape[0]) < LANES - 2
    s = qkv.shape[0]
    t = min(FOX_BLOCK, s)
    nb = s // t
    nt = (((1,), (1,)), ((), ()))
    nn = (((1,), (0,)), ((), ()))
    scale = FOX_HEADDIM ** -0.5

    def body(*refs):
        ins, (o_ref, mrun_ref), (m_sc, acc_sc, mt_sc), comm = _carried(carry, refs, 7, 2, 3)
        q_ref, k_ref, v_ref, c_ref, kn_ref, cs_ref, ce_ref = ins
        hp, i = pl.program_id(0), pl.program_id(1)
        if carry is not None:
            @pl.when((hp == 0) & (i == 0))
            def _():
                carry.start(*comm)

        first = _first_half()
        lane = lax.broadcasted_iota(jnp.int32, (1, LANES), 1)
        m_sc[...] = jnp.full_like(m_sc, NEG)
        acc_sc[...] = jnp.zeros_like(acc_sc)
        mt_sc[...] = jnp.zeros_like(mt_sc)
        q2 = q_ref[...] * scale
        zero = jnp.zeros_like(q2)
        qs = [jnp.where(first, q2, zero), jnp.where(first, zero, q2)]
        qn = []
        for hh in range(2):
            qf = qs[hh].astype(F32)
            qn.append(jnp.broadcast_to(jnp.sqrt(jnp.sum(qf * qf, axis=1, keepdims=True)), (t, LANES)))

        def head_block(hh, j, k2, vxh, masked):
            sc = lax.dot_general(qs[hh], k2, nt, preferred_element_type=F32) + _pair_bias(c_ref, hh, i, j, t)
            if masked:
                sc = jnp.where(_causal(t), sc, NEG)
            m_prev = m_sc[hh]
            m_new = jnp.maximum(m_prev, jnp.max(sc, axis=1, keepdims=True))
            pr = jnp.exp(sc - jnp.tile(m_new, (1, t // LANES))).astype(BF16)
            pv = lax.dot_general(pr, vxh, nn, preferred_element_type=F32)
            acc_sc[hh] = jnp.exp(m_prev - m_new) * acc_sc[hh] + pv
            m_sc[hh] = m_new

        def step(j, masked):
            at = pl.ds(pl.multiple_of(j * t, t), t)
            k2, v2 = k_ref[at, :], v_ref[at, :]
            one = jnp.ones_like(v2)
            vx = [jnp.where(first, v2, one), jnp.where(first, one, v2)]
            for hh in range(2):
                if masked:
                    head_block(hh, j, k2, vx[hh], True)
                else:
                    bound = _fox_bound(qn[hh], kn_ref, cs_ref, ce_ref, 2 * hp + hh, i, j)

                    @pl.when(jnp.max(bound - m_sc[hh]) > -FOX_DEAD)
                    def _(hh=hh):
                        head_block(hh, j, k2, vx[hh], False)

                mt_sc[hh] = jnp.where(lane == j, m_sc[hh], mt_sc[hh])

        step(i, True)
        lax.fori_loop(0, i, lambda n, carry: (step(i - 1 - n, False), carry)[1], 0)
        acc_a, acc_b = acc_sc[0], acc_sc[1]
        den_a = jnp.where(first, pltpu.roll(acc_a, HALF, 1), acc_a)
        den_b = jnp.where(first, acc_b, pltpu.roll(acc_b, HALF, 1))
        o_ref[...] = jnp.where(first, acc_a / den_a, acc_b / den_b)
        for hh, den in enumerate((den_a, den_b)):
            stats = jnp.where(lane == LANES - 1, m_sc[hh] + jnp.log(den), mt_sc[hh])
            mrun_ref[:, hh * LANES:(hh + 1) * LANES] = jnp.where(lane == LANES - 2, qn[hh], stats)
        if carry is not None:
            @pl.when((pl.program_id(0) == PAIRS - 1) & (i == nb - 1))
            def _():
                carry.wait(*comm)

    nq = D_FOX // LANES
    smem = pl.BlockSpec(memory_space=pltpu.SMEM)
    extra = carry if carry is not None else _Exchange([], "c", False)
    return pl.pallas_call(
        body, name=name, grid=(PAIRS, nb),
        in_specs=[pl.BlockSpec((t, LANES), lambda hp, i: (i, hp)),
                  pl.BlockSpec((s, LANES), lambda hp, i: (0, nq + hp)),
                  pl.BlockSpec((s, LANES), lambda hp, i: (0, 2 * nq + hp)),
                  pl.BlockSpec((2, 1, s), lambda hp, i: (hp, 0, 0))] + [smem] * 3 + extra.in_specs,
        out_specs=[pl.BlockSpec((t, LANES), lambda hp, i: (i, hp)),
                   pl.BlockSpec((t, 2 * LANES), lambda hp, i: (i, hp))] + extra.out_specs,
        out_shape=[jax.ShapeDtypeStruct((s, D_FOX), F32), jax.ShapeDtypeStruct((s, 2 * D_FOX), F32)] + extra.out_shape,
        scratch_shapes=[pltpu.VMEM((2, t, LANES), F32)] * 3 + (carry.scratch if carry is not None else []),
        compiler_params=_cparams(("arbitrary", "arbitrary")),
    )(qkv, qkv, qkv, c, *bounds, *extra.arrs)


def _fox3_bwd(qkv, c, bounds, dox, mrun, *, carry=None, name):
    s = qkv.shape[0]
    t = min(FOX_BLOCK, s)
    nb = s // t
    nt = (((1,), (1,)), ((), ()))
    nn = (((1,), (0,)), ((), ()))
    tn = (((0,), (0,)), ((), ()))
    scale = FOX_HEADDIM ** -0.5

    def body(*refs):
        ins, outs, scratch, comm = _carried(carry, refs, 9, 4, 3)
        k_ref, v_ref, q_ref, c_ref, do_ref, mrun_ref, kn_ref, cs_ref, ce_ref = ins
        dq_ref, dk_ref, dv_ref, dc_ref = outs
        dk_sc, dv_sc, dc_sc = scratch
        hp, j = pl.program_id(0), pl.program_id(1)
        if carry is not None:
            @pl.when((hp == 0) & (j == 0))
            def _():
                carry.start(*comm)

        first = _first_half()
        halves = [first, jnp.logical_not(first)]
        lane = lax.broadcasted_iota(jnp.int32, (1, LANES), 1)

        @pl.when(j == 0)
        def _():
            dq_ref[...] = jnp.zeros_like(dq_ref)

        dk_sc[...] = jnp.zeros_like(dk_sc)
        dv_sc[...] = jnp.zeros_like(dv_sc)
        dc_sc[...] = jnp.zeros_like(dc_sc)
        k2, v2 = k_ref[...], v_ref[...]
        one = jnp.ones_like(v2)
        vx = [jnp.where(first, v2, one), jnp.where(first, one, v2)]

        def pick(stats, which):
            return jnp.sum(jnp.where(lane == which, stats, 0.0), axis=1, keepdims=True)

        def head_block(hh, i, at, qsh, stats, masked):
            mine = slice(hh * LANES, (hh + 1) * LANES)
            dov = do_ref[at, mine]
            sc = lax.dot_general(qsh, k2, nt, preferred_element_type=F32) + _pair_bias(c_ref, hh, i, j, t)
            mj = pick(stats, j)
            gain = jnp.broadcast_to(jnp.exp(mj - pick(stats, LANES - 1)), (t, LANES))
            mj = jnp.broadcast_to(mj, (t, LANES))
            pb = jnp.exp(sc - jnp.tile(mj, (1, t // LANES))).astype(BF16)
            if masked:
                pb = jnp.where(_causal(t), pb, jnp.zeros_like(pb))
            pr = pb.astype(F32) * jnp.tile(gain, (1, t // LANES))
            ds = pr * lax.dot_general(dov, vx[hh], nt, preferred_element_type=F32)
            dc_sc[hh] -= jnp.sum(ds, axis=0, keepdims=True)
            dsb = ds.astype(BF16)
            dvh = lax.dot_general(pr.astype(BF16), dov, tn, preferred_element_type=F32)
            dv_sc[...] += jnp.where(halves[hh], dvh, 0.0)
            dk_sc[...] += lax.dot_general(dsb, qsh, tn, preferred_element_type=F32)
            dqh = lax.dot_general(dsb, k2, nn, preferred_element_type=F32)
            dq_ref[at, :] += jnp.where(halves[hh], dqh, 0.0)

        def step(i, masked):
            at = pl.ds(pl.multiple_of(i * t, t), t)
            q2 = q_ref[at, :] * scale
            zero = jnp.zeros_like(q2)
            qs = [jnp.where(first, q2, zero), jnp.where(first, zero, q2)]
            for hh in range(2):
                stats = mrun_ref[at, hh * LANES:(hh + 1) * LANES]
                if masked:
                    head_block(hh, i, at, qs[hh], stats, True)
                else:
                    bound = _fox_bound(pick(stats, LANES - 2), kn_ref, cs_ref, ce_ref, 2 * hp + hh, i, j)

                    @pl.when(jnp.max(bound - pick(stats, j + 1)) > -FOX_DEAD)
                    def _(hh=hh, stats=stats):
                        head_block(hh, i, at, qs[hh], stats, False)

        step(j, True)
        lax.fori_loop(j + 1, nb, lambda i, carry: (step(i, False), carry)[1], 0)
        dk_ref[...] = dk_sc[...]
        dv_ref[...] = dv_sc[...].astype(dv_ref.dtype)
        dc_ref[...] = dc_sc[...]
        if carry is not None:
            @pl.when((pl.program_id(0) == PAIRS - 1) & (j == nb - 1))
            def _():
                carry.wait(*comm)

    nq = D_FOX // LANES
    blk = pl.BlockSpec((t, LANES), lambda hp, j: (j, hp))
    smem = pl.BlockSpec(memory_space=pltpu.SMEM)
    extra = carry if carry is not None else _Exchange([], "c", False)
    return pl.pallas_call(
        body, name=name, grid=(PAIRS, nb),
        in_specs=[pl.BlockSpec((t, LANES), lambda hp, j: (j, nq + hp)),
                  pl.BlockSpec((t, LANES), lambda hp, j: (j, 2 * nq + hp)),
                  pl.BlockSpec((s, LANES), lambda hp, j: (0, hp)),
                  pl.BlockSpec((2, 1, s), lambda hp, j: (hp, 0, 0)),
                  pl.BlockSpec((s, 2 * LANES), lambda hp, j: (0, hp)),
                  pl.BlockSpec((s, 2 * LANES), lambda hp, j: (0, hp))] + [smem] * 3 + extra.in_specs,
        out_specs=[pl.BlockSpec((s, LANES), lambda hp, j: (0, hp)), blk, blk,
                   pl.BlockSpec((2, 1, t), lambda hp, j: (hp, 0, j))] + extra.out_specs,
        out_shape=[jax.ShapeDtypeStruct((s, D_FOX), F32), jax.ShapeDtypeStruct((s, D_FOX), F32),
                   jax.ShapeDtypeStruct((s, D_FOX), BF16), jax.ShapeDtypeStruct((FOX_HEADS, 1, s), F32)]
        + extra.out_shape,
        scratch_shapes=[pltpu.VMEM((t, LANES), F32), pltpu.VMEM((t, LANES), F32), pltpu.VMEM((2, 1, t), F32)]
        + (carry.scratch if carry is not None else []),
        compiler_params=_cparams(("arbitrary", "arbitrary")),
    )(qkv, qkv, qkv, c, dox, mrun, *bounds, *extra.arrs)


def _final(x1, out1, g, tgt, *, tm=256, name):
    s, d = x1.shape
    tm = min(tm, s)

    def body(x_ref, o_ref, g_ref, t_ref, dx_ref, do_ref, dg_ref, loss_ref):
        i = pl.program_id(0)

        @pl.when(i == 0)
        def _():
            dg_ref[...] = jnp.zeros_like(dg_ref)
            loss_ref[...] = jnp.zeros_like(loss_ref)

        tv = t_ref[...]

        def lossf(xv, ov, gv):
            err = jnp.square(xv + _rms(ov, gv) - tv)
            return 0.5 * jnp.sum(jnp.mean(err, axis=-1, keepdims=True), axis=0, keepdims=True)

        val, vjp = jax.vjp(lossf, x_ref[...], o_ref[...], g_ref[...])
        dx, do, dg = vjp(jnp.ones((1, 1), F32))
        dx_ref[...] = dx
        do_ref[...] = do.astype(do_ref.dtype)
        dg_ref[...] += dg
        loss_ref[...] += val

    row = pl.BlockSpec((tm, d), lambda i: (i, 0))
    par = pl.BlockSpec((1, d), lambda i: (0, 0))
    return pl.pallas_call(
        body, name=name, grid=(s // tm,), in_specs=[row, row, par, row],
        out_specs=[row, row, par, pl.BlockSpec((1, 1), lambda i: (0, 0))],
        out_shape=[jax.ShapeDtypeStruct((s, d), F32), jax.ShapeDtypeStruct((s, d), BF16),
                   jax.ShapeDtypeStruct((1, d), F32), jax.ShapeDtypeStruct((1, 1), F32)],
        compiler_params=_cparams(("arbitrary",)),
    )(x1, out1, g, tgt)


def _row_tile(r):
    return LANES if r % LANES == 0 else r


def _sum_slots(parts, *, out_dtype=F32, name):
    p, r, c = parts.shape
    tr = _row_tile(r)

    def body(p_ref, o_ref):
        acc = p_ref[0].astype(F32)
        for k in range(1, p):
            acc = acc + p_ref[k].astype(F32)
        o_ref[...] = acc.astype(o_ref.dtype)

    return pl.pallas_call(
        body, name=name, grid=(r // tr,),
        in_specs=[pl.BlockSpec((p, tr, c), lambda i: (0, i, 0))],
        out_specs=pl.BlockSpec((tr, c), lambda i: (i, 0)),
        out_shape=jax.ShapeDtypeStruct((r, c), out_dtype),
        compiler_params=_cparams(("parallel",)),
    )(parts)


def _adamw(w, gparts, m, v, *, name):
    r, c = w.shape
    p = gparts.shape[0]
    tr = _row_tile(r)

    def body(w_ref, g_ref, m_ref, v_ref, go_ref, d_ref, mo_ref, vo_ref):
        g = g_ref[0].astype(F32)
        for k in range(1, p):
            g = g + g_ref[k].astype(F32)
        mn = ADAM_B1 * m_ref[...] + (1.0 - ADAM_B1) * g
        vn = ADAM_B2 * v_ref[...] + (1.0 - ADAM_B2) * jnp.square(g)
        m_hat = mn / (1.0 - ADAM_B1 ** ADAM_STEP)
        v_hat = vn / (1.0 - ADAM_B2 ** ADAM_STEP)
        go_ref[...] = g
        d_ref[...] = -ADAM_LR * (m_hat / (jnp.sqrt(v_hat) + ADAM_EPS) + ADAM_WD * w_ref[...])
        mo_ref[...] = mn
        vo_ref[...] = vn

    spec = pl.BlockSpec((tr, c), lambda i: (i, 0))
    return pl.pallas_call(
        body, name=name, grid=(r // tr,),
        in_specs=[spec, pl.BlockSpec((p, tr, c), lambda i: (0, i, 0)), spec, spec],
        out_specs=[spec] * 4, out_shape=[jax.ShapeDtypeStruct((r, c), F32)] * 4,
        compiler_params=_cparams(("parallel",)),
    )(w, gparts, m, v)


_FLIPS = {
    "xy": [(1, 0, 0), (0, 1, 0), (1, 1, 0)],
    "c": [(0, 0, 1)],
    "xyc": [(fx, fy, fc) for fx in (0, 1) for fy in (0, 1) for fc in (0, 1) if (fx, fy, fc) != (0, 0, 0)],
}


def _slot(mode, px, py, pc):
    return {"xy": 2 * px + py, "c": pc, "xyc": 4 * px + 2 * py + pc}[mode]


class _Exchange:
    def __init__(self, arrs, mode, scatter):
        self.arrs, self.mode, self.scatter = list(arrs), mode, scatter
        self.n = len(self.arrs)
        self.flips = _FLIPS[mode]
        nf = len(self.flips)
        anyspec = pl.BlockSpec(memory_space=pl.ANY)
        self.in_specs = [anyspec] * self.n
        self.out_specs = [anyspec] * self.n
        self.out_shape = [jax.ShapeDtypeStruct((nf + 1,) + (a.shape[1:] if scatter else a.shape), a.dtype)
                          for a in self.arrs]
        self.scratch = [pltpu.SemaphoreType.DMA((self.n * nf,)), pltpu.SemaphoreType.DMA((self.n * nf,)),
                        pltpu.SemaphoreType.DMA((self.n,))]

    def _copies(self, ins, outs, sems, arrivals=True):
        send, recv, loc = sems
        nf = len(self.flips)
        x, y, c = lax.axis_index("x"), lax.axis_index("y"), lax.axis_index("c")
        me = _slot(self.mode, x, y, c)
        peers = [(x ^ fx, y ^ fy, c ^ fc) for (fx, fy, fc) in self.flips]

        def src(a, slot):
            return ins[a].at[slot] if self.scatter else ins[a]

        def copy(a, j, dst_slot):
            return pltpu.make_async_remote_copy(
                src_ref=src(a, _slot(self.mode, *peers[j])), dst_ref=outs[a].at[dst_slot],
                send_sem=send.at[a * nf + j], recv_sem=recv.at[a * nf + j], device_id=peers[j], device_id_type=MESH)

        pairs = [(a, j) for a in range(self.n) for j in range(nf)]
        local = [pltpu.make_async_copy(src(a, me), outs[a].at[me], loc.at[a]) for a in range(self.n)]
        sends = [copy(a, j, me) for a, j in pairs]
        recvs = [copy(a, j, _slot(self.mode, *peers[j])) for a, j in pairs] if arrivals else []
        return local, sends, recvs

    def start(self, ins, outs, sems):
        local, sends, _ = self._copies(ins, outs, sems, arrivals=False)
        for cp in local + sends:
            cp.start()

    def wait(self, ins, outs, sems):
        local, sends, recvs = self._copies(ins, outs, sems)
        for cp in recvs:
            cp.wait_recv()
        for cp in sends:
            cp.wait_send()
        for cp in local:
            cp.wait()


def _carried(carry, refs, n_in, n_out, n_scratch):
    k = carry.n if carry is not None else 0
    ins, refs = refs[:n_in], refs[n_in:]
    cin, refs = refs[:k], refs[k:]
    outs, refs = refs[:n_out], refs[n_out:]
    cout, refs = refs[:k], refs[k:]
    scratch, sems = refs[:n_scratch], refs[n_scratch:]
    return ins, outs, scratch, (cin, cout, sems)


def _exchanges(exs, *, name):
    counts = [ex.n for ex in exs]
    total = sum(counts)

    def body(*refs):
        ins, outs, sems = refs[:total], refs[total:2 * total], refs[2 * total:]
        comms, at = [], 0
        for k, ex in enumerate(exs):
            comms.append((ins[at:at + ex.n], outs[at:at + ex.n], sems[3 * k:3 * k + 3]))
            at += ex.n
        for ex, comm in zip(exs, comms):
            ex.start(*comm)
        for ex, comm in zip(exs, comms):
            ex.wait(*comm)

    res = pl.pallas_call(
        body, name=name, in_specs=[sp for ex in exs for sp in ex.in_specs],
        out_specs=[sp for ex in exs for sp in ex.out_specs], out_shape=[sh for ex in exs for sh in ex.out_shape],
        scratch_shapes=[sc for ex in exs for sc in ex.scratch])(*[a for ex in exs for a in ex.arrs])
    out, at = [], 0
    for n in counts:
        out.append(list(res[at:at + n]))
        at += n
    return out


def _exchange(arrs, mode, scatter, *, name):
    return _exchanges([_Exchange(arrs, mode, scatter)], name=name)[0]


def _softplus(v):
    return jnp.maximum(v, 0.0) + jnp.log1p(jnp.exp(-jnp.abs(v)))


def _pad_lanes(v):
    r, n = v.shape
    return jnp.pad(v, ((0, 0), (0, -n % LANES)))


def _to_heads(v):
    s = v.shape[0]
    return v.reshape(s, -1, 64).transpose(1, 0, 2)


def _from_heads(v):
    h, s, p = v.shape
    return v.transpose(1, 0, 2).reshape(s, h * p)


def _fn_rms(v, g):
    return (_rms(v, g),)


def _fn_post(xv, ov, g):
    return (xv + _rms(ov, g),)


def _fn_act(xbc, dtp, fp, dtb, fb):
    return _silu(xbc), _softplus(dtp + dtb), -_softplus(-(fp + fb))


def _fn_mix(y, zs, o, zf, g):
    yg = y * _silu(zs)
    sq = yg * yg
    lane = lax.broadcasted_iota(jnp.int32, (1, D_SSD), 1)
    width = D_SSD // SSD_GROUPS
    rstd = jnp.zeros_like(yg)
    for gi in range(SSD_GROUPS):
        msk = ((lane >= gi * width) & (lane < (gi + 1) * width)).astype(F32)
        ms = jnp.sum(sq * msk, axis=1, keepdims=True) / width
        rstd = rstd + lax.rsqrt(ms + EPS) * msk
    return (jnp.concatenate([yg * rstd * g, o * _silu(zf)], axis=1),)


def _fn_glu(val, gate):
    return (val * jax.nn.sigmoid(gate),)


def _fn_ln(hc, z, g, b):
    mu = jnp.mean(hc, axis=-1, keepdims=True)
    xc = hc - mu
    yn = xc * lax.rsqrt(jnp.mean(xc * xc, axis=-1, keepdims=True) + EPS) * g + b
    return (_silu(yn) * _silu(z),)


class _NoComm:
    def odd_weights(self):
        return None

    def got_odd_weights(self, got, w):
        pass

    def early_grads(self, g):
        return None

    def got_early_grads(self, got):
        pass

    def early_sums(self):
        return None

    def got_early_sums(self, got):
        pass


def _local_step(x, tgt, w, comm=None):
    comm = comm or _NoComm()
    s = x.shape[0]
    d = D_MODEL
    tm = 256
    bf = lambda v: v.astype(BF16)
    c1 = lambda arr: _col(arr, 0, arr.shape[1])
    g = {}

    ew = w["e_w_in"]
    w_z, w_xbc = bf(ew[:, 0:2048]), bf(ew[:, 2048:4096])
    w_dt = bf(_pad_lanes(ew[:, 4096:4112]))
    w_qkv = bf(ew[:, 4112:7184])
    w_f = bf(_pad_lanes(ew[:, 7184:7200]))
    w_eo = bf(w["e_w_out"])
    dtb, fgb = _pad_lanes(w["e_dt_bias"]), _pad_lanes(w["e_fgate_b"])
    alog, dsk = _pad_lanes(w["e_a_log"]), _pad_lanes(w["e_d_skip"])

    (u0,) = _rowwise_fwd(_fn_rms, [c1(x)], [c1(w["e_norm_pre"])], [(d, BF16)], tm=tm, name="e_pre")
    z = _mm(u0, w_z, name="e_in_z")
    xbc_raw = _mm(u0, w_xbc, out_dtype=BF16, name="e_in_xbc")
    qkv = _mm(u0, w_qkv, out_dtype=BF16, name="e_in_qkv")
    dtp = _mm(u0, w_dt, name="e_in_dt")
    fp = _mm(u0, w_f, name="e_in_f")
    xbc_pre = _conv_fwd(xbc_raw, w["e_conv_w"], w["e_conv_b"], name="e_conv")
    act_rows = [c1(xbc_pre), c1(dtp), c1(fp)]
    act_pars = [c1(dtb), c1(fgb)]
    xbc, dt, lf = _rowwise_fwd(_fn_act, act_rows, act_pars, [(2048, F32), (LANES, F32), (LANES, F32)],
                               tm=tm, name="e_act")
    y, hsave = _ssd2_fwd(dt, xbc, alog, dsk, name="e_ssd")
    csum = _cumsum_lanes(lf[:, :FOX_HEADS].T, reverse=False, name="e_cumsum").reshape(FOX_HEADS, 1, s)
    bounds = _fox_block_bounds(qkv, csum)
    o, mrun, *got = _fox3_fwd(qkv, csum, bounds, carry=comm.odd_weights(), name="e_fox")
    comm.got_odd_weights(got, w)
    w_oi, w_oo = bf(w["o_w_in"]), bf(w["o_w_out"])
    mix_rows = [c1(y), _col(z, 0, D_SSD), c1(o), _col(z, 1, D_FOX)]
    mix_pars = [c1(w["e_ssd_norm"])]
    (hmix,) = _rowwise_fwd(_fn_mix, mix_rows, mix_pars, [(2048, BF16)], tm=tm, name="e_mix")
    out0 = _mm(hmix, w_eo, name="e_out")
    post_rows = [c1(x), c1(out0)]
    (x1,) = _rowwise_fwd(_fn_post, post_rows, [c1(w["e_norm_post"])], [(d, F32)], tm=tm, name="e_post")

    (u1,) = _rowwise_fwd(_fn_rms, [c1(x1)], [c1(w["o_norm_pre"])], [(d, BF16)], tm=tm, name="o_pre")
    p1 = _mm(u1, w_oi, out_dtype=BF16, name="o_in")
    glu_rows = [_col(p1, 0, D_CONV), _col(p1, 1, D_CONV)]
    (hg,) = _rowwise_fwd(_fn_glu, glu_rows, [], [(D_CONV, BF16)], tm=tm, name="o_glu")
    hc = _conv_fwd(hg, w["o_conv_w"], w["o_conv_b"], name="o_conv")
    ln_rows = [c1(hc), _col(p1, 2, D_CONV)]
    ln_pars = [c1(w["o_ln_g"]), c1(w["o_ln_b"])]
    (h2,) = _rowwise_fwd(_fn_ln, ln_rows, ln_pars, [(D_CONV, BF16)], tm=tm, name="o_ln")
    out1 = _mm(h2, w_oo, name="o_out")

    dx2, dout1, g["o_norm_post"], loss = _final(x1, out1, w["o_norm_post"], tgt, name="loss_head")
    dh2 = _mm(dout1, w_oo, tb=True, name="o_out_dx")
    g["o_w_out"] = _mm(h2, dout1, ta=True, name="o_out_dw")
    (dhc, dz1), (g["o_ln_g"], g["o_ln_b"]) = _rowwise_bwd(_fn_ln, ln_rows, ln_pars, [c1(dh2)], [F32, BF16],
                                                         tm=tm, name="o_ln_bwd")
    dhg = _conv_bwd_x(dhc, w["o_conv_w"], name="o_conv_dx")
    g["o_conv_w"], g["o_conv_b"] = _conv_bwd_w(hg, dhc, CONV_WIDTH, name="o_conv_dw")
    (dval, dgate), _ = _rowwise_bwd(_fn_glu, glu_rows, [], [c1(dhg)], [BF16, BF16], tm=tm, name="o_glu_bwd")
    du1 = _mm(dval, w_oi[:, 0:2048], tb=True, name="o_in_dx0")
    du1 = _mm(dgate, w_oi[:, 2048:4096], tb=True, add=du1, name="o_in_dx1")
    du1 = _mm(dz1, w_oi[:, 4096:6144], tb=True, add=du1, name="o_in_dx2")
    g["o_w_in"] = jnp.concatenate([_mm(u1, dval, ta=True, name="o_in_dw0"), _mm(u1, dgate, ta=True, name="o_in_dw1"),
                                   _mm(u1, dz1, ta=True, name="o_in_dw2")], axis=1)
    (dx1,), (g["o_norm_pre"],) = _rowwise_bwd(_fn_rms, [c1(x1)], [c1(w["o_norm_pre"])], [c1(du1)], [F32],
                                              adds={0: c1(dx2)}, tm=tm, name="o_pre_bwd")

    (dout0,), (g["e_norm_post"],) = _rowwise_bwd(_fn_post, post_rows, [c1(w["e_norm_post"])], [c1(dx1)],
                                                 [None, BF16], tm=tm, name="e_post_bwd")
    dhmix = _mm(dout0, w_eo, tb=True, name="e_out_dx")
    g["e_w_out"] = _mm(hmix, dout0, ta=True, name="e_out_dw")
    (dy, dzs, do, dzf), (g["e_ssd_norm"],) = _rowwise_bwd(_fn_mix, mix_rows, mix_pars, [c1(dhmix)],
                                                        [F32, BF16, F32, BF16], tm=tm, name="e_mix_bwd")
    dox = _fox_dopack(do, o, name="e_fox_dopack")
    dq8, dk, dv, dcs, *got = _fox3_bwd(qkv, csum, bounds, dox, mrun, carry=comm.early_grads(g), name="e_fox_bwd")
    comm.got_early_grads(got)
    dlf = _pad_lanes(_cumsum_lanes(dcs.reshape(FOX_HEADS, s), reverse=True, name="e_cumsum_bwd").T)
    dxs, ddt, dbm, dcm, dalog, ddsk, *got = _ssd2_bwd(dt, xbc, alog, dsk, hsave, dy, carry=comm.early_sums(),
                                                       name="e_ssd_bwd")
    comm.got_early_sums(got)
    dxbc = jnp.concatenate([dxs, dbm, dcm], axis=1)
    (dxbc_pre, ddtp, dfp), (ddtb, dfgb) = _rowwise_bwd(_fn_act, act_rows, act_pars, [c1(dxbc), c1(ddt), c1(dlf)],
                                                      [F32, BF16, BF16], tm=tm, name="e_act_bwd")
    dxbc_raw = bf(_conv_bwd_x(dxbc_pre, w["e_conv_w"], name="e_conv_dx"))
    g["e_conv_w"], g["e_conv_b"] = _conv_bwd_w(xbc_raw, dxbc_pre, SSD_CONV, name="e_conv_dw")
    du0 = _mm(dzs, w_z[:, :D_SSD], tb=True, name="e_in_dx0")
    du0 = _mm(dzf, w_z[:, D_SSD:], tb=True, add=du0, name="e_in_dx1")
    du0 = _mm(dxbc_raw, w_xbc, tb=True, add=du0, name="e_in_dx2")
    eighth = FOX_HEADDIM ** -0.5
    du0 = _mm(dq8, w_qkv[:, :D_FOX] * eighth, tb=True, add=du0, name="e_in_dx3q")
    du0 = _mm(dk, w_qkv[:, D_FOX:2 * D_FOX], tb=True, add=du0, name="e_in_dx3k")
    du0 = _mm(dv, w_qkv[:, 2 * D_FOX:], tb=True, add=du0, name="e_in_dx3v")
    du0 = _mm(ddtp, w_dt, tb=True, add=du0, name="e_in_dx4")
    du0 = _mm(dfp, w_f, tb=True, add=du0, name="e_in_dx5")
    g["e_w_in"] = jnp.concatenate([
        _mm(u0, dzs, ta=True, name="e_in_dw0"), _mm(u0, dzf, ta=True, name="e_in_dw1"),
        _mm(u0, dxbc_raw, ta=True, name="e_in_dw2"), _mm(u0, ddtp, ta=True, name="e_in_dw3")[:, :SSD_HEADS],
        _mm(u0, dq8, ta=True, name="e_in_dw4q") * eighth, _mm(u0, dk, ta=True, name="e_in_dw4k"),
        _mm(u0, dv, ta=True, name="e_in_dw4v"), _mm(u0, dfp, ta=True, name="e_in_dw5")[:, :FOX_HEADS]], axis=1)
    (dx,), (g["e_norm_pre"],) = _rowwise_bwd(_fn_rms, [c1(x)], [c1(w["e_norm_pre"])], [c1(du0)], [F32],
                                             adds={0: c1(dx1)}, tm=tm, name="e_pre_bwd")
    g["e_dt_bias"], g["e_fgate_b"] = ddtb[:, :SSD_HEADS], dfgb[:, :FOX_HEADS]
    g["e_a_log"], g["e_d_skip"] = dalog[:, :SSD_HEADS], ddsk[:, :SSD_HEADS]
    return loss, dx, g


_WEIGHTS = ["e_norm_pre", "e_w_in", "e_conv_w", "e_conv_b", "e_dt_bias", "e_a_log", "e_d_skip", "e_fgate_b",
            "e_ssd_norm", "e_w_out", "e_norm_post", "o_norm_pre", "o_w_in", "o_conv_w", "o_conv_b", "o_ln_g",
            "o_ln_b", "o_w_out", "o_norm_post"]
_BIG = ["e_w_in", "e_w_out", "o_w_in", "o_w_out"]
_ROW_SHARDED = ["e_w_out", "o_w_out"]
_SMALL_SHARDED = ["e_conv_w", "o_norm_pre", "o_conv_w", "o_conv_b", "o_ln_g", "o_ln_b", "o_norm_post"]
_REPLICATED = ["e_norm_pre", "e_conv_b", "e_dt_bias", "e_a_log", "e_d_skip", "e_fgate_b", "e_ssd_norm", "e_norm_post"]
_SMALL = [n for n in _WEIGHTS if n not in _BIG]
_EVEN_SHARDED = ["e_w_in", "e_w_out", "e_conv_w"]
_ODD_SHARDED = ["o_w_in", "o_w_out", "o_norm_pre", "o_conv_w", "o_conv_b", "o_ln_g", "o_ln_b", "o_norm_post"]
_EARLY_GRADS = ["o_w_in", "o_w_out", "e_w_out"]
N_CHIPS = 4


def _join(gathered, rows):
    k, r, c = gathered.shape
    return gathered.reshape(k * r, c) if rows else gathered.transpose(1, 0, 2).reshape(r, k * c)


def _split(full, rows):
    r, c = full.shape
    return full.reshape(N_CHIPS, r // N_CHIPS, c) if rows else full.reshape(r, N_CHIPS, c // N_CHIPS).transpose(1, 0, 2)


def kernel(x, e_norm_pre, e_w_in, e_conv_w, e_conv_b, e_dt_bias, e_a_log, e_d_skip, e_fgate_b, e_ssd_norm, e_w_out, e_norm_post, o_norm_pre, o_w_in, o_conv_w, o_conv_b, o_ln_g, o_ln_b, o_w_out, o_norm_post, loss_target, m_e_norm_pre, m_e_w_in, m_e_conv_w, m_e_conv_b, m_e_dt_bias, m_e_a_log, m_e_d_skip, m_e_fgate_b, m_e_ssd_norm, m_e_w_out, m_e_norm_post, m_o_norm_pre, m_o_w_in, m_o_conv_w, m_o_conv_b, m_o_ln_g, m_o_ln_b, m_o_w_out, m_o_norm_post, v_e_norm_pre, v_e_w_in, v_e_conv_w, v_e_conv_b, v_e_dt_bias, v_e_a_log, v_e_d_skip, v_e_fgate_b, v_e_ssd_norm, v_e_w_out, v_e_norm_post, v_o_norm_pre, v_o_w_in, v_o_conv_w, v_o_conv_b, v_o_ln_g, v_o_ln_b, v_o_w_out, v_o_norm_post):
    wvals = (e_norm_pre, e_w_in, e_conv_w, e_conv_b, e_dt_bias, e_a_log, e_d_skip, e_fgate_b, e_ssd_norm, e_w_out,
             e_norm_post, o_norm_pre, o_w_in, o_conv_w, o_conv_b, o_ln_g, o_ln_b, o_w_out, o_norm_post)
    mvals = (m_e_norm_pre, m_e_w_in, m_e_conv_w, m_e_conv_b, m_e_dt_bias, m_e_a_log, m_e_d_skip, m_e_fgate_b,
             m_e_ssd_norm, m_e_w_out, m_e_norm_post, m_o_norm_pre, m_o_w_in, m_o_conv_w, m_o_conv_b, m_o_ln_g,
             m_o_ln_b, m_o_w_out, m_o_norm_post)
    vvals = (v_e_norm_pre, v_e_w_in, v_e_conv_w, v_e_conv_b, v_e_dt_bias, v_e_a_log, v_e_d_skip, v_e_fgate_b,
             v_e_ssd_norm, v_e_w_out, v_e_norm_post, v_o_norm_pre, v_o_w_in, v_o_conv_w, v_o_conv_b, v_o_ln_g,
             v_o_ln_b, v_o_w_out, v_o_norm_post)

    def mat(v):
        return v.reshape(v.shape[-2:]) if v.ndim == 3 else v

    w = {n: mat(v) for n, v in zip(_WEIGHTS, wvals)}
    m = {n: mat(v) for n, v in zip(_WEIGHTS, mvals)}
    v2 = {n: mat(v) for n, v in zip(_WEIGHTS, vvals)}
    me_xy = 2 * lax.axis_index("x") + lax.axis_index("y")

    def shard(n):
        return w[n].astype(BF16) if n in _BIG else w[n]

    gathered = _exchange([shard(n) for n in _EVEN_SHARDED], "xy", False, name="gather_weights")
    full = {n: w[n] for n in _REPLICATED}
    for n, gth in zip(_EVEN_SHARDED, gathered):
        full[n] = _join(gth, n in _ROW_SHARDED)
    gparts = {}

    class _StepComm(_NoComm):
        def odd_weights(self):
            return _Exchange([shard(n) for n in _ODD_SHARDED], "xy", False)

        def got_odd_weights(self, got, wdict):
            for n, gth in zip(_ODD_SHARDED, got):
                wdict[n] = _join(gth, n in _ROW_SHARDED)

        def early_grads(self, g):
            return _Exchange([_split(g[n], n in _ROW_SHARDED).astype(BF16) for n in _EARLY_GRADS], "xy", True)

        def got_early_grads(self, got):
            self.sums = [_sum_slots(p, out_dtype=BF16, name="sum_" + n) for n, p in zip(_EARLY_GRADS, got)]

        def early_sums(self):
            return _Exchange(self.sums, "c", False)

        def got_early_sums(self, got):
            gparts.update(zip(_EARLY_GRADS, got))

    loss, dx, g = _local_step(x[0], loss_target[0], full, _StepComm())
    loss = lax.psum(loss[0, 0], ("x", "y", "c"))

    flat = jnp.concatenate([_pad_lanes(g[n].reshape(1, -1)) for n in _SMALL], axis=1).reshape(-1, LANES)
    (scattered,), (all8,) = _exchanges([_Exchange([_split(g["e_w_in"], False).astype(BF16)], "xy", True),
                                        _Exchange([flat], "xyc", False)], name="scatter_grads")
    (gparts["e_w_in"],) = _exchange([_sum_slots(scattered, out_dtype=BF16, name="sum_e_w_in")], "c", False,
                                    name="pair_grads")
    total = _sum_slots(all8, name="sum_small").reshape(1, -1)
    at = 0
    for n in _SMALL:
        size = g[n].size
        gn = total[:, at:at + size].reshape(g[n].shape)
        at += size + (-size % LANES)
        if n in _SMALL_SHARDED:
            cols = gn.shape[1] // N_CHIPS
            gn = lax.dynamic_slice(gn, (0, me_xy * cols), (gn.shape[0], cols))
        gparts[n] = gn[None]

    grads, deltas, new_m, new_v = [], [], [], []
    for n, orig in zip(_WEIGHTS, wvals):
        gn, dn, mn, vn = _adamw(w[n], gparts[n], m[n], v2[n], name="adamw_" + n)
        for lst, val in zip((grads, deltas, new_m, new_v), (gn, dn, mn, vn)):
            lst.append(val.reshape(orig.shape))
    return (loss, dx[None], *grads, *deltas, *new_m, *new_v)
```

```python
import functools

import jax
import jax.numpy as jnp
from jax import lax
from jax.experimental import pallas as pl
from jax.experimental.pallas import tpu as pltpu

F32 = jnp.float32
BF16 = jnp.bfloat16
MESH = pl.DeviceIdType.MESH

D_MODEL = 1024
D_SSD = 1024
SSD_HEADS = 16
SSD_HEADDIM = 64
SSD_GROUPS = 4
SSD_HPG = 4
D_STATE = 128
SSD_CONV = 4
CHUNK = 128
D_FOX = 1024
FOX_HEADS = 16
FOX_HEADDIM = 64
D_CONV = 2048
CONV_WIDTH = 31
EPS = 1e-6
LANES = 128
VMEM_LIMIT = 56 * 1024 * 1024

ADAM_LR = 0.001
ADAM_B1 = 0.9
ADAM_B2 = 0.999
ADAM_EPS = 1e-08
ADAM_WD = 0.01
ADAM_STEP = 10


def _cparams(sem=None):
    return pltpu.CompilerParams(dimension_semantics=sem, vmem_limit_bytes=VMEM_LIMIT)


def _mm(a, b, *, ta=False, tb=False, add=None, out_dtype=F32, tm=1024, tn=None, tk=2048, name):
    m = a.shape[1] if ta else a.shape[0]
    k = a.shape[0] if ta else a.shape[1]
    n = b.shape[0] if tb else b.shape[1]
    if tn is None:
        tn = 1024
    tm, tn = min(tm, m), min(tn, n)
    tk = max(t for t in range(LANES, min(tk, k) + 1, LANES) if k % t == 0)
    assert m % tm == 0 and n % tn == 0 and k % tk == 0, (m, n, k, tm, tn, tk)
    nk = k // tk
    dims = (((0 if ta else 1,), (1 if tb else 0,)), ((), ()))

    def body(*refs):
        if add is None:
            a_ref, b_ref, o_ref = refs[:3]
            c_ref = None
        else:
            a_ref, b_ref, c_ref, o_ref = refs[:4]
        kk = pl.program_id(2)
        prod = lax.dot_general(a_ref[...].astype(BF16), b_ref[...].astype(BF16), dims, preferred_element_type=F32)
        if nk == 1:
            o_ref[...] = (prod if c_ref is None else prod + c_ref[...].astype(F32)).astype(o_ref.dtype)
            return
        acc_ref = refs[-1]

        @pl.when(kk == 0)
        def _():
            acc_ref[...] = prod if c_ref is None else prod + c_ref[...].astype(F32)

        @pl.when((kk > 0) & (kk < nk - 1))
        def _():
            acc_ref[...] += prod

        @pl.when(kk == nk - 1)
        def _():
            o_ref[...] = (acc_ref[...] + prod).astype(o_ref.dtype)

    a_spec = (pl.BlockSpec((tk, tm), lambda j, i, kk: (kk, i)) if ta
              else pl.BlockSpec((tm, tk), lambda j, i, kk: (i, kk)))
    b_spec = (pl.BlockSpec((tn, tk), lambda j, i, kk: (j, kk)) if tb
              else pl.BlockSpec((tk, tn), lambda j, i, kk: (kk, j)))
    o_spec = pl.BlockSpec((tm, tn), lambda j, i, kk: (i, j))
    in_specs, args = [a_spec, b_spec], [a, b]
    if add is not None:
        in_specs.append(o_spec)
        args.append(add)
    return pl.pallas_call(
        body, name=name, grid=(n // tn, m // tm, nk),
        in_specs=in_specs, out_specs=o_spec,
        out_shape=jax.ShapeDtypeStruct((m, n), out_dtype),
        scratch_shapes=[pltpu.VMEM((tm, tn), F32)] if nk > 1 else [],
        compiler_params=_cparams(("parallel", "parallel", "arbitrary")),
    )(*args)


def _col(arr, cb, width):
    return (arr, cb, width)


def _row_specs(ops, tm):
    return [pl.BlockSpec((tm, w), lambda i, cb=cb: (i, cb)) for (_, cb, w) in ops]


def _par_specs(ops):
    return [pl.BlockSpec((a.shape[0], w), lambda i, cb=cb: (0, cb)) for (a, cb, w) in ops]


def _rowwise_fwd(fn, rows, params, outs, *, tm, name):
    s = rows[0][0].shape[0]
    tm = min(tm, s)
    nr, npar = len(rows), len(params)

    def body(*refs):
        rv = [r[...].astype(F32) for r in refs[:nr]]
        pv = [p[...].astype(F32) for p in refs[nr:nr + npar]]
        res = fn(*rv, *pv)
        for o_ref, val in zip(refs[nr + npar:], res):
            o_ref[...] = val.astype(o_ref.dtype)

    return pl.pallas_call(
        body, name=name, grid=(s // tm,),
        in_specs=_row_specs(rows, tm) + _par_specs(params),
        out_specs=[pl.BlockSpec((tm, w), lambda i: (i, 0)) for (w, _) in outs],
        out_shape=[jax.ShapeDtypeStruct((s, w), dt) for (w, dt) in outs],
        compiler_params=_cparams(("parallel",)),
    )(*[r[0] for r in rows], *[p[0] for p in params])


def _rowwise_bwd(fn, rows, params, couts, row_grads, *, adds=None, tm, name):
    adds = adds or {}
    s = rows[0][0].shape[0]
    tm = min(tm, s)
    nr, npar, nc = len(rows), len(params), len(couts)
    add_keys = sorted(adds)
    want = [i for i, dt in enumerate(row_grads) if dt is not None]

    def body(*refs):
        i = pl.program_id(0)
        rv = [r[...].astype(F32) for r in refs[:nr]]
        pv = [p[...].astype(F32) for p in refs[nr:nr + npar]]
        cv = [c[...].astype(F32) for c in refs[nr + npar:nr + npar + nc]]
        av = {k: r[...].astype(F32) for k, r in zip(add_keys, refs[nr + npar + nc:nr + npar + nc + len(add_keys)])}
        orefs = refs[nr + npar + nc + len(add_keys):]
        _, vjp = jax.vjp(lambda rr, pp: tuple(fn(*rr, *pp)), rv, pv)
        drows, dpars = vjp(tuple(cv))
        for o_ref, ri in zip(orefs[:len(want)], want):
            g = drows[ri]
            if ri in av:
                g = g + av[ri]
            o_ref[...] = g.astype(o_ref.dtype)

        @pl.when(i == 0)
        def _():
            for o_ref in orefs[len(want):]:
                o_ref[...] = jnp.zeros_like(o_ref)

        for o_ref, g in zip(orefs[len(want):], dpars):
            o_ref[...] += g

    add_ops = [adds[k] for k in add_keys]
    out_specs = ([pl.BlockSpec((tm, rows[ri][2]), lambda i: (i, 0)) for ri in want]
                 + [pl.BlockSpec((p[0].shape[0], p[2]), lambda i: (0, 0)) for p in params])
    out_shape = ([jax.ShapeDtypeStruct((s, rows[ri][2]), row_grads[ri]) for ri in want]
                 + [jax.ShapeDtypeStruct((p[0].shape[0], p[2]), F32) for p in params])
    res = pl.pallas_call(
        body, name=name, grid=(s // tm,),
        in_specs=_row_specs(rows, tm) + _par_specs(params) + _row_specs(couts, tm) + _row_specs(add_ops, tm),
        out_specs=out_specs, out_shape=out_shape,
        compiler_params=_cparams(("arbitrary",)),
    )(*[r[0] for r in rows], *[p[0] for p in params], *[c[0] for c in couts], *[a[0] for a in add_ops])
    return res[:len(want)], res[len(want):]


def _silu(v):
    return v * jax.nn.sigmoid(v)


def _rms(v, g):
    return v * lax.rsqrt(jnp.mean(v * v, axis=-1, keepdims=True) + EPS) * g


SUBLANES = 8
CONV_ROWS = 256


def _halo(shifts):
    up = lambda v: -(-v // SUBLANES) * SUBLANES
    return up(max(0, -min(shifts))), up(max(0, max(shifts)))


def _fill_halo(xp_sc, x_ref, front, back):
    s = x_ref.shape[0]
    if front:
        xp_sc[0:front, :] = jnp.zeros((front, LANES), F32)
    if back:
        xp_sc[front + s:front + s + back, :] = jnp.zeros((back, LANES), F32)
    xp_sc[front:front + s, :] = x_ref[...].astype(F32)


def _shift_conv(x, w, b, shifts, *, name):
    s, c = x.shape
    tr = min(CONV_ROWS, s)
    nk = len(shifts)
    front, back = _halo(shifts)

    def body(*refs):
        if b is None:
            x_ref, w_ref, o_ref, xp_sc = refs
        else:
            x_ref, w_ref, b_ref, o_ref, xp_sc = refs
        _fill_halo(xp_sc, x_ref, front, back)

        def chunk(r, carry):
            base = pl.multiple_of(r * tr, tr)
            acc = jnp.zeros((tr, LANES), F32) if b is None else jnp.broadcast_to(b_ref[...], (tr, LANES))
            for kk in range(nk):
                acc = acc + xp_sc[pl.ds(base + front + shifts[kk], tr), :] * w_ref[kk:kk + 1, :]
            o_ref[pl.ds(base, tr), :] = acc
            return carry

        lax.fori_loop(0, s // tr, chunk, 0)

    strip = pl.BlockSpec((s, LANES), lambda cb: (0, cb))
    in_specs = [strip, pl.BlockSpec((nk, LANES), lambda cb: (0, cb))]
    args = [x, w]
    if b is not None:
        in_specs.append(pl.BlockSpec((1, LANES), lambda cb: (0, cb)))
        args.append(b)
    return pl.pallas_call(
        body, name=name, grid=(c // LANES,), in_specs=in_specs, out_specs=strip,
        out_shape=jax.ShapeDtypeStruct((s, c), F32),
        scratch_shapes=[pltpu.VMEM((front + s + back, LANES), F32)],
        compiler_params=_cparams(("parallel",)),
    )(*args)


def _conv_fwd(x, w, b, *, name):
    k = w.shape[0]
    return _shift_conv(x, w, b, [kk - (k - 1) for kk in range(k)], name=name)


def _conv_bwd_x(dy, w, *, name):
    k = w.shape[0]
    return _shift_conv(dy, w, None, [(k - 1) - kk for kk in range(k)], name=name)


def _conv_bwd_w(x, dy, k, *, name):
    s, c = x.shape
    tr = min(CONV_ROWS, s)
    shifts = [kk - (k - 1) for kk in range(k)]
    front, back = _halo(shifts)

    def fold(v):
        return jnp.sum(v.reshape(tr // SUBLANES, SUBLANES, LANES), axis=0)

    def body(x_ref, dy_ref, dw_ref, db_ref, xp_sc, dw_sc, db_sc):
        _fill_halo(xp_sc, x_ref, front, back)
        dw_sc[...] = jnp.zeros_like(dw_sc)
        db_sc[...] = jnp.zeros_like(db_sc)

        def chunk(r, carry):
            base = pl.multiple_of(r * tr, tr)
            dyv = dy_ref[pl.ds(base, tr), :]
            db_sc[...] += fold(dyv)
            for kk in range(k):
                dw_sc[kk] += fold(xp_sc[pl.ds(base + front + shifts[kk], tr), :] * dyv)
            return carry

        lax.fori_loop(0, s // tr, chunk, 0)
        db_ref[...] = jnp.sum(db_sc[...], axis=0, keepdims=True)
        for kk in range(k):
            dw_ref[kk:kk + 1, :] = jnp.sum(dw_sc[kk], axis=0, keepdims=True)

    strip = pl.BlockSpec((s, LANES), lambda cb: (0, cb))
    return pl.pallas_call(
        body, name=name, grid=(c // LANES,), in_specs=[strip, strip],
        out_specs=[pl.BlockSpec((k, LANES), lambda cb: (0, cb)), pl.BlockSpec((1, LANES), lambda cb: (0, cb))],
        out_shape=[jax.ShapeDtypeStruct((k, c), F32), jax.ShapeDtypeStruct((1, c), F32)],
        scratch_shapes=[pltpu.VMEM((front + s + back, LANES), F32), pltpu.VMEM((k, SUBLANES, LANES), F32),
                        pltpu.VMEM((SUBLANES, LANES), F32)],
        compiler_params=_cparams(("parallel",)),
    )(x, dy)


_DIMS = {"nn": ((1,), (0,)), "nt": ((1,), (1,)), "tn": ((0,), (0,))}


def _bd(a, b, mode):
    return lax.dot_general(a.astype(BF16), b.astype(BF16), (_DIMS[mode], ((), ())), preferred_element_type=F32)


@functools.partial(jax.custom_vjp, nondiff_argnums=(2,))
def _bdot(a, b, mode):
    return _bd(a, b, mode)


def _bdot_fwd(a, b, mode):
    return _bd(a, b, mode), (a, b)


def _bdot_bwd(mode, res, g):
    a, b = res
    if mode == "nn":
        return _bd(g, b, "nt"), _bd(a, g, "tn")
    if mode == "nt":
        return _bd(g, b, "nn"), _bd(g, a, "tn")
    return _bd(b, g, "nt"), _bd(a, g, "nn")


_bdot.defvjp(_bdot_fwd, _bdot_bwd)


def _split3(v):
    hi = v.astype(BF16)
    r1 = v - hi.astype(F32)
    mid = r1.astype(BF16)
    lo = (r1 - mid.astype(F32)).astype(BF16)
    return hi, mid, lo


def _mask_dot(mask01, v, mode):
    out = None
    for part in _split3(v):
        if mode == "vn":
            t = lax.dot_general(part, mask01, (_DIMS["nn"], ((), ())), preferred_element_type=F32)
        else:
            t = lax.dot_general(mask01, part, (_DIMS[mode], ((), ())), preferred_element_type=F32)
        out = t if out is None else out + t
    return out


def _lower_tri(n):
    r = lax.broadcasted_iota(jnp.int32, (n, n), 0)
    c = lax.broadcasted_iota(jnp.int32, (n, n), 1)
    return (r >= c).astype(BF16)


@jax.custom_vjp
def _tri_dot(w):
    return _mask_dot(_lower_tri(w.shape[0]), w, "nn")


def _tri_dot_fwd(w):
    return _tri_dot(w), None


def _tri_dot_bwd(_, g):
    return (_mask_dot(_lower_tri(g.shape[0]), g, "tn"),)


_tri_dot.defvjp(_tri_dot_fwd, _tri_dot_bwd)


def _cumsum_lanes(x, *, reverse, name):
    h, s = x.shape
    n = s // LANES

    def body(x_ref, o_ref):
        r = lax.broadcasted_iota(jnp.int32, (LANES, LANES), 0)
        c = lax.broadcasted_iota(jnp.int32, (LANES, LANES), 1)
        m01 = ((r >= c) if reverse else (r <= c)).astype(BF16)

        def step(t, carry):
            ci = (n - 1 - t) if reverse else t
            at = pl.ds(pl.multiple_of(ci * LANES, LANES), LANES)
            blk = x_ref[:, at]
            o_ref[:, at] = _mask_dot(m01, blk, "vn") + carry
            return carry + jnp.sum(blk, axis=1, keepdims=True)

        lax.fori_loop(0, n, step, jnp.zeros((h, 1), F32))

    return pl.pallas_call(body, name=name, out_shape=jax.ShapeDtypeStruct((h, s), F32),
                          compiler_params=_cparams())(x)


SSD_PAIRS = SSD_HPG // 2


def _ssd2_chunk(xs, dt, bm, cm, hin, a, dsk, head0):
    n = CHUNK
    row = lax.broadcasted_iota(jnp.int32, (n, n), 0)
    col = lax.broadcasted_iota(jnp.int32, (n, n), 1)
    lower = row >= col
    ustrict = (row > col).astype(F32)
    lane = lax.broadcasted_iota(jnp.int32, (1, LANES), 1)
    sub = lax.broadcasted_iota(jnp.int32, (n, 1), 0)
    e_first = (sub == 0).astype(F32)
    e_last = (sub == n - 1).astype(F32)
    lane0 = (lane == 0).astype(F32)
    half_l = [(lane < LANES // 2).astype(F32), (lane >= LANES // 2).astype(F32)]
    half_s = [(sub < LANES // 2).astype(F32), (sub >= LANES // 2).astype(F32)]
    cb = _bdot(cm, bm, "nt")
    da = dt * (-jnp.exp(a))
    ys, houts = [], []
    for pr in range(SSD_PAIRS):
        y = jnp.zeros((n, LANES), F32)
        xdte = jnp.zeros((n, LANES), F32)
        lane_gain = jnp.zeros((n, LANES), F32)
        row_gain = jnp.zeros((LANES, 1), F32)
        for hf in range(2):
            oh = (lane == head0 + 2 * pr + hf).astype(F32)
            dt_col = jnp.sum(dt * oh, axis=1, keepdims=True)
            da_col = jnp.sum(da * oh, axis=1, keepdims=True)
            dsk_h = jnp.sum(dsk * oh, axis=1, keepdims=True)
            seg = _tri_dot(da_col * ustrict)
            decay = jnp.where(lower, jnp.exp(seg), 0.0)
            cs_col = jnp.sum(seg * lane0, axis=1, keepdims=True) + jnp.sum(da_col * e_first, axis=0, keepdims=True)
            total = jnp.sum(cs_col * e_last, axis=0, keepdims=True)
            xh = xs[pr] * half_l[hf]
            xd = xh * dt_col
            y = y + _bdot(cb * decay, xd, "nn") + xh * dsk_h
            xdte = xdte + xd * jnp.exp(total - cs_col)
            lane_gain = lane_gain + jnp.exp(cs_col) * half_l[hf]
            row_gain = row_gain + jnp.exp(total) * half_s[hf]
        houts.append(hin[pr] * row_gain + _bdot(xdte, bm, "tn"))
        ys.append(y + _bdot(cm, hin[pr], "nt") * lane_gain)
    return ys, houts


SSD_STEP = 4


def _ssd2_steps(s):
    per = min(SSD_STEP, s // CHUNK)
    return per, s // (CHUNK * per)


def _ssd2_multi(xs, dt, bm, cm, hin, a, dsk, head0):
    ys = []
    for k in range(len(dt)):
        y, hin = _ssd2_chunk(xs[k], dt[k], bm[k], cm[k], hin, a, dsk, head0)
        ys.append(y)
    return ys, hin


def _ssd2_specs(per, nc, rev):
    cc = (lambda c: nc - 1 - c) if rev else (lambda c: c)
    rows = CHUNK * per
    act = pl.BlockSpec((rows, SSD_PAIRS * LANES), lambda c, g: (cc(c), g))
    row = pl.BlockSpec((rows, LANES), lambda c, g: (cc(c), 0))
    bmat = pl.BlockSpec((rows, LANES), lambda c, g: (cc(c), D_SSD // LANES + g))
    cmat = pl.BlockSpec((rows, LANES), lambda c, g: (cc(c), D_SSD // LANES + SSD_GROUPS + g))
    par = pl.BlockSpec((1, LANES), lambda c, g: (0, 0))
    hs = pl.BlockSpec((1, SSD_PAIRS, LANES, D_STATE), lambda c, g: (cc(c), g, 0, 0))
    return act, row, bmat, cmat, par, hs


def _chunk_rows(ref, k):
    return ref[k * CHUNK:(k + 1) * CHUNK, :]


def _pair_cols(ref, k):
    return [ref[k * CHUNK:(k + 1) * CHUNK, pr * LANES:(pr + 1) * LANES] for pr in range(SSD_PAIRS)]


def _ssd2_fwd(dt, xbc, a, dsk, *, name):
    s = xbc.shape[0]
    per, nc = _ssd2_steps(s)
    act, row, bmat, cmat, par, hs = _ssd2_specs(per, nc, False)

    def body(xs_ref, dt_ref, bm_ref, cm_ref, a_ref, dsk_ref, y_ref, hs_ref, h_sc):
        c, g = pl.program_id(0), pl.program_id(1)

        @pl.when(c == 0)
        def _():
            h_sc[pl.ds(g * SSD_PAIRS, SSD_PAIRS)] = jnp.zeros((SSD_PAIRS, LANES, D_STATE), F32)

        hin = [h_sc[g * SSD_PAIRS + pr] for pr in range(SSD_PAIRS)]
        ks = range(per)
        ys, houts = _ssd2_multi([_pair_cols(xs_ref, k) for k in ks], [_chunk_rows(dt_ref, k) for k in ks],
                                [_chunk_rows(bm_ref, k) for k in ks], [_chunk_rows(cm_ref, k) for k in ks],
                                hin, a_ref[...], dsk_ref[...], g * SSD_HPG)
        for pr in range(SSD_PAIRS):
            for k in ks:
                y_ref[k * CHUNK:(k + 1) * CHUNK, pr * LANES:(pr + 1) * LANES] = ys[k][pr]
            hs_ref[0, pr] = hin[pr]
            h_sc[g * SSD_PAIRS + pr] = houts[pr]

    return pl.pallas_call(
        body, name=name, grid=(nc, SSD_GROUPS),
        in_specs=[act, row, bmat, cmat, par, par], out_specs=[act, hs],
        out_shape=[jax.ShapeDtypeStruct((s, D_SSD), F32),
                   jax.ShapeDtypeStruct((nc, SSD_HEADS // 2, LANES, D_STATE), F32)],
        scratch_shapes=[pltpu.VMEM((SSD_HEADS // 2, LANES, D_STATE), F32)],
        compiler_params=_cparams(("arbitrary", "arbitrary")),
    )(xbc, dt, xbc, xbc, a, dsk)


def _ssd2_bwd(dt, xbc, a, dsk, hsave, dy, *, carry=None, name):
    s = xbc.shape[0]
    per, nc = _ssd2_steps(s)
    act, row, bmat, cmat, par, hs = _ssd2_specs(per, nc, True)
    gmat = pl.BlockSpec((CHUNK * per, LANES), lambda c, g: (nc - 1 - c, g))

    def body(*refs):
        ins, outs, (dh_sc,), comm = _carried(carry, refs, 8, 6, 1)
        xs_ref, dt_ref, bm_ref, cm_ref, a_ref, dsk_ref, hs_ref, dy_ref = ins
        dxs_ref, ddt_ref, dbm_ref, dcm_ref, da_ref, ddsk_ref = outs
        c, g = pl.program_id(0), pl.program_id(1)
        if carry is not None:
            @pl.when((c == 0) & (g == 0))
            def _():
                carry.start(*comm)

        @pl.when(c == 0)
        def _():
            dh_sc[pl.ds(g * SSD_PAIRS, SSD_PAIRS)] = jnp.zeros((SSD_PAIRS, LANES, D_STATE), F32)

        @pl.when((c == 0) & (g == 0))
        def _():
            da_ref[...] = jnp.zeros_like(da_ref)
            ddsk_ref[...] = jnp.zeros_like(ddsk_ref)

        @pl.when(g == 0)
        def _():
            ddt_ref[...] = jnp.zeros_like(ddt_ref)

        head0 = g * SSD_HPG
        ks = range(per)
        prim = ([_pair_cols(xs_ref, k) for k in ks], [_chunk_rows(dt_ref, k) for k in ks],
                [_chunk_rows(bm_ref, k) for k in ks], [_chunk_rows(cm_ref, k) for k in ks],
                [hs_ref[0, pr] for pr in range(SSD_PAIRS)], a_ref[...], dsk_ref[...])
        _, vjp = jax.vjp(lambda *p: _ssd2_multi(*p, head0), *prim)
        cot = ([_pair_cols(dy_ref, k) for k in ks], [dh_sc[g * SSD_PAIRS + pr] for pr in range(SSD_PAIRS)])
        dxs, ddt, dbm, dcm, dhin, da, ddsk = vjp(cot)
        for pr in range(SSD_PAIRS):
            for k in ks:
                dxs_ref[k * CHUNK:(k + 1) * CHUNK, pr * LANES:(pr + 1) * LANES] = dxs[k][pr]
            dh_sc[g * SSD_PAIRS + pr] = dhin[pr]
        for k in ks:
            ddt_ref[k * CHUNK:(k + 1) * CHUNK, :] += ddt[k]
            dbm_ref[k * CHUNK:(k + 1) * CHUNK, :] = dbm[k]
            dcm_ref[k * CHUNK:(k + 1) * CHUNK, :] = dcm[k]
        da_ref[...] += da
        ddsk_ref[...] += ddsk
        if carry is not None:
            @pl.when((c == nc - 1) & (g == SSD_GROUPS - 1))
            def _():
                carry.wait(*comm)

    extra = carry if carry is not None else _Exchange([], "c", False)
    return pl.pallas_call(
        body, name=name, grid=(nc, SSD_GROUPS),
        in_specs=[act, row, bmat, cmat, par, par, hs, act] + extra.in_specs,
        out_specs=[act, row, gmat, gmat, par, par] + extra.out_specs,
        out_shape=[jax.ShapeDtypeStruct((s, D_SSD), F32), jax.ShapeDtypeStruct((s, LANES), F32),
                   jax.ShapeDtypeStruct((s, SSD_GROUPS * D_STATE), F32),
                   jax.ShapeDtypeStruct((s, SSD_GROUPS * D_STATE), F32),
                   jax.ShapeDtypeStruct((1, LANES), F32), jax.ShapeDtypeStruct((1, LANES), F32)] + extra.out_shape,
        scratch_shapes=[pltpu.VMEM((SSD_HEADS // 2, LANES, D_STATE), F32)]
        + (carry.scratch if carry is not None else []),
        compiler_params=_cparams(("arbitrary", "arbitrary")),
    )(xbc, dt, xbc, xbc, a, dsk, hsave, dy, *extra.arrs)


FOX_BLOCK = 512
NEG = -1e30


def _fox_scores(q, k, cref, ck, strictly_below):
    t = q.shape[0]
    s = lax.dot_general(q, k, (_DIMS["nt"], ((), ())), preferred_element_type=F32) * (FOX_HEADDIM ** -0.5)
    s = s + (cref - ck)
    row = lax.broadcasted_iota(jnp.int32, (t, t), 0)
    col = lax.broadcasted_iota(jnp.int32, (t, t), 1)
    mask = (row >= col) | strictly_below
    return s, mask


def _fox_fwd(q, k, v, c, *, name):
    h, s, p = q.shape
    t = min(FOX_BLOCK, s)
    nb = s // t

    def body(q_ref, k_ref, v_ref, cq_ref, ck_ref, o_ref, lse_ref, m_sc, l_sc, acc_sc):
        i, j = pl.program_id(1), pl.program_id(2)

        @pl.when(j == 0)
        def _():
            m_sc[...] = jnp.full_like(m_sc, NEG)
            l_sc[...] = jnp.zeros_like(l_sc)
            acc_sc[...] = jnp.zeros_like(acc_sc)

        @pl.when(j <= i)
        def _():
            sc, mask = _fox_scores(q_ref[0], k_ref[0], cq_ref[0, 0:1, 0:1], ck_ref[0], j < i)
            sc = jnp.where(mask, sc, NEG)
            m_old = m_sc[...]
            m_new = jnp.maximum(m_old, jnp.max(sc, axis=1, keepdims=True))
            alpha = jnp.exp(m_old - m_new)
            pr = jnp.exp(sc - m_new)
            l_sc[...] = alpha * l_sc[...] + jnp.sum(pr, axis=1, keepdims=True)
            pr_hi = pr.astype(BF16)
            pr_lo = (pr - pr_hi.astype(F32)).astype(BF16)
            pv = (lax.dot_general(pr_hi, v_ref[0], (_DIMS["nn"], ((), ())), preferred_element_type=F32)
                  + lax.dot_general(pr_lo, v_ref[0], (_DIMS["nn"], ((), ())), preferred_element_type=F32))
            acc_sc[...] = alpha * acc_sc[...] + pv
            m_sc[...] = m_new

        @pl.when(j == i)
        def _():
            o_ref[0] = acc_sc[...] / l_sc[...]
            lse_ref[0] = jnp.broadcast_to(m_sc[...] + jnp.log(l_sc[...]), (t, LANES))

    qspec = pl.BlockSpec((1, t, p), lambda hh, i, j: (hh, i, 0))
    kspec = pl.BlockSpec((1, t, p), lambda hh, i, j: (hh, jnp.minimum(j, i), 0))
    return pl.pallas_call(
        body, name=name, grid=(h, nb, nb),
        in_specs=[qspec, kspec, kspec,
                  pl.BlockSpec((1, 1, t), lambda hh, i, j: (hh, 0, i)),
                  pl.BlockSpec((1, 1, t), lambda hh, i, j: (hh, 0, jnp.minimum(j, i)))],
        out_specs=[qspec, pl.BlockSpec((1, t, LANES), lambda hh, i, j: (hh, i, 0))],
        out_shape=[jax.ShapeDtypeStruct((h, s, p), F32), jax.ShapeDtypeStruct((h, s, LANES), F32)],
        scratch_shapes=[pltpu.VMEM((t, 1), F32), pltpu.VMEM((t, 1), F32), pltpu.VMEM((t, p), F32)],
        compiler_params=_cparams(("parallel", "arbitrary", "arbitrary")),
    )(q, k, v, c, c)


def _fox_bwd(q, k, v, c, o, lse, do, *, name):
    h, s, p = q.shape
    t = min(FOX_BLOCK, s)
    nb = s // t
    scale = FOX_HEADDIM ** -0.5

    def body(q_ref, k_ref, v_ref, cq_ref, ck_ref, o_ref, lse_ref, do_ref,
             dq_ref, dk_ref, dv_ref, dc_ref, dk_sc, dv_sc, dc_sc):
        j, i = pl.program_id(1), pl.program_id(2)

        @pl.when(i == 0)
        def _():
            dk_sc[...] = jnp.zeros_like(dk_sc)
            dv_sc[...] = jnp.zeros_like(dv_sc)
            dc_sc[...] = jnp.zeros_like(dc_sc)

        @pl.when(i >= j)
        def _():
            qv, kv, vv = q_ref[0], k_ref[0], v_ref[0]
            sc, mask = _fox_scores(qv, kv, cq_ref[0, 0:1, 0:1], ck_ref[0], i > j)
            pr = jnp.where(mask, jnp.exp(sc - lse_ref[0, :, 0:1]), 0.0)
            dov = do_ref[0]
            dob = dov.astype(BF16)
            prb = pr.astype(BF16)
            dv_sc[...] += lax.dot_general(prb, dob, (_DIMS["tn"], ((), ())), preferred_element_type=F32)
            dp = lax.dot_general(dob, vv, (_DIMS["nt"], ((), ())), preferred_element_type=F32)
            dcol = jnp.sum(dob.astype(F32) * o_ref[0], axis=1, keepdims=True)
            ds = pr * (dp - dcol)
            dc_sc[...] -= jnp.sum(ds, axis=0, keepdims=True)
            dsb = ds.astype(BF16)
            dqc = scale * lax.dot_general(dsb, kv, (_DIMS["nn"], ((), ())), preferred_element_type=F32)
            at = pl.ds(pl.multiple_of(i * t, t), t)

            @pl.when(j == 0)
            def _():
                dq_ref[0, at, :] = dqc

            @pl.when(j > 0)
            def _():
                dq_ref[0, at, :] += dqc

            dk_sc[...] += scale * lax.dot_general(dsb, qv, (_DIMS["tn"], ((), ())), preferred_element_type=F32)

        @pl.when(i == nb - 1)
        def _():
            dk_ref[0] = dk_sc[...]
            dv_ref[0] = dv_sc[...]
            dc_ref[0] = dc_sc[...]

    qspec = pl.BlockSpec((1, t, p), lambda hh, j, i: (hh, jnp.maximum(i, j), 0))
    kspec = pl.BlockSpec((1, t, p), lambda hh, j, i: (hh, j, 0))
    cq = pl.BlockSpec((1, 1, t), lambda hh, j, i: (hh, 0, jnp.maximum(i, j)))
    ck = pl.BlockSpec((1, 1, t), lambda hh, j, i: (hh, 0, j))
    return pl.pallas_call(
        body, name=name, grid=(h, nb, nb),
        in_specs=[qspec, kspec, kspec, cq, ck, qspec,
                  pl.BlockSpec((1, t, LANES), lambda hh, j, i: (hh, jnp.maximum(i, j), 0)), qspec],
        out_specs=[pl.BlockSpec((1, s, p), lambda hh, j, i: (hh, 0, 0)), kspec, kspec, ck],
        out_shape=[jax.ShapeDtypeStruct((h, s, p), F32), jax.ShapeDtypeStruct((h, s, p), F32),
                   jax.ShapeDtypeStruct((h, s, p), F32), jax.ShapeDtypeStruct((h, 1, s), F32)],
        scratch_shapes=[pltpu.VMEM((t, p), F32), pltpu.VMEM((t, p), F32), pltpu.VMEM((1, t), F32)],
        compiler_params=_cparams(("parallel", "arbitrary", "arbitrary")),
    )(q, k, v, c, c, o, lse, do)


AUX = 64


def _pack(main, cols):
    h, s, p = main.shape
    parts = [main.astype(BF16)]
    if cols:
        parts.append(jnp.stack(cols, axis=-1).astype(BF16))
    parts.append(jnp.zeros((h, s, LANES - p - len(cols)), BF16))
    return jnp.concatenate(parts, axis=-1)


def _terms(v):
    hi = lax.reduce_precision(v, 8, 7)
    mid = lax.reduce_precision(v - hi, 8, 7)
    lo = lax.reduce_precision(v - hi - mid, 8, 7)
    return [hi, mid, lo]


def _fox_pack_qkv(q, k, v):
    h, s, _ = q.shape
    one = jnp.ones((h, s), F32)
    return _pack(q * (FOX_HEADDIM ** -0.5), []), _pack(k, []), _pack(v, [one, one, one])


def _fox_bias(c_ref, qblock, kblock, t):
    lane = lax.broadcasted_iota(jnp.int32, (1, LANES), 1)
    cq = c_ref[0, :, pl.ds(pl.multiple_of(qblock * t, LANES), LANES)]
    cref = jnp.sum(jnp.where(lane == 0, cq, 0.0), axis=1, keepdims=True)
    return cref - c_ref[0, :, pl.ds(pl.multiple_of(kblock * t, LANES), t)]


def _fox_rowdot(do, o, *, tm=256, name):
    s, d = do.shape
    tm = min(tm, s)

    def body(do_ref, o_ref, d_ref):
        prod = do_ref[...].astype(BF16).astype(F32) * o_ref[...]
        r = lax.broadcasted_iota(jnp.int32, (d, LANES), 0)
        c = lax.broadcasted_iota(jnp.int32, (d, LANES), 1)
        mine = (r >= c * FOX_HEADDIM) & (r < (c + 1) * FOX_HEADDIM)
        d_ref[...] = _mask_dot(mine.astype(BF16), prod, "vn")

    row = pl.BlockSpec((tm, d), lambda i: (i, 0))
    return pl.pallas_call(body, name=name, grid=(s // tm,), in_specs=[row, row],
                          out_specs=pl.BlockSpec((tm, LANES), lambda i: (i, 0)),
                          out_shape=jax.ShapeDtypeStruct((s, LANES), F32),
                          compiler_params=_cparams(("parallel",)))(do, o)


def _causal(t):
    return lax.broadcasted_iota(jnp.int32, (t, t), 0) >= lax.broadcasted_iota(jnp.int32, (t, t), 1)


def _fox2_fwd(qp, kp, vp, c, *, name):
    h, s, _ = qp.shape
    t = min(FOX_BLOCK, s)
    nb = s // t
    nt = (((1,), (1,)), ((), ()))
    nn = (((1,), (0,)), ((), ()))

    def body(q_ref, k_ref, v_ref, c_ref, o_ref, lse_ref, m_sc, acc_sc):
        i = pl.program_id(1)
        m_sc[...] = jnp.full_like(m_sc, NEG)
        acc_sc[...] = jnp.zeros_like(acc_sc)
        qv = q_ref[0]

        def step(j, masked):
            at = pl.ds(pl.multiple_of(j * t, t), t)
            kv, vv = k_ref[0, at, :], v_ref[0, at, :]
            sc = lax.dot_general(qv, kv, nt, preferred_element_type=F32) + _fox_bias(c_ref, i, j, t)
            if masked:
                sc = jnp.where(_causal(t), sc, NEG)
            m_prev = m_sc[...]
            m_new = jnp.maximum(m_prev, jnp.max(sc, axis=1, keepdims=True))
            pr = jnp.exp(sc - jnp.tile(m_new, (1, t // LANES)))
            pr_hi = pr.astype(BF16)
            pr_lo = (pr - pr_hi.astype(F32)).astype(BF16)
            pv = (lax.dot_general(pr_hi, vv, nn, preferred_element_type=F32)
                  + lax.dot_general(pr_lo, vv, nn, preferred_element_type=F32))
            acc_sc[...] = jnp.exp(m_prev - m_new) * acc_sc[...] + pv
            m_sc[...] = m_new

        lax.fori_loop(0, i, lambda j, carry: (step(j, False), carry)[1], 0)
        step(i, True)
        acc = acc_sc[...]
        lane = lax.broadcasted_iota(jnp.int32, (1, LANES), 1)
        den = jnp.sum(jnp.where(lane == AUX, acc, 0.0), axis=1, keepdims=True)
        o_ref[0] = (acc / den)[:, :FOX_HEADDIM]
        lse_ref[0] = m_sc[...] + jnp.log(den)

    whole = pl.BlockSpec((1, s, LANES), lambda hh, i: (hh, 0, 0))
    return pl.pallas_call(
        body, name=name, grid=(h, nb),
        in_specs=[pl.BlockSpec((1, t, LANES), lambda hh, i: (hh, i, 0)), whole, whole,
                  pl.BlockSpec((1, 1, s), lambda hh, i: (hh, 0, 0))],
        out_specs=[pl.BlockSpec((1, t, FOX_HEADDIM), lambda hh, i: (hh, i, 0)),
                   pl.BlockSpec((1, t, LANES), lambda hh, i: (hh, i, 0))],
        out_shape=[jax.ShapeDtypeStruct((h, s, FOX_HEADDIM), F32), jax.ShapeDtypeStruct((h, s, LANES), F32)],
        scratch_shapes=[pltpu.VMEM((t, LANES), F32), pltpu.VMEM((t, LANES), F32)],
        compiler_params=_cparams(("parallel", "arbitrary")),
    )(qp, kp, vp, c)


def _fox2_bwd(qp, kp, vp, c, dop, lse, *, name):
    h, s, _ = qp.shape
    t = min(FOX_BLOCK, s)
    nb = s // t
    nt = (((1,), (1,)), ((), ()))
    nn = (((1,), (0,)), ((), ()))
    tn = (((0,), (0,)), ((), ()))

    def body(k_ref, v_ref, q_ref, c_ref, do_ref, lse_ref, dq_ref, dk_ref, dv_ref, dc_ref, dk_sc, dv_sc, dc_sc):
        j = pl.program_id(1)

        @pl.when(j == 0)
        def _():
            dq_ref[...] = jnp.zeros_like(dq_ref)

        dk_sc[...] = jnp.zeros_like(dk_sc)
        dv_sc[...] = jnp.zeros_like(dv_sc)
        dc_sc[...] = jnp.zeros_like(dc_sc)
        kv, vv = k_ref[0], v_ref[0]

        def step(i, masked):
            at = pl.ds(pl.multiple_of(i * t, t), t)
            qv, dov = q_ref[0, at, :], do_ref[0, at, :]
            sc = lax.dot_general(qv, kv, nt, preferred_element_type=F32) + _fox_bias(c_ref, i, j, t)
            pr = jnp.exp(sc - jnp.tile(lse_ref[0, at, :], (1, t // LANES)))
            if masked:
                pr = jnp.where(_causal(t), pr, 0.0)
            ds = pr * lax.dot_general(dov, vv, nt, preferred_element_type=F32)
            dc_sc[...] -= jnp.sum(ds, axis=0, keepdims=True)
            dsb = ds.astype(BF16)
            dv_sc[...] += lax.dot_general(pr.astype(BF16), dov, tn, preferred_element_type=F32)
            dk_sc[...] += lax.dot_general(dsb, qv, tn, preferred_element_type=F32)
            dq_ref[0, at, :] += lax.dot_general(dsb, kv, nn, preferred_element_type=F32)

        step(j, True)
        lax.fori_loop(j + 1, nb, lambda i, carry: (step(i, False), carry)[1], 0)
        dk_ref[0] = dk_sc[...]
        dv_ref[0] = dv_sc[...].astype(dv_ref.dtype)
        dc_ref[0] = dc_sc[...]

    whole = pl.BlockSpec((1, s, LANES), lambda hh, j: (hh, 0, 0))
    blk = pl.BlockSpec((1, t, LANES), lambda hh, j: (hh, j, 0))
    return pl.pallas_call(
        body, name=name, grid=(h, nb),
        in_specs=[blk, blk, whole, pl.BlockSpec((1, 1, s), lambda hh, j: (hh, 0, 0)), whole, whole],
        out_specs=[whole, blk, blk, pl.BlockSpec((1, 1, t), lambda hh, j: (hh, 0, j))],
        out_shape=[jax.ShapeDtypeStruct((h, s, LANES), F32), jax.ShapeDtypeStruct((h, s, LANES), F32),
                   jax.ShapeDtypeStruct((h, s, LANES), BF16), jax.ShapeDtypeStruct((h, 1, s), F32)],
        scratch_shapes=[pltpu.VMEM((t, LANES), F32), pltpu.VMEM((t, LANES), F32), pltpu.VMEM((1, t), F32)],
        compiler_params=_cparams(("parallel", "arbitrary")),
    )(kp, vp, qp, c, dop, lse)


PAIRS = FOX_HEADS // 2
HALF = LANES // 2


def _first_half():
    return lax.broadcasted_iota(jnp.int32, (1, LANES), 1) < HALF


def _pair_bias(c_ref, hh, qblock, kblock, t):
    lane = lax.broadcasted_iota(jnp.int32, (1, LANES), 1)
    cq = c_ref[hh, :, pl.ds(pl.multiple_of(qblock * t, LANES), LANES)]
    cref = jnp.sum(jnp.where(lane == 0, cq, 0.0), axis=1, keepdims=True)
    return cref - c_ref[hh, :, pl.ds(pl.multiple_of(kblock * t, LANES), t)]


def _fox_dopack(do, o, *, tm=256, name):
    s, d = do.shape
    tm = min(tm, s)

    def body(do_ref, o_ref, out_ref):
        dov = do_ref[...].astype(BF16)
        prod = dov.astype(F32) * o_ref[...]
        r = lax.broadcasted_iota(jnp.int32, (d, LANES), 0)
        c = lax.broadcasted_iota(jnp.int32, (d, LANES), 1)
        heads = ((r >= c * FOX_HEADDIM) & (r < (c + 1) * FOX_HEADDIM)).astype(BF16)
        negd = -_mask_dot(heads, prod, "vn")
        hr = lax.broadcasted_iota(jnp.int32, (LANES, 2 * d), 0)
        col = lax.broadcasted_iota(jnp.int32, (LANES, 2 * d), 1)
        base = (hr >> 1) * (2 * LANES) + jnp.where((hr & 1) == 0, HALF, LANES)
        terms = None
        for kk, part in enumerate(_split3(negd)):
            place = ((col == base + kk) & (hr < FOX_HEADS)).astype(BF16)
            tk = lax.dot_general(part, place, (_DIMS["nn"], ((), ())), preferred_element_type=F32)
            terms = tk if terms is None else terms + tk
        first = _first_half()
        zero = jnp.zeros((tm, LANES), BF16)
        pieces = []
        for hp in range(PAIRS):
            blk = dov[:, hp * LANES:(hp + 1) * LANES]
            pieces += [jnp.where(first, blk, zero), jnp.where(first, zero, blk)]
        out_ref[...] = (jnp.concatenate(pieces, axis=1).astype(F32) + terms).astype(BF16)

    row = pl.BlockSpec((tm, d), lambda i: (i, 0))
    return pl.pallas_call(body, name=name, grid=(s // tm,), in_specs=[row, row],
                          out_specs=pl.BlockSpec((tm, 2 * d), lambda i: (i, 0)),
                          out_shape=jax.ShapeDtypeStruct((s, 2 * d), BF16),
                          compiler_params=_cparams(("parallel",)))(do, o)


FOX_DEAD = 110.0
BOUND_SLACK = 1.001


def _fox_block_bounds(qkv, c):
    s = qkv.shape[0]
    t = min(FOX_BLOCK, s)
    k = qkv[:, D_FOX:2 * D_FOX].astype(F32).reshape(s // t, t, FOX_HEADS, FOX_HEADDIM)
    kn = jnp.sqrt(jnp.max(jnp.sum(k * k, axis=-1), axis=1)).T
    return kn, c[:, 0, ::t], c[:, 0, t - 1::t]


def _fox_bound(qn, kn_ref, cs_ref, ce_ref, h, qblock, kblock):
    return qn * (kn_ref[h, kblock] * BOUND_SLACK) + (cs_ref[h, qblock] - ce_ref[h, kblock])


def _run_live(live, work):
    @pl.when(live[0] & live[1])
    def _():
        work(0)
        work(1)

    @pl.when(live[0] & jnp.logical_not(live[1]))
    def _():
        work(0)

    @pl.when(jnp.logical_not(live[0]) & live[1])
    def _():
        work(1)


def _fox3_fwd(qkv, c, bounds, *, carry=None, name):
    assert qkv.shape[0] // min(FOX_BLOCK, qkv.shape[0]) < LANES - 2
    s = qkv.shape[0]
    t = min(FOX_BLOCK, s)
    nb = s // t
    nt = (((1,), (1,)), ((), ()))
    nn = (((1,), (0,)), ((), ()))
    scale = FOX_HEADDIM ** -0.5

    def body(*refs):
        ins, (o_ref, mrun_ref), (m_sc, acc_sc, mt_sc), comm = _carried(carry, refs, 7, 2, 3)
        q_ref, k_ref, v_ref, c_ref, kn_ref, cs_ref, ce_ref = ins
        hp, i = pl.program_id(0), pl.program_id(1)
        if carry is not None:
            @pl.when((hp == 0) & (i == 0))
            def _():
                carry.start(*comm)

        first = _first_half()
        lane = lax.broadcasted_iota(jnp.int32, (1, LANES), 1)
        m_sc[...] = jnp.full_like(m_sc, NEG)
        acc_sc[...] = jnp.zeros_like(acc_sc)
        mt_sc[...] = jnp.zeros_like(mt_sc)
        q2 = q_ref[...] * scale
        zero = jnp.zeros_like(q2)
        qs = [jnp.where(first, q2, zero), jnp.where(first, zero, q2)]
        qn = []
        for hh in range(2):
            qf = qs[hh].astype(F32)
            qn.append(jnp.broadcast_to(jnp.sqrt(jnp.sum(qf * qf, axis=1, keepdims=True)), (t, LANES)))

        def head_block(hh, j, k2, vxh, masked):
            sc = lax.dot_general(qs[hh], k2, nt, preferred_element_type=F32) + _pair_bias(c_ref, hh, i, j, t)
            if masked:
                sc = jnp.where(_causal(t), sc, NEG)
            m_prev = m_sc[hh]
            m_new = jnp.maximum(m_prev, jnp.max(sc, axis=1, keepdims=True))
            pr = jnp.exp(sc - jnp.tile(m_new, (1, t // LANES))).astype(BF16)
            pv = lax.dot_general(pr, vxh, nn, preferred_element_type=F32)
            acc_sc[hh] = jnp.exp(m_prev - m_new) * acc_sc[hh] + pv
            m_sc[hh] = m_new

        def step(j, masked):
            at = pl.ds(pl.multiple_of(j * t, t), t)
            k2, v2 = k_ref[at, :], v_ref[at, :]
            one = jnp.ones_like(v2)
            vx = [jnp.where(first, v2, one), jnp.where(first, one, v2)]
            if masked:
                for hh in range(2):
                    head_block(hh, j, k2, vx[hh], True)
            else:
                live = [jnp.max(_fox_bound(qn[hh], kn_ref, cs_ref, ce_ref, 2 * hp + hh, i, j) - m_sc[hh]) > -FOX_DEAD
                        for hh in range(2)]
                _run_live(live, lambda hh: head_block(hh, j, k2, vx[hh], False))
            for hh in range(2):
                mt_sc[hh] = jnp.where(lane == j, m_sc[hh], mt_sc[hh])

        step(i, True)
        lax.fori_loop(0, i, lambda n, carry: (step(i - 1 - n, False), carry)[1], 0)
        acc_a, acc_b = acc_sc[0], acc_sc[1]
        den_a = jnp.where(first, pltpu.roll(acc_a, HALF, 1), acc_a)
        den_b = jnp.where(first, acc_b, pltpu.roll(acc_b, HALF, 1))
        o_ref[...] = jnp.where(first, acc_a / den_a, acc_b / den_b)
        for hh, den in enumerate((den_a, den_b)):
            stats = jnp.where(lane == LANES - 1, m_sc[hh] + jnp.log(den), mt_sc[hh])
            mrun_ref[:, hh * LANES:(hh + 1) * LANES] = jnp.where(lane == LANES - 2, qn[hh], stats)
        if carry is not None:
            @pl.when((pl.program_id(0) == PAIRS - 1) & (i == nb - 1))
            def _():
                carry.wait(*comm)

    nq = D_FOX // LANES
    smem = pl.BlockSpec(memory_space=pltpu.SMEM)
    extra = carry if carry is not None else _Exchange([], "c", False)
    return pl.pallas_call(
        body, name=name, grid=(PAIRS, nb),
        in_specs=[pl.BlockSpec((t, LANES), lambda hp, i: (i, hp)),
                  pl.BlockSpec((s, LANES), lambda hp, i: (0, nq + hp)),
                  pl.BlockSpec((s, LANES), lambda hp, i: (0, 2 * nq + hp)),
                  pl.BlockSpec((2, 1, s), lambda hp, i: (hp, 0, 0))] + [smem] * 3 + extra.in_specs,
        out_specs=[pl.BlockSpec((t, LANES), lambda hp, i: (i, hp)),
                   pl.BlockSpec((t, 2 * LANES), lambda hp, i: (i, hp))] + extra.out_specs,
        out_shape=[jax.ShapeDtypeStruct((s, D_FOX), F32), jax.ShapeDtypeStruct((s, 2 * D_FOX), F32)] + extra.out_shape,
        scratch_shapes=[pltpu.VMEM((2, t, LANES), F32)] * 3 + (carry.scratch if carry is not None else []),
        compiler_params=_cparams(("arbitrary", "arbitrary")),
    )(qkv, qkv, qkv, c, *bounds, *extra.arrs)


def _fox3_bwd(qkv, c, bounds, dox, mrun, *, carry=None, name):
    s = qkv.shape[0]
    t = min(FOX_BLOCK, s)
    nb = s // t
    nt = (((1,), (1,)), ((), ()))
    nn = (((1,), (0,)), ((), ()))
    tn = (((0,), (0,)), ((), ()))
    scale = FOX_HEADDIM ** -0.5

    def body(*refs):
        ins, outs, scratch, comm = _carried(carry, refs, 9, 4, 3)
        k_ref, v_ref, q_ref, c_ref, do_ref, mrun_ref, kn_ref, cs_ref, ce_ref = ins
        dq_ref, dk_ref, dv_ref, dc_ref = outs
        dk_sc, dv_sc, dc_sc = scratch
        hp, j = pl.program_id(0), pl.program_id(1)
        if carry is not None:
            @pl.when((hp == 0) & (j == 0))
            def _():
                carry.start(*comm)

        first = _first_half()
        halves = [first, jnp.logical_not(first)]
        lane = lax.broadcasted_iota(jnp.int32, (1, LANES), 1)

        @pl.when(j == 0)
        def _():
            dq_ref[...] = jnp.zeros_like(dq_ref)

        dk_sc[...] = jnp.zeros_like(dk_sc)
        dv_sc[...] = jnp.zeros_like(dv_sc)
        dc_sc[...] = jnp.zeros_like(dc_sc)
        k2, v2 = k_ref[...], v_ref[...]
        one = jnp.ones_like(v2)
        vx = [jnp.where(first, v2, one), jnp.where(first, one, v2)]

        def pick(stats, which):
            return jnp.sum(jnp.where(lane == which, stats, 0.0), axis=1, keepdims=True)

        def head_block(hh, i, at, qsh, stats, masked):
            mine = slice(hh * LANES, (hh + 1) * LANES)
            dov = do_ref[at, mine]
            sc = lax.dot_general(qsh, k2, nt, preferred_element_type=F32) + _pair_bias(c_ref, hh, i, j, t)
            mj = pick(stats, j)
            gain = jnp.broadcast_to(jnp.exp(mj - pick(stats, LANES - 1)), (t, LANES))
            mj = jnp.broadcast_to(mj, (t, LANES))
            pb = jnp.exp(sc - jnp.tile(mj, (1, t // LANES))).astype(BF16)
            if masked:
                pb = jnp.where(_causal(t), pb, jnp.zeros_like(pb))
            pr = pb.astype(F32) * jnp.tile(gain, (1, t // LANES))
            ds = pr * lax.dot_general(dov, vx[hh], nt, preferred_element_type=F32)
            dc_sc[hh] -= jnp.sum(ds, axis=0, keepdims=True)
            dsb = ds.astype(BF16)
            dvh = lax.dot_general(pr.astype(BF16), dov, tn, preferred_element_type=F32)
            dv_sc[...] += jnp.where(halves[hh], dvh, 0.0)
            dk_sc[...] += lax.dot_general(dsb, qsh, tn, preferred_element_type=F32)
            dqh = lax.dot_general(dsb, k2, nn, preferred_element_type=F32)
            dq_ref[at, :] += jnp.where(halves[hh], dqh, 0.0)

        def step(i, masked):
            at = pl.ds(pl.multiple_of(i * t, t), t)
            q2 = q_ref[at, :] * scale
            zero = jnp.zeros_like(q2)
            qs = [jnp.where(first, q2, zero), jnp.where(first, zero, q2)]
            stats = [mrun_ref[at, hh * LANES:(hh + 1) * LANES] for hh in range(2)]
            if masked:
                for hh in range(2):
                    head_block(hh, i, at, qs[hh], stats[hh], True)
            else:
                live = [jnp.max(_fox_bound(pick(stats[hh], LANES - 2), kn_ref, cs_ref, ce_ref, 2 * hp + hh, i, j)
                                - pick(stats[hh], j + 1)) > -FOX_DEAD for hh in range(2)]
                _run_live(live, lambda hh: head_block(hh, i, at, qs[hh], stats[hh], False))

        step(j, True)
        lax.fori_loop(j + 1, nb, lambda i, carry: (step(i, False), carry)[1], 0)
        dk_ref[...] = dk_sc[...]
        dv_ref[...] = dv_sc[...].astype(dv_ref.dtype)
        dc_ref[...] = dc_sc[...]
        if carry is not None:
            @pl.when((pl.program_id(0) == PAIRS - 1) & (j == nb - 1))
            def _():
                carry.wait(*comm)

    nq = D_FOX // LANES
    blk = pl.BlockSpec((t, LANES), lambda hp, j: (j, hp))
    smem = pl.BlockSpec(memory_space=pltpu.SMEM)
    extra = carry if carry is not None else _Exchange([], "c", False)
    return pl.pallas_call(
        body, name=name, grid=(PAIRS, nb),
        in_specs=[pl.BlockSpec((t, LANES), lambda hp, j: (j, nq + hp)),
                  pl.BlockSpec((t, LANES), lambda hp, j: (j, 2 * nq + hp)),
                  pl.BlockSpec((s, LANES), lambda hp, j: (0, hp)),
                  pl.BlockSpec((2, 1, s), lambda hp, j: (hp, 0, 0)),
                  pl.BlockSpec((s, 2 * LANES), lambda hp, j: (0, hp)),
                  pl.BlockSpec((s, 2 * LANES), lambda hp, j: (0, hp))] + [smem] * 3 + extra.in_specs,
        out_specs=[pl.BlockSpec((s, LANES), lambda hp, j: (0, hp)), blk, blk,
                   pl.BlockSpec((2, 1, t), lambda hp, j: (hp, 0, j))] + extra.out_specs,
        out_shape=[jax.ShapeDtypeStruct((s, D_FOX), F32), jax.ShapeDtypeStruct((s, D_FOX), F32),
                   jax.ShapeDtypeStruct((s, D_FOX), BF16), jax.ShapeDtypeStruct((FOX_HEADS, 1, s), F32)]
        + extra.out_shape,
        scratch_shapes=[pltpu.VMEM((t, LANES), F32), pltpu.VMEM((t, LANES), F32), pltpu.VMEM((2, 1, t), F32)]
        + (carry.scratch if carry is not None else []),
        compiler_params=_cparams(("arbitrary", "arbitrary")),
    )(qkv, qkv, qkv, c, dox, mrun, *bounds, *extra.arrs)


def _final(x1, out1, g, tgt, *, tm=256, name):
    s, d = x1.shape
    tm = min(tm, s)

    def body(x_ref, o_ref, g_ref, t_ref, dx_ref, do_ref, dg_ref, loss_ref):
        i = pl.program_id(0)

        @pl.when(i == 0)
        def _():
            dg_ref[...] = jnp.zeros_like(dg_ref)
            loss_ref[...] = jnp.zeros_like(loss_ref)

        tv = t_ref[...]

        def lossf(xv, ov, gv):
            err = jnp.square(xv + _rms(ov, gv) - tv)
            return 0.5 * jnp.sum(jnp.mean(err, axis=-1, keepdims=True), axis=0, keepdims=True)

        val, vjp = jax.vjp(lossf, x_ref[...], o_ref[...], g_ref[...])
        dx, do, dg = vjp(jnp.ones((1, 1), F32))
        dx_ref[...] = dx
        do_ref[...] = do.astype(do_ref.dtype)
        dg_ref[...] += dg
        loss_ref[...] += val

    row = pl.BlockSpec((tm, d), lambda i: (i, 0))
    par = pl.BlockSpec((1, d), lambda i: (0, 0))
    return pl.pallas_call(
        body, name=name, grid=(s // tm,), in_specs=[row, row, par, row],
        out_specs=[row, row, par, pl.BlockSpec((1, 1), lambda i: (0, 0))],
        out_shape=[jax.ShapeDtypeStruct((s, d), F32), jax.ShapeDtypeStruct((s, d), BF16),
                   jax.ShapeDtypeStruct((1, d), F32), jax.ShapeDtypeStruct((1, 1), F32)],
        compiler_params=_cparams(("arbitrary",)),
    )(x1, out1, g, tgt)


def _row_tile(r):
    return LANES if r % LANES == 0 else r


def _sum_slots(parts, *, out_dtype=F32, name):
    p, r, c = parts.shape
    tr = _row_tile(r)

    def body(p_ref, o_ref):
        acc = p_ref[0].astype(F32)
        for k in range(1, p):
            acc = acc + p_ref[k].astype(F32)
        o_ref[...] = acc.astype(o_ref.dtype)

    return pl.pallas_call(
        body, name=name, grid=(r // tr,),
        in_specs=[pl.BlockSpec((p, tr, c), lambda i: (0, i, 0))],
        out_specs=pl.BlockSpec((tr, c), lambda i: (i, 0)),
        out_shape=jax.ShapeDtypeStruct((r, c), out_dtype),
        compiler_params=_cparams(("parallel",)),
    )(parts)


def _adamw(w, gparts, m, v, *, name):
    r, c = w.shape
    p = gparts.shape[0]
    tr = _row_tile(r)

    def body(w_ref, g_ref, m_ref, v_ref, go_ref, d_ref, mo_ref, vo_ref):
        g = g_ref[0].astype(F32)
        for k in range(1, p):
            g = g + g_ref[k].astype(F32)
        mn = ADAM_B1 * m_ref[...] + (1.0 - ADAM_B1) * g
        vn = ADAM_B2 * v_ref[...] + (1.0 - ADAM_B2) * jnp.square(g)
        m_hat = mn / (1.0 - ADAM_B1 ** ADAM_STEP)
        v_hat = vn / (1.0 - ADAM_B2 ** ADAM_STEP)
        go_ref[...] = g
        d_ref[...] = -ADAM_LR * (m_hat / (jnp.sqrt(v_hat) + ADAM_EPS) + ADAM_WD * w_ref[...])
        mo_ref[...] = mn
        vo_ref[...] = vn

    spec = pl.BlockSpec((tr, c), lambda i: (i, 0))
    return pl.pallas_call(
        body, name=name, grid=(r // tr,),
        in_specs=[spec, pl.BlockSpec((p, tr, c), lambda i: (0, i, 0)), spec, spec],
        out_specs=[spec] * 4, out_shape=[jax.ShapeDtypeStruct((r, c), F32)] * 4,
        compiler_params=_cparams(("parallel",)),
    )(w, gparts, m, v)


_FLIPS = {
    "xy": [(1, 0, 0), (0, 1, 0), (1, 1, 0)],
    "c": [(0, 0, 1)],
    "xyc": [(fx, fy, fc) for fx in (0, 1) for fy in (0, 1) for fc in (0, 1) if (fx, fy, fc) != (0, 0, 0)],
}


def _slot(mode, px, py, pc):
    return {"xy": 2 * px + py, "c": pc, "xyc": 4 * px + 2 * py + pc}[mode]


class _Exchange:
    def __init__(self, arrs, mode, scatter):
        self.arrs, self.mode, self.scatter = list(arrs), mode, scatter
        self.n = len(self.arrs)
        self.flips = _FLIPS[mode]
        nf = len(self.flips)
        anyspec = pl.BlockSpec(memory_space=pl.ANY)
        self.in_specs = [anyspec] * self.n
        self.out_specs = [anyspec] * self.n
        self.out_shape = [jax.ShapeDtypeStruct((nf + 1,) + (a.shape[1:] if scatter else a.shape), a.dtype)
                          for a in self.arrs]
        self.scratch = [pltpu.SemaphoreType.DMA((self.n * nf,)), pltpu.SemaphoreType.DMA((self.n * nf,)),
                        pltpu.SemaphoreType.DMA((self.n,))]

    def _copies(self, ins, outs, sems, arrivals=True):
        send, recv, loc = sems
        nf = len(self.flips)
        x, y, c = lax.axis_index("x"), lax.axis_index("y"), lax.axis_index("c")
        me = _slot(self.mode, x, y, c)
        peers = [(x ^ fx, y ^ fy, c ^ fc) for (fx, fy, fc) in self.flips]

        def src(a, slot):
            return ins[a].at[slot] if self.scatter else ins[a]

        def copy(a, j, dst_slot):
            return pltpu.make_async_remote_copy(
                src_ref=src(a, _slot(self.mode, *peers[j])), dst_ref=outs[a].at[dst_slot],
                send_sem=send.at[a * nf + j], recv_sem=recv.at[a * nf + j], device_id=peers[j], device_id_type=MESH)

        pairs = [(a, j) for a in range(self.n) for j in range(nf)]
        local = [pltpu.make_async_copy(src(a, me), outs[a].at[me], loc.at[a]) for a in range(self.n)]
        sends = [copy(a, j, me) for a, j in pairs]
        recvs = [copy(a, j, _slot(self.mode, *peers[j])) for a, j in pairs] if arrivals else []
        return local, sends, recvs

    def start(self, ins, outs, sems):
        local, sends, _ = self._copies(ins, outs, sems, arrivals=False)
        for cp in local + sends:
            cp.start()

    def wait(self, ins, outs, sems):
        local, sends, recvs = self._copies(ins, outs, sems)
        for cp in recvs:
            cp.wait_recv()
        for cp in sends:
            cp.wait_send()
        for cp in local:
            cp.wait()


def _carried(carry, refs, n_in, n_out, n_scratch):
    k = carry.n if carry is not None else 0
    ins, refs = refs[:n_in], refs[n_in:]
    cin, refs = refs[:k], refs[k:]
    outs, refs = refs[:n_out], refs[n_out:]
    cout, refs = refs[:k], refs[k:]
    scratch, sems = refs[:n_scratch], refs[n_scratch:]
    return ins, outs, scratch, (cin, cout, sems)


def _exchanges(exs, *, name):
    counts = [ex.n for ex in exs]
    total = sum(counts)

    def body(*refs):
        ins, outs, sems = refs[:total], refs[total:2 * total], refs[2 * total:]
        comms, at = [], 0
        for k, ex in enumerate(exs):
            comms.append((ins[at:at + ex.n], outs[at:at + ex.n], sems[3 * k:3 * k + 3]))
            at += ex.n
        for ex, comm in zip(exs, comms):
            ex.start(*comm)
        for ex, comm in zip(exs, comms):
            ex.wait(*comm)

    res = pl.pallas_call(
        body, name=name, in_specs=[sp for ex in exs for sp in ex.in_specs],
        out_specs=[sp for ex in exs for sp in ex.out_specs], out_shape=[sh for ex in exs for sh in ex.out_shape],
        scratch_shapes=[sc for ex in exs for sc in ex.scratch])(*[a for ex in exs for a in ex.arrs])
    out, at = [], 0
    for n in counts:
        out.append(list(res[at:at + n]))
        at += n
    return out


def _exchange(arrs, mode, scatter, *, name):
    return _exchanges([_Exchange(arrs, mode, scatter)], name=name)[0]


def _softplus(v):
    return jnp.maximum(v, 0.0) + jnp.log1p(jnp.exp(-jnp.abs(v)))


def _pad_lanes(v):
    r, n = v.shape
    return jnp.pad(v, ((0, 0), (0, -n % LANES)))


def _to_heads(v):
    s = v.shape[0]
    return v.reshape(s, -1, 64).transpose(1, 0, 2)


def _from_heads(v):
    h, s, p = v.shape
    return v.transpose(1, 0, 2).reshape(s, h * p)


def _fn_rms(v, g):
    return (_rms(v, g),)


def _fn_post(xv, ov, g):
    return (xv + _rms(ov, g),)


def _fn_act(xbc, dtp, fp, dtb, fb):
    return _silu(xbc), _softplus(dtp + dtb), -_softplus(-(fp + fb))


def _fn_mix(y, zs, o, zf, g):
    yg = y * _silu(zs)
    sq = yg * yg
    lane = lax.broadcasted_iota(jnp.int32, (1, D_SSD), 1)
    width = D_SSD // SSD_GROUPS
    rstd = jnp.zeros_like(yg)
    for gi in range(SSD_GROUPS):
        msk = ((lane >= gi * width) & (lane < (gi + 1) * width)).astype(F32)
        ms = jnp.sum(sq * msk, axis=1, keepdims=True) / width
        rstd = rstd + lax.rsqrt(ms + EPS) * msk
    return (jnp.concatenate([yg * rstd * g, o * _silu(zf)], axis=1),)


def _fn_glu(val, gate):
    return (val * jax.nn.sigmoid(gate),)


def _fn_ln(hc, z, g, b):
    mu = jnp.mean(hc, axis=-1, keepdims=True)
    xc = hc - mu
    yn = xc * lax.rsqrt(jnp.mean(xc * xc, axis=-1, keepdims=True) + EPS) * g + b
    return (_silu(yn) * _silu(z),)


class _NoComm:
    def odd_weights(self):
        return None

    def got_odd_weights(self, got, w):
        pass

    def early_grads(self, g):
        return None

    def got_early_grads(self, got):
        pass

    def early_sums(self):
        return None

    def got_early_sums(self, got):
        pass


def _local_step(x, tgt, w, comm=None):
    comm = comm or _NoComm()
    s = x.shape[0]
    d = D_MODEL
    tm = 256
    bf = lambda v: v.astype(BF16)
    c1 = lambda arr: _col(arr, 0, arr.shape[1])
    g = {}

    ew = w["e_w_in"]
    w_z, w_xbc = bf(ew[:, 0:2048]), bf(ew[:, 2048:4096])
    w_dt = bf(_pad_lanes(ew[:, 4096:4112]))
    w_qkv = bf(ew[:, 4112:7184])
    w_f = bf(_pad_lanes(ew[:, 7184:7200]))
    w_eo = bf(w["e_w_out"])
    dtb, fgb = _pad_lanes(w["e_dt_bias"]), _pad_lanes(w["e_fgate_b"])
    alog, dsk = _pad_lanes(w["e_a_log"]), _pad_lanes(w["e_d_skip"])

    (u0,) = _rowwise_fwd(_fn_rms, [c1(x)], [c1(w["e_norm_pre"])], [(d, BF16)], tm=tm, name="e_pre")
    z = _mm(u0, w_z, name="e_in_z")
    xbc_raw = _mm(u0, w_xbc, out_dtype=BF16, name="e_in_xbc")
    qkv = _mm(u0, w_qkv, out_dtype=BF16, name="e_in_qkv")
    dtp = _mm(u0, w_dt, name="e_in_dt")
    fp = _mm(u0, w_f, name="e_in_f")
    xbc_pre = _conv_fwd(xbc_raw, w["e_conv_w"], w["e_conv_b"], name="e_conv")
    act_rows = [c1(xbc_pre), c1(dtp), c1(fp)]
    act_pars = [c1(dtb), c1(fgb)]
    xbc, dt, lf = _rowwise_fwd(_fn_act, act_rows, act_pars, [(2048, F32), (LANES, F32), (LANES, F32)],
                               tm=tm, name="e_act")
    y, hsave = _ssd2_fwd(dt, xbc, alog, dsk, name="e_ssd")
    csum = _cumsum_lanes(lf[:, :FOX_HEADS].T, reverse=False, name="e_cumsum").reshape(FOX_HEADS, 1, s)
    bounds = _fox_block_bounds(qkv, csum)
    o, mrun, *got = _fox3_fwd(qkv, csum, bounds, carry=comm.odd_weights(), name="e_fox")
    comm.got_odd_weights(got, w)
    w_oi, w_oo = bf(w["o_w_in"]), bf(w["o_w_out"])
    mix_rows = [c1(y), _col(z, 0, D_SSD), c1(o), _col(z, 1, D_FOX)]
    mix_pars = [c1(w["e_ssd_norm"])]
    (hmix,) = _rowwise_fwd(_fn_mix, mix_rows, mix_pars, [(2048, BF16)], tm=tm, name="e_mix")
    out0 = _mm(hmix, w_eo, name="e_out")
    post_rows = [c1(x), c1(out0)]
    (x1,) = _rowwise_fwd(_fn_post, post_rows, [c1(w["e_norm_post"])], [(d, F32)], tm=tm, name="e_post")

    (u1,) = _rowwise_fwd(_fn_rms, [c1(x1)], [c1(w["o_norm_pre"])], [(d, BF16)], tm=tm, name="o_pre")
    p1 = _mm(u1, w_oi, out_dtype=BF16, name="o_in")
    glu_rows = [_col(p1, 0, D_CONV), _col(p1, 1, D_CONV)]
    (hg,) = _rowwise_fwd(_fn_glu, glu_rows, [], [(D_CONV, BF16)], tm=tm, name="o_glu")
    hc = _conv_fwd(hg, w["o_conv_w"], w["o_conv_b"], name="o_conv")
    ln_rows = [c1(hc), _col(p1, 2, D_CONV)]
    ln_pars = [c1(w["o_ln_g"]), c1(w["o_ln_b"])]
    (h2,) = _rowwise_fwd(_fn_ln, ln_rows, ln_pars, [(D_CONV, BF16)], tm=tm, name="o_ln")
    out1 = _mm(h2, w_oo, name="o_out")

    dx2, dout1, g["o_norm_post"], loss = _final(x1, out1, w["o_norm_post"], tgt, name="loss_head")
    dh2 = _mm(dout1, w_oo, tb=True, name="o_out_dx")
    g["o_w_out"] = _mm(h2, dout1, ta=True, name="o_out_dw")
    (dhc, dz1), (g["o_ln_g"], g["o_ln_b"]) = _rowwise_bwd(_fn_ln, ln_rows, ln_pars, [c1(dh2)], [F32, BF16],
                                                         tm=tm, name="o_ln_bwd")
    dhg = _conv_bwd_x(dhc, w["o_conv_w"], name="o_conv_dx")
    g["o_conv_w"], g["o_conv_b"] = _conv_bwd_w(hg, dhc, CONV_WIDTH, name="o_conv_dw")
    (dval, dgate), _ = _rowwise_bwd(_fn_glu, glu_rows, [], [c1(dhg)], [BF16, BF16], tm=tm, name="o_glu_bwd")
    du1 = _mm(dval, w_oi[:, 0:2048], tb=True, name="o_in_dx0")
    du1 = _mm(dgate, w_oi[:, 2048:4096], tb=True, add=du1, name="o_in_dx1")
    du1 = _mm(dz1, w_oi[:, 4096:6144], tb=True, add=du1, name="o_in_dx2")
    g["o_w_in"] = jnp.concatenate([_mm(u1, dval, ta=True, name="o_in_dw0"), _mm(u1, dgate, ta=True, name="o_in_dw1"),
                                   _mm(u1, dz1, ta=True, name="o_in_dw2")], axis=1)
    (dx1,), (g["o_norm_pre"],) = _rowwise_bwd(_fn_rms, [c1(x1)], [c1(w["o_norm_pre"])], [c1(du1)], [F32],
                                              adds={0: c1(dx2)}, tm=tm, name="o_pre_bwd")

    (dout0,), (g["e_norm_post"],) = _rowwise_bwd(_fn_post, post_rows, [c1(w["e_norm_post"])], [c1(dx1)],
                                                 [None, BF16], tm=tm, name="e_post_bwd")
    dhmix = _mm(dout0, w_eo, tb=True, name="e_out_dx")
    g["e_w_out"] = _mm(hmix, dout0, ta=True, name="e_out_dw")
    (dy, dzs, do, dzf), (g["e_ssd_norm"],) = _rowwise_bwd(_fn_mix, mix_rows, mix_pars, [c1(dhmix)],
                                                        [F32, BF16, F32, BF16], tm=tm, name="e_mix_bwd")
    dox = _fox_dopack(do, o, name="e_fox_dopack")
    dq8, dk, dv, dcs, *got = _fox3_bwd(qkv, csum, bounds, dox, mrun, carry=comm.early_grads(g), name="e_fox_bwd")
    comm.got_early_grads(got)
    dlf = _pad_lanes(_cumsum_lanes(dcs.reshape(FOX_HEADS, s), reverse=True, name="e_cumsum_bwd").T)
    dxs, ddt, dbm, dcm, dalog, ddsk, *got = _ssd2_bwd(dt, xbc, alog, dsk, hsave, dy, carry=comm.early_sums(),
                                                       name="e_ssd_bwd")
    comm.got_early_sums(got)
    dxbc = jnp.concatenate([dxs, dbm, dcm], axis=1)
    (dxbc_pre, ddtp, dfp), (ddtb, dfgb) = _rowwise_bwd(_fn_act, act_rows, act_pars, [c1(dxbc), c1(ddt), c1(dlf)],
                                                      [F32, BF16, BF16], tm=tm, name="e_act_bwd")
    dxbc_raw = bf(_conv_bwd_x(dxbc_pre, w["e_conv_w"], name="e_conv_dx"))
    g["e_conv_w"], g["e_conv_b"] = _conv_bwd_w(xbc_raw, dxbc_pre, SSD_CONV, name="e_conv_dw")
    du0 = _mm(dzs, w_z[:, :D_SSD], tb=True, name="e_in_dx0")
    du0 = _mm(dzf, w_z[:, D_SSD:], tb=True, add=du0, name="e_in_dx1")
    du0 = _mm(dxbc_raw, w_xbc, tb=True, add=du0, name="e_in_dx2")
    eighth = FOX_HEADDIM ** -0.5
    du0 = _mm(dq8, w_qkv[:, :D_FOX] * eighth, tb=True, add=du0, name="e_in_dx3q")
    du0 = _mm(dk, w_qkv[:, D_FOX:2 * D_FOX], tb=True, add=du0, name="e_in_dx3k")
    du0 = _mm(dv, w_qkv[:, 2 * D_FOX:], tb=True, add=du0, name="e_in_dx3v")
    du0 = _mm(ddtp, w_dt, tb=True, add=du0, name="e_in_dx4")
    du0 = _mm(dfp, w_f, tb=True, add=du0, name="e_in_dx5")
    g["e_w_in"] = jnp.concatenate([
        _mm(u0, dzs, ta=True, name="e_in_dw0"), _mm(u0, dzf, ta=True, name="e_in_dw1"),
        _mm(u0, dxbc_raw, ta=True, name="e_in_dw2"), _mm(u0, ddtp, ta=True, name="e_in_dw3")[:, :SSD_HEADS],
        _mm(u0, dq8, ta=True, name="e_in_dw4q") * eighth, _mm(u0, dk, ta=True, name="e_in_dw4k"),
        _mm(u0, dv, ta=True, name="e_in_dw4v"), _mm(u0, dfp, ta=True, name="e_in_dw5")[:, :FOX_HEADS]], axis=1)
    (dx,), (g["e_norm_pre"],) = _rowwise_bwd(_fn_rms, [c1(x)], [c1(w["e_norm_pre"])], [c1(du0)], [F32],
                                             adds={0: c1(dx1)}, tm=tm, name="e_pre_bwd")
    g["e_dt_bias"], g["e_fgate_b"] = ddtb[:, :SSD_HEADS], dfgb[:, :FOX_HEADS]
    g["e_a_log"], g["e_d_skip"] = dalog[:, :SSD_HEADS], ddsk[:, :SSD_HEADS]
    return loss, dx, g


_WEIGHTS = ["e_norm_pre", "e_w_in", "e_conv_w", "e_conv_b", "e_dt_bias", "e_a_log", "e_d_skip", "e_fgate_b",
            "e_ssd_norm", "e_w_out", "e_norm_post", "o_norm_pre", "o_w_in", "o_conv_w", "o_conv_b", "o_ln_g",
            "o_ln_b", "o_w_out", "o_norm_post"]
_BIG = ["e_w_in", "e_w_out", "o_w_in", "o_w_out"]
_ROW_SHARDED = ["e_w_out", "o_w_out"]
_SMALL_SHARDED = ["e_conv_w", "o_norm_pre", "o_conv_w", "o_conv_b", "o_ln_g", "o_ln_b", "o_norm_post"]
_REPLICATED = ["e_norm_pre", "e_conv_b", "e_dt_bias", "e_a_log", "e_d_skip", "e_fgate_b", "e_ssd_norm", "e_norm_post"]
_SMALL = [n for n in _WEIGHTS if n not in _BIG]
_EVEN_SHARDED = ["e_w_in", "e_w_out", "e_conv_w"]
_ODD_SHARDED = ["o_w_in", "o_w_out", "o_norm_pre", "o_conv_w", "o_conv_b", "o_ln_g", "o_ln_b", "o_norm_post"]
_EARLY_GRADS = ["o_w_in", "o_w_out", "e_w_out"]
N_CHIPS = 4


def _join(gathered, rows):
    k, r, c = gathered.shape
    return gathered.reshape(k * r, c) if rows else gathered.transpose(1, 0, 2).reshape(r, k * c)


def _split(full, rows):
    r, c = full.shape
    return full.reshape(N_CHIPS, r // N_CHIPS, c) if rows else full.reshape(r, N_CHIPS, c // N_CHIPS).transpose(1, 0, 2)


def kernel(x, e_norm_pre, e_w_in, e_conv_w, e_conv_b, e_dt_bias, e_a_log, e_d_skip, e_fgate_b, e_ssd_norm, e_w_out, e_norm_post, o_norm_pre, o_w_in, o_conv_w, o_conv_b, o_ln_g, o_ln_b, o_w_out, o_norm_post, loss_target, m_e_norm_pre, m_e_w_in, m_e_conv_w, m_e_conv_b, m_e_dt_bias, m_e_a_log, m_e_d_skip, m_e_fgate_b, m_e_ssd_norm, m_e_w_out, m_e_norm_post, m_o_norm_pre, m_o_w_in, m_o_conv_w, m_o_conv_b, m_o_ln_g, m_o_ln_b, m_o_w_out, m_o_norm_post, v_e_norm_pre, v_e_w_in, v_e_conv_w, v_e_conv_b, v_e_dt_bias, v_e_a_log, v_e_d_skip, v_e_fgate_b, v_e_ssd_norm, v_e_w_out, v_e_norm_post, v_o_norm_pre, v_o_w_in, v_o_conv_w, v_o_conv_b, v_o_ln_g, v_o_ln_b, v_o_w_out, v_o_norm_post):
    wvals = (e_norm_pre, e_w_in, e_conv_w, e_conv_b, e_dt_bias, e_a_log, e_d_skip, e_fgate_b, e_ssd_norm, e_w_out,
             e_norm_post, o_norm_pre, o_w_in, o_conv_w, o_conv_b, o_ln_g, o_ln_b, o_w_out, o_norm_post)
    mvals = (m_e_norm_pre, m_e_w_in, m_e_conv_w, m_e_conv_b, m_e_dt_bias, m_e_a_log, m_e_d_skip, m_e_fgate_b,
             m_e_ssd_norm, m_e_w_out, m_e_norm_post, m_o_norm_pre, m_o_w_in, m_o_conv_w, m_o_conv_b, m_o_ln_g,
             m_o_ln_b, m_o_w_out, m_o_norm_post)
    vvals = (v_e_norm_pre, v_e_w_in, v_e_conv_w, v_e_conv_b, v_e_dt_bias, v_e_a_log, v_e_d_skip, v_e_fgate_b,
             v_e_ssd_norm, v_e_w_out, v_e_norm_post, v_o_norm_pre, v_o_w_in, v_o_conv_w, v_o_conv_b, v_o_ln_g,
             v_o_ln_b, v_o_w_out, v_o_norm_post)

    def mat(v):
        return v.reshape(v.shape[-2:]) if v.ndim == 3 else v

    w = {n: mat(v) for n, v in zip(_WEIGHTS, wvals)}
    m = {n: mat(v) for n, v in zip(_WEIGHTS, mvals)}
    v2 = {n: mat(v) for n, v in zip(_WEIGHTS, vvals)}
    me_xy = 2 * lax.axis_index("x") + lax.axis_index("y")

    def shard(n):
        return w[n].astype(BF16) if n in _BIG else w[n]

    gathered = _exchange([shard(n) for n in _EVEN_SHARDED], "xy", False, name="gather_weights")
    full = {n: w[n] for n in _REPLICATED}
    for n, gth in zip(_EVEN_SHARDED, gathered):
        full[n] = _join(gth, n in _ROW_SHARDED)
    gparts = {}

    class _StepComm(_NoComm):
        def odd_weights(self):
            return _Exchange([shard(n) for n in _ODD_SHARDED], "xy", False)

        def got_odd_weights(self, got, wdict):
            for n, gth in zip(_ODD_SHARDED, got):
                wdict[n] = _join(gth, n in _ROW_SHARDED)

        def early_grads(self, g):
            return _Exchange([_split(g[n], n in _ROW_SHARDED).astype(BF16) for n in _EARLY_GRADS], "xy", True)

        def got_early_grads(self, got):
            self.sums = [_sum_slots(p, out_dtype=BF16, name="sum_" + n) for n, p in zip(_EARLY_GRADS, got)]

        def early_sums(self):
            return _Exchange(self.sums, "c", False)

        def got_early_sums(self, got):
            gparts.update(zip(_EARLY_GRADS, got))

    loss, dx, g = _local_step(x[0], loss_target[0], full, _StepComm())
    loss = lax.psum(loss[0, 0], ("x", "y", "c"))

    flat = jnp.concatenate([_pad_lanes(g[n].reshape(1, -1)) for n in _SMALL], axis=1).reshape(-1, LANES)
    (scattered,), (all8,) = _exchanges([_Exchange([_split(g["e_w_in"], False).astype(BF16)], "xy", True),
                                        _Exchange([flat], "xyc", False)], name="scatter_grads")
    (gparts["e_w_in"],) = _exchange([_sum_slots(scattered, out_dtype=BF16, name="sum_e_w_in")], "c", False,
                                    name="pair_grads")
    total = _sum_slots(all8, name="sum_small").reshape(1, -1)
    at = 0
    for n in _SMALL:
        size = g[n].size
        gn = total[:, at:at + size].reshape(g[n].shape)
        at += size + (-size % LANES)
        if n in _SMALL_SHARDED:
            cols = gn.shape[1] // N_CHIPS
            gn = lax.dynamic_slice(gn, (0, me_xy * cols), (gn.shape[0], cols))
        gparts[n] = gn[None]

    grads, deltas, new_m, new_v = [], [], [], []
    for n, orig in zip(_WEIGHTS, wvals):
        gn, dn, mn, vn = _adamw(w[n], gparts[n], m[n], v2[n], name="adamw_" + n)
        for lst, val in zip((grads, deltas, new_m, new_v), (gn, dn, mn, vn)):
            lst.append(val.reshape(orig.shape))
    return (loss, dx[None], *grads, *deltas, *new_m, *new_v)
```

```python
import functools

import jax
import jax.numpy as jnp
from jax import lax
from jax.experimental import pallas as pl
from jax.experimental.pallas import tpu as pltpu

F32 = jnp.float32
BF16 = jnp.bfloat16
MESH = pl.DeviceIdType.MESH

D_MODEL = 1024
D_SSD = 1024
SSD_HEADS = 16
SSD_HEADDIM = 64
SSD_GROUPS = 4
SSD_HPG = 4
D_STATE = 128
SSD_CONV = 4
CHUNK = 128
D_FOX = 1024
FOX_HEADS = 16
FOX_HEADDIM = 64
D_CONV = 2048
CONV_WIDTH = 31
EPS = 1e-6
LANES = 128
VMEM_LIMIT = 56 * 1024 * 1024

ADAM_LR = 0.001
ADAM_B1 = 0.9
ADAM_B2 = 0.999
ADAM_EPS = 1e-08
ADAM_WD = 0.01
ADAM_STEP = 10


def _cparams(sem=None):
    return pltpu.CompilerParams(dimension_semantics=sem, vmem_limit_bytes=VMEM_LIMIT)


def _mm(a, b, *, ta=False, tb=False, add=None, out_dtype=F32, tm=1024, tn=None, tk=2048, name):
    m = a.shape[1] if ta else a.shape[0]
    k = a.shape[0] if ta else a.shape[1]
    n = b.shape[0] if tb else b.shape[1]
    if tn is None:
        tn = 1024
    tm, tn = min(tm, m), min(tn, n)
    tk = max(t for t in range(LANES, min(tk, k) + 1, LANES) if k % t == 0)
    assert m % tm == 0 and n % tn == 0 and k % tk == 0, (m, n, k, tm, tn, tk)
    nk = k // tk
    dims = (((0 if ta else 1,), (1 if tb else 0,)), ((), ()))

    def body(*refs):
        if add is None:
            a_ref, b_ref, o_ref = refs[:3]
            c_ref = None
        else:
            a_ref, b_ref, c_ref, o_ref = refs[:4]
        kk = pl.program_id(2)
        prod = lax.dot_general(a_ref[...].astype(BF16), b_ref[...].astype(BF16), dims, preferred_element_type=F32)
        if nk == 1:
            o_ref[...] = (prod if c_ref is None else prod + c_ref[...].astype(F32)).astype(o_ref.dtype)
            return
        acc_ref = refs[-1]

        @pl.when(kk == 0)
        def _():
            acc_ref[...] = prod if c_ref is None else prod + c_ref[...].astype(F32)

        @pl.when((kk > 0) & (kk < nk - 1))
        def _():
            acc_ref[...] += prod

        @pl.when(kk == nk - 1)
        def _():
            o_ref[...] = (acc_ref[...] + prod).astype(o_ref.dtype)

    a_spec = (pl.BlockSpec((tk, tm), lambda j, i, kk: (kk, i)) if ta
              else pl.BlockSpec((tm, tk), lambda j, i, kk: (i, kk)))
    b_spec = (pl.BlockSpec((tn, tk), lambda j, i, kk: (j, kk)) if tb
              else pl.BlockSpec((tk, tn), lambda j, i, kk: (kk, j)))
    o_spec = pl.BlockSpec((tm, tn), lambda j, i, kk: (i, j))
    in_specs, args = [a_spec, b_spec], [a, b]
    if add is not None:
        in_specs.append(o_spec)
        args.append(add)
    return pl.pallas_call(
        body, name=name, grid=(n // tn, m // tm, nk),
        in_specs=in_specs, out_specs=o_spec,
        out_shape=jax.ShapeDtypeStruct((m, n), out_dtype),
        scratch_shapes=[pltpu.VMEM((tm, tn), F32)] if nk > 1 else [],
        compiler_params=_cparams(("parallel", "parallel", "arbitrary")),
    )(*args)


def _col(arr, cb, width):
    return (arr, cb, width)


def _row_specs(ops, tm):
    return [pl.BlockSpec((tm, w), lambda i, cb=cb: (i, cb)) for (_, cb, w) in ops]


def _par_specs(ops):
    return [pl.BlockSpec((a.shape[0], w), lambda i, cb=cb: (0, cb)) for (a, cb, w) in ops]


def _rowwise_fwd(fn, rows, params, outs, *, tm, name):
    s = rows[0][0].shape[0]
    tm = min(tm, s)
    nr, npar = len(rows), len(params)

    def body(*refs):
        rv = [r[...].astype(F32) for r in refs[:nr]]
        pv = [p[...].astype(F32) for p in refs[nr:nr + npar]]
        res = fn(*rv, *pv)
        for o_ref, val in zip(refs[nr + npar:], res):
            o_ref[...] = val.astype(o_ref.dtype)

    return pl.pallas_call(
        body, name=name, grid=(s // tm,),
        in_specs=_row_specs(rows, tm) + _par_specs(params),
        out_specs=[pl.BlockSpec((tm, w), lambda i: (i, 0)) for (w, _) in outs],
        out_shape=[jax.ShapeDtypeStruct((s, w), dt) for (w, dt) in outs],
        compiler_params=_cparams(("parallel",)),
    )(*[r[0] for r in rows], *[p[0] for p in params])


def _rowwise_bwd(fn, rows, params, couts, row_grads, *, adds=None, tm, name):
    adds = adds or {}
    s = rows[0][0].shape[0]
    tm = min(tm, s)
    nr, npar, nc = len(rows), len(params), len(couts)
    add_keys = sorted(adds)
    want = [i for i, dt in enumerate(row_grads) if dt is not None]

    def body(*refs):
        i = pl.program_id(0)
        rv = [r[...].astype(F32) for r in refs[:nr]]
        pv = [p[...].astype(F32) for p in refs[nr:nr + npar]]
        cv = [c[...].astype(F32) for c in refs[nr + npar:nr + npar + nc]]
        av = {k: r[...].astype(F32) for k, r in zip(add_keys, refs[nr + npar + nc:nr + npar + nc + len(add_keys)])}
        orefs = refs[nr + npar + nc + len(add_keys):]
        _, vjp = jax.vjp(lambda rr, pp: tuple(fn(*rr, *pp)), rv, pv)
        drows, dpars = vjp(tuple(cv))
        for o_ref, ri in zip(orefs[:len(want)], want):
            g = drows[ri]
            if ri in av:
                g = g + av[ri]
            o_ref[...] = g.astype(o_ref.dtype)

        @pl.when(i == 0)
        def _():
            for o_ref in orefs[len(want):]:
                o_ref[...] = jnp.zeros_like(o_ref)

        for o_ref, g in zip(orefs[len(want):], dpars):
            o_ref[...] += g

    add_ops = [adds[k] for k in add_keys]
    out_specs = ([pl.BlockSpec((tm, rows[ri][2]), lambda i: (i, 0)) for ri in want]
                 + [pl.BlockSpec((p[0].shape[0], p[2]), lambda i: (0, 0)) for p in params])
    out_shape = ([jax.ShapeDtypeStruct((s, rows[ri][2]), row_grads[ri]) for ri in want]
                 + [jax.ShapeDtypeStruct((p[0].shape[0], p[2]), F32) for p in params])
    res = pl.pallas_call(
        body, name=name, grid=(s // tm,),
        in_specs=_row_specs(rows, tm) + _par_specs(params) + _row_specs(couts, tm) + _row_specs(add_ops, tm),
        out_specs=out_specs, out_shape=out_shape,
        compiler_params=_cparams(("arbitrary",)),
    )(*[r[0] for r in rows], *[p[0] for p in params], *[c[0] for c in couts], *[a[0] for a in add_ops])
    return res[:len(want)], res[len(want):]


def _silu(v):
    return v * jax.nn.sigmoid(v)


def _rms(v, g):
    return v * lax.rsqrt(jnp.mean(v * v, axis=-1, keepdims=True) + EPS) * g


SUBLANES = 8
CONV_ROWS = 256


def _halo(shifts):
    up = lambda v: -(-v // SUBLANES) * SUBLANES
    return up(max(0, -min(shifts))), up(max(0, max(shifts)))


def _fill_halo(xp_sc, x_ref, front, back):
    s = x_ref.shape[0]
    if front:
        xp_sc[0:front, :] = jnp.zeros((front, LANES), F32)
    if back:
        xp_sc[front + s:front + s + back, :] = jnp.zeros((back, LANES), F32)
    xp_sc[front:front + s, :] = x_ref[...].astype(F32)


def _shift_conv(x, w, b, shifts, *, name):
    s, c = x.shape
    tr = min(CONV_ROWS, s)
    nk = len(shifts)
    front, back = _halo(shifts)

    def body(*refs):
        if b is None:
            x_ref, w_ref, o_ref, xp_sc = refs
        else:
            x_ref, w_ref, b_ref, o_ref, xp_sc = refs
        _fill_halo(xp_sc, x_ref, front, back)

        def chunk(r, carry):
            base = pl.multiple_of(r * tr, tr)
            acc = jnp.zeros((tr, LANES), F32) if b is None else jnp.broadcast_to(b_ref[...], (tr, LANES))
            for kk in range(nk):
                acc = acc + xp_sc[pl.ds(base + front + shifts[kk], tr), :] * w_ref[kk:kk + 1, :]
            o_ref[pl.ds(base, tr), :] = acc
            return carry

        lax.fori_loop(0, s // tr, chunk, 0)

    strip = pl.BlockSpec((s, LANES), lambda cb: (0, cb))
    in_specs = [strip, pl.BlockSpec((nk, LANES), lambda cb: (0, cb))]
    args = [x, w]
    if b is not None:
        in_specs.append(pl.BlockSpec((1, LANES), lambda cb: (0, cb)))
        args.append(b)
    return pl.pallas_call(
        body, name=name, grid=(c // LANES,), in_specs=in_specs, out_specs=strip,
        out_shape=jax.ShapeDtypeStruct((s, c), F32),
        scratch_shapes=[pltpu.VMEM((front + s + back, LANES), F32)],
        compiler_params=_cparams(("parallel",)),
    )(*args)


def _conv_fwd(x, w, b, *, name):
    k = w.shape[0]
    return _shift_conv(x, w, b, [kk - (k - 1) for kk in range(k)], name=name)


def _conv_bwd_x(dy, w, *, name):
    k = w.shape[0]
    return _shift_conv(dy, w, None, [(k - 1) - kk for kk in range(k)], name=name)


def _conv_bwd_w(x, dy, k, *, name):
    s, c = x.shape
    tr = min(CONV_ROWS, s)
    shifts = [kk - (k - 1) for kk in range(k)]
    front, back = _halo(shifts)

    def fold(v):
        return jnp.sum(v.reshape(tr // SUBLANES, SUBLANES, LANES), axis=0)

    def body(x_ref, dy_ref, dw_ref, db_ref, xp_sc, dw_sc, db_sc):
        _fill_halo(xp_sc, x_ref, front, back)
        dw_sc[...] = jnp.zeros_like(dw_sc)
        db_sc[...] = jnp.zeros_like(db_sc)

        def chunk(r, carry):
            base = pl.multiple_of(r * tr, tr)
            dyv = dy_ref[pl.ds(base, tr), :]
            db_sc[...] += fold(dyv)
            for kk in range(k):
                dw_sc[kk] += fold(xp_sc[pl.ds(base + front + shifts[kk], tr), :] * dyv)
            return carry

        lax.fori_loop(0, s // tr, chunk, 0)
        db_ref[...] = jnp.sum(db_sc[...], axis=0, keepdims=True)
        for kk in range(k):
            dw_ref[kk:kk + 1, :] = jnp.sum(dw_sc[kk], axis=0, keepdims=True)

    strip = pl.BlockSpec((s, LANES), lambda cb: (0, cb))
    return pl.pallas_call(
        body, name=name, grid=(c // LANES,), in_specs=[strip, strip],
        out_specs=[pl.BlockSpec((k, LANES), lambda cb: (0, cb)), pl.BlockSpec((1, LANES), lambda cb: (0, cb))],
        out_shape=[jax.ShapeDtypeStruct((k, c), F32), jax.ShapeDtypeStruct((1, c), F32)],
        scratch_shapes=[pltpu.VMEM((front + s + back, LANES), F32), pltpu.VMEM((k, SUBLANES, LANES), F32),
                        pltpu.VMEM((SUBLANES, LANES), F32)],
        compiler_params=_cparams(("parallel",)),
    )(x, dy)


_DIMS = {"nn": ((1,), (0,)), "nt": ((1,), (1,)), "tn": ((0,), (0,))}


def _bd(a, b, mode):
    return lax.dot_general(a.astype(BF16), b.astype(BF16), (_DIMS[mode], ((), ())), preferred_element_type=F32)


@functools.partial(jax.custom_vjp, nondiff_argnums=(2,))
def _bdot(a, b, mode):
    return _bd(a, b, mode)


def _bdot_fwd(a, b, mode):
    return _bd(a, b, mode), (a, b)


def _bdot_bwd(mode, res, g):
    a, b = res
    if mode == "nn":
        return _bd(g, b, "nt"), _bd(a, g, "tn")
    if mode == "nt":
        return _bd(g, b, "nn"), _bd(g, a, "tn")
    return _bd(b, g, "nt"), _bd(a, g, "nn")


_bdot.defvjp(_bdot_fwd, _bdot_bwd)


def _split3(v):
    hi = v.astype(BF16)
    r1 = v - hi.astype(F32)
    mid = r1.astype(BF16)
    lo = (r1 - mid.astype(F32)).astype(BF16)
    return hi, mid, lo


def _mask_dot(mask01, v, mode):
    out = None
    for part in _split3(v):
        if mode == "vn":
            t = lax.dot_general(part, mask01, (_DIMS["nn"], ((), ())), preferred_element_type=F32)
        else:
            t = lax.dot_general(mask01, part, (_DIMS[mode], ((), ())), preferred_element_type=F32)
        out = t if out is None else out + t
    return out


def _lower_tri(n):
    r = lax.broadcasted_iota(jnp.int32, (n, n), 0)
    c = lax.broadcasted_iota(jnp.int32, (n, n), 1)
    return (r >= c).astype(BF16)


@jax.custom_vjp
def _tri_dot(w):
    return _mask_dot(_lower_tri(w.shape[0]), w, "nn")


def _tri_dot_fwd(w):
    return _tri_dot(w), None


def _tri_dot_bwd(_, g):
    return (_mask_dot(_lower_tri(g.shape[0]), g, "tn"),)


_tri_dot.defvjp(_tri_dot_fwd, _tri_dot_bwd)


def _cumsum_lanes(x, *, reverse, name):
    h, s = x.shape
    n = s // LANES

    def body(x_ref, o_ref):
        r = lax.broadcasted_iota(jnp.int32, (LANES, LANES), 0)
        c = lax.broadcasted_iota(jnp.int32, (LANES, LANES), 1)
        m01 = ((r >= c) if reverse else (r <= c)).astype(BF16)

        def step(t, carry):
            ci = (n - 1 - t) if reverse else t
            at = pl.ds(pl.multiple_of(ci * LANES, LANES), LANES)
            blk = x_ref[:, at]
            o_ref[:, at] = _mask_dot(m01, blk, "vn") + carry
            return carry + jnp.sum(blk, axis=1, keepdims=True)

        lax.fori_loop(0, n, step, jnp.zeros((h, 1), F32))

    return pl.pallas_call(body, name=name, out_shape=jax.ShapeDtypeStruct((h, s), F32),
                          compiler_params=_cparams())(x)


SSD_PAIRS = SSD_HPG // 2


def _ssd2_chunk(xs, dt, bm, cm, hin, a, dsk, head0):
    n = CHUNK
    row = lax.broadcasted_iota(jnp.int32, (n, n), 0)
    col = lax.broadcasted_iota(jnp.int32, (n, n), 1)
    lower = row >= col
    ustrict = (row > col).astype(F32)
    lane = lax.broadcasted_iota(jnp.int32, (1, LANES), 1)
    sub = lax.broadcasted_iota(jnp.int32, (n, 1), 0)
    e_first = (sub == 0).astype(F32)
    e_last = (sub == n - 1).astype(F32)
    lane0 = (lane == 0).astype(F32)
    half_l = [(lane < LANES // 2).astype(F32), (lane >= LANES // 2).astype(F32)]
    half_s = [(sub < LANES // 2).astype(F32), (sub >= LANES // 2).astype(F32)]
    cb = _bdot(cm, bm, "nt")
    da = dt * (-jnp.exp(a))
    ys, houts = [], []
    for pr in range(SSD_PAIRS):
        y = jnp.zeros((n, LANES), F32)
        xdte = jnp.zeros((n, LANES), F32)
        lane_gain = jnp.zeros((n, LANES), F32)
        row_gain = jnp.zeros((LANES, 1), F32)
        for hf in range(2):
            oh = (lane == head0 + 2 * pr + hf).astype(F32)
            dt_col = jnp.sum(dt * oh, axis=1, keepdims=True)
            da_col = jnp.sum(da * oh, axis=1, keepdims=True)
            dsk_h = jnp.sum(dsk * oh, axis=1, keepdims=True)
            seg = _tri_dot(da_col * ustrict)
            decay = jnp.where(lower, jnp.exp(seg), 0.0)
            cs_col = jnp.sum(seg * lane0, axis=1, keepdims=True) + jnp.sum(da_col * e_first, axis=0, keepdims=True)
            total = jnp.sum(cs_col * e_last, axis=0, keepdims=True)
            xh = xs[pr] * half_l[hf]
            xd = xh * dt_col
            y = y + _bdot(cb * decay, xd, "nn") + xh * dsk_h
            xdte = xdte + xd * jnp.exp(total - cs_col)
            lane_gain = lane_gain + jnp.exp(cs_col) * half_l[hf]
            row_gain = row_gain + jnp.exp(total) * half_s[hf]
        houts.append(hin[pr] * row_gain + _bdot(xdte, bm, "tn"))
        ys.append(y + _bdot(cm, hin[pr], "nt") * lane_gain)
    return ys, houts


SSD_STEP = 4


def _ssd2_steps(s):
    per = min(SSD_STEP, s // CHUNK)
    return per, s // (CHUNK * per)


def _ssd2_multi(xs, dt, bm, cm, hin, a, dsk, head0):
    ys = []
    for k in range(len(dt)):
        y, hin = _ssd2_chunk(xs[k], dt[k], bm[k], cm[k], hin, a, dsk, head0)
        ys.append(y)
    return ys, hin


def _ssd2_specs(per, nc, rev):
    cc = (lambda c: nc - 1 - c) if rev else (lambda c: c)
    rows = CHUNK * per
    act = pl.BlockSpec((rows, SSD_PAIRS * LANES), lambda c, g: (cc(c), g))
    row = pl.BlockSpec((rows, LANES), lambda c, g: (cc(c), 0))
    bmat = pl.BlockSpec((rows, LANES), lambda c, g: (cc(c), D_SSD // LANES + g))
    cmat = pl.BlockSpec((rows, LANES), lambda c, g: (cc(c), D_SSD // LANES + SSD_GROUPS + g))
    par = pl.BlockSpec((1, LANES), lambda c, g: (0, 0))
    hs = pl.BlockSpec((1, SSD_PAIRS, LANES, D_STATE), lambda c, g: (cc(c), g, 0, 0))
    return act, row, bmat, cmat, par, hs


def _chunk_rows(ref, k):
    return ref[k * CHUNK:(k + 1) * CHUNK, :]


def _pair_cols(ref, k):
    return [ref[k * CHUNK:(k + 1) * CHUNK, pr * LANES:(pr + 1) * LANES] for pr in range(SSD_PAIRS)]


def _ssd2_fwd(dt, xbc, a, dsk, *, name):
    s = xbc.shape[0]
    per, nc = _ssd2_steps(s)
    act, row, bmat, cmat, par, hs = _ssd2_specs(per, nc, False)

    def body(xs_ref, dt_ref, bm_ref, cm_ref, a_ref, dsk_ref, y_ref, hs_ref, h_sc):
        c, g = pl.program_id(0), pl.program_id(1)

        @pl.when(c == 0)
        def _():
            h_sc[pl.ds(g * SSD_PAIRS, SSD_PAIRS)] = jnp.zeros((SSD_PAIRS, LANES, D_STATE), F32)

        hin = [h_sc[g * SSD_PAIRS + pr] for pr in range(SSD_PAIRS)]
        ks = range(per)
        ys, houts = _ssd2_multi([_pair_cols(xs_ref, k) for k in ks], [_chunk_rows(dt_ref, k) for k in ks],
                                [_chunk_rows(bm_ref, k) for k in ks], [_chunk_rows(cm_ref, k) for k in ks],
                                hin, a_ref[...], dsk_ref[...], g * SSD_HPG)
        for pr in range(SSD_PAIRS):
            for k in ks:
                y_ref[k * CHUNK:(k + 1) * CHUNK, pr * LANES:(pr + 1) * LANES] = ys[k][pr]
            hs_ref[0, pr] = hin[pr]
            h_sc[g * SSD_PAIRS + pr] = houts[pr]

    return pl.pallas_call(
        body, name=name, grid=(nc, SSD_GROUPS),
        in_specs=[act, row, bmat, cmat, par, par], out_specs=[act, hs],
        out_shape=[jax.ShapeDtypeStruct((s, D_SSD), F32),
                   jax.ShapeDtypeStruct((nc, SSD_HEADS // 2, LANES, D_STATE), F32)],
        scratch_shapes=[pltpu.VMEM((SSD_HEADS // 2, LANES, D_STATE), F32)],
        compiler_params=_cparams(("arbitrary", "arbitrary")),
    )(xbc, dt, xbc, xbc, a, dsk)


def _ssd2_bwd(dt, xbc, a, dsk, hsave, dy, *, carry=None, name):
    s = xbc.shape[0]
    per, nc = _ssd2_steps(s)
    act, row, bmat, cmat, par, hs = _ssd2_specs(per, nc, True)
    gmat = pl.BlockSpec((CHUNK * per, LANES), lambda c, g: (nc - 1 - c, g))

    def body(*refs):
        ins, outs, (dh_sc,), comm = _carried(carry, refs, 8, 6, 1)
        xs_ref, dt_ref, bm_ref, cm_ref, a_ref, dsk_ref, hs_ref, dy_ref = ins
        dxs_ref, ddt_ref, dbm_ref, dcm_ref, da_ref, ddsk_ref = outs
        c, g = pl.program_id(0), pl.program_id(1)
        if carry is not None:
            @pl.when((c == 0) & (g == 0))
            def _():
                carry.start(*comm)

        @pl.when(c == 0)
        def _():
            dh_sc[pl.ds(g * SSD_PAIRS, SSD_PAIRS)] = jnp.zeros((SSD_PAIRS, LANES, D_STATE), F32)

        @pl.when((c == 0) & (g == 0))
        def _():
            da_ref[...] = jnp.zeros_like(da_ref)
            ddsk_ref[...] = jnp.zeros_like(ddsk_ref)

        @pl.when(g == 0)
        def _():
            ddt_ref[...] = jnp.zeros_like(ddt_ref)

        head0 = g * SSD_HPG
        ks = range(per)
        prim = ([_pair_cols(xs_ref, k) for k in ks], [_chunk_rows(dt_ref, k) for k in ks],
                [_chunk_rows(bm_ref, k) for k in ks], [_chunk_rows(cm_ref, k) for k in ks],
                [hs_ref[0, pr] for pr in range(SSD_PAIRS)], a_ref[...], dsk_ref[...])
        _, vjp = jax.vjp(lambda *p: _ssd2_multi(*p, head0), *prim)
        cot = ([_pair_cols(dy_ref, k) for k in ks], [dh_sc[g * SSD_PAIRS + pr] for pr in range(SSD_PAIRS)])
        dxs, ddt, dbm, dcm, dhin, da, ddsk = vjp(cot)
        for pr in range(SSD_PAIRS):
            for k in ks:
                dxs_ref[k * CHUNK:(k + 1) * CHUNK, pr * LANES:(pr + 1) * LANES] = dxs[k][pr]
            dh_sc[g * SSD_PAIRS + pr] = dhin[pr]
        for k in ks:
            ddt_ref[k * CHUNK:(k + 1) * CHUNK, :] += ddt[k]
            dbm_ref[k * CHUNK:(k + 1) * CHUNK, :] = dbm[k]
            dcm_ref[k * CHUNK:(k + 1) * CHUNK, :] = dcm[k]
        da_ref[...] += da
        ddsk_ref[...] += ddsk
        if carry is not None:
            @pl.when((c == nc - 1) & (g == SSD_GROUPS - 1))
            def _():
                carry.wait(*comm)

    extra = carry if carry is not None else _Exchange([], "c", False)
    return pl.pallas_call(
        body, name=name, grid=(nc, SSD_GROUPS),
        in_specs=[act, row, bmat, cmat, par, par, hs, act] + extra.in_specs,
        out_specs=[act, row, gmat, gmat, par, par] + extra.out_specs,
        out_shape=[jax.ShapeDtypeStruct((s, D_SSD), F32), jax.ShapeDtypeStruct((s, LANES), F32),
                   jax.ShapeDtypeStruct((s, SSD_GROUPS * D_STATE), F32),
                   jax.ShapeDtypeStruct((s, SSD_GROUPS * D_STATE), F32),
                   jax.ShapeDtypeStruct((1, LANES), F32), jax.ShapeDtypeStruct((1, LANES), F32)] + extra.out_shape,
        scratch_shapes=[pltpu.VMEM((SSD_HEADS // 2, LANES, D_STATE), F32)]
        + (carry.scratch if carry is not None else []),
        compiler_params=_cparams(("arbitrary", "arbitrary")),
    )(xbc, dt, xbc, xbc, a, dsk, hsave, dy, *extra.arrs)


FOX_BLOCK = 256
NEG = -1e30


def _fox_scores(q, k, cref, ck, strictly_below):
    t = q.shape[0]
    s = lax.dot_general(q, k, (_DIMS["nt"], ((), ())), preferred_element_type=F32) * (FOX_HEADDIM ** -0.5)
    s = s + (cref - ck)
    row = lax.broadcasted_iota(jnp.int32, (t, t), 0)
    col = lax.broadcasted_iota(jnp.int32, (t, t), 1)
    mask = (row >= col) | strictly_below
    return s, mask


def _fox_fwd(q, k, v, c, *, name):
    h, s, p = q.shape
    t = min(FOX_BLOCK, s)
    nb = s // t

    def body(q_ref, k_ref, v_ref, cq_ref, ck_ref, o_ref, lse_ref, m_sc, l_sc, acc_sc):
        i, j = pl.program_id(1), pl.program_id(2)

        @pl.when(j == 0)
        def _():
            m_sc[...] = jnp.full_like(m_sc, NEG)
            l_sc[...] = jnp.zeros_like(l_sc)
            acc_sc[...] = jnp.zeros_like(acc_sc)

        @pl.when(j <= i)
        def _():
            sc, mask = _fox_scores(q_ref[0], k_ref[0], cq_ref[0, 0:1, 0:1], ck_ref[0], j < i)
            sc = jnp.where(mask, sc, NEG)
            m_old = m_sc[...]
            m_new = jnp.maximum(m_old, jnp.max(sc, axis=1, keepdims=True))
            alpha = jnp.exp(m_old - m_new)
            pr = jnp.exp(sc - m_new)
            l_sc[...] = alpha * l_sc[...] + jnp.sum(pr, axis=1, keepdims=True)
            pr_hi = pr.astype(BF16)
            pr_lo = (pr - pr_hi.astype(F32)).astype(BF16)
            pv = (lax.dot_general(pr_hi, v_ref[0], (_DIMS["nn"], ((), ())), preferred_element_type=F32)
                  + lax.dot_general(pr_lo, v_ref[0], (_DIMS["nn"], ((), ())), preferred_element_type=F32))
            acc_sc[...] = alpha * acc_sc[...] + pv
            m_sc[...] = m_new

        @pl.when(j == i)
        def _():
            o_ref[0] = acc_sc[...] / l_sc[...]
            lse_ref[0] = jnp.broadcast_to(m_sc[...] + jnp.log(l_sc[...]), (t, LANES))

    qspec = pl.BlockSpec((1, t, p), lambda hh, i, j: (hh, i, 0))
    kspec = pl.BlockSpec((1, t, p), lambda hh, i, j: (hh, jnp.minimum(j, i), 0))
    return pl.pallas_call(
        body, name=name, grid=(h, nb, nb),
        in_specs=[qspec, kspec, kspec,
                  pl.BlockSpec((1, 1, t), lambda hh, i, j: (hh, 0, i)),
                  pl.BlockSpec((1, 1, t), lambda hh, i, j: (hh, 0, jnp.minimum(j, i)))],
        out_specs=[qspec, pl.BlockSpec((1, t, LANES), lambda hh, i, j: (hh, i, 0))],
        out_shape=[jax.ShapeDtypeStruct((h, s, p), F32), jax.ShapeDtypeStruct((h, s, LANES), F32)],
        scratch_shapes=[pltpu.VMEM((t, 1), F32), pltpu.VMEM((t, 1), F32), pltpu.VMEM((t, p), F32)],
        compiler_params=_cparams(("parallel", "arbitrary", "arbitrary")),
    )(q, k, v, c, c)


def _fox_bwd(q, k, v, c, o, lse, do, *, name):
    h, s, p = q.shape
    t = min(FOX_BLOCK, s)
    nb = s // t
    scale = FOX_HEADDIM ** -0.5

    def body(q_ref, k_ref, v_ref, cq_ref, ck_ref, o_ref, lse_ref, do_ref,
             dq_ref, dk_ref, dv_ref, dc_ref, dk_sc, dv_sc, dc_sc):
        j, i = pl.program_id(1), pl.program_id(2)

        @pl.when(i == 0)
        def _():
            dk_sc[...] = jnp.zeros_like(dk_sc)
            dv_sc[...] = jnp.zeros_like(dv_sc)
            dc_sc[...] = jnp.zeros_like(dc_sc)

        @pl.when(i >= j)
        def _():
            qv, kv, vv = q_ref[0], k_ref[0], v_ref[0]
            sc, mask = _fox_scores(qv, kv, cq_ref[0, 0:1, 0:1], ck_ref[0], i > j)
            pr = jnp.where(mask, jnp.exp(sc - lse_ref[0, :, 0:1]), 0.0)
            dov = do_ref[0]
            dob = dov.astype(BF16)
            prb = pr.astype(BF16)
            dv_sc[...] += lax.dot_general(prb, dob, (_DIMS["tn"], ((), ())), preferred_element_type=F32)
            dp = lax.dot_general(dob, vv, (_DIMS["nt"], ((), ())), preferred_element_type=F32)
            dcol = jnp.sum(dob.astype(F32) * o_ref[0], axis=1, keepdims=True)
            ds = pr * (dp - dcol)
            dc_sc[...] -= jnp.sum(ds, axis=0, keepdims=True)
            dsb = ds.astype(BF16)
            dqc = scale * lax.dot_general(dsb, kv, (_DIMS["nn"], ((), ())), preferred_element_type=F32)
            at = pl.ds(pl.multiple_of(i * t, t), t)

            @pl.when(j == 0)
            def _():
                dq_ref[0, at, :] = dqc

            @pl.when(j > 0)
            def _():
                dq_ref[0, at, :] += dqc

            dk_sc[...] += scale * lax.dot_general(dsb, qv, (_DIMS["tn"], ((), ())), preferred_element_type=F32)

        @pl.when(i == nb - 1)
        def _():
            dk_ref[0] = dk_sc[...]
            dv_ref[0] = dv_sc[...]
            dc_ref[0] = dc_sc[...]

    qspec = pl.BlockSpec((1, t, p), lambda hh, j, i: (hh, jnp.maximum(i, j), 0))
    kspec = pl.BlockSpec((1, t, p), lambda hh, j, i: (hh, j, 0))
    cq = pl.BlockSpec((1, 1, t), lambda hh, j, i: (hh, 0, jnp.maximum(i, j)))
    ck = pl.BlockSpec((1, 1, t), lambda hh, j, i: (hh, 0, j))
    return pl.pallas_call(
        body, name=name, grid=(h, nb, nb),
        in_specs=[qspec, kspec, kspec, cq, ck, qspec,
                  pl.BlockSpec((1, t, LANES), lambda hh, j, i: (hh, jnp.maximum(i, j), 0)), qspec],
        out_specs=[pl.BlockSpec((1, s, p), lambda hh, j, i: (hh, 0, 0)), kspec, kspec, ck],
        out_shape=[jax.ShapeDtypeStruct((h, s, p), F32), jax.ShapeDtypeStruct((h, s, p), F32),
                   jax.ShapeDtypeStruct((h, s, p), F32), jax.ShapeDtypeStruct((h, 1, s), F32)],
        scratch_shapes=[pltpu.VMEM((t, p), F32), pltpu.VMEM((t, p), F32), pltpu.VMEM((1, t), F32)],
        compiler_params=_cparams(("parallel", "arbitrary", "arbitrary")),
    )(q, k, v, c, c, o, lse, do)


AUX = 64


def _pack(main, cols):
    h, s, p = main.shape
    parts = [main.astype(BF16)]
    if cols:
        parts.append(jnp.stack(cols, axis=-1).astype(BF16))
    parts.append(jnp.zeros((h, s, LANES - p - len(cols)), BF16))
    return jnp.concatenate(parts, axis=-1)


def _terms(v):
    hi = lax.reduce_precision(v, 8, 7)
    mid = lax.reduce_precision(v - hi, 8, 7)
    lo = lax.reduce_precision(v - hi - mid, 8, 7)
    return [hi, mid, lo]


def _fox_pack_qkv(q, k, v):
    h, s, _ = q.shape
    one = jnp.ones((h, s), F32)
    return _pack(q * (FOX_HEADDIM ** -0.5), []), _pack(k, []), _pack(v, [one, one, one])


def _fox_bias(c_ref, qblock, kblock, t):
    lane = lax.broadcasted_iota(jnp.int32, (1, LANES), 1)
    cq = c_ref[0, :, pl.ds(pl.multiple_of(qblock * t, LANES), LANES)]
    cref = jnp.sum(jnp.where(lane == 0, cq, 0.0), axis=1, keepdims=True)
    return cref - c_ref[0, :, pl.ds(pl.multiple_of(kblock * t, LANES), t)]


def _fox_rowdot(do, o, *, tm=256, name):
    s, d = do.shape
    tm = min(tm, s)

    def body(do_ref, o_ref, d_ref):
        prod = do_ref[...].astype(BF16).astype(F32) * o_ref[...]
        r = lax.broadcasted_iota(jnp.int32, (d, LANES), 0)
        c = lax.broadcasted_iota(jnp.int32, (d, LANES), 1)
        mine = (r >= c * FOX_HEADDIM) & (r < (c + 1) * FOX_HEADDIM)
        d_ref[...] = _mask_dot(mine.astype(BF16), prod, "vn")

    row = pl.BlockSpec((tm, d), lambda i: (i, 0))
    return pl.pallas_call(body, name=name, grid=(s // tm,), in_specs=[row, row],
                          out_specs=pl.BlockSpec((tm, LANES), lambda i: (i, 0)),
                          out_shape=jax.ShapeDtypeStruct((s, LANES), F32),
                          compiler_params=_cparams(("parallel",)))(do, o)


def _causal(t):
    return lax.broadcasted_iota(jnp.int32, (t, t), 0) >= lax.broadcasted_iota(jnp.int32, (t, t), 1)


def _fox2_fwd(qp, kp, vp, c, *, name):
    h, s, _ = qp.shape
    t = min(FOX_BLOCK, s)
    nb = s // t
    nt = (((1,), (1,)), ((), ()))
    nn = (((1,), (0,)), ((), ()))

    def body(q_ref, k_ref, v_ref, c_ref, o_ref, lse_ref, m_sc, acc_sc):
        i = pl.program_id(1)
        m_sc[...] = jnp.full_like(m_sc, NEG)
        acc_sc[...] = jnp.zeros_like(acc_sc)
        qv = q_ref[0]

        def step(j, masked):
            at = pl.ds(pl.multiple_of(j * t, t), t)
            kv, vv = k_ref[0, at, :], v_ref[0, at, :]
            sc = lax.dot_general(qv, kv, nt, preferred_element_type=F32) + _fox_bias(c_ref, i, j, t)
            if masked:
                sc = jnp.where(_causal(t), sc, NEG)
            m_prev = m_sc[...]
            m_new = jnp.maximum(m_prev, jnp.max(sc, axis=1, keepdims=True))
            pr = jnp.exp(sc - jnp.tile(m_new, (1, t // LANES)))
            pr_hi = pr.astype(BF16)
            pr_lo = (pr - pr_hi.astype(F32)).astype(BF16)
            pv = (lax.dot_general(pr_hi, vv, nn, preferred_element_type=F32)
                  + lax.dot_general(pr_lo, vv, nn, preferred_element_type=F32))
            acc_sc[...] = jnp.exp(m_prev - m_new) * acc_sc[...] + pv
            m_sc[...] = m_new

        lax.fori_loop(0, i, lambda j, carry: (step(j, False), carry)[1], 0)
        step(i, True)
        acc = acc_sc[...]
        lane = lax.broadcasted_iota(jnp.int32, (1, LANES), 1)
        den = jnp.sum(jnp.where(lane == AUX, acc, 0.0), axis=1, keepdims=True)
        o_ref[0] = (acc / den)[:, :FOX_HEADDIM]
        lse_ref[0] = m_sc[...] + jnp.log(den)

    whole = pl.BlockSpec((1, s, LANES), lambda hh, i: (hh, 0, 0))
    return pl.pallas_call(
        body, name=name, grid=(h, nb),
        in_specs=[pl.BlockSpec((1, t, LANES), lambda hh, i: (hh, i, 0)), whole, whole,
                  pl.BlockSpec((1, 1, s), lambda hh, i: (hh, 0, 0))],
        out_specs=[pl.BlockSpec((1, t, FOX_HEADDIM), lambda hh, i: (hh, i, 0)),
                   pl.BlockSpec((1, t, LANES), lambda hh, i: (hh, i, 0))],
        out_shape=[jax.ShapeDtypeStruct((h, s, FOX_HEADDIM), F32), jax.ShapeDtypeStruct((h, s, LANES), F32)],
        scratch_shapes=[pltpu.VMEM((t, LANES), F32), pltpu.VMEM((t, LANES), F32)],
        compiler_params=_cparams(("parallel", "arbitrary")),
    )(qp, kp, vp, c)


def _fox2_bwd(qp, kp, vp, c, dop, lse, *, name):
    h, s, _ = qp.shape
    t = min(FOX_BLOCK, s)
    nb = s // t
    nt = (((1,), (1,)), ((), ()))
    nn = (((1,), (0,)), ((), ()))
    tn = (((0,), (0,)), ((), ()))

    def body(k_ref, v_ref, q_ref, c_ref, do_ref, lse_ref, dq_ref, dk_ref, dv_ref, dc_ref, dk_sc, dv_sc, dc_sc):
        j = pl.program_id(1)

        @pl.when(j == 0)
        def _():
            dq_ref[...] = jnp.zeros_like(dq_ref)

        dk_sc[...] = jnp.zeros_like(dk_sc)
        dv_sc[...] = jnp.zeros_like(dv_sc)
        dc_sc[...] = jnp.zeros_like(dc_sc)
        kv, vv = k_ref[0], v_ref[0]

        def step(i, masked):
            at = pl.ds(pl.multiple_of(i * t, t), t)
            qv, dov = q_ref[0, at, :], do_ref[0, at, :]
            sc = lax.dot_general(qv, kv, nt, preferred_element_type=F32) + _fox_bias(c_ref, i, j, t)
            pr = jnp.exp(sc - jnp.tile(lse_ref[0, at, :], (1, t // LANES)))
            if masked:
                pr = jnp.where(_causal(t), pr, 0.0)
            ds = pr * lax.dot_general(dov, vv, nt, preferred_element_type=F32)
            dc_sc[...] -= jnp.sum(ds, axis=0, keepdims=True)
            dsb = ds.astype(BF16)
            dv_sc[...] += lax.dot_general(pr.astype(BF16), dov, tn, preferred_element_type=F32)
            dk_sc[...] += lax.dot_general(dsb, qv, tn, preferred_element_type=F32)
            dq_ref[0, at, :] += lax.dot_general(dsb, kv, nn, preferred_element_type=F32)

        step(j, True)
        lax.fori_loop(j + 1, nb, lambda i, carry: (step(i, False), carry)[1], 0)
        dk_ref[0] = dk_sc[...]
        dv_ref[0] = dv_sc[...].astype(dv_ref.dtype)
        dc_ref[0] = dc_sc[...]

    whole = pl.BlockSpec((1, s, LANES), lambda hh, j: (hh, 0, 0))
    blk = pl.BlockSpec((1, t, LANES), lambda hh, j: (hh, j, 0))
    return pl.pallas_call(
        body, name=name, grid=(h, nb),
        in_specs=[blk, blk, whole, pl.BlockSpec((1, 1, s), lambda hh, j: (hh, 0, 0)), whole, whole],
        out_specs=[whole, blk, blk, pl.BlockSpec((1, 1, t), lambda hh, j: (hh, 0, j))],
        out_shape=[jax.ShapeDtypeStruct((h, s, LANES), F32), jax.ShapeDtypeStruct((h, s, LANES), F32),
                   jax.ShapeDtypeStruct((h, s, LANES), BF16), jax.ShapeDtypeStruct((h, 1, s), F32)],
        scratch_shapes=[pltpu.VMEM((t, LANES), F32), pltpu.VMEM((t, LANES), F32), pltpu.VMEM((1, t), F32)],
        compiler_params=_cparams(("parallel", "arbitrary")),
    )(kp, vp, qp, c, dop, lse)


PAIRS = FOX_HEADS // 2
HALF = LANES // 2


def _first_half():
    return lax.broadcasted_iota(jnp.int32, (1, LANES), 1) < HALF


def _pair_bias(c_ref, hh, qblock, kblock, t):
    lane = lax.broadcasted_iota(jnp.int32, (1, LANES), 1)
    cq = c_ref[hh, :, pl.ds(pl.multiple_of(qblock * t, LANES), LANES)]
    cref = jnp.sum(jnp.where(lane == 0, cq, 0.0), axis=1, keepdims=True)
    return cref - c_ref[hh, :, pl.ds(pl.multiple_of(kblock * t, LANES), t)]


def _fox_dopack(do, o, *, tm=256, name):
    s, d = do.shape
    tm = min(tm, s)

    def body(do_ref, o_ref, out_ref):
        dov = do_ref[...].astype(BF16)
        prod = dov.astype(F32) * o_ref[...]
        r = lax.broadcasted_iota(jnp.int32, (d, LANES), 0)
        c = lax.broadcasted_iota(jnp.int32, (d, LANES), 1)
        heads = ((r >= c * FOX_HEADDIM) & (r < (c + 1) * FOX_HEADDIM)).astype(BF16)
        negd = -_mask_dot(heads, prod, "vn")
        hr = lax.broadcasted_iota(jnp.int32, (LANES, 2 * d), 0)
        col = lax.broadcasted_iota(jnp.int32, (LANES, 2 * d), 1)
        base = (hr >> 1) * (2 * LANES) + jnp.where((hr & 1) == 0, HALF, LANES)
        terms = None
        for kk, part in enumerate(_split3(negd)):
            place = ((col == base + kk) & (hr < FOX_HEADS)).astype(BF16)
            tk = lax.dot_general(part, place, (_DIMS["nn"], ((), ())), preferred_element_type=F32)
            terms = tk if terms is None else terms + tk
        first = _first_half()
        zero = jnp.zeros((tm, LANES), BF16)
        pieces = []
        for hp in range(PAIRS):
            blk = dov[:, hp * LANES:(hp + 1) * LANES]
            pieces += [jnp.where(first, blk, zero), jnp.where(first, zero, blk)]
        out_ref[...] = (jnp.concatenate(pieces, axis=1).astype(F32) + terms).astype(BF16)

    row = pl.BlockSpec((tm, d), lambda i: (i, 0))
    return pl.pallas_call(body, name=name, grid=(s // tm,), in_specs=[row, row],
                          out_specs=pl.BlockSpec((tm, 2 * d), lambda i: (i, 0)),
                          out_shape=jax.ShapeDtypeStruct((s, 2 * d), BF16),
                          compiler_params=_cparams(("parallel",)))(do, o)


FOX_DEAD = 110.0
BOUND_SLACK = 1.001


def _fox_block_bounds(qkv, c):
    s = qkv.shape[0]
    t = min(FOX_BLOCK, s)
    k = qkv[:, D_FOX:2 * D_FOX].astype(F32).reshape(s // t, t, FOX_HEADS, FOX_HEADDIM)
    kn = jnp.sqrt(jnp.max(jnp.sum(k * k, axis=-1), axis=1)).T
    return kn, c[:, 0, ::t], c[:, 0, t - 1::t]


def _fox_bound(qn, kn_ref, cs_ref, ce_ref, h, qblock, kblock):
    return qn * (kn_ref[h, kblock] * BOUND_SLACK) + (cs_ref[h, qblock] - ce_ref[h, kblock])


def _run_live(live, work):
    @pl.when(live[0] & live[1])
    def _():
        work(0)
        work(1)

    @pl.when(live[0] & jnp.logical_not(live[1]))
    def _():
        work(0)

    @pl.when(jnp.logical_not(live[0]) & live[1])
    def _():
        work(1)


def _fox3_fwd(qkv, c, bounds, *, carry=None, name):
    assert qkv.shape[0] // min(FOX_BLOCK, qkv.shape[0]) < LANES - 2
    s = qkv.shape[0]
    t = min(FOX_BLOCK, s)
    nb = s // t
    nt = (((1,), (1,)), ((), ()))
    nn = (((1,), (0,)), ((), ()))
    scale = FOX_HEADDIM ** -0.5

    def body(*refs):
        ins, (o_ref, mrun_ref), (m_sc, acc_sc, mt_sc), comm = _carried(carry, refs, 7, 2, 3)
        q_ref, k_ref, v_ref, c_ref, kn_ref, cs_ref, ce_ref = ins
        hp, i = pl.program_id(0), pl.program_id(1)
        if carry is not None:
            @pl.when((hp == 0) & (i == 0))
            def _():
                carry.start(*comm)

        first = _first_half()
        lane = lax.broadcasted_iota(jnp.int32, (1, LANES), 1)
        m_sc[...] = jnp.full_like(m_sc, NEG)
        acc_sc[...] = jnp.zeros_like(acc_sc)
        mt_sc[...] = jnp.zeros_like(mt_sc)
        q2 = q_ref[...] * scale
        zero = jnp.zeros_like(q2)
        qs = [jnp.where(first, q2, zero), jnp.where(first, zero, q2)]
        qn = []
        for hh in range(2):
            qf = qs[hh].astype(F32)
            qn.append(jnp.broadcast_to(jnp.sqrt(jnp.sum(qf * qf, axis=1, keepdims=True)), (t, LANES)))

        def head_block(hh, j, k2, vxh, masked):
            sc = lax.dot_general(qs[hh], k2, nt, preferred_element_type=F32) + _pair_bias(c_ref, hh, i, j, t)
            if masked:
                sc = jnp.where(_causal(t), sc, NEG)
            m_prev = m_sc[hh]
            m_new = jnp.maximum(m_prev, jnp.max(sc, axis=1, keepdims=True))
            pr = jnp.exp(sc - jnp.tile(m_new, (1, t // LANES))).astype(BF16)
            pv = lax.dot_general(pr, vxh, nn, preferred_element_type=F32)
            acc_sc[hh] = jnp.exp(m_prev - m_new) * acc_sc[hh] + pv
            m_sc[hh] = m_new

        def step(j, masked):
            at = pl.ds(pl.multiple_of(j * t, t), t)
            k2, v2 = k_ref[at, :], v_ref[at, :]
            one = jnp.ones_like(v2)
            vx = [jnp.where(first, v2, one), jnp.where(first, one, v2)]
            if masked:
                for hh in range(2):
                    head_block(hh, j, k2, vx[hh], True)
            else:
                live = [jnp.max(_fox_bound(qn[hh], kn_ref, cs_ref, ce_ref, 2 * hp + hh, i, j) - m_sc[hh]) > -FOX_DEAD
                        for hh in range(2)]
                _run_live(live, lambda hh: head_block(hh, j, k2, vx[hh], False))
            for hh in range(2):
                mt_sc[hh] = jnp.where(lane == j, m_sc[hh], mt_sc[hh])

        step(i, True)
        lax.fori_loop(0, i, lambda n, carry: (step(i - 1 - n, False), carry)[1], 0)
        acc_a, acc_b = acc_sc[0], acc_sc[1]
        den_a = jnp.where(first, pltpu.roll(acc_a, HALF, 1), acc_a)
        den_b = jnp.where(first, acc_b, pltpu.roll(acc_b, HALF, 1))
        o_ref[...] = jnp.where(first, acc_a / den_a, acc_b / den_b)
        for hh, den in enumerate((den_a, den_b)):
            stats = jnp.where(lane == LANES - 1, m_sc[hh] + jnp.log(den), mt_sc[hh])
            mrun_ref[:, hh * LANES:(hh + 1) * LANES] = jnp.where(lane == LANES - 2, qn[hh], stats)
        if carry is not None:
            @pl.when((pl.program_id(0) == PAIRS - 1) & (i == nb - 1))
            def _():
                carry.wait(*comm)

    nq = D_FOX // LANES
    smem = pl.BlockSpec(memory_space=pltpu.SMEM)
    extra = carry if carry is not None else _Exchange([], "c", False)
    return pl.pallas_call(
        body, name=name, grid=(PAIRS, nb),
        in_specs=[pl.BlockSpec((t, LANES), lambda hp, i: (i, hp)),
                  pl.BlockSpec((s, LANES), lambda hp, i: (0, nq + hp)),
                  pl.BlockSpec((s, LANES), lambda hp, i: (0, 2 * nq + hp)),
                  pl.BlockSpec((2, 1, s), lambda hp, i: (hp, 0, 0))] + [smem] * 3 + extra.in_specs,
        out_specs=[pl.BlockSpec((t, LANES), lambda hp, i: (i, hp)),
                   pl.BlockSpec((t, 2 * LANES), lambda hp, i: (i, hp))] + extra.out_specs,
        out_shape=[jax.ShapeDtypeStruct((s, D_FOX), F32), jax.ShapeDtypeStruct((s, 2 * D_FOX), F32)] + extra.out_shape,
        scratch_shapes=[pltpu.VMEM((2, t, LANES), F32)] * 3 + (carry.scratch if carry is not None else []),
        compiler_params=_cparams(("arbitrary", "arbitrary")),
    )(qkv, qkv, qkv, c, *bounds, *extra.arrs)


def _fox3_bwd(qkv, c, bounds, dox, mrun, *, carry=None, name):
    s = qkv.shape[0]
    t = min(FOX_BLOCK, s)
    nb = s // t
    nt = (((1,), (1,)), ((), ()))
    nn = (((1,), (0,)), ((), ()))
    tn = (((0,), (0,)), ((), ()))
    scale = FOX_HEADDIM ** -0.5

    def body(*refs):
        ins, outs, scratch, comm = _carried(carry, refs, 9, 4, 3)
        k_ref, v_ref, q_ref, c_ref, do_ref, mrun_ref, kn_ref, cs_ref, ce_ref = ins
        dq_ref, dk_ref, dv_ref, dc_ref = outs
        dk_sc, dv_sc, dc_sc = scratch
        hp, j = pl.program_id(0), pl.program_id(1)
        if carry is not None:
            @pl.when((hp == 0) & (j == 0))
            def _():
                carry.start(*comm)

        first = _first_half()
        halves = [first, jnp.logical_not(first)]
        lane = lax.broadcasted_iota(jnp.int32, (1, LANES), 1)

        @pl.when(j == 0)
        def _():
            dq_ref[...] = jnp.zeros_like(dq_ref)

        dk_sc[...] = jnp.zeros_like(dk_sc)
        dv_sc[...] = jnp.zeros_like(dv_sc)
        dc_sc[...] = jnp.zeros_like(dc_sc)
        k2, v2 = k_ref[...], v_ref[...]
        one = jnp.ones_like(v2)
        vx = [jnp.where(first, v2, one), jnp.where(first, one, v2)]

        def pick(stats, which):
            return jnp.sum(jnp.where(lane == which, stats, 0.0), axis=1, keepdims=True)

        def head_block(hh, i, at, qsh, stats, masked):
            mine = slice(hh * LANES, (hh + 1) * LANES)
            dov = do_ref[at, mine]
            sc = lax.dot_general(qsh, k2, nt, preferred_element_type=F32) + _pair_bias(c_ref, hh, i, j, t)
            mj = pick(stats, j)
            gain = jnp.broadcast_to(jnp.exp(mj - pick(stats, LANES - 1)), (t, LANES))
            mj = jnp.broadcast_to(mj, (t, LANES))
            pb = jnp.exp(sc - jnp.tile(mj, (1, t // LANES))).astype(BF16)
            if masked:
                pb = jnp.where(_causal(t), pb, jnp.zeros_like(pb))
            pr = pb.astype(F32) * jnp.tile(gain, (1, t // LANES))
            ds = pr * lax.dot_general(dov, vx[hh], nt, preferred_element_type=F32)
            dc_sc[hh] -= jnp.sum(ds, axis=0, keepdims=True)
            dsb = ds.astype(BF16)
            dvh = lax.dot_general(pr.astype(BF16), dov, tn, preferred_element_type=F32)
            dv_sc[...] += jnp.where(halves[hh], dvh, 0.0)
            dk_sc[...] += lax.dot_general(dsb, qsh, tn, preferred_element_type=F32)
            dqh = lax.dot_general(dsb, k2, nn, preferred_element_type=F32)
            dq_ref[at, :] += jnp.where(halves[hh], dqh, 0.0)

        def step(i, masked):
            at = pl.ds(pl.multiple_of(i * t, t), t)
            q2 = q_ref[at, :] * scale
            zero = jnp.zeros_like(q2)
            qs = [jnp.where(first, q2, zero), jnp.where(first, zero, q2)]
            stats = [mrun_ref[at, hh * LANES:(hh + 1) * LANES] for hh in range(2)]
            if masked:
                for hh in range(2):
                    head_block(hh, i, at, qs[hh], stats[hh], True)
            else:
                live = [jnp.max(_fox_bound(pick(stats[hh], LANES - 2), kn_ref, cs_ref, ce_ref, 2 * hp + hh, i, j)
                                - pick(stats[hh], j + 1)) > -FOX_DEAD for hh in range(2)]
                _run_live(live, lambda hh: head_block(hh, i, at, qs[hh], stats[hh], False))

        step(j, True)
        lax.fori_loop(j + 1, nb, lambda i, carry: (step(i, False), carry)[1], 0)
        dk_ref[...] = dk_sc[...]
        dv_ref[...] = dv_sc[...].astype(dv_ref.dtype)
        dc_ref[...] = dc_sc[...]
        if carry is not None:
            @pl.when((pl.program_id(0) == PAIRS - 1) & (j == nb - 1))
            def _():
                carry.wait(*comm)

    nq = D_FOX // LANES
    blk = pl.BlockSpec((t, LANES), lambda hp, j: (j, hp))
    smem = pl.BlockSpec(memory_space=pltpu.SMEM)
    extra = carry if carry is not None else _Exchange([], "c", False)
    return pl.pallas_call(
        body, name=name, grid=(PAIRS, nb),
        in_specs=[pl.BlockSpec((t, LANES), lambda hp, j: (j, nq + hp)),
                  pl.BlockSpec((t, LANES), lambda hp, j: (j, 2 * nq + hp)),
                  pl.BlockSpec((s, LANES), lambda hp, j: (0, hp)),
                  pl.BlockSpec((2, 1, s), lambda hp, j: (hp, 0, 0)),
                  pl.BlockSpec((s, 2 * LANES), lambda hp, j: (0, hp)),
                  pl.BlockSpec((s, 2 * LANES), lambda hp, j: (0, hp))] + [smem] * 3 + extra.in_specs,
        out_specs=[pl.BlockSpec((s, LANES), lambda hp, j: (0, hp)), blk, blk,
                   pl.BlockSpec((2, 1, t), lambda hp, j: (hp, 0, j))] + extra.out_specs,
        out_shape=[jax.ShapeDtypeStruct((s, D_FOX), F32), jax.ShapeDtypeStruct((s, D_FOX), F32),
                   jax.ShapeDtypeStruct((s, D_FOX), BF16), jax.ShapeDtypeStruct((FOX_HEADS, 1, s), F32)]
        + extra.out_shape,
        scratch_shapes=[pltpu.VMEM((t, LANES), F32), pltpu.VMEM((t, LANES), F32), pltpu.VMEM((2, 1, t), F32)]
        + (carry.scratch if carry is not None else []),
        compiler_params=_cparams(("arbitrary", "arbitrary")),
    )(qkv, qkv, qkv, c, dox, mrun, *bounds, *extra.arrs)


def _final(x1, out1, g, tgt, *, tm=256, name):
    s, d = x1.shape
    tm = min(tm, s)

    def body(x_ref, o_ref, g_ref, t_ref, dx_ref, do_ref, dg_ref, loss_ref):
        i = pl.program_id(0)

        @pl.when(i == 0)
        def _():
            dg_ref[...] = jnp.zeros_like(dg_ref)
            loss_ref[...] = jnp.zeros_like(loss_ref)

        tv = t_ref[...]

        def lossf(xv, ov, gv):
            err = jnp.square(xv + _rms(ov, gv) - tv)
            return 0.5 * jnp.sum(jnp.mean(err, axis=-1, keepdims=True), axis=0, keepdims=True)

        val, vjp = jax.vjp(lossf, x_ref[...], o_ref[...], g_ref[...])
        dx, do, dg = vjp(jnp.ones((1, 1), F32))
        dx_ref[...] = dx
        do_ref[...] = do.astype(do_ref.dtype)
        dg_ref[...] += dg
        loss_ref[...] += val

    row = pl.BlockSpec((tm, d), lambda i: (i, 0))
    par = pl.BlockSpec((1, d), lambda i: (0, 0))
    return pl.pallas_call(
        body, name=name, grid=(s // tm,), in_specs=[row, row, par, row],
        out_specs=[row, row, par, pl.BlockSpec((1, 1), lambda i: (0, 0))],
        out_shape=[jax.ShapeDtypeStruct((s, d), F32), jax.ShapeDtypeStruct((s, d), BF16),
                   jax.ShapeDtypeStruct((1, d), F32), jax.ShapeDtypeStruct((1, 1), F32)],
        compiler_params=_cparams(("arbitrary",)),
    )(x1, out1, g, tgt)


def _row_tile(r):
    return LANES if r % LANES == 0 else r


def _sum_slots(parts, *, out_dtype=F32, name):
    p, r, c = parts.shape
    tr = _row_tile(r)

    def body(p_ref, o_ref):
        acc = p_ref[0].astype(F32)
        for k in range(1, p):
            acc = acc + p_ref[k].astype(F32)
        o_ref[...] = acc.astype(o_ref.dtype)

    return pl.pallas_call(
        body, name=name, grid=(r // tr,),
        in_specs=[pl.BlockSpec((p, tr, c), lambda i: (0, i, 0))],
        out_specs=pl.BlockSpec((tr, c), lambda i: (i, 0)),
        out_shape=jax.ShapeDtypeStruct((r, c), out_dtype),
        compiler_params=_cparams(("parallel",)),
    )(parts)


def _adamw(w, gparts, m, v, *, name):
    r, c = w.shape
    p = gparts.shape[0]
    tr = _row_tile(r)

    def body(w_ref, g_ref, m_ref, v_ref, go_ref, d_ref, mo_ref, vo_ref):
        g = g_ref[0].astype(F32)
        for k in range(1, p):
            g = g + g_ref[k].astype(F32)
        mn = ADAM_B1 * m_ref[...] + (1.0 - ADAM_B1) * g
        vn = ADAM_B2 * v_ref[...] + (1.0 - ADAM_B2) * jnp.square(g)
        m_hat = mn / (1.0 - ADAM_B1 ** ADAM_STEP)
        v_hat = vn / (1.0 - ADAM_B2 ** ADAM_STEP)
        go_ref[...] = g
        d_ref[...] = -ADAM_LR * (m_hat / (jnp.sqrt(v_hat) + ADAM_EPS) + ADAM_WD * w_ref[...])
        mo_ref[...] = mn
        vo_ref[...] = vn

    spec = pl.BlockSpec((tr, c), lambda i: (i, 0))
    return pl.pallas_call(
        body, name=name, grid=(r // tr,),
        in_specs=[spec, pl.BlockSpec((p, tr, c), lambda i: (0, i, 0)), spec, spec],
        out_specs=[spec] * 4, out_shape=[jax.ShapeDtypeStruct((r, c), F32)] * 4,
        compiler_params=_cparams(("parallel",)),
    )(w, gparts, m, v)


_FLIPS = {
    "xy": [(1, 0, 0), (0, 1, 0), (1, 1, 0)],
    "c": [(0, 0, 1)],
    "xyc": [(fx, fy, fc) for fx in (0, 1) for fy in (0, 1) for fc in (0, 1) if (fx, fy, fc) != (0, 0, 0)],
}


def _slot(mode, px, py, pc):
    return {"xy": 2 * px + py, "c": pc, "xyc": 4 * px + 2 * py + pc}[mode]


class _Exchange:
    def __init__(self, arrs, mode, scatter):
        self.arrs, self.mode, self.scatter = list(arrs), mode, scatter
        self.n = len(self.arrs)
        self.flips = _FLIPS[mode]
        nf = len(self.flips)
        anyspec = pl.BlockSpec(memory_space=pl.ANY)
        self.in_specs = [anyspec] * self.n
        self.out_specs = [anyspec] * self.n
        self.out_shape = [jax.ShapeDtypeStruct((nf + 1,) + (a.shape[1:] if scatter else a.shape), a.dtype)
                          for a in self.arrs]
        self.scratch = [pltpu.SemaphoreType.DMA((self.n * nf,)), pltpu.SemaphoreType.DMA((self.n * nf,)),
                        pltpu.SemaphoreType.DMA((self.n,))]

    def _copies(self, ins, outs, sems, arrivals=True):
        send, recv, loc = sems
        nf = len(self.flips)
        x, y, c = lax.axis_index("x"), lax.axis_index("y"), lax.axis_index("c")
        me = _slot(self.mode, x, y, c)
        peers = [(x ^ fx, y ^ fy, c ^ fc) for (fx, fy, fc) in self.flips]

        def src(a, slot):
            return ins[a].at[slot] if self.scatter else ins[a]

        def copy(a, j, dst_slot):
            return pltpu.make_async_remote_copy(
                src_ref=src(a, _slot(self.mode, *peers[j])), dst_ref=outs[a].at[dst_slot],
                send_sem=send.at[a * nf + j], recv_sem=recv.at[a * nf + j], device_id=peers[j], device_id_type=MESH)

        pairs = [(a, j) for a in range(self.n) for j in range(nf)]
        local = [pltpu.make_async_copy(src(a, me), outs[a].at[me], loc.at[a]) for a in range(self.n)]
        sends = [copy(a, j, me) for a, j in pairs]
        recvs = [copy(a, j, _slot(self.mode, *peers[j])) for a, j in pairs] if arrivals else []
        return local, sends, recvs

    def start(self, ins, outs, sems):
        local, sends, _ = self._copies(ins, outs, sems, arrivals=False)
        for cp in local + sends:
            cp.start()

    def wait(self, ins, outs, sems):
        local, sends, recvs = self._copies(ins, outs, sems)
        for cp in recvs:
            cp.wait_recv()
        for cp in sends:
            cp.wait_send()
        for cp in local:
            cp.wait()


def _carried(carry, refs, n_in, n_out, n_scratch):
    k = carry.n if carry is not None else 0
    ins, refs = refs[:n_in], refs[n_in:]
    cin, refs = refs[:k], refs[k:]
    outs, refs = refs[:n_out], refs[n_out:]
    cout, refs = refs[:k], refs[k:]
    scratch, sems = refs[:n_scratch], refs[n_scratch:]
    return ins, outs, scratch, (cin, cout, sems)


def _exchanges(exs, *, name):
    counts = [ex.n for ex in exs]
    total = sum(counts)

    def body(*refs):
        ins, outs, sems = refs[:total], refs[total:2 * total], refs[2 * total:]
        comms, at = [], 0
        for k, ex in enumerate(exs):
            comms.append((ins[at:at + ex.n], outs[at:at + ex.n], sems[3 * k:3 * k + 3]))
            at += ex.n
        for ex, comm in zip(exs, comms):
            ex.start(*comm)
        for ex, comm in zip(exs, comms):
            ex.wait(*comm)

    res = pl.pallas_call(
        body, name=name, in_specs=[sp for ex in exs for sp in ex.in_specs],
        out_specs=[sp for ex in exs for sp in ex.out_specs], out_shape=[sh for ex in exs for sh in ex.out_shape],
        scratch_shapes=[sc for ex in exs for sc in ex.scratch])(*[a for ex in exs for a in ex.arrs])
    out, at = [], 0
    for n in counts:
        out.append(list(res[at:at + n]))
        at += n
    return out


def _exchange(arrs, mode, scatter, *, name):
    return _exchanges([_Exchange(arrs, mode, scatter)], name=name)[0]


def _softplus(v):
    return jnp.maximum(v, 0.0) + jnp.log1p(jnp.exp(-jnp.abs(v)))


def _pad_lanes(v):
    r, n = v.shape
    return jnp.pad(v, ((0, 0), (0, -n % LANES)))


def _to_heads(v):
    s = v.shape[0]
    return v.reshape(s, -1, 64).transpose(1, 0, 2)


def _from_heads(v):
    h, s, p = v.shape
    return v.transpose(1, 0, 2).reshape(s, h * p)


def _fn_rms(v, g):
    return (_rms(v, g),)


def _fn_post(xv, ov, g):
    return (xv + _rms(ov, g),)


def _fn_act(xbc, dtp, fp, dtb, fb):
    return _silu(xbc), _softplus(dtp + dtb), -_softplus(-(fp + fb))


def _fn_mix(y, zs, o, zf, g):
    yg = y * _silu(zs)
    sq = yg * yg
    lane = lax.broadcasted_iota(jnp.int32, (1, D_SSD), 1)
    width = D_SSD // SSD_GROUPS
    rstd = jnp.zeros_like(yg)
    for gi in range(SSD_GROUPS):
        msk = ((lane >= gi * width) & (lane < (gi + 1) * width)).astype(F32)
        ms = jnp.sum(sq * msk, axis=1, keepdims=True) / width
        rstd = rstd + lax.rsqrt(ms + EPS) * msk
    return (jnp.concatenate([yg * rstd * g, o * _silu(zf)], axis=1),)


def _fn_glu(val, gate):
    return (val * jax.nn.sigmoid(gate),)


def _fn_ln(hc, z, g, b):
    mu = jnp.mean(hc, axis=-1, keepdims=True)
    xc = hc - mu
    yn = xc * lax.rsqrt(jnp.mean(xc * xc, axis=-1, keepdims=True) + EPS) * g + b
    return (_silu(yn) * _silu(z),)


class _NoComm:
    def odd_weights(self):
        return None

    def got_odd_weights(self, got, w):
        pass

    def early_grads(self, g):
        return None

    def got_early_grads(self, got):
        pass

    def early_sums(self):
        return None

    def got_early_sums(self, got):
        pass


def _local_step(x, tgt, w, comm=None):
    comm = comm or _NoComm()
    s = x.shape[0]
    d = D_MODEL
    tm = 256
    bf = lambda v: v.astype(BF16)
    c1 = lambda arr: _col(arr, 0, arr.shape[1])
    g = {}

    ew = w["e_w_in"]
    w_z, w_xbc = bf(ew[:, 0:2048]), bf(ew[:, 2048:4096])
    w_dt = bf(_pad_lanes(ew[:, 4096:4112]))
    w_qkv = bf(ew[:, 4112:7184])
    w_f = bf(_pad_lanes(ew[:, 7184:7200]))
    w_eo = bf(w["e_w_out"])
    dtb, fgb = _pad_lanes(w["e_dt_bias"]), _pad_lanes(w["e_fgate_b"])
    alog, dsk = _pad_lanes(w["e_a_log"]), _pad_lanes(w["e_d_skip"])

    (u0,) = _rowwise_fwd(_fn_rms, [c1(x)], [c1(w["e_norm_pre"])], [(d, BF16)], tm=tm, name="e_pre")
    z = _mm(u0, w_z, name="e_in_z")
    xbc_raw = _mm(u0, w_xbc, out_dtype=BF16, name="e_in_xbc")
    qkv = _mm(u0, w_qkv, out_dtype=BF16, name="e_in_qkv")
    dtp = _mm(u0, w_dt, name="e_in_dt")
    fp = _mm(u0, w_f, name="e_in_f")
    xbc_pre = _conv_fwd(xbc_raw, w["e_conv_w"], w["e_conv_b"], name="e_conv")
    act_rows = [c1(xbc_pre), c1(dtp), c1(fp)]
    act_pars = [c1(dtb), c1(fgb)]
    xbc, dt, lf = _rowwise_fwd(_fn_act, act_rows, act_pars, [(2048, F32), (LANES, F32), (LANES, F32)],
                               tm=tm, name="e_act")
    y, hsave = _ssd2_fwd(dt, xbc, alog, dsk, name="e_ssd")
    csum = _cumsum_lanes(lf[:, :FOX_HEADS].T, reverse=False, name="e_cumsum").reshape(FOX_HEADS, 1, s)
    bounds = _fox_block_bounds(qkv, csum)
    o, mrun, *got = _fox3_fwd(qkv, csum, bounds, carry=comm.odd_weights(), name="e_fox")
    comm.got_odd_weights(got, w)
    w_oi, w_oo = bf(w["o_w_in"]), bf(w["o_w_out"])
    mix_rows = [c1(y), _col(z, 0, D_SSD), c1(o), _col(z, 1, D_FOX)]
    mix_pars = [c1(w["e_ssd_norm"])]
    (hmix,) = _rowwise_fwd(_fn_mix, mix_rows, mix_pars, [(2048, BF16)], tm=tm, name="e_mix")
    out0 = _mm(hmix, w_eo, name="e_out")
    post_rows = [c1(x), c1(out0)]
    (x1,) = _rowwise_fwd(_fn_post, post_rows, [c1(w["e_norm_post"])], [(d, F32)], tm=tm, name="e_post")

    (u1,) = _rowwise_fwd(_fn_rms, [c1(x1)], [c1(w["o_norm_pre"])], [(d, BF16)], tm=tm, name="o_pre")
    p1 = _mm(u1, w_oi, out_dtype=BF16, name="o_in")
    glu_rows = [_col(p1, 0, D_CONV), _col(p1, 1, D_CONV)]
    (hg,) = _rowwise_fwd(_fn_glu, glu_rows, [], [(D_CONV, BF16)], tm=tm, name="o_glu")
    hc = _conv_fwd(hg, w["o_conv_w"], w["o_conv_b"], name="o_conv")
    ln_rows = [c1(hc), _col(p1, 2, D_CONV)]
    ln_pars = [c1(w["o_ln_g"]), c1(w["o_ln_b"])]
    (h2,) = _rowwise_fwd(_fn_ln, ln_rows, ln_pars, [(D_CONV, BF16)], tm=tm, name="o_ln")
    out1 = _mm(h2, w_oo, name="o_out")

    dx2, dout1, g["o_norm_post"], loss = _final(x1, out1, w["o_norm_post"], tgt, name="loss_head")
    dh2 = _mm(dout1, w_oo, tb=True, name="o_out_dx")
    g["o_w_out"] = _mm(h2, dout1, ta=True, name="o_out_dw")
    (dhc, dz1), (g["o_ln_g"], g["o_ln_b"]) = _rowwise_bwd(_fn_ln, ln_rows, ln_pars, [c1(dh2)], [F32, BF16],
                                                         tm=tm, name="o_ln_bwd")
    dhg = _conv_bwd_x(dhc, w["o_conv_w"], name="o_conv_dx")
    g["o_conv_w"], g["o_conv_b"] = _conv_bwd_w(hg, dhc, CONV_WIDTH, name="o_conv_dw")
    (dval, dgate), _ = _rowwise_bwd(_fn_glu, glu_rows, [], [c1(dhg)], [BF16, BF16], tm=tm, name="o_glu_bwd")
    du1 = _mm(dval, w_oi[:, 0:2048], tb=True, name="o_in_dx0")
    du1 = _mm(dgate, w_oi[:, 2048:4096], tb=True, add=du1, name="o_in_dx1")
    du1 = _mm(dz1, w_oi[:, 4096:6144], tb=True, add=du1, name="o_in_dx2")
    g["o_w_in"] = jnp.concatenate([_mm(u1, dval, ta=True, name="o_in_dw0"), _mm(u1, dgate, ta=True, name="o_in_dw1"),
                                   _mm(u1, dz1, ta=True, name="o_in_dw2")], axis=1)
    (dx1,), (g["o_norm_pre"],) = _rowwise_bwd(_fn_rms, [c1(x1)], [c1(w["o_norm_pre"])], [c1(du1)], [F32],
                                              adds={0: c1(dx2)}, tm=tm, name="o_pre_bwd")

    (dout0,), (g["e_norm_post"],) = _rowwise_bwd(_fn_post, post_rows, [c1(w["e_norm_post"])], [c1(dx1)],
                                                 [None, BF16], tm=tm, name="e_post_bwd")
    dhmix = _mm(dout0, w_eo, tb=True, name="e_out_dx")
    g["e_w_out"] = _mm(hmix, dout0, ta=True, name="e_out_dw")
    (dy, dzs, do, dzf), (g["e_ssd_norm"],) = _rowwise_bwd(_fn_mix, mix_rows, mix_pars, [c1(dhmix)],
                                                        [F32, BF16, F32, BF16], tm=tm, name="e_mix_bwd")
    dox = _fox_dopack(do, o, name="e_fox_dopack")
    dq8, dk, dv, dcs, *got = _fox3_bwd(qkv, csum, bounds, dox, mrun, carry=comm.early_grads(g), name="e_fox_bwd")
    comm.got_early_grads(got)
    dlf = _pad_lanes(_cumsum_lanes(dcs.reshape(FOX_HEADS, s), reverse=True, name="e_cumsum_bwd").T)
    dxs, ddt, dbm, dcm, dalog, ddsk, *got = _ssd2_bwd(dt, xbc, alog, dsk, hsave, dy, carry=comm.early_sums(),
                                                       name="e_ssd_bwd")
    comm.got_early_sums(got)
    dxbc = jnp.concatenate([dxs, dbm, dcm], axis=1)
    (dxbc_pre, ddtp, dfp), (ddtb, dfgb) = _rowwise_bwd(_fn_act, act_rows, act_pars, [c1(dxbc), c1(ddt), c1(dlf)],
                                                      [F32, BF16, BF16], tm=tm, name="e_act_bwd")
    dxbc_raw = bf(_conv_bwd_x(dxbc_pre, w["e_conv_w"], name="e_conv_dx"))
    g["e_conv_w"], g["e_conv_b"] = _conv_bwd_w(xbc_raw, dxbc_pre, SSD_CONV, name="e_conv_dw")
    du0 = _mm(dzs, w_z[:, :D_SSD], tb=True, name="e_in_dx0")
    du0 = _mm(dzf, w_z[:, D_SSD:], tb=True, add=du0, name="e_in_dx1")
    du0 = _mm(dxbc_raw, w_xbc, tb=True, add=du0, name="e_in_dx2")
    eighth = FOX_HEADDIM ** -0.5
    du0 = _mm(dq8, w_qkv[:, :D_FOX] * eighth, tb=True, add=du0, name="e_in_dx3q")
    du0 = _mm(dk, w_qkv[:, D_FOX:2 * D_FOX], tb=True, add=du0, name="e_in_dx3k")
    du0 = _mm(dv, w_qkv[:, 2 * D_FOX:], tb=True, add=du0, name="e_in_dx3v")
    du0 = _mm(ddtp, w_dt, tb=True, add=du0, name="e_in_dx4")
    du0 = _mm(dfp, w_f, tb=True, add=du0, name="e_in_dx5")
    g["e_w_in"] = jnp.concatenate([
        _mm(u0, dzs, ta=True, name="e_in_dw0"), _mm(u0, dzf, ta=True, name="e_in_dw1"),
        _mm(u0, dxbc_raw, ta=True, name="e_in_dw2"), _mm(u0, ddtp, ta=True, name="e_in_dw3")[:, :SSD_HEADS],
        _mm(u0, dq8, ta=True, name="e_in_dw4q") * eighth, _mm(u0, dk, ta=True, name="e_in_dw4k"),
        _mm(u0, dv, ta=True, name="e_in_dw4v"), _mm(u0, dfp, ta=True, name="e_in_dw5")[:, :FOX_HEADS]], axis=1)
    (dx,), (g["e_norm_pre"],) = _rowwise_bwd(_fn_rms, [c1(x)], [c1(w["e_norm_pre"])], [c1(du0)], [F32],
                                             adds={0: c1(dx1)}, tm=tm, name="e_pre_bwd")
    g["e_dt_bias"], g["e_fgate_b"] = ddtb[:, :SSD_HEADS], dfgb[:, :FOX_HEADS]
    g["e_a_log"], g["e_d_skip"] = dalog[:, :SSD_HEADS], ddsk[:, :SSD_HEADS]
    return loss, dx, g


_WEIGHTS = ["e_norm_pre", "e_w_in", "e_conv_w", "e_conv_b", "e_dt_bias", "e_a_log", "e_d_skip", "e_fgate_b",
            "e_ssd_norm", "e_w_out", "e_norm_post", "o_norm_pre", "o_w_in", "o_conv_w", "o_conv_b", "o_ln_g",
            "o_ln_b", "o_w_out", "o_norm_post"]
_BIG = ["e_w_in", "e_w_out", "o_w_in", "o_w_out"]
_ROW_SHARDED = ["e_w_out", "o_w_out"]
_SMALL_SHARDED = ["e_conv_w", "o_norm_pre", "o_conv_w", "o_conv_b", "o_ln_g", "o_ln_b", "o_norm_post"]
_REPLICATED = ["e_norm_pre", "e_conv_b", "e_dt_bias", "e_a_log", "e_d_skip", "e_fgate_b", "e_ssd_norm", "e_norm_post"]
_SMALL = [n for n in _WEIGHTS if n not in _BIG]
_EVEN_SHARDED = ["e_w_in", "e_w_out", "e_conv_w"]
_ODD_SHARDED = ["o_w_in", "o_w_out", "o_norm_pre", "o_conv_w", "o_conv_b", "o_ln_g", "o_ln_b", "o_norm_post"]
_EARLY_GRADS = ["o_w_in", "o_w_out", "e_w_out"]
N_CHIPS = 4


def _join(gathered, rows):
    k, r, c = gathered.shape
    return gathered.reshape(k * r, c) if rows else gathered.transpose(1, 0, 2).reshape(r, k * c)


def _split(full, rows):
    r, c = full.shape
    return full.reshape(N_CHIPS, r // N_CHIPS, c) if rows else full.reshape(r, N_CHIPS, c // N_CHIPS).transpose(1, 0, 2)


def kernel(x, e_norm_pre, e_w_in, e_conv_w, e_conv_b, e_dt_bias, e_a_log, e_d_skip, e_fgate_b, e_ssd_norm, e_w_out, e_norm_post, o_norm_pre, o_w_in, o_conv_w, o_conv_b, o_ln_g, o_ln_b, o_w_out, o_norm_post, loss_target, m_e_norm_pre, m_e_w_in, m_e_conv_w, m_e_conv_b, m_e_dt_bias, m_e_a_log, m_e_d_skip, m_e_fgate_b, m_e_ssd_norm, m_e_w_out, m_e_norm_post, m_o_norm_pre, m_o_w_in, m_o_conv_w, m_o_conv_b, m_o_ln_g, m_o_ln_b, m_o_w_out, m_o_norm_post, v_e_norm_pre, v_e_w_in, v_e_conv_w, v_e_conv_b, v_e_dt_bias, v_e_a_log, v_e_d_skip, v_e_fgate_b, v_e_ssd_norm, v_e_w_out, v_e_norm_post, v_o_norm_pre, v_o_w_in, v_o_conv_w, v_o_conv_b, v_o_ln_g, v_o_ln_b, v_o_w_out, v_o_norm_post):
    wvals = (e_norm_pre, e_w_in, e_conv_w, e_conv_b, e_dt_bias, e_a_log, e_d_skip, e_fgate_b, e_ssd_norm, e_w_out,
             e_norm_post, o_norm_pre, o_w_in, o_conv_w, o_conv_b, o_ln_g, o_ln_b, o_w_out, o_norm_post)
    mvals = (m_e_norm_pre, m_e_w_in, m_e_conv_w, m_e_conv_b, m_e_dt_bias, m_e_a_log, m_e_d_skip, m_e_fgate_b,
             m_e_ssd_norm, m_e_w_out, m_e_norm_post, m_o_norm_pre, m_o_w_in, m_o_conv_w, m_o_conv_b, m_o_ln_g,
             m_o_ln_b, m_o_w_out, m_o_norm_post)
    vvals = (v_e_norm_pre, v_e_w_in, v_e_conv_w, v_e_conv_b, v_e_dt_bias, v_e_a_log, v_e_d_skip, v_e_fgate_b,
             v_e_ssd_norm, v_e_w_out, v_e_norm_post, v_o_norm_pre, v_o_w_in, v_o_conv_w, v_o_conv_b, v_o_ln_g,
             v_o_ln_b, v_o_w_out, v_o_norm_post)

    def mat(v):
        return v.reshape(v.shape[-2:]) if v.ndim == 3 else v

    w = {n: mat(v) for n, v in zip(_WEIGHTS, wvals)}
    m = {n: mat(v) for n, v in zip(_WEIGHTS, mvals)}
    v2 = {n: mat(v) for n, v in zip(_WEIGHTS, vvals)}
    me_xy = 2 * lax.axis_index("x") + lax.axis_index("y")

    def shard(n):
        return w[n].astype(BF16) if n in _BIG else w[n]

    gathered = _exchange([shard(n) for n in _EVEN_SHARDED], "xy", False, name="gather_weights")
    full = {n: w[n] for n in _REPLICATED}
    for n, gth in zip(_EVEN_SHARDED, gathered):
        full[n] = _join(gth, n in _ROW_SHARDED)
    gparts = {}

    class _StepComm(_NoComm):
        def odd_weights(self):
            return _Exchange([shard(n) for n in _ODD_SHARDED], "xy", False)

        def got_odd_weights(self, got, wdict):
            for n, gth in zip(_ODD_SHARDED, got):
                wdict[n] = _join(gth, n in _ROW_SHARDED)

        def early_grads(self, g):
            return _Exchange([_split(g[n], n in _ROW_SHARDED).astype(BF16) for n in _EARLY_GRADS], "xy", True)

        def got_early_grads(self, got):
            self.sums = [_sum_slots(p, out_dtype=BF16, name="sum_" + n) for n, p in zip(_EARLY_GRADS, got)]

        def early_sums(self):
            return _Exchange(self.sums, "c", False)

        def got_early_sums(self, got):
            gparts.update(zip(_EARLY_GRADS, got))

    loss, dx, g = _local_step(x[0], loss_target[0], full, _StepComm())
    loss = lax.psum(loss[0, 0], ("x", "y", "c"))

    flat = jnp.concatenate([_pad_lanes(g[n].reshape(1, -1)) for n in _SMALL], axis=1).reshape(-1, LANES)
    (scattered,), (all8,) = _exchanges([_Exchange([_split(g["e_w_in"], False).astype(BF16)], "xy", True),
                                        _Exchange([flat], "xyc", False)], name="scatter_grads")
    (gparts["e_w_in"],) = _exchange([_sum_slots(scattered, out_dtype=BF16, name="sum_e_w_in")], "c", False,
                                    name="pair_grads")
    total = _sum_slots(all8, name="sum_small").reshape(1, -1)
    at = 0
    for n in _SMALL:
        size = g[n].size
        gn = total[:, at:at + size].reshape(g[n].shape)
        at += size + (-size % LANES)
        if n in _SMALL_SHARDED:
            cols = gn.shape[1] // N_CHIPS
            gn = lax.dynamic_slice(gn, (0, me_xy * cols), (gn.shape[0], cols))
        gparts[n] = gn[None]

    grads, deltas, new_m, new_v = [], [], [], []
    for n, orig in zip(_WEIGHTS, wvals):
        gn, dn, mn, vn = _adamw(w[n], gparts[n], m[n], v2[n], name="adamw_" + n)
        for lst, val in zip((grads, deltas, new_m, new_v), (gn, dn, mn, vn)):
            lst.append(val.reshape(orig.shape))
    return (loss, dx[None], *grads, *deltas, *new_m, *new_v)
```

```python
import functools

import jax
import jax.numpy as jnp
from jax import lax
from jax.experimental import pallas as pl
from jax.experimental.pallas import tpu as pltpu

F32 = jnp.float32
BF16 = jnp.bfloat16
MESH = pl.DeviceIdType.MESH

D_MODEL = 1024
D_SSD = 1024
SSD_HEADS = 16
SSD_HEADDIM = 64
SSD_GROUPS = 4
SSD_HPG = 4
D_STATE = 128
SSD_CONV = 4
CHUNK = 128
D_FOX = 1024
FOX_HEADS = 16
FOX_HEADDIM = 64
D_CONV = 2048
CONV_WIDTH = 31
EPS = 1e-6
LANES = 128
VMEM_LIMIT = 56 * 1024 * 1024

ADAM_LR = 0.001
ADAM_B1 = 0.9
ADAM_B2 = 0.999
ADAM_EPS = 1e-08
ADAM_WD = 0.01
ADAM_STEP = 10


def _cparams(sem=None):
    return pltpu.CompilerParams(dimension_semantics=sem, vmem_limit_bytes=VMEM_LIMIT)


def _mm(a, b, *, ta=False, tb=False, add=None, out_dtype=F32, tm=1024, tn=None, tk=2048, name):
    m = a.shape[1] if ta else a.shape[0]
    k = a.shape[0] if ta else a.shape[1]
    n = b.shape[0] if tb else b.shape[1]
    if tn is None:
        tn = 1024
    tm, tn = min(tm, m), min(tn, n)
    tk = max(t for t in range(LANES, min(tk, k) + 1, LANES) if k % t == 0)
    assert m % tm == 0 and n % tn == 0 and k % tk == 0, (m, n, k, tm, tn, tk)
    nk = k // tk
    dims = (((0 if ta else 1,), (1 if tb else 0,)), ((), ()))

    def body(*refs):
        if add is None:
            a_ref, b_ref, o_ref = refs[:3]
            c_ref = None
        else:
            a_ref, b_ref, c_ref, o_ref = refs[:4]
        kk = pl.program_id(2)
        prod = lax.dot_general(a_ref[...].astype(BF16), b_ref[...].astype(BF16), dims, preferred_element_type=F32)
        if nk == 1:
            o_ref[...] = (prod if c_ref is None else prod + c_ref[...].astype(F32)).astype(o_ref.dtype)
            return
        acc_ref = refs[-1]

        @pl.when(kk == 0)
        def _():
            acc_ref[...] = prod if c_ref is None else prod + c_ref[...].astype(F32)

        @pl.when((kk > 0) & (kk < nk - 1))
        def _():
            acc_ref[...] += prod

        @pl.when(kk == nk - 1)
        def _():
            o_ref[...] = (acc_ref[...] + prod).astype(o_ref.dtype)

    a_spec = (pl.BlockSpec((tk, tm), lambda j, i, kk: (kk, i)) if ta
              else pl.BlockSpec((tm, tk), lambda j, i, kk: (i, kk)))
    b_spec = (pl.BlockSpec((tn, tk), lambda j, i, kk: (j, kk)) if tb
              else pl.BlockSpec((tk, tn), lambda j, i, kk: (kk, j)))
    o_spec = pl.BlockSpec((tm, tn), lambda j, i, kk: (i, j))
    in_specs, args = [a_spec, b_spec], [a, b]
    if add is not None:
        in_specs.append(o_spec)
        args.append(add)
    return pl.pallas_call(
        body, name=name, grid=(n // tn, m // tm, nk),
        in_specs=in_specs, out_specs=o_spec,
        out_shape=jax.ShapeDtypeStruct((m, n), out_dtype),
        scratch_shapes=[pltpu.VMEM((tm, tn), F32)] if nk > 1 else [],
        compiler_params=_cparams(("parallel", "parallel", "arbitrary")),
    )(*args)


def _col(arr, cb, width):
    return (arr, cb, width)


def _row_specs(ops, tm):
    return [pl.BlockSpec((tm, w), lambda i, cb=cb: (i, cb)) for (_, cb, w) in ops]


def _par_specs(ops):
    return [pl.BlockSpec((a.shape[0], w), lambda i, cb=cb: (0, cb)) for (a, cb, w) in ops]


def _rowwise_fwd(fn, rows, params, outs, *, tm, name):
    s = rows[0][0].shape[0]
    tm = min(tm, s)
    nr, npar = len(rows), len(params)

    def body(*refs):
        rv = [r[...].astype(F32) for r in refs[:nr]]
        pv = [p[...].astype(F32) for p in refs[nr:nr + npar]]
        res = fn(*rv, *pv)
        for o_ref, val in zip(refs[nr + npar:], res):
            o_ref[...] = val.astype(o_ref.dtype)

    return pl.pallas_call(
        body, name=name, grid=(s // tm,),
        in_specs=_row_specs(rows, tm) + _par_specs(params),
        out_specs=[pl.BlockSpec((tm, w), lambda i: (i, 0)) for (w, _) in outs],
        out_shape=[jax.ShapeDtypeStruct((s, w), dt) for (w, dt) in outs],
        compiler_params=_cparams(("parallel",)),
    )(*[r[0] for r in rows], *[p[0] for p in params])


def _rowwise_bwd(fn, rows, params, couts, row_grads, *, adds=None, tm, name):
    adds = adds or {}
    s = rows[0][0].shape[0]
    tm = min(tm, s)
    nr, npar, nc = len(rows), len(params), len(couts)
    add_keys = sorted(adds)
    want = [i for i, dt in enumerate(row_grads) if dt is not None]

    def body(*refs):
        i = pl.program_id(0)
        rv = [r[...].astype(F32) for r in refs[:nr]]
        pv = [p[...].astype(F32) for p in refs[nr:nr + npar]]
        cv = [c[...].astype(F32) for c in refs[nr + npar:nr + npar + nc]]
        av = {k: r[...].astype(F32) for k, r in zip(add_keys, refs[nr + npar + nc:nr + npar + nc + len(add_keys)])}
        orefs = refs[nr + npar + nc + len(add_keys):]
        _, vjp = jax.vjp(lambda rr, pp: tuple(fn(*rr, *pp)), rv, pv)
        drows, dpars = vjp(tuple(cv))
        for o_ref, ri in zip(orefs[:len(want)], want):
            g = drows[ri]
            if ri in av:
                g = g + av[ri]
            o_ref[...] = g.astype(o_ref.dtype)

        @pl.when(i == 0)
        def _():
            for o_ref in orefs[len(want):]:
                o_ref[...] = jnp.zeros_like(o_ref)

        for o_ref, g in zip(orefs[len(want):], dpars):
            o_ref[...] += g

    add_ops = [adds[k] for k in add_keys]
    out_specs = ([pl.BlockSpec((tm, rows[ri][2]), lambda i: (i, 0)) for ri in want]
                 + [pl.BlockSpec((p[0].shape[0], p[2]), lambda i: (0, 0)) for p in params])
    out_shape = ([jax.ShapeDtypeStruct((s, rows[ri][2]), row_grads[ri]) for ri in want]
                 + [jax.ShapeDtypeStruct((p[0].shape[0], p[2]), F32) for p in params])
    res = pl.pallas_call(
        body, name=name, grid=(s // tm,),
        in_specs=_row_specs(rows, tm) + _par_specs(params) + _row_specs(couts, tm) + _row_specs(add_ops, tm),
        out_specs=out_specs, out_shape=out_shape,
        compiler_params=_cparams(("arbitrary",)),
    )(*[r[0] for r in rows], *[p[0] for p in params], *[c[0] for c in couts], *[a[0] for a in add_ops])
    return res[:len(want)], res[len(want):]


def _silu(v):
    return v * jax.nn.sigmoid(v)


def _rms(v, g):
    return v * lax.rsqrt(jnp.mean(v * v, axis=-1, keepdims=True) + EPS) * g


SUBLANES = 8
CONV_ROWS = 256


def _halo(shifts):
    up = lambda v: -(-v // SUBLANES) * SUBLANES
    return up(max(0, -min(shifts))), up(max(0, max(shifts)))


def _fill_halo(xp_sc, x_ref, front, back):
    s = x_ref.shape[0]
    if front:
        xp_sc[0:front, :] = jnp.zeros((front, LANES), F32)
    if back:
        xp_sc[front + s:front + s + back, :] = jnp.zeros((back, LANES), F32)
    xp_sc[front:front + s, :] = x_ref[...].astype(F32)


def _shift_conv(x, w, b, shifts, *, name):
    s, c = x.shape
    tr = min(CONV_ROWS, s)
    nk = len(shifts)
    front, back = _halo(shifts)

    def body(*refs):
        if b is None:
            x_ref, w_ref, o_ref, xp_sc = refs
        else:
            x_ref, w_ref, b_ref, o_ref, xp_sc = refs
        _fill_halo(xp_sc, x_ref, front, back)

        def chunk(r, carry):
            base = pl.multiple_of(r * tr, tr)
            acc = jnp.zeros((tr, LANES), F32) if b is None else jnp.broadcast_to(b_ref[...], (tr, LANES))
            for kk in range(nk):
                acc = acc + xp_sc[pl.ds(base + front + shifts[kk], tr), :] * w_ref[kk:kk + 1, :]
            o_ref[pl.ds(base, tr), :] = acc
            return carry

        lax.fori_loop(0, s // tr, chunk, 0)

    strip = pl.BlockSpec((s, LANES), lambda cb: (0, cb))
    in_specs = [strip, pl.BlockSpec((nk, LANES), lambda cb: (0, cb))]
    args = [x, w]
    if b is not None:
        in_specs.append(pl.BlockSpec((1, LANES), lambda cb: (0, cb)))
        args.append(b)
    return pl.pallas_call(
        body, name=name, grid=(c // LANES,), in_specs=in_specs, out_specs=strip,
        out_shape=jax.ShapeDtypeStruct((s, c), F32),
        scratch_shapes=[pltpu.VMEM((front + s + back, LANES), F32)],
        compiler_params=_cparams(("parallel",)),
    )(*args)


def _conv_fwd(x, w, b, *, name):
    k = w.shape[0]
    return _shift_conv(x, w, b, [kk - (k - 1) for kk in range(k)], name=name)


def _conv_bwd_x(dy, w, *, name):
    k = w.shape[0]
    return _shift_conv(dy, w, None, [(k - 1) - kk for kk in range(k)], name=name)


def _conv_bwd_w(x, dy, k, *, name):
    s, c = x.shape
    tr = min(CONV_ROWS, s)
    shifts = [kk - (k - 1) for kk in range(k)]
    front, back = _halo(shifts)

    def fold(v):
        return jnp.sum(v.reshape(tr // SUBLANES, SUBLANES, LANES), axis=0)

    def body(x_ref, dy_ref, dw_ref, db_ref, xp_sc, dw_sc, db_sc):
        _fill_halo(xp_sc, x_ref, front, back)
        dw_sc[...] = jnp.zeros_like(dw_sc)
        db_sc[...] = jnp.zeros_like(db_sc)

        def chunk(r, carry):
            base = pl.multiple_of(r * tr, tr)
            dyv = dy_ref[pl.ds(base, tr), :]
            db_sc[...] += fold(dyv)
            for kk in range(k):
                dw_sc[kk] += fold(xp_sc[pl.ds(base + front + shifts[kk], tr), :] * dyv)
            return carry

        lax.fori_loop(0, s // tr, chunk, 0)
        db_ref[...] = jnp.sum(db_sc[...], axis=0, keepdims=True)
        for kk in range(k):
            dw_ref[kk:kk + 1, :] = jnp.sum(dw_sc[kk], axis=0, keepdims=True)

    strip = pl.BlockSpec((s, LANES), lambda cb: (0, cb))
    return pl.pallas_call(
        body, name=name, grid=(c // LANES,), in_specs=[strip, strip],
        out_specs=[pl.BlockSpec((k, LANES), lambda cb: (0, cb)), pl.BlockSpec((1, LANES), lambda cb: (0, cb))],
        out_shape=[jax.ShapeDtypeStruct((k, c), F32), jax.ShapeDtypeStruct((1, c), F32)],
        scratch_shapes=[pltpu.VMEM((front + s + back, LANES), F32), pltpu.VMEM((k, SUBLANES, LANES), F32),
                        pltpu.VMEM((SUBLANES, LANES), F32)],
        compiler_params=_cparams(("parallel",)),
    )(x, dy)


_DIMS = {"nn": ((1,), (0,)), "nt": ((1,), (1,)), "tn": ((0,), (0,))}


def _bd(a, b, mode):
    return lax.dot_general(a.astype(BF16), b.astype(BF16), (_DIMS[mode], ((), ())), preferred_element_type=F32)


@functools.partial(jax.custom_vjp, nondiff_argnums=(2,))
def _bdot(a, b, mode):
    return _bd(a, b, mode)


def _bdot_fwd(a, b, mode):
    return _bd(a, b, mode), (a, b)


def _bdot_bwd(mode, res, g):
    a, b = res
    if mode == "nn":
        return _bd(g, b, "nt"), _bd(a, g, "tn")
    if mode == "nt":
        return _bd(g, b, "nn"), _bd(g, a, "tn")
    return _bd(b, g, "nt"), _bd(a, g, "nn")


_bdot.defvjp(_bdot_fwd, _bdot_bwd)


def _split3(v):
    hi = v.astype(BF16)
    r1 = v - hi.astype(F32)
    mid = r1.astype(BF16)
    lo = (r1 - mid.astype(F32)).astype(BF16)
    return hi, mid, lo


def _mask_dot(mask01, v, mode):
    out = None
    for part in _split3(v):
        if mode == "vn":
            t = lax.dot_general(part, mask01, (_DIMS["nn"], ((), ())), preferred_element_type=F32)
        else:
            t = lax.dot_general(mask01, part, (_DIMS[mode], ((), ())), preferred_element_type=F32)
        out = t if out is None else out + t
    return out


def _lower_tri(n):
    r = lax.broadcasted_iota(jnp.int32, (n, n), 0)
    c = lax.broadcasted_iota(jnp.int32, (n, n), 1)
    return (r >= c).astype(BF16)


@jax.custom_vjp
def _tri_dot(w):
    return _mask_dot(_lower_tri(w.shape[0]), w, "nn")


def _tri_dot_fwd(w):
    return _tri_dot(w), None


def _tri_dot_bwd(_, g):
    return (_mask_dot(_lower_tri(g.shape[0]), g, "tn"),)


_tri_dot.defvjp(_tri_dot_fwd, _tri_dot_bwd)


def _cumsum_lanes(x, *, reverse, name):
    h, s = x.shape
    n = s // LANES

    def body(x_ref, o_ref):
        r = lax.broadcasted_iota(jnp.int32, (LANES, LANES), 0)
        c = lax.broadcasted_iota(jnp.int32, (LANES, LANES), 1)
        m01 = ((r >= c) if reverse else (r <= c)).astype(BF16)

        def step(t, carry):
            ci = (n - 1 - t) if reverse else t
            at = pl.ds(pl.multiple_of(ci * LANES, LANES), LANES)
            blk = x_ref[:, at]
            o_ref[:, at] = _mask_dot(m01, blk, "vn") + carry
            return carry + jnp.sum(blk, axis=1, keepdims=True)

        lax.fori_loop(0, n, step, jnp.zeros((h, 1), F32))

    return pl.pallas_call(body, name=name, out_shape=jax.ShapeDtypeStruct((h, s), F32),
                          compiler_params=_cparams())(x)


SSD_PAIRS = SSD_HPG // 2


def _ssd2_chunk(xs, dt, bm, cm, hin, a, dsk, head0):
    n = CHUNK
    row = lax.broadcasted_iota(jnp.int32, (n, n), 0)
    col = lax.broadcasted_iota(jnp.int32, (n, n), 1)
    lower = row >= col
    ustrict = (row > col).astype(F32)
    lane = lax.broadcasted_iota(jnp.int32, (1, LANES), 1)
    sub = lax.broadcasted_iota(jnp.int32, (n, 1), 0)
    e_first = (sub == 0).astype(F32)
    e_last = (sub == n - 1).astype(F32)
    lane0 = (lane == 0).astype(F32)
    half_l = [(lane < LANES // 2).astype(F32), (lane >= LANES // 2).astype(F32)]
    half_s = [(sub < LANES // 2).astype(F32), (sub >= LANES // 2).astype(F32)]
    cb = _bdot(cm, bm, "nt")
    da = dt * (-jnp.exp(a))
    ys, houts = [], []
    for pr in range(SSD_PAIRS):
        y = jnp.zeros((n, LANES), F32)
        xdte = jnp.zeros((n, LANES), F32)
        lane_gain = jnp.zeros((n, LANES), F32)
        row_gain = jnp.zeros((LANES, 1), F32)
        for hf in range(2):
            oh = (lane == head0 + 2 * pr + hf).astype(F32)
            dt_col = jnp.sum(dt * oh, axis=1, keepdims=True)
            da_col = jnp.sum(da * oh, axis=1, keepdims=True)
            dsk_h = jnp.sum(dsk * oh, axis=1, keepdims=True)
            seg = _tri_dot(da_col * ustrict)
            decay = jnp.where(lower, jnp.exp(seg), 0.0)
            cs_col = jnp.sum(seg * lane0, axis=1, keepdims=True) + jnp.sum(da_col * e_first, axis=0, keepdims=True)
            total = jnp.sum(cs_col * e_last, axis=0, keepdims=True)
            xh = xs[pr] * half_l[hf]
            xd = xh * dt_col
            y = y + _bdot(cb * decay, xd, "nn") + xh * dsk_h
            xdte = xdte + xd * jnp.exp(total - cs_col)
            lane_gain = lane_gain + jnp.exp(cs_col) * half_l[hf]
            row_gain = row_gain + jnp.exp(total) * half_s[hf]
        houts.append(hin[pr] * row_gain + _bdot(xdte, bm, "tn"))
        ys.append(y + _bdot(cm, hin[pr], "nt") * lane_gain)
    return ys, houts


SSD_STEP = 4


def _ssd2_steps(s):
    per = min(SSD_STEP, s // CHUNK)
    return per, s // (CHUNK * per)


def _ssd2_multi(xs, dt, bm, cm, hin, a, dsk, head0):
    ys = []
    for k in range(len(dt)):
        y, hin = _ssd2_chunk(xs[k], dt[k], bm[k], cm[k], hin, a, dsk, head0)
        ys.append(y)
    return ys, hin


def _ssd2_specs(per, nc, rev):
    cc = (lambda c: nc - 1 - c) if rev else (lambda c: c)
    rows = CHUNK * per
    act = pl.BlockSpec((rows, SSD_PAIRS * LANES), lambda c, g: (cc(c), g))
    row = pl.BlockSpec((rows, LANES), lambda c, g: (cc(c), 0))
    bmat = pl.BlockSpec((rows, LANES), lambda c, g: (cc(c), D_SSD // LANES + g))
    cmat = pl.BlockSpec((rows, LANES), lambda c, g: (cc(c), D_SSD // LANES + SSD_GROUPS + g))
    par = pl.BlockSpec((1, LANES), lambda c, g: (0, 0))
    hs = pl.BlockSpec((1, SSD_PAIRS, LANES, D_STATE), lambda c, g: (cc(c), g, 0, 0))
    return act, row, bmat, cmat, par, hs


def _chunk_rows(ref, k):
    return ref[k * CHUNK:(k + 1) * CHUNK, :]


def _pair_cols(ref, k):
    return [ref[k * CHUNK:(k + 1) * CHUNK, pr * LANES:(pr + 1) * LANES] for pr in range(SSD_PAIRS)]


def _ssd2_fwd(dt, xbc, a, dsk, *, name):
    s = xbc.shape[0]
    per, nc = _ssd2_steps(s)
    act, row, bmat, cmat, par, hs = _ssd2_specs(per, nc, False)

    def body(xs_ref, dt_ref, bm_ref, cm_ref, a_ref, dsk_ref, y_ref, hs_ref, h_sc):
        c, g = pl.program_id(0), pl.program_id(1)

        @pl.when(c == 0)
        def _():
            h_sc[pl.ds(g * SSD_PAIRS, SSD_PAIRS)] = jnp.zeros((SSD_PAIRS, LANES, D_STATE), F32)

        hin = [h_sc[g * SSD_PAIRS + pr] for pr in range(SSD_PAIRS)]
        ks = range(per)
        ys, houts = _ssd2_multi([_pair_cols(xs_ref, k) for k in ks], [_chunk_rows(dt_ref, k) for k in ks],
                                [_chunk_rows(bm_ref, k) for k in ks], [_chunk_rows(cm_ref, k) for k in ks],
                                hin, a_ref[...], dsk_ref[...], g * SSD_HPG)
        for pr in range(SSD_PAIRS):
            for k in ks:
                y_ref[k * CHUNK:(k + 1) * CHUNK, pr * LANES:(pr + 1) * LANES] = ys[k][pr]
            hs_ref[0, pr] = hin[pr]
            h_sc[g * SSD_PAIRS + pr] = houts[pr]

    return pl.pallas_call(
        body, name=name, grid=(nc, SSD_GROUPS),
        in_specs=[act, row, bmat, cmat, par, par], out_specs=[act, hs],
        out_shape=[jax.ShapeDtypeStruct((s, D_SSD), F32),
                   jax.ShapeDtypeStruct((nc, SSD_HEADS // 2, LANES, D_STATE), F32)],
        scratch_shapes=[pltpu.VMEM((SSD_HEADS // 2, LANES, D_STATE), F32)],
        compiler_params=_cparams(("arbitrary", "arbitrary")),
    )(xbc, dt, xbc, xbc, a, dsk)


def _ssd2_bwd(dt, xbc, a, dsk, hsave, dy, *, carry=None, name):
    s = xbc.shape[0]
    per, nc = _ssd2_steps(s)
    act, row, bmat, cmat, par, hs = _ssd2_specs(per, nc, True)
    gmat = pl.BlockSpec((CHUNK * per, LANES), lambda c, g: (nc - 1 - c, g))

    def body(*refs):
        ins, outs, (dh_sc,), comm = _carried(carry, refs, 8, 6, 1)
        xs_ref, dt_ref, bm_ref, cm_ref, a_ref, dsk_ref, hs_ref, dy_ref = ins
        dxs_ref, ddt_ref, dbm_ref, dcm_ref, da_ref, ddsk_ref = outs
        c, g = pl.program_id(0), pl.program_id(1)
        if carry is not None:
            @pl.when((c == 0) & (g == 0))
            def _():
                carry.start(*comm)

        @pl.when(c == 0)
        def _():
            dh_sc[pl.ds(g * SSD_PAIRS, SSD_PAIRS)] = jnp.zeros((SSD_PAIRS, LANES, D_STATE), F32)

        @pl.when((c == 0) & (g == 0))
        def _():
            da_ref[...] = jnp.zeros_like(da_ref)
            ddsk_ref[...] = jnp.zeros_like(ddsk_ref)

        @pl.when(g == 0)
        def _():
            ddt_ref[...] = jnp.zeros_like(ddt_ref)

        head0 = g * SSD_HPG
        ks = range(per)
        prim = ([_pair_cols(xs_ref, k) for k in ks], [_chunk_rows(dt_ref, k) for k in ks],
                [_chunk_rows(bm_ref, k) for k in ks], [_chunk_rows(cm_ref, k) for k in ks],
                [hs_ref[0, pr] for pr in range(SSD_PAIRS)], a_ref[...], dsk_ref[...])
        _, vjp = jax.vjp(lambda *p: _ssd2_multi(*p, head0), *prim)
        cot = ([_pair_cols(dy_ref, k) for k in ks], [dh_sc[g * SSD_PAIRS + pr] for pr in range(SSD_PAIRS)])
        dxs, ddt, dbm, dcm, dhin, da, ddsk = vjp(cot)
        for pr in range(SSD_PAIRS):
            for k in ks:
                dxs_ref[k * CHUNK:(k + 1) * CHUNK, pr * LANES:(pr + 1) * LANES] = dxs[k][pr]
            dh_sc[g * SSD_PAIRS + pr] = dhin[pr]
        for k in ks:
            ddt_ref[k * CHUNK:(k + 1) * CHUNK, :] += ddt[k]
            dbm_ref[k * CHUNK:(k + 1) * CHUNK, :] = dbm[k]
            dcm_ref[k * CHUNK:(k + 1) * CHUNK, :] = dcm[k]
        da_ref[...] += da
        ddsk_ref[...] += ddsk
        if carry is not None:
            @pl.when((c == nc - 1) & (g == SSD_GROUPS - 1))
            def _():
                carry.wait(*comm)

    extra = carry if carry is not None else _Exchange([], "c", False)
    return pl.pallas_call(
        body, name=name, grid=(nc, SSD_GROUPS),
        in_specs=[act, row, bmat, cmat, par, par, hs, act] + extra.in_specs,
        out_specs=[act, row, gmat, gmat, par, par] + extra.out_specs,
        out_shape=[jax.ShapeDtypeStruct((s, D_SSD), F32), jax.ShapeDtypeStruct((s, LANES), F32),
                   jax.ShapeDtypeStruct((s, SSD_GROUPS * D_STATE), F32),
                   jax.ShapeDtypeStruct((s, SSD_GROUPS * D_STATE), F32),
                   jax.ShapeDtypeStruct((1, LANES), F32), jax.ShapeDtypeStruct((1, LANES), F32)] + extra.out_shape,
        scratch_shapes=[pltpu.VMEM((SSD_HEADS // 2, LANES, D_STATE), F32)]
        + (carry.scratch if carry is not None else []),
        compiler_params=_cparams(("arbitrary", "arbitrary")),
    )(xbc, dt, xbc, xbc, a, dsk, hsave, dy, *extra.arrs)


FOX_BLOCK = 512
NEG = -1e30
PAIRS = FOX_HEADS // 2
HALF = LANES // 2


def _causal(t):
    return lax.broadcasted_iota(jnp.int32, (t, t), 0) >= lax.broadcasted_iota(jnp.int32, (t, t), 1)


def _first_half():
    return lax.broadcasted_iota(jnp.int32, (1, LANES), 1) < HALF


def _pair_bias(c_ref, hh, qblock, kblock, t):
    lane = lax.broadcasted_iota(jnp.int32, (1, LANES), 1)
    cq = c_ref[hh, :, pl.ds(pl.multiple_of(qblock * t, LANES), LANES)]
    cref = jnp.sum(jnp.where(lane == 0, cq, 0.0), axis=1, keepdims=True)
    return cref - c_ref[hh, :, pl.ds(pl.multiple_of(kblock * t, LANES), t)]


def _fox_dopack(do, o, *, tm=256, name):
    s, d = do.shape
    tm = min(tm, s)

    def body(do_ref, o_ref, out_ref):
        dov = do_ref[...].astype(BF16)
        prod = dov.astype(F32) * o_ref[...]
        r = lax.broadcasted_iota(jnp.int32, (d, LANES), 0)
        c = lax.broadcasted_iota(jnp.int32, (d, LANES), 1)
        heads = ((r >= c * FOX_HEADDIM) & (r < (c + 1) * FOX_HEADDIM)).astype(BF16)
        negd = -_mask_dot(heads, prod, "vn")
        hr = lax.broadcasted_iota(jnp.int32, (LANES, 2 * d), 0)
        col = lax.broadcasted_iota(jnp.int32, (LANES, 2 * d), 1)
        base = (hr >> 1) * (2 * LANES) + jnp.where((hr & 1) == 0, HALF, LANES)
        terms = None
        for kk, part in enumerate(_split3(negd)):
            place = ((col == base + kk) & (hr < FOX_HEADS)).astype(BF16)
            tk = lax.dot_general(part, place, (_DIMS["nn"], ((), ())), preferred_element_type=F32)
            terms = tk if terms is None else terms + tk
        first = _first_half()
        zero = jnp.zeros((tm, LANES), BF16)
        pieces = []
        for hp in range(PAIRS):
            blk = dov[:, hp * LANES:(hp + 1) * LANES]
            pieces += [jnp.where(first, blk, zero), jnp.where(first, zero, blk)]
        out_ref[...] = (jnp.concatenate(pieces, axis=1).astype(F32) + terms).astype(BF16)

    row = pl.BlockSpec((tm, d), lambda i: (i, 0))
    return pl.pallas_call(body, name=name, grid=(s // tm,), in_specs=[row, row],
                          out_specs=pl.BlockSpec((tm, 2 * d), lambda i: (i, 0)),
                          out_shape=jax.ShapeDtypeStruct((s, 2 * d), BF16),
                          compiler_params=_cparams(("parallel",)))(do, o)


FOX_DEAD = 110.0
BOUND_SLACK = 1.001


def _fox_block_bounds(qkv, c):
    s = qkv.shape[0]
    t = min(FOX_BLOCK, s)
    k = qkv[:, D_FOX:2 * D_FOX].astype(F32).reshape(s // t, t, FOX_HEADS, FOX_HEADDIM)
    kn = jnp.sqrt(jnp.max(jnp.sum(k * k, axis=-1), axis=1)).T
    return kn, c[:, 0, ::t], c[:, 0, t - 1::t]


def _fox_live(qmax, mmin, kn_ref, cs_ref, ce_ref, h, qblock, kblock):
    bound = qmax * (kn_ref[h, kblock] * BOUND_SLACK) + (cs_ref[h, qblock] - ce_ref[h, kblock])
    return bound - mmin > -FOX_DEAD


def _run_live(live, work):
    @pl.when(live[0] & live[1])
    def _():
        work(0)
        work(1)

    @pl.when(live[0] & jnp.logical_not(live[1]))
    def _():
        work(0)

    @pl.when(jnp.logical_not(live[0]) & live[1])
    def _():
        work(1)


def _fox3_fwd(qkv, c, bounds, *, carry=None, name):
    assert qkv.shape[0] // min(FOX_BLOCK, qkv.shape[0]) < LANES - 1
    s = qkv.shape[0]
    t = min(FOX_BLOCK, s)
    nb = s // t
    nt = (((1,), (1,)), ((), ()))
    nn = (((1,), (0,)), ((), ()))
    scale = FOX_HEADDIM ** -0.5

    def body(*refs):
        ins, (o_ref, mrun_ref, qmax_ref, mmin_ref), (m_sc, acc_sc, mt_sc), comm = _carried(carry, refs, 7, 4, 3)
        q_ref, k_ref, v_ref, c_ref, kn_ref, cs_ref, ce_ref = ins
        hp, i = pl.program_id(0), pl.program_id(1)
        if carry is not None:
            @pl.when((hp == 0) & (i == 0))
            def _():
                carry.start(*comm)

        first = _first_half()
        lane = lax.broadcasted_iota(jnp.int32, (1, LANES), 1)
        m_sc[...] = jnp.full_like(m_sc, NEG)
        acc_sc[...] = jnp.zeros_like(acc_sc)
        mt_sc[...] = jnp.zeros_like(mt_sc)
        q2 = q_ref[...] * scale
        zero = jnp.zeros_like(q2)
        qs = [jnp.where(first, q2, zero), jnp.where(first, zero, q2)]

        def head_block(hh, j, k2, vxh, masked):
            sc = lax.dot_general(qs[hh], k2, nt, preferred_element_type=F32) + _pair_bias(c_ref, hh, i, j, t)
            if masked:
                sc = jnp.where(_causal(t), sc, NEG)
            m_prev = m_sc[hh]
            m_new = jnp.maximum(m_prev, jnp.max(sc, axis=1, keepdims=True))
            pr = jnp.exp(sc - jnp.tile(m_new, (1, t // LANES))).astype(BF16)
            pv = lax.dot_general(pr, vxh, nn, preferred_element_type=F32)
            acc_sc[hh] = jnp.exp(m_prev - m_new) * acc_sc[hh] + pv
            m_sc[hh] = m_new

        def step(j, masked):
            at = pl.ds(pl.multiple_of(j * t, t), t)
            k2, v2 = k_ref[at, :], v_ref[at, :]
            one = jnp.ones_like(v2)
            vx = [jnp.where(first, v2, one), jnp.where(first, one, v2)]
            if masked:
                for hh in range(2):
                    head_block(hh, j, k2, vx[hh], True)
            else:
                live = [_fox_live(qmax_ref[2 * hp + hh, i], mmin_ref[2 * hp + hh, i], kn_ref, cs_ref, ce_ref,
                                  2 * hp + hh, i, j) for hh in range(2)]
                _run_live(live, lambda hh: head_block(hh, j, k2, vx[hh], False))
            for hh in range(2):
                mt_sc[hh] = jnp.where(lane == j, m_sc[hh], mt_sc[hh])

        step(i, True)
        for hh in range(2):
            qf = qs[hh].astype(F32)
            qmax_ref[2 * hp + hh, i] = jnp.sqrt(jnp.max(jnp.sum(qf * qf, axis=1, keepdims=True)))
            mmin_ref[2 * hp + hh, i] = jnp.min(m_sc[hh])
        lax.fori_loop(0, i, lambda n, carry: (step(i - 1 - n, False), carry)[1], 0)
        acc_a, acc_b = acc_sc[0], acc_sc[1]
        den_a = jnp.where(first, pltpu.roll(acc_a, HALF, 1), acc_a)
        den_b = jnp.where(first, acc_b, pltpu.roll(acc_b, HALF, 1))
        o_ref[...] = jnp.where(first, acc_a / den_a, acc_b / den_b)
        for hh, den in enumerate((den_a, den_b)):
            mrun_ref[:, hh * LANES:(hh + 1) * LANES] = jnp.where(lane == LANES - 1, m_sc[hh] + jnp.log(den), mt_sc[hh])
        if carry is not None:
            @pl.when((pl.program_id(0) == PAIRS - 1) & (i == nb - 1))
            def _():
                carry.wait(*comm)

    nq = D_FOX // LANES
    smem = pl.BlockSpec(memory_space=pltpu.SMEM)
    extra = carry if carry is not None else _Exchange([], "c", False)
    return pl.pallas_call(
        body, name=name, grid=(PAIRS, nb),
        in_specs=[pl.BlockSpec((t, LANES), lambda hp, i: (i, hp)),
                  pl.BlockSpec((s, LANES), lambda hp, i: (0, nq + hp)),
                  pl.BlockSpec((s, LANES), lambda hp, i: (0, 2 * nq + hp)),
                  pl.BlockSpec((2, 1, s), lambda hp, i: (hp, 0, 0))] + [smem] * 3 + extra.in_specs,
        out_specs=[pl.BlockSpec((t, LANES), lambda hp, i: (i, hp)),
                   pl.BlockSpec((t, 2 * LANES), lambda hp, i: (i, hp)), smem, smem] + extra.out_specs,
        out_shape=[jax.ShapeDtypeStruct((s, D_FOX), F32), jax.ShapeDtypeStruct((s, 2 * D_FOX), F32),
                   jax.ShapeDtypeStruct((FOX_HEADS, nb), F32), jax.ShapeDtypeStruct((FOX_HEADS, nb), F32)]
        + extra.out_shape,
        scratch_shapes=[pltpu.VMEM((2, t, LANES), F32)] * 3 + (carry.scratch if carry is not None else []),
        compiler_params=_cparams(("arbitrary", "arbitrary")),
    )(qkv, qkv, qkv, c, *bounds, *extra.arrs)


def _fox3_bwd(qkv, c, bounds, dox, mrun, *, carry=None, name):
    s = qkv.shape[0]
    t = min(FOX_BLOCK, s)
    nb = s // t
    nt = (((1,), (1,)), ((), ()))
    nn = (((1,), (0,)), ((), ()))
    tn = (((0,), (0,)), ((), ()))
    scale = FOX_HEADDIM ** -0.5

    def body(*refs):
        ins, outs, scratch, comm = _carried(carry, refs, 11, 4, 3)
        k_ref, v_ref, q_ref, c_ref, do_ref, mrun_ref, kn_ref, cs_ref, ce_ref, qmax_ref, mmin_ref = ins
        dq_ref, dk_ref, dv_ref, dc_ref = outs
        dk_sc, dv_sc, dc_sc = scratch
        hp, j = pl.program_id(0), pl.program_id(1)
        if carry is not None:
            @pl.when((hp == 0) & (j == 0))
            def _():
                carry.start(*comm)

        first = _first_half()
        halves = [first, jnp.logical_not(first)]
        lane = lax.broadcasted_iota(jnp.int32, (1, LANES), 1)

        @pl.when(j == 0)
        def _():
            dq_ref[...] = jnp.zeros_like(dq_ref)

        dk_sc[...] = jnp.zeros_like(dk_sc)
        dv_sc[...] = jnp.zeros_like(dv_sc)
        dc_sc[...] = jnp.zeros_like(dc_sc)
        k2, v2 = k_ref[...], v_ref[...]
        one = jnp.ones_like(v2)
        vx = [jnp.where(first, v2, one), jnp.where(first, one, v2)]

        def pick(stats, which):
            return jnp.sum(jnp.where(lane == which, stats, 0.0), axis=1, keepdims=True)

        def head_block(hh, i, at, qsh, stats, masked):
            mine = slice(hh * LANES, (hh + 1) * LANES)
            dov = do_ref[at, mine]
            sc = lax.dot_general(qsh, k2, nt, preferred_element_type=F32) + _pair_bias(c_ref, hh, i, j, t)
            mj = pick(stats, j)
            gain = jnp.broadcast_to(jnp.exp(mj - pick(stats, LANES - 1)), (t, LANES))
            mj = jnp.broadcast_to(mj, (t, LANES))
            pb = jnp.exp(sc - jnp.tile(mj, (1, t // LANES))).astype(BF16)
            if masked:
                pb = jnp.where(_causal(t), pb, jnp.zeros_like(pb))
            pr = pb.astype(F32) * jnp.tile(gain, (1, t // LANES))
            ds = pr * lax.dot_general(dov, vx[hh], nt, preferred_element_type=F32)
            dc_sc[hh] -= jnp.sum(ds, axis=0, keepdims=True)
            dsb = ds.astype(BF16)
            dvh = lax.dot_general(pr.astype(BF16), dov, tn, preferred_element_type=F32)
            dv_sc[...] += jnp.where(halves[hh], dvh, 0.0)
            dk_sc[...] += lax.dot_general(dsb, qsh, tn, preferred_element_type=F32)
            dqh = lax.dot_general(dsb, k2, nn, preferred_element_type=F32)
            dq_ref[at, :] += jnp.where(halves[hh], dqh, 0.0)

        def step(i, masked):
            at = pl.ds(pl.multiple_of(i * t, t), t)
            q2 = q_ref[at, :] * scale
            zero = jnp.zeros_like(q2)
            qs = [jnp.where(first, q2, zero), jnp.where(first, zero, q2)]
            stats = [mrun_ref[at, hh * LANES:(hh + 1) * LANES] for hh in range(2)]
            if masked:
                for hh in range(2):
                    head_block(hh, i, at, qs[hh], stats[hh], True)
            else:
                live = [_fox_live(qmax_ref[2 * hp + hh, i], mmin_ref[2 * hp + hh, i], kn_ref, cs_ref, ce_ref,
                                  2 * hp + hh, i, j) for hh in range(2)]
                _run_live(live, lambda hh: head_block(hh, i, at, qs[hh], stats[hh], False))

        step(j, True)
        lax.fori_loop(j + 1, nb, lambda i, carry: (step(i, False), carry)[1], 0)
        dk_ref[...] = dk_sc[...]
        dv_ref[...] = dv_sc[...].astype(dv_ref.dtype)
        dc_ref[...] = dc_sc[...]
        if carry is not None:
            @pl.when((pl.program_id(0) == PAIRS - 1) & (j == nb - 1))
            def _():
                carry.wait(*comm)

    nq = D_FOX // LANES
    blk = pl.BlockSpec((t, LANES), lambda hp, j: (j, hp))
    smem = pl.BlockSpec(memory_space=pltpu.SMEM)
    extra = carry if carry is not None else _Exchange([], "c", False)
    return pl.pallas_call(
        body, name=name, grid=(PAIRS, nb),
        in_specs=[pl.BlockSpec((t, LANES), lambda hp, j: (j, nq + hp)),
                  pl.BlockSpec((t, LANES), lambda hp, j: (j, 2 * nq + hp)),
                  pl.BlockSpec((s, LANES), lambda hp, j: (0, hp)),
                  pl.BlockSpec((2, 1, s), lambda hp, j: (hp, 0, 0)),
                  pl.BlockSpec((s, 2 * LANES), lambda hp, j: (0, hp)),
                  pl.BlockSpec((s, 2 * LANES), lambda hp, j: (0, hp))] + [smem] * 5 + extra.in_specs,
        out_specs=[pl.BlockSpec((s, LANES), lambda hp, j: (0, hp)), blk, blk,
                   pl.BlockSpec((2, 1, t), lambda hp, j: (hp, 0, j))] + extra.out_specs,
        out_shape=[jax.ShapeDtypeStruct((s, D_FOX), F32), jax.ShapeDtypeStruct((s, D_FOX), F32),
                   jax.ShapeDtypeStruct((s, D_FOX), BF16), jax.ShapeDtypeStruct((FOX_HEADS, 1, s), F32)]
        + extra.out_shape,
        scratch_shapes=[pltpu.VMEM((t, LANES), F32), pltpu.VMEM((t, LANES), F32), pltpu.VMEM((2, 1, t), F32)]
        + (carry.scratch if carry is not None else []),
        compiler_params=_cparams(("arbitrary", "arbitrary")),
    )(qkv, qkv, qkv, c, dox, mrun, *bounds, *extra.arrs)


def _final(x1, out1, g, tgt, *, tm=256, name):
    s, d = x1.shape
    tm = min(tm, s)

    def body(x_ref, o_ref, g_ref, t_ref, dx_ref, do_ref, dg_ref, loss_ref):
        i = pl.program_id(0)

        @pl.when(i == 0)
        def _():
            dg_ref[...] = jnp.zeros_like(dg_ref)
            loss_ref[...] = jnp.zeros_like(loss_ref)

        tv = t_ref[...]

        def lossf(xv, ov, gv):
            err = jnp.square(xv + _rms(ov, gv) - tv)
            return 0.5 * jnp.sum(jnp.mean(err, axis=-1, keepdims=True), axis=0, keepdims=True)

        val, vjp = jax.vjp(lossf, x_ref[...], o_ref[...], g_ref[...])
        dx, do, dg = vjp(jnp.ones((1, 1), F32))
        dx_ref[...] = dx
        do_ref[...] = do.astype(do_ref.dtype)
        dg_ref[...] += dg
        loss_ref[...] += val

    row = pl.BlockSpec((tm, d), lambda i: (i, 0))
    par = pl.BlockSpec((1, d), lambda i: (0, 0))
    return pl.pallas_call(
        body, name=name, grid=(s // tm,), in_specs=[row, row, par, row],
        out_specs=[row, row, par, pl.BlockSpec((1, 1), lambda i: (0, 0))],
        out_shape=[jax.ShapeDtypeStruct((s, d), F32), jax.ShapeDtypeStruct((s, d), BF16),
                   jax.ShapeDtypeStruct((1, d), F32), jax.ShapeDtypeStruct((1, 1), F32)],
        compiler_params=_cparams(("arbitrary",)),
    )(x1, out1, g, tgt)


def _row_tile(r):
    return LANES if r % LANES == 0 else r


def _sum_slots(parts, *, out_dtype=F32, name):
    p, r, c = parts.shape
    tr = _row_tile(r)

    def body(p_ref, o_ref):
        acc = p_ref[0].astype(F32)
        for k in range(1, p):
            acc = acc + p_ref[k].astype(F32)
        o_ref[...] = acc.astype(o_ref.dtype)

    return pl.pallas_call(
        body, name=name, grid=(r // tr,),
        in_specs=[pl.BlockSpec((p, tr, c), lambda i: (0, i, 0))],
        out_specs=pl.BlockSpec((tr, c), lambda i: (i, 0)),
        out_shape=jax.ShapeDtypeStruct((r, c), out_dtype),
        compiler_params=_cparams(("parallel",)),
    )(parts)


def _adamw(w, gparts, m, v, *, name):
    r, c = w.shape
    p = gparts.shape[0]
    tr = _row_tile(r)

    def body(w_ref, g_ref, m_ref, v_ref, go_ref, d_ref, mo_ref, vo_ref):
        g = g_ref[0].astype(F32)
        for k in range(1, p):
            g = g + g_ref[k].astype(F32)
        mn = ADAM_B1 * m_ref[...] + (1.0 - ADAM_B1) * g
        vn = ADAM_B2 * v_ref[...] + (1.0 - ADAM_B2) * jnp.square(g)
        m_hat = mn / (1.0 - ADAM_B1 ** ADAM_STEP)
        v_hat = vn / (1.0 - ADAM_B2 ** ADAM_STEP)
        go_ref[...] = g
        d_ref[...] = -ADAM_LR * (m_hat / (jnp.sqrt(v_hat) + ADAM_EPS) + ADAM_WD * w_ref[...])
        mo_ref[...] = mn
        vo_ref[...] = vn

    spec = pl.BlockSpec((tr, c), lambda i: (i, 0))
    return pl.pallas_call(
        body, name=name, grid=(r // tr,),
        in_specs=[spec, pl.BlockSpec((p, tr, c), lambda i: (0, i, 0)), spec, spec],
        out_specs=[spec] * 4, out_shape=[jax.ShapeDtypeStruct((r, c), F32)] * 4,
        compiler_params=_cparams(("parallel",)),
    )(w, gparts, m, v)


_FLIPS = {
    "xy": [(1, 0, 0), (0, 1, 0), (1, 1, 0)],
    "c": [(0, 0, 1)],
    "xyc": [(fx, fy, fc) for fx in (0, 1) for fy in (0, 1) for fc in (0, 1) if (fx, fy, fc) != (0, 0, 0)],
}


def _slot(mode, px, py, pc):
    return {"xy": 2 * px + py, "c": pc, "xyc": 4 * px + 2 * py + pc}[mode]


class _Exchange:
    def __init__(self, arrs, mode, scatter):
        self.arrs, self.mode, self.scatter = list(arrs), mode, scatter
        self.n = len(self.arrs)
        self.flips = _FLIPS[mode]
        nf = len(self.flips)
        anyspec = pl.BlockSpec(memory_space=pl.ANY)
        self.in_specs = [anyspec] * self.n
        self.out_specs = [anyspec] * self.n
        self.out_shape = [jax.ShapeDtypeStruct((nf + 1,) + (a.shape[1:] if scatter else a.shape), a.dtype)
                          for a in self.arrs]
        self.scratch = [pltpu.SemaphoreType.DMA((self.n * nf,)), pltpu.SemaphoreType.DMA((self.n * nf,)),
                        pltpu.SemaphoreType.DMA((self.n,))]

    def _copies(self, ins, outs, sems, arrivals=True):
        send, recv, loc = sems
        nf = len(self.flips)
        x, y, c = lax.axis_index("x"), lax.axis_index("y"), lax.axis_index("c")
        me = _slot(self.mode, x, y, c)
        peers = [(x ^ fx, y ^ fy, c ^ fc) for (fx, fy, fc) in self.flips]

        def src(a, slot):
            return ins[a].at[slot] if self.scatter else ins[a]

        def copy(a, j, dst_slot):
            return pltpu.make_async_remote_copy(
                src_ref=src(a, _slot(self.mode, *peers[j])), dst_ref=outs[a].at[dst_slot],
                send_sem=send.at[a * nf + j], recv_sem=recv.at[a * nf + j], device_id=peers[j], device_id_type=MESH)

        pairs = [(a, j) for a in range(self.n) for j in range(nf)]
        local = [pltpu.make_async_copy(src(a, me), outs[a].at[me], loc.at[a]) for a in range(self.n)]
        sends = [copy(a, j, me) for a, j in pairs]
        recvs = [copy(a, j, _slot(self.mode, *peers[j])) for a, j in pairs] if arrivals else []
        return local, sends, recvs

    def start(self, ins, outs, sems):
        local, sends, _ = self._copies(ins, outs, sems, arrivals=False)
        for cp in local + sends:
            cp.start()

    def wait(self, ins, outs, sems):
        local, sends, recvs = self._copies(ins, outs, sems)
        for cp in recvs:
            cp.wait_recv()
        for cp in sends:
            cp.wait_send()
        for cp in local:
            cp.wait()


def _carried(carry, refs, n_in, n_out, n_scratch):
    k = carry.n if carry is not None else 0
    ins, refs = refs[:n_in], refs[n_in:]
    cin, refs = refs[:k], refs[k:]
    outs, refs = refs[:n_out], refs[n_out:]
    cout, refs = refs[:k], refs[k:]
    scratch, sems = refs[:n_scratch], refs[n_scratch:]
    return ins, outs, scratch, (cin, cout, sems)


def _exchanges(exs, *, name):
    counts = [ex.n for ex in exs]
    total = sum(counts)

    def body(*refs):
        ins, outs, sems = refs[:total], refs[total:2 * total], refs[2 * total:]
        comms, at = [], 0
        for k, ex in enumerate(exs):
            comms.append((ins[at:at + ex.n], outs[at:at + ex.n], sems[3 * k:3 * k + 3]))
            at += ex.n
        for ex, comm in zip(exs, comms):
            ex.start(*comm)
        for ex, comm in zip(exs, comms):
            ex.wait(*comm)

    res = pl.pallas_call(
        body, name=name, in_specs=[sp for ex in exs for sp in ex.in_specs],
        out_specs=[sp for ex in exs for sp in ex.out_specs], out_shape=[sh for ex in exs for sh in ex.out_shape],
        scratch_shapes=[sc for ex in exs for sc in ex.scratch])(*[a for ex in exs for a in ex.arrs])
    out, at = [], 0
    for n in counts:
        out.append(list(res[at:at + n]))
        at += n
    return out


def _exchange(arrs, mode, scatter, *, name):
    return _exchanges([_Exchange(arrs, mode, scatter)], name=name)[0]


def _softplus(v):
    return jnp.maximum(v, 0.0) + jnp.log1p(jnp.exp(-jnp.abs(v)))


def _pad_lanes(v):
    r, n = v.shape
    return jnp.pad(v, ((0, 0), (0, -n % LANES)))


def _fn_rms(v, g):
    return (_rms(v, g),)


def _fn_post(xv, ov, g):
    return (xv + _rms(ov, g),)


def _fn_act(xbc, dtp, fp, dtb, fb):
    return _silu(xbc), _softplus(dtp + dtb), -_softplus(-(fp + fb))


def _fn_mix(y, zs, o, zf, g):
    yg = y * _silu(zs)
    sq = yg * yg
    lane = lax.broadcasted_iota(jnp.int32, (1, D_SSD), 1)
    width = D_SSD // SSD_GROUPS
    rstd = jnp.zeros_like(yg)
    for gi in range(SSD_GROUPS):
        msk = ((lane >= gi * width) & (lane < (gi + 1) * width)).astype(F32)
        ms = jnp.sum(sq * msk, axis=1, keepdims=True) / width
        rstd = rstd + lax.rsqrt(ms + EPS) * msk
    return (jnp.concatenate([yg * rstd * g, o * _silu(zf)], axis=1),)


def _fn_glu(val, gate):
    return (val * jax.nn.sigmoid(gate),)


def _fn_ln(hc, z, g, b):
    mu = jnp.mean(hc, axis=-1, keepdims=True)
    xc = hc - mu
    yn = xc * lax.rsqrt(jnp.mean(xc * xc, axis=-1, keepdims=True) + EPS) * g + b
    return (_silu(yn) * _silu(z),)


class _NoComm:
    def odd_weights(self):
        return None

    def got_odd_weights(self, got, w):
        pass

    def early_grads(self, g):
        return None

    def got_early_grads(self, got):
        pass

    def early_sums(self):
        return None

    def got_early_sums(self, got):
        pass


def _local_step(x, tgt, w, comm=None):
    comm = comm or _NoComm()
    s = x.shape[0]
    d = D_MODEL
    tm = 256
    bf = lambda v: v.astype(BF16)
    c1 = lambda arr: _col(arr, 0, arr.shape[1])
    g = {}

    ew = w["e_w_in"]
    w_z, w_xbc = bf(ew[:, 0:2048]), bf(ew[:, 2048:4096])
    w_dt = bf(_pad_lanes(ew[:, 4096:4112]))
    w_qkv = bf(ew[:, 4112:7184])
    w_f = bf(_pad_lanes(ew[:, 7184:7200]))
    w_eo = bf(w["e_w_out"])
    dtb, fgb = _pad_lanes(w["e_dt_bias"]), _pad_lanes(w["e_fgate_b"])
    alog, dsk = _pad_lanes(w["e_a_log"]), _pad_lanes(w["e_d_skip"])

    (u0,) = _rowwise_fwd(_fn_rms, [c1(x)], [c1(w["e_norm_pre"])], [(d, BF16)], tm=tm, name="e_pre")
    z = _mm(u0, w_z, name="e_in_z")
    xbc_raw = _mm(u0, w_xbc, out_dtype=BF16, name="e_in_xbc")
    qkv = _mm(u0, w_qkv, out_dtype=BF16, name="e_in_qkv")
    dtp = _mm(u0, w_dt, name="e_in_dt")
    fp = _mm(u0, w_f, name="e_in_f")
    xbc_pre = _conv_fwd(xbc_raw, w["e_conv_w"], w["e_conv_b"], name="e_conv")
    act_rows = [c1(xbc_pre), c1(dtp), c1(fp)]
    act_pars = [c1(dtb), c1(fgb)]
    xbc, dt, lf = _rowwise_fwd(_fn_act, act_rows, act_pars, [(2048, F32), (LANES, F32), (LANES, F32)],
                               tm=tm, name="e_act")
    y, hsave = _ssd2_fwd(dt, xbc, alog, dsk, name="e_ssd")
    csum = _cumsum_lanes(lf[:, :FOX_HEADS].T, reverse=False, name="e_cumsum").reshape(FOX_HEADS, 1, s)
    bounds = _fox_block_bounds(qkv, csum)
    o, mrun, qmax, mmin, *got = _fox3_fwd(qkv, csum, bounds, carry=comm.odd_weights(), name="e_fox")
    bounds = (*bounds, qmax, mmin)
    comm.got_odd_weights(got, w)
    w_oi, w_oo = bf(w["o_w_in"]), bf(w["o_w_out"])
    mix_rows = [c1(y), _col(z, 0, D_SSD), c1(o), _col(z, 1, D_FOX)]
    mix_pars = [c1(w["e_ssd_norm"])]
    (hmix,) = _rowwise_fwd(_fn_mix, mix_rows, mix_pars, [(2048, BF16)], tm=tm, name="e_mix")
    out0 = _mm(hmix, w_eo, name="e_out")
    post_rows = [c1(x), c1(out0)]
    (x1,) = _rowwise_fwd(_fn_post, post_rows, [c1(w["e_norm_post"])], [(d, F32)], tm=tm, name="e_post")

    (u1,) = _rowwise_fwd(_fn_rms, [c1(x1)], [c1(w["o_norm_pre"])], [(d, BF16)], tm=tm, name="o_pre")
    p1 = _mm(u1, w_oi, out_dtype=BF16, name="o_in")
    glu_rows = [_col(p1, 0, D_CONV), _col(p1, 1, D_CONV)]
    (hg,) = _rowwise_fwd(_fn_glu, glu_rows, [], [(D_CONV, BF16)], tm=tm, name="o_glu")
    hc = _conv_fwd(hg, w["o_conv_w"], w["o_conv_b"], name="o_conv")
    ln_rows = [c1(hc), _col(p1, 2, D_CONV)]
    ln_pars = [c1(w["o_ln_g"]), c1(w["o_ln_b"])]
    (h2,) = _rowwise_fwd(_fn_ln, ln_rows, ln_pars, [(D_CONV, BF16)], tm=tm, name="o_ln")
    out1 = _mm(h2, w_oo, name="o_out")

    dx2, dout1, g["o_norm_post"], loss = _final(x1, out1, w["o_norm_post"], tgt, name="loss_head")
    dh2 = _mm(dout1, w_oo, tb=True, name="o_out_dx")
    g["o_w_out"] = _mm(h2, dout1, ta=True, name="o_out_dw")
    (dhc, dz1), (g["o_ln_g"], g["o_ln_b"]) = _rowwise_bwd(_fn_ln, ln_rows, ln_pars, [c1(dh2)], [F32, BF16],
                                                         tm=tm, name="o_ln_bwd")
    dhg = _conv_bwd_x(dhc, w["o_conv_w"], name="o_conv_dx")
    g["o_conv_w"], g["o_conv_b"] = _conv_bwd_w(hg, dhc, CONV_WIDTH, name="o_conv_dw")
    (dval, dgate), _ = _rowwise_bwd(_fn_glu, glu_rows, [], [c1(dhg)], [BF16, BF16], tm=tm, name="o_glu_bwd")
    du1 = _mm(dval, w_oi[:, 0:2048], tb=True, name="o_in_dx0")
    du1 = _mm(dgate, w_oi[:, 2048:4096], tb=True, add=du1, name="o_in_dx1")
    du1 = _mm(dz1, w_oi[:, 4096:6144], tb=True, add=du1, name="o_in_dx2")
    g["o_w_in"] = jnp.concatenate([_mm(u1, dval, ta=True, name="o_in_dw0"), _mm(u1, dgate, ta=True, name="o_in_dw1"),
                                   _mm(u1, dz1, ta=True, name="o_in_dw2")], axis=1)
    (dx1,), (g["o_norm_pre"],) = _rowwise_bwd(_fn_rms, [c1(x1)], [c1(w["o_norm_pre"])], [c1(du1)], [F32],
                                              adds={0: c1(dx2)}, tm=tm, name="o_pre_bwd")

    (dout0,), (g["e_norm_post"],) = _rowwise_bwd(_fn_post, post_rows, [c1(w["e_norm_post"])], [c1(dx1)],
                                                 [None, BF16], tm=tm, name="e_post_bwd")
    dhmix = _mm(dout0, w_eo, tb=True, name="e_out_dx")
    g["e_w_out"] = _mm(hmix, dout0, ta=True, name="e_out_dw")
    (dy, dzs, do, dzf), (g["e_ssd_norm"],) = _rowwise_bwd(_fn_mix, mix_rows, mix_pars, [c1(dhmix)],
                                                        [F32, BF16, F32, BF16], tm=tm, name="e_mix_bwd")
    dox = _fox_dopack(do, o, name="e_fox_dopack")
    dq8, dk, dv, dcs, *got = _fox3_bwd(qkv, csum, bounds, dox, mrun, carry=comm.early_grads(g), name="e_fox_bwd")
    comm.got_early_grads(got)
    dlf = _pad_lanes(_cumsum_lanes(dcs.reshape(FOX_HEADS, s), reverse=True, name="e_cumsum_bwd").T)
    dxs, ddt, dbm, dcm, dalog, ddsk, *got = _ssd2_bwd(dt, xbc, alog, dsk, hsave, dy, carry=comm.early_sums(),
                                                       name="e_ssd_bwd")
    comm.got_early_sums(got)
    dxbc = jnp.concatenate([dxs, dbm, dcm], axis=1)
    (dxbc_pre, ddtp, dfp), (ddtb, dfgb) = _rowwise_bwd(_fn_act, act_rows, act_pars, [c1(dxbc), c1(ddt), c1(dlf)],
                                                      [F32, BF16, BF16], tm=tm, name="e_act_bwd")
    dxbc_raw = bf(_conv_bwd_x(dxbc_pre, w["e_conv_w"], name="e_conv_dx"))
    g["e_conv_w"], g["e_conv_b"] = _conv_bwd_w(xbc_raw, dxbc_pre, SSD_CONV, name="e_conv_dw")
    du0 = _mm(dzs, w_z[:, :D_SSD], tb=True, name="e_in_dx0")
    du0 = _mm(dzf, w_z[:, D_SSD:], tb=True, add=du0, name="e_in_dx1")
    du0 = _mm(dxbc_raw, w_xbc, tb=True, add=du0, name="e_in_dx2")
    eighth = FOX_HEADDIM ** -0.5
    du0 = _mm(dq8, w_qkv[:, :D_FOX] * eighth, tb=True, add=du0, name="e_in_dx3q")
    du0 = _mm(dk, w_qkv[:, D_FOX:2 * D_FOX], tb=True, add=du0, name="e_in_dx3k")
    du0 = _mm(dv, w_qkv[:, 2 * D_FOX:], tb=True, add=du0, name="e_in_dx3v")
    du0 = _mm(ddtp, w_dt, tb=True, add=du0, name="e_in_dx4")
    du0 = _mm(dfp, w_f, tb=True, add=du0, name="e_in_dx5")
    g["e_w_in"] = jnp.concatenate([
        _mm(u0, dzs, ta=True, name="e_in_dw0"), _mm(u0, dzf, ta=True, name="e_in_dw1"),
        _mm(u0, dxbc_raw, ta=True, name="e_in_dw2"), _mm(u0, ddtp, ta=True, name="e_in_dw3")[:, :SSD_HEADS],
        _mm(u0, dq8, ta=True, name="e_in_dw4q") * eighth, _mm(u0, dk, ta=True, name="e_in_dw4k"),
        _mm(u0, dv, ta=True, name="e_in_dw4v"), _mm(u0, dfp, ta=True, name="e_in_dw5")[:, :FOX_HEADS]], axis=1)
    (dx,), (g["e_norm_pre"],) = _rowwise_bwd(_fn_rms, [c1(x)], [c1(w["e_norm_pre"])], [c1(du0)], [F32],
                                             adds={0: c1(dx1)}, tm=tm, name="e_pre_bwd")
    g["e_dt_bias"], g["e_fgate_b"] = ddtb[:, :SSD_HEADS], dfgb[:, :FOX_HEADS]
    g["e_a_log"], g["e_d_skip"] = dalog[:, :SSD_HEADS], ddsk[:, :SSD_HEADS]
    return loss, dx, g


_WEIGHTS = ["e_norm_pre", "e_w_in", "e_conv_w", "e_conv_b", "e_dt_bias", "e_a_log", "e_d_skip", "e_fgate_b",
            "e_ssd_norm", "e_w_out", "e_norm_post", "o_norm_pre", "o_w_in", "o_conv_w", "o_conv_b", "o_ln_g",
            "o_ln_b", "o_w_out", "o_norm_post"]
_BIG = ["e_w_in", "e_w_out", "o_w_in", "o_w_out"]
_ROW_SHARDED = ["e_w_out", "o_w_out"]
_SMALL_SHARDED = ["e_conv_w", "o_norm_pre", "o_conv_w", "o_conv_b", "o_ln_g", "o_ln_b", "o_norm_post"]
_REPLICATED = ["e_norm_pre", "e_conv_b", "e_dt_bias", "e_a_log", "e_d_skip", "e_fgate_b", "e_ssd_norm", "e_norm_post"]
_SMALL = [n for n in _WEIGHTS if n not in _BIG]
_EVEN_SHARDED = ["e_w_in", "e_w_out", "e_conv_w"]
_ODD_SHARDED = ["o_w_in", "o_w_out", "o_norm_pre", "o_conv_w", "o_conv_b", "o_ln_g", "o_ln_b", "o_norm_post"]
_EARLY_GRADS = ["o_w_in", "o_w_out", "e_w_out"]
N_CHIPS = 4


def _join(gathered, rows):
    k, r, c = gathered.shape
    return gathered.reshape(k * r, c) if rows else gathered.transpose(1, 0, 2).reshape(r, k * c)


def _split(full, rows):
    r, c = full.shape
    return full.reshape(N_CHIPS, r // N_CHIPS, c) if rows else full.reshape(r, N_CHIPS, c // N_CHIPS).transpose(1, 0, 2)


def kernel(x, e_norm_pre, e_w_in, e_conv_w, e_conv_b, e_dt_bias, e_a_log, e_d_skip, e_fgate_b, e_ssd_norm, e_w_out, e_norm_post, o_norm_pre, o_w_in, o_conv_w, o_conv_b, o_ln_g, o_ln_b, o_w_out, o_norm_post, loss_target, m_e_norm_pre, m_e_w_in, m_e_conv_w, m_e_conv_b, m_e_dt_bias, m_e_a_log, m_e_d_skip, m_e_fgate_b, m_e_ssd_norm, m_e_w_out, m_e_norm_post, m_o_norm_pre, m_o_w_in, m_o_conv_w, m_o_conv_b, m_o_ln_g, m_o_ln_b, m_o_w_out, m_o_norm_post, v_e_norm_pre, v_e_w_in, v_e_conv_w, v_e_conv_b, v_e_dt_bias, v_e_a_log, v_e_d_skip, v_e_fgate_b, v_e_ssd_norm, v_e_w_out, v_e_norm_post, v_o_norm_pre, v_o_w_in, v_o_conv_w, v_o_conv_b, v_o_ln_g, v_o_ln_b, v_o_w_out, v_o_norm_post):
    wvals = (e_norm_pre, e_w_in, e_conv_w, e_conv_b, e_dt_bias, e_a_log, e_d_skip, e_fgate_b, e_ssd_norm, e_w_out,
             e_norm_post, o_norm_pre, o_w_in, o_conv_w, o_conv_b, o_ln_g, o_ln_b, o_w_out, o_norm_post)
    mvals = (m_e_norm_pre, m_e_w_in, m_e_conv_w, m_e_conv_b, m_e_dt_bias, m_e_a_log, m_e_d_skip, m_e_fgate_b,
             m_e_ssd_norm, m_e_w_out, m_e_norm_post, m_o_norm_pre, m_o_w_in, m_o_conv_w, m_o_conv_b, m_o_ln_g,
             m_o_ln_b, m_o_w_out, m_o_norm_post)
    vvals = (v_e_norm_pre, v_e_w_in, v_e_conv_w, v_e_conv_b, v_e_dt_bias, v_e_a_log, v_e_d_skip, v_e_fgate_b,
             v_e_ssd_norm, v_e_w_out, v_e_norm_post, v_o_norm_pre, v_o_w_in, v_o_conv_w, v_o_conv_b, v_o_ln_g,
             v_o_ln_b, v_o_w_out, v_o_norm_post)

    def mat(v):
        return v.reshape(v.shape[-2:]) if v.ndim == 3 else v

    w = {n: mat(v) for n, v in zip(_WEIGHTS, wvals)}
    m = {n: mat(v) for n, v in zip(_WEIGHTS, mvals)}
    v2 = {n: mat(v) for n, v in zip(_WEIGHTS, vvals)}
    me_xy = 2 * lax.axis_index("x") + lax.axis_index("y")

    def shard(n):
        return w[n].astype(BF16) if n in _BIG else w[n]

    gathered = _exchange([shard(n) for n in _EVEN_SHARDED], "xy", False, name="gather_weights")
    full = {n: w[n] for n in _REPLICATED}
    for n, gth in zip(_EVEN_SHARDED, gathered):
        full[n] = _join(gth, n in _ROW_SHARDED)
    gparts = {}

    class _StepComm(_NoComm):
        def odd_weights(self):
            return _Exchange([shard(n) for n in _ODD_SHARDED], "xy", False)

        def got_odd_weights(self, got, wdict):
            for n, gth in zip(_ODD_SHARDED, got):
                wdict[n] = _join(gth, n in _ROW_SHARDED)

        def early_grads(self, g):
            return _Exchange([_split(g[n], n in _ROW_SHARDED).astype(BF16) for n in _EARLY_GRADS], "xy", True)

        def got_early_grads(self, got):
            self.sums = [_sum_slots(p, out_dtype=BF16, name="sum_" + n) for n, p in zip(_EARLY_GRADS, got)]

        def early_sums(self):
            return _Exchange(self.sums, "c", False)

        def got_early_sums(self, got):
            gparts.update(zip(_EARLY_GRADS, got))

    loss, dx, g = _local_step(x[0], loss_target[0], full, _StepComm())
    loss = lax.psum(loss[0, 0], ("x", "y", "c"))

    flat = jnp.concatenate([_pad_lanes(g[n].reshape(1, -1)) for n in _SMALL], axis=1).reshape(-1, LANES)
    (scattered,), (all8,) = _exchanges([_Exchange([_split(g["e_w_in"], False).astype(BF16)], "xy", True),
                                        _Exchange([flat], "xyc", False)], name="scatter_grads")
    (gparts["e_w_in"],) = _exchange([_sum_slots(scattered, out_dtype=BF16, name="sum_e_w_in")], "c", False,
                                    name="pair_grads")
    total = _sum_slots(all8, name="sum_small").reshape(1, -1)
    at = 0
    for n in _SMALL:
        size = g[n].size
        gn = total[:, at:at + size].reshape(g[n].shape)
        at += size + (-size % LANES)
        if n in _SMALL_SHARDED:
            cols = gn.shape[1] // N_CHIPS
            gn = lax.dynamic_slice(gn, (0, me_xy * cols), (gn.shape[0], cols))
        gparts[n] = gn[None]

    grads, deltas, new_m, new_v = [], [], [], []
    for n, orig in zip(_WEIGHTS, wvals):
        gn, dn, mn, vn = _adamw(w[n], gparts[n], m[n], v2[n], name="adamw_" + n)
        for lst, val in zip((grads, deltas, new_m, new_v), (gn, dn, mn, vn)):
            lst.append(val.reshape(orig.shape))
    return (loss, dx[None], *grads, *deltas, *new_m, *new_v)
```

```python
import functools

import jax
import jax.numpy as jnp
from jax import lax
from jax.experimental import pallas as pl
from jax.experimental.pallas import tpu as pltpu

F32 = jnp.float32
BF16 = jnp.bfloat16
MESH = pl.DeviceIdType.MESH

D_MODEL = 1024
D_SSD = 1024
SSD_HEADS = 16
SSD_HEADDIM = 64
SSD_GROUPS = 4
SSD_HPG = 4
D_STATE = 128
SSD_CONV = 4
CHUNK = 128
D_FOX = 1024
FOX_HEADS = 16
FOX_HEADDIM = 64
D_CONV = 2048
CONV_WIDTH = 31
EPS = 1e-6
LANES = 128
VMEM_LIMIT = 56 * 1024 * 1024

ADAM_LR = 0.001
ADAM_B1 = 0.9
ADAM_B2 = 0.999
ADAM_EPS = 1e-08
ADAM_WD = 0.01
ADAM_STEP = 10


def _cparams(sem=None):
    return pltpu.CompilerParams(dimension_semantics=sem, vmem_limit_bytes=VMEM_LIMIT)


def _mm(a, b, *, ta=False, tb=False, add=None, out_dtype=F32, tm=1024, tn=None, tk=2048, name):
    m = a.shape[1] if ta else a.shape[0]
    k = a.shape[0] if ta else a.shape[1]
    n = b.shape[0] if tb else b.shape[1]
    if tn is None:
        tn = 1024
    tm, tn = min(tm, m), min(tn, n)
    tk = max(t for t in range(LANES, min(tk, k) + 1, LANES) if k % t == 0)
    assert m % tm == 0 and n % tn == 0 and k % tk == 0, (m, n, k, tm, tn, tk)
    nk = k // tk
    dims = (((0 if ta else 1,), (1 if tb else 0,)), ((), ()))

    def body(*refs):
        if add is None:
            a_ref, b_ref, o_ref = refs[:3]
            c_ref = None
        else:
            a_ref, b_ref, c_ref, o_ref = refs[:4]
        kk = pl.program_id(2)
        prod = lax.dot_general(a_ref[...].astype(BF16), b_ref[...].astype(BF16), dims, preferred_element_type=F32)
        if nk == 1:
            o_ref[...] = (prod if c_ref is None else prod + c_ref[...].astype(F32)).astype(o_ref.dtype)
            return
        acc_ref = refs[-1]

        @pl.when(kk == 0)
        def _():
            acc_ref[...] = prod if c_ref is None else prod + c_ref[...].astype(F32)

        @pl.when((kk > 0) & (kk < nk - 1))
        def _():
            acc_ref[...] += prod

        @pl.when(kk == nk - 1)
        def _():
            o_ref[...] = (acc_ref[...] + prod).astype(o_ref.dtype)

    a_spec = (pl.BlockSpec((tk, tm), lambda j, i, kk: (kk, i)) if ta
              else pl.BlockSpec((tm, tk), lambda j, i, kk: (i, kk)))
    b_spec = (pl.BlockSpec((tn, tk), lambda j, i, kk: (j, kk)) if tb
              else pl.BlockSpec((tk, tn), lambda j, i, kk: (kk, j)))
    o_spec = pl.BlockSpec((tm, tn), lambda j, i, kk: (i, j))
    in_specs, args = [a_spec, b_spec], [a, b]
    if add is not None:
        in_specs.append(o_spec)
        args.append(add)
    return pl.pallas_call(
        body, name=name, grid=(n // tn, m // tm, nk),
        in_specs=in_specs, out_specs=o_spec,
        out_shape=jax.ShapeDtypeStruct((m, n), out_dtype),
        scratch_shapes=[pltpu.VMEM((tm, tn), F32)] if nk > 1 else [],
        compiler_params=_cparams(("parallel", "parallel", "arbitrary")),
    )(*args)


def _col(arr, cb, width):
    return (arr, cb, width)


def _row_specs(ops, tm):
    return [pl.BlockSpec((tm, w), lambda i, cb=cb: (i, cb)) for (_, cb, w) in ops]


def _par_specs(ops):
    return [pl.BlockSpec((a.shape[0], w), lambda i, cb=cb: (0, cb)) for (a, cb, w) in ops]


def _rowwise_fwd(fn, rows, params, outs, *, tm, name):
    s = rows[0][0].shape[0]
    tm = min(tm, s)
    nr, npar = len(rows), len(params)

    def body(*refs):
        rv = [r[...].astype(F32) for r in refs[:nr]]
        pv = [p[...].astype(F32) for p in refs[nr:nr + npar]]
        res = fn(*rv, *pv)
        for o_ref, val in zip(refs[nr + npar:], res):
            o_ref[...] = val.astype(o_ref.dtype)

    return pl.pallas_call(
        body, name=name, grid=(s // tm,),
        in_specs=_row_specs(rows, tm) + _par_specs(params),
        out_specs=[pl.BlockSpec((tm, w), lambda i: (i, 0)) for (w, _) in outs],
        out_shape=[jax.ShapeDtypeStruct((s, w), dt) for (w, dt) in outs],
        compiler_params=_cparams(("parallel",)),
    )(*[r[0] for r in rows], *[p[0] for p in params])


def _rowwise_bwd(fn, rows, params, couts, row_grads, *, adds=None, tm, name):
    adds = adds or {}
    s = rows[0][0].shape[0]
    tm = min(tm, s)
    nr, npar, nc = len(rows), len(params), len(couts)
    add_keys = sorted(adds)
    want = [i for i, dt in enumerate(row_grads) if dt is not None]

    def body(*refs):
        i = pl.program_id(0)
        rv = [r[...].astype(F32) for r in refs[:nr]]
        pv = [p[...].astype(F32) for p in refs[nr:nr + npar]]
        cv = [c[...].astype(F32) for c in refs[nr + npar:nr + npar + nc]]
        av = {k: r[...].astype(F32) for k, r in zip(add_keys, refs[nr + npar + nc:nr + npar + nc + len(add_keys)])}
        orefs = refs[nr + npar + nc + len(add_keys):]
        _, vjp = jax.vjp(lambda rr, pp: tuple(fn(*rr, *pp)), rv, pv)
        drows, dpars = vjp(tuple(cv))
        for o_ref, ri in zip(orefs[:len(want)], want):
            g = drows[ri]
            if ri in av:
                g = g + av[ri]
            o_ref[...] = g.astype(o_ref.dtype)

        @pl.when(i == 0)
        def _():
            for o_ref in orefs[len(want):]:
                o_ref[...] = jnp.zeros_like(o_ref)

        for o_ref, g in zip(orefs[len(want):], dpars):
            o_ref[...] += g

    add_ops = [adds[k] for k in add_keys]
    out_specs = ([pl.BlockSpec((tm, rows[ri][2]), lambda i: (i, 0)) for ri in want]
                 + [pl.BlockSpec((p[0].shape[0], p[2]), lambda i: (0, 0)) for p in params])
    out_shape = ([jax.ShapeDtypeStruct((s, rows[ri][2]), row_grads[ri]) for ri in want]
                 + [jax.ShapeDtypeStruct((p[0].shape[0], p[2]), F32) for p in params])
    res = pl.pallas_call(
        body, name=name, grid=(s // tm,),
        in_specs=_row_specs(rows, tm) + _par_specs(params) + _row_specs(couts, tm) + _row_specs(add_ops, tm),
        out_specs=out_specs, out_shape=out_shape,
        compiler_params=_cparams(("arbitrary",)),
    )(*[r[0] for r in rows], *[p[0] for p in params], *[c[0] for c in couts], *[a[0] for a in add_ops])
    return res[:len(want)], res[len(want):]


def _silu(v):
    return v * jax.nn.sigmoid(v)


def _rms(v, g):
    return v * lax.rsqrt(jnp.mean(v * v, axis=-1, keepdims=True) + EPS) * g


SUBLANES = 8
CONV_ROWS = 256


def _halo(shifts):
    up = lambda v: -(-v // SUBLANES) * SUBLANES
    return up(max(0, -min(shifts))), up(max(0, max(shifts)))


def _fill_halo(xp_sc, x_ref, front, back):
    s = x_ref.shape[0]
    if front:
        xp_sc[0:front, :] = jnp.zeros((front, LANES), F32)
    if back:
        xp_sc[front + s:front + s + back, :] = jnp.zeros((back, LANES), F32)
    xp_sc[front:front + s, :] = x_ref[...].astype(F32)


def _shift_conv(x, w, b, shifts, *, name):
    s, c = x.shape
    tr = min(CONV_ROWS, s)
    nk = len(shifts)
    front, back = _halo(shifts)

    def body(*refs):
        if b is None:
            x_ref, w_ref, o_ref, xp_sc = refs
        else:
            x_ref, w_ref, b_ref, o_ref, xp_sc = refs
        _fill_halo(xp_sc, x_ref, front, back)

        def chunk(r, carry):
            base = pl.multiple_of(r * tr, tr)
            acc = jnp.zeros((tr, LANES), F32) if b is None else jnp.broadcast_to(b_ref[...], (tr, LANES))
            for kk in range(nk):
                acc = acc + xp_sc[pl.ds(base + front + shifts[kk], tr), :] * w_ref[kk:kk + 1, :]
            o_ref[pl.ds(base, tr), :] = acc
            return carry

        lax.fori_loop(0, s // tr, chunk, 0)

    strip = pl.BlockSpec((s, LANES), lambda cb: (0, cb))
    in_specs = [strip, pl.BlockSpec((nk, LANES), lambda cb: (0, cb))]
    args = [x, w]
    if b is not None:
        in_specs.append(pl.BlockSpec((1, LANES), lambda cb: (0, cb)))
        args.append(b)
    return pl.pallas_call(
        body, name=name, grid=(c // LANES,), in_specs=in_specs, out_specs=strip,
        out_shape=jax.ShapeDtypeStruct((s, c), F32),
        scratch_shapes=[pltpu.VMEM((front + s + back, LANES), F32)],
        compiler_params=_cparams(("parallel",)),
    )(*args)


def _conv_fwd(x, w, b, *, name):
    k = w.shape[0]
    return _shift_conv(x, w, b, [kk - (k - 1) for kk in range(k)], name=name)


def _conv_bwd_x(dy, w, *, name):
    k = w.shape[0]
    return _shift_conv(dy, w, None, [(k - 1) - kk for kk in range(k)], name=name)


def _conv_bwd_w(x, dy, k, *, name):
    s, c = x.shape
    tr = min(CONV_ROWS, s)
    shifts = [kk - (k - 1) for kk in range(k)]
    front, back = _halo(shifts)

    def fold(v):
        return jnp.sum(v.reshape(tr // SUBLANES, SUBLANES, LANES), axis=0)

    def body(x_ref, dy_ref, dw_ref, db_ref, xp_sc, dw_sc, db_sc):
        _fill_halo(xp_sc, x_ref, front, back)
        dw_sc[...] = jnp.zeros_like(dw_sc)
        db_sc[...] = jnp.zeros_like(db_sc)

        def chunk(r, carry):
            base = pl.multiple_of(r * tr, tr)
            dyv = dy_ref[pl.ds(base, tr), :]
            db_sc[...] += fold(dyv)
            for kk in range(k):
                dw_sc[kk] += fold(xp_sc[pl.ds(base + front + shifts[kk], tr), :] * dyv)
            return carry

        lax.fori_loop(0, s // tr, chunk, 0)
        db_ref[...] = jnp.sum(db_sc[...], axis=0, keepdims=True)
        for kk in range(k):
            dw_ref[kk:kk + 1, :] = jnp.sum(dw_sc[kk], axis=0, keepdims=True)

    strip = pl.BlockSpec((s, LANES), lambda cb: (0, cb))
    return pl.pallas_call(
        body, name=name, grid=(c // LANES,), in_specs=[strip, strip],
        out_specs=[pl.BlockSpec((k, LANES), lambda cb: (0, cb)), pl.BlockSpec((1, LANES), lambda cb: (0, cb))],
        out_shape=[jax.ShapeDtypeStruct((k, c), F32), jax.ShapeDtypeStruct((1, c), F32)],
        scratch_shapes=[pltpu.VMEM((front + s + back, LANES), F32), pltpu.VMEM((k, SUBLANES, LANES), F32),
                        pltpu.VMEM((SUBLANES, LANES), F32)],
        compiler_params=_cparams(("parallel",)),
    )(x, dy)


_DIMS = {"nn": ((1,), (0,)), "nt": ((1,), (1,)), "tn": ((0,), (0,))}


def _bd(a, b, mode):
    return lax.dot_general(a.astype(BF16), b.astype(BF16), (_DIMS[mode], ((), ())), preferred_element_type=F32)


@functools.partial(jax.custom_vjp, nondiff_argnums=(2,))
def _bdot(a, b, mode):
    return _bd(a, b, mode)


def _bdot_fwd(a, b, mode):
    return _bd(a, b, mode), (a, b)


def _bdot_bwd(mode, res, g):
    a, b = res
    if mode == "nn":
        return _bd(g, b, "nt"), _bd(a, g, "tn")
    if mode == "nt":
        return _bd(g, b, "nn"), _bd(g, a, "tn")
    return _bd(b, g, "nt"), _bd(a, g, "nn")


_bdot.defvjp(_bdot_fwd, _bdot_bwd)


def _split3(v):
    hi = v.astype(BF16)
    r1 = v - hi.astype(F32)
    mid = r1.astype(BF16)
    lo = (r1 - mid.astype(F32)).astype(BF16)
    return hi, mid, lo


def _mask_dot(mask01, v, mode, terms=3):
    out = None
    for part in _split3(v)[:terms]:
        if mode == "vn":
            t = lax.dot_general(part, mask01, (_DIMS["nn"], ((), ())), preferred_element_type=F32)
        else:
            t = lax.dot_general(mask01, part, (_DIMS[mode], ((), ())), preferred_element_type=F32)
        out = t if out is None else out + t
    return out


def _lower_tri(n):
    r = lax.broadcasted_iota(jnp.int32, (n, n), 0)
    c = lax.broadcasted_iota(jnp.int32, (n, n), 1)
    return (r >= c).astype(BF16)


@jax.custom_vjp
def _tri_dot(w):
    return _mask_dot(_lower_tri(w.shape[0]), w, "nn", terms=2)


def _tri_dot_fwd(w):
    return _tri_dot(w), None


def _tri_dot_bwd(_, g):
    return (_mask_dot(_lower_tri(g.shape[0]), g, "tn", terms=2),)


_tri_dot.defvjp(_tri_dot_fwd, _tri_dot_bwd)


def _cumsum_lanes(x, *, reverse, name):
    h, s = x.shape
    n = s // LANES

    def body(x_ref, o_ref):
        r = lax.broadcasted_iota(jnp.int32, (LANES, LANES), 0)
        c = lax.broadcasted_iota(jnp.int32, (LANES, LANES), 1)
        m01 = ((r >= c) if reverse else (r <= c)).astype(BF16)

        def step(t, carry):
            ci = (n - 1 - t) if reverse else t
            at = pl.ds(pl.multiple_of(ci * LANES, LANES), LANES)
            blk = x_ref[:, at]
            o_ref[:, at] = _mask_dot(m01, blk, "vn") + carry
            return carry + jnp.sum(blk, axis=1, keepdims=True)

        lax.fori_loop(0, n, step, jnp.zeros((h, 1), F32))

    return pl.pallas_call(body, name=name, out_shape=jax.ShapeDtypeStruct((h, s), F32),
                          compiler_params=_cparams())(x)


SSD_PAIRS = SSD_HPG // 2


def _ssd2_chunk(xs, dt, bm, cm, hin, a, dsk, head0):
    n = CHUNK
    row = lax.broadcasted_iota(jnp.int32, (n, n), 0)
    col = lax.broadcasted_iota(jnp.int32, (n, n), 1)
    lower = row >= col
    ustrict = (row > col).astype(F32)
    lane = lax.broadcasted_iota(jnp.int32, (1, LANES), 1)
    sub = lax.broadcasted_iota(jnp.int32, (n, 1), 0)
    e_first = (sub == 0).astype(F32)
    e_last = (sub == n - 1).astype(F32)
    lane0 = (lane == 0).astype(F32)
    half_l = [(lane < LANES // 2).astype(F32), (lane >= LANES // 2).astype(F32)]
    half_s = [(sub < LANES // 2).astype(F32), (sub >= LANES // 2).astype(F32)]
    cb = _bdot(cm, bm, "nt")
    da = dt * (-jnp.exp(a))
    ys, houts = [], []
    for pr in range(SSD_PAIRS):
        y = jnp.zeros((n, LANES), F32)
        xdte = jnp.zeros((n, LANES), F32)
        lane_gain = jnp.zeros((n, LANES), F32)
        row_gain = jnp.zeros((LANES, 1), F32)
        for hf in range(2):
            oh = (lane == head0 + 2 * pr + hf).astype(F32)
            dt_col = jnp.sum(dt * oh, axis=1, keepdims=True)
            da_col = jnp.sum(da * oh, axis=1, keepdims=True)
            dsk_h = jnp.sum(dsk * oh, axis=1, keepdims=True)
            seg = _tri_dot(da_col * ustrict)
            decay = jnp.where(lower, jnp.exp(seg), 0.0)
            cs_col = jnp.sum(seg * lane0, axis=1, keepdims=True) + jnp.sum(da_col * e_first, axis=0, keepdims=True)
            total = jnp.sum(cs_col * e_last, axis=0, keepdims=True)
            xh = xs[pr] * half_l[hf]
            xd = xh * dt_col
            y = y + _bdot(cb * decay, xd, "nn") + xh * dsk_h
            xdte = xdte + xd * jnp.exp(total - cs_col)
            lane_gain = lane_gain + jnp.exp(cs_col) * half_l[hf]
            row_gain = row_gain + jnp.exp(total) * half_s[hf]
        houts.append(hin[pr] * row_gain + _bdot(xdte, bm, "tn"))
        ys.append(y + _bdot(cm, hin[pr], "nt") * lane_gain)
    return ys, houts


SSD_STEP = 4


def _ssd2_steps(s):
    per = min(SSD_STEP, s // CHUNK)
    return per, s // (CHUNK * per)


def _ssd2_multi(xs, dt, bm, cm, hin, a, dsk, head0):
    ys = []
    for k in range(len(dt)):
        y, hin = _ssd2_chunk(xs[k], dt[k], bm[k], cm[k], hin, a, dsk, head0)
        ys.append(y)
    return ys, hin


def _ssd2_specs(per, nc, rev):
    cc = (lambda c: nc - 1 - c) if rev else (lambda c: c)
    rows = CHUNK * per
    act = pl.BlockSpec((rows, SSD_PAIRS * LANES), lambda c, g: (cc(c), g))
    row = pl.BlockSpec((rows, LANES), lambda c, g: (cc(c), 0))
    bmat = pl.BlockSpec((rows, LANES), lambda c, g: (cc(c), D_SSD // LANES + g))
    cmat = pl.BlockSpec((rows, LANES), lambda c, g: (cc(c), D_SSD // LANES + SSD_GROUPS + g))
    par = pl.BlockSpec((1, LANES), lambda c, g: (0, 0))
    hs = pl.BlockSpec((1, SSD_PAIRS, LANES, D_STATE), lambda c, g: (cc(c), g, 0, 0))
    return act, row, bmat, cmat, par, hs


def _chunk_rows(ref, k):
    return ref[k * CHUNK:(k + 1) * CHUNK, :]


def _pair_cols(ref, k):
    return [ref[k * CHUNK:(k + 1) * CHUNK, pr * LANES:(pr + 1) * LANES] for pr in range(SSD_PAIRS)]


def _ssd2_fwd(dt, xbc, a, dsk, *, name):
    s = xbc.shape[0]
    per, nc = _ssd2_steps(s)
    act, row, bmat, cmat, par, hs = _ssd2_specs(per, nc, False)

    def body(xs_ref, dt_ref, bm_ref, cm_ref, a_ref, dsk_ref, y_ref, hs_ref, h_sc):
        c, g = pl.program_id(0), pl.program_id(1)

        @pl.when(c == 0)
        def _():
            h_sc[pl.ds(g * SSD_PAIRS, SSD_PAIRS)] = jnp.zeros((SSD_PAIRS, LANES, D_STATE), F32)

        hin = [h_sc[g * SSD_PAIRS + pr] for pr in range(SSD_PAIRS)]
        ks = range(per)
        ys, houts = _ssd2_multi([_pair_cols(xs_ref, k) for k in ks], [_chunk_rows(dt_ref, k) for k in ks],
                                [_chunk_rows(bm_ref, k) for k in ks], [_chunk_rows(cm_ref, k) for k in ks],
                                hin, a_ref[...], dsk_ref[...], g * SSD_HPG)
        for pr in range(SSD_PAIRS):
            for k in ks:
                y_ref[k * CHUNK:(k + 1) * CHUNK, pr * LANES:(pr + 1) * LANES] = ys[k][pr]
            hs_ref[0, pr] = hin[pr]
            h_sc[g * SSD_PAIRS + pr] = houts[pr]

    return pl.pallas_call(
        body, name=name, grid=(nc, SSD_GROUPS),
        in_specs=[act, row, bmat, cmat, par, par], out_specs=[act, hs],
        out_shape=[jax.ShapeDtypeStruct((s, D_SSD), F32),
                   jax.ShapeDtypeStruct((nc, SSD_HEADS // 2, LANES, D_STATE), F32)],
        scratch_shapes=[pltpu.VMEM((SSD_HEADS // 2, LANES, D_STATE), F32)],
        compiler_params=_cparams(("arbitrary", "arbitrary")),
    )(xbc, dt, xbc, xbc, a, dsk)


def _ssd2_bwd(dt, xbc, a, dsk, hsave, dy, *, carry=None, name):
    s = xbc.shape[0]
    per, nc = _ssd2_steps(s)
    act, row, bmat, cmat, par, hs = _ssd2_specs(per, nc, True)
    gmat = pl.BlockSpec((CHUNK * per, LANES), lambda c, g: (nc - 1 - c, g))

    def body(*refs):
        ins, outs, (dh_sc,), comm = _carried(carry, refs, 8, 6, 1)
        xs_ref, dt_ref, bm_ref, cm_ref, a_ref, dsk_ref, hs_ref, dy_ref = ins
        dxs_ref, ddt_ref, dbm_ref, dcm_ref, da_ref, ddsk_ref = outs
        c, g = pl.program_id(0), pl.program_id(1)
        if carry is not None:
            @pl.when((c == 0) & (g == 0))
            def _():
                carry.start(*comm)

        @pl.when(c == 0)
        def _():
            dh_sc[pl.ds(g * SSD_PAIRS, SSD_PAIRS)] = jnp.zeros((SSD_PAIRS, LANES, D_STATE), F32)

        @pl.when((c == 0) & (g == 0))
        def _():
            da_ref[...] = jnp.zeros_like(da_ref)
            ddsk_ref[...] = jnp.zeros_like(ddsk_ref)

        @pl.when(g == 0)
        def _():
            ddt_ref[...] = jnp.zeros_like(ddt_ref)

        head0 = g * SSD_HPG
        ks = range(per)
        prim = ([_pair_cols(xs_ref, k) for k in ks], [_chunk_rows(dt_ref, k) for k in ks],
                [_chunk_rows(bm_ref, k) for k in ks], [_chunk_rows(cm_ref, k) for k in ks],
                [hs_ref[0, pr] for pr in range(SSD_PAIRS)], a_ref[...], dsk_ref[...])
        _, vjp = jax.vjp(lambda *p: _ssd2_multi(*p, head0), *prim)
        cot = ([_pair_cols(dy_ref, k) for k in ks], [dh_sc[g * SSD_PAIRS + pr] for pr in range(SSD_PAIRS)])
        dxs, ddt, dbm, dcm, dhin, da, ddsk = vjp(cot)
        for pr in range(SSD_PAIRS):
            for k in ks:
                dxs_ref[k * CHUNK:(k + 1) * CHUNK, pr * LANES:(pr + 1) * LANES] = dxs[k][pr]
            dh_sc[g * SSD_PAIRS + pr] = dhin[pr]
        for k in ks:
            ddt_ref[k * CHUNK:(k + 1) * CHUNK, :] += ddt[k]
            dbm_ref[k * CHUNK:(k + 1) * CHUNK, :] = dbm[k]
            dcm_ref[k * CHUNK:(k + 1) * CHUNK, :] = dcm[k]
        da_ref[...] += da
        ddsk_ref[...] += ddsk
        if carry is not None:
            @pl.when((c == nc - 1) & (g == SSD_GROUPS - 1))
            def _():
                carry.wait(*comm)

    extra = carry if carry is not None else _Exchange([], "c", False)
    return pl.pallas_call(
        body, name=name, grid=(nc, SSD_GROUPS),
        in_specs=[act, row, bmat, cmat, par, par, hs, act] + extra.in_specs,
        out_specs=[act, row, gmat, gmat, par, par] + extra.out_specs,
        out_shape=[jax.ShapeDtypeStruct((s, D_SSD), F32), jax.ShapeDtypeStruct((s, LANES), F32),
                   jax.ShapeDtypeStruct((s, SSD_GROUPS * D_STATE), F32),
                   jax.ShapeDtypeStruct((s, SSD_GROUPS * D_STATE), F32),
                   jax.ShapeDtypeStruct((1, LANES), F32), jax.ShapeDtypeStruct((1, LANES), F32)] + extra.out_shape,
        scratch_shapes=[pltpu.VMEM((SSD_HEADS // 2, LANES, D_STATE), F32)]
        + (carry.scratch if carry is not None else []),
        compiler_params=_cparams(("arbitrary", "arbitrary")),
    )(xbc, dt, xbc, xbc, a, dsk, hsave, dy, *extra.arrs)


FOX_BLOCK = 512
NEG = -1e30
PAIRS = FOX_HEADS // 2
HALF = LANES // 2


def _causal(t):
    return lax.broadcasted_iota(jnp.int32, (t, t), 0) >= lax.broadcasted_iota(jnp.int32, (t, t), 1)


def _first_half():
    return lax.broadcasted_iota(jnp.int32, (1, LANES), 1) < HALF


def _pair_bias(c_ref, hh, qblock, kblock, t):
    lane = lax.broadcasted_iota(jnp.int32, (1, LANES), 1)
    cq = c_ref[hh, :, pl.ds(pl.multiple_of(qblock * t, LANES), LANES)]
    cref = jnp.sum(jnp.where(lane == 0, cq, 0.0), axis=1, keepdims=True)
    return cref - c_ref[hh, :, pl.ds(pl.multiple_of(kblock * t, LANES), t)]


def _fox_dopack(do, o, *, tm=256, name):
    s, d = do.shape
    tm = min(tm, s)

    def body(do_ref, o_ref, out_ref):
        dov = do_ref[...].astype(BF16)
        prod = dov.astype(F32) * o_ref[...]
        r = lax.broadcasted_iota(jnp.int32, (d, LANES), 0)
        c = lax.broadcasted_iota(jnp.int32, (d, LANES), 1)
        heads = ((r >= c * FOX_HEADDIM) & (r < (c + 1) * FOX_HEADDIM)).astype(BF16)
        negd = -_mask_dot(heads, prod, "vn")
        hr = lax.broadcasted_iota(jnp.int32, (LANES, 2 * d), 0)
        col = lax.broadcasted_iota(jnp.int32, (LANES, 2 * d), 1)
        base = (hr >> 1) * (2 * LANES) + jnp.where((hr & 1) == 0, HALF, LANES)
        terms = None
        for kk, part in enumerate(_split3(negd)):
            place = ((col == base + kk) & (hr < FOX_HEADS)).astype(BF16)
            tk = lax.dot_general(part, place, (_DIMS["nn"], ((), ())), preferred_element_type=F32)
            terms = tk if terms is None else terms + tk
        first = _first_half()
        zero = jnp.zeros((tm, LANES), BF16)
        pieces = []
        for hp in range(PAIRS):
            blk = dov[:, hp * LANES:(hp + 1) * LANES]
            pieces += [jnp.where(first, blk, zero), jnp.where(first, zero, blk)]
        out_ref[...] = (jnp.concatenate(pieces, axis=1).astype(F32) + terms).astype(BF16)

    row = pl.BlockSpec((tm, d), lambda i: (i, 0))
    return pl.pallas_call(body, name=name, grid=(s // tm,), in_specs=[row, row],
                          out_specs=pl.BlockSpec((tm, 2 * d), lambda i: (i, 0)),
                          out_shape=jax.ShapeDtypeStruct((s, 2 * d), BF16),
                          compiler_params=_cparams(("parallel",)))(do, o)


FOX_DEAD = 110.0
BOUND_SLACK = 1.001


def _fox_block_bounds(qkv, c):
    s = qkv.shape[0]
    t = min(FOX_BLOCK, s)
    k = qkv[:, D_FOX:2 * D_FOX].astype(F32).reshape(s // t, t, FOX_HEADS, FOX_HEADDIM)
    kn = jnp.sqrt(jnp.max(jnp.sum(k * k, axis=-1), axis=1)).T
    return kn, c[:, 0, ::t], c[:, 0, t - 1::t]


def _fox_live(qmax, mmin, kn_ref, cs_ref, ce_ref, h, qblock, kblock):
    bound = qmax * (kn_ref[h, kblock] * BOUND_SLACK) + (cs_ref[h, qblock] - ce_ref[h, kblock])
    return bound - mmin > -FOX_DEAD


def _run_live(live, work):
    @pl.when(live[0] & live[1])
    def _():
        work(0)
        work(1)

    @pl.when(live[0] & jnp.logical_not(live[1]))
    def _():
        work(0)

    @pl.when(jnp.logical_not(live[0]) & live[1])
    def _():
        work(1)


def _fox3_fwd(qkv, c, bounds, *, carry=None, name):
    assert qkv.shape[0] // min(FOX_BLOCK, qkv.shape[0]) < LANES - 1
    s = qkv.shape[0]
    t = min(FOX_BLOCK, s)
    nb = s // t
    nt = (((1,), (1,)), ((), ()))
    nn = (((1,), (0,)), ((), ()))
    scale = FOX_HEADDIM ** -0.5

    def body(*refs):
        ins, (o_ref, mrun_ref, qmax_ref, mmin_ref), (m_sc, acc_sc, mt_sc), comm = _carried(carry, refs, 7, 4, 3)
        q_ref, k_ref, v_ref, c_ref, kn_ref, cs_ref, ce_ref = ins
        hp, i = pl.program_id(0), pl.program_id(1)
        if carry is not None:
            @pl.when((hp == 0) & (i == 0))
            def _():
                carry.start(*comm)

        first = _first_half()
        lane = lax.broadcasted_iota(jnp.int32, (1, LANES), 1)
        m_sc[...] = jnp.full_like(m_sc, NEG)
        acc_sc[...] = jnp.zeros_like(acc_sc)
        mt_sc[...] = jnp.zeros_like(mt_sc)
        q2 = q_ref[...] * scale
        zero = jnp.zeros_like(q2)
        qs = [jnp.where(first, q2, zero), jnp.where(first, zero, q2)]

        def head_block(hh, j, k2, vxh, masked):
            sc = lax.dot_general(qs[hh], k2, nt, preferred_element_type=F32) + _pair_bias(c_ref, hh, i, j, t)
            if masked:
                sc = jnp.where(_causal(t), sc, NEG)
            m_prev = m_sc[hh]
            m_new = jnp.maximum(m_prev, jnp.max(sc, axis=1, keepdims=True))
            pr = jnp.exp(sc - jnp.tile(m_new, (1, t // LANES))).astype(BF16)
            pv = lax.dot_general(pr, vxh, nn, preferred_element_type=F32)
            acc_sc[hh] = jnp.exp(m_prev - m_new) * acc_sc[hh] + pv
            m_sc[hh] = m_new

        def step(j, masked):
            at = pl.ds(pl.multiple_of(j * t, t), t)
            k2, v2 = k_ref[at, :], v_ref[at, :]
            one = jnp.ones_like(v2)
            vx = [jnp.where(first, v2, one), jnp.where(first, one, v2)]
            if masked:
                for hh in range(2):
                    head_block(hh, j, k2, vx[hh], True)
            else:
                live = [_fox_live(qmax_ref[2 * hp + hh, i], mmin_ref[2 * hp + hh, i], kn_ref, cs_ref, ce_ref,
                                  2 * hp + hh, i, j) for hh in range(2)]
                _run_live(live, lambda hh: head_block(hh, j, k2, vx[hh], False))
            for hh in range(2):
                mt_sc[hh] = jnp.where(lane == j, m_sc[hh], mt_sc[hh])

        step(i, True)
        for hh in range(2):
            qf = qs[hh].astype(F32)
            qmax_ref[2 * hp + hh, i] = jnp.sqrt(jnp.max(jnp.sum(qf * qf, axis=1, keepdims=True)))
            mmin_ref[2 * hp + hh, i] = jnp.min(m_sc[hh])
        lax.fori_loop(0, i, lambda n, carry: (step(i - 1 - n, False), carry)[1], 0)
        acc_a, acc_b = acc_sc[0], acc_sc[1]
        den_a = jnp.where(first, pltpu.roll(acc_a, HALF, 1), acc_a)
        den_b = jnp.where(first, acc_b, pltpu.roll(acc_b, HALF, 1))
        o_ref[...] = jnp.where(first, acc_a / den_a, acc_b / den_b)
        for hh, den in enumerate((den_a, den_b)):
            mrun_ref[:, hh * LANES:(hh + 1) * LANES] = jnp.where(lane == LANES - 1, m_sc[hh] + jnp.log(den), mt_sc[hh])
        if carry is not None:
            @pl.when((pl.program_id(0) == PAIRS - 1) & (i == nb - 1))
            def _():
                carry.wait(*comm)

    nq = D_FOX // LANES
    smem = pl.BlockSpec(memory_space=pltpu.SMEM)
    extra = carry if carry is not None else _Exchange([], "c", False)
    return pl.pallas_call(
        body, name=name, grid=(PAIRS, nb),
        in_specs=[pl.BlockSpec((t, LANES), lambda hp, i: (i, hp)),
                  pl.BlockSpec((s, LANES), lambda hp, i: (0, nq + hp)),
                  pl.BlockSpec((s, LANES), lambda hp, i: (0, 2 * nq + hp)),
                  pl.BlockSpec((2, 1, s), lambda hp, i: (hp, 0, 0))] + [smem] * 3 + extra.in_specs,
        out_specs=[pl.BlockSpec((t, LANES), lambda hp, i: (i, hp)),
                   pl.BlockSpec((t, 2 * LANES), lambda hp, i: (i, hp)), smem, smem] + extra.out_specs,
        out_shape=[jax.ShapeDtypeStruct((s, D_FOX), F32), jax.ShapeDtypeStruct((s, 2 * D_FOX), F32),
                   jax.ShapeDtypeStruct((FOX_HEADS, nb), F32), jax.ShapeDtypeStruct((FOX_HEADS, nb), F32)]
        + extra.out_shape,
        scratch_shapes=[pltpu.VMEM((2, t, LANES), F32)] * 3 + (carry.scratch if carry is not None else []),
        compiler_params=_cparams(("arbitrary", "arbitrary")),
    )(qkv, qkv, qkv, c, *bounds, *extra.arrs)


def _fox3_bwd(qkv, c, bounds, dox, mrun, *, carry=None, name):
    s = qkv.shape[0]
    t = min(FOX_BLOCK, s)
    nb = s // t
    nt = (((1,), (1,)), ((), ()))
    nn = (((1,), (0,)), ((), ()))
    tn = (((0,), (0,)), ((), ()))
    scale = FOX_HEADDIM ** -0.5

    def body(*refs):
        ins, outs, scratch, comm = _carried(carry, refs, 11, 4, 3)
        k_ref, v_ref, q_ref, c_ref, do_ref, mrun_ref, kn_ref, cs_ref, ce_ref, qmax_ref, mmin_ref = ins
        dq_ref, dk_ref, dv_ref, dc_ref = outs
        dk_sc, dv_sc, dc_sc = scratch
        hp, j = pl.program_id(0), pl.program_id(1)
        if carry is not None:
            @pl.when((hp == 0) & (j == 0))
            def _():
                carry.start(*comm)

        first = _first_half()
        halves = [first, jnp.logical_not(first)]
        lane = lax.broadcasted_iota(jnp.int32, (1, LANES), 1)

        @pl.when(j == 0)
        def _():
            dq_ref[...] = jnp.zeros_like(dq_ref)

        dk_sc[...] = jnp.zeros_like(dk_sc)
        dv_sc[...] = jnp.zeros_like(dv_sc)
        dc_sc[...] = jnp.zeros_like(dc_sc)
        k2, v2 = k_ref[...], v_ref[...]
        one = jnp.ones_like(v2)
        vx = [jnp.where(first, v2, one), jnp.where(first, one, v2)]

        def pick(stats, which):
            return jnp.sum(jnp.where(lane == which, stats, 0.0), axis=1, keepdims=True)

        def head_block(hh, i, at, qsh, stats, masked):
            mine = slice(hh * LANES, (hh + 1) * LANES)
            dov = do_ref[at, mine]
            sc = lax.dot_general(qsh, k2, nt, preferred_element_type=F32) + _pair_bias(c_ref, hh, i, j, t)
            mj = pick(stats, j)
            gain = jnp.broadcast_to(jnp.exp(mj - pick(stats, LANES - 1)), (t, LANES))
            mj = jnp.broadcast_to(mj, (t, LANES))
            pb = jnp.exp(sc - jnp.tile(mj, (1, t // LANES))).astype(BF16)
            if masked:
                pb = jnp.where(_causal(t), pb, jnp.zeros_like(pb))
            pr = pb.astype(F32) * jnp.tile(gain, (1, t // LANES))
            ds = pr * lax.dot_general(dov, vx[hh], nt, preferred_element_type=F32)
            dc_sc[hh] -= jnp.sum(ds, axis=0, keepdims=True)
            dsb = ds.astype(BF16)
            dvh = lax.dot_general(pr.astype(BF16), dov, tn, preferred_element_type=F32)
            dv_sc[...] += jnp.where(halves[hh], dvh, 0.0)
            dk_sc[...] += lax.dot_general(dsb, qsh, tn, preferred_element_type=F32)
            dqh = lax.dot_general(dsb, k2, nn, preferred_element_type=F32)
            dq_ref[at, :] += jnp.where(halves[hh], dqh, 0.0)

        def step(i, masked):
            at = pl.ds(pl.multiple_of(i * t, t), t)
            q2 = q_ref[at, :] * scale
            zero = jnp.zeros_like(q2)
            qs = [jnp.where(first, q2, zero), jnp.where(first, zero, q2)]
            stats = [mrun_ref[at, hh * LANES:(hh + 1) * LANES] for hh in range(2)]
            if masked:
                for hh in range(2):
                    head_block(hh, i, at, qs[hh], stats[hh], True)
            else:
                live = [_fox_live(qmax_ref[2 * hp + hh, i], mmin_ref[2 * hp + hh, i], kn_ref, cs_ref, ce_ref,
                                  2 * hp + hh, i, j) for hh in range(2)]
                _run_live(live, lambda hh: head_block(hh, i, at, qs[hh], stats[hh], False))

        step(j, True)
        lax.fori_loop(j + 1, nb, lambda i, carry: (step(i, False), carry)[1], 0)
        dk_ref[...] = dk_sc[...]
        dv_ref[...] = dv_sc[...].astype(dv_ref.dtype)
        dc_ref[...] = dc_sc[...]
        if carry is not None:
            @pl.when((pl.program_id(0) == PAIRS - 1) & (j == nb - 1))
            def _():
                carry.wait(*comm)

    nq = D_FOX // LANES
    blk = pl.BlockSpec((t, LANES), lambda hp, j: (j, hp))
    smem = pl.BlockSpec(memory_space=pltpu.SMEM)
    extra = carry if carry is not None else _Exchange([], "c", False)
    return pl.pallas_call(
        body, name=name, grid=(PAIRS, nb),
        in_specs=[pl.BlockSpec((t, LANES), lambda hp, j: (j, nq + hp)),
                  pl.BlockSpec((t, LANES), lambda hp, j: (j, 2 * nq + hp)),
                  pl.BlockSpec((s, LANES), lambda hp, j: (0, hp)),
                  pl.BlockSpec((2, 1, s), lambda hp, j: (hp, 0, 0)),
                  pl.BlockSpec((s, 2 * LANES), lambda hp, j: (0, hp)),
                  pl.BlockSpec((s, 2 * LANES), lambda hp, j: (0, hp))] + [smem] * 5 + extra.in_specs,
        out_specs=[pl.BlockSpec((s, LANES), lambda hp, j: (0, hp)), blk, blk,
                   pl.BlockSpec((2, 1, t), lambda hp, j: (hp, 0, j))] + extra.out_specs,
        out_shape=[jax.ShapeDtypeStruct((s, D_FOX), F32), jax.ShapeDtypeStruct((s, D_FOX), F32),
                   jax.ShapeDtypeStruct((s, D_FOX), BF16), jax.ShapeDtypeStruct((FOX_HEADS, 1, s), F32)]
        + extra.out_shape,
        scratch_shapes=[pltpu.VMEM((t, LANES), F32), pltpu.VMEM((t, LANES), F32), pltpu.VMEM((2, 1, t), F32)]
        + (carry.scratch if carry is not None else []),
        compiler_params=_cparams(("arbitrary", "arbitrary")),
    )(qkv, qkv, qkv, c, dox, mrun, *bounds, *extra.arrs)


def _final(x1, out1, g, tgt, *, tm=256, name):
    s, d = x1.shape
    tm = min(tm, s)

    def body(x_ref, o_ref, g_ref, t_ref, dx_ref, do_ref, dg_ref, loss_ref):
        i = pl.program_id(0)

        @pl.when(i == 0)
        def _():
            dg_ref[...] = jnp.zeros_like(dg_ref)
            loss_ref[...] = jnp.zeros_like(loss_ref)

        tv = t_ref[...]

        def lossf(xv, ov, gv):
            err = jnp.square(xv + _rms(ov, gv) - tv)
            return 0.5 * jnp.sum(jnp.mean(err, axis=-1, keepdims=True), axis=0, keepdims=True)

        val, vjp = jax.vjp(lossf, x_ref[...], o_ref[...], g_ref[...])
        dx, do, dg = vjp(jnp.ones((1, 1), F32))
        dx_ref[...] = dx
        do_ref[...] = do.astype(do_ref.dtype)
        dg_ref[...] += dg
        loss_ref[...] += val

    row = pl.BlockSpec((tm, d), lambda i: (i, 0))
    par = pl.BlockSpec((1, d), lambda i: (0, 0))
    return pl.pallas_call(
        body, name=name, grid=(s // tm,), in_specs=[row, row, par, row],
        out_specs=[row, row, par, pl.BlockSpec((1, 1), lambda i: (0, 0))],
        out_shape=[jax.ShapeDtypeStruct((s, d), F32), jax.ShapeDtypeStruct((s, d), BF16),
                   jax.ShapeDtypeStruct((1, d), F32), jax.ShapeDtypeStruct((1, 1), F32)],
        compiler_params=_cparams(("arbitrary",)),
    )(x1, out1, g, tgt)


def _row_tile(r):
    return LANES if r % LANES == 0 else r


def _sum_slots(parts, *, out_dtype=F32, name):
    p, r, c = parts.shape
    tr = _row_tile(r)

    def body(p_ref, o_ref):
        acc = p_ref[0].astype(F32)
        for k in range(1, p):
            acc = acc + p_ref[k].astype(F32)
        o_ref[...] = acc.astype(o_ref.dtype)

    return pl.pallas_call(
        body, name=name, grid=(r // tr,),
        in_specs=[pl.BlockSpec((p, tr, c), lambda i: (0, i, 0))],
        out_specs=pl.BlockSpec((tr, c), lambda i: (i, 0)),
        out_shape=jax.ShapeDtypeStruct((r, c), out_dtype),
        compiler_params=_cparams(("parallel",)),
    )(parts)


def _adamw(w, gparts, m, v, *, name):
    r, c = w.shape
    p = gparts.shape[0]
    tr = _row_tile(r)

    def body(w_ref, g_ref, m_ref, v_ref, go_ref, d_ref, mo_ref, vo_ref):
        g = g_ref[0].astype(F32)
        for k in range(1, p):
            g = g + g_ref[k].astype(F32)
        mn = ADAM_B1 * m_ref[...] + (1.0 - ADAM_B1) * g
        vn = ADAM_B2 * v_ref[...] + (1.0 - ADAM_B2) * jnp.square(g)
        m_hat = mn / (1.0 - ADAM_B1 ** ADAM_STEP)
        v_hat = vn / (1.0 - ADAM_B2 ** ADAM_STEP)
        go_ref[...] = g
        d_ref[...] = -ADAM_LR * (m_hat / (jnp.sqrt(v_hat) + ADAM_EPS) + ADAM_WD * w_ref[...])
        mo_ref[...] = mn
        vo_ref[...] = vn

    spec = pl.BlockSpec((tr, c), lambda i: (i, 0))
    return pl.pallas_call(
        body, name=name, grid=(r // tr,),
        in_specs=[spec, pl.BlockSpec((p, tr, c), lambda i: (0, i, 0)), spec, spec],
        out_specs=[spec] * 4, out_shape=[jax.ShapeDtypeStruct((r, c), F32)] * 4,
        compiler_params=_cparams(("parallel",)),
    )(w, gparts, m, v)


_FLIPS = {
    "xy": [(1, 0, 0), (0, 1, 0), (1, 1, 0)],
    "c": [(0, 0, 1)],
    "xyc": [(fx, fy, fc) for fx in (0, 1) for fy in (0, 1) for fc in (0, 1) if (fx, fy, fc) != (0, 0, 0)],
}


def _slot(mode, px, py, pc):
    return {"xy": 2 * px + py, "c": pc, "xyc": 4 * px + 2 * py + pc}[mode]


class _Exchange:
    def __init__(self, arrs, mode, scatter):
        self.arrs, self.mode, self.scatter = list(arrs), mode, scatter
        self.n = len(self.arrs)
        self.flips = _FLIPS[mode]
        nf = len(self.flips)
        anyspec = pl.BlockSpec(memory_space=pl.ANY)
        self.in_specs = [anyspec] * self.n
        self.out_specs = [anyspec] * self.n
        self.out_shape = [jax.ShapeDtypeStruct((nf + 1,) + (a.shape[1:] if scatter else a.shape), a.dtype)
                          for a in self.arrs]
        self.scratch = [pltpu.SemaphoreType.DMA((self.n * nf,)), pltpu.SemaphoreType.DMA((self.n * nf,)),
                        pltpu.SemaphoreType.DMA((self.n,))]

    def _copies(self, ins, outs, sems, arrivals=True):
        send, recv, loc = sems
        nf = len(self.flips)
        x, y, c = lax.axis_index("x"), lax.axis_index("y"), lax.axis_index("c")
        me = _slot(self.mode, x, y, c)
        peers = [(x ^ fx, y ^ fy, c ^ fc) for (fx, fy, fc) in self.flips]

        def src(a, slot):
            return ins[a].at[slot] if self.scatter else ins[a]

        def copy(a, j, dst_slot):
            return pltpu.make_async_remote_copy(
                src_ref=src(a, _slot(self.mode, *peers[j])), dst_ref=outs[a].at[dst_slot],
                send_sem=send.at[a * nf + j], recv_sem=recv.at[a * nf + j], device_id=peers[j], device_id_type=MESH)

        pairs = [(a, j) for a in range(self.n) for j in range(nf)]
        local = [pltpu.make_async_copy(src(a, me), outs[a].at[me], loc.at[a]) for a in range(self.n)]
        sends = [copy(a, j, me) for a, j in pairs]
        recvs = [copy(a, j, _slot(self.mode, *peers[j])) for a, j in pairs] if arrivals else []
        return local, sends, recvs

    def start(self, ins, outs, sems):
        local, sends, _ = self._copies(ins, outs, sems, arrivals=False)
        for cp in local + sends:
            cp.start()

    def wait(self, ins, outs, sems):
        local, sends, recvs = self._copies(ins, outs, sems)
        for cp in recvs:
            cp.wait_recv()
        for cp in sends:
            cp.wait_send()
        for cp in local:
            cp.wait()


def _carried(carry, refs, n_in, n_out, n_scratch):
    k = carry.n if carry is not None else 0
    ins, refs = refs[:n_in], refs[n_in:]
    cin, refs = refs[:k], refs[k:]
    outs, refs = refs[:n_out], refs[n_out:]
    cout, refs = refs[:k], refs[k:]
    scratch, sems = refs[:n_scratch], refs[n_scratch:]
    return ins, outs, scratch, (cin, cout, sems)


def _exchanges(exs, *, name):
    counts = [ex.n for ex in exs]
    total = sum(counts)

    def body(*refs):
        ins, outs, sems = refs[:total], refs[total:2 * total], refs[2 * total:]
        comms, at = [], 0
        for k, ex in enumerate(exs):
            comms.append((ins[at:at + ex.n], outs[at:at + ex.n], sems[3 * k:3 * k + 3]))
            at += ex.n
        for ex, comm in zip(exs, comms):
            ex.start(*comm)
        for ex, comm in zip(exs, comms):
            ex.wait(*comm)

    res = pl.pallas_call(
        body, name=name, in_specs=[sp for ex in exs for sp in ex.in_specs],
        out_specs=[sp for ex in exs for sp in ex.out_specs], out_shape=[sh for ex in exs for sh in ex.out_shape],
        scratch_shapes=[sc for ex in exs for sc in ex.scratch])(*[a for ex in exs for a in ex.arrs])
    out, at = [], 0
    for n in counts:
        out.append(list(res[at:at + n]))
        at += n
    return out


def _exchange(arrs, mode, scatter, *, name):
    return _exchanges([_Exchange(arrs, mode, scatter)], name=name)[0]


def _softplus(v):
    return jnp.maximum(v, 0.0) + jnp.log1p(jnp.exp(-jnp.abs(v)))


def _pad_lanes(v):
    r, n = v.shape
    return jnp.pad(v, ((0, 0), (0, -n % LANES)))


def _fn_rms(v, g):
    return (_rms(v, g),)


def _fn_post(xv, ov, g):
    return (xv + _rms(ov, g),)


def _fn_act(xbc, dtp, fp, dtb, fb):
    return _silu(xbc), _softplus(dtp + dtb), -_softplus(-(fp + fb))


def _fn_mix(y, zs, o, zf, g):
    yg = y * _silu(zs)
    sq = yg * yg
    lane = lax.broadcasted_iota(jnp.int32, (1, D_SSD), 1)
    width = D_SSD // SSD_GROUPS
    rstd = jnp.zeros_like(yg)
    for gi in range(SSD_GROUPS):
        msk = ((lane >= gi * width) & (lane < (gi + 1) * width)).astype(F32)
        ms = jnp.sum(sq * msk, axis=1, keepdims=True) / width
        rstd = rstd + lax.rsqrt(ms + EPS) * msk
    return (jnp.concatenate([yg * rstd * g, o * _silu(zf)], axis=1),)


def _fn_glu(val, gate):
    return (val * jax.nn.sigmoid(gate),)


def _fn_ln(hc, z, g, b):
    mu = jnp.mean(hc, axis=-1, keepdims=True)
    xc = hc - mu
    yn = xc * lax.rsqrt(jnp.mean(xc * xc, axis=-1, keepdims=True) + EPS) * g + b
    return (_silu(yn) * _silu(z),)


class _NoComm:
    def odd_weights(self):
        return None

    def got_odd_weights(self, got, w):
        pass

    def early_grads(self, g):
        return None

    def got_early_grads(self, got):
        pass

    def early_sums(self):
        return None

    def got_early_sums(self, got):
        pass


def _local_step(x, tgt, w, comm=None):
    comm = comm or _NoComm()
    s = x.shape[0]
    d = D_MODEL
    tm = 256
    bf = lambda v: v.astype(BF16)
    c1 = lambda arr: _col(arr, 0, arr.shape[1])
    g = {}

    ew = w["e_w_in"]
    w_z, w_xbc = bf(ew[:, 0:2048]), bf(ew[:, 2048:4096])
    w_dt = bf(_pad_lanes(ew[:, 4096:4112]))
    w_qkv = bf(ew[:, 4112:7184])
    w_f = bf(_pad_lanes(ew[:, 7184:7200]))
    dtb, fgb = _pad_lanes(w["e_dt_bias"]), _pad_lanes(w["e_fgate_b"])
    alog, dsk = _pad_lanes(w["e_a_log"]), _pad_lanes(w["e_d_skip"])

    (u0,) = _rowwise_fwd(_fn_rms, [c1(x)], [c1(w["e_norm_pre"])], [(d, BF16)], tm=tm, name="e_pre")
    z = _mm(u0, w_z, name="e_in_z")
    xbc_raw = _mm(u0, w_xbc, out_dtype=BF16, name="e_in_xbc")
    qkv = _mm(u0, w_qkv, out_dtype=BF16, name="e_in_qkv")
    dtp = _mm(u0, w_dt, name="e_in_dt")
    fp = _mm(u0, w_f, name="e_in_f")
    xbc_pre = _conv_fwd(xbc_raw, w["e_conv_w"], w["e_conv_b"], name="e_conv")
    act_rows = [c1(xbc_pre), c1(dtp), c1(fp)]
    act_pars = [c1(dtb), c1(fgb)]
    xbc, dt, lf = _rowwise_fwd(_fn_act, act_rows, act_pars, [(2048, F32), (LANES, F32), (LANES, F32)],
                               tm=tm, name="e_act")
    y, hsave = _ssd2_fwd(dt, xbc, alog, dsk, name="e_ssd")
    csum = _cumsum_lanes(lf[:, :FOX_HEADS].T, reverse=False, name="e_cumsum").reshape(FOX_HEADS, 1, s)
    bounds = _fox_block_bounds(qkv, csum)
    o, mrun, qmax, mmin, *got = _fox3_fwd(qkv, csum, bounds, carry=comm.odd_weights(), name="e_fox")
    bounds = (*bounds, qmax, mmin)
    comm.got_odd_weights(got, w)
    w_eo, w_oi, w_oo = bf(w["e_w_out"]), bf(w["o_w_in"]), bf(w["o_w_out"])
    mix_rows = [c1(y), _col(z, 0, D_SSD), c1(o), _col(z, 1, D_FOX)]
    mix_pars = [c1(w["e_ssd_norm"])]
    (hmix,) = _rowwise_fwd(_fn_mix, mix_rows, mix_pars, [(2048, BF16)], tm=tm, name="e_mix")
    out0 = _mm(hmix, w_eo, name="e_out")
    post_rows = [c1(x), c1(out0)]
    (x1,) = _rowwise_fwd(_fn_post, post_rows, [c1(w["e_norm_post"])], [(d, F32)], tm=tm, name="e_post")

    (u1,) = _rowwise_fwd(_fn_rms, [c1(x1)], [c1(w["o_norm_pre"])], [(d, BF16)], tm=tm, name="o_pre")
    p1 = _mm(u1, w_oi, out_dtype=BF16, name="o_in")
    glu_rows = [_col(p1, 0, D_CONV), _col(p1, 1, D_CONV)]
    (hg,) = _rowwise_fwd(_fn_glu, glu_rows, [], [(D_CONV, BF16)], tm=tm, name="o_glu")
    hc = _conv_fwd(hg, w["o_conv_w"], w["o_conv_b"], name="o_conv")
    ln_rows = [c1(hc), _col(p1, 2, D_CONV)]
    ln_pars = [c1(w["o_ln_g"]), c1(w["o_ln_b"])]
    (h2,) = _rowwise_fwd(_fn_ln, ln_rows, ln_pars, [(D_CONV, BF16)], tm=tm, name="o_ln")
    out1 = _mm(h2, w_oo, name="o_out")

    dx2, dout1, g["o_norm_post"], loss = _final(x1, out1, w["o_norm_post"], tgt, name="loss_head")
    dh2 = _mm(dout1, w_oo, tb=True, name="o_out_dx")
    g["o_w_out"] = _mm(h2, dout1, ta=True, name="o_out_dw")
    (dhc, dz1), (g["o_ln_g"], g["o_ln_b"]) = _rowwise_bwd(_fn_ln, ln_rows, ln_pars, [c1(dh2)], [F32, BF16],
                                                         tm=tm, name="o_ln_bwd")
    dhg = _conv_bwd_x(dhc, w["o_conv_w"], name="o_conv_dx")
    g["o_conv_w"], g["o_conv_b"] = _conv_bwd_w(hg, dhc, CONV_WIDTH, name="o_conv_dw")
    (dval, dgate), _ = _rowwise_bwd(_fn_glu, glu_rows, [], [c1(dhg)], [BF16, BF16], tm=tm, name="o_glu_bwd")
    du1 = _mm(dval, w_oi[:, 0:2048], tb=True, name="o_in_dx0")
    du1 = _mm(dgate, w_oi[:, 2048:4096], tb=True, add=du1, name="o_in_dx1")
    du1 = _mm(dz1, w_oi[:, 4096:6144], tb=True, add=du1, name="o_in_dx2")
    g["o_w_in"] = jnp.concatenate([_mm(u1, dval, ta=True, name="o_in_dw0"), _mm(u1, dgate, ta=True, name="o_in_dw1"),
                                   _mm(u1, dz1, ta=True, name="o_in_dw2")], axis=1)
    (dx1,), (g["o_norm_pre"],) = _rowwise_bwd(_fn_rms, [c1(x1)], [c1(w["o_norm_pre"])], [c1(du1)], [F32],
                                              adds={0: c1(dx2)}, tm=tm, name="o_pre_bwd")

    (dout0,), (g["e_norm_post"],) = _rowwise_bwd(_fn_post, post_rows, [c1(w["e_norm_post"])], [c1(dx1)],
                                                 [None, BF16], tm=tm, name="e_post_bwd")
    dhmix = _mm(dout0, w_eo, tb=True, name="e_out_dx")
    g["e_w_out"] = _mm(hmix, dout0, ta=True, name="e_out_dw")
    (dy, dzs, do, dzf), (g["e_ssd_norm"],) = _rowwise_bwd(_fn_mix, mix_rows, mix_pars, [c1(dhmix)],
                                                        [F32, BF16, F32, BF16], tm=tm, name="e_mix_bwd")
    dox = _fox_dopack(do, o, name="e_fox_dopack")
    dq8, dk, dv, dcs, *got = _fox3_bwd(qkv, csum, bounds, dox, mrun, carry=comm.early_grads(g), name="e_fox_bwd")
    comm.got_early_grads(got)
    dlf = _pad_lanes(_cumsum_lanes(dcs.reshape(FOX_HEADS, s), reverse=True, name="e_cumsum_bwd").T)
    dxs, ddt, dbm, dcm, dalog, ddsk, *got = _ssd2_bwd(dt, xbc, alog, dsk, hsave, dy, carry=comm.early_sums(),
                                                       name="e_ssd_bwd")
    comm.got_early_sums(got)
    dxbc = jnp.concatenate([dxs, dbm, dcm], axis=1)
    (dxbc_pre, ddtp, dfp), (ddtb, dfgb) = _rowwise_bwd(_fn_act, act_rows, act_pars, [c1(dxbc), c1(ddt), c1(dlf)],
                                                      [F32, BF16, BF16], tm=tm, name="e_act_bwd")
    dxbc_raw = bf(_conv_bwd_x(dxbc_pre, w["e_conv_w"], name="e_conv_dx"))
    g["e_conv_w"], g["e_conv_b"] = _conv_bwd_w(xbc_raw, dxbc_pre, SSD_CONV, name="e_conv_dw")
    du0 = _mm(dzs, w_z[:, :D_SSD], tb=True, name="e_in_dx0")
    du0 = _mm(dzf, w_z[:, D_SSD:], tb=True, add=du0, name="e_in_dx1")
    du0 = _mm(dxbc_raw, w_xbc, tb=True, add=du0, name="e_in_dx2")
    eighth = FOX_HEADDIM ** -0.5
    du0 = _mm(dq8, w_qkv[:, :D_FOX] * eighth, tb=True, add=du0, name="e_in_dx3q")
    du0 = _mm(dk, w_qkv[:, D_FOX:2 * D_FOX], tb=True, add=du0, name="e_in_dx3k")
    du0 = _mm(dv, w_qkv[:, 2 * D_FOX:], tb=True, add=du0, name="e_in_dx3v")
    du0 = _mm(ddtp, w_dt, tb=True, add=du0, name="e_in_dx4")
    du0 = _mm(dfp, w_f, tb=True, add=du0, name="e_in_dx5")
    g["e_w_in"] = jnp.concatenate([
        _mm(u0, dzs, ta=True, name="e_in_dw0"), _mm(u0, dzf, ta=True, name="e_in_dw1"),
        _mm(u0, dxbc_raw, ta=True, name="e_in_dw2"), _mm(u0, ddtp, ta=True, name="e_in_dw3")[:, :SSD_HEADS],
        _mm(u0, dq8, ta=True, name="e_in_dw4q") * eighth, _mm(u0, dk, ta=True, name="e_in_dw4k"),
        _mm(u0, dv, ta=True, name="e_in_dw4v"), _mm(u0, dfp, ta=True, name="e_in_dw5")[:, :FOX_HEADS]], axis=1)
    (dx,), (g["e_norm_pre"],) = _rowwise_bwd(_fn_rms, [c1(x)], [c1(w["e_norm_pre"])], [c1(du0)], [F32],
                                             adds={0: c1(dx1)}, tm=tm, name="e_pre_bwd")
    g["e_dt_bias"], g["e_fgate_b"] = ddtb[:, :SSD_HEADS], dfgb[:, :FOX_HEADS]
    g["e_a_log"], g["e_d_skip"] = dalog[:, :SSD_HEADS], ddsk[:, :SSD_HEADS]
    return loss, dx, g


_WEIGHTS = ["e_norm_pre", "e_w_in", "e_conv_w", "e_conv_b", "e_dt_bias", "e_a_log", "e_d_skip", "e_fgate_b",
            "e_ssd_norm", "e_w_out", "e_norm_post", "o_norm_pre", "o_w_in", "o_conv_w", "o_conv_b", "o_ln_g",
            "o_ln_b", "o_w_out", "o_norm_post"]
_BIG = ["e_w_in", "e_w_out", "o_w_in", "o_w_out"]
_ROW_SHARDED = ["e_w_out", "o_w_out"]
_SMALL_SHARDED = ["e_conv_w", "o_norm_pre", "o_conv_w", "o_conv_b", "o_ln_g", "o_ln_b", "o_norm_post"]
_REPLICATED = ["e_norm_pre", "e_conv_b", "e_dt_bias", "e_a_log", "e_d_skip", "e_fgate_b", "e_ssd_norm", "e_norm_post"]
_SMALL = [n for n in _WEIGHTS if n not in _BIG]
_EVEN_SHARDED = ["e_w_in", "e_conv_w"]
_ODD_SHARDED = ["e_w_out", "o_w_in", "o_w_out", "o_norm_pre", "o_conv_w", "o_conv_b", "o_ln_g", "o_ln_b",
                "o_norm_post"]
_EARLY_GRADS = ["o_w_in", "o_w_out", "e_w_out"]
N_CHIPS = 4


def _join(gathered, rows):
    k, r, c = gathered.shape
    return gathered.reshape(k * r, c) if rows else gathered.transpose(1, 0, 2).reshape(r, k * c)


def _split(full, rows):
    r, c = full.shape
    return full.reshape(N_CHIPS, r // N_CHIPS, c) if rows else full.reshape(r, N_CHIPS, c // N_CHIPS).transpose(1, 0, 2)


def kernel(x, e_norm_pre, e_w_in, e_conv_w, e_conv_b, e_dt_bias, e_a_log, e_d_skip, e_fgate_b, e_ssd_norm, e_w_out, e_norm_post, o_norm_pre, o_w_in, o_conv_w, o_conv_b, o_ln_g, o_ln_b, o_w_out, o_norm_post, loss_target, m_e_norm_pre, m_e_w_in, m_e_conv_w, m_e_conv_b, m_e_dt_bias, m_e_a_log, m_e_d_skip, m_e_fgate_b, m_e_ssd_norm, m_e_w_out, m_e_norm_post, m_o_norm_pre, m_o_w_in, m_o_conv_w, m_o_conv_b, m_o_ln_g, m_o_ln_b, m_o_w_out, m_o_norm_post, v_e_norm_pre, v_e_w_in, v_e_conv_w, v_e_conv_b, v_e_dt_bias, v_e_a_log, v_e_d_skip, v_e_fgate_b, v_e_ssd_norm, v_e_w_out, v_e_norm_post, v_o_norm_pre, v_o_w_in, v_o_conv_w, v_o_conv_b, v_o_ln_g, v_o_ln_b, v_o_w_out, v_o_norm_post):
    wvals = (e_norm_pre, e_w_in, e_conv_w, e_conv_b, e_dt_bias, e_a_log, e_d_skip, e_fgate_b, e_ssd_norm, e_w_out,
             e_norm_post, o_norm_pre, o_w_in, o_conv_w, o_conv_b, o_ln_g, o_ln_b, o_w_out, o_norm_post)
    mvals = (m_e_norm_pre, m_e_w_in, m_e_conv_w, m_e_conv_b, m_e_dt_bias, m_e_a_log, m_e_d_skip, m_e_fgate_b,
             m_e_ssd_norm, m_e_w_out, m_e_norm_post, m_o_norm_pre, m_o_w_in, m_o_conv_w, m_o_conv_b, m_o_ln_g,
             m_o_ln_b, m_o_w_out, m_o_norm_post)
    vvals = (v_e_norm_pre, v_e_w_in, v_e_conv_w, v_e_conv_b, v_e_dt_bias, v_e_a_log, v_e_d_skip, v_e_fgate_b,
             v_e_ssd_norm, v_e_w_out, v_e_norm_post, v_o_norm_pre, v_o_w_in, v_o_conv_w, v_o_conv_b, v_o_ln_g,
             v_o_ln_b, v_o_w_out, v_o_norm_post)

    def mat(v):
        return v.reshape(v.shape[-2:]) if v.ndim == 3 else v

    w = {n: mat(v) for n, v in zip(_WEIGHTS, wvals)}
    m = {n: mat(v) for n, v in zip(_WEIGHTS, mvals)}
    v2 = {n: mat(v) for n, v in zip(_WEIGHTS, vvals)}
    me_xy = 2 * lax.axis_index("x") + lax.axis_index("y")

    def shard(n):
        return w[n].astype(BF16) if n in _BIG else w[n]

    gathered = _exchange([shard(n) for n in _EVEN_SHARDED], "xy", False, name="gather_weights")
    full = {n: w[n] for n in _REPLICATED}
    for n, gth in zip(_EVEN_SHARDED, gathered):
        full[n] = _join(gth, n in _ROW_SHARDED)
    gparts = {}

    class _StepComm(_NoComm):
        def odd_weights(self):
            return _Exchange([shard(n) for n in _ODD_SHARDED], "xy", False)

        def got_odd_weights(self, got, wdict):
            for n, gth in zip(_ODD_SHARDED, got):
                wdict[n] = _join(gth, n in _ROW_SHARDED)

        def early_grads(self, g):
            return _Exchange([_split(g[n], n in _ROW_SHARDED).astype(BF16) for n in _EARLY_GRADS], "xy", True)

        def got_early_grads(self, got):
            self.sums = [_sum_slots(p, out_dtype=BF16, name="sum_" + n) for n, p in zip(_EARLY_GRADS, got)]

        def early_sums(self):
            return _Exchange(self.sums, "c", False)

        def got_early_sums(self, got):
            gparts.update(zip(_EARLY_GRADS, got))

    loss, dx, g = _local_step(x[0], loss_target[0], full, _StepComm())
    loss = lax.psum(loss[0, 0], ("x", "y", "c"))

    flat = jnp.concatenate([_pad_lanes(g[n].reshape(1, -1)) for n in _SMALL], axis=1).reshape(-1, LANES)
    (scattered,), (all8,) = _exchanges([_Exchange([_split(g["e_w_in"], False).astype(BF16)], "xy", True),
                                        _Exchange([flat], "xyc", False)], name="scatter_grads")
    (gparts["e_w_in"],) = _exchange([_sum_slots(scattered, out_dtype=BF16, name="sum_e_w_in")], "c", False,
                                    name="pair_grads")
    total = _sum_slots(all8, name="sum_small").reshape(1, -1)
    at = 0
    for n in _SMALL:
        size = g[n].size
        gn = total[:, at:at + size].reshape(g[n].shape)
        at += size + (-size % LANES)
        if n in _SMALL_SHARDED:
            cols = gn.shape[1] // N_CHIPS
            gn = lax.dynamic_slice(gn, (0, me_xy * cols), (gn.shape[0], cols))
        gparts[n] = gn[None]

    grads, deltas, new_m, new_v = [], [], [], []
    for n, orig in zip(_WEIGHTS, wvals):
        gn, dn, mn, vn = _adamw(w[n], gparts[n], m[n], v2[n], name="adamw_" + n)
        for lst, val in zip((grads, deltas, new_m, new_v), (gn, dn, mn, vn)):
            lst.append(val.reshape(orig.shape))
    return (loss, dx[None], *grads, *deltas, *new_m, *new_v)
```

```python
import functools

import jax
import jax.numpy as jnp
from jax import lax
from jax.experimental import pallas as pl
from jax.experimental.pallas import tpu as pltpu

F32 = jnp.float32
BF16 = jnp.bfloat16
MESH = pl.DeviceIdType.MESH

D_MODEL = 1024
D_SSD = 1024
SSD_HEADS = 16
SSD_HEADDIM = 64
SSD_GROUPS = 4
SSD_HPG = 4
D_STATE = 128
SSD_CONV = 4
CHUNK = 128
D_FOX = 1024
FOX_HEADS = 16
FOX_HEADDIM = 64
D_CONV = 2048
CONV_WIDTH = 31
EPS = 1e-6
LANES = 128
VMEM_LIMIT = 56 * 1024 * 1024

ADAM_LR = 0.001
ADAM_B1 = 0.9
ADAM_B2 = 0.999
ADAM_EPS = 1e-08
ADAM_WD = 0.01
ADAM_STEP = 10


def _cparams(sem=None):
    return pltpu.CompilerParams(dimension_semantics=sem, vmem_limit_bytes=VMEM_LIMIT)


def _mm(a, b, *, ta=False, tb=False, add=None, out_dtype=F32, tm=1024, tn=None, tk=2048, name):
    m = a.shape[1] if ta else a.shape[0]
    k = a.shape[0] if ta else a.shape[1]
    n = b.shape[0] if tb else b.shape[1]
    if tn is None:
        tn = 1024
    tm, tn = min(tm, m), min(tn, n)
    tk = max(t for t in range(LANES, min(tk, k) + 1, LANES) if k % t == 0)
    assert m % tm == 0 and n % tn == 0 and k % tk == 0, (m, n, k, tm, tn, tk)
    nk = k // tk
    dims = (((0 if ta else 1,), (1 if tb else 0,)), ((), ()))

    def body(*refs):
        if add is None:
            a_ref, b_ref, o_ref = refs[:3]
            c_ref = None
        else:
            a_ref, b_ref, c_ref, o_ref = refs[:4]
        kk = pl.program_id(2)
        prod = lax.dot_general(a_ref[...].astype(BF16), b_ref[...].astype(BF16), dims, preferred_element_type=F32)
        if nk == 1:
            o_ref[...] = (prod if c_ref is None else prod + c_ref[...].astype(F32)).astype(o_ref.dtype)
            return
        acc_ref = refs[-1]

        @pl.when(kk == 0)
        def _():
            acc_ref[...] = prod if c_ref is None else prod + c_ref[...].astype(F32)

        @pl.when((kk > 0) & (kk < nk - 1))
        def _():
            acc_ref[...] += prod

        @pl.when(kk == nk - 1)
        def _():
            o_ref[...] = (acc_ref[...] + prod).astype(o_ref.dtype)

    a_spec = (pl.BlockSpec((tk, tm), lambda j, i, kk: (kk, i)) if ta
              else pl.BlockSpec((tm, tk), lambda j, i, kk: (i, kk)))
    b_spec = (pl.BlockSpec((tn, tk), lambda j, i, kk: (j, kk)) if tb
              else pl.BlockSpec((tk, tn), lambda j, i, kk: (kk, j)))
    o_spec = pl.BlockSpec((tm, tn), lambda j, i, kk: (i, j))
    in_specs, args = [a_spec, b_spec], [a, b]
    if add is not None:
        in_specs.append(o_spec)
        args.append(add)
    return pl.pallas_call(
        body, name=name, grid=(n // tn, m // tm, nk),
        in_specs=in_specs, out_specs=o_spec,
        out_shape=jax.ShapeDtypeStruct((m, n), out_dtype),
        scratch_shapes=[pltpu.VMEM((tm, tn), F32)] if nk > 1 else [],
        compiler_params=_cparams(("parallel", "parallel", "arbitrary")),
    )(*args)


def _col(arr, cb, width):
    return (arr, cb, width)


def _row_specs(ops, tm):
    return [pl.BlockSpec((tm, w), lambda i, cb=cb: (i, cb)) for (_, cb, w) in ops]


def _par_specs(ops):
    return [pl.BlockSpec((a.shape[0], w), lambda i, cb=cb: (0, cb)) for (a, cb, w) in ops]


def _rowwise_fwd(fn, rows, params, outs, *, tm, name):
    s = rows[0][0].shape[0]
    tm = min(tm, s)
    nr, npar = len(rows), len(params)

    def body(*refs):
        rv = [r[...].astype(F32) for r in refs[:nr]]
        pv = [p[...].astype(F32) for p in refs[nr:nr + npar]]
        res = fn(*rv, *pv)
        for o_ref, val in zip(refs[nr + npar:], res):
            o_ref[...] = val.astype(o_ref.dtype)

    return pl.pallas_call(
        body, name=name, grid=(s // tm,),
        in_specs=_row_specs(rows, tm) + _par_specs(params),
        out_specs=[pl.BlockSpec((tm, w), lambda i: (i, 0)) for (w, _) in outs],
        out_shape=[jax.ShapeDtypeStruct((s, w), dt) for (w, dt) in outs],
        compiler_params=_cparams(("parallel",)),
    )(*[r[0] for r in rows], *[p[0] for p in params])


def _rowwise_bwd(fn, rows, params, couts, row_grads, *, adds=None, tm, name):
    adds = adds or {}
    s = rows[0][0].shape[0]
    tm = min(tm, s)
    nr, npar, nc = len(rows), len(params), len(couts)
    add_keys = sorted(adds)
    want = [i for i, dt in enumerate(row_grads) if dt is not None]

    def body(*refs):
        i = pl.program_id(0)
        rv = [r[...].astype(F32) for r in refs[:nr]]
        pv = [p[...].astype(F32) for p in refs[nr:nr + npar]]
        cv = [c[...].astype(F32) for c in refs[nr + npar:nr + npar + nc]]
        av = {k: r[...].astype(F32) for k, r in zip(add_keys, refs[nr + npar + nc:nr + npar + nc + len(add_keys)])}
        orefs = refs[nr + npar + nc + len(add_keys):]
        _, vjp = jax.vjp(lambda rr, pp: tuple(fn(*rr, *pp)), rv, pv)
        drows, dpars = vjp(tuple(cv))
        for o_ref, ri in zip(orefs[:len(want)], want):
            g = drows[ri]
            if ri in av:
                g = g + av[ri]
            o_ref[...] = g.astype(o_ref.dtype)

        @pl.when(i == 0)
        def _():
            for o_ref in orefs[len(want):]:
                o_ref[...] = jnp.zeros_like(o_ref)

        for o_ref, g in zip(orefs[len(want):], dpars):
            o_ref[...] += g

    add_ops = [adds[k] for k in add_keys]
    out_specs = ([pl.BlockSpec((tm, rows[ri][2]), lambda i: (i, 0)) for ri in want]
                 + [pl.BlockSpec((p[0].shape[0], p[2]), lambda i: (0, 0)) for p in params])
    out_shape = ([jax.ShapeDtypeStruct((s, rows[ri][2]), row_grads[ri]) for ri in want]
                 + [jax.ShapeDtypeStruct((p[0].shape[0], p[2]), F32) for p in params])
    res = pl.pallas_call(
        body, name=name, grid=(s // tm,),
        in_specs=_row_specs(rows, tm) + _par_specs(params) + _row_specs(couts, tm) + _row_specs(add_ops, tm),
        out_specs=out_specs, out_shape=out_shape,
        compiler_params=_cparams(("arbitrary",)),
    )(*[r[0] for r in rows], *[p[0] for p in params], *[c[0] for c in couts], *[a[0] for a in add_ops])
    return res[:len(want)], res[len(want):]


def _silu(v):
    return v * jax.nn.sigmoid(v)


def _rms(v, g):
    return v * lax.rsqrt(jnp.mean(v * v, axis=-1, keepdims=True) + EPS) * g


SUBLANES = 8
CONV_ROWS = 256


def _halo(shifts):
    up = lambda v: -(-v // SUBLANES) * SUBLANES
    return up(max(0, -min(shifts))), up(max(0, max(shifts)))


def _fill_halo(xp_sc, x_ref, front, back):
    s = x_ref.shape[0]
    if front:
        xp_sc[0:front, :] = jnp.zeros((front, LANES), F32)
    if back:
        xp_sc[front + s:front + s + back, :] = jnp.zeros((back, LANES), F32)
    xp_sc[front:front + s, :] = x_ref[...].astype(F32)


def _shift_conv(x, w, b, shifts, *, name):
    s, c = x.shape
    tr = min(CONV_ROWS, s)
    nk = len(shifts)
    front, back = _halo(shifts)

    def body(*refs):
        if b is None:
            x_ref, w_ref, o_ref, xp_sc = refs
        else:
            x_ref, w_ref, b_ref, o_ref, xp_sc = refs
        _fill_halo(xp_sc, x_ref, front, back)

        def chunk(r, carry):
            base = pl.multiple_of(r * tr, tr)
            acc = jnp.zeros((tr, LANES), F32) if b is None else jnp.broadcast_to(b_ref[...], (tr, LANES))
            for kk in range(nk):
                acc = acc + xp_sc[pl.ds(base + front + shifts[kk], tr), :] * w_ref[kk:kk + 1, :]
            o_ref[pl.ds(base, tr), :] = acc
            return carry

        lax.fori_loop(0, s // tr, chunk, 0)

    strip = pl.BlockSpec((s, LANES), lambda cb: (0, cb))
    in_specs = [strip, pl.BlockSpec((nk, LANES), lambda cb: (0, cb))]
    args = [x, w]
    if b is not None:
        in_specs.append(pl.BlockSpec((1, LANES), lambda cb: (0, cb)))
        args.append(b)
    return pl.pallas_call(
        body, name=name, grid=(c // LANES,), in_specs=in_specs, out_specs=strip,
        out_shape=jax.ShapeDtypeStruct((s, c), F32),
        scratch_shapes=[pltpu.VMEM((front + s + back, LANES), F32)],
        compiler_params=_cparams(("parallel",)),
    )(*args)


def _conv_fwd(x, w, b, *, name):
    k = w.shape[0]
    return _shift_conv(x, w, b, [kk - (k - 1) for kk in range(k)], name=name)


def _conv_bwd_x(dy, w, *, name):
    k = w.shape[0]
    return _shift_conv(dy, w, None, [(k - 1) - kk for kk in range(k)], name=name)


def _conv_bwd_w(x, dy, k, *, name):
    s, c = x.shape
    tr = min(CONV_ROWS, s)
    shifts = [kk - (k - 1) for kk in range(k)]
    front, back = _halo(shifts)

    def fold(v):
        return jnp.sum(v.reshape(tr // SUBLANES, SUBLANES, LANES), axis=0)

    def body(x_ref, dy_ref, dw_ref, db_ref, xp_sc, dw_sc, db_sc):
        _fill_halo(xp_sc, x_ref, front, back)
        dw_sc[...] = jnp.zeros_like(dw_sc)
        db_sc[...] = jnp.zeros_like(db_sc)

        def chunk(r, carry):
            base = pl.multiple_of(r * tr, tr)
            dyv = dy_ref[pl.ds(base, tr), :]
            db_sc[...] += fold(dyv)
            for kk in range(k):
                dw_sc[kk] += fold(xp_sc[pl.ds(base + front + shifts[kk], tr), :] * dyv)
            return carry

        lax.fori_loop(0, s // tr, chunk, 0)
        db_ref[...] = jnp.sum(db_sc[...], axis=0, keepdims=True)
        for kk in range(k):
            dw_ref[kk:kk + 1, :] = jnp.sum(dw_sc[kk], axis=0, keepdims=True)

    strip = pl.BlockSpec((s, LANES), lambda cb: (0, cb))
    return pl.pallas_call(
        body, name=name, grid=(c // LANES,), in_specs=[strip, strip],
        out_specs=[pl.BlockSpec((k, LANES), lambda cb: (0, cb)), pl.BlockSpec((1, LANES), lambda cb: (0, cb))],
        out_shape=[jax.ShapeDtypeStruct((k, c), F32), jax.ShapeDtypeStruct((1, c), F32)],
        scratch_shapes=[pltpu.VMEM((front + s + back, LANES), F32), pltpu.VMEM((k, SUBLANES, LANES), F32),
                        pltpu.VMEM((SUBLANES, LANES), F32)],
        compiler_params=_cparams(("parallel",)),
    )(x, dy)


_DIMS = {"nn": ((1,), (0,)), "nt": ((1,), (1,)), "tn": ((0,), (0,))}


def _bd(a, b, mode):
    return lax.dot_general(a.astype(BF16), b.astype(BF16), (_DIMS[mode], ((), ())), preferred_element_type=F32)


@functools.partial(jax.custom_vjp, nondiff_argnums=(2,))
def _bdot(a, b, mode):
    return _bd(a, b, mode)


def _bdot_fwd(a, b, mode):
    return _bd(a, b, mode), (a, b)


def _bdot_bwd(mode, res, g):
    a, b = res
    if mode == "nn":
        return _bd(g, b, "nt"), _bd(a, g, "tn")
    if mode == "nt":
        return _bd(g, b, "nn"), _bd(g, a, "tn")
    return _bd(b, g, "nt"), _bd(a, g, "nn")


_bdot.defvjp(_bdot_fwd, _bdot_bwd)


def _split3(v):
    hi = v.astype(BF16)
    r1 = v - hi.astype(F32)
    mid = r1.astype(BF16)
    lo = (r1 - mid.astype(F32)).astype(BF16)
    return hi, mid, lo


def _mask_dot(mask01, v, mode, terms=3):
    out = None
    for part in _split3(v)[:terms]:
        if mode == "vn":
            t = lax.dot_general(part, mask01, (_DIMS["nn"], ((), ())), preferred_element_type=F32)
        else:
            t = lax.dot_general(mask01, part, (_DIMS[mode], ((), ())), preferred_element_type=F32)
        out = t if out is None else out + t
    return out


def _lower_tri(n):
    r = lax.broadcasted_iota(jnp.int32, (n, n), 0)
    c = lax.broadcasted_iota(jnp.int32, (n, n), 1)
    return (r >= c).astype(BF16)


@jax.custom_vjp
def _tri_dot(w):
    return _mask_dot(_lower_tri(w.shape[0]), w, "nn", terms=2)


def _tri_dot_fwd(w):
    return _tri_dot(w), None


def _tri_dot_bwd(_, g):
    return (_mask_dot(_lower_tri(g.shape[0]), g, "tn", terms=2),)


_tri_dot.defvjp(_tri_dot_fwd, _tri_dot_bwd)


def _cumsum_lanes(x, *, reverse, name):
    h, s = x.shape
    n = s // LANES

    def body(x_ref, o_ref):
        r = lax.broadcasted_iota(jnp.int32, (LANES, LANES), 0)
        c = lax.broadcasted_iota(jnp.int32, (LANES, LANES), 1)
        m01 = ((r >= c) if reverse else (r <= c)).astype(BF16)

        def step(t, carry):
            ci = (n - 1 - t) if reverse else t
            at = pl.ds(pl.multiple_of(ci * LANES, LANES), LANES)
            blk = x_ref[:, at]
            o_ref[:, at] = _mask_dot(m01, blk, "vn") + carry
            return carry + jnp.sum(blk, axis=1, keepdims=True)

        lax.fori_loop(0, n, step, jnp.zeros((h, 1), F32))

    return pl.pallas_call(body, name=name, out_shape=jax.ShapeDtypeStruct((h, s), F32),
                          compiler_params=_cparams())(x)


SSD_PAIRS = SSD_HPG // 2


def _ssd2_chunk(xs, dt, bm, cm, hin, a, dsk, head0):
    n = CHUNK
    row = lax.broadcasted_iota(jnp.int32, (n, n), 0)
    col = lax.broadcasted_iota(jnp.int32, (n, n), 1)
    lower = row >= col
    ustrict = (row > col).astype(F32)
    lane = lax.broadcasted_iota(jnp.int32, (1, LANES), 1)
    sub = lax.broadcasted_iota(jnp.int32, (n, 1), 0)
    e_first = (sub == 0).astype(F32)
    e_last = (sub == n - 1).astype(F32)
    lane0 = (lane == 0).astype(F32)
    half_l = [(lane < LANES // 2).astype(F32), (lane >= LANES // 2).astype(F32)]
    half_s = [(sub < LANES // 2).astype(F32), (sub >= LANES // 2).astype(F32)]
    cb = _bdot(cm, bm, "nt")
    da = dt * (-jnp.exp(a))
    ys, houts = [], []
    for pr in range(SSD_PAIRS):
        y = jnp.zeros((n, LANES), F32)
        xdte = jnp.zeros((n, LANES), F32)
        lane_gain = jnp.zeros((n, LANES), F32)
        row_gain = jnp.zeros((LANES, 1), F32)
        for hf in range(2):
            oh = (lane == head0 + 2 * pr + hf).astype(F32)
            dt_col = jnp.sum(dt * oh, axis=1, keepdims=True)
            da_col = jnp.sum(da * oh, axis=1, keepdims=True)
            dsk_h = jnp.sum(dsk * oh, axis=1, keepdims=True)
            seg = _tri_dot(da_col * ustrict)
            decay = jnp.where(lower, jnp.exp(seg), 0.0)
            cs_col = jnp.sum(seg * lane0, axis=1, keepdims=True) + jnp.sum(da_col * e_first, axis=0, keepdims=True)
            total = jnp.sum(cs_col * e_last, axis=0, keepdims=True)
            xh = xs[pr] * half_l[hf]
            xd = xh * dt_col
            y = y + _bdot(cb * decay, xd, "nn") + xh * dsk_h
            xdte = xdte + xd * jnp.exp(total - cs_col)
            lane_gain = lane_gain + jnp.exp(cs_col) * half_l[hf]
            row_gain = row_gain + jnp.exp(total) * half_s[hf]
        houts.append(hin[pr] * row_gain + _bdot(xdte, bm, "tn"))
        ys.append(y + _bdot(cm, hin[pr], "nt") * lane_gain)
    return ys, houts


SSD_STEP = 4


def _ssd2_steps(s):
    per = min(SSD_STEP, s // CHUNK)
    return per, s // (CHUNK * per)


def _ssd2_multi(xs, dt, bm, cm, hin, a, dsk, head0):
    ys = []
    for k in range(len(dt)):
        y, hin = _ssd2_chunk(xs[k], dt[k], bm[k], cm[k], hin, a, dsk, head0)
        ys.append(y)
    return ys, hin


def _ssd2_specs(per, nc, rev):
    cc = (lambda c: nc - 1 - c) if rev else (lambda c: c)
    rows = CHUNK * per
    act = pl.BlockSpec((rows, SSD_PAIRS * LANES), lambda c, g: (cc(c), g))
    row = pl.BlockSpec((rows, LANES), lambda c, g: (cc(c), 0))
    bmat = pl.BlockSpec((rows, LANES), lambda c, g: (cc(c), D_SSD // LANES + g))
    cmat = pl.BlockSpec((rows, LANES), lambda c, g: (cc(c), D_SSD // LANES + SSD_GROUPS + g))
    par = pl.BlockSpec((1, LANES), lambda c, g: (0, 0))
    hs = pl.BlockSpec((1, SSD_PAIRS, LANES, D_STATE), lambda c, g: (cc(c), g, 0, 0))
    return act, row, bmat, cmat, par, hs


def _chunk_rows(ref, k):
    return ref[k * CHUNK:(k + 1) * CHUNK, :]


def _pair_cols(ref, k):
    return [ref[k * CHUNK:(k + 1) * CHUNK, pr * LANES:(pr + 1) * LANES] for pr in range(SSD_PAIRS)]


def _ssd2_fwd(dt, xbc, a, dsk, *, name):
    s = xbc.shape[0]
    per, nc = _ssd2_steps(s)
    act, row, bmat, cmat, par, hs = _ssd2_specs(per, nc, False)

    def body(xs_ref, dt_ref, bm_ref, cm_ref, a_ref, dsk_ref, y_ref, hs_ref, h_sc):
        c, g = pl.program_id(0), pl.program_id(1)

        @pl.when(c == 0)
        def _():
            h_sc[pl.ds(g * SSD_PAIRS, SSD_PAIRS)] = jnp.zeros((SSD_PAIRS, LANES, D_STATE), F32)

        hin = [h_sc[g * SSD_PAIRS + pr] for pr in range(SSD_PAIRS)]
        ks = range(per)
        ys, houts = _ssd2_multi([_pair_cols(xs_ref, k) for k in ks], [_chunk_rows(dt_ref, k) for k in ks],
                                [_chunk_rows(bm_ref, k) for k in ks], [_chunk_rows(cm_ref, k) for k in ks],
                                hin, a_ref[...], dsk_ref[...], g * SSD_HPG)
        for pr in range(SSD_PAIRS):
            for k in ks:
                y_ref[k * CHUNK:(k + 1) * CHUNK, pr * LANES:(pr + 1) * LANES] = ys[k][pr]
            hs_ref[0, pr] = hin[pr]
            h_sc[g * SSD_PAIRS + pr] = houts[pr]

    return pl.pallas_call(
        body, name=name, grid=(nc, SSD_GROUPS),
        in_specs=[act, row, bmat, cmat, par, par], out_specs=[act, hs],
        out_shape=[jax.ShapeDtypeStruct((s, D_SSD), F32),
                   jax.ShapeDtypeStruct((nc, SSD_HEADS // 2, LANES, D_STATE), F32)],
        scratch_shapes=[pltpu.VMEM((SSD_HEADS // 2, LANES, D_STATE), F32)],
        compiler_params=_cparams(("arbitrary", "arbitrary")),
    )(xbc, dt, xbc, xbc, a, dsk)


def _ssd2_bwd(dt, xbc, a, dsk, hsave, dy, *, carry=None, name):
    s = xbc.shape[0]
    per, nc = _ssd2_steps(s)
    act, row, bmat, cmat, par, hs = _ssd2_specs(per, nc, True)
    gmat = pl.BlockSpec((CHUNK * per, LANES), lambda c, g: (nc - 1 - c, g))

    def body(*refs):
        ins, outs, (dh_sc,), comm = _carried(carry, refs, 8, 6, 1)
        xs_ref, dt_ref, bm_ref, cm_ref, a_ref, dsk_ref, hs_ref, dy_ref = ins
        dxs_ref, ddt_ref, dbm_ref, dcm_ref, da_ref, ddsk_ref = outs
        c, g = pl.program_id(0), pl.program_id(1)
        if carry is not None:
            @pl.when((c == 0) & (g == 0))
            def _():
                carry.start(*comm)

        @pl.when(c == 0)
        def _():
            dh_sc[pl.ds(g * SSD_PAIRS, SSD_PAIRS)] = jnp.zeros((SSD_PAIRS, LANES, D_STATE), F32)

        @pl.when((c == 0) & (g == 0))
        def _():
            da_ref[...] = jnp.zeros_like(da_ref)
            ddsk_ref[...] = jnp.zeros_like(ddsk_ref)

        @pl.when(g == 0)
        def _():
            ddt_ref[...] = jnp.zeros_like(ddt_ref)

        head0 = g * SSD_HPG
        ks = range(per)
        prim = ([_pair_cols(xs_ref, k) for k in ks], [_chunk_rows(dt_ref, k) for k in ks],
                [_chunk_rows(bm_ref, k) for k in ks], [_chunk_rows(cm_ref, k) for k in ks],
                [hs_ref[0, pr] for pr in range(SSD_PAIRS)], a_ref[...], dsk_ref[...])
        _, vjp = jax.vjp(lambda *p: _ssd2_multi(*p, head0), *prim)
        cot = ([_pair_cols(dy_ref, k) for k in ks], [dh_sc[g * SSD_PAIRS + pr] for pr in range(SSD_PAIRS)])
        dxs, ddt, dbm, dcm, dhin, da, ddsk = vjp(cot)
        for pr in range(SSD_PAIRS):
            for k in ks:
                dxs_ref[k * CHUNK:(k + 1) * CHUNK, pr * LANES:(pr + 1) * LANES] = dxs[k][pr]
            dh_sc[g * SSD_PAIRS + pr] = dhin[pr]
        for k in ks:
            ddt_ref[k * CHUNK:(k + 1) * CHUNK, :] += ddt[k]
            dbm_ref[k * CHUNK:(k + 1) * CHUNK, :] = dbm[k]
            dcm_ref[k * CHUNK:(k + 1) * CHUNK, :] = dcm[k]
        da_ref[...] += da
        ddsk_ref[...] += ddsk
        if carry is not None:
            @pl.when((c == nc - 1) & (g == SSD_GROUPS - 1))
            def _():
                carry.wait(*comm)

    extra = carry if carry is not None else _Exchange([], "c", False)
    return pl.pallas_call(
        body, name=name, grid=(nc, SSD_GROUPS),
        in_specs=[act, row, bmat, cmat, par, par, hs, act] + extra.in_specs,
        out_specs=[act, row, gmat, gmat, par, par] + extra.out_specs,
        out_shape=[jax.ShapeDtypeStruct((s, D_SSD), F32), jax.ShapeDtypeStruct((s, LANES), F32),
                   jax.ShapeDtypeStruct((s, SSD_GROUPS * D_STATE), F32),
                   jax.ShapeDtypeStruct((s, SSD_GROUPS * D_STATE), F32),
                   jax.ShapeDtypeStruct((1, LANES), F32), jax.ShapeDtypeStruct((1, LANES), F32)] + extra.out_shape,
        scratch_shapes=[pltpu.VMEM((SSD_HEADS // 2, LANES, D_STATE), F32)]
        + (carry.scratch if carry is not None else []),
        compiler_params=_cparams(("arbitrary", "arbitrary")),
    )(xbc, dt, xbc, xbc, a, dsk, hsave, dy, *extra.arrs)


FOX_BLOCK = 512
NEG = -1e30
PAIRS = FOX_HEADS // 2
HALF = LANES // 2


def _causal(t):
    return lax.broadcasted_iota(jnp.int32, (t, t), 0) >= lax.broadcasted_iota(jnp.int32, (t, t), 1)


def _first_half():
    return lax.broadcasted_iota(jnp.int32, (1, LANES), 1) < HALF


def _pair_bias(c_ref, hh, qblock, kblock, t):
    lane = lax.broadcasted_iota(jnp.int32, (1, LANES), 1)
    cq = c_ref[hh, :, pl.ds(pl.multiple_of(qblock * t, LANES), LANES)]
    cref = jnp.sum(jnp.where(lane == 0, cq, 0.0), axis=1, keepdims=True)
    return cref - c_ref[hh, :, pl.ds(pl.multiple_of(kblock * t, LANES), t)]


def _fox_dopack(do, o, *, tm=256, name):
    s, d = do.shape
    tm = min(tm, s)

    def body(do_ref, o_ref, out_ref):
        dov = do_ref[...].astype(BF16)
        prod = dov.astype(F32) * o_ref[...]
        r = lax.broadcasted_iota(jnp.int32, (d, LANES), 0)
        c = lax.broadcasted_iota(jnp.int32, (d, LANES), 1)
        heads = ((r >= c * FOX_HEADDIM) & (r < (c + 1) * FOX_HEADDIM)).astype(BF16)
        negd = -_mask_dot(heads, prod, "vn")
        hr = lax.broadcasted_iota(jnp.int32, (LANES, 2 * d), 0)
        col = lax.broadcasted_iota(jnp.int32, (LANES, 2 * d), 1)
        base = (hr >> 1) * (2 * LANES) + jnp.where((hr & 1) == 0, HALF, LANES)
        terms = None
        for kk, part in enumerate(_split3(negd)):
            place = ((col == base + kk) & (hr < FOX_HEADS)).astype(BF16)
            tk = lax.dot_general(part, place, (_DIMS["nn"], ((), ())), preferred_element_type=F32)
            terms = tk if terms is None else terms + tk
        first = _first_half()
        zero = jnp.zeros((tm, LANES), BF16)
        pieces = []
        for hp in range(PAIRS):
            blk = dov[:, hp * LANES:(hp + 1) * LANES]
            pieces += [jnp.where(first, blk, zero), jnp.where(first, zero, blk)]
        out_ref[...] = (jnp.concatenate(pieces, axis=1).astype(F32) + terms).astype(BF16)

    row = pl.BlockSpec((tm, d), lambda i: (i, 0))
    return pl.pallas_call(body, name=name, grid=(s // tm,), in_specs=[row, row],
                          out_specs=pl.BlockSpec((tm, 2 * d), lambda i: (i, 0)),
                          out_shape=jax.ShapeDtypeStruct((s, 2 * d), BF16),
                          compiler_params=_cparams(("parallel",)))(do, o)


FOX_DEAD = 110.0
BOUND_SLACK = 1.001


def _fox_block_bounds(qkv, c):
    s = qkv.shape[0]
    t = min(FOX_BLOCK, s)
    k = qkv[:, D_FOX:2 * D_FOX].astype(F32).reshape(s // t, t, FOX_HEADS, FOX_HEADDIM)
    kn = jnp.sqrt(jnp.max(jnp.sum(k * k, axis=-1), axis=1)).T
    return kn, c[:, 0, ::t], c[:, 0, t - 1::t]


def _fox_live(qmax, mmin, kn_ref, cs_ref, ce_ref, h, qblock, kblock):
    bound = qmax * (kn_ref[h, kblock] * BOUND_SLACK) + (cs_ref[h, qblock] - ce_ref[h, kblock])
    return bound - mmin > -FOX_DEAD


def _run_live(live, work):
    @pl.when(live[0] & live[1])
    def _():
        work(0)
        work(1)

    @pl.when(live[0] & jnp.logical_not(live[1]))
    def _():
        work(0)

    @pl.when(jnp.logical_not(live[0]) & live[1])
    def _():
        work(1)


def _fox3_fwd(qkv, c, bounds, *, carry=None, name):
    assert qkv.shape[0] // min(FOX_BLOCK, qkv.shape[0]) < LANES - 1
    s = qkv.shape[0]
    t = min(FOX_BLOCK, s)
    nb = s // t
    nt = (((1,), (1,)), ((), ()))
    nn = (((1,), (0,)), ((), ()))
    scale = FOX_HEADDIM ** -0.5

    def body(*refs):
        ins, (o_ref, mrun_ref, qmax_ref, mmin_ref), (m_sc, acc_sc, mt_sc), comm = _carried(carry, refs, 7, 4, 3)
        q_ref, k_ref, v_ref, c_ref, kn_ref, cs_ref, ce_ref = ins
        hp, i = pl.program_id(0), pl.program_id(1)
        if carry is not None:
            @pl.when((hp == 0) & (i == 0))
            def _():
                carry.start(*comm)

        first = _first_half()
        lane = lax.broadcasted_iota(jnp.int32, (1, LANES), 1)
        m_sc[...] = jnp.full_like(m_sc, NEG)
        acc_sc[...] = jnp.zeros_like(acc_sc)
        mt_sc[...] = jnp.zeros_like(mt_sc)
        q2 = q_ref[...] * scale
        zero = jnp.zeros_like(q2)
        qs = [jnp.where(first, q2, zero), jnp.where(first, zero, q2)]

        def head_block(hh, j, k2, vxh, masked):
            sc = lax.dot_general(qs[hh], k2, nt, preferred_element_type=F32) + _pair_bias(c_ref, hh, i, j, t)
            if masked:
                sc = jnp.where(_causal(t), sc, NEG)
            m_prev = m_sc[hh]
            m_new = jnp.maximum(m_prev, jnp.max(sc, axis=1, keepdims=True))
            pr = jnp.exp(sc - jnp.tile(m_new, (1, t // LANES))).astype(BF16)
            pv = lax.dot_general(pr, vxh, nn, preferred_element_type=F32)
            acc_sc[hh] = jnp.exp(m_prev - m_new) * acc_sc[hh] + pv
            m_sc[hh] = m_new

        def step(j, masked):
            at = pl.ds(pl.multiple_of(j * t, t), t)
            k2, v2 = k_ref[at, :], v_ref[at, :]
            one = jnp.ones_like(v2)
            vx = [jnp.where(first, v2, one), jnp.where(first, one, v2)]
            if masked:
                for hh in range(2):
                    head_block(hh, j, k2, vx[hh], True)
            else:
                live = [_fox_live(qmax_ref[2 * hp + hh, i], mmin_ref[2 * hp + hh, i], kn_ref, cs_ref, ce_ref,
                                  2 * hp + hh, i, j) for hh in range(2)]
                _run_live(live, lambda hh: head_block(hh, j, k2, vx[hh], False))
            for hh in range(2):
                mt_sc[hh] = jnp.where(lane == j, m_sc[hh], mt_sc[hh])

        step(i, True)
        for hh in range(2):
            qf = qs[hh].astype(F32)
            qmax_ref[2 * hp + hh, i] = jnp.sqrt(jnp.max(jnp.sum(qf * qf, axis=1, keepdims=True)))
            mmin_ref[2 * hp + hh, i] = jnp.min(m_sc[hh])
        lax.fori_loop(0, i, lambda n, carry: (step(i - 1 - n, False), carry)[1], 0)
        acc_a, acc_b = acc_sc[0], acc_sc[1]
        den_a = jnp.where(first, pltpu.roll(acc_a, HALF, 1), acc_a)
        den_b = jnp.where(first, acc_b, pltpu.roll(acc_b, HALF, 1))
        o_ref[...] = jnp.where(first, acc_a / den_a, acc_b / den_b)
        for hh, den in enumerate((den_a, den_b)):
            mrun_ref[:, hh * LANES:(hh + 1) * LANES] = jnp.where(lane == LANES - 1, m_sc[hh] + jnp.log(den), mt_sc[hh])
        if carry is not None:
            @pl.when((pl.program_id(0) == PAIRS - 1) & (i == nb - 1))
            def _():
                carry.wait(*comm)

    nq = D_FOX // LANES
    smem = pl.BlockSpec(memory_space=pltpu.SMEM)
    extra = carry if carry is not None else _Exchange([], "c", False)
    return pl.pallas_call(
        body, name=name, grid=(PAIRS, nb),
        in_specs=[pl.BlockSpec((t, LANES), lambda hp, i: (i, hp)),
                  pl.BlockSpec((s, LANES), lambda hp, i: (0, nq + hp)),
                  pl.BlockSpec((s, LANES), lambda hp, i: (0, 2 * nq + hp)),
                  pl.BlockSpec((2, 1, s), lambda hp, i: (hp, 0, 0))] + [smem] * 3 + extra.in_specs,
        out_specs=[pl.BlockSpec((t, LANES), lambda hp, i: (i, hp)),
                   pl.BlockSpec((t, 2 * LANES), lambda hp, i: (i, hp)), smem, smem] + extra.out_specs,
        out_shape=[jax.ShapeDtypeStruct((s, D_FOX), F32), jax.ShapeDtypeStruct((s, 2 * D_FOX), F32),
                   jax.ShapeDtypeStruct((FOX_HEADS, nb), F32), jax.ShapeDtypeStruct((FOX_HEADS, nb), F32)]
        + extra.out_shape,
        scratch_shapes=[pltpu.VMEM((2, t, LANES), F32)] * 3 + (carry.scratch if carry is not None else []),
        compiler_params=_cparams(("arbitrary", "arbitrary")),
    )(qkv, qkv, qkv, c, *bounds, *extra.arrs)


def _fox3_bwd(qkv, c, bounds, dox, mrun, *, carry=None, name):
    s = qkv.shape[0]
    t = min(FOX_BLOCK, s)
    nb = s // t
    nt = (((1,), (1,)), ((), ()))
    nn = (((1,), (0,)), ((), ()))
    tn = (((0,), (0,)), ((), ()))
    scale = FOX_HEADDIM ** -0.5

    def body(*refs):
        ins, outs, scratch, comm = _carried(carry, refs, 11, 4, 3)
        k_ref, v_ref, q_ref, c_ref, do_ref, mrun_ref, kn_ref, cs_ref, ce_ref, qmax_ref, mmin_ref = ins
        dq_ref, dk_ref, dv_ref, dc_ref = outs
        dk_sc, dv_sc, dc_sc = scratch
        hp, j = pl.program_id(0), pl.program_id(1)
        if carry is not None:
            @pl.when((hp == 0) & (j == 0))
            def _():
                carry.start(*comm)

        first = _first_half()
        halves = [first, jnp.logical_not(first)]
        lane = lax.broadcasted_iota(jnp.int32, (1, LANES), 1)

        @pl.when(j == 0)
        def _():
            dq_ref[...] = jnp.zeros_like(dq_ref)

        dk_sc[...] = jnp.zeros_like(dk_sc)
        dv_sc[...] = jnp.zeros_like(dv_sc)
        dc_sc[...] = jnp.zeros_like(dc_sc)
        k2, v2 = k_ref[...], v_ref[...]
        one = jnp.ones_like(v2)
        vx = [jnp.where(first, v2, one), jnp.where(first, one, v2)]

        def pick(stats, which):
            return jnp.sum(jnp.where(lane == which, stats, 0.0), axis=1, keepdims=True)

        def head_block(hh, i, at, qsh, stats, masked):
            mine = slice(hh * LANES, (hh + 1) * LANES)
            dov = do_ref[at, mine]
            sc = lax.dot_general(qsh, k2, nt, preferred_element_type=F32) + _pair_bias(c_ref, hh, i, j, t)
            mj = pick(stats, j)
            gain = jnp.broadcast_to(jnp.exp(mj - pick(stats, LANES - 1)), (t, LANES))
            mj = jnp.broadcast_to(mj, (t, LANES))
            pb = jnp.exp(sc - jnp.tile(mj, (1, t // LANES))).astype(BF16)
            if masked:
                pb = jnp.where(_causal(t), pb, jnp.zeros_like(pb))
            pr = pb.astype(F32) * jnp.tile(gain, (1, t // LANES))
            ds = pr * lax.dot_general(dov, vx[hh], nt, preferred_element_type=F32)
            dc_sc[hh] -= jnp.sum(ds, axis=0, keepdims=True)
            dsb = ds.astype(BF16)
            dvh = lax.dot_general(pr.astype(BF16), dov, tn, preferred_element_type=F32)
            dv_sc[...] += jnp.where(halves[hh], dvh, 0.0)
            dk_sc[...] += lax.dot_general(dsb, qsh, tn, preferred_element_type=F32)
            dqh = lax.dot_general(dsb, k2, nn, preferred_element_type=F32)
            dq_ref[at, :] += jnp.where(halves[hh], dqh, 0.0)

        def step(i, masked):
            at = pl.ds(pl.multiple_of(i * t, t), t)
            q2 = q_ref[at, :] * scale
            zero = jnp.zeros_like(q2)
            qs = [jnp.where(first, q2, zero), jnp.where(first, zero, q2)]
            stats = [mrun_ref[at, hh * LANES:(hh + 1) * LANES] for hh in range(2)]
            if masked:
                for hh in range(2):
                    head_block(hh, i, at, qs[hh], stats[hh], True)
            else:
                live = [_fox_live(qmax_ref[2 * hp + hh, i], mmin_ref[2 * hp + hh, i], kn_ref, cs_ref, ce_ref,
                                  2 * hp + hh, i, j) for hh in range(2)]
                _run_live(live, lambda hh: head_block(hh, i, at, qs[hh], stats[hh], False))

        step(j, True)
        lax.fori_loop(j + 1, nb, lambda i, carry: (step(i, False), carry)[1], 0)
        dk_ref[...] = dk_sc[...]
        dv_ref[...] = dv_sc[...].astype(dv_ref.dtype)
        dc_ref[...] = dc_sc[...]
        if carry is not None:
            @pl.when((pl.program_id(0) == PAIRS - 1) & (j == nb - 1))
            def _():
                carry.wait(*comm)

    nq = D_FOX // LANES
    blk = pl.BlockSpec((t, LANES), lambda hp, j: (j, hp))
    smem = pl.BlockSpec(memory_space=pltpu.SMEM)
    extra = carry if carry is not None else _Exchange([], "c", False)
    return pl.pallas_call(
        body, name=name, grid=(PAIRS, nb),
        in_specs=[pl.BlockSpec((t, LANES), lambda hp, j: (j, nq + hp)),
                  pl.BlockSpec((t, LANES), lambda hp, j: (j, 2 * nq + hp)),
                  pl.BlockSpec((s, LANES), lambda hp, j: (0, hp)),
                  pl.BlockSpec((2, 1, s), lambda hp, j: (hp, 0, 0)),
                  pl.BlockSpec((s, 2 * LANES), lambda hp, j: (0, hp)),
                  pl.BlockSpec((s, 2 * LANES), lambda hp, j: (0, hp))] + [smem] * 5 + extra.in_specs,
        out_specs=[pl.BlockSpec((s, LANES), lambda hp, j: (0, hp)), blk, blk,
                   pl.BlockSpec((2, 1, t), lambda hp, j: (hp, 0, j))] + extra.out_specs,
        out_shape=[jax.ShapeDtypeStruct((s, D_FOX), F32), jax.ShapeDtypeStruct((s, D_FOX), F32),
                   jax.ShapeDtypeStruct((s, D_FOX), BF16), jax.ShapeDtypeStruct((FOX_HEADS, 1, s), F32)]
        + extra.out_shape,
        scratch_shapes=[pltpu.VMEM((t, LANES), F32), pltpu.VMEM((t, LANES), F32), pltpu.VMEM((2, 1, t), F32)]
        + (carry.scratch if carry is not None else []),
        compiler_params=_cparams(("arbitrary", "arbitrary")),
    )(qkv, qkv, qkv, c, dox, mrun, *bounds, *extra.arrs)


def _final(x1, out1, g, tgt, *, tm=256, name):
    s, d = x1.shape
    tm = min(tm, s)

    def body(x_ref, o_ref, g_ref, t_ref, dx_ref, do_ref, dg_ref, loss_ref):
        i = pl.program_id(0)

        @pl.when(i == 0)
        def _():
            dg_ref[...] = jnp.zeros_like(dg_ref)
            loss_ref[...] = jnp.zeros_like(loss_ref)

        tv = t_ref[...]

        def lossf(xv, ov, gv):
            err = jnp.square(xv + _rms(ov, gv) - tv)
            return 0.5 * jnp.sum(jnp.mean(err, axis=-1, keepdims=True), axis=0, keepdims=True)

        val, vjp = jax.vjp(lossf, x_ref[...], o_ref[...], g_ref[...])
        dx, do, dg = vjp(jnp.ones((1, 1), F32))
        dx_ref[...] = dx
        do_ref[...] = do.astype(do_ref.dtype)
        dg_ref[...] += dg
        loss_ref[...] += val

    row = pl.BlockSpec((tm, d), lambda i: (i, 0))
    par = pl.BlockSpec((1, d), lambda i: (0, 0))
    return pl.pallas_call(
        body, name=name, grid=(s // tm,), in_specs=[row, row, par, row],
        out_specs=[row, row, par, pl.BlockSpec((1, 1), lambda i: (0, 0))],
        out_shape=[jax.ShapeDtypeStruct((s, d), F32), jax.ShapeDtypeStruct((s, d), BF16),
                   jax.ShapeDtypeStruct((1, d), F32), jax.ShapeDtypeStruct((1, 1), F32)],
        compiler_params=_cparams(("arbitrary",)),
    )(x1, out1, g, tgt)


def _row_tile(r):
    return LANES if r % LANES == 0 else r


def _sum_slots(parts, *, out_dtype=F32, name):
    p, r, c = parts.shape
    tr = _row_tile(r)

    def body(p_ref, o_ref):
        acc = p_ref[0].astype(F32)
        for k in range(1, p):
            acc = acc + p_ref[k].astype(F32)
        o_ref[...] = acc.astype(o_ref.dtype)

    return pl.pallas_call(
        body, name=name, grid=(r // tr,),
        in_specs=[pl.BlockSpec((p, tr, c), lambda i: (0, i, 0))],
        out_specs=pl.BlockSpec((tr, c), lambda i: (i, 0)),
        out_shape=jax.ShapeDtypeStruct((r, c), out_dtype),
        compiler_params=_cparams(("parallel",)),
    )(parts)


def _adamw(w, gparts, m, v, *, name):
    r, c = w.shape
    p = gparts.shape[0]
    tr = _row_tile(r)

    def body(w_ref, g_ref, m_ref, v_ref, go_ref, d_ref, mo_ref, vo_ref):
        g = g_ref[0].astype(F32)
        for k in range(1, p):
            g = g + g_ref[k].astype(F32)
        mn = ADAM_B1 * m_ref[...] + (1.0 - ADAM_B1) * g
        vn = ADAM_B2 * v_ref[...] + (1.0 - ADAM_B2) * jnp.square(g)
        m_hat = mn / (1.0 - ADAM_B1 ** ADAM_STEP)
        v_hat = vn / (1.0 - ADAM_B2 ** ADAM_STEP)
        go_ref[...] = g
        d_ref[...] = -ADAM_LR * (m_hat / (jnp.sqrt(v_hat) + ADAM_EPS) + ADAM_WD * w_ref[...])
        mo_ref[...] = mn
        vo_ref[...] = vn

    spec = pl.BlockSpec((tr, c), lambda i: (i, 0))
    return pl.pallas_call(
        body, name=name, grid=(r // tr,),
        in_specs=[spec, pl.BlockSpec((p, tr, c), lambda i: (0, i, 0)), spec, spec],
        out_specs=[spec] * 4, out_shape=[jax.ShapeDtypeStruct((r, c), F32)] * 4,
        compiler_params=_cparams(("parallel",)),
    )(w, gparts, m, v)


_FLIPS = {
    "xy": [(1, 0, 0), (0, 1, 0), (1, 1, 0)],
    "c": [(0, 0, 1)],
    "xyc": [(fx, fy, fc) for fx in (0, 1) for fy in (0, 1) for fc in (0, 1) if (fx, fy, fc) != (0, 0, 0)],
}


def _slot(mode, px, py, pc):
    return {"xy": 2 * px + py, "c": pc, "xyc": 4 * px + 2 * py + pc}[mode]


class _Exchange:
    def __init__(self, arrs, mode, scatter):
        self.arrs, self.mode, self.scatter = list(arrs), mode, scatter
        self.n = len(self.arrs)
        self.flips = _FLIPS[mode]
        nf = len(self.flips)
        anyspec = pl.BlockSpec(memory_space=pl.ANY)
        self.in_specs = [anyspec] * self.n
        self.out_specs = [anyspec] * self.n
        self.out_shape = [jax.ShapeDtypeStruct((nf + 1,) + (a.shape[1:] if scatter else a.shape), a.dtype)
                          for a in self.arrs]
        self.scratch = [pltpu.SemaphoreType.DMA((self.n * nf,)), pltpu.SemaphoreType.DMA((self.n * nf,)),
                        pltpu.SemaphoreType.DMA((self.n,))]

    def _copies(self, ins, outs, sems, arrivals=True):
        send, recv, loc = sems
        nf = len(self.flips)
        x, y, c = lax.axis_index("x"), lax.axis_index("y"), lax.axis_index("c")
        me = _slot(self.mode, x, y, c)
        peers = [(x ^ fx, y ^ fy, c ^ fc) for (fx, fy, fc) in self.flips]

        def src(a, slot):
            return ins[a].at[slot] if self.scatter else ins[a]

        def copy(a, j, dst_slot):
            return pltpu.make_async_remote_copy(
                src_ref=src(a, _slot(self.mode, *peers[j])), dst_ref=outs[a].at[dst_slot],
                send_sem=send.at[a * nf + j], recv_sem=recv.at[a * nf + j], device_id=peers[j], device_id_type=MESH)

        pairs = [(a, j) for a in range(self.n) for j in range(nf)]
        local = [pltpu.make_async_copy(src(a, me), outs[a].at[me], loc.at[a]) for a in range(self.n)]
        sends = [copy(a, j, me) for a, j in pairs]
        recvs = [copy(a, j, _slot(self.mode, *peers[j])) for a, j in pairs] if arrivals else []
        return local, sends, recvs

    def start(self, ins, outs, sems):
        local, sends, _ = self._copies(ins, outs, sems, arrivals=False)
        for cp in local + sends:
            cp.start()

    def wait(self, ins, outs, sems):
        local, sends, recvs = self._copies(ins, outs, sems)
        for cp in recvs:
            cp.wait_recv()
        for cp in sends:
            cp.wait_send()
        for cp in local:
            cp.wait()


def _carried(carry, refs, n_in, n_out, n_scratch):
    k = carry.n if carry is not None else 0
    ins, refs = refs[:n_in], refs[n_in:]
    cin, refs = refs[:k], refs[k:]
    outs, refs = refs[:n_out], refs[n_out:]
    cout, refs = refs[:k], refs[k:]
    scratch, sems = refs[:n_scratch], refs[n_scratch:]
    return ins, outs, scratch, (cin, cout, sems)


def _exchanges(exs, *, name):
    counts = [ex.n for ex in exs]
    total = sum(counts)

    def body(*refs):
        ins, outs, sems = refs[:total], refs[total:2 * total], refs[2 * total:]
        comms, at = [], 0
        for k, ex in enumerate(exs):
            comms.append((ins[at:at + ex.n], outs[at:at + ex.n], sems[3 * k:3 * k + 3]))
            at += ex.n
        for ex, comm in zip(exs, comms):
            ex.start(*comm)
        for ex, comm in zip(exs, comms):
            ex.wait(*comm)

    res = pl.pallas_call(
        body, name=name, in_specs=[sp for ex in exs for sp in ex.in_specs],
        out_specs=[sp for ex in exs for sp in ex.out_specs], out_shape=[sh for ex in exs for sh in ex.out_shape],
        scratch_shapes=[sc for ex in exs for sc in ex.scratch])(*[a for ex in exs for a in ex.arrs])
    out, at = [], 0
    for n in counts:
        out.append(list(res[at:at + n]))
        at += n
    return out


def _exchange(arrs, mode, scatter, *, name):
    return _exchanges([_Exchange(arrs, mode, scatter)], name=name)[0]


def _softplus(v):
    return jnp.maximum(v, 0.0) + jnp.log1p(jnp.exp(-jnp.abs(v)))


def _pad_lanes(v):
    r, n = v.shape
    return jnp.pad(v, ((0, 0), (0, -n % LANES)))


def _fn_rms(v, g):
    return (_rms(v, g),)


def _fn_post(xv, ov, g):
    return (xv + _rms(ov, g),)


def _fn_act(xbc, dtp, fp, dtb, fb):
    return _silu(xbc), _softplus(dtp + dtb), -_softplus(-(fp + fb))


def _fn_mix(y, zs, o, zf, g):
    yg = y * _silu(zs)
    sq = yg * yg
    lane = lax.broadcasted_iota(jnp.int32, (1, D_SSD), 1)
    width = D_SSD // SSD_GROUPS
    rstd = jnp.zeros_like(yg)
    for gi in range(SSD_GROUPS):
        msk = ((lane >= gi * width) & (lane < (gi + 1) * width)).astype(F32)
        ms = jnp.sum(sq * msk, axis=1, keepdims=True) / width
        rstd = rstd + lax.rsqrt(ms + EPS) * msk
    return (jnp.concatenate([yg * rstd * g, o * _silu(zf)], axis=1),)


def _fn_glu(val, gate):
    return (val * jax.nn.sigmoid(gate),)


def _fn_ln(hc, z, g, b):
    mu = jnp.mean(hc, axis=-1, keepdims=True)
    xc = hc - mu
    yn = xc * lax.rsqrt(jnp.mean(xc * xc, axis=-1, keepdims=True) + EPS) * g + b
    return (_silu(yn) * _silu(z),)


class _NoComm:
    def odd_weights(self):
        return None

    def got_odd_weights(self, got, w):
        pass

    def early_grads(self, g):
        return None

    def got_early_grads(self, got):
        pass

    def early_sums(self):
        return None

    def got_early_sums(self, got):
        pass


def _local_step(x, tgt, w, comm=None):
    comm = comm or _NoComm()
    s = x.shape[0]
    d = D_MODEL
    tm = 256
    tf = 512
    bf = lambda v: v.astype(BF16)
    c1 = lambda arr: _col(arr, 0, arr.shape[1])
    g = {}

    ew = w["e_w_in"]
    w_z, w_xbc = bf(ew[:, 0:2048]), bf(ew[:, 2048:4096])
    w_dt = bf(_pad_lanes(ew[:, 4096:4112]))
    w_qkv = bf(ew[:, 4112:7184])
    w_f = bf(_pad_lanes(ew[:, 7184:7200]))
    dtb, fgb = _pad_lanes(w["e_dt_bias"]), _pad_lanes(w["e_fgate_b"])
    alog, dsk = _pad_lanes(w["e_a_log"]), _pad_lanes(w["e_d_skip"])

    (u0,) = _rowwise_fwd(_fn_rms, [c1(x)], [c1(w["e_norm_pre"])], [(d, BF16)], tm=tf, name="e_pre")
    z = _mm(u0, w_z, name="e_in_z")
    xbc_raw = _mm(u0, w_xbc, out_dtype=BF16, name="e_in_xbc")
    qkv = _mm(u0, w_qkv, out_dtype=BF16, name="e_in_qkv")
    dtp = _mm(u0, w_dt, name="e_in_dt")
    fp = _mm(u0, w_f, name="e_in_f")
    xbc_pre = _conv_fwd(xbc_raw, w["e_conv_w"], w["e_conv_b"], name="e_conv")
    act_rows = [c1(xbc_pre), c1(dtp), c1(fp)]
    act_pars = [c1(dtb), c1(fgb)]
    xbc, dt, lf = _rowwise_fwd(_fn_act, act_rows, act_pars, [(2048, F32), (LANES, F32), (LANES, F32)],
                               tm=tf, name="e_act")
    y, hsave = _ssd2_fwd(dt, xbc, alog, dsk, name="e_ssd")
    csum = _cumsum_lanes(lf[:, :FOX_HEADS].T, reverse=False, name="e_cumsum").reshape(FOX_HEADS, 1, s)
    bounds = _fox_block_bounds(qkv, csum)
    o, mrun, qmax, mmin, *got = _fox3_fwd(qkv, csum, bounds, carry=comm.odd_weights(), name="e_fox")
    bounds = (*bounds, qmax, mmin)
    comm.got_odd_weights(got, w)
    w_eo, w_oi, w_oo = bf(w["e_w_out"]), bf(w["o_w_in"]), bf(w["o_w_out"])
    mix_rows = [c1(y), _col(z, 0, D_SSD), c1(o), _col(z, 1, D_FOX)]
    mix_pars = [c1(w["e_ssd_norm"])]
    (hmix,) = _rowwise_fwd(_fn_mix, mix_rows, mix_pars, [(2048, BF16)], tm=tf, name="e_mix")
    out0 = _mm(hmix, w_eo, name="e_out")
    post_rows = [c1(x), c1(out0)]
    (x1,) = _rowwise_fwd(_fn_post, post_rows, [c1(w["e_norm_post"])], [(d, F32)], tm=tf, name="e_post")

    (u1,) = _rowwise_fwd(_fn_rms, [c1(x1)], [c1(w["o_norm_pre"])], [(d, BF16)], tm=tf, name="o_pre")
    p1 = _mm(u1, w_oi, out_dtype=BF16, name="o_in")
    glu_rows = [_col(p1, 0, D_CONV), _col(p1, 1, D_CONV)]
    (hg,) = _rowwise_fwd(_fn_glu, glu_rows, [], [(D_CONV, BF16)], tm=tf, name="o_glu")
    hc = _conv_fwd(hg, w["o_conv_w"], w["o_conv_b"], name="o_conv")
    ln_rows = [c1(hc), _col(p1, 2, D_CONV)]
    ln_pars = [c1(w["o_ln_g"]), c1(w["o_ln_b"])]
    (h2,) = _rowwise_fwd(_fn_ln, ln_rows, ln_pars, [(D_CONV, BF16)], tm=tf, name="o_ln")
    out1 = _mm(h2, w_oo, name="o_out")

    dx2, dout1, g["o_norm_post"], loss = _final(x1, out1, w["o_norm_post"], tgt, name="loss_head")
    dh2 = _mm(dout1, w_oo, tb=True, name="o_out_dx")
    g["o_w_out"] = _mm(h2, dout1, ta=True, name="o_out_dw")
    (dhc, dz1), (g["o_ln_g"], g["o_ln_b"]) = _rowwise_bwd(_fn_ln, ln_rows, ln_pars, [c1(dh2)], [F32, BF16],
                                                         tm=tm, name="o_ln_bwd")
    dhg = _conv_bwd_x(dhc, w["o_conv_w"], name="o_conv_dx")
    g["o_conv_w"], g["o_conv_b"] = _conv_bwd_w(hg, dhc, CONV_WIDTH, name="o_conv_dw")
    (dval, dgate), _ = _rowwise_bwd(_fn_glu, glu_rows, [], [c1(dhg)], [BF16, BF16], tm=tm, name="o_glu_bwd")
    du1 = _mm(dval, w_oi[:, 0:2048], tb=True, name="o_in_dx0")
    du1 = _mm(dgate, w_oi[:, 2048:4096], tb=True, add=du1, name="o_in_dx1")
    du1 = _mm(dz1, w_oi[:, 4096:6144], tb=True, add=du1, name="o_in_dx2")
    g["o_w_in"] = jnp.concatenate([_mm(u1, dval, ta=True, name="o_in_dw0"), _mm(u1, dgate, ta=True, name="o_in_dw1"),
                                   _mm(u1, dz1, ta=True, name="o_in_dw2")], axis=1)
    (dx1,), (g["o_norm_pre"],) = _rowwise_bwd(_fn_rms, [c1(x1)], [c1(w["o_norm_pre"])], [c1(du1)], [F32],
                                              adds={0: c1(dx2)}, tm=tm, name="o_pre_bwd")

    (dout0,), (g["e_norm_post"],) = _rowwise_bwd(_fn_post, post_rows, [c1(w["e_norm_post"])], [c1(dx1)],
                                                 [None, BF16], tm=tm, name="e_post_bwd")
    dhmix = _mm(dout0, w_eo, tb=True, name="e_out_dx")
    g["e_w_out"] = _mm(hmix, dout0, ta=True, name="e_out_dw")
    (dy, dzs, do, dzf), (g["e_ssd_norm"],) = _rowwise_bwd(_fn_mix, mix_rows, mix_pars, [c1(dhmix)],
                                                        [F32, BF16, F32, BF16], tm=tm, name="e_mix_bwd")
    dox = _fox_dopack(do, o, name="e_fox_dopack")
    dq8, dk, dv, dcs, *got = _fox3_bwd(qkv, csum, bounds, dox, mrun, carry=comm.early_grads(g), name="e_fox_bwd")
    comm.got_early_grads(got)
    dlf = _pad_lanes(_cumsum_lanes(dcs.reshape(FOX_HEADS, s), reverse=True, name="e_cumsum_bwd").T)
    dxs, ddt, dbm, dcm, dalog, ddsk, *got = _ssd2_bwd(dt, xbc, alog, dsk, hsave, dy, carry=comm.early_sums(),
                                                       name="e_ssd_bwd")
    comm.got_early_sums(got)
    dxbc = jnp.concatenate([dxs, dbm, dcm], axis=1)
    (dxbc_pre, ddtp, dfp), (ddtb, dfgb) = _rowwise_bwd(_fn_act, act_rows, act_pars, [c1(dxbc), c1(ddt), c1(dlf)],
                                                      [F32, BF16, BF16], tm=tm, name="e_act_bwd")
    dxbc_raw = bf(_conv_bwd_x(dxbc_pre, w["e_conv_w"], name="e_conv_dx"))
    g["e_conv_w"], g["e_conv_b"] = _conv_bwd_w(xbc_raw, dxbc_pre, SSD_CONV, name="e_conv_dw")
    du0 = _mm(dzs, w_z[:, :D_SSD], tb=True, name="e_in_dx0")
    du0 = _mm(dzf, w_z[:, D_SSD:], tb=True, add=du0, name="e_in_dx1")
    du0 = _mm(dxbc_raw, w_xbc, tb=True, add=du0, name="e_in_dx2")
    eighth = FOX_HEADDIM ** -0.5
    du0 = _mm(dq8, w_qkv[:, :D_FOX] * eighth, tb=True, add=du0, name="e_in_dx3q")
    du0 = _mm(dk, w_qkv[:, D_FOX:2 * D_FOX], tb=True, add=du0, name="e_in_dx3k")
    du0 = _mm(dv, w_qkv[:, 2 * D_FOX:], tb=True, add=du0, name="e_in_dx3v")
    du0 = _mm(ddtp, w_dt, tb=True, add=du0, name="e_in_dx4")
    du0 = _mm(dfp, w_f, tb=True, add=du0, name="e_in_dx5")
    g["e_w_in"] = jnp.concatenate([
        _mm(u0, dzs, ta=True, name="e_in_dw0"), _mm(u0, dzf, ta=True, name="e_in_dw1"),
        _mm(u0, dxbc_raw, ta=True, name="e_in_dw2"), _mm(u0, ddtp, ta=True, name="e_in_dw3")[:, :SSD_HEADS],
        _mm(u0, dq8, ta=True, name="e_in_dw4q") * eighth, _mm(u0, dk, ta=True, name="e_in_dw4k"),
        _mm(u0, dv, ta=True, name="e_in_dw4v"), _mm(u0, dfp, ta=True, name="e_in_dw5")[:, :FOX_HEADS]], axis=1)
    (dx,), (g["e_norm_pre"],) = _rowwise_bwd(_fn_rms, [c1(x)], [c1(w["e_norm_pre"])], [c1(du0)], [F32],
                                             adds={0: c1(dx1)}, tm=tm, name="e_pre_bwd")
    g["e_dt_bias"], g["e_fgate_b"] = ddtb[:, :SSD_HEADS], dfgb[:, :FOX_HEADS]
    g["e_a_log"], g["e_d_skip"] = dalog[:, :SSD_HEADS], ddsk[:, :SSD_HEADS]
    return loss, dx, g


_WEIGHTS = ["e_norm_pre", "e_w_in", "e_conv_w", "e_conv_b", "e_dt_bias", "e_a_log", "e_d_skip", "e_fgate_b",
            "e_ssd_norm", "e_w_out", "e_norm_post", "o_norm_pre", "o_w_in", "o_conv_w", "o_conv_b", "o_ln_g",
            "o_ln_b", "o_w_out", "o_norm_post"]
_BIG = ["e_w_in", "e_w_out", "o_w_in", "o_w_out"]
_ROW_SHARDED = ["e_w_out", "o_w_out"]
_SMALL_SHARDED = ["e_conv_w", "o_norm_pre", "o_conv_w", "o_conv_b", "o_ln_g", "o_ln_b", "o_norm_post"]
_REPLICATED = ["e_norm_pre", "e_conv_b", "e_dt_bias", "e_a_log", "e_d_skip", "e_fgate_b", "e_ssd_norm", "e_norm_post"]
_SMALL = [n for n in _WEIGHTS if n not in _BIG]
_EVEN_SHARDED = ["e_w_in", "e_conv_w"]
_ODD_SHARDED = ["e_w_out", "o_w_in", "o_w_out", "o_norm_pre", "o_conv_w", "o_conv_b", "o_ln_g", "o_ln_b",
                "o_norm_post"]
_EARLY_GRADS = ["o_w_in", "o_w_out", "e_w_out"]
N_CHIPS = 4


def _join(gathered, rows):
    k, r, c = gathered.shape
    return gathered.reshape(k * r, c) if rows else gathered.transpose(1, 0, 2).reshape(r, k * c)


def _split(full, rows):
    r, c = full.shape
    return full.reshape(N_CHIPS, r // N_CHIPS, c) if rows else full.reshape(r, N_CHIPS, c // N_CHIPS).transpose(1, 0, 2)


def kernel(x, e_norm_pre, e_w_in, e_conv_w, e_conv_b, e_dt_bias, e_a_log, e_d_skip, e_fgate_b, e_ssd_norm, e_w_out, e_norm_post, o_norm_pre, o_w_in, o_conv_w, o_conv_b, o_ln_g, o_ln_b, o_w_out, o_norm_post, loss_target, m_e_norm_pre, m_e_w_in, m_e_conv_w, m_e_conv_b, m_e_dt_bias, m_e_a_log, m_e_d_skip, m_e_fgate_b, m_e_ssd_norm, m_e_w_out, m_e_norm_post, m_o_norm_pre, m_o_w_in, m_o_conv_w, m_o_conv_b, m_o_ln_g, m_o_ln_b, m_o_w_out, m_o_norm_post, v_e_norm_pre, v_e_w_in, v_e_conv_w, v_e_conv_b, v_e_dt_bias, v_e_a_log, v_e_d_skip, v_e_fgate_b, v_e_ssd_norm, v_e_w_out, v_e_norm_post, v_o_norm_pre, v_o_w_in, v_o_conv_w, v_o_conv_b, v_o_ln_g, v_o_ln_b, v_o_w_out, v_o_norm_post):
    wvals = (e_norm_pre, e_w_in, e_conv_w, e_conv_b, e_dt_bias, e_a_log, e_d_skip, e_fgate_b, e_ssd_norm, e_w_out,
             e_norm_post, o_norm_pre, o_w_in, o_conv_w, o_conv_b, o_ln_g, o_ln_b, o_w_out, o_norm_post)
    mvals = (m_e_norm_pre, m_e_w_in, m_e_conv_w, m_e_conv_b, m_e_dt_bias, m_e_a_log, m_e_d_skip, m_e_fgate_b,
             m_e_ssd_norm, m_e_w_out, m_e_norm_post, m_o_norm_pre, m_o_w_in, m_o_conv_w, m_o_conv_b, m_o_ln_g,
             m_o_ln_b, m_o_w_out, m_o_norm_post)
    vvals = (v_e_norm_pre, v_e_w_in, v_e_conv_w, v_e_conv_b, v_e_dt_bias, v_e_a_log, v_e_d_skip, v_e_fgate_b,
             v_e_ssd_norm, v_e_w_out, v_e_norm_post, v_o_norm_pre, v_o_w_in, v_o_conv_w, v_o_conv_b, v_o_ln_g,
             v_o_ln_b, v_o_w_out, v_o_norm_post)

    def mat(v):
        return v.reshape(v.shape[-2:]) if v.ndim == 3 else v

    w = {n: mat(v) for n, v in zip(_WEIGHTS, wvals)}
    m = {n: mat(v) for n, v in zip(_WEIGHTS, mvals)}
    v2 = {n: mat(v) for n, v in zip(_WEIGHTS, vvals)}
    me_xy = 2 * lax.axis_index("x") + lax.axis_index("y")

    def shard(n):
        return w[n].astype(BF16) if n in _BIG else w[n]

    gathered = _exchange([shard(n) for n in _EVEN_SHARDED], "xy", False, name="gather_weights")
    full = {n: w[n] for n in _REPLICATED}
    for n, gth in zip(_EVEN_SHARDED, gathered):
        full[n] = _join(gth, n in _ROW_SHARDED)
    gparts = {}

    class _StepComm(_NoComm):
        def odd_weights(self):
            return _Exchange([shard(n) for n in _ODD_SHARDED], "xy", False)

        def got_odd_weights(self, got, wdict):
            for n, gth in zip(_ODD_SHARDED, got):
                wdict[n] = _join(gth, n in _ROW_SHARDED)

        def early_grads(self, g):
            return _Exchange([_split(g[n], n in _ROW_SHARDED).astype(BF16) for n in _EARLY_GRADS], "xy", True)

        def got_early_grads(self, got):
            self.sums = [_sum_slots(p, out_dtype=BF16, name="sum_" + n) for n, p in zip(_EARLY_GRADS, got)]

        def early_sums(self):
            return _Exchange(self.sums, "c", False)

        def got_early_sums(self, got):
            gparts.update(zip(_EARLY_GRADS, got))

    loss, dx, g = _local_step(x[0], loss_target[0], full, _StepComm())
    loss = lax.psum(loss[0, 0], ("x", "y", "c"))

    flat = jnp.concatenate([_pad_lanes(g[n].reshape(1, -1)) for n in _SMALL], axis=1).reshape(-1, LANES)
    (scattered,), (all8,) = _exchanges([_Exchange([_split(g["e_w_in"], False).astype(BF16)], "xy", True),
                                        _Exchange([flat], "xyc", False)], name="scatter_grads")
    (gparts["e_w_in"],) = _exchange([_sum_slots(scattered, out_dtype=BF16, name="sum_e_w_in")], "c", False,
                                    name="pair_grads")
    total = _sum_slots(all8, name="sum_small").reshape(1, -1)
    at = 0
    for n in _SMALL:
        size = g[n].size
        gn = total[:, at:at + size].reshape(g[n].shape)
        at += size + (-size % LANES)
        if n in _SMALL_SHARDED:
            cols = gn.shape[1] // N_CHIPS
            gn = lax.dynamic_slice(gn, (0, me_xy * cols), (gn.shape[0], cols))
        gparts[n] = gn[None]

    grads, deltas, new_m, new_v = [], [], [], []
    for n, orig in zip(_WEIGHTS, wvals):
        gn, dn, mn, vn = _adamw(w[n], gparts[n], m[n], v2[n], name="adamw_" + n)
        for lst, val in zip((grads, deltas, new_m, new_v), (gn, dn, mn, vn)):
            lst.append(val.reshape(orig.shape))
    return (loss, dx[None], *grads, *deltas, *new_m, *new_v)
```

```python
import functools

import jax
import jax.numpy as jnp
from jax import lax
from jax.experimental import pallas as pl
from jax.experimental.pallas import tpu as pltpu

F32 = jnp.float32
BF16 = jnp.bfloat16
MESH = pl.DeviceIdType.MESH

D_MODEL = 1024
D_SSD = 1024
SSD_HEADS = 16
SSD_HEADDIM = 64
SSD_GROUPS = 4
SSD_HPG = 4
D_STATE = 128
SSD_CONV = 4
CHUNK = 128
D_FOX = 1024
FOX_HEADS = 16
FOX_HEADDIM = 64
D_CONV = 2048
CONV_WIDTH = 31
EPS = 1e-6
LANES = 128
VMEM_LIMIT = 56 * 1024 * 1024

ADAM_LR = 0.001
ADAM_B1 = 0.9
ADAM_B2 = 0.999
ADAM_EPS = 1e-08
ADAM_WD = 0.01
ADAM_STEP = 10


def _cparams(sem=None):
    return pltpu.CompilerParams(dimension_semantics=sem, vmem_limit_bytes=VMEM_LIMIT)


def _mm(a, b, *, ta=False, tb=False, add=None, out_dtype=F32, tm=1024, tn=None, tk=2048, name):
    m = a.shape[1] if ta else a.shape[0]
    k = a.shape[0] if ta else a.shape[1]
    n = b.shape[0] if tb else b.shape[1]
    if tn is None:
        tn = 1024
    tm, tn = min(tm, m), min(tn, n)
    tk = max(t for t in range(LANES, min(tk, k) + 1, LANES) if k % t == 0)
    assert m % tm == 0 and n % tn == 0 and k % tk == 0, (m, n, k, tm, tn, tk)
    nk = k // tk
    dims = (((0 if ta else 1,), (1 if tb else 0,)), ((), ()))

    def body(*refs):
        if add is None:
            a_ref, b_ref, o_ref = refs[:3]
            c_ref = None
        else:
            a_ref, b_ref, c_ref, o_ref = refs[:4]
        kk = pl.program_id(2)
        prod = lax.dot_general(a_ref[...].astype(BF16), b_ref[...].astype(BF16), dims, preferred_element_type=F32)
        if nk == 1:
            o_ref[...] = (prod if c_ref is None else prod + c_ref[...].astype(F32)).astype(o_ref.dtype)
            return
        acc_ref = refs[-1]

        @pl.when(kk == 0)
        def _():
            acc_ref[...] = prod if c_ref is None else prod + c_ref[...].astype(F32)

        @pl.when((kk > 0) & (kk < nk - 1))
        def _():
            acc_ref[...] += prod

        @pl.when(kk == nk - 1)
        def _():
            o_ref[...] = (acc_ref[...] + prod).astype(o_ref.dtype)

    a_spec = (pl.BlockSpec((tk, tm), lambda j, i, kk: (kk, i)) if ta
              else pl.BlockSpec((tm, tk), lambda j, i, kk: (i, kk)))
    b_spec = (pl.BlockSpec((tn, tk), lambda j, i, kk: (j, kk)) if tb
              else pl.BlockSpec((tk, tn), lambda j, i, kk: (kk, j)))
    o_spec = pl.BlockSpec((tm, tn), lambda j, i, kk: (i, j))
    in_specs, args = [a_spec, b_spec], [a, b]
    if add is not None:
        in_specs.append(o_spec)
        args.append(add)
    return pl.pallas_call(
        body, name=name, grid=(n // tn, m // tm, nk),
        in_specs=in_specs, out_specs=o_spec,
        out_shape=jax.ShapeDtypeStruct((m, n), out_dtype),
        scratch_shapes=[pltpu.VMEM((tm, tn), F32)] if nk > 1 else [],
        compiler_params=_cparams(("parallel", "parallel", "arbitrary")),
    )(*args)


def _col(arr, cb, width):
    return (arr, cb, width)


def _row_specs(ops, tm):
    return [pl.BlockSpec((tm, w), lambda i, cb=cb: (i, cb)) for (_, cb, w) in ops]


def _par_specs(ops):
    return [pl.BlockSpec((a.shape[0], w), lambda i, cb=cb: (0, cb)) for (a, cb, w) in ops]


def _rowwise_fwd(fn, rows, params, outs, *, tm, name):
    s = rows[0][0].shape[0]
    tm = min(tm, s)
    nr, npar = len(rows), len(params)

    def body(*refs):
        rv = [r[...].astype(F32) for r in refs[:nr]]
        pv = [p[...].astype(F32) for p in refs[nr:nr + npar]]
        res = fn(*rv, *pv)
        for o_ref, val in zip(refs[nr + npar:], res):
            o_ref[...] = val.astype(o_ref.dtype)

    return pl.pallas_call(
        body, name=name, grid=(s // tm,),
        in_specs=_row_specs(rows, tm) + _par_specs(params),
        out_specs=[pl.BlockSpec((tm, w), lambda i: (i, 0)) for (w, _) in outs],
        out_shape=[jax.ShapeDtypeStruct((s, w), dt) for (w, dt) in outs],
        compiler_params=_cparams(("parallel",)),
    )(*[r[0] for r in rows], *[p[0] for p in params])


def _rowwise_bwd(fn, rows, params, couts, row_grads, *, adds=None, tm, name):
    adds = adds or {}
    s = rows[0][0].shape[0]
    tm = min(tm, s)
    nr, npar, nc = len(rows), len(params), len(couts)
    add_keys = sorted(adds)
    want = [i for i, dt in enumerate(row_grads) if dt is not None]

    def body(*refs):
        i = pl.program_id(0)
        rv = [r[...].astype(F32) for r in refs[:nr]]
        pv = [p[...].astype(F32) for p in refs[nr:nr + npar]]
        cv = [c[...].astype(F32) for c in refs[nr + npar:nr + npar + nc]]
        av = {k: r[...].astype(F32) for k, r in zip(add_keys, refs[nr + npar + nc:nr + npar + nc + len(add_keys)])}
        orefs = refs[nr + npar + nc + len(add_keys):]
        _, vjp = jax.vjp(lambda rr, pp: tuple(fn(*rr, *pp)), rv, pv)
        drows, dpars = vjp(tuple(cv))
        for o_ref, ri in zip(orefs[:len(want)], want):
            g = drows[ri]
            if ri in av:
                g = g + av[ri]
            o_ref[...] = g.astype(o_ref.dtype)

        @pl.when(i == 0)
        def _():
            for o_ref in orefs[len(want):]:
                o_ref[...] = jnp.zeros_like(o_ref)

        for o_ref, g in zip(orefs[len(want):], dpars):
            o_ref[...] += g

    add_ops = [adds[k] for k in add_keys]
    out_specs = ([pl.BlockSpec((tm, rows[ri][2]), lambda i: (i, 0)) for ri in want]
                 + [pl.BlockSpec((p[0].shape[0], p[2]), lambda i: (0, 0)) for p in params])
    out_shape = ([jax.ShapeDtypeStruct((s, rows[ri][2]), row_grads[ri]) for ri in want]
                 + [jax.ShapeDtypeStruct((p[0].shape[0], p[2]), F32) for p in params])
    res = pl.pallas_call(
        body, name=name, grid=(s // tm,),
        in_specs=_row_specs(rows, tm) + _par_specs(params) + _row_specs(couts, tm) + _row_specs(add_ops, tm),
        out_specs=out_specs, out_shape=out_shape,
        compiler_params=_cparams(("arbitrary",)),
    )(*[r[0] for r in rows], *[p[0] for p in params], *[c[0] for c in couts], *[a[0] for a in add_ops])
    return res[:len(want)], res[len(want):]


def _silu(v):
    return v * jax.nn.sigmoid(v)


def _rms(v, g):
    return v * lax.rsqrt(jnp.mean(v * v, axis=-1, keepdims=True) + EPS) * g


SUBLANES = 8
CONV_ROWS = 256


def _halo(shifts):
    up = lambda v: -(-v // SUBLANES) * SUBLANES
    return up(max(0, -min(shifts))), up(max(0, max(shifts)))


def _fill_halo(xp_sc, x_ref, front, back):
    s = x_ref.shape[0]
    if front:
        xp_sc[0:front, :] = jnp.zeros((front, LANES), F32)
    if back:
        xp_sc[front + s:front + s + back, :] = jnp.zeros((back, LANES), F32)
    xp_sc[front:front + s, :] = x_ref[...].astype(F32)


def _shift_conv(x, w, b, shifts, *, name):
    s, c = x.shape
    tr = min(CONV_ROWS, s)
    nk = len(shifts)
    front, back = _halo(shifts)

    def body(*refs):
        if b is None:
            x_ref, w_ref, o_ref, xp_sc = refs
        else:
            x_ref, w_ref, b_ref, o_ref, xp_sc = refs
        _fill_halo(xp_sc, x_ref, front, back)

        def chunk(r, carry):
            base = pl.multiple_of(r * tr, tr)
            acc = jnp.zeros((tr, LANES), F32) if b is None else jnp.broadcast_to(b_ref[...], (tr, LANES))
            for kk in range(nk):
                acc = acc + xp_sc[pl.ds(base + front + shifts[kk], tr), :] * w_ref[kk:kk + 1, :]
            o_ref[pl.ds(base, tr), :] = acc
            return carry

        lax.fori_loop(0, s // tr, chunk, 0)

    strip = pl.BlockSpec((s, LANES), lambda cb: (0, cb))
    in_specs = [strip, pl.BlockSpec((nk, LANES), lambda cb: (0, cb))]
    args = [x, w]
    if b is not None:
        in_specs.append(pl.BlockSpec((1, LANES), lambda cb: (0, cb)))
        args.append(b)
    return pl.pallas_call(
        body, name=name, grid=(c // LANES,), in_specs=in_specs, out_specs=strip,
        out_shape=jax.ShapeDtypeStruct((s, c), F32),
        scratch_shapes=[pltpu.VMEM((front + s + back, LANES), F32)],
        compiler_params=_cparams(("parallel",)),
    )(*args)


def _conv_fwd(x, w, b, *, name):
    k = w.shape[0]
    return _shift_conv(x, w, b, [kk - (k - 1) for kk in range(k)], name=name)


def _conv_bwd_x(dy, w, *, name):
    k = w.shape[0]
    return _shift_conv(dy, w, None, [(k - 1) - kk for kk in range(k)], name=name)


def _conv_bwd_w(x, dy, k, *, name):
    s, c = x.shape
    tr = min(CONV_ROWS, s)
    shifts = [kk - (k - 1) for kk in range(k)]
    front, back = _halo(shifts)

    def fold(v):
        return jnp.sum(v.reshape(tr // SUBLANES, SUBLANES, LANES), axis=0)

    def body(x_ref, dy_ref, dw_ref, db_ref, xp_sc, dw_sc, db_sc):
        _fill_halo(xp_sc, x_ref, front, back)
        dw_sc[...] = jnp.zeros_like(dw_sc)
        db_sc[...] = jnp.zeros_like(db_sc)

        def chunk(r, carry):
            base = pl.multiple_of(r * tr, tr)
            dyv = dy_ref[pl.ds(base, tr), :]
            db_sc[...] += fold(dyv)
            for kk in range(k):
                dw_sc[kk] += fold(xp_sc[pl.ds(base + front + shifts[kk], tr), :] * dyv)
            return carry

        lax.fori_loop(0, s // tr, chunk, 0)
        db_ref[...] = jnp.sum(db_sc[...], axis=0, keepdims=True)
        for kk in range(k):
            dw_ref[kk:kk + 1, :] = jnp.sum(dw_sc[kk], axis=0, keepdims=True)

    strip = pl.BlockSpec((s, LANES), lambda cb: (0, cb))
    return pl.pallas_call(
        body, name=name, grid=(c // LANES,), in_specs=[strip, strip],
        out_specs=[pl.BlockSpec((k, LANES), lambda cb: (0, cb)), pl.BlockSpec((1, LANES), lambda cb: (0, cb))],
        out_shape=[jax.ShapeDtypeStruct((k, c), F32), jax.ShapeDtypeStruct((1, c), F32)],
        scratch_shapes=[pltpu.VMEM((front + s + back, LANES), F32), pltpu.VMEM((k, SUBLANES, LANES), F32),
                        pltpu.VMEM((SUBLANES, LANES), F32)],
        compiler_params=_cparams(("parallel",)),
    )(x, dy)


_DIMS = {"nn": ((1,), (0,)), "nt": ((1,), (1,)), "tn": ((0,), (0,))}


def _bd(a, b, mode):
    return lax.dot_general(a.astype(BF16), b.astype(BF16), (_DIMS[mode], ((), ())), preferred_element_type=F32)


@functools.partial(jax.custom_vjp, nondiff_argnums=(2,))
def _bdot(a, b, mode):
    return _bd(a, b, mode)


def _bdot_fwd(a, b, mode):
    return _bd(a, b, mode), (a, b)


def _bdot_bwd(mode, res, g):
    a, b = res
    if mode == "nn":
        return _bd(g, b, "nt"), _bd(a, g, "tn")
    if mode == "nt":
        return _bd(g, b, "nn"), _bd(g, a, "tn")
    return _bd(b, g, "nt"), _bd(a, g, "nn")


_bdot.defvjp(_bdot_fwd, _bdot_bwd)


def _split3(v):
    hi = v.astype(BF16)
    r1 = v - hi.astype(F32)
    mid = r1.astype(BF16)
    lo = (r1 - mid.astype(F32)).astype(BF16)
    return hi, mid, lo


def _mask_dot(mask01, v, mode, terms=3):
    out = None
    for part in _split3(v)[:terms]:
        if mode == "vn":
            t = lax.dot_general(part, mask01, (_DIMS["nn"], ((), ())), preferred_element_type=F32)
        else:
            t = lax.dot_general(mask01, part, (_DIMS[mode], ((), ())), preferred_element_type=F32)
        out = t if out is None else out + t
    return out


def _lower_tri(n):
    r = lax.broadcasted_iota(jnp.int32, (n, n), 0)
    c = lax.broadcasted_iota(jnp.int32, (n, n), 1)
    return (r >= c).astype(BF16)


@jax.custom_vjp
def _tri_dot(w):
    return _mask_dot(_lower_tri(w.shape[0]), w, "nn", terms=2)


def _tri_dot_fwd(w):
    return _tri_dot(w), None


def _tri_dot_bwd(_, g):
    return (_mask_dot(_lower_tri(g.shape[0]), g, "tn", terms=2),)


_tri_dot.defvjp(_tri_dot_fwd, _tri_dot_bwd)


def _cumsum_lanes(x, *, reverse, name):
    h, s = x.shape
    n = s // LANES

    def body(x_ref, o_ref):
        r = lax.broadcasted_iota(jnp.int32, (LANES, LANES), 0)
        c = lax.broadcasted_iota(jnp.int32, (LANES, LANES), 1)
        m01 = ((r >= c) if reverse else (r <= c)).astype(BF16)

        def step(t, carry):
            ci = (n - 1 - t) if reverse else t
            at = pl.ds(pl.multiple_of(ci * LANES, LANES), LANES)
            blk = x_ref[:, at]
            o_ref[:, at] = _mask_dot(m01, blk, "vn") + carry
            return carry + jnp.sum(blk, axis=1, keepdims=True)

        lax.fori_loop(0, n, step, jnp.zeros((h, 1), F32))

    return pl.pallas_call(body, name=name, out_shape=jax.ShapeDtypeStruct((h, s), F32),
                          compiler_params=_cparams())(x)


SSD_PAIRS = SSD_HPG // 2


def _ssd2_chunk(xs, dt, bm, cm, hin, a, dsk, head0):
    n = CHUNK
    row = lax.broadcasted_iota(jnp.int32, (n, n), 0)
    col = lax.broadcasted_iota(jnp.int32, (n, n), 1)
    lower = row >= col
    ustrict = (row > col).astype(F32)
    lane = lax.broadcasted_iota(jnp.int32, (1, LANES), 1)
    sub = lax.broadcasted_iota(jnp.int32, (n, 1), 0)
    e_first = (sub == 0).astype(F32)
    e_last = (sub == n - 1).astype(F32)
    lane0 = (lane == 0).astype(F32)
    half_l = [(lane < LANES // 2).astype(F32), (lane >= LANES // 2).astype(F32)]
    half_s = [(sub < LANES // 2).astype(F32), (sub >= LANES // 2).astype(F32)]
    cb = _bdot(cm, bm, "nt")
    da = dt * (-jnp.exp(a))
    ys, houts = [], []
    for pr in range(SSD_PAIRS):
        y = jnp.zeros((n, LANES), F32)
        xdte = jnp.zeros((n, LANES), F32)
        lane_gain = jnp.zeros((n, LANES), F32)
        row_gain = jnp.zeros((LANES, 1), F32)
        for hf in range(2):
            oh = (lane == head0 + 2 * pr + hf).astype(F32)
            dt_col = jnp.sum(dt * oh, axis=1, keepdims=True)
            da_col = jnp.sum(da * oh, axis=1, keepdims=True)
            dsk_h = jnp.sum(dsk * oh, axis=1, keepdims=True)
            seg = _tri_dot(da_col * ustrict)
            decay = jnp.where(lower, jnp.exp(seg), 0.0)
            cs_col = jnp.sum(seg * lane0, axis=1, keepdims=True) + jnp.sum(da_col * e_first, axis=0, keepdims=True)
            total = jnp.sum(cs_col * e_last, axis=0, keepdims=True)
            xh = xs[pr] * half_l[hf]
            xd = xh * dt_col
            y = y + _bdot(cb * decay, xd, "nn") + xh * dsk_h
            xdte = xdte + xd * jnp.exp(total - cs_col)
            lane_gain = lane_gain + jnp.exp(cs_col) * half_l[hf]
            row_gain = row_gain + jnp.exp(total) * half_s[hf]
        houts.append(hin[pr] * row_gain + _bdot(xdte, bm, "tn"))
        ys.append(y + _bdot(cm, hin[pr], "nt") * lane_gain)
    return ys, houts


SSD_STEP = 4


def _ssd2_steps(s):
    per = min(SSD_STEP, s // CHUNK)
    return per, s // (CHUNK * per)


def _ssd2_multi(xs, dt, bm, cm, hin, a, dsk, head0):
    ys = []
    for k in range(len(dt)):
        y, hin = _ssd2_chunk(xs[k], dt[k], bm[k], cm[k], hin, a, dsk, head0)
        ys.append(y)
    return ys, hin


def _ssd2_specs(per, nc, rev):
    cc = (lambda c: nc - 1 - c) if rev else (lambda c: c)
    rows = CHUNK * per
    act = pl.BlockSpec((rows, SSD_PAIRS * LANES), lambda c, g: (cc(c), g))
    row = pl.BlockSpec((rows, LANES), lambda c, g: (cc(c), 0))
    bmat = pl.BlockSpec((rows, LANES), lambda c, g: (cc(c), D_SSD // LANES + g))
    cmat = pl.BlockSpec((rows, LANES), lambda c, g: (cc(c), D_SSD // LANES + SSD_GROUPS + g))
    par = pl.BlockSpec((1, LANES), lambda c, g: (0, 0))
    hs = pl.BlockSpec((1, SSD_PAIRS, LANES, D_STATE), lambda c, g: (cc(c), g, 0, 0))
    return act, row, bmat, cmat, par, hs


def _chunk_rows(ref, k):
    return ref[k * CHUNK:(k + 1) * CHUNK, :]


def _pair_cols(ref, k):
    return [ref[k * CHUNK:(k + 1) * CHUNK, pr * LANES:(pr + 1) * LANES] for pr in range(SSD_PAIRS)]


def _ssd2_fwd(dt, xbc, a, dsk, *, name):
    s = xbc.shape[0]
    per, nc = _ssd2_steps(s)
    act, row, bmat, cmat, par, hs = _ssd2_specs(per, nc, False)

    def body(xs_ref, dt_ref, bm_ref, cm_ref, a_ref, dsk_ref, y_ref, hs_ref, h_sc):
        c, g = pl.program_id(0), pl.program_id(1)

        @pl.when(c == 0)
        def _():
            h_sc[pl.ds(g * SSD_PAIRS, SSD_PAIRS)] = jnp.zeros((SSD_PAIRS, LANES, D_STATE), F32)

        hin = [h_sc[g * SSD_PAIRS + pr] for pr in range(SSD_PAIRS)]
        ks = range(per)
        ys, houts = _ssd2_multi([_pair_cols(xs_ref, k) for k in ks], [_chunk_rows(dt_ref, k) for k in ks],
                                [_chunk_rows(bm_ref, k) for k in ks], [_chunk_rows(cm_ref, k) for k in ks],
                                hin, a_ref[...], dsk_ref[...], g * SSD_HPG)
        for pr in range(SSD_PAIRS):
            for k in ks:
                y_ref[k * CHUNK:(k + 1) * CHUNK, pr * LANES:(pr + 1) * LANES] = ys[k][pr]
            hs_ref[0, pr] = hin[pr]
            h_sc[g * SSD_PAIRS + pr] = houts[pr]

    return pl.pallas_call(
        body, name=name, grid=(nc, SSD_GROUPS),
        in_specs=[act, row, bmat, cmat, par, par], out_specs=[act, hs],
        out_shape=[jax.ShapeDtypeStruct((s, D_SSD), F32),
                   jax.ShapeDtypeStruct((nc, SSD_HEADS // 2, LANES, D_STATE), F32)],
        scratch_shapes=[pltpu.VMEM((SSD_HEADS // 2, LANES, D_STATE), F32)],
        compiler_params=_cparams(("arbitrary", "arbitrary")),
    )(xbc, dt, xbc, xbc, a, dsk)


def _ssd2_bwd(dt, xbc, a, dsk, hsave, dy, *, carry=None, name):
    s = xbc.shape[0]
    per, nc = _ssd2_steps(s)
    act, row, bmat, cmat, par, hs = _ssd2_specs(per, nc, True)
    gmat = pl.BlockSpec((CHUNK * per, LANES), lambda c, g: (nc - 1 - c, g))

    def body(*refs):
        ins, outs, (dh_sc,), comm = _carried(carry, refs, 8, 6, 1)
        xs_ref, dt_ref, bm_ref, cm_ref, a_ref, dsk_ref, hs_ref, dy_ref = ins
        dxs_ref, ddt_ref, dbm_ref, dcm_ref, da_ref, ddsk_ref = outs
        c, g = pl.program_id(0), pl.program_id(1)
        if carry is not None:
            @pl.when((c == 0) & (g == 0))
            def _():
                carry.start(*comm)

        @pl.when(c == 0)
        def _():
            dh_sc[pl.ds(g * SSD_PAIRS, SSD_PAIRS)] = jnp.zeros((SSD_PAIRS, LANES, D_STATE), F32)

        @pl.when((c == 0) & (g == 0))
        def _():
            da_ref[...] = jnp.zeros_like(da_ref)
            ddsk_ref[...] = jnp.zeros_like(ddsk_ref)

        @pl.when(g == 0)
        def _():
            ddt_ref[...] = jnp.zeros_like(ddt_ref)

        head0 = g * SSD_HPG
        ks = range(per)
        prim = ([_pair_cols(xs_ref, k) for k in ks], [_chunk_rows(dt_ref, k) for k in ks],
                [_chunk_rows(bm_ref, k) for k in ks], [_chunk_rows(cm_ref, k) for k in ks],
                [hs_ref[0, pr] for pr in range(SSD_PAIRS)], a_ref[...], dsk_ref[...])
        _, vjp = jax.vjp(lambda *p: _ssd2_multi(*p, head0), *prim)
        cot = ([_pair_cols(dy_ref, k) for k in ks], [dh_sc[g * SSD_PAIRS + pr] for pr in range(SSD_PAIRS)])
        dxs, ddt, dbm, dcm, dhin, da, ddsk = vjp(cot)
        for pr in range(SSD_PAIRS):
            for k in ks:
                dxs_ref[k * CHUNK:(k + 1) * CHUNK, pr * LANES:(pr + 1) * LANES] = dxs[k][pr]
            dh_sc[g * SSD_PAIRS + pr] = dhin[pr]
        for k in ks:
            ddt_ref[k * CHUNK:(k + 1) * CHUNK, :] += ddt[k]
            dbm_ref[k * CHUNK:(k + 1) * CHUNK, :] = dbm[k]
            dcm_ref[k * CHUNK:(k + 1) * CHUNK, :] = dcm[k]
        da_ref[...] += da
        ddsk_ref[...] += ddsk
        if carry is not None:
            @pl.when((c == nc - 1) & (g == SSD_GROUPS - 1))
            def _():
                carry.wait(*comm)

    extra = carry if carry is not None else _Exchange([], "c", False)
    return pl.pallas_call(
        body, name=name, grid=(nc, SSD_GROUPS),
        in_specs=[act, row, bmat, cmat, par, par, hs, act] + extra.in_specs,
        out_specs=[act, row, gmat, gmat, par, par] + extra.out_specs,
        out_shape=[jax.ShapeDtypeStruct((s, D_SSD), F32), jax.ShapeDtypeStruct((s, LANES), F32),
                   jax.ShapeDtypeStruct((s, SSD_GROUPS * D_STATE), F32),
                   jax.ShapeDtypeStruct((s, SSD_GROUPS * D_STATE), F32),
                   jax.ShapeDtypeStruct((1, LANES), F32), jax.ShapeDtypeStruct((1, LANES), F32)] + extra.out_shape,
        scratch_shapes=[pltpu.VMEM((SSD_HEADS // 2, LANES, D_STATE), F32)]
        + (carry.scratch if carry is not None else []),
        compiler_params=_cparams(("arbitrary", "arbitrary")),
    )(xbc, dt, xbc, xbc, a, dsk, hsave, dy, *extra.arrs)


FOX_BLOCK = 512
NEG = -1e30
PAIRS = FOX_HEADS // 2
HALF = LANES // 2


def _causal(t):
    return lax.broadcasted_iota(jnp.int32, (t, t), 0) >= lax.broadcasted_iota(jnp.int32, (t, t), 1)


def _first_half():
    return lax.broadcasted_iota(jnp.int32, (1, LANES), 1) < HALF


def _pair_bias(c_ref, hh, qblock, kblock, t):
    lane = lax.broadcasted_iota(jnp.int32, (1, LANES), 1)
    cq = c_ref[hh, :, pl.ds(pl.multiple_of(qblock * t, LANES), LANES)]
    cref = jnp.sum(jnp.where(lane == 0, cq, 0.0), axis=1, keepdims=True)
    return cref - c_ref[hh, :, pl.ds(pl.multiple_of(kblock * t, LANES), t)]


def _fox_dopack(do, o, *, tm=256, name):
    s, d = do.shape
    tm = min(tm, s)

    def body(do_ref, o_ref, out_ref):
        dov = do_ref[...].astype(BF16)
        prod = dov.astype(F32) * o_ref[...]
        r = lax.broadcasted_iota(jnp.int32, (d, LANES), 0)
        c = lax.broadcasted_iota(jnp.int32, (d, LANES), 1)
        heads = ((r >= c * FOX_HEADDIM) & (r < (c + 1) * FOX_HEADDIM)).astype(BF16)
        negd = -_mask_dot(heads, prod, "vn")
        hr = lax.broadcasted_iota(jnp.int32, (LANES, 2 * d), 0)
        col = lax.broadcasted_iota(jnp.int32, (LANES, 2 * d), 1)
        base = (hr >> 1) * (2 * LANES) + jnp.where((hr & 1) == 0, HALF, LANES)
        terms = None
        for kk, part in enumerate(_split3(negd)):
            place = ((col == base + kk) & (hr < FOX_HEADS)).astype(BF16)
            tk = lax.dot_general(part, place, (_DIMS["nn"], ((), ())), preferred_element_type=F32)
            terms = tk if terms is None else terms + tk
        first = _first_half()
        zero = jnp.zeros((tm, LANES), BF16)
        pieces = []
        for hp in range(PAIRS):
            blk = dov[:, hp * LANES:(hp + 1) * LANES]
            pieces += [jnp.where(first, blk, zero), jnp.where(first, zero, blk)]
        out_ref[...] = (jnp.concatenate(pieces, axis=1).astype(F32) + terms).astype(BF16)

    row = pl.BlockSpec((tm, d), lambda i: (i, 0))
    return pl.pallas_call(body, name=name, grid=(s // tm,), in_specs=[row, row],
                          out_specs=pl.BlockSpec((tm, 2 * d), lambda i: (i, 0)),
                          out_shape=jax.ShapeDtypeStruct((s, 2 * d), BF16),
                          compiler_params=_cparams(("parallel",)))(do, o)


FOX_DEAD = 110.0
BOUND_SLACK = 1.001


def _fox_block_bounds(qkv, c):
    s = qkv.shape[0]
    t = min(FOX_BLOCK, s)
    k = qkv[:, D_FOX:2 * D_FOX].astype(F32).reshape(s // t, t, FOX_HEADS, FOX_HEADDIM)
    kn = jnp.sqrt(jnp.max(jnp.sum(k * k, axis=-1), axis=1)).T
    return kn, c[:, 0, ::t], c[:, 0, t - 1::t]


def _fox_live(qmax, mmin, kn_ref, cs_ref, ce_ref, h, qblock, kblock):
    bound = qmax * (kn_ref[h, kblock] * BOUND_SLACK) + (cs_ref[h, qblock] - ce_ref[h, kblock])
    return bound - mmin > -FOX_DEAD


def _run_live(live, work):
    @pl.when(live[0] & live[1])
    def _():
        work(0)
        work(1)

    @pl.when(live[0] & jnp.logical_not(live[1]))
    def _():
        work(0)

    @pl.when(jnp.logical_not(live[0]) & live[1])
    def _():
        work(1)


def _fox3_fwd(qkv, c, bounds, *, carry=None, name):
    assert qkv.shape[0] // min(FOX_BLOCK, qkv.shape[0]) < LANES - 1
    s = qkv.shape[0]
    t = min(FOX_BLOCK, s)
    nb = s // t
    nt = (((1,), (1,)), ((), ()))
    nn = (((1,), (0,)), ((), ()))
    scale = FOX_HEADDIM ** -0.5

    def body(*refs):
        ins, (o_ref, mrun_ref, qmax_ref, mmin_ref), (m_sc, acc_sc, mt_sc), comm = _carried(carry, refs, 7, 4, 3)
        q_ref, k_ref, v_ref, c_ref, kn_ref, cs_ref, ce_ref = ins
        hp, i = pl.program_id(0), pl.program_id(1)
        if carry is not None:
            @pl.when((hp == 0) & (i == 0))
            def _():
                carry.start(*comm)

        first = _first_half()
        lane = lax.broadcasted_iota(jnp.int32, (1, LANES), 1)
        m_sc[...] = jnp.full_like(m_sc, NEG)
        acc_sc[...] = jnp.zeros_like(acc_sc)
        mt_sc[...] = jnp.zeros_like(mt_sc)
        q2 = q_ref[...] * scale
        zero = jnp.zeros_like(q2)
        qs = [jnp.where(first, q2, zero), jnp.where(first, zero, q2)]

        def head_block(hh, j, k2, vxh, masked):
            sc = lax.dot_general(qs[hh], k2, nt, preferred_element_type=F32) + _pair_bias(c_ref, hh, i, j, t)
            if masked:
                sc = jnp.where(_causal(t), sc, NEG)
            m_prev = m_sc[hh]
            m_new = jnp.maximum(m_prev, jnp.max(sc, axis=1, keepdims=True))
            pr = jnp.exp(sc - jnp.tile(m_new, (1, t // LANES))).astype(BF16)
            pv = lax.dot_general(pr, vxh, nn, preferred_element_type=F32)
            acc_sc[hh] = jnp.exp(m_prev - m_new) * acc_sc[hh] + pv
            m_sc[hh] = m_new

        def step(j, masked):
            at = pl.ds(pl.multiple_of(j * t, t), t)
            k2, v2 = k_ref[at, :], v_ref[at, :]
            one = jnp.ones_like(v2)
            vx = [jnp.where(first, v2, one), jnp.where(first, one, v2)]
            if masked:
                for hh in range(2):
                    head_block(hh, j, k2, vx[hh], True)
            else:
                live = [_fox_live(qmax_ref[2 * hp + hh, i], mmin_ref[2 * hp + hh, i], kn_ref, cs_ref, ce_ref,
                                  2 * hp + hh, i, j) for hh in range(2)]
                _run_live(live, lambda hh: head_block(hh, j, k2, vx[hh], False))
            for hh in range(2):
                mt_sc[hh] = jnp.where(lane == j, m_sc[hh], mt_sc[hh])

        step(i, True)
        for hh in range(2):
            qf = qs[hh].astype(F32)
            qmax_ref[2 * hp + hh, i] = jnp.sqrt(jnp.max(jnp.sum(qf * qf, axis=1, keepdims=True)))
            mmin_ref[2 * hp + hh, i] = jnp.min(m_sc[hh])
        lax.fori_loop(0, i, lambda n, carry: (step(i - 1 - n, False), carry)[1], 0)
        acc_a, acc_b = acc_sc[0], acc_sc[1]
        den_a = jnp.where(first, pltpu.roll(acc_a, HALF, 1), acc_a)
        den_b = jnp.where(first, acc_b, pltpu.roll(acc_b, HALF, 1))
        o_ref[...] = jnp.where(first, acc_a / den_a, acc_b / den_b)
        for hh, den in enumerate((den_a, den_b)):
            mrun_ref[:, hh * LANES:(hh + 1) * LANES] = jnp.where(lane == LANES - 1, m_sc[hh] + jnp.log(den), mt_sc[hh])
        if carry is not None:
            @pl.when((pl.program_id(0) == PAIRS - 1) & (i == nb - 1))
            def _():
                carry.wait(*comm)

    nq = D_FOX // LANES
    smem = pl.BlockSpec(memory_space=pltpu.SMEM)
    extra = carry if carry is not None else _Exchange([], "c", False)
    return pl.pallas_call(
        body, name=name, grid=(PAIRS, nb),
        in_specs=[pl.BlockSpec((t, LANES), lambda hp, i: (i, hp)),
                  pl.BlockSpec((s, LANES), lambda hp, i: (0, nq + hp)),
                  pl.BlockSpec((s, LANES), lambda hp, i: (0, 2 * nq + hp)),
                  pl.BlockSpec((2, 1, s), lambda hp, i: (hp, 0, 0))] + [smem] * 3 + extra.in_specs,
        out_specs=[pl.BlockSpec((t, LANES), lambda hp, i: (i, hp)),
                   pl.BlockSpec((t, 2 * LANES), lambda hp, i: (i, hp)), smem, smem] + extra.out_specs,
        out_shape=[jax.ShapeDtypeStruct((s, D_FOX), F32), jax.ShapeDtypeStruct((s, 2 * D_FOX), F32),
                   jax.ShapeDtypeStruct((FOX_HEADS, nb), F32), jax.ShapeDtypeStruct((FOX_HEADS, nb), F32)]
        + extra.out_shape,
        scratch_shapes=[pltpu.VMEM((2, t, LANES), F32)] * 3 + (carry.scratch if carry is not None else []),
        compiler_params=_cparams(("arbitrary", "arbitrary")),
    )(qkv, qkv, qkv, c, *bounds, *extra.arrs)


def _fox3_bwd(qkv, c, bounds, dox, mrun, *, carry=None, name):
    s = qkv.shape[0]
    t = min(FOX_BLOCK, s)
    nb = s // t
    nt = (((1,), (1,)), ((), ()))
    nn = (((1,), (0,)), ((), ()))
    tn = (((0,), (0,)), ((), ()))
    scale = FOX_HEADDIM ** -0.5

    def body(*refs):
        ins, outs, scratch, comm = _carried(carry, refs, 11, 4, 3)
        k_ref, v_ref, q_ref, c_ref, do_ref, mrun_ref, kn_ref, cs_ref, ce_ref, qmax_ref, mmin_ref = ins
        dq_ref, dk_ref, dv_ref, dc_ref = outs
        dk_sc, dv_sc, dc_sc = scratch
        hp, j = pl.program_id(0), pl.program_id(1)
        if carry is not None:
            @pl.when((hp == 0) & (j == 0))
            def _():
                carry.start(*comm)

        first = _first_half()
        halves = [first, jnp.logical_not(first)]
        lane = lax.broadcasted_iota(jnp.int32, (1, LANES), 1)

        @pl.when(j == 0)
        def _():
            dq_ref[...] = jnp.zeros_like(dq_ref)

        dk_sc[...] = jnp.zeros_like(dk_sc)
        dv_sc[...] = jnp.zeros_like(dv_sc)
        dc_sc[...] = jnp.zeros_like(dc_sc)
        k2, v2 = k_ref[...], v_ref[...]
        one = jnp.ones_like(v2)
        vx = [jnp.where(first, v2, one), jnp.where(first, one, v2)]

        def pick(stats, which):
            return jnp.sum(jnp.where(lane == which, stats, 0.0), axis=1, keepdims=True)

        def head_block(hh, i, at, qsh, stats, masked):
            mine = slice(hh * LANES, (hh + 1) * LANES)
            dov = do_ref[at, mine]
            sc = lax.dot_general(qsh, k2, nt, preferred_element_type=F32) + _pair_bias(c_ref, hh, i, j, t)
            mj = pick(stats, j)
            gain = jnp.broadcast_to(jnp.exp(mj - pick(stats, LANES - 1)), (t, LANES))
            mj = jnp.broadcast_to(mj, (t, LANES))
            pb = jnp.exp(sc - jnp.tile(mj, (1, t // LANES))).astype(BF16)
            if masked:
                pb = jnp.where(_causal(t), pb, jnp.zeros_like(pb))
            pr = pb.astype(F32) * jnp.tile(gain, (1, t // LANES))
            ds = pr * lax.dot_general(dov, vx[hh], nt, preferred_element_type=F32)
            dc_sc[hh] -= jnp.sum(ds, axis=0, keepdims=True)
            dsb = ds.astype(BF16)
            dvh = lax.dot_general(pr.astype(BF16), dov, tn, preferred_element_type=F32)
            dv_sc[...] += jnp.where(halves[hh], dvh, 0.0)
            dk_sc[...] += lax.dot_general(dsb, qsh, tn, preferred_element_type=F32)
            dqh = lax.dot_general(dsb, k2, nn, preferred_element_type=F32)
            dq_ref[at, :] += jnp.where(halves[hh], dqh, 0.0)

        def step(i, masked):
            at = pl.ds(pl.multiple_of(i * t, t), t)
            q2 = q_ref[at, :] * scale
            zero = jnp.zeros_like(q2)
            qs = [jnp.where(first, q2, zero), jnp.where(first, zero, q2)]
            stats = [mrun_ref[at, hh * LANES:(hh + 1) * LANES] for hh in range(2)]
            if masked:
                for hh in range(2):
                    head_block(hh, i, at, qs[hh], stats[hh], True)
            else:
                live = [_fox_live(qmax_ref[2 * hp + hh, i], mmin_ref[2 * hp + hh, i], kn_ref, cs_ref, ce_ref,
                                  2 * hp + hh, i, j) for hh in range(2)]
                _run_live(live, lambda hh: head_block(hh, i, at, qs[hh], stats[hh], False))

        step(j, True)
        lax.fori_loop(j + 1, nb, lambda i, carry: (step(i, False), carry)[1], 0)
        dk_ref[...] = dk_sc[...]
        dv_ref[...] = dv_sc[...].astype(dv_ref.dtype)
        dc_ref[...] = dc_sc[...]
        if carry is not None:
            @pl.when((pl.program_id(0) == PAIRS - 1) & (j == nb - 1))
            def _():
                carry.wait(*comm)

    nq = D_FOX // LANES
    blk = pl.BlockSpec((t, LANES), lambda hp, j: (j, hp))
    smem = pl.BlockSpec(memory_space=pltpu.SMEM)
    extra = carry if carry is not None else _Exchange([], "c", False)
    return pl.pallas_call(
        body, name=name, grid=(PAIRS, nb),
        in_specs=[pl.BlockSpec((t, LANES), lambda hp, j: (j, nq + hp)),
                  pl.BlockSpec((t, LANES), lambda hp, j: (j, 2 * nq + hp)),
                  pl.BlockSpec((s, LANES), lambda hp, j: (0, hp)),
                  pl.BlockSpec((2, 1, s), lambda hp, j: (hp, 0, 0)),
                  pl.BlockSpec((s, 2 * LANES), lambda hp, j: (0, hp)),
                  pl.BlockSpec((s, 2 * LANES), lambda hp, j: (0, hp))] + [smem] * 5 + extra.in_specs,
        out_specs=[pl.BlockSpec((s, LANES), lambda hp, j: (0, hp)), blk, blk,
                   pl.BlockSpec((2, 1, t), lambda hp, j: (hp, 0, j))] + extra.out_specs,
        out_shape=[jax.ShapeDtypeStruct((s, D_FOX), F32), jax.ShapeDtypeStruct((s, D_FOX), F32),
                   jax.ShapeDtypeStruct((s, D_FOX), BF16), jax.ShapeDtypeStruct((FOX_HEADS, 1, s), F32)]
        + extra.out_shape,
        scratch_shapes=[pltpu.VMEM((t, LANES), F32), pltpu.VMEM((t, LANES), F32), pltpu.VMEM((2, 1, t), F32)]
        + (carry.scratch if carry is not None else []),
        compiler_params=_cparams(("arbitrary", "arbitrary")),
    )(qkv, qkv, qkv, c, dox, mrun, *bounds, *extra.arrs)


def _final(x1, out1, g, tgt, *, tm=512, name):
    s, d = x1.shape
    tm = min(tm, s)

    def body(x_ref, o_ref, g_ref, t_ref, dx_ref, do_ref, dg_ref, loss_ref):
        i = pl.program_id(0)

        @pl.when(i == 0)
        def _():
            dg_ref[...] = jnp.zeros_like(dg_ref)
            loss_ref[...] = jnp.zeros_like(loss_ref)

        tv = t_ref[...]

        def lossf(xv, ov, gv):
            err = jnp.square(xv + _rms(ov, gv) - tv)
            return 0.5 * jnp.sum(jnp.mean(err, axis=-1, keepdims=True), axis=0, keepdims=True)

        val, vjp = jax.vjp(lossf, x_ref[...], o_ref[...], g_ref[...])
        dx, do, dg = vjp(jnp.ones((1, 1), F32))
        dx_ref[...] = dx
        do_ref[...] = do.astype(do_ref.dtype)
        dg_ref[...] += dg
        loss_ref[...] += val

    row = pl.BlockSpec((tm, d), lambda i: (i, 0))
    par = pl.BlockSpec((1, d), lambda i: (0, 0))
    return pl.pallas_call(
        body, name=name, grid=(s // tm,), in_specs=[row, row, par, row],
        out_specs=[row, row, par, pl.BlockSpec((1, 1), lambda i: (0, 0))],
        out_shape=[jax.ShapeDtypeStruct((s, d), F32), jax.ShapeDtypeStruct((s, d), BF16),
                   jax.ShapeDtypeStruct((1, d), F32), jax.ShapeDtypeStruct((1, 1), F32)],
        compiler_params=_cparams(("arbitrary",)),
    )(x1, out1, g, tgt)


def _row_tile(r):
    return LANES if r % LANES == 0 else r


def _sum_slots(parts, *, out_dtype=F32, name):
    p, r, c = parts.shape
    tr = _row_tile(r)

    def body(p_ref, o_ref):
        acc = p_ref[0].astype(F32)
        for k in range(1, p):
            acc = acc + p_ref[k].astype(F32)
        o_ref[...] = acc.astype(o_ref.dtype)

    return pl.pallas_call(
        body, name=name, grid=(r // tr,),
        in_specs=[pl.BlockSpec((p, tr, c), lambda i: (0, i, 0))],
        out_specs=pl.BlockSpec((tr, c), lambda i: (i, 0)),
        out_shape=jax.ShapeDtypeStruct((r, c), out_dtype),
        compiler_params=_cparams(("parallel",)),
    )(parts)


def _adamw(w, gparts, m, v, *, name):
    r, c = w.shape
    p = gparts.shape[0]
    tr = _row_tile(r)

    def body(w_ref, g_ref, m_ref, v_ref, go_ref, d_ref, mo_ref, vo_ref):
        g = g_ref[0].astype(F32)
        for k in range(1, p):
            g = g + g_ref[k].astype(F32)
        mn = ADAM_B1 * m_ref[...] + (1.0 - ADAM_B1) * g
        vn = ADAM_B2 * v_ref[...] + (1.0 - ADAM_B2) * jnp.square(g)
        m_hat = mn / (1.0 - ADAM_B1 ** ADAM_STEP)
        v_hat = vn / (1.0 - ADAM_B2 ** ADAM_STEP)
        go_ref[...] = g
        d_ref[...] = -ADAM_LR * (m_hat / (jnp.sqrt(v_hat) + ADAM_EPS) + ADAM_WD * w_ref[...])
        mo_ref[...] = mn
        vo_ref[...] = vn

    spec = pl.BlockSpec((tr, c), lambda i: (i, 0))
    return pl.pallas_call(
        body, name=name, grid=(r // tr,),
        in_specs=[spec, pl.BlockSpec((p, tr, c), lambda i: (0, i, 0)), spec, spec],
        out_specs=[spec] * 4, out_shape=[jax.ShapeDtypeStruct((r, c), F32)] * 4,
        compiler_params=_cparams(("parallel",)),
    )(w, gparts, m, v)


_FLIPS = {
    "xy": [(1, 0, 0), (0, 1, 0), (1, 1, 0)],
    "c": [(0, 0, 1)],
    "xyc": [(fx, fy, fc) for fx in (0, 1) for fy in (0, 1) for fc in (0, 1) if (fx, fy, fc) != (0, 0, 0)],
}


def _slot(mode, px, py, pc):
    return {"xy": 2 * px + py, "c": pc, "xyc": 4 * px + 2 * py + pc}[mode]


class _Exchange:
    def __init__(self, arrs, mode, scatter):
        self.arrs, self.mode, self.scatter = list(arrs), mode, scatter
        self.n = len(self.arrs)
        self.flips = _FLIPS[mode]
        nf = len(self.flips)
        anyspec = pl.BlockSpec(memory_space=pl.ANY)
        self.in_specs = [anyspec] * self.n
        self.out_specs = [anyspec] * self.n
        self.out_shape = [jax.ShapeDtypeStruct((nf + 1,) + (a.shape[1:] if scatter else a.shape), a.dtype)
                          for a in self.arrs]
        self.scratch = [pltpu.SemaphoreType.DMA((self.n * nf,)), pltpu.SemaphoreType.DMA((self.n * nf,)),
                        pltpu.SemaphoreType.DMA((self.n,))]

    def _copies(self, ins, outs, sems, arrivals=True):
        send, recv, loc = sems
        nf = len(self.flips)
        x, y, c = lax.axis_index("x"), lax.axis_index("y"), lax.axis_index("c")
        me = _slot(self.mode, x, y, c)
        peers = [(x ^ fx, y ^ fy, c ^ fc) for (fx, fy, fc) in self.flips]

        def src(a, slot):
            return ins[a].at[slot] if self.scatter else ins[a]

        def copy(a, j, dst_slot):
            return pltpu.make_async_remote_copy(
                src_ref=src(a, _slot(self.mode, *peers[j])), dst_ref=outs[a].at[dst_slot],
                send_sem=send.at[a * nf + j], recv_sem=recv.at[a * nf + j], device_id=peers[j], device_id_type=MESH)

        pairs = [(a, j) for a in range(self.n) for j in range(nf)]
        local = [pltpu.make_async_copy(src(a, me), outs[a].at[me], loc.at[a]) for a in range(self.n)]
        sends = [copy(a, j, me) for a, j in pairs]
        recvs = [copy(a, j, _slot(self.mode, *peers[j])) for a, j in pairs] if arrivals else []
        return local, sends, recvs

    def start(self, ins, outs, sems):
        local, sends, _ = self._copies(ins, outs, sems, arrivals=False)
        for cp in local + sends:
            cp.start()

    def wait(self, ins, outs, sems):
        local, sends, recvs = self._copies(ins, outs, sems)
        for cp in recvs:
            cp.wait_recv()
        for cp in sends:
            cp.wait_send()
        for cp in local:
            cp.wait()


def _carried(carry, refs, n_in, n_out, n_scratch):
    k = carry.n if carry is not None else 0
    ins, refs = refs[:n_in], refs[n_in:]
    cin, refs = refs[:k], refs[k:]
    outs, refs = refs[:n_out], refs[n_out:]
    cout, refs = refs[:k], refs[k:]
    scratch, sems = refs[:n_scratch], refs[n_scratch:]
    return ins, outs, scratch, (cin, cout, sems)


def _exchanges(exs, *, name):
    counts = [ex.n for ex in exs]
    total = sum(counts)

    def body(*refs):
        ins, outs, sems = refs[:total], refs[total:2 * total], refs[2 * total:]
        comms, at = [], 0
        for k, ex in enumerate(exs):
            comms.append((ins[at:at + ex.n], outs[at:at + ex.n], sems[3 * k:3 * k + 3]))
            at += ex.n
        for ex, comm in zip(exs, comms):
            ex.start(*comm)
        for ex, comm in zip(exs, comms):
            ex.wait(*comm)

    res = pl.pallas_call(
        body, name=name, in_specs=[sp for ex in exs for sp in ex.in_specs],
        out_specs=[sp for ex in exs for sp in ex.out_specs], out_shape=[sh for ex in exs for sh in ex.out_shape],
        scratch_shapes=[sc for ex in exs for sc in ex.scratch])(*[a for ex in exs for a in ex.arrs])
    out, at = [], 0
    for n in counts:
        out.append(list(res[at:at + n]))
        at += n
    return out


def _exchange(arrs, mode, scatter, *, name):
    return _exchanges([_Exchange(arrs, mode, scatter)], name=name)[0]


def _softplus(v):
    return jnp.maximum(v, 0.0) + jnp.log1p(jnp.exp(-jnp.abs(v)))


def _pad_lanes(v):
    r, n = v.shape
    return jnp.pad(v, ((0, 0), (0, -n % LANES)))


def _fn_rms(v, g):
    return (_rms(v, g),)


def _fn_post(xv, ov, g):
    return (xv + _rms(ov, g),)


def _fn_act(xbc, dtp, fp, dtb, fb):
    return _silu(xbc), _softplus(dtp + dtb), -_softplus(-(fp + fb))


def _fn_mix(y, zs, o, zf, g):
    yg = y * _silu(zs)
    sq = yg * yg
    lane = lax.broadcasted_iota(jnp.int32, (1, D_SSD), 1)
    width = D_SSD // SSD_GROUPS
    rstd = jnp.zeros_like(yg)
    for gi in range(SSD_GROUPS):
        msk = ((lane >= gi * width) & (lane < (gi + 1) * width)).astype(F32)
        ms = jnp.sum(sq * msk, axis=1, keepdims=True) / width
        rstd = rstd + lax.rsqrt(ms + EPS) * msk
    return (jnp.concatenate([yg * rstd * g, o * _silu(zf)], axis=1),)


def _fn_glu(val, gate):
    return (val * jax.nn.sigmoid(gate),)


def _fn_ln(hc, z, g, b):
    mu = jnp.mean(hc, axis=-1, keepdims=True)
    xc = hc - mu
    yn = xc * lax.rsqrt(jnp.mean(xc * xc, axis=-1, keepdims=True) + EPS) * g + b
    return (_silu(yn) * _silu(z),)


class _NoComm:
    def odd_weights(self):
        return None

    def got_odd_weights(self, got, w):
        pass

    def early_grads(self, g):
        return None

    def got_early_grads(self, got):
        pass

    def early_sums(self):
        return None

    def got_early_sums(self, got):
        pass


def _local_step(x, tgt, w, comm=None):
    comm = comm or _NoComm()
    s = x.shape[0]
    d = D_MODEL
    tm = 256
    tf = 512
    bf = lambda v: v.astype(BF16)
    c1 = lambda arr: _col(arr, 0, arr.shape[1])
    g = {}

    ew = w["e_w_in"]
    w_z, w_xbc = bf(ew[:, 0:2048]), bf(ew[:, 2048:4096])
    w_dt = bf(_pad_lanes(ew[:, 4096:4112]))
    w_qkv = bf(ew[:, 4112:7184])
    w_f = bf(_pad_lanes(ew[:, 7184:7200]))
    dtb, fgb = _pad_lanes(w["e_dt_bias"]), _pad_lanes(w["e_fgate_b"])
    alog, dsk = _pad_lanes(w["e_a_log"]), _pad_lanes(w["e_d_skip"])

    (u0,) = _rowwise_fwd(_fn_rms, [c1(x)], [c1(w["e_norm_pre"])], [(d, BF16)], tm=tf, name="e_pre")
    z = _mm(u0, w_z, name="e_in_z")
    xbc_raw = _mm(u0, w_xbc, out_dtype=BF16, name="e_in_xbc")
    qkv = _mm(u0, w_qkv, out_dtype=BF16, name="e_in_qkv")
    dtp = _mm(u0, w_dt, name="e_in_dt")
    fp = _mm(u0, w_f, name="e_in_f")
    xbc_pre = _conv_fwd(xbc_raw, w["e_conv_w"], w["e_conv_b"], name="e_conv")
    act_rows = [c1(xbc_pre), c1(dtp), c1(fp)]
    act_pars = [c1(dtb), c1(fgb)]
    xbc, dt, lf = _rowwise_fwd(_fn_act, act_rows, act_pars, [(2048, F32), (LANES, F32), (LANES, F32)],
                               tm=tf, name="e_act")
    y, hsave = _ssd2_fwd(dt, xbc, alog, dsk, name="e_ssd")
    csum = _cumsum_lanes(lf[:, :FOX_HEADS].T, reverse=False, name="e_cumsum").reshape(FOX_HEADS, 1, s)
    bounds = _fox_block_bounds(qkv, csum)
    o, mrun, qmax, mmin, *got = _fox3_fwd(qkv, csum, bounds, carry=comm.odd_weights(), name="e_fox")
    bounds = (*bounds, qmax, mmin)
    comm.got_odd_weights(got, w)
    w_eo, w_oi, w_oo = bf(w["e_w_out"]), bf(w["o_w_in"]), bf(w["o_w_out"])
    mix_rows = [c1(y), _col(z, 0, D_SSD), c1(o), _col(z, 1, D_FOX)]
    mix_pars = [c1(w["e_ssd_norm"])]
    (hmix,) = _rowwise_fwd(_fn_mix, mix_rows, mix_pars, [(2048, BF16)], tm=tf, name="e_mix")
    out0 = _mm(hmix, w_eo, name="e_out")
    post_rows = [c1(x), c1(out0)]
    (x1,) = _rowwise_fwd(_fn_post, post_rows, [c1(w["e_norm_post"])], [(d, F32)], tm=tf, name="e_post")

    (u1,) = _rowwise_fwd(_fn_rms, [c1(x1)], [c1(w["o_norm_pre"])], [(d, BF16)], tm=tf, name="o_pre")
    p1 = _mm(u1, w_oi, out_dtype=BF16, name="o_in")
    glu_rows = [_col(p1, 0, D_CONV), _col(p1, 1, D_CONV)]
    (hg,) = _rowwise_fwd(_fn_glu, glu_rows, [], [(D_CONV, BF16)], tm=tf, name="o_glu")
    hc = _conv_fwd(hg, w["o_conv_w"], w["o_conv_b"], name="o_conv")
    ln_rows = [c1(hc), _col(p1, 2, D_CONV)]
    ln_pars = [c1(w["o_ln_g"]), c1(w["o_ln_b"])]
    (h2,) = _rowwise_fwd(_fn_ln, ln_rows, ln_pars, [(D_CONV, BF16)], tm=tf, name="o_ln")
    out1 = _mm(h2, w_oo, name="o_out")

    dx2, dout1, g["o_norm_post"], loss = _final(x1, out1, w["o_norm_post"], tgt, name="loss_head")
    dh2 = _mm(dout1, w_oo, tb=True, name="o_out_dx")
    g["o_w_out"] = _mm(h2, dout1, ta=True, name="o_out_dw")
    (dhc, dz1), (g["o_ln_g"], g["o_ln_b"]) = _rowwise_bwd(_fn_ln, ln_rows, ln_pars, [c1(dh2)], [F32, BF16],
                                                         tm=tm, name="o_ln_bwd")
    dhg = _conv_bwd_x(dhc, w["o_conv_w"], name="o_conv_dx")
    g["o_conv_w"], g["o_conv_b"] = _conv_bwd_w(hg, dhc, CONV_WIDTH, name="o_conv_dw")
    (dval, dgate), _ = _rowwise_bwd(_fn_glu, glu_rows, [], [c1(dhg)], [BF16, BF16], tm=tf, name="o_glu_bwd")
    du1 = _mm(dval, w_oi[:, 0:2048], tb=True, name="o_in_dx0")
    du1 = _mm(dgate, w_oi[:, 2048:4096], tb=True, add=du1, name="o_in_dx1")
    du1 = _mm(dz1, w_oi[:, 4096:6144], tb=True, add=du1, name="o_in_dx2")
    g["o_w_in"] = jnp.concatenate([_mm(u1, dval, ta=True, name="o_in_dw0"), _mm(u1, dgate, ta=True, name="o_in_dw1"),
                                   _mm(u1, dz1, ta=True, name="o_in_dw2")], axis=1)
    (dx1,), (g["o_norm_pre"],) = _rowwise_bwd(_fn_rms, [c1(x1)], [c1(w["o_norm_pre"])], [c1(du1)], [F32],
                                              adds={0: c1(dx2)}, tm=tf, name="o_pre_bwd")

    (dout0,), (g["e_norm_post"],) = _rowwise_bwd(_fn_post, post_rows, [c1(w["e_norm_post"])], [c1(dx1)],
                                                 [None, BF16], tm=tf, name="e_post_bwd")
    dhmix = _mm(dout0, w_eo, tb=True, name="e_out_dx")
    g["e_w_out"] = _mm(hmix, dout0, ta=True, name="e_out_dw")
    (dy, dzs, do, dzf), (g["e_ssd_norm"],) = _rowwise_bwd(_fn_mix, mix_rows, mix_pars, [c1(dhmix)],
                                                        [F32, BF16, F32, BF16], tm=tm, name="e_mix_bwd")
    dox = _fox_dopack(do, o, name="e_fox_dopack")
    dq8, dk, dv, dcs, *got = _fox3_bwd(qkv, csum, bounds, dox, mrun, carry=comm.early_grads(g), name="e_fox_bwd")
    comm.got_early_grads(got)
    dlf = _pad_lanes(_cumsum_lanes(dcs.reshape(FOX_HEADS, s), reverse=True, name="e_cumsum_bwd").T)
    dxs, ddt, dbm, dcm, dalog, ddsk, *got = _ssd2_bwd(dt, xbc, alog, dsk, hsave, dy, carry=comm.early_sums(),
                                                       name="e_ssd_bwd")
    comm.got_early_sums(got)
    dxbc = jnp.concatenate([dxs, dbm, dcm], axis=1)
    (dxbc_pre, ddtp, dfp), (ddtb, dfgb) = _rowwise_bwd(_fn_act, act_rows, act_pars, [c1(dxbc), c1(ddt), c1(dlf)],
                                                      [F32, BF16, BF16], tm=tm, name="e_act_bwd")
    dxbc_raw = bf(_conv_bwd_x(dxbc_pre, w["e_conv_w"], name="e_conv_dx"))
    g["e_conv_w"], g["e_conv_b"] = _conv_bwd_w(xbc_raw, dxbc_pre, SSD_CONV, name="e_conv_dw")
    du0 = _mm(dzs, w_z[:, :D_SSD], tb=True, name="e_in_dx0")
    du0 = _mm(dzf, w_z[:, D_SSD:], tb=True, add=du0, name="e_in_dx1")
    du0 = _mm(dxbc_raw, w_xbc, tb=True, add=du0, name="e_in_dx2")
    eighth = FOX_HEADDIM ** -0.5
    du0 = _mm(dq8, w_qkv[:, :D_FOX] * eighth, tb=True, add=du0, name="e_in_dx3q")
    du0 = _mm(dk, w_qkv[:, D_FOX:2 * D_FOX], tb=True, add=du0, name="e_in_dx3k")
    du0 = _mm(dv, w_qkv[:, 2 * D_FOX:], tb=True, add=du0, name="e_in_dx3v")
    du0 = _mm(ddtp, w_dt, tb=True, add=du0, name="e_in_dx4")
    du0 = _mm(dfp, w_f, tb=True, add=du0, name="e_in_dx5")
    g["e_w_in"] = jnp.concatenate([
        _mm(u0, dzs, ta=True, name="e_in_dw0"), _mm(u0, dzf, ta=True, name="e_in_dw1"),
        _mm(u0, dxbc_raw, ta=True, name="e_in_dw2"), _mm(u0, ddtp, ta=True, name="e_in_dw3")[:, :SSD_HEADS],
        _mm(u0, dq8, ta=True, name="e_in_dw4q") * eighth, _mm(u0, dk, ta=True, name="e_in_dw4k"),
        _mm(u0, dv, ta=True, name="e_in_dw4v"), _mm(u0, dfp, ta=True, name="e_in_dw5")[:, :FOX_HEADS]], axis=1)
    (dx,), (g["e_norm_pre"],) = _rowwise_bwd(_fn_rms, [c1(x)], [c1(w["e_norm_pre"])], [c1(du0)], [F32],
                                             adds={0: c1(dx1)}, tm=tf, name="e_pre_bwd")
    g["e_dt_bias"], g["e_fgate_b"] = ddtb[:, :SSD_HEADS], dfgb[:, :FOX_HEADS]
    g["e_a_log"], g["e_d_skip"] = dalog[:, :SSD_HEADS], ddsk[:, :SSD_HEADS]
    return loss, dx, g


_WEIGHTS = ["e_norm_pre", "e_w_in", "e_conv_w", "e_conv_b", "e_dt_bias", "e_a_log", "e_d_skip", "e_fgate_b",
            "e_ssd_norm", "e_w_out", "e_norm_post", "o_norm_pre", "o_w_in", "o_conv_w", "o_conv_b", "o_ln_g",
            "o_ln_b", "o_w_out", "o_norm_post"]
_BIG = ["e_w_in", "e_w_out", "o_w_in", "o_w_out"]
_ROW_SHARDED = ["e_w_out", "o_w_out"]
_SMALL_SHARDED = ["e_conv_w", "o_norm_pre", "o_conv_w", "o_conv_b", "o_ln_g", "o_ln_b", "o_norm_post"]
_REPLICATED = ["e_norm_pre", "e_conv_b", "e_dt_bias", "e_a_log", "e_d_skip", "e_fgate_b", "e_ssd_norm", "e_norm_post"]
_SMALL = [n for n in _WEIGHTS if n not in _BIG]
_EVEN_SHARDED = ["e_w_in", "e_conv_w"]
_ODD_SHARDED = ["e_w_out", "o_w_in", "o_w_out", "o_norm_pre", "o_conv_w", "o_conv_b", "o_ln_g", "o_ln_b",
                "o_norm_post"]
_EARLY_GRADS = ["o_w_in", "o_w_out", "e_w_out"]
N_CHIPS = 4


def _join(gathered, rows):
    k, r, c = gathered.shape
    return gathered.reshape(k * r, c) if rows else gathered.transpose(1, 0, 2).reshape(r, k * c)


def _split(full, rows):
    r, c = full.shape
    return full.reshape(N_CHIPS, r // N_CHIPS, c) if rows else full.reshape(r, N_CHIPS, c // N_CHIPS).transpose(1, 0, 2)


def kernel(x, e_norm_pre, e_w_in, e_conv_w, e_conv_b, e_dt_bias, e_a_log, e_d_skip, e_fgate_b, e_ssd_norm, e_w_out, e_norm_post, o_norm_pre, o_w_in, o_conv_w, o_conv_b, o_ln_g, o_ln_b, o_w_out, o_norm_post, loss_target, m_e_norm_pre, m_e_w_in, m_e_conv_w, m_e_conv_b, m_e_dt_bias, m_e_a_log, m_e_d_skip, m_e_fgate_b, m_e_ssd_norm, m_e_w_out, m_e_norm_post, m_o_norm_pre, m_o_w_in, m_o_conv_w, m_o_conv_b, m_o_ln_g, m_o_ln_b, m_o_w_out, m_o_norm_post, v_e_norm_pre, v_e_w_in, v_e_conv_w, v_e_conv_b, v_e_dt_bias, v_e_a_log, v_e_d_skip, v_e_fgate_b, v_e_ssd_norm, v_e_w_out, v_e_norm_post, v_o_norm_pre, v_o_w_in, v_o_conv_w, v_o_conv_b, v_o_ln_g, v_o_ln_b, v_o_w_out, v_o_norm_post):
    wvals = (e_norm_pre, e_w_in, e_conv_w, e_conv_b, e_dt_bias, e_a_log, e_d_skip, e_fgate_b, e_ssd_norm, e_w_out,
             e_norm_post, o_norm_pre, o_w_in, o_conv_w, o_conv_b, o_ln_g, o_ln_b, o_w_out, o_norm_post)
    mvals = (m_e_norm_pre, m_e_w_in, m_e_conv_w, m_e_conv_b, m_e_dt_bias, m_e_a_log, m_e_d_skip, m_e_fgate_b,
             m_e_ssd_norm, m_e_w_out, m_e_norm_post, m_o_norm_pre, m_o_w_in, m_o_conv_w, m_o_conv_b, m_o_ln_g,
             m_o_ln_b, m_o_w_out, m_o_norm_post)
    vvals = (v_e_norm_pre, v_e_w_in, v_e_conv_w, v_e_conv_b, v_e_dt_bias, v_e_a_log, v_e_d_skip, v_e_fgate_b,
             v_e_ssd_norm, v_e_w_out, v_e_norm_post, v_o_norm_pre, v_o_w_in, v_o_conv_w, v_o_conv_b, v_o_ln_g,
             v_o_ln_b, v_o_w_out, v_o_norm_post)

    def mat(v):
        return v.reshape(v.shape[-2:]) if v.ndim == 3 else v

    w = {n: mat(v) for n, v in zip(_WEIGHTS, wvals)}
    m = {n: mat(v) for n, v in zip(_WEIGHTS, mvals)}
    v2 = {n: mat(v) for n, v in zip(_WEIGHTS, vvals)}
    me_xy = 2 * lax.axis_index("x") + lax.axis_index("y")

    def shard(n):
        return w[n].astype(BF16) if n in _BIG else w[n]

    gathered = _exchange([shard(n) for n in _EVEN_SHARDED], "xy", False, name="gather_weights")
    full = {n: w[n] for n in _REPLICATED}
    for n, gth in zip(_EVEN_SHARDED, gathered):
        full[n] = _join(gth, n in _ROW_SHARDED)
    gparts = {}

    class _StepComm(_NoComm):
        def odd_weights(self):
            return _Exchange([shard(n) for n in _ODD_SHARDED], "xy", False)

        def got_odd_weights(self, got, wdict):
            for n, gth in zip(_ODD_SHARDED, got):
                wdict[n] = _join(gth, n in _ROW_SHARDED)

        def early_grads(self, g):
            return _Exchange([_split(g[n], n in _ROW_SHARDED).astype(BF16) for n in _EARLY_GRADS], "xy", True)

        def got_early_grads(self, got):
            self.sums = [_sum_slots(p, out_dtype=BF16, name="sum_" + n) for n, p in zip(_EARLY_GRADS, got)]

        def early_sums(self):
            return _Exchange(self.sums, "c", False)

        def got_early_sums(self, got):
            gparts.update(zip(_EARLY_GRADS, got))

    loss, dx, g = _local_step(x[0], loss_target[0], full, _StepComm())
    loss = lax.psum(loss[0, 0], ("x", "y", "c"))

    flat = jnp.concatenate([_pad_lanes(g[n].reshape(1, -1)) for n in _SMALL], axis=1).reshape(-1, LANES)
    (scattered,), (all8,) = _exchanges([_Exchange([_split(g["e_w_in"], False).astype(BF16)], "xy", True),
                                        _Exchange([flat], "xyc", False)], name="scatter_grads")
    (gparts["e_w_in"],) = _exchange([_sum_slots(scattered, out_dtype=BF16, name="sum_e_w_in")], "c", False,
                                    name="pair_grads")
    total = _sum_slots(all8, name="sum_small").reshape(1, -1)
    at = 0
    for n in _SMALL:
        size = g[n].size
        gn = total[:, at:at + size].reshape(g[n].shape)
        at += size + (-size % LANES)
        if n in _SMALL_SHARDED:
            cols = gn.shape[1] // N_CHIPS
            gn = lax.dynamic_slice(gn, (0, me_xy * cols), (gn.shape[0], cols))
        gparts[n] = gn[None]

    grads, deltas, new_m, new_v = [], [], [], []
    for n, orig in zip(_WEIGHTS, wvals):
        gn, dn, mn, vn = _adamw(w[n], gparts[n], m[n], v2[n], name="adamw_" + n)
        for lst, val in zip((grads, deltas, new_m, new_v), (gn, dn, mn, vn)):
            lst.append(val.reshape(orig.shape))
    return (loss, dx[None], *grads, *deltas, *new_m, *new_v)
```

```python
import functools

import jax
import jax.numpy as jnp
from jax import lax
from jax.experimental import pallas as pl
from jax.experimental.pallas import tpu as pltpu

F32 = jnp.float32
BF16 = jnp.bfloat16
MESH = pl.DeviceIdType.MESH

D_MODEL = 1024
D_SSD = 1024
SSD_HEADS = 16
SSD_HEADDIM = 64
SSD_GROUPS = 4
SSD_HPG = 4
D_STATE = 128
SSD_CONV = 4
CHUNK = 128
D_FOX = 1024
FOX_HEADS = 16
FOX_HEADDIM = 64
D_CONV = 2048
CONV_WIDTH = 31
EPS = 1e-6
LANES = 128
VMEM_LIMIT = 56 * 1024 * 1024

ADAM_LR = 0.001
ADAM_B1 = 0.9
ADAM_B2 = 0.999
ADAM_EPS = 1e-08
ADAM_WD = 0.01
ADAM_STEP = 10


def _cparams(sem=None):
    return pltpu.CompilerParams(dimension_semantics=sem, vmem_limit_bytes=VMEM_LIMIT)


def _mm(a, b, *, ta=False, tb=False, add=None, out_dtype=F32, tm=1024, tn=None, tk=2048, name):
    m = a.shape[1] if ta else a.shape[0]
    k = a.shape[0] if ta else a.shape[1]
    n = b.shape[0] if tb else b.shape[1]
    if tn is None:
        tn = 1024
    tm, tn = min(tm, m), min(tn, n)
    tk = max(t for t in range(LANES, min(tk, k) + 1, LANES) if k % t == 0)
    assert m % tm == 0 and n % tn == 0 and k % tk == 0, (m, n, k, tm, tn, tk)
    nk = k // tk
    dims = (((0 if ta else 1,), (1 if tb else 0,)), ((), ()))

    def body(*refs):
        if add is None:
            a_ref, b_ref, o_ref = refs[:3]
            c_ref = None
        else:
            a_ref, b_ref, c_ref, o_ref = refs[:4]
        kk = pl.program_id(2)
        prod = lax.dot_general(a_ref[...].astype(BF16), b_ref[...].astype(BF16), dims, preferred_element_type=F32)
        if nk == 1:
            o_ref[...] = (prod if c_ref is None else prod + c_ref[...].astype(F32)).astype(o_ref.dtype)
            return
        acc_ref = refs[-1]

        @pl.when(kk == 0)
        def _():
            acc_ref[...] = prod if c_ref is None else prod + c_ref[...].astype(F32)

        @pl.when((kk > 0) & (kk < nk - 1))
        def _():
            acc_ref[...] += prod

        @pl.when(kk == nk - 1)
        def _():
            o_ref[...] = (acc_ref[...] + prod).astype(o_ref.dtype)

    a_spec = (pl.BlockSpec((tk, tm), lambda j, i, kk: (kk, i)) if ta
              else pl.BlockSpec((tm, tk), lambda j, i, kk: (i, kk)))
    b_spec = (pl.BlockSpec((tn, tk), lambda j, i, kk: (j, kk)) if tb
              else pl.BlockSpec((tk, tn), lambda j, i, kk: (kk, j)))
    o_spec = pl.BlockSpec((tm, tn), lambda j, i, kk: (i, j))
    in_specs, args = [a_spec, b_spec], [a, b]
    if add is not None:
        in_specs.append(o_spec)
        args.append(add)
    return pl.pallas_call(
        body, name=name, grid=(n // tn, m // tm, nk),
        in_specs=in_specs, out_specs=o_spec,
        out_shape=jax.ShapeDtypeStruct((m, n), out_dtype),
        scratch_shapes=[pltpu.VMEM((tm, tn), F32)] if nk > 1 else [],
        compiler_params=_cparams(("parallel", "parallel", "arbitrary")),
    )(*args)


def _col(arr, cb, width):
    return (arr, cb, width)


def _row_specs(ops, tm):
    return [pl.BlockSpec((tm, w), lambda i, cb=cb: (i, cb)) for (_, cb, w) in ops]


def _par_specs(ops):
    return [pl.BlockSpec((a.shape[0], w), lambda i, cb=cb: (0, cb)) for (a, cb, w) in ops]


def _rowwise_fwd(fn, rows, params, outs, *, tm, name):
    s = rows[0][0].shape[0]
    tm = min(tm, s)
    nr, npar = len(rows), len(params)

    def body(*refs):
        rv = [r[...].astype(F32) for r in refs[:nr]]
        pv = [p[...].astype(F32) for p in refs[nr:nr + npar]]
        res = fn(*rv, *pv)
        for o_ref, val in zip(refs[nr + npar:], res):
            o_ref[...] = val.astype(o_ref.dtype)

    return pl.pallas_call(
        body, name=name, grid=(s // tm,),
        in_specs=_row_specs(rows, tm) + _par_specs(params),
        out_specs=[pl.BlockSpec((tm, w), lambda i: (i, 0)) for (w, _) in outs],
        out_shape=[jax.ShapeDtypeStruct((s, w), dt) for (w, dt) in outs],
        compiler_params=_cparams(("parallel",)),
    )(*[r[0] for r in rows], *[p[0] for p in params])


def _rowwise_bwd(fn, rows, params, couts, row_grads, *, adds=None, tm, name):
    adds = adds or {}
    s = rows[0][0].shape[0]
    tm = min(tm, s)
    nr, npar, nc = len(rows), len(params), len(couts)
    add_keys = sorted(adds)
    want = [i for i, dt in enumerate(row_grads) if dt is not None]

    def body(*refs):
        i = pl.program_id(0)
        rv = [r[...].astype(F32) for r in refs[:nr]]
        pv = [p[...].astype(F32) for p in refs[nr:nr + npar]]
        cv = [c[...].astype(F32) for c in refs[nr + npar:nr + npar + nc]]
        av = {k: r[...].astype(F32) for k, r in zip(add_keys, refs[nr + npar + nc:nr + npar + nc + len(add_keys)])}
        orefs = refs[nr + npar + nc + len(add_keys):]
        _, vjp = jax.vjp(lambda rr, pp: tuple(fn(*rr, *pp)), rv, pv)
        drows, dpars = vjp(tuple(cv))
        for o_ref, ri in zip(orefs[:len(want)], want):
            g = drows[ri]
            if ri in av:
                g = g + av[ri]
            o_ref[...] = g.astype(o_ref.dtype)

        @pl.when(i == 0)
        def _():
            for o_ref in orefs[len(want):]:
                o_ref[...] = jnp.zeros_like(o_ref)

        for o_ref, g in zip(orefs[len(want):], dpars):
            o_ref[...] += g

    add_ops = [adds[k] for k in add_keys]
    out_specs = ([pl.BlockSpec((tm, rows[ri][2]), lambda i: (i, 0)) for ri in want]
                 + [pl.BlockSpec((p[0].shape[0], p[2]), lambda i: (0, 0)) for p in params])
    out_shape = ([jax.ShapeDtypeStruct((s, rows[ri][2]), row_grads[ri]) for ri in want]
                 + [jax.ShapeDtypeStruct((p[0].shape[0], p[2]), F32) for p in params])
    res = pl.pallas_call(
        body, name=name, grid=(s // tm,),
        in_specs=_row_specs(rows, tm) + _par_specs(params) + _row_specs(couts, tm) + _row_specs(add_ops, tm),
        out_specs=out_specs, out_shape=out_shape,
        compiler_params=_cparams(("arbitrary",)),
    )(*[r[0] for r in rows], *[p[0] for p in params], *[c[0] for c in couts], *[a[0] for a in add_ops])
    return res[:len(want)], res[len(want):]


def _silu(v):
    return v * jax.nn.sigmoid(v)


def _rms(v, g):
    return v * lax.rsqrt(jnp.mean(v * v, axis=-1, keepdims=True) + EPS) * g


SUBLANES = 8
CONV_ROWS = 256


def _halo(shifts):
    up = lambda v: -(-v // SUBLANES) * SUBLANES
    return up(max(0, -min(shifts))), up(max(0, max(shifts)))


def _fill_halo(xp_sc, x_ref, front, back):
    s = x_ref.shape[0]
    if front:
        xp_sc[0:front, :] = jnp.zeros((front, LANES), F32)
    if back:
        xp_sc[front + s:front + s + back, :] = jnp.zeros((back, LANES), F32)
    xp_sc[front:front + s, :] = x_ref[...].astype(F32)


def _shift_conv(x, w, b, shifts, *, name):
    s, c = x.shape
    tr = min(CONV_ROWS, s)
    nk = len(shifts)
    front, back = _halo(shifts)

    def body(*refs):
        if b is None:
            x_ref, w_ref, o_ref, xp_sc = refs
        else:
            x_ref, w_ref, b_ref, o_ref, xp_sc = refs
        _fill_halo(xp_sc, x_ref, front, back)

        def chunk(r, carry):
            base = pl.multiple_of(r * tr, tr)
            acc = jnp.zeros((tr, LANES), F32) if b is None else jnp.broadcast_to(b_ref[...], (tr, LANES))
            for kk in range(nk):
                acc = acc + xp_sc[pl.ds(base + front + shifts[kk], tr), :] * w_ref[kk:kk + 1, :]
            o_ref[pl.ds(base, tr), :] = acc
            return carry

        lax.fori_loop(0, s // tr, chunk, 0)

    strip = pl.BlockSpec((s, LANES), lambda cb: (0, cb))
    in_specs = [strip, pl.BlockSpec((nk, LANES), lambda cb: (0, cb))]
    args = [x, w]
    if b is not None:
        in_specs.append(pl.BlockSpec((1, LANES), lambda cb: (0, cb)))
        args.append(b)
    return pl.pallas_call(
        body, name=name, grid=(c // LANES,), in_specs=in_specs, out_specs=strip,
        out_shape=jax.ShapeDtypeStruct((s, c), F32),
        scratch_shapes=[pltpu.VMEM((front + s + back, LANES), F32)],
        compiler_params=_cparams(("parallel",)),
    )(*args)


def _conv_fwd(x, w, b, *, name):
    k = w.shape[0]
    return _shift_conv(x, w, b, [kk - (k - 1) for kk in range(k)], name=name)


def _conv_bwd_x(dy, w, *, name):
    k = w.shape[0]
    return _shift_conv(dy, w, None, [(k - 1) - kk for kk in range(k)], name=name)


def _conv_bwd_w(x, dy, k, *, name):
    s, c = x.shape
    tr = min(CONV_ROWS, s)
    shifts = [kk - (k - 1) for kk in range(k)]
    front, back = _halo(shifts)

    def fold(v):
        return jnp.sum(v.reshape(tr // SUBLANES, SUBLANES, LANES), axis=0)

    def body(x_ref, dy_ref, dw_ref, db_ref, xp_sc, dw_sc, db_sc):
        _fill_halo(xp_sc, x_ref, front, back)
        dw_sc[...] = jnp.zeros_like(dw_sc)
        db_sc[...] = jnp.zeros_like(db_sc)

        def chunk(r, carry):
            base = pl.multiple_of(r * tr, tr)
            dyv = dy_ref[pl.ds(base, tr), :]
            db_sc[...] += fold(dyv)
            for kk in range(k):
                dw_sc[kk] += fold(xp_sc[pl.ds(base + front + shifts[kk], tr), :] * dyv)
            return carry

        lax.fori_loop(0, s // tr, chunk, 0)
        db_ref[...] = jnp.sum(db_sc[...], axis=0, keepdims=True)
        for kk in range(k):
            dw_ref[kk:kk + 1, :] = jnp.sum(dw_sc[kk], axis=0, keepdims=True)

    strip = pl.BlockSpec((s, LANES), lambda cb: (0, cb))
    return pl.pallas_call(
        body, name=name, grid=(c // LANES,), in_specs=[strip, strip],
        out_specs=[pl.BlockSpec((k, LANES), lambda cb: (0, cb)), pl.BlockSpec((1, LANES), lambda cb: (0, cb))],
        out_shape=[jax.ShapeDtypeStruct((k, c), F32), jax.ShapeDtypeStruct((1, c), F32)],
        scratch_shapes=[pltpu.VMEM((front + s + back, LANES), F32), pltpu.VMEM((k, SUBLANES, LANES), F32),
                        pltpu.VMEM((SUBLANES, LANES), F32)],
        compiler_params=_cparams(("parallel",)),
    )(x, dy)


_DIMS = {"nn": ((1,), (0,)), "nt": ((1,), (1,)), "tn": ((0,), (0,))}


def _bd(a, b, mode):
    return lax.dot_general(a.astype(BF16), b.astype(BF16), (_DIMS[mode], ((), ())), preferred_element_type=F32)


@functools.partial(jax.custom_vjp, nondiff_argnums=(2,))
def _bdot(a, b, mode):
    return _bd(a, b, mode)


def _bdot_fwd(a, b, mode):
    return _bd(a, b, mode), (a, b)


def _bdot_bwd(mode, res, g):
    a, b = res
    if mode == "nn":
        return _bd(g, b, "nt"), _bd(a, g, "tn")
    if mode == "nt":
        return _bd(g, b, "nn"), _bd(g, a, "tn")
    return _bd(b, g, "nt"), _bd(a, g, "nn")


_bdot.defvjp(_bdot_fwd, _bdot_bwd)


def _split3(v):
    hi = v.astype(BF16)
    r1 = v - hi.astype(F32)
    mid = r1.astype(BF16)
    lo = (r1 - mid.astype(F32)).astype(BF16)
    return hi, mid, lo


def _mask_dot(mask01, v, mode, terms=3):
    out = None
    for part in _split3(v)[:terms]:
        if mode == "vn":
            t = lax.dot_general(part, mask01, (_DIMS["nn"], ((), ())), preferred_element_type=F32)
        else:
            t = lax.dot_general(mask01, part, (_DIMS[mode], ((), ())), preferred_element_type=F32)
        out = t if out is None else out + t
    return out


def _lower_tri(n):
    r = lax.broadcasted_iota(jnp.int32, (n, n), 0)
    c = lax.broadcasted_iota(jnp.int32, (n, n), 1)
    return (r >= c).astype(BF16)


@jax.custom_vjp
def _tri_dot(w):
    return _mask_dot(_lower_tri(w.shape[0]), w, "nn", terms=2)


def _tri_dot_fwd(w):
    return _tri_dot(w), None


def _tri_dot_bwd(_, g):
    return (_mask_dot(_lower_tri(g.shape[0]), g, "tn", terms=2),)


_tri_dot.defvjp(_tri_dot_fwd, _tri_dot_bwd)


def _cumsum_lanes(x, *, reverse, name):
    h, s = x.shape
    n = s // LANES

    def body(x_ref, o_ref):
        r = lax.broadcasted_iota(jnp.int32, (LANES, LANES), 0)
        c = lax.broadcasted_iota(jnp.int32, (LANES, LANES), 1)
        m01 = ((r >= c) if reverse else (r <= c)).astype(BF16)

        def step(t, carry):
            ci = (n - 1 - t) if reverse else t
            at = pl.ds(pl.multiple_of(ci * LANES, LANES), LANES)
            blk = x_ref[:, at]
            o_ref[:, at] = _mask_dot(m01, blk, "vn") + carry
            return carry + jnp.sum(blk, axis=1, keepdims=True)

        lax.fori_loop(0, n, step, jnp.zeros((h, 1), F32))

    return pl.pallas_call(body, name=name, out_shape=jax.ShapeDtypeStruct((h, s), F32),
                          compiler_params=_cparams())(x)


SSD_PAIRS = SSD_HPG // 2


def _ssd2_chunk(xs, dt, bm, cm, hin, a, dsk, head0):
    n = CHUNK
    row = lax.broadcasted_iota(jnp.int32, (n, n), 0)
    col = lax.broadcasted_iota(jnp.int32, (n, n), 1)
    lower = row >= col
    ustrict = (row > col).astype(F32)
    lane = lax.broadcasted_iota(jnp.int32, (1, LANES), 1)
    sub = lax.broadcasted_iota(jnp.int32, (n, 1), 0)
    e_first = (sub == 0).astype(F32)
    e_last = (sub == n - 1).astype(F32)
    lane0 = (lane == 0).astype(F32)
    half_l = [(lane < LANES // 2).astype(F32), (lane >= LANES // 2).astype(F32)]
    half_s = [(sub < LANES // 2).astype(F32), (sub >= LANES // 2).astype(F32)]
    cb = _bdot(cm, bm, "nt")
    da = dt * (-jnp.exp(a))
    ys, houts = [], []
    for pr in range(len(xs)):
        y = jnp.zeros((n, LANES), F32)
        xdte = jnp.zeros((n, LANES), F32)
        lane_gain = jnp.zeros((n, LANES), F32)
        row_gain = jnp.zeros((LANES, 1), F32)
        for hf in range(2):
            oh = (lane == head0 + 2 * pr + hf).astype(F32)
            dt_col = jnp.sum(dt * oh, axis=1, keepdims=True)
            da_col = jnp.sum(da * oh, axis=1, keepdims=True)
            dsk_h = jnp.sum(dsk * oh, axis=1, keepdims=True)
            seg = _tri_dot(da_col * ustrict)
            decay = jnp.where(lower, jnp.exp(seg), 0.0)
            cs_col = jnp.sum(seg * lane0, axis=1, keepdims=True) + jnp.sum(da_col * e_first, axis=0, keepdims=True)
            total = jnp.sum(cs_col * e_last, axis=0, keepdims=True)
            xh = xs[pr] * half_l[hf]
            xd = xh * dt_col
            y = y + _bdot(cb * decay, xd, "nn") + xh * dsk_h
            xdte = xdte + xd * jnp.exp(total - cs_col)
            lane_gain = lane_gain + jnp.exp(cs_col) * half_l[hf]
            row_gain = row_gain + jnp.exp(total) * half_s[hf]
        houts.append(hin[pr] * row_gain + _bdot(xdte, bm, "tn"))
        ys.append(y + _bdot(cm, hin[pr], "nt") * lane_gain)
    return ys, houts


SSD_STEP = 4


def _ssd2_steps(s):
    per = min(SSD_STEP, s // CHUNK)
    return per, s // (CHUNK * per)


def _ssd2_multi(xs, dt, bm, cm, hin, a, dsk, head0):
    ys = []
    for k in range(len(dt)):
        y, hin = _ssd2_chunk(xs[k], dt[k], bm[k], cm[k], hin, a, dsk, head0)
        ys.append(y)
    return ys, hin


def _ssd2_specs(per, nc, rev):
    cc = (lambda c: nc - 1 - c) if rev else (lambda c: c)
    rows = CHUNK * per
    act = pl.BlockSpec((rows, SSD_PAIRS * LANES), lambda c, g: (cc(c), g))
    row = pl.BlockSpec((rows, LANES), lambda c, g: (cc(c), 0))
    bmat = pl.BlockSpec((rows, LANES), lambda c, g: (cc(c), D_SSD // LANES + g))
    cmat = pl.BlockSpec((rows, LANES), lambda c, g: (cc(c), D_SSD // LANES + SSD_GROUPS + g))
    par = pl.BlockSpec((1, LANES), lambda c, g: (0, 0))
    hs = pl.BlockSpec((1, SSD_PAIRS, LANES, D_STATE), lambda c, g: (cc(c), g, 0, 0))
    return act, row, bmat, cmat, par, hs


def _chunk_rows(ref, k):
    return ref[k * CHUNK:(k + 1) * CHUNK, :]


def _pair_cols(ref, k):
    return [ref[k * CHUNK:(k + 1) * CHUNK, pr * LANES:(pr + 1) * LANES] for pr in range(SSD_PAIRS)]


def _ssd2_fwd(dt, xbc, a, dsk, *, name):
    s = xbc.shape[0]
    per, nc = _ssd2_steps(s)
    act, row, bmat, cmat, par, hs = _ssd2_specs(per, nc, False)

    def body(xs_ref, dt_ref, bm_ref, cm_ref, a_ref, dsk_ref, y_ref, hs_ref, h_sc):
        c, g = pl.program_id(0), pl.program_id(1)

        @pl.when(c == 0)
        def _():
            h_sc[pl.ds(g * SSD_PAIRS, SSD_PAIRS)] = jnp.zeros((SSD_PAIRS, LANES, D_STATE), F32)

        hin = [h_sc[g * SSD_PAIRS + pr] for pr in range(SSD_PAIRS)]
        ks = range(per)
        ys, houts = _ssd2_multi([_pair_cols(xs_ref, k) for k in ks], [_chunk_rows(dt_ref, k) for k in ks],
                                [_chunk_rows(bm_ref, k) for k in ks], [_chunk_rows(cm_ref, k) for k in ks],
                                hin, a_ref[...], dsk_ref[...], g * SSD_HPG)
        for pr in range(SSD_PAIRS):
            for k in ks:
                y_ref[k * CHUNK:(k + 1) * CHUNK, pr * LANES:(pr + 1) * LANES] = ys[k][pr]
            hs_ref[0, pr] = hin[pr]
            h_sc[g * SSD_PAIRS + pr] = houts[pr]

    return pl.pallas_call(
        body, name=name, grid=(nc, SSD_GROUPS),
        in_specs=[act, row, bmat, cmat, par, par], out_specs=[act, hs],
        out_shape=[jax.ShapeDtypeStruct((s, D_SSD), F32),
                   jax.ShapeDtypeStruct((nc, SSD_HEADS // 2, LANES, D_STATE), F32)],
        scratch_shapes=[pltpu.VMEM((SSD_HEADS // 2, LANES, D_STATE), F32)],
        compiler_params=_cparams(("arbitrary", "arbitrary")),
    )(xbc, dt, xbc, xbc, a, dsk)


def _ssd2_bwd(dt, xbc, a, dsk, hsave, dy, *, carry=None, name):
    s = xbc.shape[0]
    per, nc = _ssd2_steps(s)
    rows = CHUNK * per
    rev = lambda c: nc - 1 - c
    act = pl.BlockSpec((rows, LANES), lambda c, g, p: (rev(c), SSD_PAIRS * g + p))
    row = pl.BlockSpec((rows, LANES), lambda c, g, p: (rev(c), 0))
    bmat = pl.BlockSpec((rows, LANES), lambda c, g, p: (rev(c), D_SSD // LANES + g))
    cmat = pl.BlockSpec((rows, LANES), lambda c, g, p: (rev(c), D_SSD // LANES + SSD_GROUPS + g))
    par = pl.BlockSpec((1, LANES), lambda c, g, p: (0, 0))
    hs = pl.BlockSpec((1, 1, LANES, D_STATE), lambda c, g, p: (rev(c), SSD_PAIRS * g + p, 0, 0))
    gmat = pl.BlockSpec((rows, LANES), lambda c, g, p: (rev(c), g))

    def body(*refs):
        ins, outs, (dh_sc,), comm = _carried(carry, refs, 8, 6, 1)
        xs_ref, dt_ref, bm_ref, cm_ref, a_ref, dsk_ref, hs_ref, dy_ref = ins
        dxs_ref, ddt_ref, dbm_ref, dcm_ref, da_ref, ddsk_ref = outs
        c, g, p = pl.program_id(0), pl.program_id(1), pl.program_id(2)
        first = (g == 0) & (p == 0)
        mine = g * SSD_PAIRS + p
        if carry is not None:
            @pl.when((c == 0) & first)
            def _():
                carry.start(*comm)

        @pl.when(c == 0)
        def _():
            dh_sc[mine] = jnp.zeros((LANES, D_STATE), F32)

        @pl.when((c == 0) & first)
        def _():
            da_ref[...] = jnp.zeros_like(da_ref)
            ddsk_ref[...] = jnp.zeros_like(ddsk_ref)

        @pl.when(first)
        def _():
            ddt_ref[...] = jnp.zeros_like(ddt_ref)

        @pl.when(p == 0)
        def _():
            dbm_ref[...] = jnp.zeros_like(dbm_ref)
            dcm_ref[...] = jnp.zeros_like(dcm_ref)

        head0 = g * SSD_HPG + 2 * p
        ks = range(per)
        prim = ([[_chunk_rows(xs_ref, k)] for k in ks], [_chunk_rows(dt_ref, k) for k in ks],
                [_chunk_rows(bm_ref, k) for k in ks], [_chunk_rows(cm_ref, k) for k in ks],
                [hs_ref[0, 0]], a_ref[...], dsk_ref[...])
        _, vjp = jax.vjp(lambda *q: _ssd2_multi(*q, head0), *prim)
        cot = ([[_chunk_rows(dy_ref, k)] for k in ks], [dh_sc[mine]])
        dxs, ddt, dbm, dcm, dhin, da, ddsk = vjp(cot)
        dh_sc[mine] = dhin[0]
        for k in ks:
            at = slice(k * CHUNK, (k + 1) * CHUNK)
            dxs_ref[at, :] = dxs[k][0]
            ddt_ref[at, :] += ddt[k]
            dbm_ref[at, :] += dbm[k]
            dcm_ref[at, :] += dcm[k]
        da_ref[...] += da
        ddsk_ref[...] += ddsk
        if carry is not None:
            @pl.when((c == nc - 1) & (g == SSD_GROUPS - 1) & (p == SSD_PAIRS - 1))
            def _():
                carry.wait(*comm)

    extra = carry if carry is not None else _Exchange([], "c", False)
    return pl.pallas_call(
        body, name=name, grid=(nc, SSD_GROUPS, SSD_PAIRS),
        in_specs=[act, row, bmat, cmat, par, par, hs, act] + extra.in_specs,
        out_specs=[act, row, gmat, gmat, par, par] + extra.out_specs,
        out_shape=[jax.ShapeDtypeStruct((s, D_SSD), F32), jax.ShapeDtypeStruct((s, LANES), F32),
                   jax.ShapeDtypeStruct((s, SSD_GROUPS * D_STATE), F32),
                   jax.ShapeDtypeStruct((s, SSD_GROUPS * D_STATE), F32),
                   jax.ShapeDtypeStruct((1, LANES), F32), jax.ShapeDtypeStruct((1, LANES), F32)] + extra.out_shape,
        scratch_shapes=[pltpu.VMEM((SSD_HEADS // 2, LANES, D_STATE), F32)]
        + (carry.scratch if carry is not None else []),
        compiler_params=_cparams(("arbitrary", "arbitrary", "arbitrary")),
    )(xbc, dt, xbc, xbc, a, dsk, hsave, dy, *extra.arrs)


FOX_BLOCK = 512
NEG = -1e30
PAIRS = FOX_HEADS // 2
HALF = LANES // 2


def _causal(t):
    return lax.broadcasted_iota(jnp.int32, (t, t), 0) >= lax.broadcasted_iota(jnp.int32, (t, t), 1)


def _first_half():
    return lax.broadcasted_iota(jnp.int32, (1, LANES), 1) < HALF


def _pair_bias(c_ref, hh, qblock, kblock, t):
    lane = lax.broadcasted_iota(jnp.int32, (1, LANES), 1)
    cq = c_ref[hh, :, pl.ds(pl.multiple_of(qblock * t, LANES), LANES)]
    cref = jnp.sum(jnp.where(lane == 0, cq, 0.0), axis=1, keepdims=True)
    return cref - c_ref[hh, :, pl.ds(pl.multiple_of(kblock * t, LANES), t)]


def _fox_dopack(do, o, *, tm=256, name):
    s, d = do.shape
    tm = min(tm, s)

    def body(do_ref, o_ref, out_ref):
        dov = do_ref[...].astype(BF16)
        prod = dov.astype(F32) * o_ref[...]
        r = lax.broadcasted_iota(jnp.int32, (d, LANES), 0)
        c = lax.broadcasted_iota(jnp.int32, (d, LANES), 1)
        heads = ((r >= c * FOX_HEADDIM) & (r < (c + 1) * FOX_HEADDIM)).astype(BF16)
        negd = -_mask_dot(heads, prod, "vn")
        hr = lax.broadcasted_iota(jnp.int32, (LANES, 2 * d), 0)
        col = lax.broadcasted_iota(jnp.int32, (LANES, 2 * d), 1)
        base = (hr >> 1) * (2 * LANES) + jnp.where((hr & 1) == 0, HALF, LANES)
        terms = None
        for kk, part in enumerate(_split3(negd)):
            place = ((col == base + kk) & (hr < FOX_HEADS)).astype(BF16)
            tk = lax.dot_general(part, place, (_DIMS["nn"], ((), ())), preferred_element_type=F32)
            terms = tk if terms is None else terms + tk
        first = _first_half()
        zero = jnp.zeros((tm, LANES), BF16)
        pieces = []
        for hp in range(PAIRS):
            blk = dov[:, hp * LANES:(hp + 1) * LANES]
            pieces += [jnp.where(first, blk, zero), jnp.where(first, zero, blk)]
        out_ref[...] = (jnp.concatenate(pieces, axis=1).astype(F32) + terms).astype(BF16)

    row = pl.BlockSpec((tm, d), lambda i: (i, 0))
    return pl.pallas_call(body, name=name, grid=(s // tm,), in_specs=[row, row],
                          out_specs=pl.BlockSpec((tm, 2 * d), lambda i: (i, 0)),
                          out_shape=jax.ShapeDtypeStruct((s, 2 * d), BF16),
                          compiler_params=_cparams(("parallel",)))(do, o)


FOX_DEAD = 110.0
BOUND_SLACK = 1.001


def _fox_block_bounds(qkv, c):
    s = qkv.shape[0]
    t = min(FOX_BLOCK, s)
    k = qkv[:, D_FOX:2 * D_FOX].astype(F32).reshape(s // t, t, FOX_HEADS, FOX_HEADDIM)
    kn = jnp.sqrt(jnp.max(jnp.sum(k * k, axis=-1), axis=1)).T
    return kn, c[:, 0, ::t], c[:, 0, t - 1::t]


def _fox_live(qmax, mmin, kn_ref, cs_ref, ce_ref, h, qblock, kblock):
    bound = qmax * (kn_ref[h, kblock] * BOUND_SLACK) + (cs_ref[h, qblock] - ce_ref[h, kblock])
    return bound - mmin > -FOX_DEAD


def _run_live(live, work):
    @pl.when(live[0] & live[1])
    def _():
        work(0)
        work(1)

    @pl.when(live[0] & jnp.logical_not(live[1]))
    def _():
        work(0)

    @pl.when(jnp.logical_not(live[0]) & live[1])
    def _():
        work(1)


def _fox3_fwd(qkv, c, bounds, *, carry=None, name):
    assert qkv.shape[0] // min(FOX_BLOCK, qkv.shape[0]) < LANES - 1
    s = qkv.shape[0]
    t = min(FOX_BLOCK, s)
    nb = s // t
    nt = (((1,), (1,)), ((), ()))
    nn = (((1,), (0,)), ((), ()))
    scale = FOX_HEADDIM ** -0.5

    def body(*refs):
        ins, (o_ref, mrun_ref, qmax_ref, mmin_ref), (m_sc, acc_sc, mt_sc), comm = _carried(carry, refs, 7, 4, 3)
        q_ref, k_ref, v_ref, c_ref, kn_ref, cs_ref, ce_ref = ins
        hp, i = pl.program_id(0), pl.program_id(1)
        if carry is not None:
            @pl.when((hp == 0) & (i == 0))
            def _():
                carry.start(*comm)

        first = _first_half()
        lane = lax.broadcasted_iota(jnp.int32, (1, LANES), 1)
        m_sc[...] = jnp.full_like(m_sc, NEG)
        acc_sc[...] = jnp.zeros_like(acc_sc)
        mt_sc[...] = jnp.zeros_like(mt_sc)
        q2 = q_ref[...] * scale
        zero = jnp.zeros_like(q2)
        qs = [jnp.where(first, q2, zero), jnp.where(first, zero, q2)]

        def head_block(hh, j, k2, vxh, masked):
            sc = lax.dot_general(qs[hh], k2, nt, preferred_element_type=F32) + _pair_bias(c_ref, hh, i, j, t)
            if masked:
                sc = jnp.where(_causal(t), sc, NEG)
            m_prev = m_sc[hh]
            m_new = jnp.maximum(m_prev, jnp.max(sc, axis=1, keepdims=True))
            pr = jnp.exp(sc - jnp.tile(m_new, (1, t // LANES))).astype(BF16)
            pv = lax.dot_general(pr, vxh, nn, preferred_element_type=F32)
            acc_sc[hh] = jnp.exp(m_prev - m_new) * acc_sc[hh] + pv
            m_sc[hh] = m_new

        def step(j, masked):
            at = pl.ds(pl.multiple_of(j * t, t), t)
            k2, v2 = k_ref[at, :], v_ref[at, :]
            one = jnp.ones_like(v2)
            vx = [jnp.where(first, v2, one), jnp.where(first, one, v2)]
            if masked:
                for hh in range(2):
                    head_block(hh, j, k2, vx[hh], True)
            else:
                live = [_fox_live(qmax_ref[2 * hp + hh, i], mmin_ref[2 * hp + hh, i], kn_ref, cs_ref, ce_ref,
                                  2 * hp + hh, i, j) for hh in range(2)]
                _run_live(live, lambda hh: head_block(hh, j, k2, vx[hh], False))
            for hh in range(2):
                mt_sc[hh] = jnp.where(lane == j, m_sc[hh], mt_sc[hh])

        step(i, True)
        for hh in range(2):
            qf = qs[hh].astype(F32)
            qmax_ref[2 * hp + hh, i] = jnp.sqrt(jnp.max(jnp.sum(qf * qf, axis=1, keepdims=True)))
            mmin_ref[2 * hp + hh, i] = jnp.min(m_sc[hh])
        lax.fori_loop(0, i, lambda n, carry: (step(i - 1 - n, False), carry)[1], 0)
        acc_a, acc_b = acc_sc[0], acc_sc[1]
        den_a = jnp.where(first, pltpu.roll(acc_a, HALF, 1), acc_a)
        den_b = jnp.where(first, acc_b, pltpu.roll(acc_b, HALF, 1))
        o_ref[...] = jnp.where(first, acc_a / den_a, acc_b / den_b)
        for hh, den in enumerate((den_a, den_b)):
            mrun_ref[:, hh * LANES:(hh + 1) * LANES] = jnp.where(lane == LANES - 1, m_sc[hh] + jnp.log(den), mt_sc[hh])
        if carry is not None:
            @pl.when((pl.program_id(0) == PAIRS - 1) & (i == nb - 1))
            def _():
                carry.wait(*comm)

    nq = D_FOX // LANES
    smem = pl.BlockSpec(memory_space=pltpu.SMEM)
    extra = carry if carry is not None else _Exchange([], "c", False)
    return pl.pallas_call(
        body, name=name, grid=(PAIRS, nb),
        in_specs=[pl.BlockSpec((t, LANES), lambda hp, i: (i, hp)),
                  pl.BlockSpec((s, LANES), lambda hp, i: (0, nq + hp)),
                  pl.BlockSpec((s, LANES), lambda hp, i: (0, 2 * nq + hp)),
                  pl.BlockSpec((2, 1, s), lambda hp, i: (hp, 0, 0))] + [smem] * 3 + extra.in_specs,
        out_specs=[pl.BlockSpec((t, LANES), lambda hp, i: (i, hp)),
                   pl.BlockSpec((t, 2 * LANES), lambda hp, i: (i, hp)), smem, smem] + extra.out_specs,
        out_shape=[jax.ShapeDtypeStruct((s, D_FOX), F32), jax.ShapeDtypeStruct((s, 2 * D_FOX), F32),
                   jax.ShapeDtypeStruct((FOX_HEADS, nb), F32), jax.ShapeDtypeStruct((FOX_HEADS, nb), F32)]
        + extra.out_shape,
        scratch_shapes=[pltpu.VMEM((2, t, LANES), F32)] * 3 + (carry.scratch if carry is not None else []),
        compiler_params=_cparams(("arbitrary", "arbitrary")),
    )(qkv, qkv, qkv, c, *bounds, *extra.arrs)


def _fox3_bwd(qkv, c, bounds, dox, mrun, *, carry=None, name):
    s = qkv.shape[0]
    t = min(FOX_BLOCK, s)
    nb = s // t
    nt = (((1,), (1,)), ((), ()))
    nn = (((1,), (0,)), ((), ()))
    tn = (((0,), (0,)), ((), ()))
    scale = FOX_HEADDIM ** -0.5

    def body(*refs):
        ins, outs, scratch, comm = _carried(carry, refs, 11, 4, 3)
        k_ref, v_ref, q_ref, c_ref, do_ref, mrun_ref, kn_ref, cs_ref, ce_ref, qmax_ref, mmin_ref = ins
        dq_ref, dk_ref, dv_ref, dc_ref = outs
        dk_sc, dv_sc, dc_sc = scratch
        hp, j = pl.program_id(0), pl.program_id(1)
        if carry is not None:
            @pl.when((hp == 0) & (j == 0))
            def _():
                carry.start(*comm)

        first = _first_half()
        halves = [first, jnp.logical_not(first)]
        lane = lax.broadcasted_iota(jnp.int32, (1, LANES), 1)

        @pl.when(j == 0)
        def _():
            dq_ref[...] = jnp.zeros_like(dq_ref)

        dk_sc[...] = jnp.zeros_like(dk_sc)
        dv_sc[...] = jnp.zeros_like(dv_sc)
        dc_sc[...] = jnp.zeros_like(dc_sc)
        k2, v2 = k_ref[...], v_ref[...]
        one = jnp.ones_like(v2)
        vx = [jnp.where(first, v2, one), jnp.where(first, one, v2)]

        def pick(stats, which):
            return jnp.sum(jnp.where(lane == which, stats, 0.0), axis=1, keepdims=True)

        def head_block(hh, i, at, qsh, stats, masked):
            mine = slice(hh * LANES, (hh + 1) * LANES)
            dov = do_ref[at, mine]
            sc = lax.dot_general(qsh, k2, nt, preferred_element_type=F32) + _pair_bias(c_ref, hh, i, j, t)
            mj = pick(stats, j)
            gain = jnp.broadcast_to(jnp.exp(mj - pick(stats, LANES - 1)), (t, LANES))
            mj = jnp.broadcast_to(mj, (t, LANES))
            pb = jnp.exp(sc - jnp.tile(mj, (1, t // LANES))).astype(BF16)
            if masked:
                pb = jnp.where(_causal(t), pb, jnp.zeros_like(pb))
            pr = pb.astype(F32) * jnp.tile(gain, (1, t // LANES))
            ds = pr * lax.dot_general(dov, vx[hh], nt, preferred_element_type=F32)
            dc_sc[hh] -= jnp.sum(ds, axis=0, keepdims=True)
            dsb = ds.astype(BF16)
            dvh = lax.dot_general(pr.astype(BF16), dov, tn, preferred_element_type=F32)
            dv_sc[...] += jnp.where(halves[hh], dvh, 0.0)
            dk_sc[...] += lax.dot_general(dsb, qsh, tn, preferred_element_type=F32)
            dqh = lax.dot_general(dsb, k2, nn, preferred_element_type=F32)
            dq_ref[at, :] += jnp.where(halves[hh], dqh, 0.0)

        def step(i, masked):
            at = pl.ds(pl.multiple_of(i * t, t), t)
            q2 = q_ref[at, :] * scale
            zero = jnp.zeros_like(q2)
            qs = [jnp.where(first, q2, zero), jnp.where(first, zero, q2)]
            stats = [mrun_ref[at, hh * LANES:(hh + 1) * LANES] for hh in range(2)]
            if masked:
                for hh in range(2):
                    head_block(hh, i, at, qs[hh], stats[hh], True)
            else:
                live = [_fox_live(qmax_ref[2 * hp + hh, i], mmin_ref[2 * hp + hh, i], kn_ref, cs_ref, ce_ref,
                                  2 * hp + hh, i, j) for hh in range(2)]
                _run_live(live, lambda hh: head_block(hh, i, at, qs[hh], stats[hh], False))

        step(j, True)
        lax.fori_loop(j + 1, nb, lambda i, carry: (step(i, False), carry)[1], 0)
        dk_ref[...] = dk_sc[...]
        dv_ref[...] = dv_sc[...].astype(dv_ref.dtype)
        dc_ref[...] = dc_sc[...]
        if carry is not None:
            @pl.when((pl.program_id(0) == PAIRS - 1) & (j == nb - 1))
            def _():
                carry.wait(*comm)

    nq = D_FOX // LANES
    blk = pl.BlockSpec((t, LANES), lambda hp, j: (j, hp))
    smem = pl.BlockSpec(memory_space=pltpu.SMEM)
    extra = carry if carry is not None else _Exchange([], "c", False)
    return pl.pallas_call(
        body, name=name, grid=(PAIRS, nb),
        in_specs=[pl.BlockSpec((t, LANES), lambda hp, j: (j, nq + hp)),
                  pl.BlockSpec((t, LANES), lambda hp, j: (j, 2 * nq + hp)),
                  pl.BlockSpec((s, LANES), lambda hp, j: (0, hp)),
                  pl.BlockSpec((2, 1, s), lambda hp, j: (hp, 0, 0)),
                  pl.BlockSpec((s, 2 * LANES), lambda hp, j: (0, hp)),
                  pl.BlockSpec((s, 2 * LANES), lambda hp, j: (0, hp))] + [smem] * 5 + extra.in_specs,
        out_specs=[pl.BlockSpec((s, LANES), lambda hp, j: (0, hp)), blk, blk,
                   pl.BlockSpec((2, 1, t), lambda hp, j: (hp, 0, j))] + extra.out_specs,
        out_shape=[jax.ShapeDtypeStruct((s, D_FOX), F32), jax.ShapeDtypeStruct((s, D_FOX), F32),
                   jax.ShapeDtypeStruct((s, D_FOX), BF16), jax.ShapeDtypeStruct((FOX_HEADS, 1, s), F32)]
        + extra.out_shape,
        scratch_shapes=[pltpu.VMEM((t, LANES), F32), pltpu.VMEM((t, LANES), F32), pltpu.VMEM((2, 1, t), F32)]
        + (carry.scratch if carry is not None else []),
        compiler_params=_cparams(("arbitrary", "arbitrary")),
    )(qkv, qkv, qkv, c, dox, mrun, *bounds, *extra.arrs)


def _final(x1, out1, g, tgt, *, tm=512, name):
    s, d = x1.shape
    tm = min(tm, s)

    def body(x_ref, o_ref, g_ref, t_ref, dx_ref, do_ref, dg_ref, loss_ref):
        i = pl.program_id(0)

        @pl.when(i == 0)
        def _():
            dg_ref[...] = jnp.zeros_like(dg_ref)
            loss_ref[...] = jnp.zeros_like(loss_ref)

        tv = t_ref[...]

        def lossf(xv, ov, gv):
            err = jnp.square(xv + _rms(ov, gv) - tv)
            return 0.5 * jnp.sum(jnp.mean(err, axis=-1, keepdims=True), axis=0, keepdims=True)

        val, vjp = jax.vjp(lossf, x_ref[...], o_ref[...], g_ref[...])
        dx, do, dg = vjp(jnp.ones((1, 1), F32))
        dx_ref[...] = dx
        do_ref[...] = do.astype(do_ref.dtype)
        dg_ref[...] += dg
        loss_ref[...] += val

    row = pl.BlockSpec((tm, d), lambda i: (i, 0))
    par = pl.BlockSpec((1, d), lambda i: (0, 0))
    return pl.pallas_call(
        body, name=name, grid=(s // tm,), in_specs=[row, row, par, row],
        out_specs=[row, row, par, pl.BlockSpec((1, 1), lambda i: (0, 0))],
        out_shape=[jax.ShapeDtypeStruct((s, d), F32), jax.ShapeDtypeStruct((s, d), BF16),
                   jax.ShapeDtypeStruct((1, d), F32), jax.ShapeDtypeStruct((1, 1), F32)],
        compiler_params=_cparams(("arbitrary",)),
    )(x1, out1, g, tgt)


def _row_tile(r):
    return LANES if r % LANES == 0 else r


def _sum_slots(parts, *, out_dtype=F32, name):
    p, r, c = parts.shape
    tr = _row_tile(r)

    def body(p_ref, o_ref):
        acc = p_ref[0].astype(F32)
        for k in range(1, p):
            acc = acc + p_ref[k].astype(F32)
        o_ref[...] = acc.astype(o_ref.dtype)

    return pl.pallas_call(
        body, name=name, grid=(r // tr,),
        in_specs=[pl.BlockSpec((p, tr, c), lambda i: (0, i, 0))],
        out_specs=pl.BlockSpec((tr, c), lambda i: (i, 0)),
        out_shape=jax.ShapeDtypeStruct((r, c), out_dtype),
        compiler_params=_cparams(("parallel",)),
    )(parts)


def _adamw(w, gparts, m, v, *, name):
    r, c = w.shape
    p = gparts.shape[0]
    tr = _row_tile(r)

    def body(w_ref, g_ref, m_ref, v_ref, go_ref, d_ref, mo_ref, vo_ref):
        g = g_ref[0].astype(F32)
        for k in range(1, p):
            g = g + g_ref[k].astype(F32)
        mn = ADAM_B1 * m_ref[...] + (1.0 - ADAM_B1) * g
        vn = ADAM_B2 * v_ref[...] + (1.0 - ADAM_B2) * jnp.square(g)
        m_hat = mn / (1.0 - ADAM_B1 ** ADAM_STEP)
        v_hat = vn / (1.0 - ADAM_B2 ** ADAM_STEP)
        go_ref[...] = g
        d_ref[...] = -ADAM_LR * (m_hat / (jnp.sqrt(v_hat) + ADAM_EPS) + ADAM_WD * w_ref[...])
        mo_ref[...] = mn
        vo_ref[...] = vn

    spec = pl.BlockSpec((tr, c), lambda i: (i, 0))
    return pl.pallas_call(
        body, name=name, grid=(r // tr,),
        in_specs=[spec, pl.BlockSpec((p, tr, c), lambda i: (0, i, 0)), spec, spec],
        out_specs=[spec] * 4, out_shape=[jax.ShapeDtypeStruct((r, c), F32)] * 4,
        compiler_params=_cparams(("parallel",)),
    )(w, gparts, m, v)


_FLIPS = {
    "xy": [(1, 0, 0), (0, 1, 0), (1, 1, 0)],
    "c": [(0, 0, 1)],
    "xyc": [(fx, fy, fc) for fx in (0, 1) for fy in (0, 1) for fc in (0, 1) if (fx, fy, fc) != (0, 0, 0)],
}


def _slot(mode, px, py, pc):
    return {"xy": 2 * px + py, "c": pc, "xyc": 4 * px + 2 * py + pc}[mode]


class _Exchange:
    def __init__(self, arrs, mode, scatter):
        self.arrs, self.mode, self.scatter = list(arrs), mode, scatter
        self.n = len(self.arrs)
        self.flips = _FLIPS[mode]
        nf = len(self.flips)
        anyspec = pl.BlockSpec(memory_space=pl.ANY)
        self.in_specs = [anyspec] * self.n
        self.out_specs = [anyspec] * self.n
        self.out_shape = [jax.ShapeDtypeStruct((nf + 1,) + (a.shape[1:] if scatter else a.shape), a.dtype)
                          for a in self.arrs]
        self.scratch = [pltpu.SemaphoreType.DMA((self.n * nf,)), pltpu.SemaphoreType.DMA((self.n * nf,)),
                        pltpu.SemaphoreType.DMA((self.n,))]

    def _copies(self, ins, outs, sems, arrivals=True):
        send, recv, loc = sems
        nf = len(self.flips)
        x, y, c = lax.axis_index("x"), lax.axis_index("y"), lax.axis_index("c")
        me = _slot(self.mode, x, y, c)
        peers = [(x ^ fx, y ^ fy, c ^ fc) for (fx, fy, fc) in self.flips]

        def src(a, slot):
            return ins[a].at[slot] if self.scatter else ins[a]

        def copy(a, j, dst_slot):
            return pltpu.make_async_remote_copy(
                src_ref=src(a, _slot(self.mode, *peers[j])), dst_ref=outs[a].at[dst_slot],
                send_sem=send.at[a * nf + j], recv_sem=recv.at[a * nf + j], device_id=peers[j], device_id_type=MESH)

        pairs = [(a, j) for a in range(self.n) for j in range(nf)]
        local = [pltpu.make_async_copy(src(a, me), outs[a].at[me], loc.at[a]) for a in range(self.n)]
        sends = [copy(a, j, me) for a, j in pairs]
        recvs = [copy(a, j, _slot(self.mode, *peers[j])) for a, j in pairs] if arrivals else []
        return local, sends, recvs

    def start(self, ins, outs, sems):
        local, sends, _ = self._copies(ins, outs, sems, arrivals=False)
        for cp in local + sends:
            cp.start()

    def wait(self, ins, outs, sems):
        local, sends, recvs = self._copies(ins, outs, sems)
        for cp in recvs:
            cp.wait_recv()
        for cp in sends:
            cp.wait_send()
        for cp in local:
            cp.wait()


def _carried(carry, refs, n_in, n_out, n_scratch):
    k = carry.n if carry is not None else 0
    ins, refs = refs[:n_in], refs[n_in:]
    cin, refs = refs[:k], refs[k:]
    outs, refs = refs[:n_out], refs[n_out:]
    cout, refs = refs[:k], refs[k:]
    scratch, sems = refs[:n_scratch], refs[n_scratch:]
    return ins, outs, scratch, (cin, cout, sems)


def _exchanges(exs, *, name):
    counts = [ex.n for ex in exs]
    total = sum(counts)

    def body(*refs):
        ins, outs, sems = refs[:total], refs[total:2 * total], refs[2 * total:]
        comms, at = [], 0
        for k, ex in enumerate(exs):
            comms.append((ins[at:at + ex.n], outs[at:at + ex.n], sems[3 * k:3 * k + 3]))
            at += ex.n
        for ex, comm in zip(exs, comms):
            ex.start(*comm)
        for ex, comm in zip(exs, comms):
            ex.wait(*comm)

    res = pl.pallas_call(
        body, name=name, in_specs=[sp for ex in exs for sp in ex.in_specs],
        out_specs=[sp for ex in exs for sp in ex.out_specs], out_shape=[sh for ex in exs for sh in ex.out_shape],
        scratch_shapes=[sc for ex in exs for sc in ex.scratch])(*[a for ex in exs for a in ex.arrs])
    out, at = [], 0
    for n in counts:
        out.append(list(res[at:at + n]))
        at += n
    return out


def _exchange(arrs, mode, scatter, *, name):
    return _exchanges([_Exchange(arrs, mode, scatter)], name=name)[0]


def _softplus(v):
    return jnp.maximum(v, 0.0) + jnp.log1p(jnp.exp(-jnp.abs(v)))


def _pad_lanes(v):
    r, n = v.shape
    return jnp.pad(v, ((0, 0), (0, -n % LANES)))


def _fn_rms(v, g):
    return (_rms(v, g),)


def _fn_post(xv, ov, g):
    return (xv + _rms(ov, g),)


def _fn_act(xbc, dtp, fp, dtb, fb):
    return _silu(xbc), _softplus(dtp + dtb), -_softplus(-(fp + fb))


def _fn_mix(y, zs, o, zf, g):
    yg = y * _silu(zs)
    sq = yg * yg
    lane = lax.broadcasted_iota(jnp.int32, (1, D_SSD), 1)
    width = D_SSD // SSD_GROUPS
    rstd = jnp.zeros_like(yg)
    for gi in range(SSD_GROUPS):
        msk = ((lane >= gi * width) & (lane < (gi + 1) * width)).astype(F32)
        ms = jnp.sum(sq * msk, axis=1, keepdims=True) / width
        rstd = rstd + lax.rsqrt(ms + EPS) * msk
    return (jnp.concatenate([yg * rstd * g, o * _silu(zf)], axis=1),)


def _fn_glu(val, gate):
    return (val * jax.nn.sigmoid(gate),)


def _fn_ln(hc, z, g, b):
    mu = jnp.mean(hc, axis=-1, keepdims=True)
    xc = hc - mu
    yn = xc * lax.rsqrt(jnp.mean(xc * xc, axis=-1, keepdims=True) + EPS) * g + b
    return (_silu(yn) * _silu(z),)


class _NoComm:
    def odd_weights(self):
        return None

    def got_odd_weights(self, got, w):
        pass

    def early_grads(self, g):
        return None

    def got_early_grads(self, got):
        pass

    def early_sums(self):
        return None

    def got_early_sums(self, got):
        pass


def _local_step(x, tgt, w, comm=None):
    comm = comm or _NoComm()
    s = x.shape[0]
    d = D_MODEL
    tm = 256
    tf = 512
    bf = lambda v: v.astype(BF16)
    c1 = lambda arr: _col(arr, 0, arr.shape[1])
    g = {}

    ew = w["e_w_in"]
    w_z, w_xbc = bf(ew[:, 0:2048]), bf(ew[:, 2048:4096])
    w_dt = bf(_pad_lanes(ew[:, 4096:4112]))
    w_qkv = bf(ew[:, 4112:7184])
    w_f = bf(_pad_lanes(ew[:, 7184:7200]))
    dtb, fgb = _pad_lanes(w["e_dt_bias"]), _pad_lanes(w["e_fgate_b"])
    alog, dsk = _pad_lanes(w["e_a_log"]), _pad_lanes(w["e_d_skip"])

    (u0,) = _rowwise_fwd(_fn_rms, [c1(x)], [c1(w["e_norm_pre"])], [(d, BF16)], tm=tf, name="e_pre")
    z = _mm(u0, w_z, name="e_in_z")
    xbc_raw = _mm(u0, w_xbc, out_dtype=BF16, name="e_in_xbc")
    qkv = _mm(u0, w_qkv, out_dtype=BF16, name="e_in_qkv")
    dtp = _mm(u0, w_dt, name="e_in_dt")
    fp = _mm(u0, w_f, name="e_in_f")
    xbc_pre = _conv_fwd(xbc_raw, w["e_conv_w"], w["e_conv_b"], name="e_conv")
    act_rows = [c1(xbc_pre), c1(dtp), c1(fp)]
    act_pars = [c1(dtb), c1(fgb)]
    xbc, dt, lf = _rowwise_fwd(_fn_act, act_rows, act_pars, [(2048, F32), (LANES, F32), (LANES, F32)],
                               tm=tf, name="e_act")
    y, hsave = _ssd2_fwd(dt, xbc, alog, dsk, name="e_ssd")
    csum = _cumsum_lanes(lf[:, :FOX_HEADS].T, reverse=False, name="e_cumsum").reshape(FOX_HEADS, 1, s)
    bounds = _fox_block_bounds(qkv, csum)
    o, mrun, qmax, mmin, *got = _fox3_fwd(qkv, csum, bounds, carry=comm.odd_weights(), name="e_fox")
    bounds = (*bounds, qmax, mmin)
    comm.got_odd_weights(got, w)
    w_eo, w_oi, w_oo = bf(w["e_w_out"]), bf(w["o_w_in"]), bf(w["o_w_out"])
    mix_rows = [c1(y), _col(z, 0, D_SSD), c1(o), _col(z, 1, D_FOX)]
    mix_pars = [c1(w["e_ssd_norm"])]
    (hmix,) = _rowwise_fwd(_fn_mix, mix_rows, mix_pars, [(2048, BF16)], tm=tf, name="e_mix")
    out0 = _mm(hmix, w_eo, name="e_out")
    post_rows = [c1(x), c1(out0)]
    (x1,) = _rowwise_fwd(_fn_post, post_rows, [c1(w["e_norm_post"])], [(d, F32)], tm=tf, name="e_post")

    (u1,) = _rowwise_fwd(_fn_rms, [c1(x1)], [c1(w["o_norm_pre"])], [(d, BF16)], tm=tf, name="o_pre")
    p1 = _mm(u1, w_oi, out_dtype=BF16, name="o_in")
    glu_rows = [_col(p1, 0, D_CONV), _col(p1, 1, D_CONV)]
    (hg,) = _rowwise_fwd(_fn_glu, glu_rows, [], [(D_CONV, BF16)], tm=tf, name="o_glu")
    hc = _conv_fwd(hg, w["o_conv_w"], w["o_conv_b"], name="o_conv")
    ln_rows = [c1(hc), _col(p1, 2, D_CONV)]
    ln_pars = [c1(w["o_ln_g"]), c1(w["o_ln_b"])]
    (h2,) = _rowwise_fwd(_fn_ln, ln_rows, ln_pars, [(D_CONV, BF16)], tm=tf, name="o_ln")
    out1 = _mm(h2, w_oo, name="o_out")

    dx2, dout1, g["o_norm_post"], loss = _final(x1, out1, w["o_norm_post"], tgt, name="loss_head")
    dh2 = _mm(dout1, w_oo, tb=True, name="o_out_dx")
    g["o_w_out"] = _mm(h2, dout1, ta=True, name="o_out_dw")
    (dhc, dz1), (g["o_ln_g"], g["o_ln_b"]) = _rowwise_bwd(_fn_ln, ln_rows, ln_pars, [c1(dh2)], [F32, BF16],
                                                         tm=tm, name="o_ln_bwd")
    dhg = _conv_bwd_x(dhc, w["o_conv_w"], name="o_conv_dx")
    g["o_conv_w"], g["o_conv_b"] = _conv_bwd_w(hg, dhc, CONV_WIDTH, name="o_conv_dw")
    (dval, dgate), _ = _rowwise_bwd(_fn_glu, glu_rows, [], [c1(dhg)], [BF16, BF16], tm=tf, name="o_glu_bwd")
    du1 = _mm(dval, w_oi[:, 0:2048], tb=True, name="o_in_dx0")
    du1 = _mm(dgate, w_oi[:, 2048:4096], tb=True, add=du1, name="o_in_dx1")
    du1 = _mm(dz1, w_oi[:, 4096:6144], tb=True, add=du1, name="o_in_dx2")
    g["o_w_in"] = jnp.concatenate([_mm(u1, dval, ta=True, name="o_in_dw0"), _mm(u1, dgate, ta=True, name="o_in_dw1"),
                                   _mm(u1, dz1, ta=True, name="o_in_dw2")], axis=1)
    (dx1,), (g["o_norm_pre"],) = _rowwise_bwd(_fn_rms, [c1(x1)], [c1(w["o_norm_pre"])], [c1(du1)], [F32],
                                              adds={0: c1(dx2)}, tm=tf, name="o_pre_bwd")

    (dout0,), (g["e_norm_post"],) = _rowwise_bwd(_fn_post, post_rows, [c1(w["e_norm_post"])], [c1(dx1)],
                                                 [None, BF16], tm=tf, name="e_post_bwd")
    dhmix = _mm(dout0, w_eo, tb=True, name="e_out_dx")
    g["e_w_out"] = _mm(hmix, dout0, ta=True, name="e_out_dw")
    (dy, dzs, do, dzf), (g["e_ssd_norm"],) = _rowwise_bwd(_fn_mix, mix_rows, mix_pars, [c1(dhmix)],
                                                        [F32, BF16, F32, BF16], tm=tm, name="e_mix_bwd")
    dox = _fox_dopack(do, o, name="e_fox_dopack")
    dq8, dk, dv, dcs, *got = _fox3_bwd(qkv, csum, bounds, dox, mrun, carry=comm.early_grads(g), name="e_fox_bwd")
    comm.got_early_grads(got)
    dlf = _pad_lanes(_cumsum_lanes(dcs.reshape(FOX_HEADS, s), reverse=True, name="e_cumsum_bwd").T)
    dxs, ddt, dbm, dcm, dalog, ddsk, *got = _ssd2_bwd(dt, xbc, alog, dsk, hsave, dy, carry=comm.early_sums(),
                                                       name="e_ssd_bwd")
    comm.got_early_sums(got)
    dxbc = jnp.concatenate([dxs, dbm, dcm], axis=1)
    (dxbc_pre, ddtp, dfp), (ddtb, dfgb) = _rowwise_bwd(_fn_act, act_rows, act_pars, [c1(dxbc), c1(ddt), c1(dlf)],
                                                      [F32, BF16, BF16], tm=tm, name="e_act_bwd")
    dxbc_raw = bf(_conv_bwd_x(dxbc_pre, w["e_conv_w"], name="e_conv_dx"))
    g["e_conv_w"], g["e_conv_b"] = _conv_bwd_w(xbc_raw, dxbc_pre, SSD_CONV, name="e_conv_dw")
    du0 = _mm(dzs, w_z[:, :D_SSD], tb=True, name="e_in_dx0")
    du0 = _mm(dzf, w_z[:, D_SSD:], tb=True, add=du0, name="e_in_dx1")
    du0 = _mm(dxbc_raw, w_xbc, tb=True, add=du0, name="e_in_dx2")
    eighth = FOX_HEADDIM ** -0.5
    du0 = _mm(dq8, w_qkv[:, :D_FOX] * eighth, tb=True, add=du0, name="e_in_dx3q")
    du0 = _mm(dk, w_qkv[:, D_FOX:2 * D_FOX], tb=True, add=du0, name="e_in_dx3k")
    du0 = _mm(dv, w_qkv[:, 2 * D_FOX:], tb=True, add=du0, name="e_in_dx3v")
    du0 = _mm(ddtp, w_dt, tb=True, add=du0, name="e_in_dx4")
    du0 = _mm(dfp, w_f, tb=True, add=du0, name="e_in_dx5")
    g["e_w_in"] = jnp.concatenate([
        _mm(u0, dzs, ta=True, name="e_in_dw0"), _mm(u0, dzf, ta=True, name="e_in_dw1"),
        _mm(u0, dxbc_raw, ta=True, name="e_in_dw2"), _mm(u0, ddtp, ta=True, name="e_in_dw3")[:, :SSD_HEADS],
        _mm(u0, dq8, ta=True, name="e_in_dw4q") * eighth, _mm(u0, dk, ta=True, name="e_in_dw4k"),
        _mm(u0, dv, ta=True, name="e_in_dw4v"), _mm(u0, dfp, ta=True, name="e_in_dw5")[:, :FOX_HEADS]], axis=1)
    (dx,), (g["e_norm_pre"],) = _rowwise_bwd(_fn_rms, [c1(x)], [c1(w["e_norm_pre"])], [c1(du0)], [F32],
                                             adds={0: c1(dx1)}, tm=tf, name="e_pre_bwd")
    g["e_dt_bias"], g["e_fgate_b"] = ddtb[:, :SSD_HEADS], dfgb[:, :FOX_HEADS]
    g["e_a_log"], g["e_d_skip"] = dalog[:, :SSD_HEADS], ddsk[:, :SSD_HEADS]
    return loss, dx, g


_WEIGHTS = ["e_norm_pre", "e_w_in", "e_conv_w", "e_conv_b", "e_dt_bias", "e_a_log", "e_d_skip", "e_fgate_b",
            "e_ssd_norm", "e_w_out", "e_norm_post", "o_norm_pre", "o_w_in", "o_conv_w", "o_conv_b", "o_ln_g",
            "o_ln_b", "o_w_out", "o_norm_post"]
_BIG = ["e_w_in", "e_w_out", "o_w_in", "o_w_out"]
_ROW_SHARDED = ["e_w_out", "o_w_out"]
_SMALL_SHARDED = ["e_conv_w", "o_norm_pre", "o_conv_w", "o_conv_b", "o_ln_g", "o_ln_b", "o_norm_post"]
_REPLICATED = ["e_norm_pre", "e_conv_b", "e_dt_bias", "e_a_log", "e_d_skip", "e_fgate_b", "e_ssd_norm", "e_norm_post"]
_SMALL = [n for n in _WEIGHTS if n not in _BIG]
_EVEN_SHARDED = ["e_w_in", "e_conv_w"]
_ODD_SHARDED = ["e_w_out", "o_w_in", "o_w_out", "o_norm_pre", "o_conv_w", "o_conv_b", "o_ln_g", "o_ln_b",
                "o_norm_post"]
_EARLY_GRADS = ["o_w_in", "o_w_out", "e_w_out"]
N_CHIPS = 4


def _join(gathered, rows):
    k, r, c = gathered.shape
    return gathered.reshape(k * r, c) if rows else gathered.transpose(1, 0, 2).reshape(r, k * c)


def _split(full, rows):
    r, c = full.shape
    return full.reshape(N_CHIPS, r // N_CHIPS, c) if rows else full.reshape(r, N_CHIPS, c // N_CHIPS).transpose(1, 0, 2)


def kernel(x, e_norm_pre, e_w_in, e_conv_w, e_conv_b, e_dt_bias, e_a_log, e_d_skip, e_fgate_b, e_ssd_norm, e_w_out, e_norm_post, o_norm_pre, o_w_in, o_conv_w, o_conv_b, o_ln_g, o_ln_b, o_w_out, o_norm_post, loss_target, m_e_norm_pre, m_e_w_in, m_e_conv_w, m_e_conv_b, m_e_dt_bias, m_e_a_log, m_e_d_skip, m_e_fgate_b, m_e_ssd_norm, m_e_w_out, m_e_norm_post, m_o_norm_pre, m_o_w_in, m_o_conv_w, m_o_conv_b, m_o_ln_g, m_o_ln_b, m_o_w_out, m_o_norm_post, v_e_norm_pre, v_e_w_in, v_e_conv_w, v_e_conv_b, v_e_dt_bias, v_e_a_log, v_e_d_skip, v_e_fgate_b, v_e_ssd_norm, v_e_w_out, v_e_norm_post, v_o_norm_pre, v_o_w_in, v_o_conv_w, v_o_conv_b, v_o_ln_g, v_o_ln_b, v_o_w_out, v_o_norm_post):
    wvals = (e_norm_pre, e_w_in, e_conv_w, e_conv_b, e_dt_bias, e_a_log, e_d_skip, e_fgate_b, e_ssd_norm, e_w_out,
             e_norm_post, o_norm_pre, o_w_in, o_conv_w, o_conv_b, o_ln_g, o_ln_b, o_w_out, o_norm_post)
    mvals = (m_e_norm_pre, m_e_w_in, m_e_conv_w, m_e_conv_b, m_e_dt_bias, m_e_a_log, m_e_d_skip, m_e_fgate_b,
             m_e_ssd_norm, m_e_w_out, m_e_norm_post, m_o_norm_pre, m_o_w_in, m_o_conv_w, m_o_conv_b, m_o_ln_g,
             m_o_ln_b, m_o_w_out, m_o_norm_post)
    vvals = (v_e_norm_pre, v_e_w_in, v_e_conv_w, v_e_conv_b, v_e_dt_bias, v_e_a_log, v_e_d_skip, v_e_fgate_b,
             v_e_ssd_norm, v_e_w_out, v_e_norm_post, v_o_norm_pre, v_o_w_in, v_o_conv_w, v_o_conv_b, v_o_ln_g,
             v_o_ln_b, v_o_w_out, v_o_norm_post)

    def mat(v):
        return v.reshape(v.shape[-2:]) if v.ndim == 3 else v

    w = {n: mat(v) for n, v in zip(_WEIGHTS, wvals)}
    m = {n: mat(v) for n, v in zip(_WEIGHTS, mvals)}
    v2 = {n: mat(v) for n, v in zip(_WEIGHTS, vvals)}
    me_xy = 2 * lax.axis_index("x") + lax.axis_index("y")

    def shard(n):
        return w[n].astype(BF16) if n in _BIG else w[n]

    gathered = _exchange([shard(n) for n in _EVEN_SHARDED], "xy", False, name="gather_weights")
    full = {n: w[n] for n in _REPLICATED}
    for n, gth in zip(_EVEN_SHARDED, gathered):
        full[n] = _join(gth, n in _ROW_SHARDED)
    gparts = {}

    class _StepComm(_NoComm):
        def odd_weights(self):
            return _Exchange([shard(n) for n in _ODD_SHARDED], "xy", False)

        def got_odd_weights(self, got, wdict):
            for n, gth in zip(_ODD_SHARDED, got):
                wdict[n] = _join(gth, n in _ROW_SHARDED)

        def early_grads(self, g):
            return _Exchange([_split(g[n], n in _ROW_SHARDED).astype(BF16) for n in _EARLY_GRADS], "xy", True)

        def got_early_grads(self, got):
            self.sums = [_sum_slots(p, out_dtype=BF16, name="sum_" + n) for n, p in zip(_EARLY_GRADS, got)]

        def early_sums(self):
            return _Exchange(self.sums, "c", False)

        def got_early_sums(self, got):
            gparts.update(zip(_EARLY_GRADS, got))

    loss, dx, g = _local_step(x[0], loss_target[0], full, _StepComm())
    loss = lax.psum(loss[0, 0], ("x", "y", "c"))

    flat = jnp.concatenate([_pad_lanes(g[n].reshape(1, -1)) for n in _SMALL], axis=1).reshape(-1, LANES)
    (scattered,), (all8,) = _exchanges([_Exchange([_split(g["e_w_in"], False).astype(BF16)], "xy", True),
                                        _Exchange([flat], "xyc", False)], name="scatter_grads")
    (gparts["e_w_in"],) = _exchange([_sum_slots(scattered, out_dtype=BF16, name="sum_e_w_in")], "c", False,
                                    name="pair_grads")
    total = _sum_slots(all8, name="sum_small").reshape(1, -1)
    at = 0
    for n in _SMALL:
        size = g[n].size
        gn = total[:, at:at + size].reshape(g[n].shape)
        at += size + (-size % LANES)
        if n in _SMALL_SHARDED:
            cols = gn.shape[1] // N_CHIPS
            gn = lax.dynamic_slice(gn, (0, me_xy * cols), (gn.shape[0], cols))
        gparts[n] = gn[None]

    grads, deltas, new_m, new_v = [], [], [], []
    for n, orig in zip(_WEIGHTS, wvals):
        gn, dn, mn, vn = _adamw(w[n], gparts[n], m[n], v2[n], name="adamw_" + n)
        for lst, val in zip((grads, deltas, new_m, new_v), (gn, dn, mn, vn)):
            lst.append(val.reshape(orig.shape))
    return (loss, dx[None], *grads, *deltas, *new_m, *new_v)
```
